```python
import jax, jax.numpy as jnp
from jax import lax
import numpy as np

D_MODEL = 1024
BATCH = 32
SEQ = 2048
DEPTH = 2

PLE_DIM = 256
BLOCK = 128
EPS = 1e-6
NEG = -1e30

SWA_HEADS = 8
SWA_KV_HEADS = 2
SWA_HEAD_DIM = 64
SWA_WINDOW = 128
SWA_WIDTH = SWA_HEADS * SWA_HEAD_DIM
SWA_KV_WIDTH = SWA_KV_HEADS * SWA_HEAD_DIM

MLA_HEADS = 8
MLA_NOPE = 64
MLA_ROPE = 32
MLA_V = 64
MLA_Q_LORA = 256
MLA_KV_LORA = 128
MLA_WIDTH = MLA_HEADS * MLA_V
MLA_QK = MLA_NOPE + MLA_ROPE
ROPE_THETA = 10000.0

IN_SIZES = (SWA_WIDTH, SWA_KV_WIDTH, SWA_KV_WIDTH, SWA_WIDTH,
            MLA_Q_LORA, MLA_KV_LORA, MLA_ROPE, MLA_WIDTH,
            D_MODEL, D_MODEL)
IN_WIDTH = sum(IN_SIZES)

kernel_name = "hybrid_swa_sink_mla_gated_merge"


def rms_norm(x, g):
    xf = x.astype(jnp.float32)
    y = xf * lax.rsqrt(jnp.mean(xf * xf, axis=-1, keepdims=True) + EPS)
    return (y * g.astype(jnp.float32)).astype(x.dtype)


def split_columns(z, sizes):
    idx = []
    acc = 0
    for sz in sizes[:-1]:
        acc += sz
        idx.append(acc)
    return jnp.split(z, idx, axis=-1)


def alibi_slopes(n):
    return jnp.exp2(-8.0 * (jnp.arange(n, dtype=jnp.float32) + 1.0) / n)


def apply_rope(x, pos):
    r = x.shape[-1]
    inv = ROPE_THETA ** (-jnp.arange(0, r, 2, dtype=jnp.float32) / r)
    ang = pos.astype(jnp.float32)[..., None] * inv
    cos = jnp.cos(ang)[:, :, None, :]
    sin = jnp.sin(ang)[:, :, None, :]
    xf = x.astype(jnp.float32)
    x1, x2 = xf[..., : r // 2], xf[..., r // 2:]
    out = jnp.concatenate([x1 * cos - x2 * sin, x2 * cos + x1 * sin], axis=-1)
    return out.astype(x.dtype)


def swa_sink_attention(q, k, v, sink, pos):
    b, s, h, dh = q.shape
    kvh = k.shape[2]
    g = h // kvh
    nb = s // BLOCK
    qb = q.reshape(b, nb, BLOCK, kvh, g, dh)

    def band(t):
        tail = t.shape[2:]
        pad = jnp.zeros((b, BLOCK) + tail, t.dtype)
        prev = jnp.concatenate([pad, t[:, :-BLOCK]], axis=1).reshape((b, nb, BLOCK) + tail)
        cur = t.reshape((b, nb, BLOCK) + tail)
        return jnp.concatenate([prev, cur], axis=2)

    kb, vb, pk = band(k), band(v), band(pos)
    pq = pos.reshape(b, nb, BLOCK)
    scores = jnp.einsum('bnqkgd,bnskd->bnkgqs', qb, kb,
                        preferred_element_type=jnp.float32) * (dh ** -0.5)
    dist = (pq[:, :, :, None] - pk[:, :, None, :]).astype(jnp.float32)
    slopes = alibi_slopes(h).reshape(kvh, g)
    scores = scores - slopes[None, None, :, :, None, None] * dist[:, :, None, None, :, :]
    n_i = jnp.arange(nb)[:, None, None]
    q_i = jnp.arange(BLOCK)[None, :, None]
    k_j = jnp.arange(2 * BLOCK)[None, None, :]
    t_abs = n_i * BLOCK + q_i
    s_abs = n_i * BLOCK - BLOCK + k_j
    valid = (s_abs >= 0) & (s_abs <= t_abs) & (t_abs - s_abs < SWA_WINDOW)
    scores = jnp.where(valid[None, :, None, None, :, :], scores, NEG)
    sink_b = sink.astype(jnp.float32).reshape(kvh, g)[None, None, :, :, None]
    m = jnp.maximum(jnp.max(scores, axis=-1), sink_b)
    e = jnp.exp(scores - m[..., None])
    denom = jnp.sum(e, axis=-1) + jnp.exp(sink_b - m)
    probs = e / denom[..., None]
    out = jnp.einsum('bnkgqs,bnskd->bnqkgd', probs.astype(v.dtype), vb)
    return out.reshape(b, s, h * dh)


def mla_causal_attention(q, k, v):
    b, s, h, dq = q.shape
    nb = s // BLOCK
    qb = q.reshape(b, nb, BLOCK, h, dq).transpose(1, 0, 2, 3, 4)
    kpos = jnp.arange(s)
    scale = dq ** -0.5

    def one_block(args):
        qblk, n = args
        sc = jnp.einsum('bqhd,bshd->bhqs', qblk, k,
                        preferred_element_type=jnp.float32) * scale
        qpos = n * BLOCK + jnp.arange(BLOCK)
        sc = jnp.where(kpos[None, :] <= qpos[:, None], sc, NEG)
        pr = jax.nn.softmax(sc, axis=-1)
        return jnp.einsum('bhqs,bshd->bqhd', pr.astype(v.dtype), v)

    out = lax.map(one_block, (qb, jnp.arange(nb)))
    return out.transpose(1, 0, 2, 3, 4).reshape(b, s, h * v.shape[-1])


def _fwd_setup_inputs(seed: int = 0) -> dict:
    key = jax.random.key(seed)
    ks = jax.random.split(key, 20)
    f32 = jnp.float32

    def nrm(k, shape, fan_in):
        return jax.random.normal(k, shape, f32) * (fan_in ** -0.5)

    def gain(k, shape):
        return 1.0 + 0.02 * jax.random.normal(k, shape, f32)

    x = jax.random.normal(ks[0], (BATCH, SEQ, D_MODEL), f32)
    p = jax.random.normal(ks[1], (DEPTH, BATCH, SEQ, PLE_DIM), f32)
    positions = jnp.broadcast_to(jnp.arange(SEQ, dtype=jnp.int32)[None, :], (BATCH, SEQ))
    return {
        "x": x,
        "p": p,
        "positions": positions,
        "g_mix": gain(ks[2], (DEPTH, D_MODEL)),
        "w_in": nrm(ks[3], (DEPTH, D_MODEL, IN_WIDTH), D_MODEL),
        "sink": 0.5 * jax.random.normal(ks[4], (DEPTH, SWA_HEADS), f32),
        "g_q": gain(ks[5], (DEPTH, MLA_Q_LORA)),
        "w_uq": nrm(ks[6], (DEPTH, MLA_Q_LORA, MLA_HEADS * MLA_QK), MLA_Q_LORA),
        "g_kv": gain(ks[7], (DEPTH, MLA_KV_LORA)),
        "w_ukv": nrm(ks[8], (DEPTH, MLA_KV_LORA, MLA_HEADS * (MLA_NOPE + MLA_V)), MLA_KV_LORA),
        "w_br_a": nrm(ks[9], (DEPTH, SWA_WIDTH, D_MODEL), SWA_WIDTH),
        "w_br_b": nrm(ks[10], (DEPTH, MLA_WIDTH, D_MODEL), MLA_WIDTH),
        "w_out": nrm(ks[11], (DEPTH, D_MODEL, D_MODEL), D_MODEL),
        "g_ple": gain(ks[12], (DEPTH, D_MODEL)),
        "w_ple_gate": nrm(ks[13], (DEPTH, D_MODEL, D_MODEL), D_MODEL),
        "w_ple_proj": nrm(ks[14], (DEPTH, PLE_DIM, D_MODEL), PLE_DIM),
        "g_final": gain(ks[15], (D_MODEL,)),
    }


def _fwd_reference(x, p, positions, g_mix, w_in, sink, g_q, w_uq, g_kv, w_ukv,
              w_br_a, w_br_b, w_out, g_ple, w_ple_gate, w_ple_proj, g_final):
    b, s, _ = x.shape
    for i in range(DEPTH):
        h = rms_norm(x, g_mix[i])
        z = h @ w_in[i]
        (a_q, a_k, a_v, a_gate, b_qd, b_kvd, b_kr, b_gate,
         m_a, m_b) = split_columns(z, IN_SIZES)

        qa = a_q.reshape(b, s, SWA_HEADS, SWA_HEAD_DIM)
        ka = a_k.reshape(b, s, SWA_KV_HEADS, SWA_HEAD_DIM)
        va = a_v.reshape(b, s, SWA_KV_HEADS, SWA_HEAD_DIM)
        o_a = swa_sink_attention(qa, ka, va, sink[i], positions) * jax.nn.silu(a_gate)

        qb = (rms_norm(b_qd, g_q[i]) @ w_uq[i]).reshape(b, s, MLA_HEADS, MLA_QK)
        q_nope, q_rope = qb[..., :MLA_NOPE], qb[..., MLA_NOPE:]
        q_rope = apply_rope(q_rope, positions)
        kv = (rms_norm(b_kvd, g_kv[i]) @ w_ukv[i]).reshape(b, s, MLA_HEADS, MLA_NOPE + MLA_V)
        k_nope, vb = kv[..., :MLA_NOPE], kv[..., MLA_NOPE:]
        k_rope = apply_rope(b_kr[:, :, None, :], positions)
        q_full = jnp.concatenate([q_nope, q_rope], axis=-1)
        k_full = jnp.concatenate(
            [k_nope, jnp.broadcast_to(k_rope, (b, s, MLA_HEADS, MLA_ROPE))], axis=-1)
        o_b = mla_causal_attention(q_full, k_full, vb) * jax.nn.silu(b_gate)

        y = jax.nn.sigmoid(m_a) * (o_a @ w_br_a[i]) + jax.nn.sigmoid(m_b) * (o_b @ w_br_b[i])
        x = x + y @ w_out[i]

        pg = jax.nn.sigmoid(rms_norm(x, g_ple[i]) @ w_ple_gate[i])
        x = x + pg * (p[i].astype(x.dtype) @ w_ple_proj[i])
    return rms_norm(x, g_final)


import jax as _jax
import jax.numpy as _jnp

TWIN_FORMAT = 'train_step'
FWD_PARAMS = ['x', 'p', 'positions', 'g_mix', 'w_in', 'sink', 'g_q', 'w_uq', 'g_kv', 'w_ukv', 'w_br_a', 'w_br_b', 'w_out', 'g_ple', 'w_ple_gate', 'w_ple_proj', 'g_final']
TWIN_WEIGHTS = ['g_mix', 'w_in', 'sink', 'g_q', 'w_uq', 'g_kv', 'w_ukv', 'w_br_a', 'w_br_b', 'w_out', 'g_ple', 'w_ple_gate', 'w_ple_proj', 'g_final']
TWIN_DIFF_INPUT = 'x'
TWIN_INPUTS = ['x', 'p', 'positions', 'g_mix', 'w_in', 'sink', 'g_q', 'w_uq', 'g_kv', 'w_ukv', 'w_br_a', 'w_br_b', 'w_out', 'g_ple', 'w_ple_gate', 'w_ple_proj', 'g_final', 'loss_target', 'm_g_mix', 'm_w_in', 'm_sink', 'm_g_q', 'm_w_uq', 'm_g_kv', 'm_w_ukv', 'm_w_br_a', 'm_w_br_b', 'm_w_out', 'm_g_ple', 'm_w_ple_gate', 'm_w_ple_proj', 'm_g_final', 'v_g_mix', 'v_w_in', 'v_sink', 'v_g_q', 'v_w_uq', 'v_g_kv', 'v_w_ukv', 'v_w_br_a', 'v_w_br_b', 'v_w_out', 'v_g_ple', 'v_w_ple_gate', 'v_w_ple_proj', 'v_g_final']
TWIN_OUTPUTS = ['loss', 'grad_x', 'grad_g_mix', 'grad_w_in', 'grad_sink', 'grad_g_q', 'grad_w_uq', 'grad_g_kv', 'grad_w_ukv', 'grad_w_br_a', 'grad_w_br_b', 'grad_w_out', 'grad_g_ple', 'grad_w_ple_gate', 'grad_w_ple_proj', 'grad_g_final', 'delta_g_mix', 'delta_w_in', 'delta_sink', 'delta_g_q', 'delta_w_uq', 'delta_g_kv', 'delta_w_ukv', 'delta_w_br_a', 'delta_w_br_b', 'delta_w_out', 'delta_g_ple', 'delta_w_ple_gate', 'delta_w_ple_proj', 'delta_g_final', 'new_m_g_mix', 'new_m_w_in', 'new_m_sink', 'new_m_g_q', 'new_m_w_uq', 'new_m_g_kv', 'new_m_w_ukv', 'new_m_w_br_a', 'new_m_w_br_b', 'new_m_w_out', 'new_m_g_ple', 'new_m_w_ple_gate', 'new_m_w_ple_proj', 'new_m_g_final', 'new_v_g_mix', 'new_v_w_in', 'new_v_sink', 'new_v_g_q', 'new_v_w_uq', 'new_v_g_kv', 'new_v_w_ukv', 'new_v_w_br_a', 'new_v_w_br_b', 'new_v_w_out', 'new_v_g_ple', 'new_v_w_ple_gate', 'new_v_w_ple_proj', 'new_v_g_final']
TWIN_LEAF_KINDS = {'loss': 'loss', 'grad_x': 'grad_x', 'grad_g_mix': 'grad_w', 'grad_w_in': 'grad_w', 'grad_sink': 'grad_w', 'grad_g_q': 'grad_w', 'grad_w_uq': 'grad_w', 'grad_g_kv': 'grad_w', 'grad_w_ukv': 'grad_w', 'grad_w_br_a': 'grad_w', 'grad_w_br_b': 'grad_w', 'grad_w_out': 'grad_w', 'grad_g_ple': 'grad_w', 'grad_w_ple_gate': 'grad_w', 'grad_w_ple_proj': 'grad_w', 'grad_g_final': 'grad_w', 'delta_g_mix': 'delta_w', 'delta_w_in': 'delta_w', 'delta_sink': 'delta_w', 'delta_g_q': 'delta_w', 'delta_w_uq': 'delta_w', 'delta_g_kv': 'delta_w', 'delta_w_ukv': 'delta_w', 'delta_w_br_a': 'delta_w', 'delta_w_br_b': 'delta_w', 'delta_w_out': 'delta_w', 'delta_g_ple': 'delta_w', 'delta_w_ple_gate': 'delta_w', 'delta_w_ple_proj': 'delta_w', 'delta_g_final': 'delta_w', 'new_m_g_mix': 'new_m', 'new_m_w_in': 'new_m', 'new_m_sink': 'new_m', 'new_m_g_q': 'new_m', 'new_m_w_uq': 'new_m', 'new_m_g_kv': 'new_m', 'new_m_w_ukv': 'new_m', 'new_m_w_br_a': 'new_m', 'new_m_w_br_b': 'new_m', 'new_m_w_out': 'new_m', 'new_m_g_ple': 'new_m', 'new_m_w_ple_gate': 'new_m', 'new_m_w_ple_proj': 'new_m', 'new_m_g_final': 'new_m', 'new_v_g_mix': 'new_v', 'new_v_w_in': 'new_v', 'new_v_sink': 'new_v', 'new_v_g_q': 'new_v', 'new_v_w_uq': 'new_v', 'new_v_g_kv': 'new_v', 'new_v_w_ukv': 'new_v', 'new_v_w_br_a': 'new_v', 'new_v_w_br_b': 'new_v', 'new_v_w_out': 'new_v', 'new_v_g_ple': 'new_v', 'new_v_w_ple_gate': 'new_v', 'new_v_w_ple_proj': 'new_v', 'new_v_g_final': 'new_v'}


def _forward(args):
    return _fwd_reference(*[args[k] for k in FWD_PARAMS])


def _output_shape():
    out = _jax.eval_shape(lambda: _forward(_fwd_setup_inputs(0)))
    return out.shape, out.dtype

N_MICROBATCH = 1
ADAM_LR = 0.001
ADAM_B1 = 0.9
ADAM_B2 = 0.999
ADAM_EPS = 1e-08
ADAM_WD = 0.01
ADAM_STEP = 10
PER_EXAMPLE_BATCH_AXIS = {'x': 0, 'p': 1, 'positions': 0, 'loss_target': 0}
SHARED_INPUTS = []
_WEIGHT_DTYPES = {'g_mix': _jnp.float32, 'w_in': _jnp.float32, 'sink': _jnp.float32, 'g_q': _jnp.float32, 'w_uq': _jnp.float32, 'g_kv': _jnp.float32, 'w_ukv': _jnp.float32, 'w_br_a': _jnp.float32, 'w_br_b': _jnp.float32, 'w_out': _jnp.float32, 'g_ple': _jnp.float32, 'w_ple_gate': _jnp.float32, 'w_ple_proj': _jnp.float32, 'g_final': _jnp.float32}
MOMENT_SCALE = {'g_mix': 4.704943e-02, 'w_in': 2.288329e-02, 'sink': 3.680293e-02, 'g_q': 2.428490e-02, 'w_uq': 1.425211e-02, 'g_kv': 5.628648e-02, 'w_ukv': 1.811869e-02, 'w_br_a': 2.042587e-02, 'w_br_b': 1.474224e-02, 'w_out': 2.518272e-02, 'g_ple': 4.356883e-02, 'w_ple_gate': 4.247539e-02, 'w_ple_proj': 1.087687e-01, 'g_final': 6.378043e+01}


def _to_microbatches(a, axis):
    t = _jnp.moveaxis(a, axis, 0)
    t = t.reshape((N_MICROBATCH, t.shape[0] // N_MICROBATCH) + t.shape[1:])
    return _jnp.moveaxis(t, 1, axis + 1)


def setup_inputs(seed: int = 0) -> dict:
    inp = _fwd_setup_inputs(seed)
    key = _jax.random.fold_in(_jax.random.key(seed), 7919)
    shape, _ = _output_shape()
    out = dict(inp)
    out["loss_target"] = _jax.random.normal(_jax.random.fold_in(key, 0), shape, _jnp.float32)
    for i, name in enumerate(TWIN_WEIGHTS):
        w = inp[name].astype(_jnp.float32)
        if MOMENT_SCALE is None:
            s = _jnp.sqrt(_jnp.mean(_jnp.square(w)) + 1e-30)
        else:
            s = MOMENT_SCALE[name]
        km, kv = _jax.random.split(_jax.random.fold_in(key, i + 1))
        out[name] = w
        out["m_" + name] = s * _jax.random.normal(km, w.shape, _jnp.float32)
        out["v_" + name] = (s * s) * _jax.random.uniform(kv, w.shape, _jnp.float32, 0.5, 1.5)
    if N_MICROBATCH > 1:
        for name, axis in PER_EXAMPLE_BATCH_AXIS.items():
            out[name] = _to_microbatches(out[name], axis)
    return {'x': out['x'], 'p': out['p'], 'positions': out['positions'], 'g_mix': out['g_mix'], 'w_in': out['w_in'], 'sink': out['sink'], 'g_q': out['g_q'], 'w_uq': out['w_uq'], 'g_kv': out['g_kv'], 'w_ukv': out['w_ukv'], 'w_br_a': out['w_br_a'], 'w_br_b': out['w_br_b'], 'w_out': out['w_out'], 'g_ple': out['g_ple'], 'w_ple_gate': out['w_ple_gate'], 'w_ple_proj': out['w_ple_proj'], 'g_final': out['g_final'], 'loss_target': out['loss_target'], 'm_g_mix': out['m_g_mix'], 'm_w_in': out['m_w_in'], 'm_sink': out['m_sink'], 'm_g_q': out['m_g_q'], 'm_w_uq': out['m_w_uq'], 'm_g_kv': out['m_g_kv'], 'm_w_ukv': out['m_w_ukv'], 'm_w_br_a': out['m_w_br_a'], 'm_w_br_b': out['m_w_br_b'], 'm_w_out': out['m_w_out'], 'm_g_ple': out['m_g_ple'], 'm_w_ple_gate': out['m_w_ple_gate'], 'm_w_ple_proj': out['m_w_ple_proj'], 'm_g_final': out['m_g_final'], 'v_g_mix': out['v_g_mix'], 'v_w_in': out['v_w_in'], 'v_sink': out['v_sink'], 'v_g_q': out['v_g_q'], 'v_w_uq': out['v_w_uq'], 'v_g_kv': out['v_g_kv'], 'v_w_ukv': out['v_w_ukv'], 'v_w_br_a': out['v_w_br_a'], 'v_w_br_b': out['v_w_br_b'], 'v_w_out': out['v_w_out'], 'v_g_ple': out['v_g_ple'], 'v_w_ple_gate': out['v_w_ple_gate'], 'v_w_ple_proj': out['v_w_ple_proj'], 'v_g_final': out['v_g_final']}


def _loss(weights, diff, rest, loss_target):
    with _jax.named_scope("forward"):
        args = {**rest, TWIN_DIFF_INPUT: diff, **{k: w.astype(_WEIGHT_DTYPES[k]) for k, w in weights.items()}}
        y = _forward(args)
    with _jax.named_scope("loss_head"):
        err = _jnp.square(y.astype(_jnp.float32) - loss_target)
        return 0.5 * _jnp.sum(_jnp.mean(err, axis=-1)) if err.ndim else 0.5 * err


def _adamw(w, g, m, v):
    m = ADAM_B1 * m + (1.0 - ADAM_B1) * g
    v = ADAM_B2 * v + (1.0 - ADAM_B2) * _jnp.square(g)
    m_hat = m / (1.0 - ADAM_B1 ** ADAM_STEP)
    v_hat = v / (1.0 - ADAM_B2 ** ADAM_STEP)
    delta = -ADAM_LR * (m_hat / (_jnp.sqrt(v_hat) + ADAM_EPS) + ADAM_WD * w)
    return delta, m, v


def reference(x, p, positions, g_mix, w_in, sink, g_q, w_uq, g_kv, w_ukv, w_br_a, w_br_b, w_out, g_ple, w_ple_gate, w_ple_proj, g_final, loss_target, m_g_mix, m_w_in, m_sink, m_g_q, m_w_uq, m_g_kv, m_w_ukv, m_w_br_a, m_w_br_b, m_w_out, m_g_ple, m_w_ple_gate, m_w_ple_proj, m_g_final, v_g_mix, v_w_in, v_sink, v_g_q, v_w_uq, v_g_kv, v_w_ukv, v_w_br_a, v_w_br_b, v_w_out, v_g_ple, v_w_ple_gate, v_w_ple_proj, v_g_final):
    given = dict(x=x, p=p, positions=positions, g_mix=g_mix, w_in=w_in, sink=sink, g_q=g_q, w_uq=w_uq, g_kv=g_kv, w_ukv=w_ukv, w_br_a=w_br_a, w_br_b=w_br_b, w_out=w_out, g_ple=g_ple, w_ple_gate=w_ple_gate, w_ple_proj=w_ple_proj, g_final=g_final, loss_target=loss_target, m_g_mix=m_g_mix, m_w_in=m_w_in, m_sink=m_sink, m_g_q=m_g_q, m_w_uq=m_w_uq, m_g_kv=m_g_kv, m_w_ukv=m_w_ukv, m_w_br_a=m_w_br_a, m_w_br_b=m_w_br_b, m_w_out=m_w_out, m_g_ple=m_g_ple, m_w_ple_gate=m_w_ple_gate, m_w_ple_proj=m_w_ple_proj, m_g_final=m_g_final, v_g_mix=v_g_mix, v_w_in=v_w_in, v_sink=v_sink, v_g_q=v_g_q, v_w_uq=v_w_uq, v_g_kv=v_g_kv, v_w_ukv=v_w_ukv, v_w_br_a=v_w_br_a, v_w_br_b=v_w_br_b, v_w_out=v_w_out, v_g_ple=v_g_ple, v_w_ple_gate=v_w_ple_gate, v_w_ple_proj=v_w_ple_proj, v_g_final=v_g_final)
    weights = {n: given[n] for n in TWIN_WEIGHTS}
    shared = {n: given[n] for n in SHARED_INPUTS}
    per_example = {n: given[n] for n in ['x', 'p', 'positions']}
    grad_fn = _jax.value_and_grad(_loss, argnums=(0, 1))

    def one_microbatch(ex, loss_target):
        ex = dict(ex)
        diff = ex.pop(TWIN_DIFF_INPUT)
        return grad_fn(weights, diff, {**shared, **ex}, loss_target)

    if N_MICROBATCH == 1:
        loss, (grad_w, grad_x) = one_microbatch(per_example, given["loss_target"])
    else:
        def body(carry, xs):
            loss_sum, grad_sum = carry
            l_k, (gw_k, gx_k) = one_microbatch(xs[0], xs[1])
            with _jax.named_scope("update"):
                return (loss_sum + l_k, _jax.tree.map(_jnp.add, grad_sum, gw_k)), gx_k

        init = (_jnp.zeros((), _jnp.float32), _jax.tree.map(_jnp.zeros_like, weights))
        (loss, grad_w), grad_x = _jax.lax.scan(body, init, (per_example, given["loss_target"]))
    with _jax.named_scope("update"):
        delta_w, new_m, new_v = {}, {}, {}
        for n in TWIN_WEIGHTS:
            delta_w[n], new_m[n], new_v[n] = _adamw(weights[n], grad_w[n], given["m_" + n], given["v_" + n])
    return (loss, grad_x, *[grad_w[n] for n in TWIN_WEIGHTS], *[delta_w[n] for n in TWIN_WEIGHTS],
            *[new_m[n] for n in TWIN_WEIGHTS], *[new_v[n] for n in TWIN_WEIGHTS])
```

```python
import functools

import jax
import jax.numpy as jnp
from jax import lax
from jax.experimental import pallas as pl
from jax.experimental.pallas import tpu as pltpu

F32, BF16 = jnp.float32, jnp.bfloat16
SDS = jax.ShapeDtypeStruct

D = 1024
DEPTH = 2
PLE = 256
BLK = 128
EPS = 1e-6
NEG = -1e30
SWA_H, SWA_KV, SWA_DH = 8, 2, 64
MLA_H, MLA_NOPE, MLA_ROPE, MLA_V = 8, 64, 32, 64
MLA_QK = MLA_NOPE + MLA_ROPE
QL, KVL = 256, 128
IN_W = 4256
N_DEV = 8

V7X_VMEM_BYTES = 64 * 1024 * 1024
LANES = 128
VMEM_LIMIT = V7X_VMEM_BYTES * 7 // 8

ZW = 4352
Z_MA, Z_MB, Z_AQ, Z_AG, Z_BG, Z_QD, Z_AK, Z_AV, Z_KVD, Z_KR = 0, 1024, 2048, 2560, 3072, 3584, 3840, 3968, 4096, 4224
QFW = MLA_H * LANES
KVW = QFW + MLA_H * MLA_V
MLA_SCALE = MLA_QK ** -0.5
SWA_SCALE = SWA_DH ** -0.5
ROLL_UP, ROLL_DOWN = MLA_ROPE // 2, LANES - MLA_ROPE // 2

ADAM_LR, ADAM_B1, ADAM_B2, ADAM_EPS, ADAM_WD, ADAM_STEP = 0.001, 0.9, 0.999, 1e-08, 0.01, 10

FLAT_W = 1024


def _cp(*sem):
    return pltpu.CompilerParams(dimension_semantics=sem, vmem_limit_bytes=VMEM_LIMIT)


def _row(tm, w, col=0):
    return pl.BlockSpec((tm, w), lambda i: (i, col))


def _res(shape):
    return pl.BlockSpec(shape, lambda *_: (0,) * len(shape), pipeline_mode=pl.Buffered(1))


def _acc(shape):
    return pl.BlockSpec(shape, lambda *_: (0,) * len(shape))


def _rstd(xf):
    return lax.rsqrt(jnp.mean(xf * xf, axis=-1, keepdims=True) + EPS)


def _norm_bwd(dh, n, r, g):
    dn = dh * g
    return r * (dn - n * jnp.mean(dn * n, axis=-1, keepdims=True)), dh * n


def _nt(a, b):
    return lax.dot_general(a, b, (((1,), (1,)), ((), ())), preferred_element_type=F32)


def _tn(a, b):
    return lax.dot_general(a, b, (((0,), (0,)), ((), ())), preferred_element_type=F32)


def _nn(a, b):
    return jnp.dot(a, b, preferred_element_type=F32)


def _sig(x):
    return jax.nn.sigmoid(x)


def _rope(t, c, s1, s2):
    return t * c + pltpu.roll(t, ROLL_UP, 1) * s1 + pltpu.roll(t, ROLL_DOWN, 1) * s2


def _rope_t(d, c, s1, s2):
    return d * c + pltpu.roll(d * s1, ROLL_DOWN, 1) + pltpu.roll(d * s2, ROLL_UP, 1)


def _fwd_in(x, g, w, tm):
    T = x.shape[0]

    def body(x_ref, g_ref, w_ref, z_ref, h_ref):
        xf = x_ref[...]
        h = ((xf * _rstd(xf)) * g_ref[...]).astype(BF16)
        h_ref[...] = h
        z_ref[...] = _nn(h, w_ref[...])

    return pl.pallas_call(
        body, grid=(T // tm,), name="fwd_in",
        in_specs=[_row(tm, D), _res((1, D)), _res((D, ZW))],
        out_specs=[_row(tm, ZW), _row(tm, D)],
        out_shape=[SDS((T, ZW), F32), SDS((T, D), BF16)],
        compiler_params=_cp("parallel"))(x, g, w)


def _fwd_prep(z, gq, gkv, wq, wkv, tc, ts1, ts2, tm):
    T = z.shape[0]

    def body(qd_ref, kvd_ref, kr_ref, gq_ref, gkv_ref, wq_ref, wkv_ref, c_ref, s1_ref, s2_ref, q_ref, k_ref, v_ref):
        qd, kvd = qd_ref[...], kvd_ref[...]
        hq = ((qd * _rstd(qd)) * gq_ref[...]).astype(BF16)
        hkv = ((kvd * _rstd(kvd)) * gkv_ref[...]).astype(BF16)
        qf = _nn(hq, wq_ref[...])
        kvf = _nn(hkv, wkv_ref[...])
        c, s1, s2 = c_ref[...], s1_ref[...], s2_ref[...]
        krb = _rope(kr_ref[...], c, s1, s2)
        for h in range(MLA_H):
            sl = slice(LANES * h, LANES * (h + 1))
            q_ref[:, sl] = _rope(qf[:, sl], c, s1, s2).astype(BF16)
            k_ref[:, sl] = (kvf[:, sl] + krb).astype(BF16)
        v_ref[...] = kvf[:, QFW:].astype(BF16)

    return pl.pallas_call(
        body, grid=(T // tm,), name="fwd_prep",
        in_specs=[_row(tm, QL, Z_QD // QL), _row(tm, KVL, Z_KVD // KVL), _row(tm, LANES, Z_KR // LANES),
                  _res((1, QL)), _res((1, KVL)), _res((QL, QFW)), _res((KVL, KVW)),
                  _row(tm, LANES), _row(tm, LANES), _row(tm, LANES)],
        out_specs=[_row(tm, QFW), _row(tm, QFW), _row(tm, MLA_H * MLA_V)],
        out_shape=[SDS((T, QFW), BF16), SDS((T, QFW), BF16), SDS((T, MLA_H * MLA_V), BF16)],
        compiler_params=_cp("parallel"))(z, z, z, gq, gkv, wq, wkv, tc, ts1, ts2)


def _mla_fwd(qf, kf, v, nb, seq, tq):
    T = qf.shape[0]
    nq = seq // tq
    pw = 2 * LANES

    def body(q_ref, k_ref, v_ref, o_ref, lse_ref, m_s, l_s, acc_s):
        qi, ki = pl.program_id(2), pl.program_id(3)

        @pl.when(ki == 0)
        def _():
            m_s[...] = jnp.full(m_s.shape, NEG, F32)
            l_s[...] = jnp.zeros(l_s.shape, F32)
            acc_s[...] = jnp.zeros(acc_s.shape, F32)

        @pl.when(ki <= qi)
        def _():
            rows = qi * tq + lax.broadcasted_iota(jnp.int32, (tq, tq), 0)
            cols = ki * tq + lax.broadcasted_iota(jnp.int32, (tq, tq), 1)
            mask = cols <= rows
            for j in range(2):
                s = _nt(q_ref[:, LANES * j:LANES * (j + 1)], k_ref[:, LANES * j:LANES * (j + 1)]) * MLA_SCALE
                s = jnp.where(mask, s, NEG)
                m_prev = m_s[j]
                m_new = jnp.maximum(m_prev, jnp.max(s, axis=-1, keepdims=True))
                alpha = jnp.exp(m_prev - m_new)
                p = jnp.exp(s - m_new)
                l_s[j] = alpha * l_s[j] + jnp.sum(p, axis=-1, keepdims=True)
                m_s[j] = m_new
                sl = slice(MLA_V * j, MLA_V * (j + 1))
                acc_s[:, sl] = alpha * acc_s[:, sl] + _nn(p.astype(BF16), v_ref[:, sl])

        @pl.when(ki == qi)
        def _():
            for j in range(2):
                sl = slice(MLA_V * j, MLA_V * (j + 1))
                o_ref[:, sl] = acc_s[:, sl] / l_s[j]
                lse_ref[j] = m_s[j] + jnp.log(l_s[j])

    kv_map = lambda b, hp, qi, ki: (b * nq + jnp.minimum(ki, qi), hp)
    return pl.pallas_call(
        body, grid=(nb, MLA_H // 2, nq, nq), name="mla_fwd",
        in_specs=[pl.BlockSpec((tq, pw), lambda b, hp, qi, ki: (b * nq + qi, hp)),
                  pl.BlockSpec((tq, pw), kv_map),
                  pl.BlockSpec((tq, LANES), kv_map)],
        out_specs=[pl.BlockSpec((tq, LANES), lambda b, hp, qi, ki: (b * nq + qi, hp)),
                   pl.BlockSpec((2, tq, 1), lambda b, hp, qi, ki: (hp, b * nq + qi, 0))],
        out_shape=[SDS((T, MLA_H * MLA_V), F32), SDS((MLA_H, T, 1), F32)],
        scratch_shapes=[pltpu.VMEM((2, tq, 1), F32), pltpu.VMEM((2, tq, 1), F32), pltpu.VMEM((tq, LANES), F32)],
        compiler_params=_cp("parallel", "parallel", "parallel", "arbitrary"))(qf, kf, v)


def _swa_valid(n):
    r = lax.broadcasted_iota(jnp.int32, (BLK, 2 * BLK), 0)
    c = lax.broadcasted_iota(jnp.int32, (BLK, 2 * BLK), 1)
    return (c > r) & (c <= r + BLK) & ((c >= BLK) | (n > 0))


def _swa_specs(nblk):
    cur = lambda b, n: (b * nblk + n, 0)
    prev = lambda b, n: (b * nblk + jnp.maximum(n - 1, 0), 0)
    kvc = Z_AK // (2 * BLK)
    return [pl.BlockSpec(memory_space=pltpu.SMEM),
            pl.BlockSpec((BLK, 512), lambda b, n: (b * nblk + n, Z_AQ // 512)),
            pl.BlockSpec((BLK, 2 * BLK), lambda b, n: (b * nblk + n, kvc)),
            pl.BlockSpec((BLK, 2 * BLK), lambda b, n: (b * nblk + jnp.maximum(n - 1, 0), kvc)),
            pl.BlockSpec((BLK, 1), cur),
            pl.BlockSpec((1, 1, BLK), lambda b, n: (b * nblk + n, 0, 0)),
            pl.BlockSpec((1, 1, BLK), lambda b, n: (b * nblk + jnp.maximum(n - 1, 0), 0, 0))]


def _swa_scores(n, q_ref, kvc_ref, kvp_ref, pc_ref, prc_ref, prp_ref):
    kv = jnp.concatenate([kvp_ref[...], kvc_ref[...]], axis=0)
    kb, vb = kv[:, :BLK].astype(BF16), kv[:, BLK:].astype(BF16)
    dist = pc_ref[...] - jnp.concatenate([prp_ref[0], prc_ref[0]], axis=1)
    valid = _swa_valid(n)

    def scores(h):
        g = h // (SWA_H // SWA_KV)
        qh = q_ref[:, SWA_DH * h:SWA_DH * (h + 1)].astype(BF16)
        s = _nt(qh, kb[:, SWA_DH * g:SWA_DH * (g + 1)]) * SWA_SCALE - (2.0 ** -(h + 1)) * dist
        return qh, jnp.where(valid, s, NEG)

    return kb, vb, scores


def _swa_fwd(sink, z, pos_col, pos_row, nb, seq):
    T = z.shape[0]
    nblk = seq // BLK

    def body(sink_ref, q_ref, kvc_ref, kvp_ref, pc_ref, prc_ref, prp_ref, o_ref, lse_ref):
        kb, vb, scores = _swa_scores(pl.program_id(1), q_ref, kvc_ref, kvp_ref, pc_ref, prc_ref, prp_ref)
        for h in range(SWA_H):
            g = h // (SWA_H // SWA_KV)
            _, s = scores(h)
            sk = sink_ref[h]
            m = jnp.maximum(jnp.max(s, axis=-1, keepdims=True), sk)
            e = jnp.exp(s - m)
            den = jnp.sum(e, axis=-1, keepdims=True) + jnp.exp(sk - m)
            o_ref[:, SWA_DH * h:SWA_DH * (h + 1)] = _nn(e.astype(BF16), vb[:, SWA_DH * g:SWA_DH * (g + 1)]) / den
            lse_ref[:, h:h + 1] = m + jnp.log(den)

    return pl.pallas_call(
        body, grid=(nb, nblk), name="swa_fwd",
        in_specs=_swa_specs(nblk),
        out_specs=[pl.BlockSpec((BLK, 512), lambda b, n: (b * nblk + n, 0)),
                   pl.BlockSpec((BLK, SWA_H), lambda b, n: (b * nblk + n, 0))],
        out_shape=[SDS((T, 512), F32), SDS((T, SWA_H), F32)],
        compiler_params=_cp("parallel", "parallel"))(sink, z, z, z, pos_col, pos_row, pos_row)


def _fwd_merge(x, oa, ob, z, wa, wb, wo, tm):
    T = x.shape[0]

    def body(x_ref, oa_ref, ob_ref, ag_ref, bg_ref, ma_ref, mb_ref, wa_ref, wb_ref, wo_ref, x1_ref, ua_ref, ub_ref):
        ag, bg = ag_ref[...], bg_ref[...]
        ua = _nn((oa_ref[...] * (ag * _sig(ag))).astype(BF16), wa_ref[...])
        ub = _nn((ob_ref[...] * (bg * _sig(bg))).astype(BF16), wb_ref[...])
        ua_ref[...] = ua
        ub_ref[...] = ub
        y = _sig(ma_ref[...]) * ua + _sig(mb_ref[...]) * ub
        x1_ref[...] = x_ref[...] + _nn(y.astype(BF16), wo_ref[...])

    return pl.pallas_call(
        body, grid=(T // tm,), name="fwd_merge",
        in_specs=[_row(tm, D), _row(tm, 512), _row(tm, 512), _row(tm, 512, Z_AG // 512), _row(tm, 512, Z_BG // 512),
                  _row(tm, D, Z_MA // D), _row(tm, D, Z_MB // D), _res((512, D)), _res((512, D)), _res((D, D))],
        out_specs=[_row(tm, D)] * 3,
        out_shape=[SDS((T, D), F32)] * 3,
        compiler_params=_cp("parallel"))(x, oa, ob, z, z, z, z, wa, wb, wo)


def _fwd_ple(x1, p, g, wpg, wpp, tm):
    T = x1.shape[0]

    def body(x_ref, p_ref, g_ref, wpg_ref, wpp_ref, x2_ref, pg_ref, pp_ref):
        xf = x_ref[...]
        h1 = ((xf * _rstd(xf)) * g_ref[...]).astype(BF16)
        pg = _sig(_nn(h1, wpg_ref[...]))
        pp = _nn(p_ref[...].astype(BF16), wpp_ref[...])
        pg_ref[...] = pg
        pp_ref[...] = pp
        x2_ref[...] = xf + pg * pp

    return pl.pallas_call(
        body, grid=(T // tm,), name="fwd_ple",
        in_specs=[_row(tm, D), _row(tm, PLE), _res((1, D)), _res((D, D)), _res((PLE, D))],
        out_specs=[_row(tm, D)] * 3,
        out_shape=[SDS((T, D), F32)] * 3,
        compiler_params=_cp("parallel"))(x1, p, g, wpg, wpp)


def _loss_head(x, g, tgt, tm):
    T = x.shape[0]

    def body(x_ref, g_ref, t_ref, dx_ref, dg_ref, loss_ref):
        @pl.when(pl.program_id(0) == 0)
        def _():
            dg_ref[...] = jnp.zeros(dg_ref.shape, F32)
            loss_ref[...] = jnp.zeros(loss_ref.shape, F32)

        xf, gf = x_ref[...], g_ref[...]
        r = _rstd(xf)
        n = xf * r
        err = n * gf - t_ref[...]
        loss_ref[...] += 0.5 * jnp.sum(jnp.mean(err * err, axis=-1, keepdims=True), axis=0, keepdims=True)
        dx, dgr = _norm_bwd(err * (1.0 / D), n, r, gf)
        dx_ref[...] = dx
        dg_ref[...] += jnp.sum(dgr, axis=0, keepdims=True)

    return pl.pallas_call(
        body, grid=(T // tm,), name="loss_head",
        in_specs=[_row(tm, D), _res((1, D)), _row(tm, D)],
        out_specs=[_row(tm, D), _acc((1, D)), _acc((1, LANES))],
        out_shape=[SDS((T, D), F32), SDS((1, D), F32), SDS((1, LANES), F32)],
        compiler_params=_cp("arbitrary"))(x, g, tgt)


def _bwd_ple(dx2, x1, pg, pp, p, g, wpg, tm):
    T = x1.shape[0]

    def body(d_ref, x_ref, pg_ref, pp_ref, p_ref, g_ref, w_ref, dx_ref, dwg_ref, dwp_ref, dg_ref):
        @pl.when(pl.program_id(0) == 0)
        def _():
            dwg_ref[...] = jnp.zeros(dwg_ref.shape, F32)
            dwp_ref[...] = jnp.zeros(dwp_ref.shape, F32)
            dg_ref[...] = jnp.zeros(dg_ref.shape, F32)

        d, xf, pg, gf = d_ref[...], x_ref[...], pg_ref[...], g_ref[...]
        r = _rstd(xf)
        n = xf * r
        dpgl = (d * pp_ref[...] * pg * (1.0 - pg)).astype(BF16)
        dwg_ref[...] += _tn((n * gf).astype(BF16), dpgl)
        dwp_ref[...] += _tn(p_ref[...].astype(BF16), (d * pg).astype(BF16))
        dxn, dgr = _norm_bwd(_nt(dpgl, w_ref[...]), n, r, gf)
        dx_ref[...] = d + dxn
        dg_ref[...] += jnp.sum(dgr, axis=0, keepdims=True)

    return pl.pallas_call(
        body, grid=(T // tm,), name="bwd_ple",
        in_specs=[_row(tm, D)] * 4 + [_row(tm, PLE), _res((1, D)), _res((D, D))],
        out_specs=[_row(tm, D), _acc((D, D)), _acc((PLE, D)), _acc((1, D))],
        out_shape=[SDS((T, D), F32), SDS((D, D), F32), SDS((PLE, D), F32), SDS((1, D), F32)],
        compiler_params=_cp("arbitrary"))(dx2, x1, pg, pp, p, g, wpg)


def _bwd_merge(dx1, oa, ob, z, ua, ub, wa, wb, wo, tm):
    T = dx1.shape[0]

    def body(d_ref, oa_ref, ob_ref, ag_ref, bg_ref, ma_ref, mb_ref, ua_ref, ub_ref, wa_ref, wb_ref, wo_ref,
             doa_ref, dob_ref, dag_ref, dbg_ref, dma_ref, dmb_ref, dwa_ref, dwb_ref, dwo_ref):
        @pl.when(pl.program_id(0) == 0)
        def _():
            dwa_ref[...] = jnp.zeros(dwa_ref.shape, F32)
            dwb_ref[...] = jnp.zeros(dwb_ref.shape, F32)
            dwo_ref[...] = jnp.zeros(dwo_ref.shape, F32)

        db = d_ref[...].astype(BF16)
        ua, ub = ua_ref[...], ub_ref[...]
        sa, sb = _sig(ma_ref[...]), _sig(mb_ref[...])
        dwo_ref[...] += _tn((sa * ua + sb * ub).astype(BF16), db)
        dy = _nt(db, wo_ref[...])
        dma_ref[...] = (dy * ua * sa * (1.0 - sa)).astype(BF16)
        dmb_ref[...] = (dy * ub * sb * (1.0 - sb)).astype(BF16)
        for (o_ref, gate_ref, s, w_ref, do_ref, dgate_ref, dw_ref) in (
                (oa_ref, ag_ref, sa, wa_ref, doa_ref, dag_ref, dwa_ref),
                (ob_ref, bg_ref, sb, wb_ref, dob_ref, dbg_ref, dwb_ref)):
            du = (dy * s).astype(BF16)
            raw, gate = o_ref[...], gate_ref[...]
            sg = _sig(gate)
            silu = gate * sg
            dw_ref[...] += _tn((raw * silu).astype(BF16), du)
            do = _nt(du, w_ref[...])
            do_ref[...] = do * silu
            dgate_ref[...] = (do * raw * (sg * (1.0 + gate * (1.0 - sg)))).astype(BF16)

    return pl.pallas_call(
        body, grid=(T // tm,), name="bwd_merge",
        in_specs=[_row(tm, D), _row(tm, 512), _row(tm, 512), _row(tm, 512, Z_AG // 512), _row(tm, 512, Z_BG // 512),
                  _row(tm, D, Z_MA // D), _row(tm, D, Z_MB // D), _row(tm, D), _row(tm, D),
                  _res((512, D)), _res((512, D)), _res((D, D))],
        out_specs=[_row(tm, 512)] * 4 + [_row(tm, D)] * 2 + [_acc((512, D)), _acc((512, D)), _acc((D, D))],
        out_shape=[SDS((T, 512), F32), SDS((T, 512), F32), SDS((T, 512), BF16), SDS((T, 512), BF16),
                   SDS((T, D), BF16), SDS((T, D), BF16), SDS((512, D), F32), SDS((512, D), F32), SDS((D, D), F32)],
        compiler_params=_cp("arbitrary"))(dx1, oa, ob, z, z, z, z, ua, ub, wa, wb, wo)


def _mla_bwd(qf, kf, v, do, o, lse, nb, seq, tq):
    T = qf.shape[0]
    nq = seq // tq
    pw = 2 * LANES

    def body(q_ref, k_ref, v_ref, do_ref, o_ref, lse_ref, dq_ref, dk_ref, dv_ref, dk_s, dv_s):
        ki, qi = pl.program_id(2), pl.program_id(3)

        @pl.when((ki == 0) & (qi == 0))
        def _():
            dq_ref[...] = jnp.zeros(dq_ref.shape, F32)

        @pl.when(qi == ki)
        def _():
            dk_s[...] = jnp.zeros(dk_s.shape, F32)
            dv_s[...] = jnp.zeros(dv_s.shape, F32)

        @pl.when(qi >= ki)
        def _():
            rows = qi * tq + lax.broadcasted_iota(jnp.int32, (tq, tq), 0)
            cols = ki * tq + lax.broadcasted_iota(jnp.int32, (tq, tq), 1)
            mask = cols <= rows
            qrows = pl.ds(pl.multiple_of(qi * tq, tq), tq)
            for j in range(2):
                wide = slice(LANES * j, LANES * (j + 1))
                sl = slice(MLA_V * j, MLA_V * (j + 1))
                q, k, vj = q_ref[:, wide], k_ref[:, wide], v_ref[:, sl]
                doj = do_ref[:, sl]
                dsum = jnp.sum(doj * o_ref[:, sl], axis=-1, keepdims=True)
                s = jnp.where(mask, _nt(q, k) * MLA_SCALE, NEG)
                p = jnp.exp(s - lse_ref[j])
                dob = doj.astype(BF16)
                dv_s[:, sl] += _tn(p.astype(BF16), dob)
                ds = (p * (_nt(dob, vj) - dsum) * MLA_SCALE).astype(BF16)
                dk_s[:, wide] += _tn(ds, q)
                dq_ref[qrows, wide] += _nn(ds, k)

        @pl.when(qi == nq - 1)
        def _():
            dk_ref[...] = dk_s[...]
            dv_ref[...] = dv_s[...]

    qmap = lambda b, hp, ki, qi: (b * nq + jnp.maximum(qi, ki), hp)
    kmap = lambda b, hp, ki, qi: (b * nq + ki, hp)
    return pl.pallas_call(
        body, grid=(nb, MLA_H // 2, nq, nq), name="mla_bwd",
        in_specs=[pl.BlockSpec((tq, pw), qmap), pl.BlockSpec((tq, pw), kmap), pl.BlockSpec((tq, LANES), kmap),
                  pl.BlockSpec((tq, LANES), qmap), pl.BlockSpec((tq, LANES), qmap),
                  pl.BlockSpec((2, tq, 1), lambda b, hp, ki, qi: (hp, b * nq + jnp.maximum(qi, ki), 0))],
        out_specs=[pl.BlockSpec((seq, pw), lambda b, hp, ki, qi: (b, hp)),
                   pl.BlockSpec((tq, pw), kmap), pl.BlockSpec((tq, LANES), kmap)],
        out_shape=[SDS((T, QFW), F32), SDS((T, QFW), F32), SDS((T, MLA_H * MLA_V), F32)],
        scratch_shapes=[pltpu.VMEM((tq, pw), F32), pltpu.VMEM((tq, LANES), F32)],
        compiler_params=_cp("parallel", "parallel", "arbitrary", "arbitrary"))(qf, kf, v, do, o, lse)


def _swa_bwd(sink, z, pos_col, pos_row, do, o, lse, nb, seq):
    T = z.shape[0]
    nblk = seq // BLK

    def body(sink_ref, q_ref, kvc_ref, kvp_ref, pc_ref, prc_ref, prp_ref, do_ref, o_ref, lse_ref,
             dq_ref, dkv_ref, dsink_ref):
        b, n = pl.program_id(0), pl.program_id(1)

        @pl.when((b == 0) & (n == 0))
        def _():
            dsink_ref[...] = jnp.zeros(dsink_ref.shape, F32)

        @pl.when(n == 0)
        def _():
            dkv_ref[...] = jnp.zeros(dkv_ref.shape, F32)

        kb, vb, scores = _swa_scores(n, q_ref, kvc_ref, kvp_ref, pc_ref, prc_ref, prp_ref)
        lane = lax.broadcasted_iota(jnp.int32, (1, LANES), 1)
        dsink = jnp.zeros((1, LANES), F32)
        dkv = [[None, None], [None, None]]
        for h in range(SWA_H):
            g = h // (SWA_H // SWA_KV)
            hs = slice(SWA_DH * h, SWA_DH * (h + 1))
            gs = slice(SWA_DH * g, SWA_DH * (g + 1))
            qh, s = scores(h)
            lse = lse_ref[:, h:h + 1]
            p = jnp.exp(s - lse)
            doh = do_ref[:, hs]
            dsum = jnp.sum(doh * o_ref[:, hs], axis=-1, keepdims=True)
            dob = doh.astype(BF16)
            ds = (p * (_nt(dob, vb[:, gs]) - dsum) * SWA_SCALE).astype(BF16)
            dq_ref[:, hs] = _nn(ds, kb[:, gs]).astype(BF16)
            dk, dv = _tn(ds, qh), _tn(p.astype(BF16), dob)
            dkv[g][0] = dk if dkv[g][0] is None else dkv[g][0] + dk
            dkv[g][1] = dv if dkv[g][1] is None else dkv[g][1] + dv
            dsk = jnp.sum(-jnp.exp(sink_ref[h] - lse) * dsum, axis=0, keepdims=True)
            dsink = dsink + jnp.where(lane == h, dsk, 0.0)
        dsink_ref[...] += dsink
        upd = jnp.concatenate([dkv[0][0], dkv[1][0], dkv[0][1], dkv[1][1]], axis=1)
        dkv_ref[pl.ds(pl.multiple_of(n * BLK, BLK), BLK), :] += upd[BLK:]

        @pl.when(n > 0)
        def _():
            dkv_ref[pl.ds(pl.multiple_of((n - 1) * BLK, BLK), BLK), :] += upd[:BLK]

    return pl.pallas_call(
        body, grid=(nb, nblk), name="swa_bwd",
        in_specs=_swa_specs(nblk) + [pl.BlockSpec((BLK, 512), lambda b, n: (b * nblk + n, 0))] * 2
        + [pl.BlockSpec((BLK, SWA_H), lambda b, n: (b * nblk + n, 0))],
        out_specs=[pl.BlockSpec((BLK, 512), lambda b, n: (b * nblk + n, 0)),
                   pl.BlockSpec((seq, 2 * BLK), lambda b, n: (b, 0)),
                   pl.BlockSpec((1, LANES), lambda b, n: (0, 0))],
        out_shape=[SDS((T, 512), BF16), SDS((T, 2 * BLK), F32), SDS((1, LANES), F32)],
        compiler_params=_cp("arbitrary", "arbitrary"))(sink, z, z, z, pos_col, pos_row, pos_row, do, o, lse)


def _bwd_prep(dq, dk, dv, z, gq, gkv, wq, wkv, tc, ts1, ts2, tm):
    T = z.shape[0]

    def body(dq_ref, dk_ref, dv_ref, qd_ref, kvd_ref, gq_ref, gkv_ref, wq_ref, wkv_ref, c_ref, s1_ref, s2_ref,
             dqd_ref, dkvd_ref, dkr_ref, dwq_ref, dwkv_ref, dgq_ref, dgkv_ref, dqb_s, dkvb_s):
        @pl.when(pl.program_id(0) == 0)
        def _():
            for ref in (dwq_ref, dwkv_ref, dgq_ref, dgkv_ref):
                ref[...] = jnp.zeros(ref.shape, F32)

        c, s1, s2 = c_ref[...], s1_ref[...], s2_ref[...]
        lane = lax.broadcasted_iota(jnp.int32, (1, LANES), 1)
        rope_lanes = (lane >= MLA_NOPE) & (lane < MLA_QK)
        dkb = jnp.zeros((tm, LANES), F32)
        for h in range(MLA_H):
            sl = slice(LANES * h, LANES * (h + 1))
            dqb_s[:, sl] = _rope_t(dq_ref[:, sl], c, s1, s2).astype(BF16)
            dkh = dk_ref[:, sl]
            dkb = dkb + dkh
            dkvb_s[:, sl] = dkh.astype(BF16)
        dkvb_s[:, QFW:] = dv_ref[...].astype(BF16)
        dkr_ref[...] = _rope_t(jnp.where(rope_lanes, dkb, 0.0), c, s1, s2).astype(BF16)

        for (x_ref, g_ref, w_ref, d_s, dx_ref, dw_ref, dg_ref) in (
                (qd_ref, gq_ref, wq_ref, dqb_s, dqd_ref, dwq_ref, dgq_ref),
                (kvd_ref, gkv_ref, wkv_ref, dkvb_s, dkvd_ref, dwkv_ref, dgkv_ref)):
            xf, gf, db = x_ref[...], g_ref[...], d_s[...]
            r = _rstd(xf)
            n = xf * r
            dw_ref[...] += _tn((n * gf).astype(BF16), db)
            dx, dgr = _norm_bwd(_nt(db, w_ref[...]), n, r, gf)
            dx_ref[...] = dx.astype(BF16)
            dg_ref[...] += jnp.sum(dgr, axis=0, keepdims=True)

    return pl.pallas_call(
        body, grid=(T // tm,), name="bwd_prep",
        in_specs=[_row(tm, QFW), _row(tm, QFW), _row(tm, MLA_H * MLA_V),
                  _row(tm, QL, Z_QD // QL), _row(tm, KVL, Z_KVD // KVL),
                  _res((1, QL)), _res((1, KVL)), _res((QL, QFW)), _res((KVL, KVW)),
                  _row(tm, LANES), _row(tm, LANES), _row(tm, LANES)],
        out_specs=[_row(tm, QL), _row(tm, KVL), _row(tm, LANES),
                   _acc((QL, QFW)), _acc((KVL, KVW)), _acc((1, QL)), _acc((1, KVL))],
        out_shape=[SDS((T, QL), BF16), SDS((T, KVL), BF16), SDS((T, LANES), BF16),
                   SDS((QL, QFW), F32), SDS((KVL, KVW), F32), SDS((1, QL), F32), SDS((1, KVL), F32)],
        scratch_shapes=[pltpu.VMEM((tm, QFW), BF16), pltpu.VMEM((tm, KVW), BF16)],
        compiler_params=_cp("arbitrary"))(dq, dk, dv, z, z, gq, gkv, wq, wkv, tc, ts1, ts2)


def _bwd_in(pieces, x, g, dres, w, tm):
    T = x.shape[0]
    widths = [pc.shape[1] for pc in pieces]
    assert sum(widths) == ZW
    n_p = len(pieces)

    def body(*refs):
        p_refs, (x_ref, g_ref, r_ref, w_ref, dx_ref, dz_ref, dg_ref) = refs[:n_p], refs[n_p:]

        @pl.when(pl.program_id(0) == 0)
        def _():
            dg_ref[...] = jnp.zeros(dg_ref.shape, F32)

        off = 0
        for ref, wd in zip(p_refs, widths):
            dz_ref[:, off:off + wd] = ref[...].astype(BF16)
            off += wd
        xf, gf = x_ref[...], g_ref[...]
        r = _rstd(xf)
        n = xf * r
        dx, dgr = _norm_bwd(_nt(dz_ref[...], w_ref[...]), n, r, gf)
        dx_ref[...] = r_ref[...] + dx
        dg_ref[...] += jnp.sum(dgr, axis=0, keepdims=True)

    return pl.pallas_call(
        body, grid=(T // tm,), name="bwd_in",
        in_specs=[_row(tm, wd) for wd in widths] + [_row(tm, D), _res((1, D)), _row(tm, D), _res((D, ZW))],
        out_specs=[_row(tm, D), _row(tm, ZW), _acc((1, D))],
        out_shape=[SDS((T, D), F32), SDS((T, ZW), BF16), SDS((1, D), F32)],
        compiler_params=_cp("arbitrary"))(*pieces, x, g, dres, w)


def _wgrad_in(hb, dzb, tm):
    T = hb.shape[0]
    half = ZW // 2

    def body(h_ref, dz_ref, dw_ref):
        @pl.when(pl.program_id(1) == 0)
        def _():
            dw_ref[...] = jnp.zeros(dw_ref.shape, F32)

        dw_ref[...] += _tn(h_ref[...], dz_ref[...])

    return pl.pallas_call(
        body, grid=(2, T // tm), name="wgrad_in",
        in_specs=[pl.BlockSpec((tm, D), lambda j, t: (t, 0)), pl.BlockSpec((tm, half), lambda j, t: (t, j))],
        out_specs=pl.BlockSpec((D, half), lambda j, t: (0, j)),
        out_shape=SDS((D, ZW), F32),
        compiler_params=_cp("parallel", "arbitrary"))(hb, dzb)


def _win_to_kernel(w):
    s = lambda a, b: w[..., a:b]
    zero = lambda n: jnp.zeros(w.shape[:-1] + (n,), w.dtype)
    return jnp.concatenate([s(2208, 3232), s(3232, 4256), s(0, 512), s(768, 1280), s(1696, 2208), s(1280, 1536),
                            s(512, 640), s(640, 768), s(1536, 1664), zero(64), s(1664, 1696), zero(32)], axis=-1)


def _win_from_kernel(g):
    s = lambda a, n: g[..., a:a + n]
    return jnp.concatenate([s(Z_AQ, 512), s(Z_AK, 128), s(Z_AV, 128), s(Z_AG, 512), s(Z_QD, 256), s(Z_KVD, 128),
                            s(Z_KR + MLA_NOPE, MLA_ROPE), s(Z_BG, 512), s(Z_MA, 1024), s(Z_MB, 1024)], axis=-1)


def _wuq_to_kernel(w):
    w = w.reshape(w.shape[:-1] + (MLA_H, MLA_QK))
    w = jnp.pad(w, [(0, 0)] * (w.ndim - 1) + [(0, LANES - MLA_QK)])
    return w.reshape(w.shape[:-2] + (QFW,))


def _wuq_from_kernel(g):
    g = g.reshape(g.shape[:-1] + (MLA_H, LANES))[..., :MLA_QK]
    return g.reshape(g.shape[:-2] + (MLA_H * MLA_QK,))


def _wukv_to_kernel(w):
    w = w.reshape(w.shape[:-1] + (MLA_H, MLA_NOPE + MLA_V))
    k = jnp.pad(w[..., :MLA_NOPE], [(0, 0)] * (w.ndim - 1) + [(0, LANES - MLA_NOPE)])
    v = w[..., MLA_NOPE:]
    return jnp.concatenate([k.reshape(k.shape[:-2] + (QFW,)), v.reshape(v.shape[:-2] + (MLA_H * MLA_V,))], axis=-1)


def _wukv_from_kernel(g):
    k = g[..., :QFW].reshape(g.shape[:-1] + (MLA_H, LANES))[..., :MLA_NOPE]
    v = g[..., QFW:].reshape(g.shape[:-1] + (MLA_H, MLA_V))
    kv = jnp.concatenate([k, v], axis=-1)
    return kv.reshape(kv.shape[:-2] + (MLA_H * (MLA_NOPE + MLA_V),))


def _rope_tables(pos):
    half = MLA_ROPE // 2
    inv = 10000.0 ** (-jnp.arange(0, MLA_ROPE, 2, dtype=F32) / MLA_ROPE)
    ang = pos.astype(F32)[:, None] * inv
    cos, sin = jnp.cos(ang), jnp.sin(ang)
    one = jnp.ones((pos.shape[0], MLA_NOPE), F32)
    zero = lambda n: jnp.zeros((pos.shape[0], n), F32)
    tc = jnp.concatenate([one, cos, cos, one[:, :LANES - MLA_QK]], axis=1)
    ts1 = jnp.concatenate([zero(MLA_NOPE + half), sin, zero(LANES - MLA_QK)], axis=1)
    ts2 = jnp.concatenate([zero(MLA_NOPE), -sin, zero(LANES - MLA_NOPE - half)], axis=1)
    return tc, ts1, ts2


def _local_step(x, p, positions, loss_target, small, wts):
    nb, seq, _ = x.shape
    T = nb * seq
    tm = min(256, T)
    tq = min(512, seq)
    xf = x.reshape(T, D)
    pos = positions.reshape(T)
    posf = pos.astype(F32)
    pos_col, pos_row = posf.reshape(T, 1), posf.reshape(T // BLK, 1, BLK)
    tc, ts1, ts2 = _rope_tables(pos)

    saved = []
    for i in range(DEPTH):
        w, sm = wts[i], small[i]
        z, hb = _fwd_in(xf, sm["g_mix"], w["w_in"], tm)
        oa, lse_a = _swa_fwd(sm["sink"], z, pos_col, pos_row, nb, seq)
        qf, kf, v = _fwd_prep(z, sm["g_q"], sm["g_kv"], w["w_uq"], w["w_ukv"], tc, ts1, ts2, tm)
        ob, lse_b = _mla_fwd(qf, kf, v, nb, seq, tq)
        x1, ua, ub = _fwd_merge(xf, oa, ob, z, w["w_br_a"], w["w_br_b"], w["w_out"], tm)
        x2, pg, pp = _fwd_ple(x1, p[i].reshape(T, PLE), sm["g_ple"], w["w_ple_gate"], w["w_ple_proj"], tm)
        saved.append(dict(x=xf, z=z, hb=hb, oa=oa, lse_a=lse_a, qf=qf, kf=kf, v=v, ob=ob, lse_b=lse_b,
                          x1=x1, ua=ua, ub=ub, pg=pg, pp=pp))
        xf = x2

    dx, dg_final, loss = _loss_head(xf, small["g_final"], loss_target.reshape(T, D), tm)

    grads = [None] * DEPTH
    for i in reversed(range(DEPTH)):
        w, sm, sv = wts[i], small[i], saved[i]
        dx1, dwpg, dwpp, dg_ple = _bwd_ple(dx, sv["x1"], sv["pg"], sv["pp"], p[i].reshape(T, PLE), sm["g_ple"],
                                           w["w_ple_gate"], tm)
        doa, dob, dag, dbg, dma, dmb, dwa, dwb, dwo = _bwd_merge(
            dx1, sv["oa"], sv["ob"], sv["z"], sv["ua"], sv["ub"], w["w_br_a"], w["w_br_b"], w["w_out"], tm)
        dq_b, dk_b, dv_b = _mla_bwd(sv["qf"], sv["kf"], sv["v"], dob, sv["ob"], sv["lse_b"], nb, seq, tq)
        dqd, dkvd, dkr, dwq, dwkv, dgq, dgkv = _bwd_prep(dq_b, dk_b, dv_b, sv["z"], sm["g_q"], sm["g_kv"],
                                                         w["w_uq"], w["w_ukv"], tc, ts1, ts2, tm)
        dq_a, dkv_a, dsink = _swa_bwd(sm["sink"], sv["z"], pos_col, pos_row, doa, sv["oa"], sv["lse_a"], nb, seq)
        dx, dzb, dg_mix = _bwd_in([dma, dmb, dq_a, dag, dbg, dqd, dkv_a, dkvd, dkr], sv["x"], sm["g_mix"], dx1,
                                  w["w_in"], tm)
        dwin = _wgrad_in(sv["hb"], dzb, tm)
        grads[i] = dict(g_mix=dg_mix[0], w_in=_win_from_kernel(dwin), sink=dsink[0, :SWA_H], g_q=dgq[0],
                        w_uq=_wuq_from_kernel(dwq), g_kv=dgkv[0], w_ukv=_wukv_from_kernel(dwkv), w_br_a=dwa,
                        w_br_b=dwb, w_out=dwo, g_ple=dg_ple[0], w_ple_gate=dwpg, w_ple_proj=dwpp)
    return loss, dx.reshape(nb, seq, D), grads, dg_final[0]


def _kernel_weights(full):
    wi, wq, wkv = _win_to_kernel(full["w_in"]), _wuq_to_kernel(full["w_uq"]), _wukv_to_kernel(full["w_ukv"])
    return [dict(w_in=wi[i], w_uq=wq[i], w_ukv=wkv[i], w_br_a=full["w_br_a"][i], w_br_b=full["w_br_b"][i],
                 w_out=full["w_out"][i], w_ple_gate=full["w_ple_gate"][i], w_ple_proj=full["w_ple_proj"][i])
            for i in range(DEPTH)]


def _small_params(g_mix, sink, g_q, g_kv, g_ple, g_final):
    small = {i: dict(g_mix=g_mix[i][None], sink=sink[i], g_q=g_q[i][None], g_kv=g_kv[i][None], g_ple=g_ple[i][None])
             for i in range(DEPTH)}
    small["g_final"] = g_final[None]
    return small


SHARDED = (("w_in", 2), ("w_uq", 2), ("w_ukv", 2), ("w_br_a", 2), ("w_br_b", 2), ("w_out", 1), ("w_ple_gate", 1),
           ("w_ple_proj", 2))
SMALL = ("g_mix", "sink", "g_q", "g_kv", "g_ple", "g_final")
FLAT_ROWS = 2048
ROW_TILE = 256


def _pack(shards, smalls):
    flat = jnp.concatenate([a.reshape(-1) for a in shards] + [a.reshape(-1) for a in smalls])
    return jnp.pad(flat, (0, FLAT_ROWS * FLAT_W - flat.shape[0])).reshape(FLAT_ROWS, FLAT_W)


def _unpack(flat, shard_shapes, small_shapes):
    flat = flat.reshape(-1)
    out, off = [], 0
    for shp in list(shard_shapes) + list(small_shapes):
        n = 1
        for s in shp:
            n *= s
        out.append(flat[off:off + n].reshape(shp))
        off += n
    return out[:len(shard_shapes)], out[len(shard_shapes):]


def _split_for_devices(g, axis):
    shp = g.shape
    g = g.reshape(shp[:axis] + (N_DEV, shp[axis] // N_DEV) + shp[axis + 1:])
    return jnp.moveaxis(g, axis, 0).reshape(N_DEV, -1)


def _join_from_devices(blocks, shard_shape, axis):
    full = jnp.moveaxis(blocks, 0, axis)
    return full.reshape(shard_shape[:axis] + (N_DEV * shard_shape[axis],) + shard_shape[axis + 1:])


MESH_ID = pl.DeviceIdType.MESH
ANY = pl.BlockSpec(memory_space=pl.ANY)


def _place():
    return lax.axis_index("x"), lax.axis_index("y"), lax.axis_index("c")


def _all_gather(block):
    def body(x_ref, out_ref, send_sems, recv_sems, local_sem):
        x, y, c = _place()
        me, sibling = (x, y, c), (x, y, 1 - c)
        chips = [(1 - x, y), (x, 1 - y), (1 - x, 1 - y)]

        def slot(px, py, pc):
            return out_ref.at[4 * px + 2 * py + pc]

        def copy(k, blk, to, src=None):
            return pltpu.make_async_remote_copy(
                src_ref=slot(*blk) if src is None else src, dst_ref=slot(*blk),
                send_sem=send_sems.at[k], recv_sem=recv_sems.at[k], device_id=to, device_id_type=MESH_ID)

        mine = pltpu.make_async_copy(x_ref, slot(*me), local_sem)
        mine.start()
        first = [copy(0, me, sibling, src=x_ref)]
        first += [copy(1 + j, me, (*chip, c), src=x_ref) for j, chip in enumerate(chips)]
        for cp in first:
            cp.start()
        passed = [copy(4 + j, (*chip, c), sibling) for j, chip in enumerate(chips)]
        for j, chip in enumerate(chips):
            copy(1 + j, (*chip, c), me).wait_recv()
            passed[j].start()
        copy(0, sibling, me).wait_recv()
        for j, chip in enumerate(chips):
            copy(4 + j, (*chip, 1 - c), me).wait_recv()
        for cp in first + passed:
            cp.wait_send()
        mine.wait()

    return pl.pallas_call(
        body, name="all_gather_weights",
        out_shape=SDS((N_DEV,) + block.shape, block.dtype),
        in_specs=[ANY], out_specs=ANY,
        scratch_shapes=[pltpu.SemaphoreType.DMA((7,)), pltpu.SemaphoreType.DMA((7,)), pltpu.SemaphoreType.DMA(())],
    )(block)


def _swap_sibling(a):
    def body(a_ref, out_ref, send_sem, recv_sem):
        x, y, c = _place()
        cp = pltpu.make_async_remote_copy(src_ref=a_ref, dst_ref=out_ref, send_sem=send_sem, recv_sem=recv_sem,
                                          device_id=(x, y, 1 - c), device_id_type=MESH_ID)
        cp.start()
        cp.wait()

    return pl.pallas_call(
        body, name="swap_sibling", out_shape=SDS(a.shape, a.dtype), in_specs=[ANY], out_specs=ANY,
        scratch_shapes=[pltpu.SemaphoreType.DMA(()), pltpu.SemaphoreType.DMA(())],
    )(a)


def _exchange_chips(p):
    def body(p_ref, out_ref, send_sems, recv_sems, local_sem):
        x, y, c = _place()
        mine = 2 * x + y
        local = pltpu.make_async_copy(p_ref.at[mine], out_ref.at[mine], local_sem)
        local.start()
        peers = [(1 - x, y), (x, 1 - y), (1 - x, 1 - y)]
        copies = []
        for j, (px, py) in enumerate(peers):
            cp = pltpu.make_async_remote_copy(
                src_ref=p_ref.at[2 * px + py], dst_ref=out_ref.at[mine], send_sem=send_sems.at[j],
                recv_sem=recv_sems.at[j], device_id=(px, py, c), device_id_type=MESH_ID)
            cp.start()
            copies.append(cp)
        for j, (px, py) in enumerate(peers):
            pltpu.make_async_remote_copy(
                src_ref=p_ref.at[mine], dst_ref=out_ref.at[2 * px + py], send_sem=send_sems.at[j],
                recv_sem=recv_sems.at[j], device_id=(px, py, c), device_id_type=MESH_ID).wait_recv()
        for cp in copies:
            cp.wait_send()
        local.wait()

    return pl.pallas_call(
        body, name="exchange_chips", out_shape=SDS(p.shape, p.dtype), in_specs=[ANY], out_specs=ANY,
        scratch_shapes=[pltpu.SemaphoreType.DMA((3,)), pltpu.SemaphoreType.DMA((3,)), pltpu.SemaphoreType.DMA(())],
    )(p)


def _add_pairs(a, b):
    def body(a_ref, b_ref, o_ref):
        o_ref[...] = a_ref[...] + b_ref[...]

    spec = pl.BlockSpec((1, ROW_TILE, FLAT_W), lambda k, i: (k, i, 0))
    return pl.pallas_call(
        body, grid=(a.shape[0], FLAT_ROWS // ROW_TILE), name="add_sibling", in_specs=[spec, spec], out_specs=spec,
        out_shape=SDS(a.shape, F32), compiler_params=_cp("parallel", "parallel"))(a, b)


def _sum_adamw(parts, w, m, v):
    def body(p_ref, w_ref, m_ref, v_ref, g_ref, d_ref, nm_ref, nv_ref):
        g = ((p_ref[0] + p_ref[1]) + p_ref[2]) + p_ref[3]
        nm = ADAM_B1 * m_ref[...] + (1.0 - ADAM_B1) * g
        nv = ADAM_B2 * v_ref[...] + (1.0 - ADAM_B2) * jnp.square(g)
        m_hat = nm / (1.0 - ADAM_B1 ** ADAM_STEP)
        v_hat = nv / (1.0 - ADAM_B2 ** ADAM_STEP)
        g_ref[...] = g
        nm_ref[...] = nm
        nv_ref[...] = nv
        d_ref[...] = -ADAM_LR * (m_hat / (jnp.sqrt(v_hat) + ADAM_EPS) + ADAM_WD * w_ref[...])

    spec = pl.BlockSpec((ROW_TILE, FLAT_W), lambda i: (i, 0))
    return pl.pallas_call(
        body, grid=(FLAT_ROWS // ROW_TILE,), name="sum_adamw",
        in_specs=[pl.BlockSpec((4, ROW_TILE, FLAT_W), lambda i: (0, i, 0)), spec, spec, spec],
        out_specs=[spec] * 4, out_shape=[SDS((FLAT_ROWS, FLAT_W), F32)] * 4,
        compiler_params=_cp("parallel"))(parts, w, m, v)


def kernel(x, p, positions, g_mix, w_in, sink, g_q, w_uq, g_kv, w_ukv, w_br_a, w_br_b, w_out, g_ple, w_ple_gate, w_ple_proj, g_final, loss_target, m_g_mix, m_w_in, m_sink, m_g_q, m_w_uq, m_g_kv, m_w_ukv, m_w_br_a, m_w_br_b, m_w_out, m_g_ple, m_w_ple_gate, m_w_ple_proj, m_g_final, v_g_mix, v_w_in, v_sink, v_g_q, v_w_uq, v_g_kv, v_w_ukv, v_w_br_a, v_w_br_b, v_w_out, v_g_ple, v_w_ple_gate, v_w_ple_proj, v_g_final):
    weights = dict(g_mix=g_mix, w_in=w_in, sink=sink, g_q=g_q, w_uq=w_uq, g_kv=g_kv, w_ukv=w_ukv, w_br_a=w_br_a,
                   w_br_b=w_br_b, w_out=w_out, g_ple=g_ple, w_ple_gate=w_ple_gate, w_ple_proj=w_ple_proj,
                   g_final=g_final)
    mom1 = dict(g_mix=m_g_mix, w_in=m_w_in, sink=m_sink, g_q=m_g_q, w_uq=m_w_uq, g_kv=m_g_kv, w_ukv=m_w_ukv,
                w_br_a=m_w_br_a, w_br_b=m_w_br_b, w_out=m_w_out, g_ple=m_g_ple, w_ple_gate=m_w_ple_gate,
                w_ple_proj=m_w_ple_proj, g_final=m_g_final)
    mom2 = dict(g_mix=v_g_mix, w_in=v_w_in, sink=v_sink, g_q=v_g_q, w_uq=v_w_uq, g_kv=v_g_kv, w_ukv=v_w_ukv,
                w_br_a=v_w_br_a, w_br_b=v_w_br_b, w_out=v_w_out, g_ple=v_g_ple, w_ple_gate=v_w_ple_gate,
                w_ple_proj=v_w_ple_proj, g_final=v_g_final)
    shard_names = [n for n, _ in SHARDED]
    shard_shapes = [weights[n].shape for n in shard_names]
    small_shapes = [weights[n].shape for n in SMALL]
    pack = lambda d: _pack([d[n] for n in shard_names], [d[n] for n in SMALL])
    w_flat, m_flat, v_flat = pack(weights), pack(mom1), pack(mom2)

    gathered = _all_gather(w_flat.astype(BF16)).reshape(N_DEV, -1)
    full, off = {}, 0
    for (name, axis), shp in zip(SHARDED, shard_shapes):
        n = 1
        for s in shp:
            n *= s
        full[name] = _join_from_devices(gathered[:, off:off + n].reshape((N_DEV,) + shp), shp, axis)
        off += n
    wts = _kernel_weights(full)
    small = _small_params(g_mix, sink, g_q, g_kv, g_ple, g_final)

    loss, grad_x, grads, dg_final = _local_step(x, p, positions, loss_target, small, wts)
    loss = lax.psum(loss[0, 0], ("x", "y", "c"))

    stacked = {n: jnp.stack([grads[i][n] for i in range(DEPTH)]) for n in grads[0]}
    stacked["g_final"] = dg_final
    rows = jnp.concatenate([_split_for_devices(stacked[n], axis) for n, axis in SHARDED], axis=1)
    vec = jnp.concatenate([stacked[n].reshape(-1) for n in SMALL])
    rows = jnp.concatenate([rows, jnp.broadcast_to(vec[None], (N_DEV, vec.shape[0]))], axis=1)
    rows = jnp.pad(rows, ((0, 0), (0, FLAT_ROWS * FLAT_W - rows.shape[1])))
    rows = rows.reshape(N_DEV // 2, 2, FLAT_ROWS, FLAT_W)
    c = lax.axis_index("c")
    for_me = lax.dynamic_index_in_dim(rows, c, axis=1, keepdims=False)
    for_sibling = lax.dynamic_index_in_dim(rows, 1 - c, axis=1, keepdims=False)
    chip_partial = _add_pairs(for_me, _swap_sibling(for_sibling))
    g_flat, d_flat, nm_flat, nv_flat = _sum_adamw(_exchange_chips(chip_partial), w_flat, m_flat, v_flat)

    outs = []
    for flat in (g_flat, d_flat, nm_flat, nv_flat):
        sh, sm = _unpack(flat, shard_shapes, small_shapes)
        named = dict(zip(shard_names, sh))
        named.update(zip(SMALL, sm))
        outs += [named[n] for n in weights]
    return (loss, grad_x, *outs)
```

```python
import functools

import jax
import jax.numpy as jnp
from jax import lax
from jax.experimental import pallas as pl
from jax.experimental.pallas import tpu as pltpu

F32, BF16 = jnp.float32, jnp.bfloat16
SDS = jax.ShapeDtypeStruct

D = 1024
DEPTH = 2
PLE = 256
BLK = 128
EPS = 1e-6
NEG = -1e30
SWA_H, SWA_KV, SWA_DH = 8, 2, 64
MLA_H, MLA_NOPE, MLA_ROPE, MLA_V = 8, 64, 32, 64
MLA_QK = MLA_NOPE + MLA_ROPE
QL, KVL = 256, 128
IN_W = 4256
N_DEV = 8

V7X_VMEM_BYTES = 64 * 1024 * 1024
LANES = 128
VMEM_LIMIT = V7X_VMEM_BYTES * 7 // 8

ZW = 4352
Z_MA, Z_MB, Z_AQ, Z_AG, Z_BG, Z_QD, Z_AK, Z_AV, Z_KVD, Z_KR = 0, 1024, 2048, 2560, 3072, 3584, 3840, 3968, 4096, 4224
QFW = MLA_H * LANES
KVW = QFW + MLA_H * MLA_V
MLA_SCALE = MLA_QK ** -0.5
SWA_SCALE = SWA_DH ** -0.5
ROLL_UP, ROLL_DOWN = MLA_ROPE // 2, LANES - MLA_ROPE // 2

ADAM_LR, ADAM_B1, ADAM_B2, ADAM_EPS, ADAM_WD, ADAM_STEP = 0.001, 0.9, 0.999, 1e-08, 0.01, 10

FLAT_W = 1024


def _cp(*sem):
    return pltpu.CompilerParams(dimension_semantics=sem, vmem_limit_bytes=VMEM_LIMIT)


def _row(tm, w, col=0):
    return pl.BlockSpec((tm, w), lambda i: (i, col))


def _res(shape):
    return pl.BlockSpec(shape, lambda *_: (0,) * len(shape), pipeline_mode=pl.Buffered(1))


def _acc(shape):
    return pl.BlockSpec(shape, lambda *_: (0,) * len(shape))


def _rstd(xf):
    return lax.rsqrt(jnp.mean(xf * xf, axis=-1, keepdims=True) + EPS)


def _norm_bwd(dh, n, r, g):
    dn = dh * g
    return r * (dn - n * jnp.mean(dn * n, axis=-1, keepdims=True)), dh * n


def _nt(a, b):
    return lax.dot_general(a, b, (((1,), (1,)), ((), ())), preferred_element_type=F32)


def _tn(a, b):
    return lax.dot_general(a, b, (((0,), (0,)), ((), ())), preferred_element_type=F32)


def _nn(a, b):
    return jnp.dot(a, b, preferred_element_type=F32)


def _sig(x):
    return jax.nn.sigmoid(x)


def _rope(t, c, s1, s2):
    return t * c + pltpu.roll(t, ROLL_UP, 1) * s1 + pltpu.roll(t, ROLL_DOWN, 1) * s2


def _rope_t(d, c, s1, s2):
    return d * c + pltpu.roll(d * s1, ROLL_DOWN, 1) + pltpu.roll(d * s2, ROLL_UP, 1)


def _fwd_in(x, g, w, tm):
    T = x.shape[0]

    def body(x_ref, g_ref, w_ref, z_ref, h_ref):
        xf = x_ref[...]
        h = ((xf * _rstd(xf)) * g_ref[...]).astype(BF16)
        h_ref[...] = h
        z_ref[...] = _nn(h, w_ref[...])

    return pl.pallas_call(
        body, grid=(T // tm,), name="fwd_in",
        in_specs=[_row(tm, D), _res((1, D)), _res((D, ZW))],
        out_specs=[_row(tm, ZW), _row(tm, D)],
        out_shape=[SDS((T, ZW), F32), SDS((T, D), BF16)],
        compiler_params=_cp("parallel"))(x, g, w)


def _fwd_prep(z, gq, gkv, wq, wkv, tc, ts1, ts2, tm):
    T = z.shape[0]

    def body(qd_ref, kvd_ref, kr_ref, gq_ref, gkv_ref, wq_ref, wkv_ref, c_ref, s1_ref, s2_ref, q_ref, k_ref, v_ref):
        qd, kvd = qd_ref[...], kvd_ref[...]
        hq = ((qd * _rstd(qd)) * gq_ref[...]).astype(BF16)
        hkv = ((kvd * _rstd(kvd)) * gkv_ref[...]).astype(BF16)
        qf = _nn(hq, wq_ref[...])
        kvf = _nn(hkv, wkv_ref[...])
        c, s1, s2 = c_ref[...], s1_ref[...], s2_ref[...]
        krb = _rope(kr_ref[...], c, s1, s2)
        for h in range(MLA_H):
            sl = slice(LANES * h, LANES * (h + 1))
            q_ref[:, sl] = _rope(qf[:, sl], c, s1, s2).astype(BF16)
            k_ref[:, sl] = (kvf[:, sl] + krb).astype(BF16)
        v_ref[...] = kvf[:, QFW:].astype(BF16)

    return pl.pallas_call(
        body, grid=(T // tm,), name="fwd_prep",
        in_specs=[_row(tm, QL, Z_QD // QL), _row(tm, KVL, Z_KVD // KVL), _row(tm, LANES, Z_KR // LANES),
                  _res((1, QL)), _res((1, KVL)), _res((QL, QFW)), _res((KVL, KVW)),
                  _row(tm, LANES), _row(tm, LANES), _row(tm, LANES)],
        out_specs=[_row(tm, QFW), _row(tm, QFW), _row(tm, MLA_H * MLA_V)],
        out_shape=[SDS((T, QFW), BF16), SDS((T, QFW), BF16), SDS((T, MLA_H * MLA_V), BF16)],
        compiler_params=_cp("parallel"))(z, z, z, gq, gkv, wq, wkv, tc, ts1, ts2)


def _mla_fwd(qf, kf, v, nb, seq, tq):
    T = qf.shape[0]
    nq = seq // tq
    pw = 2 * LANES

    def body(q_ref, k_ref, v_ref, o_ref, lse_ref, m_s, l_s, acc_s):
        qi, ki = pl.program_id(2), pl.program_id(3)

        @pl.when(ki == 0)
        def _():
            m_s[...] = jnp.full(m_s.shape, NEG, F32)
            l_s[...] = jnp.zeros(l_s.shape, F32)
            acc_s[...] = jnp.zeros(acc_s.shape, F32)

        @pl.when(ki <= qi)
        def _():
            rows = qi * tq + lax.broadcasted_iota(jnp.int32, (tq, tq), 0)
            cols = ki * tq + lax.broadcasted_iota(jnp.int32, (tq, tq), 1)
            mask = cols <= rows
            for j in range(2):
                s = _nt(q_ref[:, LANES * j:LANES * (j + 1)], k_ref[:, LANES * j:LANES * (j + 1)]) * MLA_SCALE
                s = jnp.where(mask, s, NEG)
                m_prev = m_s[j]
                m_new = jnp.maximum(m_prev, jnp.max(s, axis=-1, keepdims=True))
                alpha = jnp.exp(m_prev - m_new)
                p = jnp.exp(s - m_new)
                l_s[j] = alpha * l_s[j] + jnp.sum(p, axis=-1, keepdims=True)
                m_s[j] = m_new
                sl = slice(MLA_V * j, MLA_V * (j + 1))
                acc_s[:, sl] = alpha * acc_s[:, sl] + _nn(p.astype(BF16), v_ref[:, sl])

        @pl.when(ki == qi)
        def _():
            for j in range(2):
                sl = slice(MLA_V * j, MLA_V * (j + 1))
                o_ref[:, sl] = acc_s[:, sl] / l_s[j]
                lse_ref[j] = m_s[j] + jnp.log(l_s[j])

    kv_map = lambda b, hp, qi, ki: (b * nq + jnp.minimum(ki, qi), hp)
    return pl.pallas_call(
        body, grid=(nb, MLA_H // 2, nq, nq), name="mla_fwd",
        in_specs=[pl.BlockSpec((tq, pw), lambda b, hp, qi, ki: (b * nq + qi, hp)),
                  pl.BlockSpec((tq, pw), kv_map),
                  pl.BlockSpec((tq, LANES), kv_map)],
        out_specs=[pl.BlockSpec((tq, LANES), lambda b, hp, qi, ki: (b * nq + qi, hp)),
                   pl.BlockSpec((2, tq, 1), lambda b, hp, qi, ki: (hp, b * nq + qi, 0))],
        out_shape=[SDS((T, MLA_H * MLA_V), F32), SDS((MLA_H, T, 1), F32)],
        scratch_shapes=[pltpu.VMEM((2, tq, 1), F32), pltpu.VMEM((2, tq, 1), F32), pltpu.VMEM((tq, LANES), F32)],
        compiler_params=_cp("parallel", "parallel", "parallel", "arbitrary"))(qf, kf, v)


def _swa_valid(n):
    r = lax.broadcasted_iota(jnp.int32, (BLK, 2 * BLK), 0)
    c = lax.broadcasted_iota(jnp.int32, (BLK, 2 * BLK), 1)
    return (c > r) & (c <= r + BLK) & ((c >= BLK) | (n > 0))


def _swa_specs(nblk):
    cur = lambda b, n: (b * nblk + n, 0)
    prev = lambda b, n: (b * nblk + jnp.maximum(n - 1, 0), 0)
    kvc = Z_AK // (2 * BLK)
    return [pl.BlockSpec(memory_space=pltpu.SMEM),
            pl.BlockSpec((BLK, 512), lambda b, n: (b * nblk + n, Z_AQ // 512)),
            pl.BlockSpec((BLK, 2 * BLK), lambda b, n: (b * nblk + n, kvc)),
            pl.BlockSpec((BLK, 2 * BLK), lambda b, n: (b * nblk + jnp.maximum(n - 1, 0), kvc)),
            pl.BlockSpec((BLK, 1), cur),
            pl.BlockSpec((1, 1, BLK), lambda b, n: (b * nblk + n, 0, 0)),
            pl.BlockSpec((1, 1, BLK), lambda b, n: (b * nblk + jnp.maximum(n - 1, 0), 0, 0))]


def _swa_scores(n, q_ref, kvc_ref, kvp_ref, pc_ref, prc_ref, prp_ref):
    kv = jnp.concatenate([kvp_ref[...], kvc_ref[...]], axis=0)
    kb, vb = kv[:, :BLK].astype(BF16), kv[:, BLK:].astype(BF16)
    dist = pc_ref[...] - jnp.concatenate([prp_ref[0], prc_ref[0]], axis=1)
    valid = _swa_valid(n)

    def scores(h):
        g = h // (SWA_H // SWA_KV)
        qh = q_ref[:, SWA_DH * h:SWA_DH * (h + 1)].astype(BF16)
        s = _nt(qh, kb[:, SWA_DH * g:SWA_DH * (g + 1)]) * SWA_SCALE - (2.0 ** -(h + 1)) * dist
        return qh, jnp.where(valid, s, NEG)

    return kb, vb, scores


def _swa_fwd(sink, z, pos_col, pos_row, nb, seq):
    T = z.shape[0]
    nblk = seq // BLK

    def body(sink_ref, q_ref, kvc_ref, kvp_ref, pc_ref, prc_ref, prp_ref, o_ref, lse_ref):
        kb, vb, scores = _swa_scores(pl.program_id(1), q_ref, kvc_ref, kvp_ref, pc_ref, prc_ref, prp_ref)
        for h in range(SWA_H):
            g = h // (SWA_H // SWA_KV)
            _, s = scores(h)
            sk = sink_ref[h]
            m = jnp.maximum(jnp.max(s, axis=-1, keepdims=True), sk)
            e = jnp.exp(s - m)
            den = jnp.sum(e, axis=-1, keepdims=True) + jnp.exp(sk - m)
            o_ref[:, SWA_DH * h:SWA_DH * (h + 1)] = _nn(e.astype(BF16), vb[:, SWA_DH * g:SWA_DH * (g + 1)]) / den
            lse_ref[:, h:h + 1] = m + jnp.log(den)

    return pl.pallas_call(
        body, grid=(nb, nblk), name="swa_fwd",
        in_specs=_swa_specs(nblk),
        out_specs=[pl.BlockSpec((BLK, 512), lambda b, n: (b * nblk + n, 0)),
                   pl.BlockSpec((BLK, SWA_H), lambda b, n: (b * nblk + n, 0))],
        out_shape=[SDS((T, 512), F32), SDS((T, SWA_H), F32)],
        compiler_params=_cp("parallel", "parallel"))(sink, z, z, z, pos_col, pos_row, pos_row)


def _fwd_merge(x, oa, ob, z, wa, wb, wo, tm):
    T = x.shape[0]

    def body(x_ref, oa_ref, ob_ref, ag_ref, bg_ref, ma_ref, mb_ref, wa_ref, wb_ref, wo_ref, x1_ref, ua_ref, ub_ref):
        ag, bg = ag_ref[...], bg_ref[...]
        ua = _nn((oa_ref[...] * (ag * _sig(ag))).astype(BF16), wa_ref[...])
        ub = _nn((ob_ref[...] * (bg * _sig(bg))).astype(BF16), wb_ref[...])
        ua_ref[...] = ua
        ub_ref[...] = ub
        y = _sig(ma_ref[...]) * ua + _sig(mb_ref[...]) * ub
        x1_ref[...] = x_ref[...] + _nn(y.astype(BF16), wo_ref[...])

    return pl.pallas_call(
        body, grid=(T // tm,), name="fwd_merge",
        in_specs=[_row(tm, D), _row(tm, 512), _row(tm, 512), _row(tm, 512, Z_AG // 512), _row(tm, 512, Z_BG // 512),
                  _row(tm, D, Z_MA // D), _row(tm, D, Z_MB // D), _res((512, D)), _res((512, D)), _res((D, D))],
        out_specs=[_row(tm, D)] * 3,
        out_shape=[SDS((T, D), F32)] * 3,
        compiler_params=_cp("parallel"))(x, oa, ob, z, z, z, z, wa, wb, wo)


def _fwd_ple(x1, p, g, wpg, wpp, tm):
    T = x1.shape[0]

    def body(x_ref, p_ref, g_ref, wpg_ref, wpp_ref, x2_ref, pg_ref, pp_ref):
        xf = x_ref[...]
        h1 = ((xf * _rstd(xf)) * g_ref[...]).astype(BF16)
        pg = _sig(_nn(h1, wpg_ref[...]))
        pp = _nn(p_ref[...].astype(BF16), wpp_ref[...])
        pg_ref[...] = pg
        pp_ref[...] = pp
        x2_ref[...] = xf + pg * pp

    return pl.pallas_call(
        body, grid=(T // tm,), name="fwd_ple",
        in_specs=[_row(tm, D), _row(tm, PLE), _res((1, D)), _res((D, D)), _res((PLE, D))],
        out_specs=[_row(tm, D)] * 3,
        out_shape=[SDS((T, D), F32)] * 3,
        compiler_params=_cp("parallel"))(x1, p, g, wpg, wpp)


def _loss_head(x, g, tgt, tm):
    T = x.shape[0]

    def body(x_ref, g_ref, t_ref, dx_ref, dg_ref, loss_ref):
        @pl.when(pl.program_id(0) == 0)
        def _():
            dg_ref[...] = jnp.zeros(dg_ref.shape, F32)
            loss_ref[...] = jnp.zeros(loss_ref.shape, F32)

        xf, gf = x_ref[...], g_ref[...]
        r = _rstd(xf)
        n = xf * r
        err = n * gf - t_ref[...]
        loss_ref[...] += 0.5 * jnp.sum(jnp.mean(err * err, axis=-1, keepdims=True), axis=0, keepdims=True)
        dx, dgr = _norm_bwd(err * (1.0 / D), n, r, gf)
        dx_ref[...] = dx
        dg_ref[...] += jnp.sum(dgr, axis=0, keepdims=True)

    return pl.pallas_call(
        body, grid=(T // tm,), name="loss_head",
        in_specs=[_row(tm, D), _res((1, D)), _row(tm, D)],
        out_specs=[_row(tm, D), _acc((1, D)), _acc((1, LANES))],
        out_shape=[SDS((T, D), F32), SDS((1, D), F32), SDS((1, LANES), F32)],
        compiler_params=_cp("arbitrary"))(x, g, tgt)


def _bwd_ple(dx2, x1, pg, pp, p, g, wpg, tm):
    T = x1.shape[0]

    def body(d_ref, x_ref, pg_ref, pp_ref, p_ref, g_ref, w_ref, dx_ref, dwg_ref, dwp_ref, dg_ref):
        @pl.when(pl.program_id(0) == 0)
        def _():
            dwg_ref[...] = jnp.zeros(dwg_ref.shape, F32)
            dwp_ref[...] = jnp.zeros(dwp_ref.shape, F32)
            dg_ref[...] = jnp.zeros(dg_ref.shape, F32)

        d, xf, pg, gf = d_ref[...], x_ref[...], pg_ref[...], g_ref[...]
        r = _rstd(xf)
        n = xf * r
        dpgl = (d * pp_ref[...] * pg * (1.0 - pg)).astype(BF16)
        dwg_ref[...] += _tn((n * gf).astype(BF16), dpgl)
        dwp_ref[...] += _tn(p_ref[...].astype(BF16), (d * pg).astype(BF16))
        dxn, dgr = _norm_bwd(_nt(dpgl, w_ref[...]), n, r, gf)
        dx_ref[...] = d + dxn
        dg_ref[...] += jnp.sum(dgr, axis=0, keepdims=True)

    return pl.pallas_call(
        body, grid=(T // tm,), name="bwd_ple",
        in_specs=[_row(tm, D)] * 4 + [_row(tm, PLE), _res((1, D)), _res((D, D))],
        out_specs=[_row(tm, D), _acc((D, D)), _acc((PLE, D)), _acc((1, D))],
        out_shape=[SDS((T, D), F32), SDS((D, D), F32), SDS((PLE, D), F32), SDS((1, D), F32)],
        compiler_params=_cp("arbitrary"))(dx2, x1, pg, pp, p, g, wpg)


def _bwd_merge(dx1, oa, ob, z, ua, ub, wa, wb, wo, tm):
    T = dx1.shape[0]

    def body(d_ref, oa_ref, ob_ref, ag_ref, bg_ref, ma_ref, mb_ref, ua_ref, ub_ref, wa_ref, wb_ref, wo_ref,
             doa_ref, dob_ref, dag_ref, dbg_ref, dma_ref, dmb_ref, dwa_ref, dwb_ref, dwo_ref):
        @pl.when(pl.program_id(0) == 0)
        def _():
            dwa_ref[...] = jnp.zeros(dwa_ref.shape, F32)
            dwb_ref[...] = jnp.zeros(dwb_ref.shape, F32)
            dwo_ref[...] = jnp.zeros(dwo_ref.shape, F32)

        db = d_ref[...].astype(BF16)
        ua, ub = ua_ref[...], ub_ref[...]
        sa, sb = _sig(ma_ref[...]), _sig(mb_ref[...])
        dwo_ref[...] += _tn((sa * ua + sb * ub).astype(BF16), db)
        dy = _nt(db, wo_ref[...])
        dma_ref[...] = (dy * ua * sa * (1.0 - sa)).astype(BF16)
        dmb_ref[...] = (dy * ub * sb * (1.0 - sb)).astype(BF16)
        for (o_ref, gate_ref, s, w_ref, do_ref, dgate_ref, dw_ref) in (
                (oa_ref, ag_ref, sa, wa_ref, doa_ref, dag_ref, dwa_ref),
                (ob_ref, bg_ref, sb, wb_ref, dob_ref, dbg_ref, dwb_ref)):
            du = (dy * s).astype(BF16)
            raw, gate = o_ref[...], gate_ref[...]
            sg = _sig(gate)
            silu = gate * sg
            dw_ref[...] += _tn((raw * silu).astype(BF16), du)
            do = _nt(du, w_ref[...])
            do_ref[...] = do * silu
            dgate_ref[...] = (do * raw * (sg * (1.0 + gate * (1.0 - sg)))).astype(BF16)

    return pl.pallas_call(
        body, grid=(T // tm,), name="bwd_merge",
        in_specs=[_row(tm, D), _row(tm, 512), _row(tm, 512), _row(tm, 512, Z_AG // 512), _row(tm, 512, Z_BG // 512),
                  _row(tm, D, Z_MA // D), _row(tm, D, Z_MB // D), _row(tm, D), _row(tm, D),
                  _res((512, D)), _res((512, D)), _res((D, D))],
        out_specs=[_row(tm, 512)] * 4 + [_row(tm, D)] * 2 + [_acc((512, D)), _acc((512, D)), _acc((D, D))],
        out_shape=[SDS((T, 512), F32), SDS((T, 512), F32), SDS((T, 512), BF16), SDS((T, 512), BF16),
                   SDS((T, D), BF16), SDS((T, D), BF16), SDS((512, D), F32), SDS((512, D), F32), SDS((D, D), F32)],
        compiler_params=_cp("arbitrary"))(dx1, oa, ob, z, z, z, z, ua, ub, wa, wb, wo)


def _mla_bwd(qf, kf, v, do, o, lse, nb, seq, tq):
    T = qf.shape[0]
    nq = seq // tq
    pw = 2 * LANES

    def body(q_ref, k_ref, v_ref, do_ref, o_ref, lse_ref, dq_ref, dk_ref, dv_ref, dk_s, dv_s):
        ki, qi = pl.program_id(2), pl.program_id(3)

        @pl.when((ki == 0) & (qi == 0))
        def _():
            dq_ref[...] = jnp.zeros(dq_ref.shape, F32)

        @pl.when(qi == ki)
        def _():
            dk_s[...] = jnp.zeros(dk_s.shape, F32)
            dv_s[...] = jnp.zeros(dv_s.shape, F32)

        @pl.when(qi >= ki)
        def _():
            rows = qi * tq + lax.broadcasted_iota(jnp.int32, (tq, tq), 0)
            cols = ki * tq + lax.broadcasted_iota(jnp.int32, (tq, tq), 1)
            mask = cols <= rows
            qrows = pl.ds(pl.multiple_of(qi * tq, tq), tq)
            for j in range(2):
                wide = slice(LANES * j, LANES * (j + 1))
                sl = slice(MLA_V * j, MLA_V * (j + 1))
                q, k, vj = q_ref[:, wide], k_ref[:, wide], v_ref[:, sl]
                doj = do_ref[:, sl]
                dsum = jnp.sum(doj * o_ref[:, sl], axis=-1, keepdims=True)
                s = jnp.where(mask, _nt(q, k) * MLA_SCALE, NEG)
                p = jnp.exp(s - lse_ref[j])
                dob = doj.astype(BF16)
                dv_s[:, sl] += _tn(p.astype(BF16), dob)
                ds = (p * (_nt(dob, vj) - dsum) * MLA_SCALE).astype(BF16)
                dk_s[:, wide] += _tn(ds, q)
                dq_ref[qrows, wide] += _nn(ds, k)

        @pl.when(qi == nq - 1)
        def _():
            dk_ref[...] = dk_s[...]
            dv_ref[...] = dv_s[...]

    qmap = lambda b, hp, ki, qi: (b * nq + jnp.maximum(qi, ki), hp)
    kmap = lambda b, hp, ki, qi: (b * nq + ki, hp)
    return pl.pallas_call(
        body, grid=(nb, MLA_H // 2, nq, nq), name="mla_bwd",
        in_specs=[pl.BlockSpec((tq, pw), qmap), pl.BlockSpec((tq, pw), kmap), pl.BlockSpec((tq, LANES), kmap),
                  pl.BlockSpec((tq, LANES), qmap), pl.BlockSpec((tq, LANES), qmap),
                  pl.BlockSpec((2, tq, 1), lambda b, hp, ki, qi: (hp, b * nq + jnp.maximum(qi, ki), 0))],
        out_specs=[pl.BlockSpec((seq, pw), lambda b, hp, ki, qi: (b, hp)),
                   pl.BlockSpec((tq, pw), kmap), pl.BlockSpec((tq, LANES), kmap)],
        out_shape=[SDS((T, QFW), F32), SDS((T, QFW), F32), SDS((T, MLA_H * MLA_V), F32)],
        scratch_shapes=[pltpu.VMEM((tq, pw), F32), pltpu.VMEM((tq, LANES), F32)],
        compiler_params=_cp("parallel", "parallel", "arbitrary", "arbitrary"))(qf, kf, v, do, o, lse)


def _swa_bwd(sink, z, pos_col, pos_row, do, o, lse, nb, seq):
    T = z.shape[0]
    nblk = seq // BLK

    def body(sink_ref, q_ref, kvc_ref, kvp_ref, pc_ref, prc_ref, prp_ref, do_ref, o_ref, lse_ref,
             dq_ref, dkv_ref, dsink_ref):
        b, n = pl.program_id(0), pl.program_id(1)

        @pl.when((b == 0) & (n == 0))
        def _():
            dsink_ref[...] = jnp.zeros(dsink_ref.shape, F32)

        @pl.when(n == 0)
        def _():
            dkv_ref[...] = jnp.zeros(dkv_ref.shape, F32)

        kb, vb, scores = _swa_scores(n, q_ref, kvc_ref, kvp_ref, pc_ref, prc_ref, prp_ref)
        lane = lax.broadcasted_iota(jnp.int32, (1, LANES), 1)
        dsink = jnp.zeros((1, LANES), F32)
        dkv = [[None, None], [None, None]]
        for h in range(SWA_H):
            g = h // (SWA_H // SWA_KV)
            hs = slice(SWA_DH * h, SWA_DH * (h + 1))
            gs = slice(SWA_DH * g, SWA_DH * (g + 1))
            qh, s = scores(h)
            lse = lse_ref[:, h:h + 1]
            p = jnp.exp(s - lse)
            doh = do_ref[:, hs]
            dsum = jnp.sum(doh * o_ref[:, hs], axis=-1, keepdims=True)
            dob = doh.astype(BF16)
            ds = (p * (_nt(dob, vb[:, gs]) - dsum) * SWA_SCALE).astype(BF16)
            dq_ref[:, hs] = _nn(ds, kb[:, gs]).astype(BF16)
            dk, dv = _tn(ds, qh), _tn(p.astype(BF16), dob)
            dkv[g][0] = dk if dkv[g][0] is None else dkv[g][0] + dk
            dkv[g][1] = dv if dkv[g][1] is None else dkv[g][1] + dv
            dsk = jnp.sum(-jnp.exp(sink_ref[h] - lse) * dsum, axis=0, keepdims=True)
            dsink = dsink + jnp.where(lane == h, dsk, 0.0)
        dsink_ref[...] += dsink
        upd = jnp.concatenate([dkv[0][0], dkv[1][0], dkv[0][1], dkv[1][1]], axis=1)
        dkv_ref[pl.ds(pl.multiple_of(n * BLK, BLK), BLK), :] += upd[BLK:]

        @pl.when(n > 0)
        def _():
            dkv_ref[pl.ds(pl.multiple_of((n - 1) * BLK, BLK), BLK), :] += upd[:BLK]

    return pl.pallas_call(
        body, grid=(nb, nblk), name="swa_bwd",
        in_specs=_swa_specs(nblk) + [pl.BlockSpec((BLK, 512), lambda b, n: (b * nblk + n, 0))] * 2
        + [pl.BlockSpec((BLK, SWA_H), lambda b, n: (b * nblk + n, 0))],
        out_specs=[pl.BlockSpec((BLK, 512), lambda b, n: (b * nblk + n, 0)),
                   pl.BlockSpec((seq, 2 * BLK), lambda b, n: (b, 0)),
                   pl.BlockSpec((1, LANES), lambda b, n: (0, 0))],
        out_shape=[SDS((T, 512), BF16), SDS((T, 2 * BLK), F32), SDS((1, LANES), F32)],
        compiler_params=_cp("arbitrary", "arbitrary"))(sink, z, z, z, pos_col, pos_row, pos_row, do, o, lse)


def _bwd_prep(dq, dk, dv, z, gq, gkv, wq, wkv, tc, ts1, ts2, tm):
    T = z.shape[0]

    def body(dq_ref, dk_ref, dv_ref, qd_ref, kvd_ref, gq_ref, gkv_ref, wq_ref, wkv_ref, c_ref, s1_ref, s2_ref,
             dqd_ref, dkvd_ref, dkr_ref, dwq_ref, dwkv_ref, dgq_ref, dgkv_ref, dqb_s, dkvb_s):
        @pl.when(pl.program_id(0) == 0)
        def _():
            for ref in (dwq_ref, dwkv_ref, dgq_ref, dgkv_ref):
                ref[...] = jnp.zeros(ref.shape, F32)

        c, s1, s2 = c_ref[...], s1_ref[...], s2_ref[...]
        lane = lax.broadcasted_iota(jnp.int32, (1, LANES), 1)
        rope_lanes = (lane >= MLA_NOPE) & (lane < MLA_QK)
        dkb = jnp.zeros((tm, LANES), F32)
        for h in range(MLA_H):
            sl = slice(LANES * h, LANES * (h + 1))
            dqb_s[:, sl] = _rope_t(dq_ref[:, sl], c, s1, s2).astype(BF16)
            dkh = dk_ref[:, sl]
            dkb = dkb + dkh
            dkvb_s[:, sl] = dkh.astype(BF16)
        dkvb_s[:, QFW:] = dv_ref[...].astype(BF16)
        dkr_ref[...] = _rope_t(jnp.where(rope_lanes, dkb, 0.0), c, s1, s2).astype(BF16)

        for (x_ref, g_ref, w_ref, d_s, dx_ref, dw_ref, dg_ref) in (
                (qd_ref, gq_ref, wq_ref, dqb_s, dqd_ref, dwq_ref, dgq_ref),
                (kvd_ref, gkv_ref, wkv_ref, dkvb_s, dkvd_ref, dwkv_ref, dgkv_ref)):
            xf, gf, db = x_ref[...], g_ref[...], d_s[...]
            r = _rstd(xf)
            n = xf * r
            dw_ref[...] += _tn((n * gf).astype(BF16), db)
            dx, dgr = _norm_bwd(_nt(db, w_ref[...]), n, r, gf)
            dx_ref[...] = dx.astype(BF16)
            dg_ref[...] += jnp.sum(dgr, axis=0, keepdims=True)

    return pl.pallas_call(
        body, grid=(T // tm,), name="bwd_prep",
        in_specs=[_row(tm, QFW), _row(tm, QFW), _row(tm, MLA_H * MLA_V),
                  _row(tm, QL, Z_QD // QL), _row(tm, KVL, Z_KVD // KVL),
                  _res((1, QL)), _res((1, KVL)), _res((QL, QFW)), _res((KVL, KVW)),
                  _row(tm, LANES), _row(tm, LANES), _row(tm, LANES)],
        out_specs=[_row(tm, QL), _row(tm, KVL), _row(tm, LANES),
                   _acc((QL, QFW)), _acc((KVL, KVW)), _acc((1, QL)), _acc((1, KVL))],
        out_shape=[SDS((T, QL), BF16), SDS((T, KVL), BF16), SDS((T, LANES), BF16),
                   SDS((QL, QFW), F32), SDS((KVL, KVW), F32), SDS((1, QL), F32), SDS((1, KVL), F32)],
        scratch_shapes=[pltpu.VMEM((tm, QFW), BF16), pltpu.VMEM((tm, KVW), BF16)],
        compiler_params=_cp("arbitrary"))(dq, dk, dv, z, z, gq, gkv, wq, wkv, tc, ts1, ts2)


def _bwd_in(pieces, x, g, dres, w, tm):
    T = x.shape[0]
    widths = [pc.shape[1] for pc in pieces]
    assert sum(widths) == ZW
    n_p = len(pieces)

    def body(*refs):
        p_refs, (x_ref, g_ref, r_ref, w_ref, dx_ref, dz_ref, dg_ref) = refs[:n_p], refs[n_p:]

        @pl.when(pl.program_id(0) == 0)
        def _():
            dg_ref[...] = jnp.zeros(dg_ref.shape, F32)

        off = 0
        for ref, wd in zip(p_refs, widths):
            dz_ref[:, off:off + wd] = ref[...].astype(BF16)
            off += wd
        xf, gf = x_ref[...], g_ref[...]
        r = _rstd(xf)
        n = xf * r
        dx, dgr = _norm_bwd(_nt(dz_ref[...], w_ref[...]), n, r, gf)
        dx_ref[...] = r_ref[...] + dx
        dg_ref[...] += jnp.sum(dgr, axis=0, keepdims=True)

    return pl.pallas_call(
        body, grid=(T // tm,), name="bwd_in",
        in_specs=[_row(tm, wd) for wd in widths] + [_row(tm, D), _res((1, D)), _row(tm, D), _res((D, ZW))],
        out_specs=[_row(tm, D), _row(tm, ZW), _acc((1, D))],
        out_shape=[SDS((T, D), F32), SDS((T, ZW), BF16), SDS((1, D), F32)],
        compiler_params=_cp("arbitrary"))(*pieces, x, g, dres, w)


def _wgrad_in(hb, dzb, tm):
    T = hb.shape[0]
    half = ZW // 2

    def body(h_ref, dz_ref, dw_ref):
        @pl.when(pl.program_id(1) == 0)
        def _():
            dw_ref[...] = jnp.zeros(dw_ref.shape, F32)

        dw_ref[...] += _tn(h_ref[...], dz_ref[...])

    return pl.pallas_call(
        body, grid=(2, T // tm), name="wgrad_in",
        in_specs=[pl.BlockSpec((tm, D), lambda j, t: (t, 0)), pl.BlockSpec((tm, half), lambda j, t: (t, j))],
        out_specs=pl.BlockSpec((D, half), lambda j, t: (0, j)),
        out_shape=SDS((D, ZW), F32),
        compiler_params=_cp("parallel", "arbitrary"))(hb, dzb)


def _win_to_kernel(w):
    s = lambda a, b: w[..., a:b]
    zero = lambda n: jnp.zeros(w.shape[:-1] + (n,), w.dtype)
    return jnp.concatenate([s(2208, 3232), s(3232, 4256), s(0, 512), s(768, 1280), s(1696, 2208), s(1280, 1536),
                            s(512, 640), s(640, 768), s(1536, 1664), zero(64), s(1664, 1696), zero(32)], axis=-1)


def _win_from_kernel(g):
    s = lambda a, n: g[..., a:a + n]
    return jnp.concatenate([s(Z_AQ, 512), s(Z_AK, 128), s(Z_AV, 128), s(Z_AG, 512), s(Z_QD, 256), s(Z_KVD, 128),
                            s(Z_KR + MLA_NOPE, MLA_ROPE), s(Z_BG, 512), s(Z_MA, 1024), s(Z_MB, 1024)], axis=-1)


def _wuq_to_kernel(w):
    w = w.reshape(w.shape[:-1] + (MLA_H, MLA_QK))
    w = jnp.pad(w, [(0, 0)] * (w.ndim - 1) + [(0, LANES - MLA_QK)])
    return w.reshape(w.shape[:-2] + (QFW,))


def _wuq_from_kernel(g):
    g = g.reshape(g.shape[:-1] + (MLA_H, LANES))[..., :MLA_QK]
    return g.reshape(g.shape[:-2] + (MLA_H * MLA_QK,))


def _wukv_to_kernel(w):
    w = w.reshape(w.shape[:-1] + (MLA_H, MLA_NOPE + MLA_V))
    k = jnp.pad(w[..., :MLA_NOPE], [(0, 0)] * (w.ndim - 1) + [(0, LANES - MLA_NOPE)])
    v = w[..., MLA_NOPE:]
    return jnp.concatenate([k.reshape(k.shape[:-2] + (QFW,)), v.reshape(v.shape[:-2] + (MLA_H * MLA_V,))], axis=-1)


def _wukv_from_kernel(g):
    k = g[..., :QFW].reshape(g.shape[:-1] + (MLA_H, LANES))[..., :MLA_NOPE]
    v = g[..., QFW:].reshape(g.shape[:-1] + (MLA_H, MLA_V))
    kv = jnp.concatenate([k, v], axis=-1)
    return kv.reshape(kv.shape[:-2] + (MLA_H * (MLA_NOPE + MLA_V),))


def _rope_tables(pos):
    half = MLA_ROPE // 2
    inv = 10000.0 ** (-jnp.arange(0, MLA_ROPE, 2, dtype=F32) / MLA_ROPE)
    ang = pos.astype(F32)[:, None] * inv
    cos, sin = jnp.cos(ang), jnp.sin(ang)
    one = jnp.ones((pos.shape[0], MLA_NOPE), F32)
    zero = lambda n: jnp.zeros((pos.shape[0], n), F32)
    tc = jnp.concatenate([one, cos, cos, one[:, :LANES - MLA_QK]], axis=1)
    ts1 = jnp.concatenate([zero(MLA_NOPE + half), sin, zero(LANES - MLA_QK)], axis=1)
    ts2 = jnp.concatenate([zero(MLA_NOPE), -sin, zero(LANES - MLA_NOPE - half)], axis=1)
    return tc, ts1, ts2


def _local_step(x, p, positions, loss_target, small, wts):
    nb, seq, _ = x.shape
    T = nb * seq
    tm = min(256, T)
    tq = min(512, seq)
    xf = x.reshape(T, D)
    pos = positions.reshape(T)
    posf = pos.astype(F32)
    pos_col, pos_row = posf.reshape(T, 1), posf.reshape(T // BLK, 1, BLK)
    tc, ts1, ts2 = _rope_tables(pos)

    saved = []
    for i in range(DEPTH):
        w, sm = wts[i], small[i]
        z, hb = _fwd_in(xf, sm["g_mix"], w["w_in"], tm)
        oa, lse_a = _swa_fwd(sm["sink"], z, pos_col, pos_row, nb, seq)
        qf, kf, v = _fwd_prep(z, sm["g_q"], sm["g_kv"], w["w_uq"], w["w_ukv"], tc, ts1, ts2, tm)
        ob, lse_b = _mla_fwd(qf, kf, v, nb, seq, tq)
        x1, ua, ub = _fwd_merge(xf, oa, ob, z, w["w_br_a"], w["w_br_b"], w["w_out"], tm)
        x2, pg, pp = _fwd_ple(x1, p[i].reshape(T, PLE), sm["g_ple"], w["w_ple_gate"], w["w_ple_proj"], tm)
        saved.append(dict(x=xf, z=z, hb=hb, oa=oa, lse_a=lse_a, qf=qf, kf=kf, v=v, ob=ob, lse_b=lse_b,
                          x1=x1, ua=ua, ub=ub, pg=pg, pp=pp))
        xf = x2

    dx, dg_final, loss = _loss_head(xf, small["g_final"], loss_target.reshape(T, D), tm)

    grads = [None] * DEPTH
    for i in reversed(range(DEPTH)):
        w, sm, sv = wts[i], small[i], saved[i]
        dx1, dwpg, dwpp, dg_ple = _bwd_ple(dx, sv["x1"], sv["pg"], sv["pp"], p[i].reshape(T, PLE), sm["g_ple"],
                                           w["w_ple_gate"], tm)
        doa, dob, dag, dbg, dma, dmb, dwa, dwb, dwo = _bwd_merge(
            dx1, sv["oa"], sv["ob"], sv["z"], sv["ua"], sv["ub"], w["w_br_a"], w["w_br_b"], w["w_out"], tm)
        dq_b, dk_b, dv_b = _mla_bwd(sv["qf"], sv["kf"], sv["v"], dob, sv["ob"], sv["lse_b"], nb, seq, tq)
        dqd, dkvd, dkr, dwq, dwkv, dgq, dgkv = _bwd_prep(dq_b, dk_b, dv_b, sv["z"], sm["g_q"], sm["g_kv"],
                                                         w["w_uq"], w["w_ukv"], tc, ts1, ts2, tm)
        dq_a, dkv_a, dsink = _swa_bwd(sm["sink"], sv["z"], pos_col, pos_row, doa, sv["oa"], sv["lse_a"], nb, seq)
        dx, dzb, dg_mix = _bwd_in([dma, dmb, dq_a, dag, dbg, dqd, dkv_a, dkvd, dkr], sv["x"], sm["g_mix"], dx1,
                                  w["w_in"], tm)
        dwin = _wgrad_in(sv["hb"], dzb, tm)
        grads[i] = dict(g_mix=dg_mix[0], w_in=_win_from_kernel(dwin), sink=dsink[0, :SWA_H], g_q=dgq[0],
                        w_uq=_wuq_from_kernel(dwq), g_kv=dgkv[0], w_ukv=_wukv_from_kernel(dwkv), w_br_a=dwa,
                        w_br_b=dwb, w_out=dwo, g_ple=dg_ple[0], w_ple_gate=dwpg, w_ple_proj=dwpp)
    return loss, dx.reshape(nb, seq, D), grads, dg_final[0]


def _kernel_weights(full):
    wi, wq, wkv = _win_to_kernel(full["w_in"]), _wuq_to_kernel(full["w_uq"]), _wukv_to_kernel(full["w_ukv"])
    return [dict(w_in=wi[i], w_uq=wq[i], w_ukv=wkv[i], w_br_a=full["w_br_a"][i], w_br_b=full["w_br_b"][i],
                 w_out=full["w_out"][i], w_ple_gate=full["w_ple_gate"][i], w_ple_proj=full["w_ple_proj"][i])
            for i in range(DEPTH)]


def _small_params(g_mix, sink, g_q, g_kv, g_ple, g_final):
    small = {i: dict(g_mix=g_mix[i][None], sink=sink[i], g_q=g_q[i][None], g_kv=g_kv[i][None], g_ple=g_ple[i][None])
             for i in range(DEPTH)}
    small["g_final"] = g_final[None]
    return small


WIDE_ROWS, WIDE_W = DEPTH * D, IN_W // N_DEV
UQ_W = MLA_H * MLA_QK // N_DEV
ROWS_PIECES = (("w_uq", DEPTH * QL), ("w_ukv", DEPTH * KVL), ("w_br_a", DEPTH * 512), ("w_br_b", DEPTH * 512),
               ("w_out", DEPTH * D), ("w_ple_gate", DEPTH * D), ("w_ple_proj", DEPTH * PLE))
SMALL = (("g_mix", (DEPTH, D)), ("sink", (DEPTH, SWA_H)), ("g_q", (DEPTH, QL)), ("g_kv", (DEPTH, KVL)),
         ("g_ple", (DEPTH, D)), ("g_final", (D,)))
SMALL_ROWS = 48
ROWS_N = 7680
WIDE_TILE, ROWS_TILE = 256, 1920


def _to_rows(name, a):
    lead = a.shape[:-3]
    if name == "w_uq":
        a = jnp.pad(a, [(0, 0)] * (a.ndim - 1) + [(0, LANES - UQ_W)])
    return a.reshape(lead + (-1, LANES))


def _from_rows(name, r):
    lead = r.shape[:-2]
    if name in ("w_out", "w_ple_gate"):
        return r.reshape(lead + (DEPTH, D // N_DEV, D))
    r = r.reshape(lead + (DEPTH, -1, LANES))
    return r[..., :UQ_W] if name == "w_uq" else r


def _pack_rows(blocks, vectors):
    lead = blocks["w_uq"].shape[:-3]
    vec = jnp.concatenate([vectors[n].reshape(-1) for n, _ in SMALL])
    vec = jnp.pad(vec, (0, SMALL_ROWS * LANES - vec.shape[0])).reshape(SMALL_ROWS, LANES)
    parts = [_to_rows(n, blocks[n]) for n, _ in ROWS_PIECES] + [jnp.broadcast_to(vec, lead + vec.shape)]
    used = sum(r for _, r in ROWS_PIECES) + SMALL_ROWS
    parts.append(jnp.zeros(lead + (ROWS_N - used, LANES), vec.dtype))
    return jnp.concatenate(parts, axis=-2)


def _unpack_rows(rows):
    blocks, off = {}, 0
    for n, r in ROWS_PIECES:
        blocks[n] = _from_rows(n, rows[..., off:off + r, :])
        off += r
    vec = rows[..., off:off + SMALL_ROWS, :].reshape(rows.shape[:-2] + (-1,))
    vectors, voff = {}, 0
    for n, shp in SMALL:
        size = 1
        for s in shp:
            size *= s
        vectors[n] = vec[..., voff:voff + size].reshape(rows.shape[:-2] + shp)
        voff += size
    return blocks, vectors


def _join(name, blocks):
    if name in ("w_out", "w_ple_gate"):
        return jnp.moveaxis(blocks, 0, 1).reshape(DEPTH, -1, blocks.shape[-1])
    return jnp.moveaxis(blocks, 0, 2).reshape(DEPTH, blocks.shape[2], -1)


def _split(name, full):
    if name in ("w_out", "w_ple_gate"):
        return jnp.moveaxis(full.reshape(DEPTH, N_DEV, -1, full.shape[-1]), 1, 0)
    return jnp.moveaxis(full.reshape(DEPTH, full.shape[1], N_DEV, -1), 2, 0)


MESH_ID = pl.DeviceIdType.MESH
ANY = pl.BlockSpec(memory_space=pl.ANY)


def _place():
    return lax.axis_index("x"), lax.axis_index("y"), lax.axis_index("c")


def _all_gather(blocks):
    n = len(blocks)

    def body(*refs):
        x_refs, out_refs, (send_sems, recv_sems, local_sems) = refs[:n], refs[n:2 * n], refs[2 * n:]
        x, y, c = _place()
        me, sibling = (x, y, c), (x, y, 1 - c)
        chips = [(1 - x, y), (x, 1 - y), (1 - x, 1 - y)]

        def slot(a, px, py, pc):
            return out_refs[a].at[4 * px + 2 * py + pc]

        def copy(a, k, blk, to, src=None):
            return pltpu.make_async_remote_copy(
                src_ref=slot(a, *blk) if src is None else src, dst_ref=slot(a, *blk),
                send_sem=send_sems.at[7 * a + k], recv_sem=recv_sems.at[7 * a + k], device_id=to,
                device_id_type=MESH_ID)

        mine = [pltpu.make_async_copy(x_refs[a], slot(a, *me), local_sems.at[a]) for a in range(n)]
        for cp in mine:
            cp.start()
        first = []
        for a in range(n):
            first += [copy(a, 0, me, sibling, src=x_refs[a])]
            first += [copy(a, 1 + j, me, (*chip, c), src=x_refs[a]) for j, chip in enumerate(chips)]
        for cp in first:
            cp.start()
        passed = []
        for j, chip in enumerate(chips):
            for a in range(n):
                copy(a, 1 + j, (*chip, c), me).wait_recv()
                passed.append(copy(a, 4 + j, (*chip, c), sibling))
                passed[-1].start()
        for a in range(n):
            copy(a, 0, sibling, me).wait_recv()
            for j, chip in enumerate(chips):
                copy(a, 4 + j, (*chip, 1 - c), me).wait_recv()
        for cp in first + passed:
            cp.wait_send()
        for cp in mine:
            cp.wait()

    return pl.pallas_call(
        body, name="all_gather_weights",
        out_shape=[SDS((N_DEV,) + b.shape, b.dtype) for b in blocks],
        in_specs=[ANY] * n, out_specs=[ANY] * n,
        scratch_shapes=[pltpu.SemaphoreType.DMA((7 * n,)), pltpu.SemaphoreType.DMA((7 * n,)),
                        pltpu.SemaphoreType.DMA((n,))],
    )(*blocks)


def _swap_sibling(arrs):
    n = len(arrs)

    def body(*refs):
        a_refs, out_refs, (send_sems, recv_sems) = refs[:n], refs[n:2 * n], refs[2 * n:]
        x, y, c = _place()
        copies = [pltpu.make_async_remote_copy(
            src_ref=a_refs[a].at[:, 1 - c], dst_ref=out_refs[a], send_sem=send_sems.at[a], recv_sem=recv_sems.at[a],
            device_id=(x, y, 1 - c), device_id_type=MESH_ID) for a in range(n)]
        for cp in copies:
            cp.start()
        for cp in copies:
            cp.wait()

    return pl.pallas_call(
        body, name="swap_sibling", out_shape=[SDS((a.shape[0],) + a.shape[2:], a.dtype) for a in arrs],
        in_specs=[ANY] * n, out_specs=[ANY] * n,
        scratch_shapes=[pltpu.SemaphoreType.DMA((n,)), pltpu.SemaphoreType.DMA((n,))],
    )(*arrs)


def _exchange_chips(arrs):
    n = len(arrs)

    def body(*refs):
        p_refs, out_refs, (send_sems, recv_sems, local_sems) = refs[:n], refs[n:2 * n], refs[2 * n:]
        x, y, c = _place()
        mine = 2 * x + y
        local = [pltpu.make_async_copy(p_refs[a].at[mine], out_refs[a].at[mine], local_sems.at[a]) for a in range(n)]
        for cp in local:
            cp.start()
        peers = [(1 - x, y), (x, 1 - y), (1 - x, 1 - y)]

        def copy(a, j, src_chip, dst_chip):
            px, py = peers[j]
            return pltpu.make_async_remote_copy(
                src_ref=p_refs[a].at[src_chip], dst_ref=out_refs[a].at[dst_chip], send_sem=send_sems.at[3 * a + j],
                recv_sem=recv_sems.at[3 * a + j], device_id=(px, py, c), device_id_type=MESH_ID)

        copies = [copy(a, j, 2 * px + py, mine) for a in range(n) for j, (px, py) in enumerate(peers)]
        for cp in copies:
            cp.start()
        for a in range(n):
            for j, (px, py) in enumerate(peers):
                copy(a, j, mine, 2 * px + py).wait_recv()
        for cp in copies:
            cp.wait_send()
        for cp in local:
            cp.wait()

    return pl.pallas_call(
        body, name="exchange_chips", out_shape=[SDS(a.shape, a.dtype) for a in arrs],
        in_specs=[ANY] * n, out_specs=[ANY] * n,
        scratch_shapes=[pltpu.SemaphoreType.DMA((3 * n,)), pltpu.SemaphoreType.DMA((3 * n,)),
                        pltpu.SemaphoreType.DMA((n,))],
    )(*arrs)


def _add_mine(g, recv, core, tile):
    _, _, rows, width = g.shape

    def body(c_ref, g_ref, r_ref, o_ref):
        o_ref[...] = g_ref[...] + r_ref[...]

    spec = pl.BlockSpec((None, tile, width), lambda k, i, c_ref: (k, i, 0))
    return pl.pallas_call(
        body, name="add_sibling", out_shape=SDS(recv.shape, F32),
        grid_spec=pltpu.PrefetchScalarGridSpec(
            num_scalar_prefetch=1, grid=(g.shape[0], rows // tile),
            in_specs=[pl.BlockSpec((None, None, tile, width), lambda k, i, c_ref: (k, c_ref[0], i, 0)), spec],
            out_specs=spec),
        compiler_params=_cp("parallel", "parallel"))(core, g, recv)


def _sum_adamw(parts, w, m, v, tile):
    rows, width = w.shape

    def body(p_ref, w_ref, m_ref, v_ref, g_ref, d_ref, nm_ref, nv_ref):
        g = ((p_ref[0] + p_ref[1]) + p_ref[2]) + p_ref[3]
        nm = ADAM_B1 * m_ref[...] + (1.0 - ADAM_B1) * g
        nv = ADAM_B2 * v_ref[...] + (1.0 - ADAM_B2) * jnp.square(g)
        m_hat = nm / (1.0 - ADAM_B1 ** ADAM_STEP)
        v_hat = nv / (1.0 - ADAM_B2 ** ADAM_STEP)
        g_ref[...] = g
        nm_ref[...] = nm
        nv_ref[...] = nv
        d_ref[...] = -ADAM_LR * (m_hat / (jnp.sqrt(v_hat) + ADAM_EPS) + ADAM_WD * w_ref[...])

    spec = pl.BlockSpec((tile, width), lambda i: (i, 0))
    return pl.pallas_call(
        body, grid=(rows // tile,), name="sum_adamw",
        in_specs=[pl.BlockSpec((4, tile, width), lambda i: (0, i, 0)), spec, spec, spec],
        out_specs=[spec] * 4, out_shape=[SDS((rows, width), F32)] * 4,
        compiler_params=_cp("parallel"))(parts, w, m, v)


def kernel(x, p, positions, g_mix, w_in, sink, g_q, w_uq, g_kv, w_ukv, w_br_a, w_br_b, w_out, g_ple, w_ple_gate, w_ple_proj, g_final, loss_target, m_g_mix, m_w_in, m_sink, m_g_q, m_w_uq, m_g_kv, m_w_ukv, m_w_br_a, m_w_br_b, m_w_out, m_g_ple, m_w_ple_gate, m_w_ple_proj, m_g_final, v_g_mix, v_w_in, v_sink, v_g_q, v_w_uq, v_g_kv, v_w_ukv, v_w_br_a, v_w_br_b, v_w_out, v_g_ple, v_w_ple_gate, v_w_ple_proj, v_g_final):
    weights = dict(g_mix=g_mix, w_in=w_in, sink=sink, g_q=g_q, w_uq=w_uq, g_kv=g_kv, w_ukv=w_ukv, w_br_a=w_br_a,
                   w_br_b=w_br_b, w_out=w_out, g_ple=g_ple, w_ple_gate=w_ple_gate, w_ple_proj=w_ple_proj,
                   g_final=g_final)
    mom1 = dict(g_mix=m_g_mix, w_in=m_w_in, sink=m_sink, g_q=m_g_q, w_uq=m_w_uq, g_kv=m_g_kv, w_ukv=m_w_ukv,
                w_br_a=m_w_br_a, w_br_b=m_w_br_b, w_out=m_w_out, g_ple=m_g_ple, w_ple_gate=m_w_ple_gate,
                w_ple_proj=m_w_ple_proj, g_final=m_g_final)
    mom2 = dict(g_mix=v_g_mix, w_in=v_w_in, sink=v_sink, g_q=v_g_q, w_uq=v_w_uq, g_kv=v_g_kv, w_ukv=v_w_ukv,
                w_br_a=v_w_br_a, w_br_b=v_w_br_b, w_out=v_w_out, g_ple=v_g_ple, w_ple_gate=v_w_ple_gate,
                w_ple_proj=v_w_ple_proj, g_final=v_g_final)
    wide = lambda d: d["w_in"].reshape(WIDE_ROWS, WIDE_W)
    rows = lambda d: _pack_rows({n: d[n] for n, _ in ROWS_PIECES}, {n: d[n] for n, _ in SMALL})

    got_wide, got_rows = _all_gather([wide(weights).astype(BF16), rows(weights).astype(BF16)])
    blocks, _ = _unpack_rows(got_rows)
    full = {n: _join(n, blocks[n]) for n, _ in ROWS_PIECES}
    full["w_in"] = _join("w_in", got_wide.reshape(N_DEV, DEPTH, D, WIDE_W))
    wts = _kernel_weights(full)
    small = _small_params(g_mix, sink, g_q, g_kv, g_ple, g_final)

    loss, grad_x, grads, dg_final = _local_step(x, p, positions, loss_target, small, wts)
    loss = lax.psum(loss[0, 0], ("x", "y", "c"))

    stacked = {n: jnp.stack([grads[i][n] for i in range(DEPTH)]) for n in grads[0]}
    stacked["g_final"] = dg_final
    pay_wide = _split("w_in", stacked["w_in"]).reshape(N_DEV // 2, 2, WIDE_ROWS, WIDE_W)
    pay_rows = _pack_rows({n: _split(n, stacked[n]) for n, _ in ROWS_PIECES}, stacked)
    pay_rows = pay_rows.reshape(N_DEV // 2, 2, ROWS_N, LANES)
    core = lax.axis_index("c").astype(jnp.int32).reshape(1)
    from_sibling = _swap_sibling([pay_wide, pay_rows])
    chip_partial = [_add_mine(pay_wide, from_sibling[0], core, WIDE_TILE),
                    _add_mine(pay_rows, from_sibling[1], core, ROWS_TILE)]
    parts_wide, parts_rows = _exchange_chips(chip_partial)
    out_wide = _sum_adamw(parts_wide, wide(weights), wide(mom1), wide(mom2), WIDE_TILE)
    out_rows = _sum_adamw(parts_rows, rows(weights), rows(mom1), rows(mom2), ROWS_TILE)

    outs = []
    for ow, orow in zip(out_wide, out_rows):
        named, vectors = _unpack_rows(orow)
        named.update(vectors)
        named["w_in"] = ow.reshape(DEPTH, D, WIDE_W)
        outs += [named[n] for n in weights]
    return (loss, grad_x, *outs)
```

```python
import functools

import jax
import jax.numpy as jnp
from jax import lax
from jax.experimental import pallas as pl
from jax.experimental.pallas import tpu as pltpu

F32, BF16 = jnp.float32, jnp.bfloat16
SDS = jax.ShapeDtypeStruct

D = 1024
DEPTH = 2
PLE = 256
BLK = 128
EPS = 1e-6
NEG = -1e30
SWA_H, SWA_KV, SWA_DH = 8, 2, 64
MLA_H, MLA_NOPE, MLA_ROPE, MLA_V = 8, 64, 32, 64
MLA_QK = MLA_NOPE + MLA_ROPE
QL, KVL = 256, 128
IN_W = 4256
N_DEV = 8

V7X_VMEM_BYTES = 64 * 1024 * 1024
LANES = 128
VMEM_LIMIT = V7X_VMEM_BYTES * 7 // 8

ZW = 4352
Z_MA, Z_MB, Z_AQ, Z_AG, Z_BG, Z_QD, Z_AK, Z_AV, Z_KVD, Z_KR = 0, 1024, 2048, 2560, 3072, 3584, 3840, 3968, 4096, 4224
QFW = MLA_H * LANES
KVW = QFW + MLA_H * MLA_V
MLA_SCALE = MLA_QK ** -0.5
SWA_SCALE = SWA_DH ** -0.5
ROLL_UP, ROLL_DOWN = MLA_ROPE // 2, LANES - MLA_ROPE // 2

ADAM_LR, ADAM_B1, ADAM_B2, ADAM_EPS, ADAM_WD, ADAM_STEP = 0.001, 0.9, 0.999, 1e-08, 0.01, 10

FLAT_W = 1024


def _cp(*sem):
    return pltpu.CompilerParams(dimension_semantics=sem, vmem_limit_bytes=VMEM_LIMIT)


def _row(tm, w, col=0):
    return pl.BlockSpec((tm, w), lambda i: (i, col))


def _res(shape):
    return pl.BlockSpec(shape, lambda *_: (0,) * len(shape), pipeline_mode=pl.Buffered(1))


def _acc(shape):
    return pl.BlockSpec(shape, lambda *_: (0,) * len(shape))


def _rstd(xf):
    return lax.rsqrt(jnp.mean(xf * xf, axis=-1, keepdims=True) + EPS)


def _norm_bwd(dh, n, r, g):
    dn = dh * g
    return r * (dn - n * jnp.mean(dn * n, axis=-1, keepdims=True)), dh * n


def _nt(a, b):
    return lax.dot_general(a, b, (((1,), (1,)), ((), ())), preferred_element_type=F32)


def _tn(a, b):
    return lax.dot_general(a, b, (((0,), (0,)), ((), ())), preferred_element_type=F32)


def _nn(a, b):
    return jnp.dot(a, b, preferred_element_type=F32)


def _sig(x):
    return jax.nn.sigmoid(x)


def _rope(t, c, s1, s2):
    return t * c + pltpu.roll(t, ROLL_UP, 1) * s1 + pltpu.roll(t, ROLL_DOWN, 1) * s2


def _rope_t(d, c, s1, s2):
    return d * c + pltpu.roll(d * s1, ROLL_DOWN, 1) + pltpu.roll(d * s2, ROLL_UP, 1)


def _fwd_in(x, g, w, tm):
    T = x.shape[0]

    def body(x_ref, g_ref, w_ref, z_ref, h_ref):
        xf = x_ref[...]
        h = ((xf * _rstd(xf)) * g_ref[...]).astype(BF16)
        h_ref[...] = h
        z_ref[...] = _nn(h, w_ref[...])

    return pl.pallas_call(
        body, grid=(T // tm,), name="fwd_in",
        in_specs=[_row(tm, D), _res((1, D)), _res((D, ZW))],
        out_specs=[_row(tm, ZW), _row(tm, D)],
        out_shape=[SDS((T, ZW), F32), SDS((T, D), BF16)],
        compiler_params=_cp("parallel"))(x, g, w)


def _fwd_prep(z, gq, gkv, wq, wkv, tc, ts1, ts2, tm):
    T = z.shape[0]

    def body(qd_ref, kvd_ref, kr_ref, gq_ref, gkv_ref, wq_ref, wkv_ref, c_ref, s1_ref, s2_ref, q_ref, k_ref, v_ref):
        qd, kvd = qd_ref[...], kvd_ref[...]
        hq = ((qd * _rstd(qd)) * gq_ref[...]).astype(BF16)
        hkv = ((kvd * _rstd(kvd)) * gkv_ref[...]).astype(BF16)
        qf = _nn(hq, wq_ref[...])
        kvf = _nn(hkv, wkv_ref[...])
        c, s1, s2 = c_ref[...], s1_ref[...], s2_ref[...]
        krb = _rope(kr_ref[...], c, s1, s2)
        for h in range(MLA_H):
            sl = slice(LANES * h, LANES * (h + 1))
            q_ref[:, sl] = _rope(qf[:, sl], c, s1, s2).astype(BF16)
            k_ref[:, sl] = (kvf[:, sl] + krb).astype(BF16)
        v_ref[...] = kvf[:, QFW:].astype(BF16)

    return pl.pallas_call(
        body, grid=(T // tm,), name="fwd_prep",
        in_specs=[_row(tm, QL, Z_QD // QL), _row(tm, KVL, Z_KVD // KVL), _row(tm, LANES, Z_KR // LANES),
                  _res((1, QL)), _res((1, KVL)), _res((QL, QFW)), _res((KVL, KVW)),
                  _row(tm, LANES), _row(tm, LANES), _row(tm, LANES)],
        out_specs=[_row(tm, QFW), _row(tm, QFW), _row(tm, MLA_H * MLA_V)],
        out_shape=[SDS((T, QFW), BF16), SDS((T, QFW), BF16), SDS((T, MLA_H * MLA_V), BF16)],
        compiler_params=_cp("parallel"))(z, z, z, gq, gkv, wq, wkv, tc, ts1, ts2)


def _mla_fwd(qf, kf, v, nb, seq, tq):
    T = qf.shape[0]
    nq = seq // tq
    pw = 2 * LANES

    def body(q_ref, k_ref, v_ref, o_ref, lse_ref, m_s, l_s, acc_s):
        qi, ki = pl.program_id(2), pl.program_id(3)

        @pl.when(ki == 0)
        def _():
            m_s[...] = jnp.full(m_s.shape, NEG, F32)
            l_s[...] = jnp.zeros(l_s.shape, F32)
            acc_s[...] = jnp.zeros(acc_s.shape, F32)

        def step(masked):
            if masked:
                keys = lax.broadcasted_iota(jnp.int32, (tq, tq), 0)
                queries = lax.broadcasted_iota(jnp.int32, (tq, tq), 1)
                mask = keys <= queries
            for j in range(2):
                wide = slice(LANES * j, LANES * (j + 1))
                s = _nt(k_ref[:, wide], q_ref[:, wide]) * MLA_SCALE
                if masked:
                    s = jnp.where(mask, s, NEG)
                m_prev = m_s[j]
                m_new = jnp.maximum(m_prev, jnp.max(s, axis=0, keepdims=True))
                alpha = jnp.exp(m_prev - m_new)
                p = jnp.exp(s - m_new)
                l_s[j] = alpha * l_s[j] + jnp.sum(p, axis=0, keepdims=True)
                m_s[j] = m_new
                rows = slice(MLA_V * j, MLA_V * (j + 1))
                acc_s[rows, :] = alpha * acc_s[rows, :] + _tn(v_ref[:, rows], p.astype(BF16))

        @pl.when(ki < qi)
        def _():
            step(False)

        @pl.when(ki == qi)
        def _():
            step(True)
            for j in range(2):
                rows = slice(MLA_V * j, MLA_V * (j + 1))
                acc_s[rows, :] = acc_s[rows, :] / l_s[j]
                lse_ref[j:j + 1, :] = m_s[j] + jnp.log(l_s[j])
            o_ref[...] = acc_s[...].T

    kv_map = lambda b, hp, qi, ki: (b * nq + jnp.minimum(ki, qi), hp)
    return pl.pallas_call(
        body, grid=(nb, MLA_H // 2, nq, nq), name="mla_fwd",
        in_specs=[pl.BlockSpec((tq, pw), lambda b, hp, qi, ki: (b * nq + qi, hp)),
                  pl.BlockSpec((tq, pw), kv_map),
                  pl.BlockSpec((tq, LANES), kv_map)],
        out_specs=[pl.BlockSpec((tq, LANES), lambda b, hp, qi, ki: (b * nq + qi, hp)),
                   pl.BlockSpec((None, 2, tq), lambda b, hp, qi, ki: (hp, 0, b * nq + qi))],
        out_shape=[SDS((T, MLA_H * MLA_V), F32), SDS((MLA_H // 2, 2, T), F32)],
        scratch_shapes=[pltpu.VMEM((2, 1, tq), F32), pltpu.VMEM((2, 1, tq), F32), pltpu.VMEM((LANES, tq), F32)],
        compiler_params=_cp("parallel", "parallel", "parallel", "arbitrary"))(qf, kf, v)


def _swa_valid(n):
    r = lax.broadcasted_iota(jnp.int32, (BLK, 2 * BLK), 0)
    c = lax.broadcasted_iota(jnp.int32, (BLK, 2 * BLK), 1)
    return (c > r) & (c <= r + BLK) & ((c >= BLK) | (n > 0))


def _swa_specs(nblk):
    cur = lambda b, n: (b * nblk + n, 0)
    prev = lambda b, n: (b * nblk + jnp.maximum(n - 1, 0), 0)
    kvc = Z_AK // (2 * BLK)
    return [pl.BlockSpec(memory_space=pltpu.SMEM),
            pl.BlockSpec((BLK, 512), lambda b, n: (b * nblk + n, Z_AQ // 512)),
            pl.BlockSpec((BLK, 2 * BLK), lambda b, n: (b * nblk + n, kvc)),
            pl.BlockSpec((BLK, 2 * BLK), lambda b, n: (b * nblk + jnp.maximum(n - 1, 0), kvc)),
            pl.BlockSpec((BLK, 1), cur),
            pl.BlockSpec((1, 1, BLK), lambda b, n: (b * nblk + n, 0, 0)),
            pl.BlockSpec((1, 1, BLK), lambda b, n: (b * nblk + jnp.maximum(n - 1, 0), 0, 0))]


def _swa_scores(n, q_ref, kvc_ref, kvp_ref, pc_ref, prc_ref, prp_ref):
    kv = jnp.concatenate([kvp_ref[...], kvc_ref[...]], axis=0)
    kb, vb = kv[:, :BLK].astype(BF16), kv[:, BLK:].astype(BF16)
    dist = pc_ref[...] - jnp.concatenate([prp_ref[0], prc_ref[0]], axis=1)
    valid = _swa_valid(n)

    def scores(h):
        g = h // (SWA_H // SWA_KV)
        qh = q_ref[:, SWA_DH * h:SWA_DH * (h + 1)].astype(BF16)
        s = _nt(qh, kb[:, SWA_DH * g:SWA_DH * (g + 1)]) * SWA_SCALE - (2.0 ** -(h + 1)) * dist
        return qh, jnp.where(valid, s, NEG)

    return kb, vb, scores


def _swa_fwd(sink, z, pos_col, pos_row, nb, seq):
    T = z.shape[0]
    nblk = seq // BLK

    def body(sink_ref, q_ref, kvc_ref, kvp_ref, pc_ref, prc_ref, prp_ref, o_ref, lse_ref):
        kb, vb, scores = _swa_scores(pl.program_id(1), q_ref, kvc_ref, kvp_ref, pc_ref, prc_ref, prp_ref)
        for h in range(SWA_H):
            g = h // (SWA_H // SWA_KV)
            _, s = scores(h)
            sk = sink_ref[h]
            m = jnp.maximum(jnp.max(s, axis=-1, keepdims=True), sk)
            e = jnp.exp(s - m)
            den = jnp.sum(e, axis=-1, keepdims=True) + jnp.exp(sk - m)
            o_ref[:, SWA_DH * h:SWA_DH * (h + 1)] = _nn(e.astype(BF16), vb[:, SWA_DH * g:SWA_DH * (g + 1)]) / den
            lse_ref[:, h:h + 1] = m + jnp.log(den)

    return pl.pallas_call(
        body, grid=(nb, nblk), name="swa_fwd",
        in_specs=_swa_specs(nblk),
        out_specs=[pl.BlockSpec((BLK, 512), lambda b, n: (b * nblk + n, 0)),
                   pl.BlockSpec((BLK, SWA_H), lambda b, n: (b * nblk + n, 0))],
        out_shape=[SDS((T, 512), F32), SDS((T, SWA_H), F32)],
        compiler_params=_cp("parallel", "parallel"))(sink, z, z, z, pos_col, pos_row, pos_row)


def _fwd_merge(x, oa, ob, z, wa, wb, wo, tm):
    T = x.shape[0]

    def body(x_ref, oa_ref, ob_ref, ag_ref, bg_ref, ma_ref, mb_ref, wa_ref, wb_ref, wo_ref, x1_ref, ua_ref, ub_ref):
        ag, bg = ag_ref[...], bg_ref[...]
        ua = _nn((oa_ref[...] * (ag * _sig(ag))).astype(BF16), wa_ref[...])
        ub = _nn((ob_ref[...] * (bg * _sig(bg))).astype(BF16), wb_ref[...])
        ua_ref[...] = ua
        ub_ref[...] = ub
        y = _sig(ma_ref[...]) * ua + _sig(mb_ref[...]) * ub
        x1_ref[...] = x_ref[...] + _nn(y.astype(BF16), wo_ref[...])

    return pl.pallas_call(
        body, grid=(T // tm,), name="fwd_merge",
        in_specs=[_row(tm, D), _row(tm, 512), _row(tm, 512), _row(tm, 512, Z_AG // 512), _row(tm, 512, Z_BG // 512),
                  _row(tm, D, Z_MA // D), _row(tm, D, Z_MB // D), _res((512, D)), _res((512, D)), _res((D, D))],
        out_specs=[_row(tm, D)] * 3,
        out_shape=[SDS((T, D), F32)] * 3,
        compiler_params=_cp("parallel"))(x, oa, ob, z, z, z, z, wa, wb, wo)


def _fwd_ple(x1, p, g, wpg, wpp, tm):
    T = x1.shape[0]

    def body(x_ref, p_ref, g_ref, wpg_ref, wpp_ref, x2_ref, pg_ref, pp_ref):
        xf = x_ref[...]
        h1 = ((xf * _rstd(xf)) * g_ref[...]).astype(BF16)
        pg = _sig(_nn(h1, wpg_ref[...]))
        pp = _nn(p_ref[...].astype(BF16), wpp_ref[...])
        pg_ref[...] = pg
        pp_ref[...] = pp
        x2_ref[...] = xf + pg * pp

    return pl.pallas_call(
        body, grid=(T // tm,), name="fwd_ple",
        in_specs=[_row(tm, D), _row(tm, PLE), _res((1, D)), _res((D, D)), _res((PLE, D))],
        out_specs=[_row(tm, D)] * 3,
        out_shape=[SDS((T, D), F32)] * 3,
        compiler_params=_cp("parallel"))(x1, p, g, wpg, wpp)


def _loss_head(x, g, tgt, tm):
    T = x.shape[0]

    def body(x_ref, g_ref, t_ref, dx_ref, dg_ref, loss_ref):
        @pl.when(pl.program_id(0) == 0)
        def _():
            dg_ref[...] = jnp.zeros(dg_ref.shape, F32)
            loss_ref[...] = jnp.zeros(loss_ref.shape, F32)

        xf, gf = x_ref[...], g_ref[...]
        r = _rstd(xf)
        n = xf * r
        err = n * gf - t_ref[...]
        loss_ref[...] += 0.5 * jnp.sum(jnp.mean(err * err, axis=-1, keepdims=True), axis=0, keepdims=True)
        dx, dgr = _norm_bwd(err * (1.0 / D), n, r, gf)
        dx_ref[...] = dx
        dg_ref[...] += jnp.sum(dgr, axis=0, keepdims=True)

    return pl.pallas_call(
        body, grid=(T // tm,), name="loss_head",
        in_specs=[_row(tm, D), _res((1, D)), _row(tm, D)],
        out_specs=[_row(tm, D), _acc((1, D)), _acc((1, LANES))],
        out_shape=[SDS((T, D), F32), SDS((1, D), F32), SDS((1, LANES), F32)],
        compiler_params=_cp("arbitrary"))(x, g, tgt)


def _bwd_ple(dx2, x1, pg, pp, p, g, wpg, tm):
    T = x1.shape[0]

    def body(d_ref, x_ref, pg_ref, pp_ref, p_ref, g_ref, w_ref, dx_ref, dwg_ref, dwp_ref, dg_ref):
        @pl.when(pl.program_id(0) == 0)
        def _():
            dwg_ref[...] = jnp.zeros(dwg_ref.shape, F32)
            dwp_ref[...] = jnp.zeros(dwp_ref.shape, F32)
            dg_ref[...] = jnp.zeros(dg_ref.shape, F32)

        d, xf, pg, gf = d_ref[...], x_ref[...], pg_ref[...], g_ref[...]
        r = _rstd(xf)
        n = xf * r
        dpgl = (d * pp_ref[...] * pg * (1.0 - pg)).astype(BF16)
        dwg_ref[...] += _tn((n * gf).astype(BF16), dpgl)
        dwp_ref[...] += _tn(p_ref[...].astype(BF16), (d * pg).astype(BF16))
        dxn, dgr = _norm_bwd(_nt(dpgl, w_ref[...]), n, r, gf)
        dx_ref[...] = d + dxn
        dg_ref[...] += jnp.sum(dgr, axis=0, keepdims=True)

    return pl.pallas_call(
        body, grid=(T // tm,), name="bwd_ple",
        in_specs=[_row(tm, D)] * 4 + [_row(tm, PLE), _res((1, D)), _res((D, D))],
        out_specs=[_row(tm, D), _acc((D, D)), _acc((PLE, D)), _acc((1, D))],
        out_shape=[SDS((T, D), F32), SDS((D, D), F32), SDS((PLE, D), F32), SDS((1, D), F32)],
        compiler_params=_cp("arbitrary"))(dx2, x1, pg, pp, p, g, wpg)


def _bwd_merge(dx1, oa, ob, z, ua, ub, wa, wb, wo, tm):
    T = dx1.shape[0]

    def body(d_ref, oa_ref, ob_ref, ag_ref, bg_ref, ma_ref, mb_ref, ua_ref, ub_ref, wa_ref, wb_ref, wo_ref,
             doa_ref, dob_ref, dag_ref, dbg_ref, dma_ref, dmb_ref, dsum_ref, dwa_ref, dwb_ref, dwo_ref):
        @pl.when(pl.program_id(0) == 0)
        def _():
            dwa_ref[...] = jnp.zeros(dwa_ref.shape, F32)
            dwb_ref[...] = jnp.zeros(dwb_ref.shape, F32)
            dwo_ref[...] = jnp.zeros(dwo_ref.shape, F32)

        db = d_ref[...].astype(BF16)
        ua, ub = ua_ref[...], ub_ref[...]
        sa, sb = _sig(ma_ref[...]), _sig(mb_ref[...])
        dwo_ref[...] += _tn((sa * ua + sb * ub).astype(BF16), db)
        dy = _nt(db, wo_ref[...])
        dma_ref[...] = (dy * ua * sa * (1.0 - sa)).astype(BF16)
        dmb_ref[...] = (dy * ub * sb * (1.0 - sb)).astype(BF16)
        for (o_ref, gate_ref, s, w_ref, do_ref, dgate_ref, dw_ref) in (
                (oa_ref, ag_ref, sa, wa_ref, doa_ref, dag_ref, dwa_ref),
                (ob_ref, bg_ref, sb, wb_ref, dob_ref, dbg_ref, dwb_ref)):
            du = (dy * s).astype(BF16)
            raw, gate = o_ref[...], gate_ref[...]
            sg = _sig(gate)
            silu = gate * sg
            dw_ref[...] += _tn((raw * silu).astype(BF16), du)
            do = _nt(du, w_ref[...])
            draw = do * silu
            do_ref[...] = draw
            dgate_ref[...] = (do * raw * (sg * (1.0 + gate * (1.0 - sg)))).astype(BF16)
        head = lax.broadcasted_iota(jnp.int32, (MLA_H, 512), 0)
        col = lax.broadcasted_iota(jnp.int32, (MLA_H, 512), 1)
        pick = jnp.where(col // MLA_V == head, 1.0, 0.0).astype(F32)
        dsum_ref[...] = lax.dot_general(pick, draw * raw, (((1,), (1,)), ((), ())), precision=lax.Precision.HIGHEST,
                                        preferred_element_type=F32)

    return pl.pallas_call(
        body, grid=(T // tm,), name="bwd_merge",
        in_specs=[_row(tm, D), _row(tm, 512), _row(tm, 512), _row(tm, 512, Z_AG // 512), _row(tm, 512, Z_BG // 512),
                  _row(tm, D, Z_MA // D), _row(tm, D, Z_MB // D), _row(tm, D), _row(tm, D),
                  _res((512, D)), _res((512, D)), _res((D, D))],
        out_specs=[_row(tm, 512)] * 4 + [_row(tm, D)] * 2 + [pl.BlockSpec((MLA_H, tm), lambda i: (0, i))]
        + [_acc((512, D)), _acc((512, D)), _acc((D, D))],
        out_shape=[SDS((T, 512), F32), SDS((T, 512), F32), SDS((T, 512), BF16), SDS((T, 512), BF16),
                   SDS((T, D), BF16), SDS((T, D), BF16), SDS((MLA_H, T), F32),
                   SDS((512, D), F32), SDS((512, D), F32), SDS((D, D), F32)],
        compiler_params=_cp("arbitrary"))(dx1, oa, ob, z, z, z, z, ua, ub, wa, wb, wo)


def _mla_bwd(qf, kf, v, do, lse, dsum, nb, seq, tq):
    T = qf.shape[0]
    nq = seq // tq
    pw = 2 * LANES

    def body(q_ref, k_ref, v_ref, do_ref, lse_ref, dsum_ref, dq_ref, dk_ref, dv_ref, dk_s, dv_s, dqt_s):
        ki, qi = pl.program_id(2), pl.program_id(3)

        @pl.when((ki == 0) & (qi == 0))
        def _():
            dqt_s[...] = jnp.zeros(dqt_s.shape, F32)

        @pl.when(qi == ki)
        def _():
            dk_s[...] = jnp.zeros(dk_s.shape, F32)
            dv_s[...] = jnp.zeros(dv_s.shape, F32)

        def step(masked):
            if masked:
                keys = lax.broadcasted_iota(jnp.int32, (tq, tq), 0)
                queries = lax.broadcasted_iota(jnp.int32, (tq, tq), 1)
                mask = keys <= queries
            for j in range(2):
                wide = slice(LANES * j, LANES * (j + 1))
                sl = slice(MLA_V * j, MLA_V * (j + 1))
                q, k = q_ref[:, wide], k_ref[:, wide]
                dob = do_ref[:, sl].astype(BF16)
                s = _nt(k, q) * MLA_SCALE
                if masked:
                    s = jnp.where(mask, s, NEG)
                p = jnp.exp(s - lse_ref[j:j + 1, :])
                dv_s[:, sl] += _nn(p.astype(BF16), dob)
                ds = (p * (_nt(v_ref[:, sl], dob) - dsum_ref[j:j + 1, :]) * MLA_SCALE).astype(BF16)
                dk_s[:, wide] += _nn(ds, q)
                dqt_s[qi, wide, :] += _tn(k, ds)

        @pl.when(qi > ki)
        def _():
            step(False)

        @pl.when(qi == ki)
        def _():
            step(True)

        @pl.when(qi == nq - 1)
        def _():
            dk_ref[...] = dk_s[...]
            dv_ref[...] = dv_s[...]

        @pl.when((ki == nq - 1) & (qi == nq - 1))
        def _():
            for n in range(nq):
                dq_ref[tq * n:tq * (n + 1), :] = dqt_s[n].T

    qmap = lambda b, hp, ki, qi: (b * nq + jnp.maximum(qi, ki), hp)
    kmap = lambda b, hp, ki, qi: (b * nq + ki, hp)
    stat = pl.BlockSpec((None, 2, tq), lambda b, hp, ki, qi: (hp, 0, b * nq + jnp.maximum(qi, ki)))
    return pl.pallas_call(
        body, grid=(nb, MLA_H // 2, nq, nq), name="mla_bwd",
        in_specs=[pl.BlockSpec((tq, pw), qmap), pl.BlockSpec((tq, pw), kmap), pl.BlockSpec((tq, LANES), kmap),
                  pl.BlockSpec((tq, LANES), qmap), stat, stat],
        out_specs=[pl.BlockSpec((seq, pw), lambda b, hp, ki, qi: (b, hp)),
                   pl.BlockSpec((tq, pw), kmap), pl.BlockSpec((tq, LANES), kmap)],
        out_shape=[SDS((T, QFW), F32), SDS((T, QFW), F32), SDS((T, MLA_H * MLA_V), F32)],
        scratch_shapes=[pltpu.VMEM((tq, pw), F32), pltpu.VMEM((tq, LANES), F32), pltpu.VMEM((nq, pw, tq), F32)],
        compiler_params=_cp("parallel", "parallel", "arbitrary", "arbitrary"))(qf, kf, v, do, lse, dsum)


def _swa_bwd(sink, z, pos_col, pos_row, do, o, lse, nb, seq):
    T = z.shape[0]
    nblk = seq // BLK

    def body(sink_ref, q_ref, kvc_ref, kvp_ref, pc_ref, prc_ref, prp_ref, do_ref, o_ref, lse_ref,
             dq_ref, dkv_ref, dsink_ref):
        b, n = pl.program_id(0), pl.program_id(1)

        @pl.when((b == 0) & (n == 0))
        def _():
            dsink_ref[...] = jnp.zeros(dsink_ref.shape, F32)

        @pl.when(n == 0)
        def _():
            dkv_ref[...] = jnp.zeros(dkv_ref.shape, F32)

        kb, vb, scores = _swa_scores(n, q_ref, kvc_ref, kvp_ref, pc_ref, prc_ref, prp_ref)
        lane = lax.broadcasted_iota(jnp.int32, (1, LANES), 1)
        dsink = jnp.zeros((1, LANES), F32)
        dkv = [[None, None], [None, None]]
        for h in range(SWA_H):
            g = h // (SWA_H // SWA_KV)
            hs = slice(SWA_DH * h, SWA_DH * (h + 1))
            gs = slice(SWA_DH * g, SWA_DH * (g + 1))
            qh, s = scores(h)
            lse = lse_ref[:, h:h + 1]
            p = jnp.exp(s - lse)
            doh = do_ref[:, hs]
            dsum = jnp.sum(doh * o_ref[:, hs], axis=-1, keepdims=True)
            dob = doh.astype(BF16)
            ds = (p * (_nt(dob, vb[:, gs]) - dsum) * SWA_SCALE).astype(BF16)
            dq_ref[:, hs] = _nn(ds, kb[:, gs]).astype(BF16)
            dk, dv = _tn(ds, qh), _tn(p.astype(BF16), dob)
            dkv[g][0] = dk if dkv[g][0] is None else dkv[g][0] + dk
            dkv[g][1] = dv if dkv[g][1] is None else dkv[g][1] + dv
            dsk = jnp.sum(-jnp.exp(sink_ref[h] - lse) * dsum, axis=0, keepdims=True)
            dsink = dsink + jnp.where(lane == h, dsk, 0.0)
        dsink_ref[...] += dsink
        upd = jnp.concatenate([dkv[0][0], dkv[1][0], dkv[0][1], dkv[1][1]], axis=1)
        dkv_ref[pl.ds(pl.multiple_of(n * BLK, BLK), BLK), :] += upd[BLK:]

        @pl.when(n > 0)
        def _():
            dkv_ref[pl.ds(pl.multiple_of((n - 1) * BLK, BLK), BLK), :] += upd[:BLK]

    return pl.pallas_call(
        body, grid=(nb, nblk), name="swa_bwd",
        in_specs=_swa_specs(nblk) + [pl.BlockSpec((BLK, 512), lambda b, n: (b * nblk + n, 0))] * 2
        + [pl.BlockSpec((BLK, SWA_H), lambda b, n: (b * nblk + n, 0))],
        out_specs=[pl.BlockSpec((BLK, 512), lambda b, n: (b * nblk + n, 0)),
                   pl.BlockSpec((seq, 2 * BLK), lambda b, n: (b, 0)),
                   pl.BlockSpec((1, LANES), lambda b, n: (0, 0))],
        out_shape=[SDS((T, 512), BF16), SDS((T, 2 * BLK), F32), SDS((1, LANES), F32)],
        compiler_params=_cp("arbitrary", "arbitrary"))(sink, z, z, z, pos_col, pos_row, pos_row, do, o, lse)


def _bwd_prep(dq, dk, dv, z, gq, gkv, wq, wkv, tc, ts1, ts2, tm):
    T = z.shape[0]

    def body(dq_ref, dk_ref, dv_ref, qd_ref, kvd_ref, gq_ref, gkv_ref, wq_ref, wkv_ref, c_ref, s1_ref, s2_ref,
             dqd_ref, dkvd_ref, dkr_ref, dwq_ref, dwkv_ref, dgq_ref, dgkv_ref, dqb_s, dkvb_s):
        @pl.when(pl.program_id(0) == 0)
        def _():
            for ref in (dwq_ref, dwkv_ref, dgq_ref, dgkv_ref):
                ref[...] = jnp.zeros(ref.shape, F32)

        c, s1, s2 = c_ref[...], s1_ref[...], s2_ref[...]
        lane = lax.broadcasted_iota(jnp.int32, (1, LANES), 1)
        rope_lanes = (lane >= MLA_NOPE) & (lane < MLA_QK)
        dkb = jnp.zeros((tm, LANES), F32)
        for h in range(MLA_H):
            sl = slice(LANES * h, LANES * (h + 1))
            dqb_s[:, sl] = _rope_t(dq_ref[:, sl], c, s1, s2).astype(BF16)
            dkh = dk_ref[:, sl]
            dkb = dkb + dkh
            dkvb_s[:, sl] = dkh.astype(BF16)
        dkvb_s[:, QFW:] = dv_ref[...].astype(BF16)
        dkr_ref[...] = _rope_t(jnp.where(rope_lanes, dkb, 0.0), c, s1, s2).astype(BF16)

        for (x_ref, g_ref, w_ref, d_s, dx_ref, dw_ref, dg_ref) in (
                (qd_ref, gq_ref, wq_ref, dqb_s, dqd_ref, dwq_ref, dgq_ref),
                (kvd_ref, gkv_ref, wkv_ref, dkvb_s, dkvd_ref, dwkv_ref, dgkv_ref)):
            xf, gf, db = x_ref[...], g_ref[...], d_s[...]
            r = _rstd(xf)
            n = xf * r
            dw_ref[...] += _tn((n * gf).astype(BF16), db)
            dx, dgr = _norm_bwd(_nt(db, w_ref[...]), n, r, gf)
            dx_ref[...] = dx.astype(BF16)
            dg_ref[...] += jnp.sum(dgr, axis=0, keepdims=True)

    return pl.pallas_call(
        body, grid=(T // tm,), name="bwd_prep",
        in_specs=[_row(tm, QFW), _row(tm, QFW), _row(tm, MLA_H * MLA_V),
                  _row(tm, QL, Z_QD // QL), _row(tm, KVL, Z_KVD // KVL),
                  _res((1, QL)), _res((1, KVL)), _res((QL, QFW)), _res((KVL, KVW)),
                  _row(tm, LANES), _row(tm, LANES), _row(tm, LANES)],
        out_specs=[_row(tm, QL), _row(tm, KVL), _row(tm, LANES),
                   _acc((QL, QFW)), _acc((KVL, KVW)), _acc((1, QL)), _acc((1, KVL))],
        out_shape=[SDS((T, QL), BF16), SDS((T, KVL), BF16), SDS((T, LANES), BF16),
                   SDS((QL, QFW), F32), SDS((KVL, KVW), F32), SDS((1, QL), F32), SDS((1, KVL), F32)],
        scratch_shapes=[pltpu.VMEM((tm, QFW), BF16), pltpu.VMEM((tm, KVW), BF16)],
        compiler_params=_cp("arbitrary"))(dq, dk, dv, z, z, gq, gkv, wq, wkv, tc, ts1, ts2)


def _bwd_in(pieces, x, g, dres, w, tm):
    T = x.shape[0]
    widths = [pc.shape[1] for pc in pieces]
    assert sum(widths) == ZW
    n_p = len(pieces)

    def body(*refs):
        p_refs, (x_ref, g_ref, r_ref, w_ref, dx_ref, dz_ref, dg_ref) = refs[:n_p], refs[n_p:]

        @pl.when(pl.program_id(0) == 0)
        def _():
            dg_ref[...] = jnp.zeros(dg_ref.shape, F32)

        off = 0
        for ref, wd in zip(p_refs, widths):
            dz_ref[:, off:off + wd] = ref[...].astype(BF16)
            off += wd
        xf, gf = x_ref[...], g_ref[...]
        r = _rstd(xf)
        n = xf * r
        dx, dgr = _norm_bwd(_nt(dz_ref[...], w_ref[...]), n, r, gf)
        dx_ref[...] = r_ref[...] + dx
        dg_ref[...] += jnp.sum(dgr, axis=0, keepdims=True)

    return pl.pallas_call(
        body, grid=(T // tm,), name="bwd_in",
        in_specs=[_row(tm, wd) for wd in widths] + [_row(tm, D), _res((1, D)), _row(tm, D), _res((D, ZW))],
        out_specs=[_row(tm, D), _row(tm, ZW), _acc((1, D))],
        out_shape=[SDS((T, D), F32), SDS((T, ZW), BF16), SDS((1, D), F32)],
        compiler_params=_cp("arbitrary"))(*pieces, x, g, dres, w)


def _wgrad_in(hb, dzb, tm):
    T = hb.shape[0]
    half = ZW // 2

    def body(h_ref, dz_ref, dw_ref):
        @pl.when(pl.program_id(1) == 0)
        def _():
            dw_ref[...] = jnp.zeros(dw_ref.shape, F32)

        dw_ref[...] += _tn(h_ref[...], dz_ref[...])

    return pl.pallas_call(
        body, grid=(2, T // tm), name="wgrad_in",
        in_specs=[pl.BlockSpec((tm, D), lambda j, t: (t, 0)), pl.BlockSpec((tm, half), lambda j, t: (t, j))],
        out_specs=pl.BlockSpec((D, half), lambda j, t: (0, j)),
        out_shape=SDS((D, ZW), F32),
        compiler_params=_cp("parallel", "arbitrary"))(hb, dzb)


def _win_to_kernel(w):
    s = lambda a, b: w[..., a:b]
    zero = lambda n: jnp.zeros(w.shape[:-1] + (n,), w.dtype)
    return jnp.concatenate([s(2208, 3232), s(3232, 4256), s(0, 512), s(768, 1280), s(1696, 2208), s(1280, 1536),
                            s(512, 640), s(640, 768), s(1536, 1664), zero(64), s(1664, 1696), zero(32)], axis=-1)


def _win_from_kernel(g):
    s = lambda a, n: g[..., a:a + n]
    return jnp.concatenate([s(Z_AQ, 512), s(Z_AK, 128), s(Z_AV, 128), s(Z_AG, 512), s(Z_QD, 256), s(Z_KVD, 128),
                            s(Z_KR + MLA_NOPE, MLA_ROPE), s(Z_BG, 512), s(Z_MA, 1024), s(Z_MB, 1024)], axis=-1)


def _wuq_to_kernel(w):
    w = w.reshape(w.shape[:-1] + (MLA_H, MLA_QK))
    w = jnp.pad(w, [(0, 0)] * (w.ndim - 1) + [(0, LANES - MLA_QK)])
    return w.reshape(w.shape[:-2] + (QFW,))


def _wuq_from_kernel(g):
    g = g.reshape(g.shape[:-1] + (MLA_H, LANES))[..., :MLA_QK]
    return g.reshape(g.shape[:-2] + (MLA_H * MLA_QK,))


def _wukv_to_kernel(w):
    w = w.reshape(w.shape[:-1] + (MLA_H, MLA_NOPE + MLA_V))
    k = jnp.pad(w[..., :MLA_NOPE], [(0, 0)] * (w.ndim - 1) + [(0, LANES - MLA_NOPE)])
    v = w[..., MLA_NOPE:]
    return jnp.concatenate([k.reshape(k.shape[:-2] + (QFW,)), v.reshape(v.shape[:-2] + (MLA_H * MLA_V,))], axis=-1)


def _wukv_from_kernel(g):
    k = g[..., :QFW].reshape(g.shape[:-1] + (MLA_H, LANES))[..., :MLA_NOPE]
    v = g[..., QFW:].reshape(g.shape[:-1] + (MLA_H, MLA_V))
    kv = jnp.concatenate([k, v], axis=-1)
    return kv.reshape(kv.shape[:-2] + (MLA_H * (MLA_NOPE + MLA_V),))


def _rope_tables(pos):
    half = MLA_ROPE // 2
    inv = 10000.0 ** (-jnp.arange(0, MLA_ROPE, 2, dtype=F32) / MLA_ROPE)
    ang = pos.astype(F32)[:, None] * inv
    cos, sin = jnp.cos(ang), jnp.sin(ang)
    one = jnp.ones((pos.shape[0], MLA_NOPE), F32)
    zero = lambda n: jnp.zeros((pos.shape[0], n), F32)
    tc = jnp.concatenate([one, cos, cos, one[:, :LANES - MLA_QK]], axis=1)
    ts1 = jnp.concatenate([zero(MLA_NOPE + half), sin, zero(LANES - MLA_QK)], axis=1)
    ts2 = jnp.concatenate([zero(MLA_NOPE), -sin, zero(LANES - MLA_NOPE - half)], axis=1)
    return tc, ts1, ts2


def _local_step(x, p, positions, loss_target, small, wts):
    nb, seq, _ = x.shape
    T = nb * seq
    tm = min(256, T)
    tq = min(512, seq)
    xf = x.reshape(T, D)
    pos = positions.reshape(T)
    posf = pos.astype(F32)
    pos_col, pos_row = posf.reshape(T, 1), posf.reshape(T // BLK, 1, BLK)
    tc, ts1, ts2 = _rope_tables(pos)

    saved = []
    for i in range(DEPTH):
        w, sm = wts[i], small[i]
        z, hb = _fwd_in(xf, sm["g_mix"], w["w_in"], tm)
        oa, lse_a = _swa_fwd(sm["sink"], z, pos_col, pos_row, nb, seq)
        qf, kf, v = _fwd_prep(z, sm["g_q"], sm["g_kv"], w["w_uq"], w["w_ukv"], tc, ts1, ts2, tm)
        ob, lse_b = _mla_fwd(qf, kf, v, nb, seq, tq)
        x1, ua, ub = _fwd_merge(xf, oa, ob, z, w["w_br_a"], w["w_br_b"], w["w_out"], tm)
        x2, pg, pp = _fwd_ple(x1, p[i].reshape(T, PLE), sm["g_ple"], w["w_ple_gate"], w["w_ple_proj"], tm)
        saved.append(dict(x=xf, z=z, hb=hb, oa=oa, lse_a=lse_a, qf=qf, kf=kf, v=v, ob=ob, lse_b=lse_b,
                          x1=x1, ua=ua, ub=ub, pg=pg, pp=pp))
        xf = x2

    dx, dg_final, loss = _loss_head(xf, small["g_final"], loss_target.reshape(T, D), tm)

    grads = [None] * DEPTH
    for i in reversed(range(DEPTH)):
        w, sm, sv = wts[i], small[i], saved[i]
        dx1, dwpg, dwpp, dg_ple = _bwd_ple(dx, sv["x1"], sv["pg"], sv["pp"], p[i].reshape(T, PLE), sm["g_ple"],
                                           w["w_ple_gate"], tm)
        doa, dob, dag, dbg, dma, dmb, dsum_b, dwa, dwb, dwo = _bwd_merge(
            dx1, sv["oa"], sv["ob"], sv["z"], sv["ua"], sv["ub"], w["w_br_a"], w["w_br_b"], w["w_out"], tm)
        dq_b, dk_b, dv_b = _mla_bwd(sv["qf"], sv["kf"], sv["v"], dob, sv["lse_b"],
                                    dsum_b.reshape(MLA_H // 2, 2, T), nb, seq, tq)
        dqd, dkvd, dkr, dwq, dwkv, dgq, dgkv = _bwd_prep(dq_b, dk_b, dv_b, sv["z"], sm["g_q"], sm["g_kv"],
                                                         w["w_uq"], w["w_ukv"], tc, ts1, ts2, tm)
        dq_a, dkv_a, dsink = _swa_bwd(sm["sink"], sv["z"], pos_col, pos_row, doa, sv["oa"], sv["lse_a"], nb, seq)
        dx, dzb, dg_mix = _bwd_in([dma, dmb, dq_a, dag, dbg, dqd, dkv_a, dkvd, dkr], sv["x"], sm["g_mix"], dx1,
                                  w["w_in"], tm)
        dwin = _wgrad_in(sv["hb"], dzb, tm)
        grads[i] = dict(g_mix=dg_mix[0], w_in=_win_from_kernel(dwin), sink=dsink[0, :SWA_H], g_q=dgq[0],
                        w_uq=_wuq_from_kernel(dwq), g_kv=dgkv[0], w_ukv=_wukv_from_kernel(dwkv), w_br_a=dwa,
                        w_br_b=dwb, w_out=dwo, g_ple=dg_ple[0], w_ple_gate=dwpg, w_ple_proj=dwpp)
    return loss, dx.reshape(nb, seq, D), grads, dg_final[0]


def _kernel_weights(full):
    wi, wq, wkv = _win_to_kernel(full["w_in"]), _wuq_to_kernel(full["w_uq"]), _wukv_to_kernel(full["w_ukv"])
    return [dict(w_in=wi[i], w_uq=wq[i], w_ukv=wkv[i], w_br_a=full["w_br_a"][i], w_br_b=full["w_br_b"][i],
                 w_out=full["w_out"][i], w_ple_gate=full["w_ple_gate"][i], w_ple_proj=full["w_ple_proj"][i])
            for i in range(DEPTH)]


def _small_params(g_mix, sink, g_q, g_kv, g_ple, g_final):
    small = {i: dict(g_mix=g_mix[i][None], sink=sink[i], g_q=g_q[i][None], g_kv=g_kv[i][None], g_ple=g_ple[i][None])
             for i in range(DEPTH)}
    small["g_final"] = g_final[None]
    return small


WIDE_ROWS, WIDE_W = DEPTH * D, IN_W // N_DEV
UQ_W = MLA_H * MLA_QK // N_DEV
ROWS_PIECES = (("w_uq", DEPTH * QL), ("w_ukv", DEPTH * KVL), ("w_br_a", DEPTH * 512), ("w_br_b", DEPTH * 512),
               ("w_out", DEPTH * D), ("w_ple_gate", DEPTH * D), ("w_ple_proj", DEPTH * PLE))
SMALL = (("g_mix", (DEPTH, D)), ("sink", (DEPTH, SWA_H)), ("g_q", (DEPTH, QL)), ("g_kv", (DEPTH, KVL)),
         ("g_ple", (DEPTH, D)), ("g_final", (D,)))
SMALL_ROWS = 48
ROWS_N = 7680
WIDE_TILE, ROWS_TILE = 256, 1920


def _to_rows(name, a):
    lead = a.shape[:-3]
    if name == "w_uq":
        a = jnp.pad(a, [(0, 0)] * (a.ndim - 1) + [(0, LANES - UQ_W)])
    return a.reshape(lead + (-1, LANES))


def _from_rows(name, r):
    lead = r.shape[:-2]
    if name in ("w_out", "w_ple_gate"):
        return r.reshape(lead + (DEPTH, D // N_DEV, D))
    r = r.reshape(lead + (DEPTH, -1, LANES))
    return r[..., :UQ_W] if name == "w_uq" else r


def _pack_rows(blocks, vectors):
    lead = blocks["w_uq"].shape[:-3]
    vec = jnp.concatenate([vectors[n].reshape(-1) for n, _ in SMALL])
    vec = jnp.pad(vec, (0, SMALL_ROWS * LANES - vec.shape[0])).reshape(SMALL_ROWS, LANES)
    parts = [_to_rows(n, blocks[n]) for n, _ in ROWS_PIECES] + [jnp.broadcast_to(vec, lead + vec.shape)]
    used = sum(r for _, r in ROWS_PIECES) + SMALL_ROWS
    parts.append(jnp.zeros(lead + (ROWS_N - used, LANES), vec.dtype))
    return jnp.concatenate(parts, axis=-2)


def _unpack_rows(rows):
    blocks, off = {}, 0
    for n, r in ROWS_PIECES:
        blocks[n] = _from_rows(n, rows[..., off:off + r, :])
        off += r
    vec = rows[..., off:off + SMALL_ROWS, :].reshape(rows.shape[:-2] + (-1,))
    vectors, voff = {}, 0
    for n, shp in SMALL:
        size = 1
        for s in shp:
            size *= s
        vectors[n] = vec[..., voff:voff + size].reshape(rows.shape[:-2] + shp)
        voff += size
    return blocks, vectors


def _join(name, blocks):
    if name in ("w_out", "w_ple_gate"):
        return jnp.moveaxis(blocks, 0, 1).reshape(DEPTH, -1, blocks.shape[-1])
    return jnp.moveaxis(blocks, 0, 2).reshape(DEPTH, blocks.shape[2], -1)


def _split(name, full):
    if name in ("w_out", "w_ple_gate"):
        return jnp.moveaxis(full.reshape(DEPTH, N_DEV, -1, full.shape[-1]), 1, 0)
    return jnp.moveaxis(full.reshape(DEPTH, full.shape[1], N_DEV, -1), 2, 0)


MESH_ID = pl.DeviceIdType.MESH
ANY = pl.BlockSpec(memory_space=pl.ANY)


def _place():
    return lax.axis_index("x"), lax.axis_index("y"), lax.axis_index("c")


def _all_gather(blocks):
    n = len(blocks)

    def body(*refs):
        x_refs, out_refs, (send_sems, recv_sems, local_sems) = refs[:n], refs[n:2 * n], refs[2 * n:]
        x, y, c = _place()
        me, sibling = (x, y, c), (x, y, 1 - c)
        chips = [(1 - x, y), (x, 1 - y), (1 - x, 1 - y)]

        def slot(a, px, py, pc):
            return out_refs[a].at[4 * px + 2 * py + pc]

        def copy(a, k, blk, to, src=None):
            return pltpu.make_async_remote_copy(
                src_ref=slot(a, *blk) if src is None else src, dst_ref=slot(a, *blk),
                send_sem=send_sems.at[7 * a + k], recv_sem=recv_sems.at[7 * a + k], device_id=to,
                device_id_type=MESH_ID)

        mine = [pltpu.make_async_copy(x_refs[a], slot(a, *me), local_sems.at[a]) for a in range(n)]
        for cp in mine:
            cp.start()
        first = []
        for a in range(n):
            first += [copy(a, 0, me, sibling, src=x_refs[a])]
            first += [copy(a, 1 + j, me, (*chip, c), src=x_refs[a]) for j, chip in enumerate(chips)]
        for cp in first:
            cp.start()
        passed = []
        for j, chip in enumerate(chips):
            for a in range(n):
                copy(a, 1 + j, (*chip, c), me).wait_recv()
                passed.append(copy(a, 4 + j, (*chip, c), sibling))
                passed[-1].start()
        for a in range(n):
            copy(a, 0, sibling, me).wait_recv()
            for j, chip in enumerate(chips):
                copy(a, 4 + j, (*chip, 1 - c), me).wait_recv()
        for cp in first + passed:
            cp.wait_send()
        for cp in mine:
            cp.wait()

    return pl.pallas_call(
        body, name="all_gather_weights",
        out_shape=[SDS((N_DEV,) + b.shape, b.dtype) for b in blocks],
        in_specs=[ANY] * n, out_specs=[ANY] * n,
        scratch_shapes=[pltpu.SemaphoreType.DMA((7 * n,)), pltpu.SemaphoreType.DMA((7 * n,)),
                        pltpu.SemaphoreType.DMA((n,))],
    )(*blocks)


def _swap_sibling(arrs):
    n = len(arrs)

    def body(*refs):
        a_refs, out_refs, (send_sems, recv_sems) = refs[:n], refs[n:2 * n], refs[2 * n:]
        x, y, c = _place()
        copies = [pltpu.make_async_remote_copy(
            src_ref=a_refs[a].at[:, 1 - c], dst_ref=out_refs[a], send_sem=send_sems.at[a], recv_sem=recv_sems.at[a],
            device_id=(x, y, 1 - c), device_id_type=MESH_ID) for a in range(n)]
        for cp in copies:
            cp.start()
        for cp in copies:
            cp.wait()

    return pl.pallas_call(
        body, name="swap_sibling", out_shape=[SDS((a.shape[0],) + a.shape[2:], a.dtype) for a in arrs],
        in_specs=[ANY] * n, out_specs=[ANY] * n,
        scratch_shapes=[pltpu.SemaphoreType.DMA((n,)), pltpu.SemaphoreType.DMA((n,))],
    )(*arrs)


def _exchange_chips(arrs):
    n = len(arrs)

    def body(*refs):
        p_refs, out_refs, (send_sems, recv_sems, local_sems) = refs[:n], refs[n:2 * n], refs[2 * n:]
        x, y, c = _place()
        mine = 2 * x + y
        local = [pltpu.make_async_copy(p_refs[a].at[mine], out_refs[a].at[mine], local_sems.at[a]) for a in range(n)]
        for cp in local:
            cp.start()
        peers = [(1 - x, y), (x, 1 - y), (1 - x, 1 - y)]

        def copy(a, j, src_chip, dst_chip):
            px, py = peers[j]
            return pltpu.make_async_remote_copy(
                src_ref=p_refs[a].at[src_chip], dst_ref=out_refs[a].at[dst_chip], send_sem=send_sems.at[3 * a + j],
                recv_sem=recv_sems.at[3 * a + j], device_id=(px, py, c), device_id_type=MESH_ID)

        copies = [copy(a, j, 2 * px + py, mine) for a in range(n) for j, (px, py) in enumerate(peers)]
        for cp in copies:
            cp.start()
        for a in range(n):
            for j, (px, py) in enumerate(peers):
                copy(a, j, mine, 2 * px + py).wait_recv()
        for cp in copies:
            cp.wait_send()
        for cp in local:
            cp.wait()

    return pl.pallas_call(
        body, name="exchange_chips", out_shape=[SDS(a.shape, a.dtype) for a in arrs],
        in_specs=[ANY] * n, out_specs=[ANY] * n,
        scratch_shapes=[pltpu.SemaphoreType.DMA((3 * n,)), pltpu.SemaphoreType.DMA((3 * n,)),
                        pltpu.SemaphoreType.DMA((n,))],
    )(*arrs)


def _add_mine(g, recv, core, tile):
    _, _, rows, width = g.shape

    def body(c_ref, g_ref, r_ref, o_ref):
        o_ref[...] = g_ref[...] + r_ref[...]

    spec = pl.BlockSpec((None, tile, width), lambda k, i, c_ref: (k, i, 0))
    return pl.pallas_call(
        body, name="add_sibling", out_shape=SDS(recv.shape, F32),
        grid_spec=pltpu.PrefetchScalarGridSpec(
            num_scalar_prefetch=1, grid=(g.shape[0], rows // tile),
            in_specs=[pl.BlockSpec((None, None, tile, width), lambda k, i, c_ref: (k, c_ref[0], i, 0)), spec],
            out_specs=spec),
        compiler_params=_cp("parallel", "parallel"))(core, g, recv)


def _sum_adamw(parts, w, m, v, tile):
    rows, width = w.shape

    def body(p_ref, w_ref, m_ref, v_ref, g_ref, d_ref, nm_ref, nv_ref):
        g = ((p_ref[0] + p_ref[1]) + p_ref[2]) + p_ref[3]
        nm = ADAM_B1 * m_ref[...] + (1.0 - ADAM_B1) * g
        nv = ADAM_B2 * v_ref[...] + (1.0 - ADAM_B2) * jnp.square(g)
        m_hat = nm / (1.0 - ADAM_B1 ** ADAM_STEP)
        v_hat = nv / (1.0 - ADAM_B2 ** ADAM_STEP)
        g_ref[...] = g
        nm_ref[...] = nm
        nv_ref[...] = nv
        d_ref[...] = -ADAM_LR * (m_hat / (jnp.sqrt(v_hat) + ADAM_EPS) + ADAM_WD * w_ref[...])

    spec = pl.BlockSpec((tile, width), lambda i: (i, 0))
    return pl.pallas_call(
        body, grid=(rows // tile,), name="sum_adamw",
        in_specs=[pl.BlockSpec((4, tile, width), lambda i: (0, i, 0)), spec, spec, spec],
        out_specs=[spec] * 4, out_shape=[SDS((rows, width), F32)] * 4,
        compiler_params=_cp("parallel"))(parts, w, m, v)


def kernel(x, p, positions, g_mix, w_in, sink, g_q, w_uq, g_kv, w_ukv, w_br_a, w_br_b, w_out, g_ple, w_ple_gate, w_ple_proj, g_final, loss_target, m_g_mix, m_w_in, m_sink, m_g_q, m_w_uq, m_g_kv, m_w_ukv, m_w_br_a, m_w_br_b, m_w_out, m_g_ple, m_w_ple_gate, m_w_ple_proj, m_g_final, v_g_mix, v_w_in, v_sink, v_g_q, v_w_uq, v_g_kv, v_w_ukv, v_w_br_a, v_w_br_b, v_w_out, v_g_ple, v_w_ple_gate, v_w_ple_proj, v_g_final):
    weights = dict(g_mix=g_mix, w_in=w_in, sink=sink, g_q=g_q, w_uq=w_uq, g_kv=g_kv, w_ukv=w_ukv, w_br_a=w_br_a,
                   w_br_b=w_br_b, w_out=w_out, g_ple=g_ple, w_ple_gate=w_ple_gate, w_ple_proj=w_ple_proj,
                   g_final=g_final)
    mom1 = dict(g_mix=m_g_mix, w_in=m_w_in, sink=m_sink, g_q=m_g_q, w_uq=m_w_uq, g_kv=m_g_kv, w_ukv=m_w_ukv,
                w_br_a=m_w_br_a, w_br_b=m_w_br_b, w_out=m_w_out, g_ple=m_g_ple, w_ple_gate=m_w_ple_gate,
                w_ple_proj=m_w_ple_proj, g_final=m_g_final)
    mom2 = dict(g_mix=v_g_mix, w_in=v_w_in, sink=v_sink, g_q=v_g_q, w_uq=v_w_uq, g_kv=v_g_kv, w_ukv=v_w_ukv,
                w_br_a=v_w_br_a, w_br_b=v_w_br_b, w_out=v_w_out, g_ple=v_g_ple, w_ple_gate=v_w_ple_gate,
                w_ple_proj=v_w_ple_proj, g_final=v_g_final)
    wide = lambda d: d["w_in"].reshape(WIDE_ROWS, WIDE_W)
    rows = lambda d: _pack_rows({n: d[n] for n, _ in ROWS_PIECES}, {n: d[n] for n, _ in SMALL})

    got_wide, got_rows = _all_gather([wide(weights).astype(BF16), rows(weights).astype(BF16)])
    blocks, _ = _unpack_rows(got_rows)
    full = {n: _join(n, blocks[n]) for n, _ in ROWS_PIECES}
    full["w_in"] = _join("w_in", got_wide.reshape(N_DEV, DEPTH, D, WIDE_W))
    wts = _kernel_weights(full)
    small = _small_params(g_mix, sink, g_q, g_kv, g_ple, g_final)

    loss, grad_x, grads, dg_final = _local_step(x, p, positions, loss_target, small, wts)
    loss = lax.psum(loss[0, 0], ("x", "y", "c"))

    stacked = {n: jnp.stack([grads[i][n] for i in range(DEPTH)]) for n in grads[0]}
    stacked["g_final"] = dg_final
    pay_wide = _split("w_in", stacked["w_in"]).reshape(N_DEV // 2, 2, WIDE_ROWS, WIDE_W)
    pay_rows = _pack_rows({n: _split(n, stacked[n]) for n, _ in ROWS_PIECES}, stacked)
    pay_rows = pay_rows.reshape(N_DEV // 2, 2, ROWS_N, LANES)
    core = lax.axis_index("c").astype(jnp.int32).reshape(1)
    from_sibling = _swap_sibling([pay_wide, pay_rows])
    chip_partial = [_add_mine(pay_wide, from_sibling[0], core, WIDE_TILE),
                    _add_mine(pay_rows, from_sibling[1], core, ROWS_TILE)]
    parts_wide, parts_rows = _exchange_chips(chip_partial)
    out_wide = _sum_adamw(parts_wide, wide(weights), wide(mom1), wide(mom2), WIDE_TILE)
    out_rows = _sum_adamw(parts_rows, rows(weights), rows(mom1), rows(mom2), ROWS_TILE)

    outs = []
    for ow, orow in zip(out_wide, out_rows):
        named, vectors = _unpack_rows(orow)
        named.update(vectors)
        named["w_in"] = ow.reshape(DEPTH, D, WIDE_W)
        outs += [named[n] for n in weights]
    return (loss, grad_x, *outs)
```

```python
import functools

import jax
import jax.numpy as jnp
from jax import lax
from jax.experimental import pallas as pl
from jax.experimental.pallas import tpu as pltpu

F32, BF16 = jnp.float32, jnp.bfloat16
SDS = jax.ShapeDtypeStruct

D = 1024
DEPTH = 2
PLE = 256
BLK = 128
EPS = 1e-6
NEG = -1e30
SWA_H, SWA_KV, SWA_DH = 8, 2, 64
MLA_H, MLA_NOPE, MLA_ROPE, MLA_V = 8, 64, 32, 64
MLA_QK = MLA_NOPE + MLA_ROPE
QL, KVL = 256, 128
IN_W = 4256
N_DEV = 8

V7X_VMEM_BYTES = 64 * 1024 * 1024
LANES = 128
VMEM_LIMIT = V7X_VMEM_BYTES * 7 // 8

ZW = 4352
Z_MA, Z_MB, Z_AQ, Z_AG, Z_BG, Z_QD, Z_AK, Z_AV, Z_KVD, Z_KR = 0, 1024, 2048, 2560, 3072, 3584, 3840, 3968, 4096, 4224
QFW = MLA_H * LANES
KVW = QFW + MLA_H * MLA_V
MLA_SCALE = MLA_QK ** -0.5
SWA_SCALE = SWA_DH ** -0.5
ROLL_UP, ROLL_DOWN = MLA_ROPE // 2, LANES - MLA_ROPE // 2

ADAM_LR, ADAM_B1, ADAM_B2, ADAM_EPS, ADAM_WD, ADAM_STEP = 0.001, 0.9, 0.999, 1e-08, 0.01, 10

FLAT_W = 1024


def _cp(*sem):
    return pltpu.CompilerParams(dimension_semantics=sem, vmem_limit_bytes=VMEM_LIMIT)


def _row(tm, w, col=0):
    return pl.BlockSpec((tm, w), lambda i: (i, col))


def _res(shape):
    return pl.BlockSpec(shape, lambda *_: (0,) * len(shape), pipeline_mode=pl.Buffered(1))


def _acc(shape):
    return pl.BlockSpec(shape, lambda *_: (0,) * len(shape))


def _rstd(xf):
    return lax.rsqrt(jnp.mean(xf * xf, axis=-1, keepdims=True) + EPS)


def _norm_bwd(dh, n, r, g):
    dn = dh * g
    return r * (dn - n * jnp.mean(dn * n, axis=-1, keepdims=True)), dh * n


def _nt(a, b):
    return lax.dot_general(a, b, (((1,), (1,)), ((), ())), preferred_element_type=F32)


def _tn(a, b):
    return lax.dot_general(a, b, (((0,), (0,)), ((), ())), preferred_element_type=F32)


def _nn(a, b):
    return jnp.dot(a, b, preferred_element_type=F32)


def _sig(x):
    return jax.nn.sigmoid(x)


def _rope(t, c, s1, s2):
    return t * c + pltpu.roll(t, ROLL_UP, 1) * s1 + pltpu.roll(t, ROLL_DOWN, 1) * s2


def _rope_t(d, c, s1, s2):
    return d * c + pltpu.roll(d * s1, ROLL_DOWN, 1) + pltpu.roll(d * s2, ROLL_UP, 1)


def _fwd_in(x, g, w, tm):
    T = x.shape[0]

    def body(x_ref, g_ref, w_ref, z_ref, h_ref):
        xf = x_ref[...]
        h = ((xf * _rstd(xf)) * g_ref[...]).astype(BF16)
        h_ref[...] = h
        z_ref[...] = _nn(h, w_ref[...])

    return pl.pallas_call(
        body, grid=(T // tm,), name="fwd_in",
        in_specs=[_row(tm, D), _res((1, D)), _res((D, ZW))],
        out_specs=[_row(tm, ZW), _row(tm, D)],
        out_shape=[SDS((T, ZW), F32), SDS((T, D), BF16)],
        compiler_params=_cp("parallel"))(x, g, w)


def _fwd_prep(z, gq, gkv, wq, wkv, tc, ts1, ts2, tm):
    T = z.shape[0]

    def body(qd_ref, kvd_ref, kr_ref, gq_ref, gkv_ref, wq_ref, wkv_ref, c_ref, s1_ref, s2_ref, q_ref, k_ref, v_ref):
        qd, kvd = qd_ref[...], kvd_ref[...]
        hq = ((qd * _rstd(qd)) * gq_ref[...]).astype(BF16)
        hkv = ((kvd * _rstd(kvd)) * gkv_ref[...]).astype(BF16)
        qf = _nn(hq, wq_ref[...])
        kvf = _nn(hkv, wkv_ref[...])
        c, s1, s2 = c_ref[...], s1_ref[...], s2_ref[...]
        krb = _rope(kr_ref[...], c, s1, s2)
        for h in range(MLA_H):
            sl = slice(LANES * h, LANES * (h + 1))
            q_ref[:, sl] = _rope(qf[:, sl], c, s1, s2).astype(BF16)
            k_ref[:, sl] = (kvf[:, sl] + krb).astype(BF16)
        v_ref[...] = kvf[:, QFW:].astype(BF16)

    return pl.pallas_call(
        body, grid=(T // tm,), name="fwd_prep",
        in_specs=[_row(tm, QL, Z_QD // QL), _row(tm, KVL, Z_KVD // KVL), _row(tm, LANES, Z_KR // LANES),
                  _res((1, QL)), _res((1, KVL)), _res((QL, QFW)), _res((KVL, KVW)),
                  _row(tm, LANES), _row(tm, LANES), _row(tm, LANES)],
        out_specs=[_row(tm, QFW), _row(tm, QFW), _row(tm, MLA_H * MLA_V)],
        out_shape=[SDS((T, QFW), BF16), SDS((T, QFW), BF16), SDS((T, MLA_H * MLA_V), BF16)],
        compiler_params=_cp("parallel"))(z, z, z, gq, gkv, wq, wkv, tc, ts1, ts2)


def _mla_fwd(qf, kf, v, nb, seq, tq):
    T = qf.shape[0]
    nq = seq // tq
    pw = 2 * LANES

    def body(q_ref, k_ref, v_ref, o_ref, lse_ref, m_s, l_s, acc_s):
        qi, ki = pl.program_id(2), pl.program_id(3)

        @pl.when(ki == 0)
        def _():
            m_s[...] = jnp.full(m_s.shape, NEG, F32)
            l_s[...] = jnp.zeros(l_s.shape, F32)
            acc_s[...] = jnp.zeros(acc_s.shape, F32)

        def step(masked):
            if masked:
                keys = lax.broadcasted_iota(jnp.int32, (tq, tq), 0)
                queries = lax.broadcasted_iota(jnp.int32, (tq, tq), 1)
                mask = keys <= queries
            for j in range(2):
                wide = slice(LANES * j, LANES * (j + 1))
                s = _nt(k_ref[:, wide], q_ref[:, wide]) * MLA_SCALE
                if masked:
                    s = jnp.where(mask, s, NEG)
                m_prev = m_s[j]
                m_new = jnp.maximum(m_prev, jnp.max(s, axis=0, keepdims=True))
                alpha = jnp.exp(m_prev - m_new)
                p = jnp.exp(s - m_new)
                l_s[j] = alpha * l_s[j] + jnp.sum(p, axis=0, keepdims=True)
                m_s[j] = m_new
                rows = slice(MLA_V * j, MLA_V * (j + 1))
                acc_s[rows, :] = alpha * acc_s[rows, :] + _tn(v_ref[:, rows], p.astype(BF16))

        @pl.when(ki < qi)
        def _():
            step(False)

        @pl.when(ki == qi)
        def _():
            step(True)
            for j in range(2):
                rows = slice(MLA_V * j, MLA_V * (j + 1))
                acc_s[rows, :] = acc_s[rows, :] / l_s[j]
                lse_ref[j:j + 1, :] = m_s[j] + jnp.log(l_s[j])
            o_ref[...] = acc_s[...].T

    kv_map = lambda b, hp, qi, ki: (b * nq + jnp.minimum(ki, qi), hp)
    return pl.pallas_call(
        body, grid=(nb, MLA_H // 2, nq, nq), name="mla_fwd",
        in_specs=[pl.BlockSpec((tq, pw), lambda b, hp, qi, ki: (b * nq + qi, hp)),
                  pl.BlockSpec((tq, pw), kv_map),
                  pl.BlockSpec((tq, LANES), kv_map)],
        out_specs=[pl.BlockSpec((tq, LANES), lambda b, hp, qi, ki: (b * nq + qi, hp)),
                   pl.BlockSpec((None, 2, tq), lambda b, hp, qi, ki: (hp, 0, b * nq + qi))],
        out_shape=[SDS((T, MLA_H * MLA_V), F32), SDS((MLA_H // 2, 2, T), F32)],
        scratch_shapes=[pltpu.VMEM((2, 1, tq), F32), pltpu.VMEM((2, 1, tq), F32), pltpu.VMEM((LANES, tq), F32)],
        compiler_params=_cp("parallel", "parallel", "parallel", "arbitrary"))(qf, kf, v)


def _swa_specs(nblk):
    cur = lambda b, n: (b * nblk + n, 0)
    prev = lambda b, n: (b * nblk + jnp.maximum(n - 1, 0), 0)
    kvc = Z_AK // (2 * BLK)
    return [pl.BlockSpec(memory_space=pltpu.SMEM),
            pl.BlockSpec((BLK, 512), lambda b, n: (b * nblk + n, Z_AQ // 512)),
            pl.BlockSpec((BLK, 2 * BLK), lambda b, n: (b * nblk + n, kvc)),
            pl.BlockSpec((BLK, 2 * BLK), lambda b, n: (b * nblk + jnp.maximum(n - 1, 0), kvc)),
            pl.BlockSpec((BLK, 1), cur),
            pl.BlockSpec((BLK, 1), prev),
            pl.BlockSpec((1, 1, BLK), lambda b, n: (b * nblk + n, 0, 0))]


def _swa_scores(n, q_ref, kvc_ref, kvp_ref, pcc_ref, pcp_ref, pr_ref):
    kv = jnp.concatenate([kvp_ref[...], kvc_ref[...]], axis=0)
    kb, vb = kv[:, :BLK].astype(BF16), kv[:, BLK:].astype(BF16)
    dist = pr_ref[0] - jnp.concatenate([pcp_ref[...], pcc_ref[...]], axis=0)
    key = lax.broadcasted_iota(jnp.int32, (2 * BLK, BLK), 0)
    qry = lax.broadcasted_iota(jnp.int32, (2 * BLK, BLK), 1)
    valid = (key > qry) & (key <= qry + BLK) & ((key >= BLK) | (n > 0))

    def scores(h):
        g = h // (SWA_H // SWA_KV)
        qh = q_ref[:, SWA_DH * h:SWA_DH * (h + 1)].astype(BF16)
        s = _nt(kb[:, SWA_DH * g:SWA_DH * (g + 1)], qh) * SWA_SCALE - (2.0 ** -(h + 1)) * dist
        return qh, jnp.where(valid, s, NEG)

    return kb, vb, scores


def _swa_fwd(sink, z, pos_col, pos_row, nb, seq):
    T = z.shape[0]
    nblk = seq // BLK

    def body(sink_ref, q_ref, kvc_ref, kvp_ref, pcc_ref, pcp_ref, pr_ref, o_ref, lse_ref):
        kb, vb, scores = _swa_scores(pl.program_id(1), q_ref, kvc_ref, kvp_ref, pcc_ref, pcp_ref, pr_ref)
        outs = []
        for h in range(SWA_H):
            g = h // (SWA_H // SWA_KV)
            _, s = scores(h)
            sk = sink_ref[h]
            m = jnp.maximum(jnp.max(s, axis=0, keepdims=True), sk)
            e = jnp.exp(s - m)
            den = jnp.sum(e, axis=0, keepdims=True) + jnp.exp(sk - m)
            outs.append(_tn(vb[:, SWA_DH * g:SWA_DH * (g + 1)], e.astype(BF16)) / den)
            lse_ref[h:h + 1, :] = m + jnp.log(den)
        o_ref[...] = jnp.concatenate(outs, axis=0).T

    return pl.pallas_call(
        body, grid=(nb, nblk), name="swa_fwd",
        in_specs=_swa_specs(nblk),
        out_specs=[pl.BlockSpec((BLK, 512), lambda b, n: (b * nblk + n, 0)),
                   pl.BlockSpec((SWA_H, BLK), lambda b, n: (0, b * nblk + n))],
        out_shape=[SDS((T, 512), F32), SDS((SWA_H, T), F32)],
        compiler_params=_cp("parallel", "parallel"))(sink, z, z, z, pos_col, pos_col, pos_row)


def _fwd_merge(x, oa, ob, z, wa, wb, wo, tm):
    T = x.shape[0]

    def body(x_ref, oa_ref, ob_ref, ag_ref, bg_ref, ma_ref, mb_ref, wa_ref, wb_ref, wo_ref, x1_ref, ua_ref, ub_ref):
        ag, bg = ag_ref[...], bg_ref[...]
        ua = _nn((oa_ref[...] * (ag * _sig(ag))).astype(BF16), wa_ref[...])
        ub = _nn((ob_ref[...] * (bg * _sig(bg))).astype(BF16), wb_ref[...])
        ua_ref[...] = ua
        ub_ref[...] = ub
        y = _sig(ma_ref[...]) * ua + _sig(mb_ref[...]) * ub
        x1_ref[...] = x_ref[...] + _nn(y.astype(BF16), wo_ref[...])

    return pl.pallas_call(
        body, grid=(T // tm,), name="fwd_merge",
        in_specs=[_row(tm, D), _row(tm, 512), _row(tm, 512), _row(tm, 512, Z_AG // 512), _row(tm, 512, Z_BG // 512),
                  _row(tm, D, Z_MA // D), _row(tm, D, Z_MB // D), _res((512, D)), _res((512, D)), _res((D, D))],
        out_specs=[_row(tm, D)] * 3,
        out_shape=[SDS((T, D), F32)] * 3,
        compiler_params=_cp("parallel"))(x, oa, ob, z, z, z, z, wa, wb, wo)


def _fwd_ple(x1, p, g, wpg, wpp, tm):
    T = x1.shape[0]

    def body(x_ref, p_ref, g_ref, wpg_ref, wpp_ref, x2_ref, pg_ref, pp_ref):
        xf = x_ref[...]
        h1 = ((xf * _rstd(xf)) * g_ref[...]).astype(BF16)
        pg = _sig(_nn(h1, wpg_ref[...]))
        pp = _nn(p_ref[...].astype(BF16), wpp_ref[...])
        pg_ref[...] = pg
        pp_ref[...] = pp
        x2_ref[...] = xf + pg * pp

    return pl.pallas_call(
        body, grid=(T // tm,), name="fwd_ple",
        in_specs=[_row(tm, D), _row(tm, PLE), _res((1, D)), _res((D, D)), _res((PLE, D))],
        out_specs=[_row(tm, D)] * 3,
        out_shape=[SDS((T, D), F32)] * 3,
        compiler_params=_cp("parallel"))(x1, p, g, wpg, wpp)


def _loss_head(x, g, tgt, tm):
    T = x.shape[0]

    def body(x_ref, g_ref, t_ref, dx_ref, dg_ref, loss_ref):
        @pl.when(pl.program_id(0) == 0)
        def _():
            dg_ref[...] = jnp.zeros(dg_ref.shape, F32)
            loss_ref[...] = jnp.zeros(loss_ref.shape, F32)

        xf, gf = x_ref[...], g_ref[...]
        r = _rstd(xf)
        n = xf * r
        err = n * gf - t_ref[...]
        loss_ref[...] += 0.5 * jnp.sum(jnp.mean(err * err, axis=-1, keepdims=True), axis=0, keepdims=True)
        dx, dgr = _norm_bwd(err * (1.0 / D), n, r, gf)
        dx_ref[...] = dx
        dg_ref[...] += jnp.sum(dgr, axis=0, keepdims=True)

    return pl.pallas_call(
        body, grid=(T // tm,), name="loss_head",
        in_specs=[_row(tm, D), _res((1, D)), _row(tm, D)],
        out_specs=[_row(tm, D), _acc((1, D)), _acc((1, LANES))],
        out_shape=[SDS((T, D), F32), SDS((1, D), F32), SDS((1, LANES), F32)],
        compiler_params=_cp("arbitrary"))(x, g, tgt)


def _bwd_ple(dx2, x1, pg, pp, p, g, wpg, tm):
    T = x1.shape[0]

    def body(d_ref, x_ref, pg_ref, pp_ref, p_ref, g_ref, w_ref, dx_ref, dwg_ref, dwp_ref, dg_ref):
        @pl.when(pl.program_id(0) == 0)
        def _():
            dwg_ref[...] = jnp.zeros(dwg_ref.shape, F32)
            dwp_ref[...] = jnp.zeros(dwp_ref.shape, F32)
            dg_ref[...] = jnp.zeros(dg_ref.shape, F32)

        d, xf, pg, gf = d_ref[...], x_ref[...], pg_ref[...], g_ref[...]
        r = _rstd(xf)
        n = xf * r
        dpgl = (d * pp_ref[...] * pg * (1.0 - pg)).astype(BF16)
        dwg_ref[...] += _tn((n * gf).astype(BF16), dpgl)
        dwp_ref[...] += _tn(p_ref[...].astype(BF16), (d * pg).astype(BF16))
        dxn, dgr = _norm_bwd(_nt(dpgl, w_ref[...]), n, r, gf)
        dx_ref[...] = d + dxn
        dg_ref[...] += jnp.sum(dgr, axis=0, keepdims=True)

    return pl.pallas_call(
        body, grid=(T // tm,), name="bwd_ple",
        in_specs=[_row(tm, D)] * 4 + [_row(tm, PLE), _res((1, D)), _res((D, D))],
        out_specs=[_row(tm, D), _acc((D, D)), _acc((PLE, D)), _acc((1, D))],
        out_shape=[SDS((T, D), F32), SDS((D, D), F32), SDS((PLE, D), F32), SDS((1, D), F32)],
        compiler_params=_cp("arbitrary"))(dx2, x1, pg, pp, p, g, wpg)


def _bwd_merge(dx1, oa, ob, z, ua, ub, wa, wb, wo, tm):
    T = dx1.shape[0]

    def body(d_ref, oa_ref, ob_ref, ag_ref, bg_ref, ma_ref, mb_ref, ua_ref, ub_ref, wa_ref, wb_ref, wo_ref,
             doa_ref, dob_ref, dag_ref, dbg_ref, dma_ref, dmb_ref, dsa_ref, dsb_ref, dwa_ref, dwb_ref, dwo_ref):
        @pl.when(pl.program_id(0) == 0)
        def _():
            dwa_ref[...] = jnp.zeros(dwa_ref.shape, F32)
            dwb_ref[...] = jnp.zeros(dwb_ref.shape, F32)
            dwo_ref[...] = jnp.zeros(dwo_ref.shape, F32)

        db = d_ref[...].astype(BF16)
        ua, ub = ua_ref[...], ub_ref[...]
        sa, sb = _sig(ma_ref[...]), _sig(mb_ref[...])
        dwo_ref[...] += _tn((sa * ua + sb * ub).astype(BF16), db)
        dy = _nt(db, wo_ref[...])
        dma_ref[...] = (dy * ua * sa * (1.0 - sa)).astype(BF16)
        dmb_ref[...] = (dy * ub * sb * (1.0 - sb)).astype(BF16)
        head = lax.broadcasted_iota(jnp.int32, (MLA_H, 512), 0)
        col = lax.broadcasted_iota(jnp.int32, (MLA_H, 512), 1)
        pick = jnp.where(col // MLA_V == head, 1.0, 0.0).astype(F32)
        for (o_ref, gate_ref, s, w_ref, do_ref, dgate_ref, dw_ref, ds_ref) in (
                (oa_ref, ag_ref, sa, wa_ref, doa_ref, dag_ref, dwa_ref, dsa_ref),
                (ob_ref, bg_ref, sb, wb_ref, dob_ref, dbg_ref, dwb_ref, dsb_ref)):
            du = (dy * s).astype(BF16)
            raw, gate = o_ref[...], gate_ref[...]
            sg = _sig(gate)
            silu = gate * sg
            dw_ref[...] += _tn((raw * silu).astype(BF16), du)
            do = _nt(du, w_ref[...])
            draw = do * silu
            do_ref[...] = draw
            dgate_ref[...] = (do * raw * (sg * (1.0 + gate * (1.0 - sg)))).astype(BF16)
            ds_ref[...] = lax.dot_general(pick, draw * raw, (((1,), (1,)), ((), ())),
                                          precision=lax.Precision.HIGHEST, preferred_element_type=F32)

    return pl.pallas_call(
        body, grid=(T // tm,), name="bwd_merge",
        in_specs=[_row(tm, D), _row(tm, 512), _row(tm, 512), _row(tm, 512, Z_AG // 512), _row(tm, 512, Z_BG // 512),
                  _row(tm, D, Z_MA // D), _row(tm, D, Z_MB // D), _row(tm, D), _row(tm, D),
                  _res((512, D)), _res((512, D)), _res((D, D))],
        out_specs=[_row(tm, 512)] * 4 + [_row(tm, D)] * 2 + [pl.BlockSpec((MLA_H, tm), lambda i: (0, i))] * 2
        + [_acc((512, D)), _acc((512, D)), _acc((D, D))],
        out_shape=[SDS((T, 512), F32), SDS((T, 512), F32), SDS((T, 512), BF16), SDS((T, 512), BF16),
                   SDS((T, D), BF16), SDS((T, D), BF16), SDS((MLA_H, T), F32), SDS((MLA_H, T), F32),
                   SDS((512, D), F32), SDS((512, D), F32), SDS((D, D), F32)],
        compiler_params=_cp("arbitrary"))(dx1, oa, ob, z, z, z, z, ua, ub, wa, wb, wo)


def _mla_bwd(qf, kf, v, do, lse, dsum, nb, seq, tq):
    T = qf.shape[0]
    nq = seq // tq
    pw = 2 * LANES

    def body(q_ref, k_ref, v_ref, do_ref, lse_ref, dsum_ref, dq_ref, dk_ref, dv_ref, dk_s, dv_s, dqt_s):
        ki, qi = pl.program_id(2), pl.program_id(3)

        @pl.when((ki == 0) & (qi == 0))
        def _():
            dqt_s[...] = jnp.zeros(dqt_s.shape, F32)

        @pl.when(qi == ki)
        def _():
            dk_s[...] = jnp.zeros(dk_s.shape, F32)
            dv_s[...] = jnp.zeros(dv_s.shape, F32)

        def step(masked):
            if masked:
                keys = lax.broadcasted_iota(jnp.int32, (tq, tq), 0)
                queries = lax.broadcasted_iota(jnp.int32, (tq, tq), 1)
                mask = keys <= queries
            for j in range(2):
                wide = slice(LANES * j, LANES * (j + 1))
                sl = slice(MLA_V * j, MLA_V * (j + 1))
                q, k = q_ref[:, wide], k_ref[:, wide]
                dob = do_ref[:, sl].astype(BF16)
                s = _nt(k, q) * MLA_SCALE
                if masked:
                    s = jnp.where(mask, s, NEG)
                p = jnp.exp(s - lse_ref[j:j + 1, :])
                dv_s[:, sl] += _nn(p.astype(BF16), dob)
                ds = (p * (_nt(v_ref[:, sl], dob) - dsum_ref[j:j + 1, :]) * MLA_SCALE).astype(BF16)
                dk_s[:, wide] += _nn(ds, q)
                dqt_s[qi, wide, :] += _tn(k, ds)

        @pl.when(qi > ki)
        def _():
            step(False)

        @pl.when(qi == ki)
        def _():
            step(True)

        @pl.when(qi == nq - 1)
        def _():
            dk_ref[...] = dk_s[...]
            dv_ref[...] = dv_s[...]

        @pl.when((ki == nq - 1) & (qi == nq - 1))
        def _():
            for n in range(nq):
                dq_ref[tq * n:tq * (n + 1), :] = dqt_s[n].T

    qmap = lambda b, hp, ki, qi: (b * nq + jnp.maximum(qi, ki), hp)
    kmap = lambda b, hp, ki, qi: (b * nq + ki, hp)
    stat = pl.BlockSpec((None, 2, tq), lambda b, hp, ki, qi: (hp, 0, b * nq + jnp.maximum(qi, ki)))
    return pl.pallas_call(
        body, grid=(nb, MLA_H // 2, nq, nq), name="mla_bwd",
        in_specs=[pl.BlockSpec((tq, pw), qmap), pl.BlockSpec((tq, pw), kmap), pl.BlockSpec((tq, LANES), kmap),
                  pl.BlockSpec((tq, LANES), qmap), stat, stat],
        out_specs=[pl.BlockSpec((seq, pw), lambda b, hp, ki, qi: (b, hp)),
                   pl.BlockSpec((tq, pw), kmap), pl.BlockSpec((tq, LANES), kmap)],
        out_shape=[SDS((T, QFW), F32), SDS((T, QFW), F32), SDS((T, MLA_H * MLA_V), F32)],
        scratch_shapes=[pltpu.VMEM((tq, pw), F32), pltpu.VMEM((tq, LANES), F32), pltpu.VMEM((nq, pw, tq), F32)],
        compiler_params=_cp("parallel", "parallel", "arbitrary", "arbitrary"))(qf, kf, v, do, lse, dsum)


def _swa_bwd(sink, z, pos_col, pos_row, do, lse, dsum, nb, seq):
    T = z.shape[0]
    nblk = seq // BLK

    def body(sink_ref, q_ref, kvc_ref, kvp_ref, pcc_ref, pcp_ref, pr_ref, do_ref, lse_ref, dsum_ref,
             dq_ref, dkv_ref, dsink_ref):
        b, n = pl.program_id(0), pl.program_id(1)

        @pl.when((b == 0) & (n == 0))
        def _():
            dsink_ref[...] = jnp.zeros(dsink_ref.shape, F32)

        @pl.when(n == 0)
        def _():
            dkv_ref[...] = jnp.zeros(dkv_ref.shape, F32)

        kb, vb, scores = _swa_scores(n, q_ref, kvc_ref, kvp_ref, pcc_ref, pcp_ref, pr_ref)
        lane = lax.broadcasted_iota(jnp.int32, (1, LANES), 1)
        dsink = jnp.zeros((1, LANES), F32)
        dkv = [[None, None], [None, None]]
        dqs = []
        for h in range(SWA_H):
            g = h // (SWA_H // SWA_KV)
            gs = slice(SWA_DH * g, SWA_DH * (g + 1))
            qh, s = scores(h)
            lse, dsum = lse_ref[h:h + 1, :], dsum_ref[h:h + 1, :]
            p = jnp.exp(s - lse)
            dob = do_ref[:, SWA_DH * h:SWA_DH * (h + 1)].astype(BF16)
            ds = (p * (_nt(vb[:, gs], dob) - dsum) * SWA_SCALE).astype(BF16)
            dqs.append(_tn(kb[:, gs], ds))
            dk, dv = _nn(ds, qh), _nn(p.astype(BF16), dob)
            dkv[g][0] = dk if dkv[g][0] is None else dkv[g][0] + dk
            dkv[g][1] = dv if dkv[g][1] is None else dkv[g][1] + dv
            dsk = jnp.sum(-jnp.exp(sink_ref[h] - lse) * dsum, axis=1, keepdims=True)
            dsink = dsink + jnp.where(lane == h, dsk, 0.0)
        dq_ref[...] = jnp.concatenate(dqs, axis=0).T.astype(BF16)
        dsink_ref[...] += dsink
        upd = jnp.concatenate([dkv[0][0], dkv[1][0], dkv[0][1], dkv[1][1]], axis=1)
        dkv_ref[pl.ds(pl.multiple_of(n * BLK, BLK), BLK), :] += upd[BLK:]

        @pl.when(n > 0)
        def _():
            dkv_ref[pl.ds(pl.multiple_of((n - 1) * BLK, BLK), BLK), :] += upd[:BLK]

    return pl.pallas_call(
        body, grid=(nb, nblk), name="swa_bwd",
        in_specs=_swa_specs(nblk) + [pl.BlockSpec((BLK, 512), lambda b, n: (b * nblk + n, 0))]
        + [pl.BlockSpec((SWA_H, BLK), lambda b, n: (0, b * nblk + n))] * 2,
        out_specs=[pl.BlockSpec((BLK, 512), lambda b, n: (b * nblk + n, 0)),
                   pl.BlockSpec((seq, 2 * BLK), lambda b, n: (b, 0)),
                   pl.BlockSpec((1, LANES), lambda b, n: (0, 0))],
        out_shape=[SDS((T, 512), BF16), SDS((T, 2 * BLK), F32), SDS((1, LANES), F32)],
        compiler_params=_cp("arbitrary", "arbitrary"))(sink, z, z, z, pos_col, pos_col, pos_row, do, lse, dsum)


def _bwd_prep(dq, dk, dv, z, gq, gkv, wq, wkv, tc, ts1, ts2, tm):
    T = z.shape[0]

    def body(dq_ref, dk_ref, dv_ref, qd_ref, kvd_ref, gq_ref, gkv_ref, wq_ref, wkv_ref, c_ref, s1_ref, s2_ref,
             dqd_ref, dkvd_ref, dkr_ref, dwq_ref, dwkv_ref, dgq_ref, dgkv_ref, dqb_s, dkvb_s):
        @pl.when(pl.program_id(0) == 0)
        def _():
            for ref in (dwq_ref, dwkv_ref, dgq_ref, dgkv_ref):
                ref[...] = jnp.zeros(ref.shape, F32)

        c, s1, s2 = c_ref[...], s1_ref[...], s2_ref[...]
        lane = lax.broadcasted_iota(jnp.int32, (1, LANES), 1)
        rope_lanes = (lane >= MLA_NOPE) & (lane < MLA_QK)
        dkb = jnp.zeros((tm, LANES), F32)
        for h in range(MLA_H):
            sl = slice(LANES * h, LANES * (h + 1))
            dqb_s[:, sl] = _rope_t(dq_ref[:, sl], c, s1, s2).astype(BF16)
            dkh = dk_ref[:, sl]
            dkb = dkb + dkh
            dkvb_s[:, sl] = dkh.astype(BF16)
        dkvb_s[:, QFW:] = dv_ref[...].astype(BF16)
        dkr_ref[...] = _rope_t(jnp.where(rope_lanes, dkb, 0.0), c, s1, s2).astype(BF16)

        for (x_ref, g_ref, w_ref, d_s, dx_ref, dw_ref, dg_ref) in (
                (qd_ref, gq_ref, wq_ref, dqb_s, dqd_ref, dwq_ref, dgq_ref),
                (kvd_ref, gkv_ref, wkv_ref, dkvb_s, dkvd_ref, dwkv_ref, dgkv_ref)):
            xf, gf, db = x_ref[...], g_ref[...], d_s[...]
            r = _rstd(xf)
            n = xf * r
            dw_ref[...] += _tn((n * gf).astype(BF16), db)
            dx, dgr = _norm_bwd(_nt(db, w_ref[...]), n, r, gf)
            dx_ref[...] = dx.astype(BF16)
            dg_ref[...] += jnp.sum(dgr, axis=0, keepdims=True)

    return pl.pallas_call(
        body, grid=(T // tm,), name="bwd_prep",
        in_specs=[_row(tm, QFW), _row(tm, QFW), _row(tm, MLA_H * MLA_V),
                  _row(tm, QL, Z_QD // QL), _row(tm, KVL, Z_KVD // KVL),
                  _res((1, QL)), _res((1, KVL)), _res((QL, QFW)), _res((KVL, KVW)),
                  _row(tm, LANES), _row(tm, LANES), _row(tm, LANES)],
        out_specs=[_row(tm, QL), _row(tm, KVL), _row(tm, LANES),
                   _acc((QL, QFW)), _acc((KVL, KVW)), _acc((1, QL)), _acc((1, KVL))],
        out_shape=[SDS((T, QL), BF16), SDS((T, KVL), BF16), SDS((T, LANES), BF16),
                   SDS((QL, QFW), F32), SDS((KVL, KVW), F32), SDS((1, QL), F32), SDS((1, KVL), F32)],
        scratch_shapes=[pltpu.VMEM((tm, QFW), BF16), pltpu.VMEM((tm, KVW), BF16)],
        compiler_params=_cp("arbitrary"))(dq, dk, dv, z, z, gq, gkv, wq, wkv, tc, ts1, ts2)


def _bwd_in(pieces, x, g, dres, w, tm):
    T = x.shape[0]
    widths = [pc.shape[1] for pc in pieces]
    assert sum(widths) == ZW
    n_p = len(pieces)

    def body(*refs):
        p_refs, (x_ref, g_ref, r_ref, w_ref, dx_ref, dz_ref, dg_ref) = refs[:n_p], refs[n_p:]

        @pl.when(pl.program_id(0) == 0)
        def _():
            dg_ref[...] = jnp.zeros(dg_ref.shape, F32)

        off = 0
        for ref, wd in zip(p_refs, widths):
            dz_ref[:, off:off + wd] = ref[...].astype(BF16)
            off += wd
        xf, gf = x_ref[...], g_ref[...]
        r = _rstd(xf)
        n = xf * r
        dx, dgr = _norm_bwd(_nt(dz_ref[...], w_ref[...]), n, r, gf)
        dx_ref[...] = r_ref[...] + dx
        dg_ref[...] += jnp.sum(dgr, axis=0, keepdims=True)

    return pl.pallas_call(
        body, grid=(T // tm,), name="bwd_in",
        in_specs=[_row(tm, wd) for wd in widths] + [_row(tm, D), _res((1, D)), _row(tm, D), _res((D, ZW))],
        out_specs=[_row(tm, D), _row(tm, ZW), _acc((1, D))],
        out_shape=[SDS((T, D), F32), SDS((T, ZW), BF16), SDS((1, D), F32)],
        compiler_params=_cp("arbitrary"))(*pieces, x, g, dres, w)


def _wgrad_in(hb, dzb, tm):
    T = hb.shape[0]
    half = ZW // 2

    def body(h_ref, dz_ref, dw_ref):
        @pl.when(pl.program_id(1) == 0)
        def _():
            dw_ref[...] = jnp.zeros(dw_ref.shape, F32)

        dw_ref[...] += _tn(h_ref[...], dz_ref[...])

    return pl.pallas_call(
        body, grid=(2, T // tm), name="wgrad_in",
        in_specs=[pl.BlockSpec((tm, D), lambda j, t: (t, 0)), pl.BlockSpec((tm, half), lambda j, t: (t, j))],
        out_specs=pl.BlockSpec((D, half), lambda j, t: (0, j)),
        out_shape=SDS((D, ZW), F32),
        compiler_params=_cp("parallel", "arbitrary"))(hb, dzb)


def _win_to_kernel(w):
    s = lambda a, b: w[..., a:b]
    zero = lambda n: jnp.zeros(w.shape[:-1] + (n,), w.dtype)
    return jnp.concatenate([s(2208, 3232), s(3232, 4256), s(0, 512), s(768, 1280), s(1696, 2208), s(1280, 1536),
                            s(512, 640), s(640, 768), s(1536, 1664), zero(64), s(1664, 1696), zero(32)], axis=-1)


def _win_from_kernel(g):
    s = lambda a, n: g[..., a:a + n]
    return jnp.concatenate([s(Z_AQ, 512), s(Z_AK, 128), s(Z_AV, 128), s(Z_AG, 512), s(Z_QD, 256), s(Z_KVD, 128),
                            s(Z_KR + MLA_NOPE, MLA_ROPE), s(Z_BG, 512), s(Z_MA, 1024), s(Z_MB, 1024)], axis=-1)


def _wuq_to_kernel(w):
    w = w.reshape(w.shape[:-1] + (MLA_H, MLA_QK))
    w = jnp.pad(w, [(0, 0)] * (w.ndim - 1) + [(0, LANES - MLA_QK)])
    return w.reshape(w.shape[:-2] + (QFW,))


def _wuq_from_kernel(g):
    g = g.reshape(g.shape[:-1] + (MLA_H, LANES))[..., :MLA_QK]
    return g.reshape(g.shape[:-2] + (MLA_H * MLA_QK,))


def _wukv_to_kernel(w):
    w = w.reshape(w.shape[:-1] + (MLA_H, MLA_NOPE + MLA_V))
    k = jnp.pad(w[..., :MLA_NOPE], [(0, 0)] * (w.ndim - 1) + [(0, LANES - MLA_NOPE)])
    v = w[..., MLA_NOPE:]
    return jnp.concatenate([k.reshape(k.shape[:-2] + (QFW,)), v.reshape(v.shape[:-2] + (MLA_H * MLA_V,))], axis=-1)


def _wukv_from_kernel(g):
    k = g[..., :QFW].reshape(g.shape[:-1] + (MLA_H, LANES))[..., :MLA_NOPE]
    v = g[..., QFW:].reshape(g.shape[:-1] + (MLA_H, MLA_V))
    kv = jnp.concatenate([k, v], axis=-1)
    return kv.reshape(kv.shape[:-2] + (MLA_H * (MLA_NOPE + MLA_V),))


def _rope_tables(pos):
    half = MLA_ROPE // 2
    inv = 10000.0 ** (-jnp.arange(0, MLA_ROPE, 2, dtype=F32) / MLA_ROPE)
    ang = pos.astype(F32)[:, None] * inv
    cos, sin = jnp.cos(ang), jnp.sin(ang)
    one = jnp.ones((pos.shape[0], MLA_NOPE), F32)
    zero = lambda n: jnp.zeros((pos.shape[0], n), F32)
    tc = jnp.concatenate([one, cos, cos, one[:, :LANES - MLA_QK]], axis=1)
    ts1 = jnp.concatenate([zero(MLA_NOPE + half), sin, zero(LANES - MLA_QK)], axis=1)
    ts2 = jnp.concatenate([zero(MLA_NOPE), -sin, zero(LANES - MLA_NOPE - half)], axis=1)
    return tc, ts1, ts2


def _local_step(x, p, positions, loss_target, small, wts):
    nb, seq, _ = x.shape
    T = nb * seq
    tm = min(256, T)
    tq = min(512, seq)
    xf = x.reshape(T, D)
    pos = positions.reshape(T)
    posf = pos.astype(F32)
    pos_col, pos_row = posf.reshape(T, 1), posf.reshape(T // BLK, 1, BLK)
    tc, ts1, ts2 = _rope_tables(pos)

    saved = []
    for i in range(DEPTH):
        w, sm = wts[i], small[i]
        z, hb = _fwd_in(xf, sm["g_mix"], w["w_in"], tm)
        oa, lse_a = _swa_fwd(sm["sink"], z, pos_col, pos_row, nb, seq)
        qf, kf, v = _fwd_prep(z, sm["g_q"], sm["g_kv"], w["w_uq"], w["w_ukv"], tc, ts1, ts2, tm)
        ob, lse_b = _mla_fwd(qf, kf, v, nb, seq, tq)
        x1, ua, ub = _fwd_merge(xf, oa, ob, z, w["w_br_a"], w["w_br_b"], w["w_out"], tm)
        x2, pg, pp = _fwd_ple(x1, p[i].reshape(T, PLE), sm["g_ple"], w["w_ple_gate"], w["w_ple_proj"], tm)
        saved.append(dict(x=xf, z=z, hb=hb, oa=oa, lse_a=lse_a, qf=qf, kf=kf, v=v, ob=ob, lse_b=lse_b,
                          x1=x1, ua=ua, ub=ub, pg=pg, pp=pp))
        xf = x2

    dx, dg_final, loss = _loss_head(xf, small["g_final"], loss_target.reshape(T, D), tm)

    grads = [None] * DEPTH
    for i in reversed(range(DEPTH)):
        w, sm, sv = wts[i], small[i], saved[i]
        dx1, dwpg, dwpp, dg_ple = _bwd_ple(dx, sv["x1"], sv["pg"], sv["pp"], p[i].reshape(T, PLE), sm["g_ple"],
                                           w["w_ple_gate"], tm)
        doa, dob, dag, dbg, dma, dmb, dsum_a, dsum_b, dwa, dwb, dwo = _bwd_merge(
            dx1, sv["oa"], sv["ob"], sv["z"], sv["ua"], sv["ub"], w["w_br_a"], w["w_br_b"], w["w_out"], tm)
        dq_b, dk_b, dv_b = _mla_bwd(sv["qf"], sv["kf"], sv["v"], dob, sv["lse_b"],
                                    dsum_b.reshape(MLA_H // 2, 2, T), nb, seq, tq)
        dqd, dkvd, dkr, dwq, dwkv, dgq, dgkv = _bwd_prep(dq_b, dk_b, dv_b, sv["z"], sm["g_q"], sm["g_kv"],
                                                         w["w_uq"], w["w_ukv"], tc, ts1, ts2, tm)
        dq_a, dkv_a, dsink = _swa_bwd(sm["sink"], sv["z"], pos_col, pos_row, doa, sv["lse_a"], dsum_a, nb, seq)
        dx, dzb, dg_mix = _bwd_in([dma, dmb, dq_a, dag, dbg, dqd, dkv_a, dkvd, dkr], sv["x"], sm["g_mix"], dx1,
                                  w["w_in"], tm)
        dwin = _wgrad_in(sv["hb"], dzb, tm)
        grads[i] = dict(g_mix=dg_mix[0], w_in=_win_from_kernel(dwin), sink=dsink[0, :SWA_H], g_q=dgq[0],
                        w_uq=_wuq_from_kernel(dwq), g_kv=dgkv[0], w_ukv=_wukv_from_kernel(dwkv), w_br_a=dwa,
                        w_br_b=dwb, w_out=dwo, g_ple=dg_ple[0], w_ple_gate=dwpg, w_ple_proj=dwpp)
    return loss, dx.reshape(nb, seq, D), grads, dg_final[0]


def _kernel_weights(full):
    wi, wq, wkv = _win_to_kernel(full["w_in"]), _wuq_to_kernel(full["w_uq"]), _wukv_to_kernel(full["w_ukv"])
    return [dict(w_in=wi[i], w_uq=wq[i], w_ukv=wkv[i], w_br_a=full["w_br_a"][i], w_br_b=full["w_br_b"][i],
                 w_out=full["w_out"][i], w_ple_gate=full["w_ple_gate"][i], w_ple_proj=full["w_ple_proj"][i])
            for i in range(DEPTH)]


def _small_params(g_mix, sink, g_q, g_kv, g_ple, g_final):
    small = {i: dict(g_mix=g_mix[i][None], sink=sink[i], g_q=g_q[i][None], g_kv=g_kv[i][None], g_ple=g_ple[i][None])
             for i in range(DEPTH)}
    small["g_final"] = g_final[None]
    return small


WIDE_ROWS, WIDE_W = DEPTH * D, IN_W // N_DEV
UQ_W = MLA_H * MLA_QK // N_DEV
ROWS_PIECES = (("w_uq", DEPTH * QL), ("w_ukv", DEPTH * KVL), ("w_br_a", DEPTH * 512), ("w_br_b", DEPTH * 512),
               ("w_out", DEPTH * D), ("w_ple_gate", DEPTH * D), ("w_ple_proj", DEPTH * PLE))
SMALL = (("g_mix", (DEPTH, D)), ("sink", (DEPTH, SWA_H)), ("g_q", (DEPTH, QL)), ("g_kv", (DEPTH, KVL)),
         ("g_ple", (DEPTH, D)), ("g_final", (D,)))
VEC_ROWS = 48
ROWS_N = sum(r for _, r in ROWS_PIECES)
WIDE_TILE, ROWS_TILE = 256, ROWS_N // 4


def _to_rows(name, a):
    lead = a.shape[:-3]
    if name == "w_uq":
        a = jnp.pad(a, [(0, 0)] * (a.ndim - 1) + [(0, LANES - UQ_W)])
    return a.reshape(lead + (-1, LANES))


def _from_rows(name, r):
    lead = r.shape[:-2]
    if name in ("w_out", "w_ple_gate"):
        return r.reshape(lead + (DEPTH, D // N_DEV, D))
    r = r.reshape(lead + (DEPTH, -1, LANES))
    return r[..., :UQ_W] if name == "w_uq" else r


def _pack_rows(blocks):
    return jnp.concatenate([_to_rows(n, blocks[n]) for n, _ in ROWS_PIECES], axis=-2)


def _unpack_rows(rows):
    blocks, off = {}, 0
    for n, r in ROWS_PIECES:
        blocks[n] = _from_rows(n, rows[..., off:off + r, :])
        off += r
    return blocks


def _pack_vec(vectors):
    vec = jnp.concatenate([vectors[n].reshape(-1) for n, _ in SMALL])
    return jnp.pad(vec, (0, VEC_ROWS * LANES - vec.shape[0])).reshape(VEC_ROWS, LANES)


def _unpack_vec(vec):
    vec = vec.reshape(-1)
    vectors, off = {}, 0
    for n, shp in SMALL:
        size = 1
        for s in shp:
            size *= s
        vectors[n] = vec[off:off + size].reshape(shp)
        off += size
    return vectors


def _join(name, blocks):
    if name in ("w_out", "w_ple_gate"):
        return jnp.moveaxis(blocks, 0, 1).reshape(DEPTH, -1, blocks.shape[-1])
    return jnp.moveaxis(blocks, 0, 2).reshape(DEPTH, blocks.shape[2], -1)


def _split(name, full):
    if name in ("w_out", "w_ple_gate"):
        return jnp.moveaxis(full.reshape(DEPTH, N_DEV, -1, full.shape[-1]), 1, 0)
    return jnp.moveaxis(full.reshape(DEPTH, full.shape[1], N_DEV, -1), 2, 0)


MESH_ID = pl.DeviceIdType.MESH
ANY = pl.BlockSpec(memory_space=pl.ANY)


def _place():
    return lax.axis_index("x"), lax.axis_index("y"), lax.axis_index("c")


def _all_gather(blocks):
    n = len(blocks)

    def body(*refs):
        x_refs, out_refs, (send_sems, recv_sems, local_sems) = refs[:n], refs[n:2 * n], refs[2 * n:]
        x, y, c = _place()
        me, sibling = (x, y, c), (x, y, 1 - c)
        chips = [(1 - x, y), (x, 1 - y), (1 - x, 1 - y)]

        def slot(a, px, py, pc):
            return out_refs[a].at[4 * px + 2 * py + pc]

        def copy(a, k, blk, to, src=None):
            return pltpu.make_async_remote_copy(
                src_ref=slot(a, *blk) if src is None else src, dst_ref=slot(a, *blk),
                send_sem=send_sems.at[7 * a + k], recv_sem=recv_sems.at[7 * a + k], device_id=to,
                device_id_type=MESH_ID)

        mine = [pltpu.make_async_copy(x_refs[a], slot(a, *me), local_sems.at[a]) for a in range(n)]
        for cp in mine:
            cp.start()
        first = []
        for a in range(n):
            first += [copy(a, 0, me, sibling, src=x_refs[a])]
            first += [copy(a, 1 + j, me, (*chip, c), src=x_refs[a]) for j, chip in enumerate(chips)]
        for cp in first:
            cp.start()
        passed = []
        for j, chip in enumerate(chips):
            for a in range(n):
                copy(a, 1 + j, (*chip, c), me).wait_recv()
                passed.append(copy(a, 4 + j, (*chip, c), sibling))
                passed[-1].start()
        for a in range(n):
            copy(a, 0, sibling, me).wait_recv()
            for j, chip in enumerate(chips):
                copy(a, 4 + j, (*chip, 1 - c), me).wait_recv()
        for cp in first + passed:
            cp.wait_send()
        for cp in mine:
            cp.wait()

    return pl.pallas_call(
        body, name="all_gather_weights",
        out_shape=[SDS((N_DEV,) + b.shape, b.dtype) for b in blocks],
        in_specs=[ANY] * n, out_specs=[ANY] * n,
        scratch_shapes=[pltpu.SemaphoreType.DMA((7 * n,)), pltpu.SemaphoreType.DMA((7 * n,)),
                        pltpu.SemaphoreType.DMA((n,))],
    )(*blocks)


def _swap_sibling(arrs):
    n = len(arrs)

    def body(*refs):
        a_refs, out_refs, (send_sems, recv_sems) = refs[:n], refs[n:2 * n], refs[2 * n:]
        x, y, c = _place()
        copies = [pltpu.make_async_remote_copy(
            src_ref=a_refs[a].at[:, 1 - c], dst_ref=out_refs[a], send_sem=send_sems.at[a], recv_sem=recv_sems.at[a],
            device_id=(x, y, 1 - c), device_id_type=MESH_ID) for a in range(n)]
        for cp in copies:
            cp.start()
        for cp in copies:
            cp.wait()

    return pl.pallas_call(
        body, name="swap_sibling", out_shape=[SDS((a.shape[0],) + a.shape[2:], a.dtype) for a in arrs],
        in_specs=[ANY] * n, out_specs=[ANY] * n,
        scratch_shapes=[pltpu.SemaphoreType.DMA((n,)), pltpu.SemaphoreType.DMA((n,))],
    )(*arrs)


def _exchange_chips(arrs):
    n = len(arrs)

    def body(*refs):
        p_refs, out_refs, (send_sems, recv_sems, local_sems) = refs[:n], refs[n:2 * n], refs[2 * n:]
        x, y, c = _place()
        mine = 2 * x + y
        local = [pltpu.make_async_copy(p_refs[a].at[mine], out_refs[a].at[mine], local_sems.at[a]) for a in range(n)]
        for cp in local:
            cp.start()
        peers = [(1 - x, y), (x, 1 - y), (1 - x, 1 - y)]

        def copy(a, j, src_chip, dst_chip):
            px, py = peers[j]
            return pltpu.make_async_remote_copy(
                src_ref=p_refs[a].at[src_chip], dst_ref=out_refs[a].at[dst_chip], send_sem=send_sems.at[3 * a + j],
                recv_sem=recv_sems.at[3 * a + j], device_id=(px, py, c), device_id_type=MESH_ID)

        copies = [copy(a, j, 2 * px + py, mine) for a in range(n) for j, (px, py) in enumerate(peers)]
        for cp in copies:
            cp.start()
        for a in range(n):
            for j, (px, py) in enumerate(peers):
                copy(a, j, mine, 2 * px + py).wait_recv()
        for cp in copies:
            cp.wait_send()
        for cp in local:
            cp.wait()

    return pl.pallas_call(
        body, name="exchange_chips", out_shape=[SDS(a.shape, a.dtype) for a in arrs],
        in_specs=[ANY] * n, out_specs=[ANY] * n,
        scratch_shapes=[pltpu.SemaphoreType.DMA((3 * n,)), pltpu.SemaphoreType.DMA((3 * n,)),
                        pltpu.SemaphoreType.DMA((n,))],
    )(*arrs)


def _add_mine(g, recv, core, tile, dtype):
    _, _, rows, width = g.shape

    def body(c_ref, g_ref, r_ref, o_ref):
        o_ref[...] = (g_ref[...] + r_ref[...]).astype(dtype)

    spec = pl.BlockSpec((None, tile, width), lambda k, i, c_ref: (k, i, 0))
    return pl.pallas_call(
        body, name="add_sibling", out_shape=SDS(recv.shape, dtype),
        grid_spec=pltpu.PrefetchScalarGridSpec(
            num_scalar_prefetch=1, grid=(g.shape[0], rows // tile),
            in_specs=[pl.BlockSpec((None, None, tile, width), lambda k, i, c_ref: (k, c_ref[0], i, 0)), spec],
            out_specs=spec),
        compiler_params=_cp("parallel", "parallel"))(core, g, recv)


def _sum_adamw(parts, w, m, v, tile):
    rows, width = w.shape

    def body(p_ref, w_ref, m_ref, v_ref, g_ref, d_ref, nm_ref, nv_ref):
        g = ((p_ref[0].astype(F32) + p_ref[1].astype(F32)) + p_ref[2].astype(F32)) + p_ref[3].astype(F32)
        nm = ADAM_B1 * m_ref[...] + (1.0 - ADAM_B1) * g
        nv = ADAM_B2 * v_ref[...] + (1.0 - ADAM_B2) * jnp.square(g)
        m_hat = nm / (1.0 - ADAM_B1 ** ADAM_STEP)
        v_hat = nv / (1.0 - ADAM_B2 ** ADAM_STEP)
        g_ref[...] = g
        nm_ref[...] = nm
        nv_ref[...] = nv
        d_ref[...] = -ADAM_LR * (m_hat / (jnp.sqrt(v_hat) + ADAM_EPS) + ADAM_WD * w_ref[...])

    spec = pl.BlockSpec((tile, width), lambda i: (i, 0))
    return pl.pallas_call(
        body, grid=(rows // tile,), name="sum_adamw",
        in_specs=[pl.BlockSpec((4, tile, width), lambda i: (0, i, 0)), spec, spec, spec],
        out_specs=[spec] * 4, out_shape=[SDS((rows, width), F32)] * 4,
        compiler_params=_cp("parallel"))(parts, w, m, v)


def kernel(x, p, positions, g_mix, w_in, sink, g_q, w_uq, g_kv, w_ukv, w_br_a, w_br_b, w_out, g_ple, w_ple_gate, w_ple_proj, g_final, loss_target, m_g_mix, m_w_in, m_sink, m_g_q, m_w_uq, m_g_kv, m_w_ukv, m_w_br_a, m_w_br_b, m_w_out, m_g_ple, m_w_ple_gate, m_w_ple_proj, m_g_final, v_g_mix, v_w_in, v_sink, v_g_q, v_w_uq, v_g_kv, v_w_ukv, v_w_br_a, v_w_br_b, v_w_out, v_g_ple, v_w_ple_gate, v_w_ple_proj, v_g_final):
    weights = dict(g_mix=g_mix, w_in=w_in, sink=sink, g_q=g_q, w_uq=w_uq, g_kv=g_kv, w_ukv=w_ukv, w_br_a=w_br_a,
                   w_br_b=w_br_b, w_out=w_out, g_ple=g_ple, w_ple_gate=w_ple_gate, w_ple_proj=w_ple_proj,
                   g_final=g_final)
    mom1 = dict(g_mix=m_g_mix, w_in=m_w_in, sink=m_sink, g_q=m_g_q, w_uq=m_w_uq, g_kv=m_g_kv, w_ukv=m_w_ukv,
                w_br_a=m_w_br_a, w_br_b=m_w_br_b, w_out=m_w_out, g_ple=m_g_ple, w_ple_gate=m_w_ple_gate,
                w_ple_proj=m_w_ple_proj, g_final=m_g_final)
    mom2 = dict(g_mix=v_g_mix, w_in=v_w_in, sink=v_sink, g_q=v_g_q, w_uq=v_w_uq, g_kv=v_g_kv, w_ukv=v_w_ukv,
                w_br_a=v_w_br_a, w_br_b=v_w_br_b, w_out=v_w_out, g_ple=v_g_ple, w_ple_gate=v_w_ple_gate,
                w_ple_proj=v_w_ple_proj, g_final=v_g_final)
    wide = lambda d: d["w_in"].reshape(WIDE_ROWS, WIDE_W)
    rows = lambda d: _pack_rows(d)

    got_wide, got_rows = _all_gather([wide(weights).astype(BF16), rows(weights).astype(BF16)])
    blocks = _unpack_rows(got_rows)
    full = {n: _join(n, blocks[n]) for n, _ in ROWS_PIECES}
    full["w_in"] = _join("w_in", got_wide.reshape(N_DEV, DEPTH, D, WIDE_W))
    wts = _kernel_weights(full)
    small = _small_params(g_mix, sink, g_q, g_kv, g_ple, g_final)

    loss, grad_x, grads, dg_final = _local_step(x, p, positions, loss_target, small, wts)
    loss = lax.psum(loss[0, 0], ("x", "y", "c"))

    stacked = {n: jnp.stack([grads[i][n] for i in range(DEPTH)]) for n in grads[0]}
    stacked["g_final"] = dg_final
    pay_wide = _split("w_in", stacked["w_in"]).reshape(N_DEV // 2, 2, WIDE_ROWS, WIDE_W)
    pay_rows = _pack_rows({n: _split(n, stacked[n]) for n, _ in ROWS_PIECES})
    pay_rows = pay_rows.reshape(N_DEV // 2, 2, ROWS_N, LANES)
    pay_vec = jnp.broadcast_to(_pack_vec(stacked), (N_DEV // 2, 2, VEC_ROWS, LANES))
    core = lax.axis_index("c").astype(jnp.int32).reshape(1)
    from_sibling = _swap_sibling([pay_wide, pay_rows, pay_vec])
    chip_partial = [_add_mine(pay_wide, from_sibling[0], core, WIDE_TILE, BF16),
                    _add_mine(pay_rows, from_sibling[1], core, ROWS_TILE, BF16),
                    _add_mine(pay_vec, from_sibling[2], core, VEC_ROWS, F32)]
    parts_wide, parts_rows, parts_vec = _exchange_chips(chip_partial)
    out_wide = _sum_adamw(parts_wide, wide(weights), wide(mom1), wide(mom2), WIDE_TILE)
    out_rows = _sum_adamw(parts_rows, rows(weights), rows(mom1), rows(mom2), ROWS_TILE)
    out_vec = _sum_adamw(parts_vec, _pack_vec(weights), _pack_vec(mom1), _pack_vec(mom2), VEC_ROWS)

    outs = []
    for ow, orow, ovec in zip(out_wide, out_rows, out_vec):
        named = _unpack_rows(orow)
        named.update(_unpack_vec(ovec))
        named["w_in"] = ow.reshape(DEPTH, D, WIDE_W)
        outs += [named[n] for n in weights]
    return (loss, grad_x, *outs)
```

```python
import functools

import jax
import jax.numpy as jnp
from jax import lax
from jax.experimental import pallas as pl
from jax.experimental.pallas import tpu as pltpu

F32, BF16 = jnp.float32, jnp.bfloat16
SDS = jax.ShapeDtypeStruct

D = 1024
DEPTH = 2
PLE = 256
BLK = 128
EPS = 1e-6
NEG = -1e30
SWA_H, SWA_KV, SWA_DH = 8, 2, 64
MLA_H, MLA_NOPE, MLA_ROPE, MLA_V = 8, 64, 32, 64
MLA_QK = MLA_NOPE + MLA_ROPE
QL, KVL = 256, 128
IN_W = 4256
N_DEV = 8

V7X_VMEM_BYTES = 64 * 1024 * 1024
LANES = 128
VMEM_LIMIT = V7X_VMEM_BYTES * 7 // 8

ZW = 4352
Z_MA, Z_MB, Z_AQ, Z_AG, Z_BG, Z_QD, Z_AK, Z_AV, Z_KVD, Z_KR = 0, 1024, 2048, 2560, 3072, 3584, 3840, 3968, 4096, 4224
QFW = MLA_H * LANES
KVW = QFW + MLA_H * MLA_V
MLA_SCALE = MLA_QK ** -0.5
SWA_SCALE = SWA_DH ** -0.5
ROLL_UP, ROLL_DOWN = MLA_ROPE // 2, LANES - MLA_ROPE // 2

ADAM_LR, ADAM_B1, ADAM_B2, ADAM_EPS, ADAM_WD, ADAM_STEP = 0.001, 0.9, 0.999, 1e-08, 0.01, 10

FLAT_W = 1024


def _cp(*sem):
    return pltpu.CompilerParams(dimension_semantics=sem, vmem_limit_bytes=VMEM_LIMIT)


def _row(tm, w, col=0):
    return pl.BlockSpec((tm, w), lambda i: (i, col))


def _res(shape):
    return pl.BlockSpec(shape, lambda *_: (0,) * len(shape), pipeline_mode=pl.Buffered(1))


def _acc(shape):
    return pl.BlockSpec(shape, lambda *_: (0,) * len(shape))


def _rstd(xf):
    return lax.rsqrt(jnp.mean(xf * xf, axis=-1, keepdims=True) + EPS)


def _norm_bwd(dh, n, r, g):
    dn = dh * g
    return r * (dn - n * jnp.mean(dn * n, axis=-1, keepdims=True)), dh * n


def _nt(a, b):
    return lax.dot_general(a, b, (((1,), (1,)), ((), ())), preferred_element_type=F32)


def _tn(a, b):
    return lax.dot_general(a, b, (((0,), (0,)), ((), ())), preferred_element_type=F32)


def _nn(a, b):
    return jnp.dot(a, b, preferred_element_type=F32)


def _sig(x):
    return jax.nn.sigmoid(x)


def _rope(t, c, s1, s2):
    return t * c + pltpu.roll(t, ROLL_UP, 1) * s1 + pltpu.roll(t, ROLL_DOWN, 1) * s2


def _rope_t(d, c, s1, s2):
    return d * c + pltpu.roll(d * s1, ROLL_DOWN, 1) + pltpu.roll(d * s2, ROLL_UP, 1)


def _fwd_in(x, g, w, tm):
    T = x.shape[0]

    def body(x_ref, g_ref, w_ref, z_ref, h_ref):
        xf = x_ref[...]
        h = ((xf * _rstd(xf)) * g_ref[...]).astype(BF16)
        h_ref[...] = h
        z_ref[...] = _nn(h, w_ref[...])

    return pl.pallas_call(
        body, grid=(T // tm,), name="fwd_in",
        in_specs=[_row(tm, D), _res((1, D)), _res((D, ZW))],
        out_specs=[_row(tm, ZW), _row(tm, D)],
        out_shape=[SDS((T, ZW), F32), SDS((T, D), BF16)],
        compiler_params=_cp("parallel"))(x, g, w)


def _fwd_prep(z, gq, gkv, wq, wkv, tc, ts1, ts2, tm):
    T = z.shape[0]

    def body(qd_ref, kvd_ref, kr_ref, gq_ref, gkv_ref, wq_ref, wkv_ref, c_ref, s1_ref, s2_ref, q_ref, k_ref, v_ref):
        qd, kvd = qd_ref[...], kvd_ref[...]
        hq = ((qd * _rstd(qd)) * gq_ref[...]).astype(BF16)
        hkv = ((kvd * _rstd(kvd)) * gkv_ref[...]).astype(BF16)
        qf = _nn(hq, wq_ref[...])
        kvf = _nn(hkv, wkv_ref[...])
        c, s1, s2 = c_ref[...], s1_ref[...], s2_ref[...]
        krb = _rope(kr_ref[...], c, s1, s2)
        for h in range(MLA_H):
            sl = slice(LANES * h, LANES * (h + 1))
            q_ref[:, sl] = _rope(qf[:, sl], c, s1, s2).astype(BF16)
            k_ref[:, sl] = (kvf[:, sl] + krb).astype(BF16)
        v_ref[...] = kvf[:, QFW:].astype(BF16)

    return pl.pallas_call(
        body, grid=(T // tm,), name="fwd_prep",
        in_specs=[_row(tm, QL, Z_QD // QL), _row(tm, KVL, Z_KVD // KVL), _row(tm, LANES, Z_KR // LANES),
                  _res((1, QL)), _res((1, KVL)), _res((QL, QFW)), _res((KVL, KVW)),
                  _row(tm, LANES), _row(tm, LANES), _row(tm, LANES)],
        out_specs=[_row(tm, QFW), _row(tm, QFW), _row(tm, MLA_H * MLA_V)],
        out_shape=[SDS((T, QFW), BF16), SDS((T, QFW), BF16), SDS((T, MLA_H * MLA_V), BF16)],
        compiler_params=_cp("parallel"))(z, z, z, gq, gkv, wq, wkv, tc, ts1, ts2)


def _mla_fwd(qf, kf, v, nb, seq, tq):
    T = qf.shape[0]
    nq = seq // tq
    pw = 2 * LANES

    def body(q_ref, k_ref, v_ref, o_ref, lse_ref, m_s, l_s, acc_s):
        qi, ki = pl.program_id(2), pl.program_id(3)

        @pl.when(ki == 0)
        def _():
            m_s[...] = jnp.full(m_s.shape, NEG, F32)
            l_s[...] = jnp.zeros(l_s.shape, F32)
            acc_s[...] = jnp.zeros(acc_s.shape, F32)

        def step(masked):
            if masked:
                keys = lax.broadcasted_iota(jnp.int32, (tq, tq), 0)
                queries = lax.broadcasted_iota(jnp.int32, (tq, tq), 1)
                mask = keys <= queries
            ss = []
            for j in range(2):
                wide = slice(LANES * j, LANES * (j + 1))
                s = _nt(k_ref[:, wide], q_ref[:, wide]) * MLA_SCALE
                ss.append(jnp.where(mask, s, NEG) if masked else s)
            ps, alphas = [], []
            for j in range(2):
                m_prev = m_s[j]
                m_new = jnp.maximum(m_prev, jnp.max(ss[j], axis=0, keepdims=True))
                alpha = jnp.exp(m_prev - m_new)
                p = jnp.exp(ss[j] - m_new)
                l_s[j] = alpha * l_s[j] + jnp.sum(p, axis=0, keepdims=True)
                m_s[j] = m_new
                ps.append(p.astype(BF16))
                alphas.append(alpha)
            for j in range(2):
                rows = slice(MLA_V * j, MLA_V * (j + 1))
                acc_s[rows, :] = alphas[j] * acc_s[rows, :] + _tn(v_ref[:, rows], ps[j])

        @pl.when(ki < qi)
        def _():
            step(False)

        @pl.when(ki == qi)
        def _():
            step(True)
            for j in range(2):
                rows = slice(MLA_V * j, MLA_V * (j + 1))
                acc_s[rows, :] = acc_s[rows, :] / l_s[j]
                lse_ref[j:j + 1, :] = m_s[j] + jnp.log(l_s[j])
            o_ref[...] = acc_s[...].T

    kv_map = lambda b, hp, qi, ki: (b * nq + jnp.minimum(ki, qi), hp)
    return pl.pallas_call(
        body, grid=(nb, MLA_H // 2, nq, nq), name="mla_fwd",
        in_specs=[pl.BlockSpec((tq, pw), lambda b, hp, qi, ki: (b * nq + qi, hp)),
                  pl.BlockSpec((tq, pw), kv_map),
                  pl.BlockSpec((tq, LANES), kv_map)],
        out_specs=[pl.BlockSpec((tq, LANES), lambda b, hp, qi, ki: (b * nq + qi, hp)),
                   pl.BlockSpec((None, 2, tq), lambda b, hp, qi, ki: (hp, 0, b * nq + qi))],
        out_shape=[SDS((T, MLA_H * MLA_V), F32), SDS((MLA_H // 2, 2, T), F32)],
        scratch_shapes=[pltpu.VMEM((2, 1, tq), F32), pltpu.VMEM((2, 1, tq), F32), pltpu.VMEM((LANES, tq), F32)],
        compiler_params=_cp("parallel", "parallel", "parallel", "arbitrary"))(qf, kf, v)


def _swa_specs(nblk):
    cur = lambda b, n: (b * nblk + n, 0)
    prev = lambda b, n: (b * nblk + jnp.maximum(n - 1, 0), 0)
    kvc = Z_AK // (2 * BLK)
    return [pl.BlockSpec(memory_space=pltpu.SMEM),
            pl.BlockSpec((BLK, 512), lambda b, n: (b * nblk + n, Z_AQ // 512)),
            pl.BlockSpec((BLK, 2 * BLK), lambda b, n: (b * nblk + n, kvc)),
            pl.BlockSpec((BLK, 2 * BLK), lambda b, n: (b * nblk + jnp.maximum(n - 1, 0), kvc)),
            pl.BlockSpec((BLK, 1), cur),
            pl.BlockSpec((BLK, 1), prev),
            pl.BlockSpec((1, 1, BLK), lambda b, n: (b * nblk + n, 0, 0))]


def _swa_scores(n, q_ref, kvc_ref, kvp_ref, pcc_ref, pcp_ref, pr_ref):
    kv = jnp.concatenate([kvp_ref[...], kvc_ref[...]], axis=0)
    kb, vb = kv[:, :BLK].astype(BF16), kv[:, BLK:].astype(BF16)
    dist = pr_ref[0] - jnp.concatenate([pcp_ref[...], pcc_ref[...]], axis=0)
    key = lax.broadcasted_iota(jnp.int32, (2 * BLK, BLK), 0)
    qry = lax.broadcasted_iota(jnp.int32, (2 * BLK, BLK), 1)
    valid = (key > qry) & (key <= qry + BLK) & ((key >= BLK) | (n > 0))

    def scores(h):
        g = h // (SWA_H // SWA_KV)
        qh = q_ref[:, SWA_DH * h:SWA_DH * (h + 1)].astype(BF16)
        s = _nt(kb[:, SWA_DH * g:SWA_DH * (g + 1)], qh) * SWA_SCALE - (2.0 ** -(h + 1)) * dist
        return qh, jnp.where(valid, s, NEG)

    return kb, vb, scores


def _swa_fwd(sink, z, pos_col, pos_row, nb, seq):
    T = z.shape[0]
    nblk = seq // BLK

    def body(sink_ref, q_ref, kvc_ref, kvp_ref, pcc_ref, pcp_ref, pr_ref, o_ref, lse_ref):
        kb, vb, scores = _swa_scores(pl.program_id(1), q_ref, kvc_ref, kvp_ref, pcc_ref, pcp_ref, pr_ref)
        ss = [scores(h)[1] for h in range(SWA_H)]
        es, dens = [], []
        for h in range(SWA_H):
            sk = sink_ref[h]
            m = jnp.maximum(jnp.max(ss[h], axis=0, keepdims=True), sk)
            e = jnp.exp(ss[h] - m)
            den = jnp.sum(e, axis=0, keepdims=True) + jnp.exp(sk - m)
            lse_ref[h:h + 1, :] = m + jnp.log(den)
            es.append(e.astype(BF16))
            dens.append(den)
        outs = []
        for h in range(SWA_H):
            g = h // (SWA_H // SWA_KV)
            outs.append(_tn(vb[:, SWA_DH * g:SWA_DH * (g + 1)], es[h]) / dens[h])
        o_ref[...] = jnp.concatenate(outs, axis=0).T

    return pl.pallas_call(
        body, grid=(nb, nblk), name="swa_fwd",
        in_specs=_swa_specs(nblk),
        out_specs=[pl.BlockSpec((BLK, 512), lambda b, n: (b * nblk + n, 0)),
                   pl.BlockSpec((SWA_H, BLK), lambda b, n: (0, b * nblk + n))],
        out_shape=[SDS((T, 512), F32), SDS((SWA_H, T), F32)],
        compiler_params=_cp("parallel", "parallel"))(sink, z, z, z, pos_col, pos_col, pos_row)


def _fwd_merge(x, oa, ob, z, wa, wb, wo, tm):
    T = x.shape[0]

    def body(x_ref, oa_ref, ob_ref, ag_ref, bg_ref, ma_ref, mb_ref, wa_ref, wb_ref, wo_ref, x1_ref, ua_ref, ub_ref):
        ag, bg = ag_ref[...], bg_ref[...]
        ua = _nn((oa_ref[...] * (ag * _sig(ag))).astype(BF16), wa_ref[...])
        ub = _nn((ob_ref[...] * (bg * _sig(bg))).astype(BF16), wb_ref[...])
        ua_ref[...] = ua
        ub_ref[...] = ub
        y = _sig(ma_ref[...]) * ua + _sig(mb_ref[...]) * ub
        x1_ref[...] = x_ref[...] + _nn(y.astype(BF16), wo_ref[...])

    return pl.pallas_call(
        body, grid=(T // tm,), name="fwd_merge",
        in_specs=[_row(tm, D), _row(tm, 512), _row(tm, 512), _row(tm, 512, Z_AG // 512), _row(tm, 512, Z_BG // 512),
                  _row(tm, D, Z_MA // D), _row(tm, D, Z_MB // D), _res((512, D)), _res((512, D)), _res((D, D))],
        out_specs=[_row(tm, D)] * 3,
        out_shape=[SDS((T, D), F32)] * 3,
        compiler_params=_cp("parallel"))(x, oa, ob, z, z, z, z, wa, wb, wo)


def _fwd_ple(x1, p, g, wpg, wpp, tm):
    T = x1.shape[0]

    def body(x_ref, p_ref, g_ref, wpg_ref, wpp_ref, x2_ref, pg_ref, pp_ref):
        xf = x_ref[...]
        h1 = ((xf * _rstd(xf)) * g_ref[...]).astype(BF16)
        pg = _sig(_nn(h1, wpg_ref[...]))
        pp = _nn(p_ref[...].astype(BF16), wpp_ref[...])
        pg_ref[...] = pg
        pp_ref[...] = pp
        x2_ref[...] = xf + pg * pp

    return pl.pallas_call(
        body, grid=(T // tm,), name="fwd_ple",
        in_specs=[_row(tm, D), _row(tm, PLE), _res((1, D)), _res((D, D)), _res((PLE, D))],
        out_specs=[_row(tm, D)] * 3,
        out_shape=[SDS((T, D), F32)] * 3,
        compiler_params=_cp("parallel"))(x1, p, g, wpg, wpp)


def _loss_head(x, g, tgt, tm):
    T = x.shape[0]

    def body(x_ref, g_ref, t_ref, dx_ref, dg_ref, loss_ref):
        @pl.when(pl.program_id(0) == 0)
        def _():
            dg_ref[...] = jnp.zeros(dg_ref.shape, F32)
            loss_ref[...] = jnp.zeros(loss_ref.shape, F32)

        xf, gf = x_ref[...], g_ref[...]
        r = _rstd(xf)
        n = xf * r
        err = n * gf - t_ref[...]
        loss_ref[...] += 0.5 * jnp.sum(jnp.mean(err * err, axis=-1, keepdims=True), axis=0, keepdims=True)
        dx, dgr = _norm_bwd(err * (1.0 / D), n, r, gf)
        dx_ref[...] = dx
        dg_ref[...] += jnp.sum(dgr, axis=0, keepdims=True)

    return pl.pallas_call(
        body, grid=(T // tm,), name="loss_head",
        in_specs=[_row(tm, D), _res((1, D)), _row(tm, D)],
        out_specs=[_row(tm, D), _acc((1, D)), _acc((1, LANES))],
        out_shape=[SDS((T, D), F32), SDS((1, D), F32), SDS((1, LANES), F32)],
        compiler_params=_cp("arbitrary"))(x, g, tgt)


def _bwd_ple(dx2, x1, pg, pp, p, g, wpg, tm):
    T = x1.shape[0]

    def body(d_ref, x_ref, pg_ref, pp_ref, p_ref, g_ref, w_ref, dx_ref, dwg_ref, dwp_ref, dg_ref):
        @pl.when(pl.program_id(0) == 0)
        def _():
            dwg_ref[...] = jnp.zeros(dwg_ref.shape, F32)
            dwp_ref[...] = jnp.zeros(dwp_ref.shape, F32)
            dg_ref[...] = jnp.zeros(dg_ref.shape, F32)

        d, xf, pg, gf = d_ref[...], x_ref[...], pg_ref[...], g_ref[...]
        r = _rstd(xf)
        n = xf * r
        dpgl = (d * pp_ref[...] * pg * (1.0 - pg)).astype(BF16)
        dwg_ref[...] += _tn((n * gf).astype(BF16), dpgl)
        dwp_ref[...] += _tn(p_ref[...].astype(BF16), (d * pg).astype(BF16))
        dxn, dgr = _norm_bwd(_nt(dpgl, w_ref[...]), n, r, gf)
        dx_ref[...] = d + dxn
        dg_ref[...] += jnp.sum(dgr, axis=0, keepdims=True)

    return pl.pallas_call(
        body, grid=(T // tm,), name="bwd_ple",
        in_specs=[_row(tm, D)] * 4 + [_row(tm, PLE), _res((1, D)), _res((D, D))],
        out_specs=[_row(tm, D), _acc((D, D)), _acc((PLE, D)), _acc((1, D))],
        out_shape=[SDS((T, D), F32), SDS((D, D), F32), SDS((PLE, D), F32), SDS((1, D), F32)],
        compiler_params=_cp("arbitrary"))(dx2, x1, pg, pp, p, g, wpg)


def _bwd_merge(dx1, oa, ob, z, ua, ub, wa, wb, wo, tm):
    T = dx1.shape[0]

    def body(d_ref, oa_ref, ob_ref, ag_ref, bg_ref, ma_ref, mb_ref, ua_ref, ub_ref, wa_ref, wb_ref, wo_ref,
             doa_ref, dob_ref, dag_ref, dbg_ref, dma_ref, dmb_ref, dsa_ref, dsb_ref, dwa_ref, dwb_ref, dwo_ref):
        @pl.when(pl.program_id(0) == 0)
        def _():
            dwa_ref[...] = jnp.zeros(dwa_ref.shape, F32)
            dwb_ref[...] = jnp.zeros(dwb_ref.shape, F32)
            dwo_ref[...] = jnp.zeros(dwo_ref.shape, F32)

        db = d_ref[...].astype(BF16)
        ua, ub = ua_ref[...], ub_ref[...]
        sa, sb = _sig(ma_ref[...]), _sig(mb_ref[...])
        dwo_ref[...] += _tn((sa * ua + sb * ub).astype(BF16), db)
        dy = _nt(db, wo_ref[...])
        dma_ref[...] = (dy * ua * sa * (1.0 - sa)).astype(BF16)
        dmb_ref[...] = (dy * ub * sb * (1.0 - sb)).astype(BF16)
        for (o_ref, gate_ref, s, w_ref, do_ref, dgate_ref, dw_ref, ds_ref) in (
                (oa_ref, ag_ref, sa, wa_ref, doa_ref, dag_ref, dwa_ref, dsa_ref),
                (ob_ref, bg_ref, sb, wb_ref, dob_ref, dbg_ref, dwb_ref, dsb_ref)):
            du = (dy * s).astype(BF16)
            raw, gate = o_ref[...], gate_ref[...]
            sg = _sig(gate)
            silu = gate * sg
            dw_ref[...] += _tn((raw * silu).astype(BF16), du)
            do = _nt(du, w_ref[...])
            draw = do * silu
            do_ref[...] = draw
            dgate_ref[...] = (do * raw * (sg * (1.0 + gate * (1.0 - sg)))).astype(BF16)
            ds_ref[...] = jnp.sum((draw * raw).T.reshape(MLA_H, MLA_V, tm), axis=1)

    return pl.pallas_call(
        body, grid=(T // tm,), name="bwd_merge",
        in_specs=[_row(tm, D), _row(tm, 512), _row(tm, 512), _row(tm, 512, Z_AG // 512), _row(tm, 512, Z_BG // 512),
                  _row(tm, D, Z_MA // D), _row(tm, D, Z_MB // D), _row(tm, D), _row(tm, D),
                  _res((512, D)), _res((512, D)), _res((D, D))],
        out_specs=[_row(tm, 512)] * 4 + [_row(tm, D)] * 2 + [pl.BlockSpec((MLA_H, tm), lambda i: (0, i))] * 2
        + [_acc((512, D)), _acc((512, D)), _acc((D, D))],
        out_shape=[SDS((T, 512), F32), SDS((T, 512), F32), SDS((T, 512), BF16), SDS((T, 512), BF16),
                   SDS((T, D), BF16), SDS((T, D), BF16), SDS((MLA_H, T), F32), SDS((MLA_H, T), F32),
                   SDS((512, D), F32), SDS((512, D), F32), SDS((D, D), F32)],
        compiler_params=_cp("arbitrary"))(dx1, oa, ob, z, z, z, z, ua, ub, wa, wb, wo)


def _mla_bwd(qf, kf, v, do, lse, dsum, nb, seq, tq):
    T = qf.shape[0]
    nq = seq // tq
    pw = 2 * LANES

    def body(q_ref, k_ref, v_ref, do_ref, lse_ref, dsum_ref, dq_ref, dk_ref, dv_ref, dk_s, dv_s, dqt_s):
        ki, qi = pl.program_id(2), pl.program_id(3)

        @pl.when((ki == 0) & (qi == 0))
        def _():
            dqt_s[...] = jnp.zeros(dqt_s.shape, F32)

        @pl.when(qi == ki)
        def _():
            dk_s[...] = jnp.zeros(dk_s.shape, F32)
            dv_s[...] = jnp.zeros(dv_s.shape, F32)

        def step(masked):
            if masked:
                keys = lax.broadcasted_iota(jnp.int32, (tq, tq), 0)
                queries = lax.broadcasted_iota(jnp.int32, (tq, tq), 1)
                mask = keys <= queries
            for j in range(2):
                wide = slice(LANES * j, LANES * (j + 1))
                sl = slice(MLA_V * j, MLA_V * (j + 1))
                q, k = q_ref[:, wide], k_ref[:, wide]
                dob = do_ref[:, sl].astype(BF16)
                s = _nt(k, q) * MLA_SCALE
                if masked:
                    s = jnp.where(mask, s, NEG)
                p = jnp.exp(s - lse_ref[j:j + 1, :])
                dv_s[:, sl] += _nn(p.astype(BF16), dob)
                ds = (p * (_nt(v_ref[:, sl], dob) - dsum_ref[j:j + 1, :]) * MLA_SCALE).astype(BF16)
                dk_s[:, wide] += _nn(ds, q)
                dqt_s[qi, wide, :] += _tn(k, ds)

        @pl.when(qi > ki)
        def _():
            step(False)

        @pl.when(qi == ki)
        def _():
            step(True)

        @pl.when(qi == nq - 1)
        def _():
            dk_ref[...] = dk_s[...]
            dv_ref[...] = dv_s[...]

        @pl.when((ki == nq - 1) & (qi == nq - 1))
        def _():
            for n in range(nq):
                dq_ref[tq * n:tq * (n + 1), :] = dqt_s[n].T

    qmap = lambda b, hp, ki, qi: (b * nq + jnp.maximum(qi, ki), hp)
    kmap = lambda b, hp, ki, qi: (b * nq + ki, hp)
    stat = pl.BlockSpec((None, 2, tq), lambda b, hp, ki, qi: (hp, 0, b * nq + jnp.maximum(qi, ki)))
    return pl.pallas_call(
        body, grid=(nb, MLA_H // 2, nq, nq), name="mla_bwd",
        in_specs=[pl.BlockSpec((tq, pw), qmap), pl.BlockSpec((tq, pw), kmap), pl.BlockSpec((tq, LANES), kmap),
                  pl.BlockSpec((tq, LANES), qmap), stat, stat],
        out_specs=[pl.BlockSpec((seq, pw), lambda b, hp, ki, qi: (b, hp)),
                   pl.BlockSpec((tq, pw), kmap), pl.BlockSpec((tq, LANES), kmap)],
        out_shape=[SDS((T, QFW), F32), SDS((T, QFW), F32), SDS((T, MLA_H * MLA_V), F32)],
        scratch_shapes=[pltpu.VMEM((tq, pw), F32), pltpu.VMEM((tq, LANES), F32), pltpu.VMEM((nq, pw, tq), F32)],
        compiler_params=_cp("parallel", "parallel", "arbitrary", "arbitrary"))(qf, kf, v, do, lse, dsum)


def _swa_bwd(sink, z, pos_col, pos_row, do, lse, dsum, nb, seq):
    T = z.shape[0]
    nblk = seq // BLK

    def body(sink_ref, q_ref, kvc_ref, kvp_ref, pcc_ref, pcp_ref, pr_ref, do_ref, lse_ref, dsum_ref,
             dq_ref, dkv_ref, dsink_ref):
        b, n = pl.program_id(0), pl.program_id(1)

        @pl.when((b == 0) & (n == 0))
        def _():
            dsink_ref[...] = jnp.zeros(dsink_ref.shape, F32)

        @pl.when(n == 0)
        def _():
            dkv_ref[...] = jnp.zeros(dkv_ref.shape, F32)

        kb, vb, scores = _swa_scores(n, q_ref, kvc_ref, kvp_ref, pcc_ref, pcp_ref, pr_ref)
        lane = lax.broadcasted_iota(jnp.int32, (1, LANES), 1)
        dsink = jnp.zeros((1, LANES), F32)
        dkv = [[None, None], [None, None]]
        dqs = []
        gsl = lambda h: slice(SWA_DH * (h // (SWA_H // SWA_KV)), SWA_DH * (h // (SWA_H // SWA_KV) + 1))
        qs, ss, dobs, dps = [], [], [], []
        for h in range(SWA_H):
            qh, s = scores(h)
            dob = do_ref[:, SWA_DH * h:SWA_DH * (h + 1)].astype(BF16)
            qs.append(qh)
            ss.append(s)
            dobs.append(dob)
            dps.append(_nt(vb[:, gsl(h)], dob))
        pbs, dss = [], []
        for h in range(SWA_H):
            lse, dsum = lse_ref[h:h + 1, :], dsum_ref[h:h + 1, :]
            p = jnp.exp(ss[h] - lse)
            pbs.append(p.astype(BF16))
            dss.append((p * (dps[h] - dsum) * SWA_SCALE).astype(BF16))
            dsk = jnp.sum(-jnp.exp(sink_ref[h] - lse) * dsum, axis=1, keepdims=True)
            dsink = dsink + jnp.where(lane == h, dsk, 0.0)
        for h in range(SWA_H):
            g = h // (SWA_H // SWA_KV)
            dqs.append(_tn(kb[:, gsl(h)], dss[h]))
            dk, dv = _nn(dss[h], qs[h]), _nn(pbs[h], dobs[h])
            dkv[g][0] = dk if dkv[g][0] is None else dkv[g][0] + dk
            dkv[g][1] = dv if dkv[g][1] is None else dkv[g][1] + dv
        dq_ref[...] = jnp.concatenate(dqs, axis=0).T.astype(BF16)
        dsink_ref[...] += dsink
        upd = jnp.concatenate([dkv[0][0], dkv[1][0], dkv[0][1], dkv[1][1]], axis=1)
        dkv_ref[pl.ds(pl.multiple_of(n * BLK, BLK), BLK), :] += upd[BLK:]

        @pl.when(n > 0)
        def _():
            dkv_ref[pl.ds(pl.multiple_of((n - 1) * BLK, BLK), BLK), :] += upd[:BLK]

    return pl.pallas_call(
        body, grid=(nb, nblk), name="swa_bwd",
        in_specs=_swa_specs(nblk) + [pl.BlockSpec((BLK, 512), lambda b, n: (b * nblk + n, 0))]
        + [pl.BlockSpec((SWA_H, BLK), lambda b, n: (0, b * nblk + n))] * 2,
        out_specs=[pl.BlockSpec((BLK, 512), lambda b, n: (b * nblk + n, 0)),
                   pl.BlockSpec((seq, 2 * BLK), lambda b, n: (b, 0)),
                   pl.BlockSpec((1, LANES), lambda b, n: (0, 0))],
        out_shape=[SDS((T, 512), BF16), SDS((T, 2 * BLK), F32), SDS((1, LANES), F32)],
        compiler_params=_cp("arbitrary", "arbitrary"))(sink, z, z, z, pos_col, pos_col, pos_row, do, lse, dsum)


def _bwd_prep(dq, dk, dv, z, gq, gkv, wq, wkv, tc, ts1, ts2, tm):
    T = z.shape[0]

    def body(dq_ref, dk_ref, dv_ref, qd_ref, kvd_ref, gq_ref, gkv_ref, wq_ref, wkv_ref, c_ref, s1_ref, s2_ref,
             dqd_ref, dkvd_ref, dkr_ref, dwq_ref, dwkv_ref, dgq_ref, dgkv_ref, dqb_s, dkvb_s):
        @pl.when(pl.program_id(0) == 0)
        def _():
            for ref in (dwq_ref, dwkv_ref, dgq_ref, dgkv_ref):
                ref[...] = jnp.zeros(ref.shape, F32)

        c, s1, s2 = c_ref[...], s1_ref[...], s2_ref[...]
        lane = lax.broadcasted_iota(jnp.int32, (1, LANES), 1)
        rope_lanes = (lane >= MLA_NOPE) & (lane < MLA_QK)
        dkb = jnp.zeros((tm, LANES), F32)
        for h in range(MLA_H):
            sl = slice(LANES * h, LANES * (h + 1))
            dqb_s[:, sl] = _rope_t(dq_ref[:, sl], c, s1, s2).astype(BF16)
            dkh = dk_ref[:, sl]
            dkb = dkb + dkh
            dkvb_s[:, sl] = dkh.astype(BF16)
        dkvb_s[:, QFW:] = dv_ref[...].astype(BF16)
        dkr_ref[...] = _rope_t(jnp.where(rope_lanes, dkb, 0.0), c, s1, s2).astype(BF16)

        for (x_ref, g_ref, w_ref, d_s, dx_ref, dw_ref, dg_ref) in (
                (qd_ref, gq_ref, wq_ref, dqb_s, dqd_ref, dwq_ref, dgq_ref),
                (kvd_ref, gkv_ref, wkv_ref, dkvb_s, dkvd_ref, dwkv_ref, dgkv_ref)):
            xf, gf, db = x_ref[...], g_ref[...], d_s[...]
            r = _rstd(xf)
            n = xf * r
            dw_ref[...] += _tn((n * gf).astype(BF16), db)
            dx, dgr = _norm_bwd(_nt(db, w_ref[...]), n, r, gf)
            dx_ref[...] = dx.astype(BF16)
            dg_ref[...] += jnp.sum(dgr, axis=0, keepdims=True)

    return pl.pallas_call(
        body, grid=(T // tm,), name="bwd_prep",
        in_specs=[_row(tm, QFW), _row(tm, QFW), _row(tm, MLA_H * MLA_V),
                  _row(tm, QL, Z_QD // QL), _row(tm, KVL, Z_KVD // KVL),
                  _res((1, QL)), _res((1, KVL)), _res((QL, QFW)), _res((KVL, KVW)),
                  _row(tm, LANES), _row(tm, LANES), _row(tm, LANES)],
        out_specs=[_row(tm, QL), _row(tm, KVL), _row(tm, LANES),
                   _acc((QL, QFW)), _acc((KVL, KVW)), _acc((1, QL)), _acc((1, KVL))],
        out_shape=[SDS((T, QL), BF16), SDS((T, KVL), BF16), SDS((T, LANES), BF16),
                   SDS((QL, QFW), F32), SDS((KVL, KVW), F32), SDS((1, QL), F32), SDS((1, KVL), F32)],
        scratch_shapes=[pltpu.VMEM((tm, QFW), BF16), pltpu.VMEM((tm, KVW), BF16)],
        compiler_params=_cp("arbitrary"))(dq, dk, dv, z, z, gq, gkv, wq, wkv, tc, ts1, ts2)


def _bwd_in(pieces, x, g, dres, w, tm):
    T = x.shape[0]
    widths = [pc.shape[1] for pc in pieces]
    assert sum(widths) == ZW
    n_p = len(pieces)

    def body(*refs):
        p_refs, (x_ref, g_ref, r_ref, w_ref, dx_ref, dz_ref, dg_ref) = refs[:n_p], refs[n_p:]

        @pl.when(pl.program_id(0) == 0)
        def _():
            dg_ref[...] = jnp.zeros(dg_ref.shape, F32)

        off = 0
        for ref, wd in zip(p_refs, widths):
            dz_ref[:, off:off + wd] = ref[...].astype(BF16)
            off += wd
        xf, gf = x_ref[...], g_ref[...]
        r = _rstd(xf)
        n = xf * r
        dx, dgr = _norm_bwd(_nt(dz_ref[...], w_ref[...]), n, r, gf)
        dx_ref[...] = r_ref[...] + dx
        dg_ref[...] += jnp.sum(dgr, axis=0, keepdims=True)

    return pl.pallas_call(
        body, grid=(T // tm,), name="bwd_in",
        in_specs=[_row(tm, wd) for wd in widths] + [_row(tm, D), _res((1, D)), _row(tm, D), _res((D, ZW))],
        out_specs=[_row(tm, D), _row(tm, ZW), _acc((1, D))],
        out_shape=[SDS((T, D), F32), SDS((T, ZW), BF16), SDS((1, D), F32)],
        compiler_params=_cp("arbitrary"))(*pieces, x, g, dres, w)


def _wgrad_in(hb, dzb, tm):
    T = hb.shape[0]
    half = ZW // 2

    def body(h_ref, dz_ref, dw_ref):
        @pl.when(pl.program_id(1) == 0)
        def _():
            dw_ref[...] = jnp.zeros(dw_ref.shape, F32)

        dw_ref[...] += _tn(h_ref[...], dz_ref[...])

    return pl.pallas_call(
        body, grid=(2, T // tm), name="wgrad_in",
        in_specs=[pl.BlockSpec((tm, D), lambda j, t: (t, 0)), pl.BlockSpec((tm, half), lambda j, t: (t, j))],
        out_specs=pl.BlockSpec((D, half), lambda j, t: (0, j)),
        out_shape=SDS((D, ZW), F32),
        compiler_params=_cp("parallel", "arbitrary"))(hb, dzb)


def _win_to_kernel(w):
    s = lambda a, b: w[..., a:b]
    zero = lambda n: jnp.zeros(w.shape[:-1] + (n,), w.dtype)
    return jnp.concatenate([s(2208, 3232), s(3232, 4256), s(0, 512), s(768, 1280), s(1696, 2208), s(1280, 1536),
                            s(512, 640), s(640, 768), s(1536, 1664), zero(64), s(1664, 1696), zero(32)], axis=-1)


def _win_from_kernel(g):
    s = lambda a, n: g[..., a:a + n]
    return jnp.concatenate([s(Z_AQ, 512), s(Z_AK, 128), s(Z_AV, 128), s(Z_AG, 512), s(Z_QD, 256), s(Z_KVD, 128),
                            s(Z_KR + MLA_NOPE, MLA_ROPE), s(Z_BG, 512), s(Z_MA, 1024), s(Z_MB, 1024)], axis=-1)


def _wuq_to_kernel(w):
    w = w.reshape(w.shape[:-1] + (MLA_H, MLA_QK))
    w = jnp.pad(w, [(0, 0)] * (w.ndim - 1) + [(0, LANES - MLA_QK)])
    return w.reshape(w.shape[:-2] + (QFW,))


def _wuq_from_kernel(g):
    g = g.reshape(g.shape[:-1] + (MLA_H, LANES))[..., :MLA_QK]
    return g.reshape(g.shape[:-2] + (MLA_H * MLA_QK,))


def _wukv_to_kernel(w):
    w = w.reshape(w.shape[:-1] + (MLA_H, MLA_NOPE + MLA_V))
    k = jnp.pad(w[..., :MLA_NOPE], [(0, 0)] * (w.ndim - 1) + [(0, LANES - MLA_NOPE)])
    v = w[..., MLA_NOPE:]
    return jnp.concatenate([k.reshape(k.shape[:-2] + (QFW,)), v.reshape(v.shape[:-2] + (MLA_H * MLA_V,))], axis=-1)


def _wukv_from_kernel(g):
    k = g[..., :QFW].reshape(g.shape[:-1] + (MLA_H, LANES))[..., :MLA_NOPE]
    v = g[..., QFW:].reshape(g.shape[:-1] + (MLA_H, MLA_V))
    kv = jnp.concatenate([k, v], axis=-1)
    return kv.reshape(kv.shape[:-2] + (MLA_H * (MLA_NOPE + MLA_V),))


def _rope_tables(pos):
    half = MLA_ROPE // 2
    inv = 10000.0 ** (-jnp.arange(0, MLA_ROPE, 2, dtype=F32) / MLA_ROPE)
    ang = pos.astype(F32)[:, None] * inv
    cos, sin = jnp.cos(ang), jnp.sin(ang)
    one = jnp.ones((pos.shape[0], MLA_NOPE), F32)
    zero = lambda n: jnp.zeros((pos.shape[0], n), F32)
    tc = jnp.concatenate([one, cos, cos, one[:, :LANES - MLA_QK]], axis=1)
    ts1 = jnp.concatenate([zero(MLA_NOPE + half), sin, zero(LANES - MLA_QK)], axis=1)
    ts2 = jnp.concatenate([zero(MLA_NOPE), -sin, zero(LANES - MLA_NOPE - half)], axis=1)
    return tc, ts1, ts2


def _local_step(x, p, positions, loss_target, small, wts):
    nb, seq, _ = x.shape
    T = nb * seq
    tm = min(256, T)
    tq = min(512, seq)
    xf = x.reshape(T, D)
    pos = positions.reshape(T)
    posf = pos.astype(F32)
    pos_col, pos_row = posf.reshape(T, 1), posf.reshape(T // BLK, 1, BLK)
    tc, ts1, ts2 = _rope_tables(pos)

    saved = []
    for i in range(DEPTH):
        w, sm = wts[i], small[i]
        z, hb = _fwd_in(xf, sm["g_mix"], w["w_in"], tm)
        oa, lse_a = _swa_fwd(sm["sink"], z, pos_col, pos_row, nb, seq)
        qf, kf, v = _fwd_prep(z, sm["g_q"], sm["g_kv"], w["w_uq"], w["w_ukv"], tc, ts1, ts2, tm)
        ob, lse_b = _mla_fwd(qf, kf, v, nb, seq, tq)
        x1, ua, ub = _fwd_merge(xf, oa, ob, z, w["w_br_a"], w["w_br_b"], w["w_out"], tm)
        x2, pg, pp = _fwd_ple(x1, p[i].reshape(T, PLE), sm["g_ple"], w["w_ple_gate"], w["w_ple_proj"], tm)
        saved.append(dict(x=xf, z=z, hb=hb, oa=oa, lse_a=lse_a, qf=qf, kf=kf, v=v, ob=ob, lse_b=lse_b,
                          x1=x1, ua=ua, ub=ub, pg=pg, pp=pp))
        xf = x2

    dx, dg_final, loss = _loss_head(xf, small["g_final"], loss_target.reshape(T, D), tm)

    grads = [None] * DEPTH
    for i in reversed(range(DEPTH)):
        w, sm, sv = wts[i], small[i], saved[i]
        dx1, dwpg, dwpp, dg_ple = _bwd_ple(dx, sv["x1"], sv["pg"], sv["pp"], p[i].reshape(T, PLE), sm["g_ple"],
                                           w["w_ple_gate"], tm)
        doa, dob, dag, dbg, dma, dmb, dsum_a, dsum_b, dwa, dwb, dwo = _bwd_merge(
            dx1, sv["oa"], sv["ob"], sv["z"], sv["ua"], sv["ub"], w["w_br_a"], w["w_br_b"], w["w_out"], tm)
        dq_b, dk_b, dv_b = _mla_bwd(sv["qf"], sv["kf"], sv["v"], dob, sv["lse_b"],
                                    dsum_b.reshape(MLA_H // 2, 2, T), nb, seq, tq)
        dqd, dkvd, dkr, dwq, dwkv, dgq, dgkv = _bwd_prep(dq_b, dk_b, dv_b, sv["z"], sm["g_q"], sm["g_kv"],
                                                         w["w_uq"], w["w_ukv"], tc, ts1, ts2, tm)
        dq_a, dkv_a, dsink = _swa_bwd(sm["sink"], sv["z"], pos_col, pos_row, doa, sv["lse_a"], dsum_a, nb, seq)
        dx, dzb, dg_mix = _bwd_in([dma, dmb, dq_a, dag, dbg, dqd, dkv_a, dkvd, dkr], sv["x"], sm["g_mix"], dx1,
                                  w["w_in"], tm)
        dwin = _wgrad_in(sv["hb"], dzb, tm)
        grads[i] = dict(g_mix=dg_mix[0], w_in=_win_from_kernel(dwin), sink=dsink[0, :SWA_H], g_q=dgq[0],
                        w_uq=_wuq_from_kernel(dwq), g_kv=dgkv[0], w_ukv=_wukv_from_kernel(dwkv), w_br_a=dwa,
                        w_br_b=dwb, w_out=dwo, g_ple=dg_ple[0], w_ple_gate=dwpg, w_ple_proj=dwpp)
    return loss, dx.reshape(nb, seq, D), grads, dg_final[0]


def _kernel_weights(full):
    wi, wq, wkv = _win_to_kernel(full["w_in"]), _wuq_to_kernel(full["w_uq"]), _wukv_to_kernel(full["w_ukv"])
    return [dict(w_in=wi[i], w_uq=wq[i], w_ukv=wkv[i], w_br_a=full["w_br_a"][i], w_br_b=full["w_br_b"][i],
                 w_out=full["w_out"][i], w_ple_gate=full["w_ple_gate"][i], w_ple_proj=full["w_ple_proj"][i])
            for i in range(DEPTH)]


def _small_params(g_mix, sink, g_q, g_kv, g_ple, g_final):
    small = {i: dict(g_mix=g_mix[i][None], sink=sink[i], g_q=g_q[i][None], g_kv=g_kv[i][None], g_ple=g_ple[i][None])
             for i in range(DEPTH)}
    small["g_final"] = g_final[None]
    return small


WIDE_ROWS, WIDE_W = DEPTH * D, IN_W // N_DEV
UQ_W = MLA_H * MLA_QK // N_DEV
ROWS_PIECES = (("w_uq", DEPTH * QL), ("w_ukv", DEPTH * KVL), ("w_br_a", DEPTH * 512), ("w_br_b", DEPTH * 512),
               ("w_out", DEPTH * D), ("w_ple_gate", DEPTH * D), ("w_ple_proj", DEPTH * PLE))
SMALL = (("g_mix", (DEPTH, D)), ("sink", (DEPTH, SWA_H)), ("g_q", (DEPTH, QL)), ("g_kv", (DEPTH, KVL)),
         ("g_ple", (DEPTH, D)), ("g_final", (D,)))
VEC_ROWS = 48
ROWS_N = sum(r for _, r in ROWS_PIECES)
WIDE_TILE, ROWS_TILE = 256, ROWS_N // 4


def _to_rows(name, a):
    lead = a.shape[:-3]
    if name == "w_uq":
        a = jnp.pad(a, [(0, 0)] * (a.ndim - 1) + [(0, LANES - UQ_W)])
    return a.reshape(lead + (-1, LANES))


def _from_rows(name, r):
    lead = r.shape[:-2]
    if name in ("w_out", "w_ple_gate"):
        return r.reshape(lead + (DEPTH, D // N_DEV, D))
    r = r.reshape(lead + (DEPTH, -1, LANES))
    return r[..., :UQ_W] if name == "w_uq" else r


def _pack_rows(blocks):
    return jnp.concatenate([_to_rows(n, blocks[n]) for n, _ in ROWS_PIECES], axis=-2)


def _unpack_rows(rows):
    blocks, off = {}, 0
    for n, r in ROWS_PIECES:
        blocks[n] = _from_rows(n, rows[..., off:off + r, :])
        off += r
    return blocks


def _pack_vec(vectors):
    vec = jnp.concatenate([vectors[n].reshape(-1) for n, _ in SMALL])
    return jnp.pad(vec, (0, VEC_ROWS * LANES - vec.shape[0])).reshape(VEC_ROWS, LANES)


def _unpack_vec(vec):
    vec = vec.reshape(-1)
    vectors, off = {}, 0
    for n, shp in SMALL:
        size = 1
        for s in shp:
            size *= s
        vectors[n] = vec[off:off + size].reshape(shp)
        off += size
    return vectors


def _join(name, blocks):
    if name in ("w_out", "w_ple_gate"):
        return jnp.moveaxis(blocks, 0, 1).reshape(DEPTH, -1, blocks.shape[-1])
    return jnp.moveaxis(blocks, 0, 2).reshape(DEPTH, blocks.shape[2], -1)


def _split(name, full):
    if name in ("w_out", "w_ple_gate"):
        return jnp.moveaxis(full.reshape(DEPTH, N_DEV, -1, full.shape[-1]), 1, 0)
    return jnp.moveaxis(full.reshape(DEPTH, full.shape[1], N_DEV, -1), 2, 0)


MESH_ID = pl.DeviceIdType.MESH
ANY = pl.BlockSpec(memory_space=pl.ANY)


def _place():
    return lax.axis_index("x"), lax.axis_index("y"), lax.axis_index("c")


def _all_gather(blocks):
    n = len(blocks)

    def body(*refs):
        x_refs, out_refs, (send_sems, recv_sems, local_sems) = refs[:n], refs[n:2 * n], refs[2 * n:]
        x, y, c = _place()
        me, sibling = (x, y, c), (x, y, 1 - c)
        chips = [(1 - x, y), (x, 1 - y), (1 - x, 1 - y)]

        def slot(a, px, py, pc):
            return out_refs[a].at[4 * px + 2 * py + pc]

        def copy(a, k, blk, to, src=None):
            return pltpu.make_async_remote_copy(
                src_ref=slot(a, *blk) if src is None else src, dst_ref=slot(a, *blk),
                send_sem=send_sems.at[7 * a + k], recv_sem=recv_sems.at[7 * a + k], device_id=to,
                device_id_type=MESH_ID)

        mine = [pltpu.make_async_copy(x_refs[a], slot(a, *me), local_sems.at[a]) for a in range(n)]
        for cp in mine:
            cp.start()
        first = []
        for a in range(n):
            first += [copy(a, 0, me, sibling, src=x_refs[a])]
            first += [copy(a, 1 + j, me, (*chip, c), src=x_refs[a]) for j, chip in enumerate(chips)]
        for cp in first:
            cp.start()
        passed = []
        for j, chip in enumerate(chips):
            for a in range(n):
                copy(a, 1 + j, (*chip, c), me).wait_recv()
                passed.append(copy(a, 4 + j, (*chip, c), sibling))
                passed[-1].start()
        for a in range(n):
            copy(a, 0, sibling, me).wait_recv()
            for j, chip in enumerate(chips):
                copy(a, 4 + j, (*chip, 1 - c), me).wait_recv()
        for cp in first + passed:
            cp.wait_send()
        for cp in mine:
            cp.wait()

    return pl.pallas_call(
        body, name="all_gather_weights",
        out_shape=[SDS((N_DEV,) + b.shape, b.dtype) for b in blocks],
        in_specs=[ANY] * n, out_specs=[ANY] * n,
        scratch_shapes=[pltpu.SemaphoreType.DMA((7 * n,)), pltpu.SemaphoreType.DMA((7 * n,)),
                        pltpu.SemaphoreType.DMA((n,))],
    )(*blocks)


def _swap_sibling(arrs):
    n = len(arrs)

    def body(*refs):
        a_refs, out_refs, (send_sems, recv_sems) = refs[:n], refs[n:2 * n], refs[2 * n:]
        x, y, c = _place()
        copies = [pltpu.make_async_remote_copy(
            src_ref=a_refs[a].at[:, 1 - c], dst_ref=out_refs[a], send_sem=send_sems.at[a], recv_sem=recv_sems.at[a],
            device_id=(x, y, 1 - c), device_id_type=MESH_ID) for a in range(n)]
        for cp in copies:
            cp.start()
        for cp in copies:
            cp.wait()

    return pl.pallas_call(
        body, name="swap_sibling", out_shape=[SDS((a.shape[0],) + a.shape[2:], a.dtype) for a in arrs],
        in_specs=[ANY] * n, out_specs=[ANY] * n,
        scratch_shapes=[pltpu.SemaphoreType.DMA((n,)), pltpu.SemaphoreType.DMA((n,))],
    )(*arrs)


def _exchange_chips(arrs):
    n = len(arrs)

    def body(*refs):
        p_refs, out_refs, (send_sems, recv_sems, local_sems) = refs[:n], refs[n:2 * n], refs[2 * n:]
        x, y, c = _place()
        mine = 2 * x + y
        local = [pltpu.make_async_copy(p_refs[a].at[mine], out_refs[a].at[mine], local_sems.at[a]) for a in range(n)]
        for cp in local:
            cp.start()
        peers = [(1 - x, y), (x, 1 - y), (1 - x, 1 - y)]

        def copy(a, j, src_chip, dst_chip):
            px, py = peers[j]
            return pltpu.make_async_remote_copy(
                src_ref=p_refs[a].at[src_chip], dst_ref=out_refs[a].at[dst_chip], send_sem=send_sems.at[3 * a + j],
                recv_sem=recv_sems.at[3 * a + j], device_id=(px, py, c), device_id_type=MESH_ID)

        copies = [copy(a, j, 2 * px + py, mine) for a in range(n) for j, (px, py) in enumerate(peers)]
        for cp in copies:
            cp.start()
        for a in range(n):
            for j, (px, py) in enumerate(peers):
                copy(a, j, mine, 2 * px + py).wait_recv()
        for cp in copies:
            cp.wait_send()
        for cp in local:
            cp.wait()

    return pl.pallas_call(
        body, name="exchange_chips", out_shape=[SDS(a.shape, a.dtype) for a in arrs],
        in_specs=[ANY] * n, out_specs=[ANY] * n,
        scratch_shapes=[pltpu.SemaphoreType.DMA((3 * n,)), pltpu.SemaphoreType.DMA((3 * n,)),
                        pltpu.SemaphoreType.DMA((n,))],
    )(*arrs)


def _add_mine(g, recv, core, tile, dtype):
    _, _, rows, width = g.shape

    def body(c_ref, g_ref, r_ref, o_ref):
        o_ref[...] = (g_ref[...] + r_ref[...]).astype(dtype)

    spec = pl.BlockSpec((None, tile, width), lambda k, i, c_ref: (k, i, 0))
    return pl.pallas_call(
        body, name="add_sibling", out_shape=SDS(recv.shape, dtype),
        grid_spec=pltpu.PrefetchScalarGridSpec(
            num_scalar_prefetch=1, grid=(g.shape[0], rows // tile),
            in_specs=[pl.BlockSpec((None, None, tile, width), lambda k, i, c_ref: (k, c_ref[0], i, 0)), spec],
            out_specs=spec),
        compiler_params=_cp("parallel", "parallel"))(core, g, recv)


def _sum_adamw(parts, w, m, v, tile):
    rows, width = w.shape

    def body(p_ref, w_ref, m_ref, v_ref, g_ref, d_ref, nm_ref, nv_ref):
        g = ((p_ref[0].astype(F32) + p_ref[1].astype(F32)) + p_ref[2].astype(F32)) + p_ref[3].astype(F32)
        nm = ADAM_B1 * m_ref[...] + (1.0 - ADAM_B1) * g
        nv = ADAM_B2 * v_ref[...] + (1.0 - ADAM_B2) * jnp.square(g)
        m_hat = nm / (1.0 - ADAM_B1 ** ADAM_STEP)
        v_hat = nv / (1.0 - ADAM_B2 ** ADAM_STEP)
        g_ref[...] = g
        nm_ref[...] = nm
        nv_ref[...] = nv
        d_ref[...] = -ADAM_LR * (m_hat / (jnp.sqrt(v_hat) + ADAM_EPS) + ADAM_WD * w_ref[...])

    spec = pl.BlockSpec((tile, width), lambda i: (i, 0))
    return pl.pallas_call(
        body, grid=(rows // tile,), name="sum_adamw",
        in_specs=[pl.BlockSpec((4, tile, width), lambda i: (0, i, 0)), spec, spec, spec],
        out_specs=[spec] * 4, out_shape=[SDS((rows, width), F32)] * 4,
        compiler_params=_cp("parallel"))(parts, w, m, v)


def kernel(x, p, positions, g_mix, w_in, sink, g_q, w_uq, g_kv, w_ukv, w_br_a, w_br_b, w_out, g_ple, w_ple_gate, w_ple_proj, g_final, loss_target, m_g_mix, m_w_in, m_sink, m_g_q, m_w_uq, m_g_kv, m_w_ukv, m_w_br_a, m_w_br_b, m_w_out, m_g_ple, m_w_ple_gate, m_w_ple_proj, m_g_final, v_g_mix, v_w_in, v_sink, v_g_q, v_w_uq, v_g_kv, v_w_ukv, v_w_br_a, v_w_br_b, v_w_out, v_g_ple, v_w_ple_gate, v_w_ple_proj, v_g_final):
    weights = dict(g_mix=g_mix, w_in=w_in, sink=sink, g_q=g_q, w_uq=w_uq, g_kv=g_kv, w_ukv=w_ukv, w_br_a=w_br_a,
                   w_br_b=w_br_b, w_out=w_out, g_ple=g_ple, w_ple_gate=w_ple_gate, w_ple_proj=w_ple_proj,
                   g_final=g_final)
    mom1 = dict(g_mix=m_g_mix, w_in=m_w_in, sink=m_sink, g_q=m_g_q, w_uq=m_w_uq, g_kv=m_g_kv, w_ukv=m_w_ukv,
                w_br_a=m_w_br_a, w_br_b=m_w_br_b, w_out=m_w_out, g_ple=m_g_ple, w_ple_gate=m_w_ple_gate,
                w_ple_proj=m_w_ple_proj, g_final=m_g_final)
    mom2 = dict(g_mix=v_g_mix, w_in=v_w_in, sink=v_sink, g_q=v_g_q, w_uq=v_w_uq, g_kv=v_g_kv, w_ukv=v_w_ukv,
                w_br_a=v_w_br_a, w_br_b=v_w_br_b, w_out=v_w_out, g_ple=v_g_ple, w_ple_gate=v_w_ple_gate,
                w_ple_proj=v_w_ple_proj, g_final=v_g_final)
    wide = lambda d: d["w_in"].reshape(WIDE_ROWS, WIDE_W)
    rows = lambda d: _pack_rows(d)

    got_wide, got_rows = _all_gather([wide(weights).astype(BF16), rows(weights).astype(BF16)])
    blocks = _unpack_rows(got_rows)
    full = {n: _join(n, blocks[n]) for n, _ in ROWS_PIECES}
    full["w_in"] = _join("w_in", got_wide.reshape(N_DEV, DEPTH, D, WIDE_W))
    wts = _kernel_weights(full)
    small = _small_params(g_mix, sink, g_q, g_kv, g_ple, g_final)

    loss, grad_x, grads, dg_final = _local_step(x, p, positions, loss_target, small, wts)
    loss = lax.psum(loss[0, 0], ("x", "y", "c"))

    stacked = {n: jnp.stack([grads[i][n] for i in range(DEPTH)]) for n in grads[0]}
    stacked["g_final"] = dg_final
    pay_wide = _split("w_in", stacked["w_in"]).reshape(N_DEV // 2, 2, WIDE_ROWS, WIDE_W)
    pay_rows = _pack_rows({n: _split(n, stacked[n]) for n, _ in ROWS_PIECES})
    pay_rows = pay_rows.reshape(N_DEV // 2, 2, ROWS_N, LANES)
    pay_vec = jnp.broadcast_to(_pack_vec(stacked), (N_DEV // 2, 2, VEC_ROWS, LANES))
    core = lax.axis_index("c").astype(jnp.int32).reshape(1)
    from_sibling = _swap_sibling([pay_wide, pay_rows, pay_vec])
    chip_partial = [_add_mine(pay_wide, from_sibling[0], core, WIDE_TILE, BF16),
                    _add_mine(pay_rows, from_sibling[1], core, ROWS_TILE, BF16),
                    _add_mine(pay_vec, from_sibling[2], core, VEC_ROWS, F32)]
    parts_wide, parts_rows, parts_vec = _exchange_chips(chip_partial)
    out_wide = _sum_adamw(parts_wide, wide(weights), wide(mom1), wide(mom2), WIDE_TILE)
    out_rows = _sum_adamw(parts_rows, rows(weights), rows(mom1), rows(mom2), ROWS_TILE)
    out_vec = _sum_adamw(parts_vec, _pack_vec(weights), _pack_vec(mom1), _pack_vec(mom2), VEC_ROWS)

    outs = []
    for ow, orow, ovec in zip(out_wide, out_rows, out_vec):
        named = _unpack_rows(orow)
        named.update(_unpack_vec(ovec))
        named["w_in"] = ow.reshape(DEPTH, D, WIDE_W)
        outs += [named[n] for n in weights]
    return (loss, grad_x, *outs)
```

```python
import functools

import jax
import jax.numpy as jnp
from jax import lax
from jax.experimental import pallas as pl
from jax.experimental.pallas import tpu as pltpu

F32, BF16 = jnp.float32, jnp.bfloat16
SDS = jax.ShapeDtypeStruct

D = 1024
DEPTH = 2
PLE = 256
BLK = 128
EPS = 1e-6
NEG = -1e30
SWA_H, SWA_KV, SWA_DH = 8, 2, 64
MLA_H, MLA_NOPE, MLA_ROPE, MLA_V = 8, 64, 32, 64
MLA_QK = MLA_NOPE + MLA_ROPE
QL, KVL = 256, 128
IN_W = 4256
N_DEV = 8

V7X_VMEM_BYTES = 64 * 1024 * 1024
LANES = 128
VMEM_LIMIT = V7X_VMEM_BYTES * 7 // 8

ZW = 4352
Z_MA, Z_MB, Z_AQ, Z_AG, Z_BG, Z_QD, Z_AK, Z_AV, Z_KVD, Z_KR = 0, 1024, 2048, 2560, 3072, 3584, 3840, 3968, 4096, 4224
QFW = MLA_H * LANES
KVW = QFW + MLA_H * MLA_V
MLA_SCALE = MLA_QK ** -0.5
SWA_SCALE = SWA_DH ** -0.5
ROLL_UP, ROLL_DOWN = MLA_ROPE // 2, LANES - MLA_ROPE // 2

ADAM_LR, ADAM_B1, ADAM_B2, ADAM_EPS, ADAM_WD, ADAM_STEP = 0.001, 0.9, 0.999, 1e-08, 0.01, 10

FLAT_W = 1024


def _cp(*sem):
    return pltpu.CompilerParams(dimension_semantics=sem, vmem_limit_bytes=VMEM_LIMIT)


def _row(tm, w, col=0):
    return pl.BlockSpec((tm, w), lambda i: (i, col))


def _res(shape):
    return pl.BlockSpec(shape, lambda *_: (0,) * len(shape), pipeline_mode=pl.Buffered(1))


def _acc(shape):
    return pl.BlockSpec(shape, lambda *_: (0,) * len(shape))


def _rstd(xf):
    return lax.rsqrt(jnp.mean(xf * xf, axis=-1, keepdims=True) + EPS)


def _norm_bwd(dh, n, r, g):
    dn = dh * g
    return r * (dn - n * jnp.mean(dn * n, axis=-1, keepdims=True)), dh * n


def _nt(a, b):
    return lax.dot_general(a, b, (((1,), (1,)), ((), ())), preferred_element_type=F32)


def _tn(a, b):
    return lax.dot_general(a, b, (((0,), (0,)), ((), ())), preferred_element_type=F32)


def _nn(a, b):
    return jnp.dot(a, b, preferred_element_type=F32)


def _sig(x):
    return jax.nn.sigmoid(x)


def _rope(t, c, s1, s2):
    return t * c + pltpu.roll(t, ROLL_UP, 1) * s1 + pltpu.roll(t, ROLL_DOWN, 1) * s2


def _rope_t(d, c, s1, s2):
    return d * c + pltpu.roll(d * s1, ROLL_DOWN, 1) + pltpu.roll(d * s2, ROLL_UP, 1)


def _fwd_in(x, g, w, tm):
    T = x.shape[0]

    def body(x_ref, g_ref, w_ref, z_ref, h_ref):
        xf = x_ref[...]
        h = ((xf * _rstd(xf)) * g_ref[...]).astype(BF16)
        h_ref[...] = h
        z_ref[...] = _nn(h, w_ref[...])

    return pl.pallas_call(
        body, grid=(T // tm,), name="fwd_in",
        in_specs=[_row(tm, D), _res((1, D)), _res((D, ZW))],
        out_specs=[_row(tm, ZW), _row(tm, D)],
        out_shape=[SDS((T, ZW), F32), SDS((T, D), BF16)],
        compiler_params=_cp("parallel"))(x, g, w)


def _fwd_prep(z, gq, gkv, wq, wkv, tc, ts1, ts2, tm):
    T = z.shape[0]

    def body(qd_ref, kvd_ref, kr_ref, gq_ref, gkv_ref, wq_ref, wkv_ref, c_ref, s1_ref, s2_ref, q_ref, k_ref, v_ref):
        qd, kvd = qd_ref[...], kvd_ref[...]
        hq = ((qd * _rstd(qd)) * gq_ref[...]).astype(BF16)
        hkv = ((kvd * _rstd(kvd)) * gkv_ref[...]).astype(BF16)
        qf = _nn(hq, wq_ref[...])
        kvf = _nn(hkv, wkv_ref[...])
        c, s1, s2 = c_ref[...], s1_ref[...], s2_ref[...]
        krb = _rope(kr_ref[...], c, s1, s2)
        for h in range(MLA_H):
            sl = slice(LANES * h, LANES * (h + 1))
            q_ref[:, sl] = _rope(qf[:, sl], c, s1, s2).astype(BF16)
            k_ref[:, sl] = (kvf[:, sl] + krb).astype(BF16)
        v_ref[...] = kvf[:, QFW:].astype(BF16)

    return pl.pallas_call(
        body, grid=(T // tm,), name="fwd_prep",
        in_specs=[_row(tm, QL, Z_QD // QL), _row(tm, KVL, Z_KVD // KVL), _row(tm, LANES, Z_KR // LANES),
                  _res((1, QL)), _res((1, KVL)), _res((QL, QFW)), _res((KVL, KVW)),
                  _row(tm, LANES), _row(tm, LANES), _row(tm, LANES)],
        out_specs=[_row(tm, QFW), _row(tm, QFW), _row(tm, MLA_H * MLA_V)],
        out_shape=[SDS((T, QFW), BF16), SDS((T, QFW), BF16), SDS((T, MLA_H * MLA_V), BF16)],
        compiler_params=_cp("parallel"))(z, z, z, gq, gkv, wq, wkv, tc, ts1, ts2)


def _mla_fwd(qf, kf, v, nb, seq, tq):
    T = qf.shape[0]
    nq = seq // tq
    pw = 2 * LANES
    pairs = [(qi, ki) for qi in range(nq) for ki in range(qi + 1)]
    qi_tab = jnp.array([qk[0] for qk in pairs], jnp.int32)
    ki_tab = jnp.array([qk[1] for qk in pairs], jnp.int32)

    def body(qi_ref, ki_ref, q_ref, k_ref, v_ref, o_ref, lse_ref, m_s, l_s, acc_s):
        qi, ki = qi_ref[pl.program_id(2)], ki_ref[pl.program_id(2)]

        @pl.when(ki == 0)
        def _():
            m_s[...] = jnp.full(m_s.shape, NEG, F32)
            l_s[...] = jnp.zeros(l_s.shape, F32)
            acc_s[...] = jnp.zeros(acc_s.shape, F32)

        def step(masked):
            if masked:
                keys = lax.broadcasted_iota(jnp.int32, (tq, tq), 0)
                queries = lax.broadcasted_iota(jnp.int32, (tq, tq), 1)
                mask = keys <= queries
            ss = []
            for j in range(2):
                wide = slice(LANES * j, LANES * (j + 1))
                s = _nt(k_ref[:, wide], q_ref[:, wide]) * MLA_SCALE
                ss.append(jnp.where(mask, s, NEG) if masked else s)
            ps, alphas = [], []
            for j in range(2):
                m_prev = m_s[j]
                m_new = jnp.maximum(m_prev, jnp.max(ss[j], axis=0, keepdims=True))
                alpha = jnp.exp(m_prev - m_new)
                p = jnp.exp(ss[j] - m_new)
                l_s[j] = alpha * l_s[j] + jnp.sum(p, axis=0, keepdims=True)
                m_s[j] = m_new
                ps.append(p.astype(BF16))
                alphas.append(alpha)
            for j in range(2):
                rows = slice(MLA_V * j, MLA_V * (j + 1))
                acc_s[rows, :] = alphas[j] * acc_s[rows, :] + _tn(v_ref[:, rows], ps[j])

        @pl.when(ki < qi)
        def _():
            step(False)

        @pl.when(ki == qi)
        def _():
            step(True)
            for j in range(2):
                rows = slice(MLA_V * j, MLA_V * (j + 1))
                acc_s[rows, :] = acc_s[rows, :] / l_s[j]
                lse_ref[j:j + 1, :] = m_s[j] + jnp.log(l_s[j])
            o_ref[...] = acc_s[...].T

    q_map = lambda b, hp, s, qi_ref, ki_ref: (b * nq + qi_ref[s], hp)
    kv_map = lambda b, hp, s, qi_ref, ki_ref: (b * nq + ki_ref[s], hp)
    return pl.pallas_call(
        body, name="mla_fwd",
        grid_spec=pltpu.PrefetchScalarGridSpec(
            num_scalar_prefetch=2, grid=(nb, MLA_H // 2, len(pairs)),
            in_specs=[pl.BlockSpec((tq, pw), q_map), pl.BlockSpec((tq, pw), kv_map),
                      pl.BlockSpec((tq, LANES), kv_map)],
            out_specs=[pl.BlockSpec((tq, LANES), q_map),
                       pl.BlockSpec((None, 2, tq), lambda b, hp, s, qi_ref, ki_ref: (hp, 0, b * nq + qi_ref[s]))],
            scratch_shapes=[pltpu.VMEM((2, 1, tq), F32), pltpu.VMEM((2, 1, tq), F32),
                            pltpu.VMEM((LANES, tq), F32)]),
        out_shape=[SDS((T, MLA_H * MLA_V), F32), SDS((MLA_H // 2, 2, T), F32)],
        compiler_params=_cp("parallel", "parallel", "arbitrary"))(qi_tab, ki_tab, qf, kf, v)


def _swa_specs(nblk):
    cur = lambda b, n: (b * nblk + n, 0)
    prev = lambda b, n: (b * nblk + jnp.maximum(n - 1, 0), 0)
    kvc = Z_AK // (2 * BLK)
    return [pl.BlockSpec(memory_space=pltpu.SMEM),
            pl.BlockSpec((BLK, 512), lambda b, n: (b * nblk + n, Z_AQ // 512)),
            pl.BlockSpec((BLK, 2 * BLK), lambda b, n: (b * nblk + n, kvc)),
            pl.BlockSpec((BLK, 2 * BLK), lambda b, n: (b * nblk + jnp.maximum(n - 1, 0), kvc)),
            pl.BlockSpec((BLK, 1), cur),
            pl.BlockSpec((BLK, 1), prev),
            pl.BlockSpec((1, 1, BLK), lambda b, n: (b * nblk + n, 0, 0))]


def _swa_scores(n, q_ref, kvc_ref, kvp_ref, pcc_ref, pcp_ref, pr_ref):
    kv = jnp.concatenate([kvp_ref[...], kvc_ref[...]], axis=0)
    kb, vb = kv[:, :BLK].astype(BF16), kv[:, BLK:].astype(BF16)
    dist = pr_ref[0] - jnp.concatenate([pcp_ref[...], pcc_ref[...]], axis=0)
    key = lax.broadcasted_iota(jnp.int32, (2 * BLK, BLK), 0)
    qry = lax.broadcasted_iota(jnp.int32, (2 * BLK, BLK), 1)
    valid = (key > qry) & (key <= qry + BLK) & ((key >= BLK) | (n > 0))

    def scores(h):
        g = h // (SWA_H // SWA_KV)
        qh = q_ref[:, SWA_DH * h:SWA_DH * (h + 1)].astype(BF16)
        s = _nt(kb[:, SWA_DH * g:SWA_DH * (g + 1)], qh) * SWA_SCALE - (2.0 ** -(h + 1)) * dist
        return qh, jnp.where(valid, s, NEG)

    return kb, vb, scores


def _swa_fwd(sink, z, pos_col, pos_row, nb, seq):
    T = z.shape[0]
    nblk = seq // BLK

    def body(sink_ref, q_ref, kvc_ref, kvp_ref, pcc_ref, pcp_ref, pr_ref, o_ref, lse_ref):
        kb, vb, scores = _swa_scores(pl.program_id(1), q_ref, kvc_ref, kvp_ref, pcc_ref, pcp_ref, pr_ref)
        ss = [scores(h)[1] for h in range(SWA_H)]
        es, dens = [], []
        for h in range(SWA_H):
            sk = sink_ref[h]
            m = jnp.maximum(jnp.max(ss[h], axis=0, keepdims=True), sk)
            e = jnp.exp(ss[h] - m)
            den = jnp.sum(e, axis=0, keepdims=True) + jnp.exp(sk - m)
            lse_ref[h:h + 1, :] = m + jnp.log(den)
            es.append(e.astype(BF16))
            dens.append(den)
        outs = []
        for h in range(SWA_H):
            g = h // (SWA_H // SWA_KV)
            outs.append(_tn(vb[:, SWA_DH * g:SWA_DH * (g + 1)], es[h]) / dens[h])
        o_ref[...] = jnp.concatenate(outs, axis=0).T

    return pl.pallas_call(
        body, grid=(nb, nblk), name="swa_fwd",
        in_specs=_swa_specs(nblk),
        out_specs=[pl.BlockSpec((BLK, 512), lambda b, n: (b * nblk + n, 0)),
                   pl.BlockSpec((SWA_H, BLK), lambda b, n: (0, b * nblk + n))],
        out_shape=[SDS((T, 512), F32), SDS((SWA_H, T), F32)],
        compiler_params=_cp("parallel", "parallel"))(sink, z, z, z, pos_col, pos_col, pos_row)


def _fwd_merge(x, oa, ob, z, wa, wb, wo, tm):
    T = x.shape[0]

    def body(x_ref, oa_ref, ob_ref, ag_ref, bg_ref, ma_ref, mb_ref, wa_ref, wb_ref, wo_ref, x1_ref, ua_ref, ub_ref):
        ag, bg = ag_ref[...], bg_ref[...]
        ua = _nn((oa_ref[...] * (ag * _sig(ag))).astype(BF16), wa_ref[...])
        ub = _nn((ob_ref[...] * (bg * _sig(bg))).astype(BF16), wb_ref[...])
        ua_ref[...] = ua
        ub_ref[...] = ub
        y = _sig(ma_ref[...]) * ua + _sig(mb_ref[...]) * ub
        x1_ref[...] = x_ref[...] + _nn(y.astype(BF16), wo_ref[...])

    return pl.pallas_call(
        body, grid=(T // tm,), name="fwd_merge",
        in_specs=[_row(tm, D), _row(tm, 512), _row(tm, 512), _row(tm, 512, Z_AG // 512), _row(tm, 512, Z_BG // 512),
                  _row(tm, D, Z_MA // D), _row(tm, D, Z_MB // D), _res((512, D)), _res((512, D)), _res((D, D))],
        out_specs=[_row(tm, D)] * 3,
        out_shape=[SDS((T, D), F32)] * 3,
        compiler_params=_cp("parallel"))(x, oa, ob, z, z, z, z, wa, wb, wo)


def _fwd_ple(x1, p, g, wpg, wpp, tm):
    T = x1.shape[0]

    def body(x_ref, p_ref, g_ref, wpg_ref, wpp_ref, x2_ref, pg_ref, pp_ref):
        xf = x_ref[...]
        h1 = ((xf * _rstd(xf)) * g_ref[...]).astype(BF16)
        pg = _sig(_nn(h1, wpg_ref[...]))
        pp = _nn(p_ref[...].astype(BF16), wpp_ref[...])
        pg_ref[...] = pg
        pp_ref[...] = pp
        x2_ref[...] = xf + pg * pp

    return pl.pallas_call(
        body, grid=(T // tm,), name="fwd_ple",
        in_specs=[_row(tm, D), _row(tm, PLE), _res((1, D)), _res((D, D)), _res((PLE, D))],
        out_specs=[_row(tm, D)] * 3,
        out_shape=[SDS((T, D), F32)] * 3,
        compiler_params=_cp("parallel"))(x1, p, g, wpg, wpp)


def _loss_head(x, g, tgt, tm):
    T = x.shape[0]

    def body(x_ref, g_ref, t_ref, dx_ref, dg_ref, loss_ref):
        @pl.when(pl.program_id(0) == 0)
        def _():
            dg_ref[...] = jnp.zeros(dg_ref.shape, F32)
            loss_ref[...] = jnp.zeros(loss_ref.shape, F32)

        xf, gf = x_ref[...], g_ref[...]
        r = _rstd(xf)
        n = xf * r
        err = n * gf - t_ref[...]
        loss_ref[...] += 0.5 * jnp.sum(jnp.mean(err * err, axis=-1, keepdims=True), axis=0, keepdims=True)
        dx, dgr = _norm_bwd(err * (1.0 / D), n, r, gf)
        dx_ref[...] = dx
        dg_ref[...] += jnp.sum(dgr, axis=0, keepdims=True)

    return pl.pallas_call(
        body, grid=(T // tm,), name="loss_head",
        in_specs=[_row(tm, D), _res((1, D)), _row(tm, D)],
        out_specs=[_row(tm, D), _acc((1, D)), _acc((1, LANES))],
        out_shape=[SDS((T, D), F32), SDS((1, D), F32), SDS((1, LANES), F32)],
        compiler_params=_cp("arbitrary"))(x, g, tgt)


def _bwd_ple(dx2, x1, pg, pp, p, g, wpg, tm):
    T = x1.shape[0]

    def body(d_ref, x_ref, pg_ref, pp_ref, p_ref, g_ref, w_ref, dx_ref, dwg_ref, dwp_ref, dg_ref):
        @pl.when(pl.program_id(0) == 0)
        def _():
            dwg_ref[...] = jnp.zeros(dwg_ref.shape, F32)
            dwp_ref[...] = jnp.zeros(dwp_ref.shape, F32)
            dg_ref[...] = jnp.zeros(dg_ref.shape, F32)

        d, xf, pg, gf = d_ref[...], x_ref[...], pg_ref[...], g_ref[...]
        r = _rstd(xf)
        n = xf * r
        dpgl = (d * pp_ref[...] * pg * (1.0 - pg)).astype(BF16)
        dwg_ref[...] += _tn((n * gf).astype(BF16), dpgl)
        dwp_ref[...] += _tn(p_ref[...].astype(BF16), (d * pg).astype(BF16))
        dxn, dgr = _norm_bwd(_nt(dpgl, w_ref[...]), n, r, gf)
        dx_ref[...] = d + dxn
        dg_ref[...] += jnp.sum(dgr, axis=0, keepdims=True)

    return pl.pallas_call(
        body, grid=(T // tm,), name="bwd_ple",
        in_specs=[_row(tm, D)] * 4 + [_row(tm, PLE), _res((1, D)), _res((D, D))],
        out_specs=[_row(tm, D), _acc((D, D)), _acc((PLE, D)), _acc((1, D))],
        out_shape=[SDS((T, D), F32), SDS((D, D), F32), SDS((PLE, D), F32), SDS((1, D), F32)],
        compiler_params=_cp("arbitrary"))(dx2, x1, pg, pp, p, g, wpg)


def _bwd_merge(dx1, oa, ob, z, ua, ub, wa, wb, wo, tm):
    T = dx1.shape[0]

    def body(d_ref, oa_ref, ob_ref, ag_ref, bg_ref, ma_ref, mb_ref, ua_ref, ub_ref, wa_ref, wb_ref, wo_ref,
             doa_ref, dob_ref, dag_ref, dbg_ref, dma_ref, dmb_ref, dsa_ref, dsb_ref, dwa_ref, dwb_ref, dwo_ref):
        @pl.when(pl.program_id(0) == 0)
        def _():
            dwa_ref[...] = jnp.zeros(dwa_ref.shape, F32)
            dwb_ref[...] = jnp.zeros(dwb_ref.shape, F32)
            dwo_ref[...] = jnp.zeros(dwo_ref.shape, F32)

        db = d_ref[...].astype(BF16)
        ua, ub = ua_ref[...], ub_ref[...]
        sa, sb = _sig(ma_ref[...]), _sig(mb_ref[...])
        dwo_ref[...] += _tn((sa * ua + sb * ub).astype(BF16), db)
        dy = _nt(db, wo_ref[...])
        dma_ref[...] = (dy * ua * sa * (1.0 - sa)).astype(BF16)
        dmb_ref[...] = (dy * ub * sb * (1.0 - sb)).astype(BF16)
        for (o_ref, gate_ref, s, w_ref, do_ref, dgate_ref, dw_ref, ds_ref) in (
                (oa_ref, ag_ref, sa, wa_ref, doa_ref, dag_ref, dwa_ref, dsa_ref),
                (ob_ref, bg_ref, sb, wb_ref, dob_ref, dbg_ref, dwb_ref, dsb_ref)):
            du = (dy * s).astype(BF16)
            raw, gate = o_ref[...], gate_ref[...]
            sg = _sig(gate)
            silu = gate * sg
            dw_ref[...] += _tn((raw * silu).astype(BF16), du)
            do = _nt(du, w_ref[...])
            draw = do * silu
            do_ref[...] = draw.astype(BF16)
            dgate_ref[...] = (do * raw * (sg * (1.0 + gate * (1.0 - sg)))).astype(BF16)
            ds_ref[...] = jnp.sum((draw * raw).T.reshape(MLA_H, MLA_V, tm), axis=1)

    return pl.pallas_call(
        body, grid=(T // tm,), name="bwd_merge",
        in_specs=[_row(tm, D), _row(tm, 512), _row(tm, 512), _row(tm, 512, Z_AG // 512), _row(tm, 512, Z_BG // 512),
                  _row(tm, D, Z_MA // D), _row(tm, D, Z_MB // D), _row(tm, D), _row(tm, D),
                  _res((512, D)), _res((512, D)), _res((D, D))],
        out_specs=[_row(tm, 512)] * 4 + [_row(tm, D)] * 2 + [pl.BlockSpec((MLA_H, tm), lambda i: (0, i))] * 2
        + [_acc((512, D)), _acc((512, D)), _acc((D, D))],
        out_shape=[SDS((T, 512), BF16), SDS((T, 512), BF16), SDS((T, 512), BF16), SDS((T, 512), BF16),
                   SDS((T, D), BF16), SDS((T, D), BF16), SDS((MLA_H, T), F32), SDS((MLA_H, T), F32),
                   SDS((512, D), F32), SDS((512, D), F32), SDS((D, D), F32)],
        compiler_params=_cp("arbitrary"))(dx1, oa, ob, z, z, z, z, ua, ub, wa, wb, wo)


def _mla_bwd(qf, kf, v, do, lse, dsum, nb, seq, tq):
    T = qf.shape[0]
    nq = seq // tq
    pw = 2 * LANES
    pairs = [(qi, ki) for ki in range(nq) for qi in range(ki, nq)]
    qi_tab = jnp.array([qk[0] for qk in pairs], jnp.int32)
    ki_tab = jnp.array([qk[1] for qk in pairs], jnp.int32)

    def body(qi_ref, ki_ref, q_ref, k_ref, v_ref, do_ref, lse_ref, dsum_ref, dq_ref, dk_ref, dv_ref, dk_s, dv_s, dqt_s):
        step_id = pl.program_id(2)
        qi, ki = qi_ref[step_id], ki_ref[step_id]

        @pl.when(step_id == 0)
        def _():
            dqt_s[...] = jnp.zeros(dqt_s.shape, F32)

        @pl.when(qi == ki)
        def _():
            dk_s[...] = jnp.zeros(dk_s.shape, F32)
            dv_s[...] = jnp.zeros(dv_s.shape, F32)

        def step(masked):
            if masked:
                keys = lax.broadcasted_iota(jnp.int32, (tq, tq), 0)
                queries = lax.broadcasted_iota(jnp.int32, (tq, tq), 1)
                mask = keys <= queries
            for j in range(2):
                wide = slice(LANES * j, LANES * (j + 1))
                sl = slice(MLA_V * j, MLA_V * (j + 1))
                q, k = q_ref[:, wide], k_ref[:, wide]
                dob = do_ref[:, sl].astype(BF16)
                s = _nt(k, q) * MLA_SCALE
                if masked:
                    s = jnp.where(mask, s, NEG)
                p = jnp.exp(s - lse_ref[j:j + 1, :])
                dv_s[:, sl] += _nn(p.astype(BF16), dob)
                ds = (p * (_nt(v_ref[:, sl], dob) - dsum_ref[j:j + 1, :]) * MLA_SCALE).astype(BF16)
                dk_s[:, wide] += _nn(ds, q)
                dqt_s[qi, wide, :] += _tn(k, ds)

        @pl.when(qi > ki)
        def _():
            step(False)

        @pl.when(qi == ki)
        def _():
            step(True)

        @pl.when(qi == nq - 1)
        def _():
            dk_ref[...] = dk_s[...]
            dv_ref[...] = dv_s[...]

        @pl.when(step_id == len(pairs) - 1)
        def _():
            for n in range(nq):
                dq_ref[tq * n:tq * (n + 1), :] = dqt_s[n].T

    qmap = lambda b, hp, s, qi_ref, ki_ref: (b * nq + qi_ref[s], hp)
    kmap = lambda b, hp, s, qi_ref, ki_ref: (b * nq + ki_ref[s], hp)
    stat = pl.BlockSpec((None, 2, tq), lambda b, hp, s, qi_ref, ki_ref: (hp, 0, b * nq + qi_ref[s]))
    return pl.pallas_call(
        body, name="mla_bwd",
        grid_spec=pltpu.PrefetchScalarGridSpec(
            num_scalar_prefetch=2, grid=(nb, MLA_H // 2, len(pairs)),
            in_specs=[pl.BlockSpec((tq, pw), qmap), pl.BlockSpec((tq, pw), kmap), pl.BlockSpec((tq, LANES), kmap),
                      pl.BlockSpec((tq, LANES), qmap), stat, stat],
            out_specs=[pl.BlockSpec((seq, pw), lambda b, hp, s, qi_ref, ki_ref: (b, hp)),
                       pl.BlockSpec((tq, pw), kmap), pl.BlockSpec((tq, LANES), kmap)],
            scratch_shapes=[pltpu.VMEM((tq, pw), F32), pltpu.VMEM((tq, LANES), F32),
                            pltpu.VMEM((nq, pw, tq), F32)]),
        out_shape=[SDS((T, QFW), F32), SDS((T, QFW), F32), SDS((T, MLA_H * MLA_V), F32)],
        compiler_params=_cp("parallel", "parallel", "arbitrary"))(qi_tab, ki_tab, qf, kf, v, do, lse, dsum)


def _swa_bwd(sink, z, pos_col, pos_row, do, lse, dsum, nb, seq):
    T = z.shape[0]
    nblk = seq // BLK

    def body(sink_ref, q_ref, kvc_ref, kvp_ref, pcc_ref, pcp_ref, pr_ref, do_ref, lse_ref, dsum_ref,
             dq_ref, dkv_ref, dsink_ref):
        b, n = pl.program_id(0), pl.program_id(1)

        @pl.when((b == 0) & (n == 0))
        def _():
            dsink_ref[...] = jnp.zeros(dsink_ref.shape, F32)

        @pl.when(n == 0)
        def _():
            dkv_ref[...] = jnp.zeros(dkv_ref.shape, F32)

        kb, vb, scores = _swa_scores(n, q_ref, kvc_ref, kvp_ref, pcc_ref, pcp_ref, pr_ref)
        lane = lax.broadcasted_iota(jnp.int32, (1, LANES), 1)
        dsink = jnp.zeros((1, LANES), F32)
        dkv = [[None, None], [None, None]]
        dqs = []
        gsl = lambda h: slice(SWA_DH * (h // (SWA_H // SWA_KV)), SWA_DH * (h // (SWA_H // SWA_KV) + 1))
        qs, ss, dobs, dps = [], [], [], []
        for h in range(SWA_H):
            qh, s = scores(h)
            dob = do_ref[:, SWA_DH * h:SWA_DH * (h + 1)].astype(BF16)
            qs.append(qh)
            ss.append(s)
            dobs.append(dob)
            dps.append(_nt(vb[:, gsl(h)], dob))
        pbs, dss = [], []
        for h in range(SWA_H):
            lse, dsum = lse_ref[h:h + 1, :], dsum_ref[h:h + 1, :]
            p = jnp.exp(ss[h] - lse)
            pbs.append(p.astype(BF16))
            dss.append((p * (dps[h] - dsum) * SWA_SCALE).astype(BF16))
            dsk = jnp.sum(-jnp.exp(sink_ref[h] - lse) * dsum, axis=1, keepdims=True)
            dsink = dsink + jnp.where(lane == h, dsk, 0.0)
        for h in range(SWA_H):
            g = h // (SWA_H // SWA_KV)
            dqs.append(_tn(kb[:, gsl(h)], dss[h]))
            dk, dv = _nn(dss[h], qs[h]), _nn(pbs[h], dobs[h])
            dkv[g][0] = dk if dkv[g][0] is None else dkv[g][0] + dk
            dkv[g][1] = dv if dkv[g][1] is None else dkv[g][1] + dv
        dq_ref[...] = jnp.concatenate(dqs, axis=0).T.astype(BF16)
        dsink_ref[...] += dsink
        upd = jnp.concatenate([dkv[0][0], dkv[1][0], dkv[0][1], dkv[1][1]], axis=1)
        dkv_ref[pl.ds(pl.multiple_of(n * BLK, BLK), BLK), :] += upd[BLK:]

        @pl.when(n > 0)
        def _():
            dkv_ref[pl.ds(pl.multiple_of((n - 1) * BLK, BLK), BLK), :] += upd[:BLK]

    return pl.pallas_call(
        body, grid=(nb, nblk), name="swa_bwd",
        in_specs=_swa_specs(nblk) + [pl.BlockSpec((BLK, 512), lambda b, n: (b * nblk + n, 0))]
        + [pl.BlockSpec((SWA_H, BLK), lambda b, n: (0, b * nblk + n))] * 2,
        out_specs=[pl.BlockSpec((BLK, 512), lambda b, n: (b * nblk + n, 0)),
                   pl.BlockSpec((seq, 2 * BLK), lambda b, n: (b, 0)),
                   pl.BlockSpec((1, LANES), lambda b, n: (0, 0))],
        out_shape=[SDS((T, 512), BF16), SDS((T, 2 * BLK), F32), SDS((1, LANES), F32)],
        compiler_params=_cp("arbitrary", "arbitrary"))(sink, z, z, z, pos_col, pos_col, pos_row, do, lse, dsum)


def _bwd_prep(dq, dk, dv, z, gq, gkv, wq, wkv, tc, ts1, ts2, tm):
    T = z.shape[0]

    def body(dq_ref, dk_ref, dv_ref, qd_ref, kvd_ref, gq_ref, gkv_ref, wq_ref, wkv_ref, c_ref, s1_ref, s2_ref,
             dqd_ref, dkvd_ref, dkr_ref, dwq_ref, dwkv_ref, dgq_ref, dgkv_ref, dqb_s, dkvb_s):
        @pl.when(pl.program_id(0) == 0)
        def _():
            for ref in (dwq_ref, dwkv_ref, dgq_ref, dgkv_ref):
                ref[...] = jnp.zeros(ref.shape, F32)

        c, s1, s2 = c_ref[...], s1_ref[...], s2_ref[...]
        lane = lax.broadcasted_iota(jnp.int32, (1, LANES), 1)
        rope_lanes = (lane >= MLA_NOPE) & (lane < MLA_QK)
        dkb = jnp.zeros((tm, LANES), F32)
        for h in range(MLA_H):
            sl = slice(LANES * h, LANES * (h + 1))
            dqb_s[:, sl] = _rope_t(dq_ref[:, sl], c, s1, s2).astype(BF16)
            dkh = dk_ref[:, sl]
            dkb = dkb + dkh
            dkvb_s[:, sl] = dkh.astype(BF16)
        dkvb_s[:, QFW:] = dv_ref[...].astype(BF16)
        dkr_ref[...] = _rope_t(jnp.where(rope_lanes, dkb, 0.0), c, s1, s2).astype(BF16)

        for (x_ref, g_ref, w_ref, d_s, dx_ref, dw_ref, dg_ref) in (
                (qd_ref, gq_ref, wq_ref, dqb_s, dqd_ref, dwq_ref, dgq_ref),
                (kvd_ref, gkv_ref, wkv_ref, dkvb_s, dkvd_ref, dwkv_ref, dgkv_ref)):
            xf, gf, db = x_ref[...], g_ref[...], d_s[...]
            r = _rstd(xf)
            n = xf * r
            dw_ref[...] += _tn((n * gf).astype(BF16), db)
            dx, dgr = _norm_bwd(_nt(db, w_ref[...]), n, r, gf)
            dx_ref[...] = dx.astype(BF16)
            dg_ref[...] += jnp.sum(dgr, axis=0, keepdims=True)

    return pl.pallas_call(
        body, grid=(T // tm,), name="bwd_prep",
        in_specs=[_row(tm, QFW), _row(tm, QFW), _row(tm, MLA_H * MLA_V),
                  _row(tm, QL, Z_QD // QL), _row(tm, KVL, Z_KVD // KVL),
                  _res((1, QL)), _res((1, KVL)), _res((QL, QFW)), _res((KVL, KVW)),
                  _row(tm, LANES), _row(tm, LANES), _row(tm, LANES)],
        out_specs=[_row(tm, QL), _row(tm, KVL), _row(tm, LANES),
                   _acc((QL, QFW)), _acc((KVL, KVW)), _acc((1, QL)), _acc((1, KVL))],
        out_shape=[SDS((T, QL), BF16), SDS((T, KVL), BF16), SDS((T, LANES), BF16),
                   SDS((QL, QFW), F32), SDS((KVL, KVW), F32), SDS((1, QL), F32), SDS((1, KVL), F32)],
        scratch_shapes=[pltpu.VMEM((tm, QFW), BF16), pltpu.VMEM((tm, KVW), BF16)],
        compiler_params=_cp("arbitrary"))(dq, dk, dv, z, z, gq, gkv, wq, wkv, tc, ts1, ts2)


def _bwd_in(pieces, x, g, dres, w, tm):
    T = x.shape[0]
    widths = [pc.shape[1] for pc in pieces]
    assert sum(widths) == ZW
    n_p = len(pieces)

    def body(*refs):
        p_refs, (x_ref, g_ref, r_ref, w_ref, dx_ref, dz_ref, dg_ref) = refs[:n_p], refs[n_p:]

        @pl.when(pl.program_id(0) == 0)
        def _():
            dg_ref[...] = jnp.zeros(dg_ref.shape, F32)

        off = 0
        for ref, wd in zip(p_refs, widths):
            dz_ref[:, off:off + wd] = ref[...].astype(BF16)
            off += wd
        xf, gf = x_ref[...], g_ref[...]
        r = _rstd(xf)
        n = xf * r
        dx, dgr = _norm_bwd(_nt(dz_ref[...], w_ref[...]), n, r, gf)
        dx_ref[...] = r_ref[...] + dx
        dg_ref[...] += jnp.sum(dgr, axis=0, keepdims=True)

    return pl.pallas_call(
        body, grid=(T // tm,), name="bwd_in",
        in_specs=[_row(tm, wd) for wd in widths] + [_row(tm, D), _res((1, D)), _row(tm, D), _res((D, ZW))],
        out_specs=[_row(tm, D), _row(tm, ZW), _acc((1, D))],
        out_shape=[SDS((T, D), F32), SDS((T, ZW), BF16), SDS((1, D), F32)],
        compiler_params=_cp("arbitrary"))(*pieces, x, g, dres, w)


def _wgrad_in(hb, dzb, tm):
    T = hb.shape[0]
    half = ZW // 2

    def body(h_ref, dz_ref, dw_ref):
        @pl.when(pl.program_id(1) == 0)
        def _():
            dw_ref[...] = jnp.zeros(dw_ref.shape, F32)

        dw_ref[...] += _tn(h_ref[...], dz_ref[...])

    return pl.pallas_call(
        body, grid=(2, T // tm), name="wgrad_in",
        in_specs=[pl.BlockSpec((tm, D), lambda j, t: (t, 0)), pl.BlockSpec((tm, half), lambda j, t: (t, j))],
        out_specs=pl.BlockSpec((D, half), lambda j, t: (0, j)),
        out_shape=SDS((D, ZW), F32),
        compiler_params=_cp("parallel", "arbitrary"))(hb, dzb)


def _win_to_kernel(w):
    s = lambda a, b: w[..., a:b]
    zero = lambda n: jnp.zeros(w.shape[:-1] + (n,), w.dtype)
    return jnp.concatenate([s(2208, 3232), s(3232, 4256), s(0, 512), s(768, 1280), s(1696, 2208), s(1280, 1536),
                            s(512, 640), s(640, 768), s(1536, 1664), zero(64), s(1664, 1696), zero(32)], axis=-1)


def _win_from_kernel(g):
    s = lambda a, n: g[..., a:a + n]
    return jnp.concatenate([s(Z_AQ, 512), s(Z_AK, 128), s(Z_AV, 128), s(Z_AG, 512), s(Z_QD, 256), s(Z_KVD, 128),
                            s(Z_KR + MLA_NOPE, MLA_ROPE), s(Z_BG, 512), s(Z_MA, 1024), s(Z_MB, 1024)], axis=-1)


def _wuq_to_kernel(w):
    w = w.reshape(w.shape[:-1] + (MLA_H, MLA_QK))
    w = jnp.pad(w, [(0, 0)] * (w.ndim - 1) + [(0, LANES - MLA_QK)])
    return w.reshape(w.shape[:-2] + (QFW,))


def _wuq_from_kernel(g):
    g = g.reshape(g.shape[:-1] + (MLA_H, LANES))[..., :MLA_QK]
    return g.reshape(g.shape[:-2] + (MLA_H * MLA_QK,))


def _wukv_to_kernel(w):
    w = w.reshape(w.shape[:-1] + (MLA_H, MLA_NOPE + MLA_V))
    k = jnp.pad(w[..., :MLA_NOPE], [(0, 0)] * (w.ndim - 1) + [(0, LANES - MLA_NOPE)])
    v = w[..., MLA_NOPE:]
    return jnp.concatenate([k.reshape(k.shape[:-2] + (QFW,)), v.reshape(v.shape[:-2] + (MLA_H * MLA_V,))], axis=-1)


def _wukv_from_kernel(g):
    k = g[..., :QFW].reshape(g.shape[:-1] + (MLA_H, LANES))[..., :MLA_NOPE]
    v = g[..., QFW:].reshape(g.shape[:-1] + (MLA_H, MLA_V))
    kv = jnp.concatenate([k, v], axis=-1)
    return kv.reshape(kv.shape[:-2] + (MLA_H * (MLA_NOPE + MLA_V),))


def _rope_tables(pos):
    half = MLA_ROPE // 2
    inv = 10000.0 ** (-jnp.arange(0, MLA_ROPE, 2, dtype=F32) / MLA_ROPE)
    ang = pos.astype(F32)[:, None] * inv
    cos, sin = jnp.cos(ang), jnp.sin(ang)
    one = jnp.ones((pos.shape[0], MLA_NOPE), F32)
    zero = lambda n: jnp.zeros((pos.shape[0], n), F32)
    tc = jnp.concatenate([one, cos, cos, one[:, :LANES - MLA_QK]], axis=1)
    ts1 = jnp.concatenate([zero(MLA_NOPE + half), sin, zero(LANES - MLA_QK)], axis=1)
    ts2 = jnp.concatenate([zero(MLA_NOPE), -sin, zero(LANES - MLA_NOPE - half)], axis=1)
    return tc, ts1, ts2


def _local_step(x, p, positions, loss_target, small, wts):
    nb, seq, _ = x.shape
    T = nb * seq
    tm = min(256, T)
    tq = min(512, seq)
    xf = x.reshape(T, D)
    pos = positions.reshape(T)
    posf = pos.astype(F32)
    pos_col, pos_row = posf.reshape(T, 1), posf.reshape(T // BLK, 1, BLK)
    tc, ts1, ts2 = _rope_tables(pos)

    saved = []
    for i in range(DEPTH):
        w, sm = wts[i], small[i]
        z, hb = _fwd_in(xf, sm["g_mix"], w["w_in"], tm)
        oa, lse_a = _swa_fwd(sm["sink"], z, pos_col, pos_row, nb, seq)
        qf, kf, v = _fwd_prep(z, sm["g_q"], sm["g_kv"], w["w_uq"], w["w_ukv"], tc, ts1, ts2, tm)
        ob, lse_b = _mla_fwd(qf, kf, v, nb, seq, tq)
        x1, ua, ub = _fwd_merge(xf, oa, ob, z, w["w_br_a"], w["w_br_b"], w["w_out"], tm)
        x2, pg, pp = _fwd_ple(x1, p[i].reshape(T, PLE), sm["g_ple"], w["w_ple_gate"], w["w_ple_proj"], tm)
        saved.append(dict(x=xf, z=z, hb=hb, oa=oa, lse_a=lse_a, qf=qf, kf=kf, v=v, ob=ob, lse_b=lse_b,
                          x1=x1, ua=ua, ub=ub, pg=pg, pp=pp))
        xf = x2

    dx, dg_final, loss = _loss_head(xf, small["g_final"], loss_target.reshape(T, D), tm)

    grads = [None] * DEPTH
    for i in reversed(range(DEPTH)):
        w, sm, sv = wts[i], small[i], saved[i]
        dx1, dwpg, dwpp, dg_ple = _bwd_ple(dx, sv["x1"], sv["pg"], sv["pp"], p[i].reshape(T, PLE), sm["g_ple"],
                                           w["w_ple_gate"], tm)
        doa, dob, dag, dbg, dma, dmb, dsum_a, dsum_b, dwa, dwb, dwo = _bwd_merge(
            dx1, sv["oa"], sv["ob"], sv["z"], sv["ua"], sv["ub"], w["w_br_a"], w["w_br_b"], w["w_out"], tm)
        dq_b, dk_b, dv_b = _mla_bwd(sv["qf"], sv["kf"], sv["v"], dob, sv["lse_b"],
                                    dsum_b.reshape(MLA_H // 2, 2, T), nb, seq, tq)
        dqd, dkvd, dkr, dwq, dwkv, dgq, dgkv = _bwd_prep(dq_b, dk_b, dv_b, sv["z"], sm["g_q"], sm["g_kv"],
                                                         w["w_uq"], w["w_ukv"], tc, ts1, ts2, tm)
        dq_a, dkv_a, dsink = _swa_bwd(sm["sink"], sv["z"], pos_col, pos_row, doa, sv["lse_a"], dsum_a, nb, seq)
        dx, dzb, dg_mix = _bwd_in([dma, dmb, dq_a, dag, dbg, dqd, dkv_a, dkvd, dkr], sv["x"], sm["g_mix"], dx1,
                                  w["w_in"], tm)
        dwin = _wgrad_in(sv["hb"], dzb, tm)
        grads[i] = dict(g_mix=dg_mix[0], w_in=_win_from_kernel(dwin), sink=dsink[0, :SWA_H], g_q=dgq[0],
                        w_uq=_wuq_from_kernel(dwq), g_kv=dgkv[0], w_ukv=_wukv_from_kernel(dwkv), w_br_a=dwa,
                        w_br_b=dwb, w_out=dwo, g_ple=dg_ple[0], w_ple_gate=dwpg, w_ple_proj=dwpp)
    return loss, dx.reshape(nb, seq, D), grads, dg_final[0]


def _kernel_weights(full):
    wi, wq, wkv = _win_to_kernel(full["w_in"]), _wuq_to_kernel(full["w_uq"]), _wukv_to_kernel(full["w_ukv"])
    return [dict(w_in=wi[i], w_uq=wq[i], w_ukv=wkv[i], w_br_a=full["w_br_a"][i], w_br_b=full["w_br_b"][i],
                 w_out=full["w_out"][i], w_ple_gate=full["w_ple_gate"][i], w_ple_proj=full["w_ple_proj"][i])
            for i in range(DEPTH)]


def _small_params(g_mix, sink, g_q, g_kv, g_ple, g_final):
    small = {i: dict(g_mix=g_mix[i][None], sink=sink[i], g_q=g_q[i][None], g_kv=g_kv[i][None], g_ple=g_ple[i][None])
             for i in range(DEPTH)}
    small["g_final"] = g_final[None]
    return small


WIDE_ROWS, WIDE_W = DEPTH * D, IN_W // N_DEV
UQ_W = MLA_H * MLA_QK // N_DEV
ROWS_PIECES = (("w_uq", DEPTH * QL), ("w_ukv", DEPTH * KVL), ("w_br_a", DEPTH * 512), ("w_br_b", DEPTH * 512),
               ("w_out", DEPTH * D), ("w_ple_gate", DEPTH * D), ("w_ple_proj", DEPTH * PLE))
SMALL = (("g_mix", (DEPTH, D)), ("sink", (DEPTH, SWA_H)), ("g_q", (DEPTH, QL)), ("g_kv", (DEPTH, KVL)),
         ("g_ple", (DEPTH, D)), ("g_final", (D,)))
VEC_ROWS = 48
ROWS_N = sum(r for _, r in ROWS_PIECES)
WIDE_TILE, ROWS_TILE = 256, ROWS_N // 4


def _to_rows(name, a):
    lead = a.shape[:-3]
    if name == "w_uq":
        a = jnp.pad(a, [(0, 0)] * (a.ndim - 1) + [(0, LANES - UQ_W)])
    return a.reshape(lead + (-1, LANES))


def _from_rows(name, r):
    lead = r.shape[:-2]
    if name in ("w_out", "w_ple_gate"):
        return r.reshape(lead + (DEPTH, D // N_DEV, D))
    r = r.reshape(lead + (DEPTH, -1, LANES))
    return r[..., :UQ_W] if name == "w_uq" else r


def _pack_rows(blocks):
    return jnp.concatenate([_to_rows(n, blocks[n]) for n, _ in ROWS_PIECES], axis=-2)


def _unpack_rows(rows):
    blocks, off = {}, 0
    for n, r in ROWS_PIECES:
        blocks[n] = _from_rows(n, rows[..., off:off + r, :])
        off += r
    return blocks


def _pack_vec(vectors, loss=None):
    parts = [vectors[n].reshape(-1) for n, _ in SMALL] + ([] if loss is None else [loss.reshape(1)])
    vec = jnp.concatenate(parts)
    return jnp.pad(vec, (0, VEC_ROWS * LANES - vec.shape[0])).reshape(1, VEC_ROWS, LANES)


def _unpack_vec(vec):
    vec = vec.reshape(-1)
    vectors, off = {}, 0
    for n, shp in SMALL:
        size = 1
        for s in shp:
            size *= s
        vectors[n] = vec[off:off + size].reshape(shp)
        off += size
    return vectors, vec[off]


def _join(name, blocks):
    if name in ("w_out", "w_ple_gate"):
        return jnp.moveaxis(blocks, 0, 1).reshape(DEPTH, -1, blocks.shape[-1])
    return jnp.moveaxis(blocks, 0, 2).reshape(DEPTH, blocks.shape[2], -1)


def _split(name, full):
    if name in ("w_out", "w_ple_gate"):
        return jnp.moveaxis(full.reshape(DEPTH, N_DEV, -1, full.shape[-1]), 1, 0)
    return jnp.moveaxis(full.reshape(DEPTH, full.shape[1], N_DEV, -1), 2, 0)


MESH_ID = pl.DeviceIdType.MESH
ANY = pl.BlockSpec(memory_space=pl.ANY)


def _place():
    return lax.axis_index("x"), lax.axis_index("y"), lax.axis_index("c")


def _all_gather(blocks):
    n = len(blocks)

    def body(*refs):
        x_refs, out_refs, (send_sems, recv_sems, local_sems) = refs[:n], refs[n:2 * n], refs[2 * n:]
        x, y, c = _place()
        me, sibling = (x, y, c), (x, y, 1 - c)
        chips = [(1 - x, y), (x, 1 - y), (1 - x, 1 - y)]

        def slot(a, px, py, pc):
            return out_refs[a].at[4 * px + 2 * py + pc]

        def copy(a, k, blk, to, src=None):
            return pltpu.make_async_remote_copy(
                src_ref=slot(a, *blk) if src is None else src, dst_ref=slot(a, *blk),
                send_sem=send_sems.at[7 * a + k], recv_sem=recv_sems.at[7 * a + k], device_id=to,
                device_id_type=MESH_ID)

        mine = [pltpu.make_async_copy(x_refs[a], slot(a, *me), local_sems.at[a]) for a in range(n)]
        for cp in mine:
            cp.start()
        first = []
        for a in range(n):
            first += [copy(a, 0, me, sibling, src=x_refs[a])]
            first += [copy(a, 1 + j, me, (*chip, c), src=x_refs[a]) for j, chip in enumerate(chips)]
        for cp in first:
            cp.start()
        passed = []
        for j, chip in enumerate(chips):
            for a in range(n):
                copy(a, 1 + j, (*chip, c), me).wait_recv()
                passed.append(copy(a, 4 + j, (*chip, c), sibling))
                passed[-1].start()
        for a in range(n):
            copy(a, 0, sibling, me).wait_recv()
            for j, chip in enumerate(chips):
                copy(a, 4 + j, (*chip, 1 - c), me).wait_recv()
        for cp in first + passed:
            cp.wait_send()
        for cp in mine:
            cp.wait()

    return pl.pallas_call(
        body, name="all_gather_weights",
        out_shape=[SDS((N_DEV,) + b.shape, b.dtype) for b in blocks],
        in_specs=[ANY] * n, out_specs=[ANY] * n,
        scratch_shapes=[pltpu.SemaphoreType.DMA((7 * n,)), pltpu.SemaphoreType.DMA((7 * n,)),
                        pltpu.SemaphoreType.DMA((n,))],
    )(*blocks)


def _swap_sibling(arrs):
    n = len(arrs)

    def body(*refs):
        a_refs, out_refs, (send_sems, recv_sems) = refs[:n], refs[n:2 * n], refs[2 * n:]
        x, y, c = _place()
        copies = [pltpu.make_async_remote_copy(
            src_ref=a_refs[a].at[:, 1 - c], dst_ref=out_refs[a], send_sem=send_sems.at[a], recv_sem=recv_sems.at[a],
            device_id=(x, y, 1 - c), device_id_type=MESH_ID) for a in range(n)]
        for cp in copies:
            cp.start()
        for cp in copies:
            cp.wait()

    return pl.pallas_call(
        body, name="swap_sibling", out_shape=[SDS((a.shape[0],) + a.shape[2:], a.dtype) for a in arrs],
        in_specs=[ANY] * n, out_specs=[ANY] * n,
        scratch_shapes=[pltpu.SemaphoreType.DMA((n,)), pltpu.SemaphoreType.DMA((n,))],
    )(*arrs)


def _exchange_chips(arrs):
    n = len(arrs)

    def body(*refs):
        p_refs, out_refs, (send_sems, recv_sems, local_sems) = refs[:n], refs[n:2 * n], refs[2 * n:]
        x, y, c = _place()
        mine = 2 * x + y
        local = [pltpu.make_async_copy(p_refs[a].at[mine], out_refs[a].at[mine], local_sems.at[a]) for a in range(n)]
        for cp in local:
            cp.start()
        peers = [(1 - x, y), (x, 1 - y), (1 - x, 1 - y)]

        def copy(a, j, src_chip, dst_chip):
            px, py = peers[j]
            return pltpu.make_async_remote_copy(
                src_ref=p_refs[a].at[src_chip], dst_ref=out_refs[a].at[dst_chip], send_sem=send_sems.at[3 * a + j],
                recv_sem=recv_sems.at[3 * a + j], device_id=(px, py, c), device_id_type=MESH_ID)

        copies = [copy(a, j, 2 * px + py, mine) for a in range(n) for j, (px, py) in enumerate(peers)]
        for cp in copies:
            cp.start()
        for a in range(n):
            for j, (px, py) in enumerate(peers):
                copy(a, j, mine, 2 * px + py).wait_recv()
        for cp in copies:
            cp.wait_send()
        for cp in local:
            cp.wait()

    return pl.pallas_call(
        body, name="exchange_chips", out_shape=[SDS(a.shape, a.dtype) for a in arrs],
        in_specs=[ANY] * n, out_specs=[ANY] * n,
        scratch_shapes=[pltpu.SemaphoreType.DMA((3 * n,)), pltpu.SemaphoreType.DMA((3 * n,)),
                        pltpu.SemaphoreType.DMA((n,))],
    )(*arrs)


def _add_mine(g, recv, core, tile, dtype):
    _, _, lead, rows, width = g.shape

    def body(c_ref, g_ref, r_ref, o_ref):
        o_ref[...] = (g_ref[...] + r_ref[...]).astype(dtype)

    spec = pl.BlockSpec((None, None, tile, width), lambda k, l, i, c_ref: (k, l, i, 0))
    return pl.pallas_call(
        body, name="add_sibling", out_shape=SDS(recv.shape, dtype),
        grid_spec=pltpu.PrefetchScalarGridSpec(
            num_scalar_prefetch=1, grid=(g.shape[0], lead, rows // tile),
            in_specs=[pl.BlockSpec((None, None, None, tile, width), lambda k, l, i, c_ref: (k, c_ref[0], l, i, 0)),
                      spec],
            out_specs=spec),
        compiler_params=_cp("parallel", "parallel", "parallel"))(core, g, recv)


def _sum_adamw(parts, w, m, v, tile):
    lead, rows, width = w.shape

    def body(p_ref, w_ref, m_ref, v_ref, g_ref, d_ref, nm_ref, nv_ref):
        g = ((p_ref[0].astype(F32) + p_ref[1].astype(F32)) + p_ref[2].astype(F32)) + p_ref[3].astype(F32)
        nm = ADAM_B1 * m_ref[...] + (1.0 - ADAM_B1) * g
        nv = ADAM_B2 * v_ref[...] + (1.0 - ADAM_B2) * jnp.square(g)
        m_hat = nm / (1.0 - ADAM_B1 ** ADAM_STEP)
        v_hat = nv / (1.0 - ADAM_B2 ** ADAM_STEP)
        g_ref[...] = g
        nm_ref[...] = nm
        nv_ref[...] = nv
        d_ref[...] = -ADAM_LR * (m_hat / (jnp.sqrt(v_hat) + ADAM_EPS) + ADAM_WD * w_ref[...])

    spec = pl.BlockSpec((None, tile, width), lambda l, i: (l, i, 0))
    return pl.pallas_call(
        body, grid=(lead, rows // tile), name="sum_adamw",
        in_specs=[pl.BlockSpec((4, None, tile, width), lambda l, i: (0, l, i, 0)), spec, spec, spec],
        out_specs=[spec] * 4, out_shape=[SDS((lead, rows, width), F32)] * 4,
        compiler_params=_cp("parallel", "parallel"))(parts, w, m, v)


def kernel(x, p, positions, g_mix, w_in, sink, g_q, w_uq, g_kv, w_ukv, w_br_a, w_br_b, w_out, g_ple, w_ple_gate, w_ple_proj, g_final, loss_target, m_g_mix, m_w_in, m_sink, m_g_q, m_w_uq, m_g_kv, m_w_ukv, m_w_br_a, m_w_br_b, m_w_out, m_g_ple, m_w_ple_gate, m_w_ple_proj, m_g_final, v_g_mix, v_w_in, v_sink, v_g_q, v_w_uq, v_g_kv, v_w_ukv, v_w_br_a, v_w_br_b, v_w_out, v_g_ple, v_w_ple_gate, v_w_ple_proj, v_g_final):
    weights = dict(g_mix=g_mix, w_in=w_in, sink=sink, g_q=g_q, w_uq=w_uq, g_kv=g_kv, w_ukv=w_ukv, w_br_a=w_br_a,
                   w_br_b=w_br_b, w_out=w_out, g_ple=g_ple, w_ple_gate=w_ple_gate, w_ple_proj=w_ple_proj,
                   g_final=g_final)
    mom1 = dict(g_mix=m_g_mix, w_in=m_w_in, sink=m_sink, g_q=m_g_q, w_uq=m_w_uq, g_kv=m_g_kv, w_ukv=m_w_ukv,
                w_br_a=m_w_br_a, w_br_b=m_w_br_b, w_out=m_w_out, g_ple=m_g_ple, w_ple_gate=m_w_ple_gate,
                w_ple_proj=m_w_ple_proj, g_final=m_g_final)
    mom2 = dict(g_mix=v_g_mix, w_in=v_w_in, sink=v_sink, g_q=v_g_q, w_uq=v_w_uq, g_kv=v_g_kv, w_ukv=v_w_ukv,
                w_br_a=v_w_br_a, w_br_b=v_w_br_b, w_out=v_w_out, g_ple=v_g_ple, w_ple_gate=v_w_ple_gate,
                w_ple_proj=v_w_ple_proj, g_final=v_g_final)
    wide = lambda d: d["w_in"]
    rows = lambda d: _pack_rows(d)[None]

    got_wide, got_rows = _all_gather([wide(weights).astype(BF16), rows(weights).astype(BF16)])
    blocks = _unpack_rows(got_rows[:, 0])
    full = {n: _join(n, blocks[n]) for n, _ in ROWS_PIECES}
    full["w_in"] = _join("w_in", got_wide)
    wts = _kernel_weights(full)
    small = _small_params(g_mix, sink, g_q, g_kv, g_ple, g_final)

    loss, grad_x, grads, dg_final = _local_step(x, p, positions, loss_target, small, wts)

    stacked = {n: jnp.stack([grads[i][n] for i in range(DEPTH)]) for n in grads[0]}
    stacked["g_final"] = dg_final
    pay_wide = _split("w_in", stacked["w_in"]).reshape(N_DEV // 2, 2, DEPTH, D, WIDE_W)
    pay_rows = _pack_rows({n: _split(n, stacked[n]) for n, _ in ROWS_PIECES})
    pay_rows = pay_rows.reshape(N_DEV // 2, 2, 1, ROWS_N, LANES)
    pay_vec = jnp.broadcast_to(_pack_vec(stacked, loss[0, 0]), (N_DEV // 2, 2, 1, VEC_ROWS, LANES))
    core = lax.axis_index("c").astype(jnp.int32).reshape(1)
    from_sibling = _swap_sibling([pay_wide, pay_rows, pay_vec])
    chip_partial = [_add_mine(pay_wide, from_sibling[0], core, WIDE_TILE, BF16),
                    _add_mine(pay_rows, from_sibling[1], core, ROWS_TILE, BF16),
                    _add_mine(pay_vec, from_sibling[2], core, VEC_ROWS, F32)]
    parts_wide, parts_rows, parts_vec = _exchange_chips(chip_partial)
    out_wide = _sum_adamw(parts_wide, wide(weights), wide(mom1), wide(mom2), WIDE_TILE)
    out_rows = _sum_adamw(parts_rows, rows(weights), rows(mom1), rows(mom2), ROWS_TILE)
    out_vec = _sum_adamw(parts_vec, _pack_vec(weights), _pack_vec(mom1), _pack_vec(mom2), VEC_ROWS)

    outs = []
    for ow, orow, ovec in zip(out_wide, out_rows, out_vec):
        named = _unpack_rows(orow[0])
        named.update(_unpack_vec(ovec)[0])
        named["w_in"] = ow
        outs += [named[n] for n in weights]
    loss = _unpack_vec(out_vec[0])[1]
    return (loss, grad_x, *outs)
```

```python
import functools

import jax
import jax.numpy as jnp
from jax import lax
from jax.experimental import pallas as pl
from jax.experimental.pallas import tpu as pltpu

F32, BF16 = jnp.float32, jnp.bfloat16
SDS = jax.ShapeDtypeStruct

D = 1024
DEPTH = 2
PLE = 256
BLK = 128
EPS = 1e-6
NEG = -1e30
SWA_H, SWA_KV, SWA_DH = 8, 2, 64
MLA_H, MLA_NOPE, MLA_ROPE, MLA_V = 8, 64, 32, 64
MLA_QK = MLA_NOPE + MLA_ROPE
QL, KVL = 256, 128
IN_W = 4256
N_DEV = 8

V7X_VMEM_BYTES = 64 * 1024 * 1024
LANES = 128
VMEM_LIMIT = V7X_VMEM_BYTES * 7 // 8

ZW = 4352
Z_MA, Z_MB, Z_AQ, Z_AG, Z_BG, Z_QD, Z_AK, Z_AV, Z_KVD, Z_KR = 0, 1024, 2048, 2560, 3072, 3584, 3840, 3968, 4096, 4224
QFW = MLA_H * LANES
KVW = QFW + MLA_H * MLA_V
MLA_SCALE = MLA_QK ** -0.5
SWA_SCALE = SWA_DH ** -0.5
ROLL_UP, ROLL_DOWN = MLA_ROPE // 2, LANES - MLA_ROPE // 2

ADAM_LR, ADAM_B1, ADAM_B2, ADAM_EPS, ADAM_WD, ADAM_STEP = 0.001, 0.9, 0.999, 1e-08, 0.01, 10

FLAT_W = 1024


def _cp(*sem):
    return pltpu.CompilerParams(dimension_semantics=sem, vmem_limit_bytes=VMEM_LIMIT)


def _row(tm, w, col=0):
    return pl.BlockSpec((tm, w), lambda i: (i, col))


def _res(shape, layer=None):
    if layer is None:
        return pl.BlockSpec(shape, lambda *_: (0,) * len(shape), pipeline_mode=pl.Buffered(1))
    return pl.BlockSpec((None,) + shape, lambda *_: (layer,) + (0,) * len(shape), pipeline_mode=pl.Buffered(1))


def _acc(shape):
    return pl.BlockSpec(shape, lambda *_: (0,) * len(shape))


def _rstd(xf):
    return lax.rsqrt(jnp.mean(xf * xf, axis=-1, keepdims=True) + EPS)


def _norm_bwd(dh, n, r, g):
    dn = dh * g
    return r * (dn - n * jnp.mean(dn * n, axis=-1, keepdims=True)), dh * n


def _nt(a, b):
    return lax.dot_general(a, b, (((1,), (1,)), ((), ())), preferred_element_type=F32)


def _tn(a, b):
    return lax.dot_general(a, b, (((0,), (0,)), ((), ())), preferred_element_type=F32)


def _nn(a, b):
    return jnp.dot(a, b, preferred_element_type=F32)


def _sig(x):
    return jax.nn.sigmoid(x)


def _rope(t, c, s1, s2):
    return t * c + pltpu.roll(t, ROLL_UP, 1) * s1 + pltpu.roll(t, ROLL_DOWN, 1) * s2


def _rope_t(d, c, s1, s2):
    return d * c + pltpu.roll(d * s1, ROLL_DOWN, 1) + pltpu.roll(d * s2, ROLL_UP, 1)


def _fwd_in(x, g, w, tm, layer):
    T = x.shape[0]

    def body(x_ref, g_ref, w_ref, z_ref, h_ref):
        xf = x_ref[...]
        h = ((xf * _rstd(xf)) * g_ref[...]).astype(BF16)
        h_ref[...] = h
        z_ref[...] = _nn(h, w_ref[...])

    return pl.pallas_call(
        body, grid=(T // tm,), name="fwd_in",
        in_specs=[_row(tm, D), _res((1, D), layer), _res((D, ZW), layer)],
        out_specs=[_row(tm, ZW), _row(tm, D)],
        out_shape=[SDS((T, ZW), F32), SDS((T, D), BF16)],
        compiler_params=_cp("parallel"))(x, g, w)


def _fwd_prep(z, gq, gkv, wq, wkv, tc, ts1, ts2, tm, layer):
    T = z.shape[0]

    def body(qd_ref, kvd_ref, kr_ref, gq_ref, gkv_ref, wq_ref, wkv_ref, c_ref, s1_ref, s2_ref, q_ref, k_ref, v_ref):
        qd, kvd = qd_ref[...], kvd_ref[...]
        hq = ((qd * _rstd(qd)) * gq_ref[...]).astype(BF16)
        hkv = ((kvd * _rstd(kvd)) * gkv_ref[...]).astype(BF16)
        qf = _nn(hq, wq_ref[...])
        kvf = _nn(hkv, wkv_ref[...])
        c, s1, s2 = c_ref[...], s1_ref[...], s2_ref[...]
        krb = _rope(kr_ref[...], c, s1, s2)
        for h in range(MLA_H):
            sl = slice(LANES * h, LANES * (h + 1))
            q_ref[:, sl] = _rope(qf[:, sl], c, s1, s2).astype(BF16)
            k_ref[:, sl] = (kvf[:, sl] + krb).astype(BF16)
        v_ref[...] = kvf[:, QFW:].astype(BF16)

    return pl.pallas_call(
        body, grid=(T // tm,), name="fwd_prep",
        in_specs=[_row(tm, QL, Z_QD // QL), _row(tm, KVL, Z_KVD // KVL), _row(tm, LANES, Z_KR // LANES),
                  _res((1, QL), layer), _res((1, KVL), layer), _res((QL, QFW), layer), _res((KVL, KVW), layer),
                  _row(tm, LANES), _row(tm, LANES), _row(tm, LANES)],
        out_specs=[_row(tm, QFW), _row(tm, QFW), _row(tm, MLA_H * MLA_V)],
        out_shape=[SDS((T, QFW), BF16), SDS((T, QFW), BF16), SDS((T, MLA_H * MLA_V), BF16)],
        compiler_params=_cp("parallel"))(z, z, z, gq, gkv, wq, wkv, tc, ts1, ts2)


def _mla_fwd(qf, kf, v, nb, seq, tq):
    T = qf.shape[0]
    nq = seq // tq
    pw = 2 * LANES
    pairs = [(qi, ki) for qi in range(nq) for ki in range(qi + 1)]
    qi_tab = jnp.array([qk[0] for qk in pairs], jnp.int32)
    ki_tab = jnp.array([qk[1] for qk in pairs], jnp.int32)

    def body(qi_ref, ki_ref, q_ref, k_ref, v_ref, o_ref, lse_ref, m_s, l_s, acc_s):
        qi, ki = qi_ref[pl.program_id(2)], ki_ref[pl.program_id(2)]

        @pl.when(ki == 0)
        def _():
            m_s[...] = jnp.full(m_s.shape, NEG, F32)
            l_s[...] = jnp.zeros(l_s.shape, F32)
            acc_s[...] = jnp.zeros(acc_s.shape, F32)

        def step(masked):
            if masked:
                keys = lax.broadcasted_iota(jnp.int32, (tq, tq), 0)
                queries = lax.broadcasted_iota(jnp.int32, (tq, tq), 1)
                mask = keys <= queries
            ss = []
            for j in range(2):
                wide = slice(LANES * j, LANES * (j + 1))
                s = _nt(k_ref[:, wide], q_ref[:, wide]) * MLA_SCALE
                ss.append(jnp.where(mask, s, NEG) if masked else s)
            ps, alphas = [], []
            for j in range(2):
                m_prev = m_s[j]
                m_new = jnp.maximum(m_prev, jnp.max(ss[j], axis=0, keepdims=True))
                alpha = jnp.exp(m_prev - m_new)
                p = jnp.exp(ss[j] - m_new)
                l_s[j] = alpha * l_s[j] + jnp.sum(p, axis=0, keepdims=True)
                m_s[j] = m_new
                ps.append(p.astype(BF16))
                alphas.append(alpha)
            for j in range(2):
                rows = slice(MLA_V * j, MLA_V * (j + 1))
                acc_s[rows, :] = alphas[j] * acc_s[rows, :] + _tn(v_ref[:, rows], ps[j])

        @pl.when(ki < qi)
        def _():
            step(False)

        @pl.when(ki == qi)
        def _():
            step(True)
            for j in range(2):
                rows = slice(MLA_V * j, MLA_V * (j + 1))
                acc_s[rows, :] = acc_s[rows, :] / l_s[j]
                lse_ref[j:j + 1, :] = m_s[j] + jnp.log(l_s[j])
            o_ref[...] = acc_s[...].T

    q_map = lambda b, hp, s, qi_ref, ki_ref: (b * nq + qi_ref[s], hp)
    kv_map = lambda b, hp, s, qi_ref, ki_ref: (b * nq + ki_ref[s], hp)
    return pl.pallas_call(
        body, name="mla_fwd",
        grid_spec=pltpu.PrefetchScalarGridSpec(
            num_scalar_prefetch=2, grid=(nb, MLA_H // 2, len(pairs)),
            in_specs=[pl.BlockSpec((tq, pw), q_map), pl.BlockSpec((tq, pw), kv_map),
                      pl.BlockSpec((tq, LANES), kv_map)],
            out_specs=[pl.BlockSpec((tq, LANES), q_map),
                       pl.BlockSpec((None, 2, tq), lambda b, hp, s, qi_ref, ki_ref: (hp, 0, b * nq + qi_ref[s]))],
            scratch_shapes=[pltpu.VMEM((2, 1, tq), F32), pltpu.VMEM((2, 1, tq), F32),
                            pltpu.VMEM((LANES, tq), F32)]),
        out_shape=[SDS((T, MLA_H * MLA_V), F32), SDS((MLA_H // 2, 2, T), F32)],
        compiler_params=_cp("parallel", "parallel", "arbitrary"))(qi_tab, ki_tab, qf, kf, v)


def _swa_specs(nblk):
    cur = lambda b, n: (b * nblk + n, 0)
    prev = lambda b, n: (b * nblk + jnp.maximum(n - 1, 0), 0)
    kvc = Z_AK // (2 * BLK)
    return [pl.BlockSpec(memory_space=pltpu.SMEM),
            pl.BlockSpec((BLK, 512), lambda b, n: (b * nblk + n, Z_AQ // 512)),
            pl.BlockSpec((BLK, 2 * BLK), lambda b, n: (b * nblk + n, kvc)),
            pl.BlockSpec((BLK, 2 * BLK), lambda b, n: (b * nblk + jnp.maximum(n - 1, 0), kvc)),
            pl.BlockSpec((BLK, 1), cur),
            pl.BlockSpec((BLK, 1), prev),
            pl.BlockSpec((1, 1, BLK), lambda b, n: (b * nblk + n, 0, 0))]


def _swa_scores(n, q_ref, kvc_ref, kvp_ref, pcc_ref, pcp_ref, pr_ref):
    kv = jnp.concatenate([kvp_ref[...], kvc_ref[...]], axis=0)
    kb, vb = kv[:, :BLK].astype(BF16), kv[:, BLK:].astype(BF16)
    dist = pr_ref[0] - jnp.concatenate([pcp_ref[...], pcc_ref[...]], axis=0)
    key = lax.broadcasted_iota(jnp.int32, (2 * BLK, BLK), 0)
    qry = lax.broadcasted_iota(jnp.int32, (2 * BLK, BLK), 1)
    valid = (key > qry) & (key <= qry + BLK) & ((key >= BLK) | (n > 0))

    def scores(h):
        g = h // (SWA_H // SWA_KV)
        qh = q_ref[:, SWA_DH * h:SWA_DH * (h + 1)].astype(BF16)
        s = _nt(kb[:, SWA_DH * g:SWA_DH * (g + 1)], qh) * SWA_SCALE - (2.0 ** -(h + 1)) * dist
        return qh, jnp.where(valid, s, NEG)

    return kb, vb, scores


def _swa_fwd(sink, z, pos_col, pos_row, nb, seq, layer):
    T = z.shape[0]
    nblk = seq // BLK

    def body(sink_ref, q_ref, kvc_ref, kvp_ref, pcc_ref, pcp_ref, pr_ref, o_ref, lse_ref):
        kb, vb, scores = _swa_scores(pl.program_id(1), q_ref, kvc_ref, kvp_ref, pcc_ref, pcp_ref, pr_ref)
        ss = [scores(h)[1] for h in range(SWA_H)]
        es, dens = [], []
        for h in range(SWA_H):
            sk = sink_ref[layer, h]
            m = jnp.maximum(jnp.max(ss[h], axis=0, keepdims=True), sk)
            e = jnp.exp(ss[h] - m)
            den = jnp.sum(e, axis=0, keepdims=True) + jnp.exp(sk - m)
            lse_ref[h:h + 1, :] = m + jnp.log(den)
            es.append(e.astype(BF16))
            dens.append(den)
        outs = []
        for h in range(SWA_H):
            g = h // (SWA_H // SWA_KV)
            outs.append(_tn(vb[:, SWA_DH * g:SWA_DH * (g + 1)], es[h]) / dens[h])
        o_ref[...] = jnp.concatenate(outs, axis=0).T

    return pl.pallas_call(
        body, grid=(nb, nblk), name="swa_fwd",
        in_specs=_swa_specs(nblk),
        out_specs=[pl.BlockSpec((BLK, 512), lambda b, n: (b * nblk + n, 0)),
                   pl.BlockSpec((SWA_H, BLK), lambda b, n: (0, b * nblk + n))],
        out_shape=[SDS((T, 512), F32), SDS((SWA_H, T), F32)],
        compiler_params=_cp("parallel", "parallel"))(sink, z, z, z, pos_col, pos_col, pos_row)


def _fwd_merge(x, oa, ob, z, wa, wb, wo, tm, layer):
    T = x.shape[0]

    def body(x_ref, oa_ref, ob_ref, ag_ref, bg_ref, ma_ref, mb_ref, wa_ref, wb_ref, wo_ref, x1_ref):
        ag, bg = ag_ref[...], bg_ref[...]
        ua = _nn((oa_ref[...] * (ag * _sig(ag))).astype(BF16), wa_ref[...])
        ub = _nn((ob_ref[...] * (bg * _sig(bg))).astype(BF16), wb_ref[...])
        y = _sig(ma_ref[...]) * ua + _sig(mb_ref[...]) * ub
        x1_ref[...] = x_ref[...] + _nn(y.astype(BF16), wo_ref[...])

    return pl.pallas_call(
        body, grid=(T // tm,), name="fwd_merge",
        in_specs=[_row(tm, D), _row(tm, 512), _row(tm, 512), _row(tm, 512, Z_AG // 512), _row(tm, 512, Z_BG // 512),
                  _row(tm, D, Z_MA // D), _row(tm, D, Z_MB // D),
                  _res((512, D), layer), _res((512, D), layer), _res((D, D), layer)],
        out_specs=_row(tm, D),
        out_shape=SDS((T, D), F32),
        compiler_params=_cp("parallel"))(x, oa, ob, z, z, z, z, wa, wb, wo)


def _fwd_ple(x1, p, g, wpg, wpp, tm, layer):
    T = x1.shape[0]

    def body(x_ref, p_ref, g_ref, wpg_ref, wpp_ref, x2_ref, pg_ref, pp_ref):
        xf = x_ref[...]
        h1 = ((xf * _rstd(xf)) * g_ref[...]).astype(BF16)
        pg = _sig(_nn(h1, wpg_ref[...]))
        pp = _nn(p_ref[...].astype(BF16), wpp_ref[...])
        pg_ref[...] = pg
        pp_ref[...] = pp
        x2_ref[...] = xf + pg * pp

    return pl.pallas_call(
        body, grid=(T // tm,), name="fwd_ple",
        in_specs=[_row(tm, D), pl.BlockSpec((None, tm, PLE), lambda i: (layer, i, 0)),
                  _res((1, D), layer), _res((D, D), layer), _res((PLE, D), layer)],
        out_specs=[_row(tm, D)] * 3,
        out_shape=[SDS((T, D), F32)] * 3,
        compiler_params=_cp("parallel"))(x1, p, g, wpg, wpp)


def _loss_head(x, g, tgt, tm):
    T = x.shape[0]

    def body(x_ref, g_ref, t_ref, dx_ref, dg_ref, loss_ref):
        @pl.when(pl.program_id(0) == 0)
        def _():
            dg_ref[...] = jnp.zeros(dg_ref.shape, F32)
            loss_ref[...] = jnp.zeros(loss_ref.shape, F32)

        xf, gf = x_ref[...], g_ref[...]
        r = _rstd(xf)
        n = xf * r
        err = n * gf - t_ref[...]
        loss_ref[...] += 0.5 * jnp.sum(jnp.mean(err * err, axis=-1, keepdims=True), axis=0, keepdims=True)
        dx, dgr = _norm_bwd(err * (1.0 / D), n, r, gf)
        dx_ref[...] = dx
        dg_ref[...] += jnp.sum(dgr, axis=0, keepdims=True)

    return pl.pallas_call(
        body, grid=(T // tm,), name="loss_head",
        in_specs=[_row(tm, D), _res((1, D)), _row(tm, D)],
        out_specs=[_row(tm, D), _acc((1, D)), _acc((1, LANES))],
        out_shape=[SDS((T, D), F32), SDS((1, D), F32), SDS((1, LANES), F32)],
        compiler_params=_cp("arbitrary"))(x, g, tgt)


def _bwd_ple(dx2, x1, pg, pp, p, g, wpg, tm, layer):
    T = x1.shape[0]

    def body(d_ref, x_ref, pg_ref, pp_ref, p_ref, g_ref, w_ref, dx_ref, dwg_ref, dwp_ref, dg_ref):
        @pl.when(pl.program_id(0) == 0)
        def _():
            dwg_ref[...] = jnp.zeros(dwg_ref.shape, F32)
            dwp_ref[...] = jnp.zeros(dwp_ref.shape, F32)
            dg_ref[...] = jnp.zeros(dg_ref.shape, F32)

        d, xf, pg, gf = d_ref[...], x_ref[...], pg_ref[...], g_ref[...]
        r = _rstd(xf)
        n = xf * r
        dpgl = (d * pp_ref[...] * pg * (1.0 - pg)).astype(BF16)
        dwg_ref[...] += _tn((n * gf).astype(BF16), dpgl)
        dwp_ref[...] += _tn(p_ref[...].astype(BF16), (d * pg).astype(BF16))
        dxn, dgr = _norm_bwd(_nt(dpgl, w_ref[...]), n, r, gf)
        dx_ref[...] = d + dxn
        dg_ref[...] += jnp.sum(dgr, axis=0, keepdims=True)

    return pl.pallas_call(
        body, grid=(T // tm,), name="bwd_ple",
        in_specs=[_row(tm, D)] * 4 + [pl.BlockSpec((None, tm, PLE), lambda i: (layer, i, 0)),
                                      _res((1, D), layer), _res((D, D), layer)],
        out_specs=[_row(tm, D), _acc((D, D)), _acc((PLE, D)), _acc((1, D))],
        out_shape=[SDS((T, D), F32), SDS((D, D), F32), SDS((PLE, D), F32), SDS((1, D), F32)],
        compiler_params=_cp("arbitrary"))(dx2, x1, pg, pp, p, g, wpg)


def _bwd_merge(dx1, oa, ob, z, wa, wb, wo, tm, layer):
    T = dx1.shape[0]

    def body(d_ref, oa_ref, ob_ref, ag_ref, bg_ref, ma_ref, mb_ref, wa_ref, wb_ref, wo_ref,
             doa_ref, dob_ref, dag_ref, dbg_ref, dma_ref, dmb_ref, dsa_ref, dsb_ref, dwa_ref, dwb_ref, dwo_ref):
        @pl.when(pl.program_id(0) == 0)
        def _():
            dwa_ref[...] = jnp.zeros(dwa_ref.shape, F32)
            dwb_ref[...] = jnp.zeros(dwb_ref.shape, F32)
            dwo_ref[...] = jnp.zeros(dwo_ref.shape, F32)

        db = d_ref[...].astype(BF16)
        gated = []
        for o_ref, gate_ref, w_ref in ((oa_ref, ag_ref, wa_ref), (ob_ref, bg_ref, wb_ref)):
            raw, gate = o_ref[...], gate_ref[...]
            sg = _sig(gate)
            silu = gate * sg
            ob16 = (raw * silu).astype(BF16)
            gated.append((raw, gate, sg, silu, ob16, _nn(ob16, w_ref[...])))
        ua, ub = gated[0][5], gated[1][5]
        sa, sb = _sig(ma_ref[...]), _sig(mb_ref[...])
        dwo_ref[...] += _tn((sa * ua + sb * ub).astype(BF16), db)
        dy = _nt(db, wo_ref[...])
        dma_ref[...] = (dy * ua * sa * (1.0 - sa)).astype(BF16)
        dmb_ref[...] = (dy * ub * sb * (1.0 - sb)).astype(BF16)
        for (s, w_ref, do_ref, dgate_ref, dw_ref, ds_ref), (raw, gate, sg, silu, ob16, _) in zip((
                (sa, wa_ref, doa_ref, dag_ref, dwa_ref, dsa_ref),
                (sb, wb_ref, dob_ref, dbg_ref, dwb_ref, dsb_ref)), gated):
            du = (dy * s).astype(BF16)
            dw_ref[...] += _tn(ob16, du)
            do = _nt(du, w_ref[...])
            draw = do * silu
            do_ref[...] = draw.astype(BF16)
            dgate_ref[...] = (do * raw * (sg * (1.0 + gate * (1.0 - sg)))).astype(BF16)
            ds_ref[...] = jnp.sum((draw * raw).T.reshape(MLA_H, MLA_V, tm), axis=1)

    return pl.pallas_call(
        body, grid=(T // tm,), name="bwd_merge",
        in_specs=[_row(tm, D), _row(tm, 512), _row(tm, 512), _row(tm, 512, Z_AG // 512), _row(tm, 512, Z_BG // 512),
                  _row(tm, D, Z_MA // D), _row(tm, D, Z_MB // D),
                  _res((512, D), layer), _res((512, D), layer), _res((D, D), layer)],
        out_specs=[_row(tm, 512)] * 4 + [_row(tm, D)] * 2 + [pl.BlockSpec((MLA_H, tm), lambda i: (0, i))] * 2
        + [_acc((512, D)), _acc((512, D)), _acc((D, D))],
        out_shape=[SDS((T, 512), BF16), SDS((T, 512), BF16), SDS((T, 512), BF16), SDS((T, 512), BF16),
                   SDS((T, D), BF16), SDS((T, D), BF16), SDS((MLA_H, T), F32), SDS((MLA_H, T), F32),
                   SDS((512, D), F32), SDS((512, D), F32), SDS((D, D), F32)],
        compiler_params=_cp("arbitrary"))(dx1, oa, ob, z, z, z, z, wa, wb, wo)


def _mla_bwd(qf, kf, v, do, lse, dsum, nb, seq, tq):
    T = qf.shape[0]
    nq = seq // tq
    pw = 2 * LANES
    pairs = [(qi, ki) for ki in range(nq) for qi in range(ki, nq)]
    qi_tab = jnp.array([qk[0] for qk in pairs], jnp.int32)
    ki_tab = jnp.array([qk[1] for qk in pairs], jnp.int32)

    def body(qi_ref, ki_ref, q_ref, k_ref, v_ref, do_ref, lse_ref, dsum_ref, dq_ref, dk_ref, dv_ref, dk_s, dv_s, dqt_s):
        step_id = pl.program_id(2)
        qi, ki = qi_ref[step_id], ki_ref[step_id]

        @pl.when(step_id == 0)
        def _():
            dqt_s[...] = jnp.zeros(dqt_s.shape, F32)

        @pl.when(qi == ki)
        def _():
            dk_s[...] = jnp.zeros(dk_s.shape, F32)
            dv_s[...] = jnp.zeros(dv_s.shape, F32)

        def step(masked):
            if masked:
                keys = lax.broadcasted_iota(jnp.int32, (tq, tq), 0)
                queries = lax.broadcasted_iota(jnp.int32, (tq, tq), 1)
                mask = keys <= queries
            for j in range(2):
                wide = slice(LANES * j, LANES * (j + 1))
                sl = slice(MLA_V * j, MLA_V * (j + 1))
                q, k = q_ref[:, wide], k_ref[:, wide]
                dob = do_ref[:, sl].astype(BF16)
                s = _nt(k, q) * MLA_SCALE
                if masked:
                    s = jnp.where(mask, s, NEG)
                p = jnp.exp(s - lse_ref[j:j + 1, :])
                dv_s[:, sl] += _nn(p.astype(BF16), dob)
                ds = (p * (_nt(v_ref[:, sl], dob) - dsum_ref[j:j + 1, :]) * MLA_SCALE).astype(BF16)
                dk_s[:, wide] += _nn(ds, q)
                dqt_s[qi, wide, :] += _tn(k, ds)

        @pl.when(qi > ki)
        def _():
            step(False)

        @pl.when(qi == ki)
        def _():
            step(True)

        @pl.when(qi == nq - 1)
        def _():
            dk_ref[...] = dk_s[...]
            dv_ref[...] = dv_s[...]

        @pl.when(step_id == len(pairs) - 1)
        def _():
            for n in range(nq):
                dq_ref[tq * n:tq * (n + 1), :] = dqt_s[n].T

    qmap = lambda b, hp, s, qi_ref, ki_ref: (b * nq + qi_ref[s], hp)
    kmap = lambda b, hp, s, qi_ref, ki_ref: (b * nq + ki_ref[s], hp)
    stat = pl.BlockSpec((None, 2, tq), lambda b, hp, s, qi_ref, ki_ref: (hp, 0, b * nq + qi_ref[s]))
    return pl.pallas_call(
        body, name="mla_bwd",
        grid_spec=pltpu.PrefetchScalarGridSpec(
            num_scalar_prefetch=2, grid=(nb, MLA_H // 2, len(pairs)),
            in_specs=[pl.BlockSpec((tq, pw), qmap), pl.BlockSpec((tq, pw), kmap), pl.BlockSpec((tq, LANES), kmap),
                      pl.BlockSpec((tq, LANES), qmap), stat, stat],
            out_specs=[pl.BlockSpec((seq, pw), lambda b, hp, s, qi_ref, ki_ref: (b, hp)),
                       pl.BlockSpec((tq, pw), kmap), pl.BlockSpec((tq, LANES), kmap)],
            scratch_shapes=[pltpu.VMEM((tq, pw), F32), pltpu.VMEM((tq, LANES), F32),
                            pltpu.VMEM((nq, pw, tq), F32)]),
        out_shape=[SDS((T, QFW), F32), SDS((T, QFW), F32), SDS((T, MLA_H * MLA_V), F32)],
        compiler_params=_cp("parallel", "parallel", "arbitrary"))(qi_tab, ki_tab, qf, kf, v, do, lse, dsum)


def _swa_bwd(sink, z, pos_col, pos_row, do, lse, dsum, nb, seq, layer):
    T = z.shape[0]
    nblk = seq // BLK

    def body(sink_ref, q_ref, kvc_ref, kvp_ref, pcc_ref, pcp_ref, pr_ref, do_ref, lse_ref, dsum_ref,
             dq_ref, dkv_ref, dsink_ref):
        b, n = pl.program_id(0), pl.program_id(1)

        @pl.when((b == 0) & (n == 0))
        def _():
            dsink_ref[...] = jnp.zeros(dsink_ref.shape, F32)

        @pl.when(n == 0)
        def _():
            dkv_ref[...] = jnp.zeros(dkv_ref.shape, F32)

        kb, vb, scores = _swa_scores(n, q_ref, kvc_ref, kvp_ref, pcc_ref, pcp_ref, pr_ref)
        lane = lax.broadcasted_iota(jnp.int32, (1, LANES), 1)
        dsink = jnp.zeros((1, LANES), F32)
        dkv = [[None, None], [None, None]]
        dqs = []
        gsl = lambda h: slice(SWA_DH * (h // (SWA_H // SWA_KV)), SWA_DH * (h // (SWA_H // SWA_KV) + 1))
        qs, ss, dobs, dps = [], [], [], []
        for h in range(SWA_H):
            qh, s = scores(h)
            dob = do_ref[:, SWA_DH * h:SWA_DH * (h + 1)].astype(BF16)
            qs.append(qh)
            ss.append(s)
            dobs.append(dob)
            dps.append(_nt(vb[:, gsl(h)], dob))
        pbs, dss = [], []
        for h in range(SWA_H):
            lse, dsum = lse_ref[h:h + 1, :], dsum_ref[h:h + 1, :]
            p = jnp.exp(ss[h] - lse)
            pbs.append(p.astype(BF16))
            dss.append((p * (dps[h] - dsum) * SWA_SCALE).astype(BF16))
            dsk = jnp.sum(-jnp.exp(sink_ref[layer, h] - lse) * dsum, axis=1, keepdims=True)
            dsink = dsink + jnp.where(lane == h, dsk, 0.0)
        for h in range(SWA_H):
            g = h // (SWA_H // SWA_KV)
            dqs.append(_tn(kb[:, gsl(h)], dss[h]))
            dk, dv = _nn(dss[h], qs[h]), _nn(pbs[h], dobs[h])
            dkv[g][0] = dk if dkv[g][0] is None else dkv[g][0] + dk
            dkv[g][1] = dv if dkv[g][1] is None else dkv[g][1] + dv
        dq_ref[...] = jnp.concatenate(dqs, axis=0).T.astype(BF16)
        dsink_ref[...] += dsink
        upd = jnp.concatenate([dkv[0][0], dkv[1][0], dkv[0][1], dkv[1][1]], axis=1)
        dkv_ref[pl.ds(pl.multiple_of(n * BLK, BLK), BLK), :] += upd[BLK:]

        @pl.when(n > 0)
        def _():
            dkv_ref[pl.ds(pl.multiple_of((n - 1) * BLK, BLK), BLK), :] += upd[:BLK]

    return pl.pallas_call(
        body, grid=(nb, nblk), name="swa_bwd",
        in_specs=_swa_specs(nblk) + [pl.BlockSpec((BLK, 512), lambda b, n: (b * nblk + n, 0))]
        + [pl.BlockSpec((SWA_H, BLK), lambda b, n: (0, b * nblk + n))] * 2,
        out_specs=[pl.BlockSpec((BLK, 512), lambda b, n: (b * nblk + n, 0)),
                   pl.BlockSpec((seq, 2 * BLK), lambda b, n: (b, 0)),
                   pl.BlockSpec((1, LANES), lambda b, n: (0, 0))],
        out_shape=[SDS((T, 512), BF16), SDS((T, 2 * BLK), F32), SDS((1, LANES), F32)],
        compiler_params=_cp("arbitrary", "arbitrary"))(sink, z, z, z, pos_col, pos_col, pos_row, do, lse, dsum)


def _bwd_prep(dq, dk, dv, z, gq, gkv, wq, wkv, tc, ts1, ts2, tm, layer):
    T = z.shape[0]

    def body(dq_ref, dk_ref, dv_ref, qd_ref, kvd_ref, gq_ref, gkv_ref, wq_ref, wkv_ref, c_ref, s1_ref, s2_ref,
             dqd_ref, dkvd_ref, dkr_ref, dwq_ref, dwkv_ref, dgq_ref, dgkv_ref, dqb_s, dkvb_s):
        @pl.when(pl.program_id(0) == 0)
        def _():
            for ref in (dwq_ref, dwkv_ref, dgq_ref, dgkv_ref):
                ref[...] = jnp.zeros(ref.shape, F32)

        c, s1, s2 = c_ref[...], s1_ref[...], s2_ref[...]
        lane = lax.broadcasted_iota(jnp.int32, (1, LANES), 1)
        rope_lanes = (lane >= MLA_NOPE) & (lane < MLA_QK)
        dkb = jnp.zeros((tm, LANES), F32)
        for h in range(MLA_H):
            sl = slice(LANES * h, LANES * (h + 1))
            dqb_s[:, sl] = _rope_t(dq_ref[:, sl], c, s1, s2).astype(BF16)
            dkh = dk_ref[:, sl]
            dkb = dkb + dkh
            dkvb_s[:, sl] = dkh.astype(BF16)
        dkvb_s[:, QFW:] = dv_ref[...].astype(BF16)
        dkr_ref[...] = _rope_t(jnp.where(rope_lanes, dkb, 0.0), c, s1, s2).astype(BF16)

        for (x_ref, g_ref, w_ref, d_s, dx_ref, dw_ref, dg_ref) in (
                (qd_ref, gq_ref, wq_ref, dqb_s, dqd_ref, dwq_ref, dgq_ref),
                (kvd_ref, gkv_ref, wkv_ref, dkvb_s, dkvd_ref, dwkv_ref, dgkv_ref)):
            xf, gf, db = x_ref[...], g_ref[...], d_s[...]
            r = _rstd(xf)
            n = xf * r
            dw_ref[...] += _tn((n * gf).astype(BF16), db)
            dx, dgr = _norm_bwd(_nt(db, w_ref[...]), n, r, gf)
            dx_ref[...] = dx.astype(BF16)
            dg_ref[...] += jnp.sum(dgr, axis=0, keepdims=True)

    return pl.pallas_call(
        body, grid=(T // tm,), name="bwd_prep",
        in_specs=[_row(tm, QFW), _row(tm, QFW), _row(tm, MLA_H * MLA_V),
                  _row(tm, QL, Z_QD // QL), _row(tm, KVL, Z_KVD // KVL),
                  _res((1, QL), layer), _res((1, KVL), layer), _res((QL, QFW), layer), _res((KVL, KVW), layer),
                  _row(tm, LANES), _row(tm, LANES), _row(tm, LANES)],
        out_specs=[_row(tm, QL), _row(tm, KVL), _row(tm, LANES),
                   _acc((QL, QFW)), _acc((KVL, KVW)), _acc((1, QL)), _acc((1, KVL))],
        out_shape=[SDS((T, QL), BF16), SDS((T, KVL), BF16), SDS((T, LANES), BF16),
                   SDS((QL, QFW), F32), SDS((KVL, KVW), F32), SDS((1, QL), F32), SDS((1, KVL), F32)],
        scratch_shapes=[pltpu.VMEM((tm, QFW), BF16), pltpu.VMEM((tm, KVW), BF16)],
        compiler_params=_cp("arbitrary"))(dq, dk, dv, z, z, gq, gkv, wq, wkv, tc, ts1, ts2)


def _bwd_in(pieces, x, g, dres, w, tm, layer):
    T = x.shape[0]
    widths = [pc.shape[1] for pc in pieces]
    assert sum(widths) == ZW
    n_p = len(pieces)

    def body(*refs):
        p_refs, (x_ref, g_ref, r_ref, w_ref, dx_ref, dz_ref, dg_ref) = refs[:n_p], refs[n_p:]

        @pl.when(pl.program_id(0) == 0)
        def _():
            dg_ref[...] = jnp.zeros(dg_ref.shape, F32)

        off = 0
        for ref, wd in zip(p_refs, widths):
            dz_ref[:, off:off + wd] = ref[...].astype(BF16)
            off += wd
        xf, gf = x_ref[...], g_ref[...]
        r = _rstd(xf)
        n = xf * r
        dx, dgr = _norm_bwd(_nt(dz_ref[...], w_ref[...]), n, r, gf)
        dx_ref[...] = r_ref[...] + dx
        dg_ref[...] += jnp.sum(dgr, axis=0, keepdims=True)

    return pl.pallas_call(
        body, grid=(T // tm,), name="bwd_in",
        in_specs=[_row(tm, wd) for wd in widths] + [_row(tm, D), _res((1, D), layer), _row(tm, D),
                                                    _res((D, ZW), layer)],
        out_specs=[_row(tm, D), _row(tm, ZW), _acc((1, D))],
        out_shape=[SDS((T, D), F32), SDS((T, ZW), BF16), SDS((1, D), F32)],
        compiler_params=_cp("arbitrary"))(*pieces, x, g, dres, w)


def _wgrad_in(hb, dzb, tm):
    T = hb.shape[0]
    half = ZW // 2

    def body(h_ref, dz_ref, dw_ref):
        @pl.when(pl.program_id(1) == 0)
        def _():
            dw_ref[...] = jnp.zeros(dw_ref.shape, F32)

        dw_ref[...] += _tn(h_ref[...], dz_ref[...])

    return pl.pallas_call(
        body, grid=(2, T // tm), name="wgrad_in",
        in_specs=[pl.BlockSpec((tm, D), lambda j, t: (t, 0)), pl.BlockSpec((tm, half), lambda j, t: (t, j))],
        out_specs=pl.BlockSpec((D, half), lambda j, t: (0, j)),
        out_shape=SDS((D, ZW), F32),
        compiler_params=_cp("parallel", "arbitrary"))(hb, dzb)


def _win_to_kernel(w):
    s = lambda a, b: w[..., a:b]
    zero = lambda n: jnp.zeros(w.shape[:-1] + (n,), w.dtype)
    return jnp.concatenate([s(2208, 3232), s(3232, 4256), s(0, 512), s(768, 1280), s(1696, 2208), s(1280, 1536),
                            s(512, 640), s(640, 768), s(1536, 1664), zero(64), s(1664, 1696), zero(32)], axis=-1)


def _win_from_kernel(g):
    s = lambda a, n: g[..., a:a + n]
    return jnp.concatenate([s(Z_AQ, 512), s(Z_AK, 128), s(Z_AV, 128), s(Z_AG, 512), s(Z_QD, 256), s(Z_KVD, 128),
                            s(Z_KR + MLA_NOPE, MLA_ROPE), s(Z_BG, 512), s(Z_MA, 1024), s(Z_MB, 1024)], axis=-1)


def _wuq_to_kernel(w):
    w = w.reshape(w.shape[:-1] + (MLA_H, MLA_QK))
    w = jnp.pad(w, [(0, 0)] * (w.ndim - 1) + [(0, LANES - MLA_QK)])
    return w.reshape(w.shape[:-2] + (QFW,))


def _wuq_from_kernel(g):
    g = g.reshape(g.shape[:-1] + (MLA_H, LANES))[..., :MLA_QK]
    return g.reshape(g.shape[:-2] + (MLA_H * MLA_QK,))


def _wukv_to_kernel(w):
    w = w.reshape(w.shape[:-1] + (MLA_H, MLA_NOPE + MLA_V))
    k = jnp.pad(w[..., :MLA_NOPE], [(0, 0)] * (w.ndim - 1) + [(0, LANES - MLA_NOPE)])
    v = w[..., MLA_NOPE:]
    return jnp.concatenate([k.reshape(k.shape[:-2] + (QFW,)), v.reshape(v.shape[:-2] + (MLA_H * MLA_V,))], axis=-1)


def _wukv_from_kernel(g):
    k = g[..., :QFW].reshape(g.shape[:-1] + (MLA_H, LANES))[..., :MLA_NOPE]
    v = g[..., QFW:].reshape(g.shape[:-1] + (MLA_H, MLA_V))
    kv = jnp.concatenate([k, v], axis=-1)
    return kv.reshape(kv.shape[:-2] + (MLA_H * (MLA_NOPE + MLA_V),))


def _rope_tables(pos):
    half = MLA_ROPE // 2
    inv = 10000.0 ** (-jnp.arange(0, MLA_ROPE, 2, dtype=F32) / MLA_ROPE)
    ang = pos.astype(F32)[:, None] * inv
    cos, sin = jnp.cos(ang), jnp.sin(ang)
    one = jnp.ones((pos.shape[0], MLA_NOPE), F32)
    zero = lambda n: jnp.zeros((pos.shape[0], n), F32)
    tc = jnp.concatenate([one, cos, cos, one[:, :LANES - MLA_QK]], axis=1)
    ts1 = jnp.concatenate([zero(MLA_NOPE + half), sin, zero(LANES - MLA_QK)], axis=1)
    ts2 = jnp.concatenate([zero(MLA_NOPE), -sin, zero(LANES - MLA_NOPE - half)], axis=1)
    return tc, ts1, ts2


def _local_step(x, p, positions, loss_target, small, wts):
    nb, seq, _ = x.shape
    T = nb * seq
    tm = min(256, T)
    tq = min(512, seq)
    xf = x.reshape(T, D)
    pos = positions.reshape(T)
    posf = pos.astype(F32)
    pos_col, pos_row = posf.reshape(T, 1), posf.reshape(T // BLK, 1, BLK)
    tc, ts1, ts2 = _rope_tables(pos)

    w, sm = wts, small
    pl_in = p.reshape(DEPTH, T, PLE)
    saved = []
    for i in range(DEPTH):
        z, hb = _fwd_in(xf, sm["g_mix"], w["w_in"], tm, i)
        oa, lse_a = _swa_fwd(sm["sink"], z, pos_col, pos_row, nb, seq, i)
        qf, kf, v = _fwd_prep(z, sm["g_q"], sm["g_kv"], w["w_uq"], w["w_ukv"], tc, ts1, ts2, tm, i)
        ob, lse_b = _mla_fwd(qf, kf, v, nb, seq, tq)
        x1 = _fwd_merge(xf, oa, ob, z, w["w_br_a"], w["w_br_b"], w["w_out"], tm, i)
        x2, pg, pp = _fwd_ple(x1, pl_in, sm["g_ple"], w["w_ple_gate"], w["w_ple_proj"], tm, i)
        saved.append(dict(x=xf, z=z, hb=hb, oa=oa, lse_a=lse_a, qf=qf, kf=kf, v=v, ob=ob, lse_b=lse_b,
                          x1=x1, pg=pg, pp=pp))
        xf = x2

    dx, dg_final, loss = _loss_head(xf, small["g_final"], loss_target.reshape(T, D), tm)

    grads = [None] * DEPTH
    for i in reversed(range(DEPTH)):
        sv = saved[i]
        dx1, dwpg, dwpp, dg_ple = _bwd_ple(dx, sv["x1"], sv["pg"], sv["pp"], pl_in, sm["g_ple"], w["w_ple_gate"],
                                           tm, i)
        doa, dob, dag, dbg, dma, dmb, dsum_a, dsum_b, dwa, dwb, dwo = _bwd_merge(
            dx1, sv["oa"], sv["ob"], sv["z"], w["w_br_a"], w["w_br_b"], w["w_out"], tm, i)
        dq_b, dk_b, dv_b = _mla_bwd(sv["qf"], sv["kf"], sv["v"], dob, sv["lse_b"],
                                    dsum_b.reshape(MLA_H // 2, 2, T), nb, seq, tq)
        dqd, dkvd, dkr, dwq, dwkv, dgq, dgkv = _bwd_prep(dq_b, dk_b, dv_b, sv["z"], sm["g_q"], sm["g_kv"],
                                                         w["w_uq"], w["w_ukv"], tc, ts1, ts2, tm, i)
        dq_a, dkv_a, dsink = _swa_bwd(sm["sink"], sv["z"], pos_col, pos_row, doa, sv["lse_a"], dsum_a, nb, seq, i)
        dx, dzb, dg_mix = _bwd_in([dma, dmb, dq_a, dag, dbg, dqd, dkv_a, dkvd, dkr], sv["x"], sm["g_mix"], dx1,
                                  w["w_in"], tm, i)
        dwin = _wgrad_in(sv["hb"], dzb, tm)
        grads[i] = dict(g_mix=dg_mix[0], w_in=_win_from_kernel(dwin), sink=dsink[0, :SWA_H], g_q=dgq[0],
                        w_uq=_wuq_from_kernel(dwq), g_kv=dgkv[0], w_ukv=_wukv_from_kernel(dwkv), w_br_a=dwa,
                        w_br_b=dwb, w_out=dwo, g_ple=dg_ple[0], w_ple_gate=dwpg, w_ple_proj=dwpp)
    return loss, dx.reshape(nb, seq, D), grads, dg_final[0]


def _kernel_weights(full):
    out = dict(full)
    out.update(w_in=_win_to_kernel(full["w_in"]), w_uq=_wuq_to_kernel(full["w_uq"]),
               w_ukv=_wukv_to_kernel(full["w_ukv"]))
    return out


def _small_params(g_mix, sink, g_q, g_kv, g_ple, g_final):
    return dict(g_mix=g_mix[:, None], sink=sink, g_q=g_q[:, None], g_kv=g_kv[:, None], g_ple=g_ple[:, None],
                g_final=g_final[None])


WIDE_ROWS, WIDE_W = DEPTH * D, IN_W // N_DEV
UQ_W = MLA_H * MLA_QK // N_DEV
ROWS_PIECES = (("w_uq", DEPTH * QL), ("w_ukv", DEPTH * KVL), ("w_br_a", DEPTH * 512), ("w_br_b", DEPTH * 512),
               ("w_out", DEPTH * D), ("w_ple_gate", DEPTH * D), ("w_ple_proj", DEPTH * PLE))
SMALL = (("g_mix", (DEPTH, D)), ("sink", (DEPTH, SWA_H)), ("g_q", (DEPTH, QL)), ("g_kv", (DEPTH, KVL)),
         ("g_ple", (DEPTH, D)), ("g_final", (D,)))
VEC_ROWS = 48
ROWS_N = sum(r for _, r in ROWS_PIECES)
WIDE_TILE, ROWS_TILE = 256, ROWS_N // 4


def _to_rows(name, a):
    lead = a.shape[:-3]
    if name == "w_uq":
        a = jnp.pad(a, [(0, 0)] * (a.ndim - 1) + [(0, LANES - UQ_W)])
    return a.reshape(lead + (-1, LANES))


def _from_rows(name, r):
    lead = r.shape[:-2]
    if name in ("w_out", "w_ple_gate"):
        return r.reshape(lead + (DEPTH, D // N_DEV, D))
    r = r.reshape(lead + (DEPTH, -1, LANES))
    return r[..., :UQ_W] if name == "w_uq" else r


def _pack_rows(blocks):
    return jnp.concatenate([_to_rows(n, blocks[n]) for n, _ in ROWS_PIECES], axis=-2)


def _unpack_rows(rows):
    blocks, off = {}, 0
    for n, r in ROWS_PIECES:
        blocks[n] = _from_rows(n, rows[..., off:off + r, :])
        off += r
    return blocks


def _pack_vec(vectors, loss=None):
    parts = [vectors[n].reshape(-1) for n, _ in SMALL] + ([] if loss is None else [loss.reshape(1)])
    vec = jnp.concatenate(parts)
    return jnp.pad(vec, (0, VEC_ROWS * LANES - vec.shape[0])).reshape(1, VEC_ROWS, LANES)


def _unpack_vec(vec):
    vec = vec.reshape(-1)
    vectors, off = {}, 0
    for n, shp in SMALL:
        size = 1
        for s in shp:
            size *= s
        vectors[n] = vec[off:off + size].reshape(shp)
        off += size
    return vectors, vec[off]


def _join(name, blocks):
    if name in ("w_out", "w_ple_gate"):
        return jnp.moveaxis(blocks, 0, 1).reshape(DEPTH, -1, blocks.shape[-1])
    return jnp.moveaxis(blocks, 0, 2).reshape(DEPTH, blocks.shape[2], -1)


def _split(name, full):
    if name in ("w_out", "w_ple_gate"):
        return jnp.moveaxis(full.reshape(DEPTH, N_DEV, -1, full.shape[-1]), 1, 0)
    return jnp.moveaxis(full.reshape(DEPTH, full.shape[1], N_DEV, -1), 2, 0)


MESH_ID = pl.DeviceIdType.MESH
ANY = pl.BlockSpec(memory_space=pl.ANY)


def _place():
    return lax.axis_index("x"), lax.axis_index("y"), lax.axis_index("c")


def _all_gather(blocks):
    n = len(blocks)

    def body(*refs):
        x_refs, out_refs, (send_sems, recv_sems, local_sems) = refs[:n], refs[n:2 * n], refs[2 * n:]
        x, y, c = _place()
        me, sibling = (x, y, c), (x, y, 1 - c)
        chips = [(1 - x, y), (x, 1 - y), (1 - x, 1 - y)]

        def slot(a, px, py, pc):
            return out_refs[a].at[4 * px + 2 * py + pc]

        def copy(a, k, blk, to, src=None):
            return pltpu.make_async_remote_copy(
                src_ref=slot(a, *blk) if src is None else src, dst_ref=slot(a, *blk),
                send_sem=send_sems.at[7 * a + k], recv_sem=recv_sems.at[7 * a + k], device_id=to,
                device_id_type=MESH_ID)

        mine = [pltpu.make_async_copy(x_refs[a], slot(a, *me), local_sems.at[a]) for a in range(n)]
        for cp in mine:
            cp.start()
        first = []
        for a in range(n):
            first += [copy(a, 0, me, sibling, src=x_refs[a])]
            first += [copy(a, 1 + j, me, (*chip, c), src=x_refs[a]) for j, chip in enumerate(chips)]
        for cp in first:
            cp.start()
        passed = []
        for j, chip in enumerate(chips):
            for a in range(n):
                copy(a, 1 + j, (*chip, c), me).wait_recv()
                passed.append(copy(a, 4 + j, (*chip, c), sibling))
                passed[-1].start()
        for a in range(n):
            copy(a, 0, sibling, me).wait_recv()
            for j, chip in enumerate(chips):
                copy(a, 4 + j, (*chip, 1 - c), me).wait_recv()
        for cp in first + passed:
            cp.wait_send()
        for cp in mine:
            cp.wait()

    return pl.pallas_call(
        body, name="all_gather_weights",
        out_shape=[SDS((N_DEV,) + b.shape, b.dtype) for b in blocks],
        in_specs=[ANY] * n, out_specs=[ANY] * n,
        scratch_shapes=[pltpu.SemaphoreType.DMA((7 * n,)), pltpu.SemaphoreType.DMA((7 * n,)),
                        pltpu.SemaphoreType.DMA((n,))],
    )(*blocks)


def _swap_sibling(arrs):
    n = len(arrs)

    def body(*refs):
        a_refs, out_refs, (send_sems, recv_sems) = refs[:n], refs[n:2 * n], refs[2 * n:]
        x, y, c = _place()
        copies = [pltpu.make_async_remote_copy(
            src_ref=a_refs[a].at[:, 1 - c], dst_ref=out_refs[a], send_sem=send_sems.at[a], recv_sem=recv_sems.at[a],
            device_id=(x, y, 1 - c), device_id_type=MESH_ID) for a in range(n)]
        for cp in copies:
            cp.start()
        for cp in copies:
            cp.wait()

    return pl.pallas_call(
        body, name="swap_sibling", out_shape=[SDS((a.shape[0],) + a.shape[2:], a.dtype) for a in arrs],
        in_specs=[ANY] * n, out_specs=[ANY] * n,
        scratch_shapes=[pltpu.SemaphoreType.DMA((n,)), pltpu.SemaphoreType.DMA((n,))],
    )(*arrs)


def _exchange_chips(arrs):
    n = len(arrs)

    def body(*refs):
        p_refs, out_refs, (send_sems, recv_sems, local_sems) = refs[:n], refs[n:2 * n], refs[2 * n:]
        x, y, c = _place()
        mine = 2 * x + y
        local = [pltpu.make_async_copy(p_refs[a].at[mine], out_refs[a].at[mine], local_sems.at[a]) for a in range(n)]
        for cp in local:
            cp.start()
        peers = [(1 - x, y), (x, 1 - y), (1 - x, 1 - y)]

        def copy(a, j, src_chip, dst_chip):
            px, py = peers[j]
            return pltpu.make_async_remote_copy(
                src_ref=p_refs[a].at[src_chip], dst_ref=out_refs[a].at[dst_chip], send_sem=send_sems.at[3 * a + j],
                recv_sem=recv_sems.at[3 * a + j], device_id=(px, py, c), device_id_type=MESH_ID)

        copies = [copy(a, j, 2 * px + py, mine) for a in range(n) for j, (px, py) in enumerate(peers)]
        for cp in copies:
            cp.start()
        for a in range(n):
            for j, (px, py) in enumerate(peers):
                copy(a, j, mine, 2 * px + py).wait_recv()
        for cp in copies:
            cp.wait_send()
        for cp in local:
            cp.wait()

    return pl.pallas_call(
        body, name="exchange_chips", out_shape=[SDS(a.shape, a.dtype) for a in arrs],
        in_specs=[ANY] * n, out_specs=[ANY] * n,
        scratch_shapes=[pltpu.SemaphoreType.DMA((3 * n,)), pltpu.SemaphoreType.DMA((3 * n,)),
                        pltpu.SemaphoreType.DMA((n,))],
    )(*arrs)


def _add_mine(g, recv, core, tile, dtype):
    _, _, lead, rows, width = g.shape

    def body(c_ref, g_ref, r_ref, o_ref):
        o_ref[...] = (g_ref[...] + r_ref[...]).astype(dtype)

    spec = pl.BlockSpec((None, None, tile, width), lambda k, l, i, c_ref: (k, l, i, 0))
    return pl.pallas_call(
        body, name="add_sibling", out_shape=SDS(recv.shape, dtype),
        grid_spec=pltpu.PrefetchScalarGridSpec(
            num_scalar_prefetch=1, grid=(g.shape[0], lead, rows // tile),
            in_specs=[pl.BlockSpec((None, None, None, tile, width), lambda k, l, i, c_ref: (k, c_ref[0], l, i, 0)),
                      spec],
            out_specs=spec),
        compiler_params=_cp("parallel", "parallel", "parallel"))(core, g, recv)


def _sum_adamw(parts, w, m, v, tile):
    lead, rows, width = w.shape

    def body(p_ref, w_ref, m_ref, v_ref, g_ref, d_ref, nm_ref, nv_ref):
        g = ((p_ref[0].astype(F32) + p_ref[1].astype(F32)) + p_ref[2].astype(F32)) + p_ref[3].astype(F32)
        nm = ADAM_B1 * m_ref[...] + (1.0 - ADAM_B1) * g
        nv = ADAM_B2 * v_ref[...] + (1.0 - ADAM_B2) * jnp.square(g)
        m_hat = nm / (1.0 - ADAM_B1 ** ADAM_STEP)
        v_hat = nv / (1.0 - ADAM_B2 ** ADAM_STEP)
        g_ref[...] = g
        nm_ref[...] = nm
        nv_ref[...] = nv
        d_ref[...] = -ADAM_LR * (m_hat / (jnp.sqrt(v_hat) + ADAM_EPS) + ADAM_WD * w_ref[...])

    spec = pl.BlockSpec((None, tile, width), lambda l, i: (l, i, 0))
    return pl.pallas_call(
        body, grid=(lead, rows // tile), name="sum_adamw",
        in_specs=[pl.BlockSpec((4, None, tile, width), lambda l, i: (0, l, i, 0)), spec, spec, spec],
        out_specs=[spec] * 4, out_shape=[SDS((lead, rows, width), F32)] * 4,
        compiler_params=_cp("parallel", "parallel"))(parts, w, m, v)


def kernel(x, p, positions, g_mix, w_in, sink, g_q, w_uq, g_kv, w_ukv, w_br_a, w_br_b, w_out, g_ple, w_ple_gate, w_ple_proj, g_final, loss_target, m_g_mix, m_w_in, m_sink, m_g_q, m_w_uq, m_g_kv, m_w_ukv, m_w_br_a, m_w_br_b, m_w_out, m_g_ple, m_w_ple_gate, m_w_ple_proj, m_g_final, v_g_mix, v_w_in, v_sink, v_g_q, v_w_uq, v_g_kv, v_w_ukv, v_w_br_a, v_w_br_b, v_w_out, v_g_ple, v_w_ple_gate, v_w_ple_proj, v_g_final):
    weights = dict(g_mix=g_mix, w_in=w_in, sink=sink, g_q=g_q, w_uq=w_uq, g_kv=g_kv, w_ukv=w_ukv, w_br_a=w_br_a,
                   w_br_b=w_br_b, w_out=w_out, g_ple=g_ple, w_ple_gate=w_ple_gate, w_ple_proj=w_ple_proj,
                   g_final=g_final)
    mom1 = dict(g_mix=m_g_mix, w_in=m_w_in, sink=m_sink, g_q=m_g_q, w_uq=m_w_uq, g_kv=m_g_kv, w_ukv=m_w_ukv,
                w_br_a=m_w_br_a, w_br_b=m_w_br_b, w_out=m_w_out, g_ple=m_g_ple, w_ple_gate=m_w_ple_gate,
                w_ple_proj=m_w_ple_proj, g_final=m_g_final)
    mom2 = dict(g_mix=v_g_mix, w_in=v_w_in, sink=v_sink, g_q=v_g_q, w_uq=v_w_uq, g_kv=v_g_kv, w_ukv=v_w_ukv,
                w_br_a=v_w_br_a, w_br_b=v_w_br_b, w_out=v_w_out, g_ple=v_g_ple, w_ple_gate=v_w_ple_gate,
                w_ple_proj=v_w_ple_proj, g_final=v_g_final)
    wide = lambda d: d["w_in"]
    rows = lambda d: _pack_rows(d)[None]

    got_wide, got_rows = _all_gather([wide(weights).astype(BF16), rows(weights).astype(BF16)])
    blocks = _unpack_rows(got_rows[:, 0])
    full = {n: _join(n, blocks[n]) for n, _ in ROWS_PIECES}
    full["w_in"] = _join("w_in", got_wide)
    wts = _kernel_weights(full)
    small = _small_params(g_mix, sink, g_q, g_kv, g_ple, g_final)

    loss, grad_x, grads, dg_final = _local_step(x, p, positions, loss_target, small, wts)

    stacked = {n: jnp.stack([grads[i][n] for i in range(DEPTH)]) for n in grads[0]}
    stacked["g_final"] = dg_final
    pay_wide = _split("w_in", stacked["w_in"]).reshape(N_DEV // 2, 2, DEPTH, D, WIDE_W)
    pay_rows = _pack_rows({n: _split(n, stacked[n]) for n, _ in ROWS_PIECES})
    pay_rows = pay_rows.reshape(N_DEV // 2, 2, 1, ROWS_N, LANES)
    pay_vec = jnp.broadcast_to(_pack_vec(stacked, loss[0, 0]), (N_DEV // 2, 2, 1, VEC_ROWS, LANES))
    core = lax.axis_index("c").astype(jnp.int32).reshape(1)
    from_sibling = _swap_sibling([pay_wide, pay_rows, pay_vec])
    chip_partial = [_add_mine(pay_wide, from_sibling[0], core, WIDE_TILE, BF16),
                    _add_mine(pay_rows, from_sibling[1], core, ROWS_TILE, BF16),
                    _add_mine(pay_vec, from_sibling[2], core, VEC_ROWS, F32)]
    parts_wide, parts_rows, parts_vec = _exchange_chips(chip_partial)
    out_wide = _sum_adamw(parts_wide, wide(weights), wide(mom1), wide(mom2), WIDE_TILE)
    out_rows = _sum_adamw(parts_rows, rows(weights), rows(mom1), rows(mom2), ROWS_TILE)
    out_vec = _sum_adamw(parts_vec, _pack_vec(weights), _pack_vec(mom1), _pack_vec(mom2), VEC_ROWS)

    outs = []
    for ow, orow, ovec in zip(out_wide, out_rows, out_vec):
        named = _unpack_rows(orow[0])
        named.update(_unpack_vec(ovec)[0])
        named["w_in"] = ow
        outs += [named[n] for n in weights]
    loss = _unpack_vec(out_vec[0])[1]
    return (loss, grad_x, *outs)
```

```python
import functools

import jax
import jax.numpy as jnp
from jax import lax
from jax.experimental import pallas as pl
from jax.experimental.pallas import tpu as pltpu

F32, BF16 = jnp.float32, jnp.bfloat16
SDS = jax.ShapeDtypeStruct

D = 1024
DEPTH = 2
PLE = 256
BLK = 128
EPS = 1e-6
NEG = -1e30
SWA_H, SWA_KV, SWA_DH = 8, 2, 64
MLA_H, MLA_NOPE, MLA_ROPE, MLA_V = 8, 64, 32, 64
MLA_QK = MLA_NOPE + MLA_ROPE
QL, KVL = 256, 128
IN_W = 4256
N_DEV = 8

V7X_VMEM_BYTES = 64 * 1024 * 1024
LANES = 128
VMEM_LIMIT = V7X_VMEM_BYTES * 7 // 8

ZW = 4352
Z_MA, Z_MB, Z_AQ, Z_AG, Z_BG, Z_QD, Z_AK, Z_AV, Z_KVD, Z_KR = 0, 1024, 2048, 2560, 3072, 3584, 3840, 3968, 4096, 4224
QFW = MLA_H * LANES
KVW = QFW + MLA_H * MLA_V
MLA_SCALE = MLA_QK ** -0.5
SWA_SCALE = SWA_DH ** -0.5
ROLL_UP, ROLL_DOWN = MLA_ROPE // 2, LANES - MLA_ROPE // 2

ADAM_LR, ADAM_B1, ADAM_B2, ADAM_EPS, ADAM_WD, ADAM_STEP = 0.001, 0.9, 0.999, 1e-08, 0.01, 10

FLAT_W = 1024


def _cp(*sem):
    return pltpu.CompilerParams(dimension_semantics=sem, vmem_limit_bytes=VMEM_LIMIT)


def _row(tm, w, col=0):
    return pl.BlockSpec((tm, w), lambda i: (i, col))


def _res(shape, layer=None):
    if layer is None:
        return pl.BlockSpec(shape, lambda *_: (0,) * len(shape), pipeline_mode=pl.Buffered(1))
    return pl.BlockSpec((None,) + shape, lambda *_: (layer,) + (0,) * len(shape), pipeline_mode=pl.Buffered(1))


def _acc(shape):
    return pl.BlockSpec(shape, lambda *_: (0,) * len(shape))


def _rstd(xf):
    return lax.rsqrt(jnp.mean(xf * xf, axis=-1, keepdims=True) + EPS)


def _norm_bwd(dh, n, r, g):
    dn = dh * g
    return r * (dn - n * jnp.mean(dn * n, axis=-1, keepdims=True)), dh * n


def _nt(a, b):
    return lax.dot_general(a, b, (((1,), (1,)), ((), ())), preferred_element_type=F32)


def _tn(a, b):
    return lax.dot_general(a, b, (((0,), (0,)), ((), ())), preferred_element_type=F32)


def _nn(a, b):
    return jnp.dot(a, b, preferred_element_type=F32)


def _sig(x):
    return jax.nn.sigmoid(x)


def _rope(t, c, s1, s2):
    return t * c + pltpu.roll(t, ROLL_UP, 1) * s1 + pltpu.roll(t, ROLL_DOWN, 1) * s2


def _rope_t(d, c, s1, s2):
    return d * c + pltpu.roll(d * s1, ROLL_DOWN, 1) + pltpu.roll(d * s2, ROLL_UP, 1)


def _fwd_in(x, g, w, tm, layer):
    T = x.shape[0]

    def body(x_ref, g_ref, w_ref, z_ref, h_ref):
        xf = x_ref[...]
        h = ((xf * _rstd(xf)) * g_ref[...]).astype(BF16)
        h_ref[...] = h
        z_ref[...] = _nn(h, w_ref[...])

    return pl.pallas_call(
        body, grid=(T // tm,), name="fwd_in",
        in_specs=[_row(tm, D), _res((1, D), layer), _res((D, ZW), layer)],
        out_specs=[_row(tm, ZW), _row(tm, D)],
        out_shape=[SDS((T, ZW), F32), SDS((T, D), BF16)],
        compiler_params=_cp("parallel"))(x, g, w)


def _fwd_prep(z, gq, gkv, wq, wkv, tc, ts1, ts2, tm, layer):
    T = z.shape[0]

    def body(qd_ref, kvd_ref, kr_ref, gq_ref, gkv_ref, wq_ref, wkv_ref, c_ref, s1_ref, s2_ref, q_ref, k_ref, v_ref):
        qd, kvd = qd_ref[...], kvd_ref[...]
        hq = ((qd * _rstd(qd)) * gq_ref[...]).astype(BF16)
        hkv = ((kvd * _rstd(kvd)) * gkv_ref[...]).astype(BF16)
        qf = _nn(hq, wq_ref[...])
        kvf = _nn(hkv, wkv_ref[...])
        c, s1, s2 = c_ref[...], s1_ref[...], s2_ref[...]
        krb = _rope(kr_ref[...], c, s1, s2)
        for h in range(MLA_H):
            sl = slice(LANES * h, LANES * (h + 1))
            q_ref[:, sl] = _rope(qf[:, sl], c, s1, s2).astype(BF16)
            k_ref[:, sl] = (kvf[:, sl] + krb).astype(BF16)
        v_ref[...] = kvf[:, QFW:].astype(BF16)

    return pl.pallas_call(
        body, grid=(T // tm,), name="fwd_prep",
        in_specs=[_row(tm, QL, Z_QD // QL), _row(tm, KVL, Z_KVD // KVL), _row(tm, LANES, Z_KR // LANES),
                  _res((1, QL), layer), _res((1, KVL), layer), _res((QL, QFW), layer), _res((KVL, KVW), layer),
                  _row(tm, LANES), _row(tm, LANES), _row(tm, LANES)],
        out_specs=[_row(tm, QFW), _row(tm, QFW), _row(tm, MLA_H * MLA_V)],
        out_shape=[SDS((T, QFW), BF16), SDS((T, QFW), BF16), SDS((T, MLA_H * MLA_V), BF16)],
        compiler_params=_cp("parallel"))(z, z, z, gq, gkv, wq, wkv, tc, ts1, ts2)


def _mla_fwd(qf, kf, v, nb, seq, tq):
    T = qf.shape[0]
    nq = seq // tq
    pw = 2 * LANES
    pairs = [(qi, ki) for qi in range(nq) for ki in range(qi + 1)]
    qi_tab = jnp.array([qk[0] for qk in pairs], jnp.int32)
    ki_tab = jnp.array([qk[1] for qk in pairs], jnp.int32)

    def body(qi_ref, ki_ref, q_ref, k_ref, v_ref, o_ref, lse_ref, m_s, l_s, acc_s):
        qi, ki = qi_ref[pl.program_id(2)], ki_ref[pl.program_id(2)]

        @pl.when(ki == 0)
        def _():
            m_s[...] = jnp.full(m_s.shape, NEG, F32)
            l_s[...] = jnp.zeros(l_s.shape, F32)
            acc_s[...] = jnp.zeros(acc_s.shape, F32)

        def step(masked):
            if masked:
                keys = lax.broadcasted_iota(jnp.int32, (tq, tq), 0)
                queries = lax.broadcasted_iota(jnp.int32, (tq, tq), 1)
                mask = keys <= queries
            ss = []
            for j in range(2):
                wide = slice(LANES * j, LANES * (j + 1))
                s = _nt(k_ref[:, wide], q_ref[:, wide]) * MLA_SCALE
                ss.append(jnp.where(mask, s, NEG) if masked else s)
            ps, alphas = [], []
            for j in range(2):
                m_prev = m_s[j]
                m_new = jnp.maximum(m_prev, jnp.max(ss[j], axis=0, keepdims=True))
                alpha = jnp.exp(m_prev - m_new)
                p = jnp.exp(ss[j] - m_new)
                l_s[j] = alpha * l_s[j] + jnp.sum(p, axis=0, keepdims=True)
                m_s[j] = m_new
                ps.append(p.astype(BF16))
                alphas.append(alpha)
            for j in range(2):
                rows = slice(MLA_V * j, MLA_V * (j + 1))
                acc_s[rows, :] = alphas[j] * acc_s[rows, :] + _tn(v_ref[:, rows], ps[j])

        @pl.when(ki < qi)
        def _():
            step(False)

        @pl.when(ki == qi)
        def _():
            step(True)
            for j in range(2):
                rows = slice(MLA_V * j, MLA_V * (j + 1))
                acc_s[rows, :] = acc_s[rows, :] / l_s[j]
                lse_ref[j:j + 1, :] = m_s[j] + jnp.log(l_s[j])
            o_ref[...] = acc_s[...].T

    q_map = lambda b, hp, s, qi_ref, ki_ref: (b * nq + qi_ref[s], hp)
    kv_map = lambda b, hp, s, qi_ref, ki_ref: (b * nq + ki_ref[s], hp)
    return pl.pallas_call(
        body, name="mla_fwd",
        grid_spec=pltpu.PrefetchScalarGridSpec(
            num_scalar_prefetch=2, grid=(nb, MLA_H // 2, len(pairs)),
            in_specs=[pl.BlockSpec((tq, pw), q_map), pl.BlockSpec((tq, pw), kv_map),
                      pl.BlockSpec((tq, LANES), kv_map)],
            out_specs=[pl.BlockSpec((tq, LANES), q_map),
                       pl.BlockSpec((None, 2, tq), lambda b, hp, s, qi_ref, ki_ref: (hp, 0, b * nq + qi_ref[s]))],
            scratch_shapes=[pltpu.VMEM((2, 1, tq), F32), pltpu.VMEM((2, 1, tq), F32),
                            pltpu.VMEM((LANES, tq), F32)]),
        out_shape=[SDS((T, MLA_H * MLA_V), F32), SDS((MLA_H // 2, 2, T), F32)],
        compiler_params=_cp("parallel", "parallel", "arbitrary"))(qi_tab, ki_tab, qf, kf, v)


def _swa_specs(nblk):
    cur = lambda b, n: (b * nblk + n, 0)
    prev = lambda b, n: (b * nblk + jnp.maximum(n - 1, 0), 0)
    kvc = Z_AK // (2 * BLK)
    return [pl.BlockSpec(memory_space=pltpu.SMEM),
            pl.BlockSpec((BLK, 512), lambda b, n: (b * nblk + n, Z_AQ // 512)),
            pl.BlockSpec((BLK, 2 * BLK), lambda b, n: (b * nblk + n, kvc)),
            pl.BlockSpec((BLK, 2 * BLK), lambda b, n: (b * nblk + jnp.maximum(n - 1, 0), kvc)),
            pl.BlockSpec((BLK, 1), cur),
            pl.BlockSpec((BLK, 1), prev),
            pl.BlockSpec((1, 1, BLK), lambda b, n: (b * nblk + n, 0, 0))]


def _swa_scores(n, q_ref, kvc_ref, kvp_ref, pcc_ref, pcp_ref, pr_ref):
    kv = jnp.concatenate([kvp_ref[...], kvc_ref[...]], axis=0)
    kb, vb = kv[:, :BLK].astype(BF16), kv[:, BLK:].astype(BF16)
    dist = pr_ref[0] - jnp.concatenate([pcp_ref[...], pcc_ref[...]], axis=0)
    key = lax.broadcasted_iota(jnp.int32, (2 * BLK, BLK), 0)
    qry = lax.broadcasted_iota(jnp.int32, (2 * BLK, BLK), 1)
    valid = (key > qry) & (key <= qry + BLK) & ((key >= BLK) | (n > 0))

    def scores(h):
        g = h // (SWA_H // SWA_KV)
        qh = q_ref[:, SWA_DH * h:SWA_DH * (h + 1)].astype(BF16)
        s = _nt(kb[:, SWA_DH * g:SWA_DH * (g + 1)], qh) * SWA_SCALE - (2.0 ** -(h + 1)) * dist
        return qh, jnp.where(valid, s, NEG)

    return kb, vb, scores


def _swa_fwd(sink, z, pos_col, pos_row, nb, seq, layer):
    T = z.shape[0]
    nblk = seq // BLK

    def body(sink_ref, q_ref, kvc_ref, kvp_ref, pcc_ref, pcp_ref, pr_ref, o_ref, lse_ref):
        kb, vb, scores = _swa_scores(pl.program_id(1), q_ref, kvc_ref, kvp_ref, pcc_ref, pcp_ref, pr_ref)
        ss = [scores(h)[1] for h in range(SWA_H)]
        es, dens = [], []
        for h in range(SWA_H):
            sk = sink_ref[layer, h]
            m = jnp.maximum(jnp.max(ss[h], axis=0, keepdims=True), sk)
            e = jnp.exp(ss[h] - m)
            den = jnp.sum(e, axis=0, keepdims=True) + jnp.exp(sk - m)
            lse_ref[h:h + 1, :] = m + jnp.log(den)
            es.append(e.astype(BF16))
            dens.append(den)
        outs = []
        for h in range(SWA_H):
            g = h // (SWA_H // SWA_KV)
            outs.append(_tn(vb[:, SWA_DH * g:SWA_DH * (g + 1)], es[h]) / dens[h])
        o_ref[...] = jnp.concatenate(outs, axis=0).T

    return pl.pallas_call(
        body, grid=(nb, nblk), name="swa_fwd",
        in_specs=_swa_specs(nblk),
        out_specs=[pl.BlockSpec((BLK, 512), lambda b, n: (b * nblk + n, 0)),
                   pl.BlockSpec((SWA_H, BLK), lambda b, n: (0, b * nblk + n))],
        out_shape=[SDS((T, 512), F32), SDS((SWA_H, T), F32)],
        compiler_params=_cp("parallel", "parallel"))(sink, z, z, z, pos_col, pos_col, pos_row)


def _fwd_merge(x, oa, ob, z, wa, wb, wo, tm, layer):
    T = x.shape[0]

    def body(x_ref, oa_ref, ob_ref, ag_ref, bg_ref, ma_ref, mb_ref, wa_ref, wb_ref, wo_ref, x1_ref):
        ag, bg = ag_ref[...], bg_ref[...]
        ua = _nn((oa_ref[...] * (ag * _sig(ag))).astype(BF16), wa_ref[...])
        ub = _nn((ob_ref[...] * (bg * _sig(bg))).astype(BF16), wb_ref[...])
        y = _sig(ma_ref[...]) * ua + _sig(mb_ref[...]) * ub
        x1_ref[...] = x_ref[...] + _nn(y.astype(BF16), wo_ref[...])

    return pl.pallas_call(
        body, grid=(T // tm,), name="fwd_merge",
        in_specs=[_row(tm, D), _row(tm, 512), _row(tm, 512), _row(tm, 512, Z_AG // 512), _row(tm, 512, Z_BG // 512),
                  _row(tm, D, Z_MA // D), _row(tm, D, Z_MB // D),
                  _res((512, D), layer), _res((512, D), layer), _res((D, D), layer)],
        out_specs=_row(tm, D),
        out_shape=SDS((T, D), F32),
        compiler_params=_cp("parallel"))(x, oa, ob, z, z, z, z, wa, wb, wo)


def _fwd_ple(x1, p, g, wpg, wpp, tm, layer):
    T = x1.shape[0]

    def body(x_ref, p_ref, g_ref, wpg_ref, wpp_ref, x2_ref, pg_ref, pp_ref):
        xf = x_ref[...]
        h1 = ((xf * _rstd(xf)) * g_ref[...]).astype(BF16)
        pg = _sig(_nn(h1, wpg_ref[...]))
        pp = _nn(p_ref[...].astype(BF16), wpp_ref[...])
        pg_ref[...] = pg
        pp_ref[...] = pp
        x2_ref[...] = xf + pg * pp

    return pl.pallas_call(
        body, grid=(T // tm,), name="fwd_ple",
        in_specs=[_row(tm, D), pl.BlockSpec((None, tm, PLE), lambda i: (layer, i, 0)),
                  _res((1, D), layer), _res((D, D), layer), _res((PLE, D), layer)],
        out_specs=[_row(tm, D)] * 3,
        out_shape=[SDS((T, D), F32)] * 3,
        compiler_params=_cp("parallel"))(x1, p, g, wpg, wpp)


def _loss_head(x, g, tgt, tm):
    T = x.shape[0]

    def body(x_ref, g_ref, t_ref, dx_ref, dg_ref, loss_ref):
        @pl.when(pl.program_id(0) == 0)
        def _():
            dg_ref[...] = jnp.zeros(dg_ref.shape, F32)
            loss_ref[...] = jnp.zeros(loss_ref.shape, F32)

        xf, gf = x_ref[...], g_ref[...]
        r = _rstd(xf)
        n = xf * r
        err = n * gf - t_ref[...]
        loss_ref[...] += 0.5 * jnp.sum(jnp.mean(err * err, axis=-1, keepdims=True), axis=0, keepdims=True)
        dx, dgr = _norm_bwd(err * (1.0 / D), n, r, gf)
        dx_ref[...] = dx
        dg_ref[...] += jnp.sum(dgr, axis=0, keepdims=True)

    return pl.pallas_call(
        body, grid=(T // tm,), name="loss_head",
        in_specs=[_row(tm, D), _res((1, D)), _row(tm, D)],
        out_specs=[_row(tm, D), _acc((1, D)), _acc((1, LANES))],
        out_shape=[SDS((T, D), F32), SDS((1, D), F32), SDS((1, LANES), F32)],
        compiler_params=_cp("arbitrary"))(x, g, tgt)


def _bwd_ple(dx2, x1, pg, pp, p, g, wpg, tm, layer):
    T = x1.shape[0]

    def body(d_ref, x_ref, pg_ref, pp_ref, p_ref, g_ref, w_ref, dx_ref, dwg_ref, dwp_ref, dg_ref):
        @pl.when(pl.program_id(0) == 0)
        def _():
            dwg_ref[...] = jnp.zeros(dwg_ref.shape, F32)
            dwp_ref[...] = jnp.zeros(dwp_ref.shape, F32)
            dg_ref[...] = jnp.zeros(dg_ref.shape, F32)

        d, xf, pg, gf = d_ref[...], x_ref[...], pg_ref[...], g_ref[...]
        r = _rstd(xf)
        n = xf * r
        dpgl = (d * pp_ref[...] * pg * (1.0 - pg)).astype(BF16)
        dwg_ref[...] += _tn((n * gf).astype(BF16), dpgl)
        dwp_ref[...] += _tn(p_ref[...].astype(BF16), (d * pg).astype(BF16))
        dxn, dgr = _norm_bwd(_nt(dpgl, w_ref[...]), n, r, gf)
        dx_ref[...] = d + dxn
        dg_ref[...] += jnp.sum(dgr, axis=0, keepdims=True)

    return pl.pallas_call(
        body, grid=(T // tm,), name="bwd_ple",
        in_specs=[_row(tm, D)] * 4 + [pl.BlockSpec((None, tm, PLE), lambda i: (layer, i, 0)),
                                      _res((1, D), layer), _res((D, D), layer)],
        out_specs=[_row(tm, D), _acc((D, D)), _acc((PLE, D)), _acc((1, D))],
        out_shape=[SDS((T, D), F32), SDS((D, D), F32), SDS((PLE, D), F32), SDS((1, D), F32)],
        compiler_params=_cp("arbitrary"))(dx2, x1, pg, pp, p, g, wpg)


def _bwd_merge(dx1, oa, ob, z, wa, wb, wo, tm, layer):
    T = dx1.shape[0]

    def body(d_ref, oa_ref, ob_ref, ag_ref, bg_ref, ma_ref, mb_ref, wa_ref, wb_ref, wo_ref,
             doa_ref, dob_ref, dag_ref, dbg_ref, dma_ref, dmb_ref, dsa_ref, dsb_ref, dwa_ref, dwb_ref, dwo_ref):
        @pl.when(pl.program_id(0) == 0)
        def _():
            dwa_ref[...] = jnp.zeros(dwa_ref.shape, F32)
            dwb_ref[...] = jnp.zeros(dwb_ref.shape, F32)
            dwo_ref[...] = jnp.zeros(dwo_ref.shape, F32)

        db = d_ref[...].astype(BF16)
        gated = []
        for o_ref, gate_ref, w_ref in ((oa_ref, ag_ref, wa_ref), (ob_ref, bg_ref, wb_ref)):
            raw, gate = o_ref[...], gate_ref[...]
            sg = _sig(gate)
            silu = gate * sg
            ob16 = (raw * silu).astype(BF16)
            gated.append((raw, gate, sg, silu, ob16, _nn(ob16, w_ref[...])))
        ua, ub = gated[0][5], gated[1][5]
        sa, sb = _sig(ma_ref[...]), _sig(mb_ref[...])
        dwo_ref[...] += _tn((sa * ua + sb * ub).astype(BF16), db)
        dy = _nt(db, wo_ref[...])
        dma_ref[...] = (dy * ua * sa * (1.0 - sa)).astype(BF16)
        dmb_ref[...] = (dy * ub * sb * (1.0 - sb)).astype(BF16)
        for (s, w_ref, do_ref, dgate_ref, dw_ref, ds_ref), (raw, gate, sg, silu, ob16, _) in zip((
                (sa, wa_ref, doa_ref, dag_ref, dwa_ref, dsa_ref),
                (sb, wb_ref, dob_ref, dbg_ref, dwb_ref, dsb_ref)), gated):
            du = (dy * s).astype(BF16)
            dw_ref[...] += _tn(ob16, du)
            do = _nt(du, w_ref[...])
            draw = do * silu
            do_ref[...] = draw.astype(BF16)
            dgate_ref[...] = (do * raw * (sg * (1.0 + gate * (1.0 - sg)))).astype(BF16)
            ds_ref[...] = jnp.sum((draw * raw).T.reshape(MLA_H, MLA_V, tm), axis=1)

    return pl.pallas_call(
        body, grid=(T // tm,), name="bwd_merge",
        in_specs=[_row(tm, D), _row(tm, 512), _row(tm, 512), _row(tm, 512, Z_AG // 512), _row(tm, 512, Z_BG // 512),
                  _row(tm, D, Z_MA // D), _row(tm, D, Z_MB // D),
                  _res((512, D), layer), _res((512, D), layer), _res((D, D), layer)],
        out_specs=[_row(tm, 512)] * 4 + [_row(tm, D)] * 2 + [pl.BlockSpec((MLA_H, tm), lambda i: (0, i))] * 2
        + [_acc((512, D)), _acc((512, D)), _acc((D, D))],
        out_shape=[SDS((T, 512), BF16), SDS((T, 512), BF16), SDS((T, 512), BF16), SDS((T, 512), BF16),
                   SDS((T, D), BF16), SDS((T, D), BF16), SDS((MLA_H, T), F32), SDS((MLA_H, T), F32),
                   SDS((512, D), F32), SDS((512, D), F32), SDS((D, D), F32)],
        compiler_params=_cp("arbitrary"))(dx1, oa, ob, z, z, z, z, wa, wb, wo)


def _mla_bwd(qf, kf, v, do, lse, dsum, nb, seq, tq):
    T = qf.shape[0]
    nq = seq // tq
    pw = 2 * LANES
    pairs = [(qi, ki) for ki in range(nq) for qi in range(ki, nq)]
    qi_tab = jnp.array([qk[0] for qk in pairs], jnp.int32)
    ki_tab = jnp.array([qk[1] for qk in pairs], jnp.int32)

    def body(qi_ref, ki_ref, q_ref, k_ref, v_ref, do_ref, lse_ref, dsum_ref, dq_ref, dk_ref, dv_ref, dk_s, dv_s, dqt_s):
        step_id = pl.program_id(2)
        qi, ki = qi_ref[step_id], ki_ref[step_id]

        @pl.when(step_id == 0)
        def _():
            dqt_s[...] = jnp.zeros(dqt_s.shape, F32)

        @pl.when(qi == ki)
        def _():
            dk_s[...] = jnp.zeros(dk_s.shape, F32)
            dv_s[...] = jnp.zeros(dv_s.shape, F32)

        def step(masked):
            if masked:
                keys = lax.broadcasted_iota(jnp.int32, (tq, tq), 0)
                queries = lax.broadcasted_iota(jnp.int32, (tq, tq), 1)
                mask = keys <= queries
            for j in range(2):
                wide = slice(LANES * j, LANES * (j + 1))
                sl = slice(MLA_V * j, MLA_V * (j + 1))
                q, k = q_ref[:, wide], k_ref[:, wide]
                dob = do_ref[:, sl].astype(BF16)
                s = _nt(k, q) * MLA_SCALE
                if masked:
                    s = jnp.where(mask, s, NEG)
                p = jnp.exp(s - lse_ref[j:j + 1, :])
                dv_s[:, sl] += _nn(p.astype(BF16), dob)
                ds = (p * (_nt(v_ref[:, sl], dob) - dsum_ref[j:j + 1, :]) * MLA_SCALE).astype(BF16)
                dk_s[:, wide] += _nn(ds, q)
                dqt_s[qi, wide, :] += _tn(k, ds)

        @pl.when(qi > ki)
        def _():
            step(False)

        @pl.when(qi == ki)
        def _():
            step(True)

        @pl.when(qi == nq - 1)
        def _():
            dk_ref[...] = dk_s[...]
            dv_ref[...] = dv_s[...]

        @pl.when(step_id == len(pairs) - 1)
        def _():
            for n in range(nq):
                dq_ref[tq * n:tq * (n + 1), :] = dqt_s[n].T

    qmap = lambda b, hp, s, qi_ref, ki_ref: (b * nq + qi_ref[s], hp)
    kmap = lambda b, hp, s, qi_ref, ki_ref: (b * nq + ki_ref[s], hp)
    stat = pl.BlockSpec((None, 2, tq), lambda b, hp, s, qi_ref, ki_ref: (hp, 0, b * nq + qi_ref[s]))
    return pl.pallas_call(
        body, name="mla_bwd",
        grid_spec=pltpu.PrefetchScalarGridSpec(
            num_scalar_prefetch=2, grid=(nb, MLA_H // 2, len(pairs)),
            in_specs=[pl.BlockSpec((tq, pw), qmap), pl.BlockSpec((tq, pw), kmap), pl.BlockSpec((tq, LANES), kmap),
                      pl.BlockSpec((tq, LANES), qmap), stat, stat],
            out_specs=[pl.BlockSpec((seq, pw), lambda b, hp, s, qi_ref, ki_ref: (b, hp)),
                       pl.BlockSpec((tq, pw), kmap), pl.BlockSpec((tq, LANES), kmap)],
            scratch_shapes=[pltpu.VMEM((tq, pw), F32), pltpu.VMEM((tq, LANES), F32),
                            pltpu.VMEM((nq, pw, tq), F32)]),
        out_shape=[SDS((T, QFW), F32), SDS((T, QFW), F32), SDS((T, MLA_H * MLA_V), F32)],
        compiler_params=_cp("parallel", "parallel", "arbitrary"))(qi_tab, ki_tab, qf, kf, v, do, lse, dsum)


def _swa_bwd(sink, z, pos_col, pos_row, do, lse, dsum, nb, seq, layer):
    T = z.shape[0]
    nblk = seq // BLK

    def body(sink_ref, q_ref, kvc_ref, kvp_ref, pcc_ref, pcp_ref, pr_ref, do_ref, lse_ref, dsum_ref,
             dq_ref, dkv_ref, dsink_ref):
        b, n = pl.program_id(0), pl.program_id(1)

        @pl.when((b == 0) & (n == 0))
        def _():
            dsink_ref[...] = jnp.zeros(dsink_ref.shape, F32)

        @pl.when(n == 0)
        def _():
            dkv_ref[...] = jnp.zeros(dkv_ref.shape, F32)

        kb, vb, scores = _swa_scores(n, q_ref, kvc_ref, kvp_ref, pcc_ref, pcp_ref, pr_ref)
        lane = lax.broadcasted_iota(jnp.int32, (1, LANES), 1)
        dsink = jnp.zeros((1, LANES), F32)
        dkv = [[None, None], [None, None]]
        dqs = []
        gsl = lambda h: slice(SWA_DH * (h // (SWA_H // SWA_KV)), SWA_DH * (h // (SWA_H // SWA_KV) + 1))
        qs, ss, dobs, dps = [], [], [], []
        for h in range(SWA_H):
            qh, s = scores(h)
            dob = do_ref[:, SWA_DH * h:SWA_DH * (h + 1)].astype(BF16)
            qs.append(qh)
            ss.append(s)
            dobs.append(dob)
            dps.append(_nt(vb[:, gsl(h)], dob))
        pbs, dss = [], []
        for h in range(SWA_H):
            lse, dsum = lse_ref[h:h + 1, :], dsum_ref[h:h + 1, :]
            p = jnp.exp(ss[h] - lse)
            pbs.append(p.astype(BF16))
            dss.append((p * (dps[h] - dsum) * SWA_SCALE).astype(BF16))
            dsk = jnp.sum(-jnp.exp(sink_ref[layer, h] - lse) * dsum, axis=1, keepdims=True)
            dsink = dsink + jnp.where(lane == h, dsk, 0.0)
        for h in range(SWA_H):
            g = h // (SWA_H // SWA_KV)
            dqs.append(_tn(kb[:, gsl(h)], dss[h]))
            dk, dv = _nn(dss[h], qs[h]), _nn(pbs[h], dobs[h])
            dkv[g][0] = dk if dkv[g][0] is None else dkv[g][0] + dk
            dkv[g][1] = dv if dkv[g][1] is None else dkv[g][1] + dv
        dq_ref[...] = jnp.concatenate(dqs, axis=0).T.astype(BF16)
        dsink_ref[...] += dsink
        upd = jnp.concatenate([dkv[0][0], dkv[1][0], dkv[0][1], dkv[1][1]], axis=1)
        dkv_ref[pl.ds(pl.multiple_of(n * BLK, BLK), BLK), :] += upd[BLK:]

        @pl.when(n > 0)
        def _():
            dkv_ref[pl.ds(pl.multiple_of((n - 1) * BLK, BLK), BLK), :] += upd[:BLK]

    return pl.pallas_call(
        body, grid=(nb, nblk), name="swa_bwd",
        in_specs=_swa_specs(nblk) + [pl.BlockSpec((BLK, 512), lambda b, n: (b * nblk + n, 0))]
        + [pl.BlockSpec((SWA_H, BLK), lambda b, n: (0, b * nblk + n))] * 2,
        out_specs=[pl.BlockSpec((BLK, 512), lambda b, n: (b * nblk + n, 0)),
                   pl.BlockSpec((seq, 2 * BLK), lambda b, n: (b, 0)),
                   pl.BlockSpec((1, LANES), lambda b, n: (0, 0))],
        out_shape=[SDS((T, 512), BF16), SDS((T, 2 * BLK), F32), SDS((1, LANES), F32)],
        compiler_params=_cp("arbitrary", "arbitrary"))(sink, z, z, z, pos_col, pos_col, pos_row, do, lse, dsum)


def _bwd_prep(dq, dk, dv, z, gq, gkv, wq, wkv, tc, ts1, ts2, tm, layer):
    T = z.shape[0]

    def body(dq_ref, dk_ref, dv_ref, qd_ref, kvd_ref, gq_ref, gkv_ref, wq_ref, wkv_ref, c_ref, s1_ref, s2_ref,
             dqd_ref, dkvd_ref, dkr_ref, dwq_ref, dwkv_ref, dgq_ref, dgkv_ref, dqb_s, dkvb_s):
        @pl.when(pl.program_id(0) == 0)
        def _():
            for ref in (dwq_ref, dwkv_ref, dgq_ref, dgkv_ref):
                ref[...] = jnp.zeros(ref.shape, F32)

        c, s1, s2 = c_ref[...], s1_ref[...], s2_ref[...]
        lane = lax.broadcasted_iota(jnp.int32, (1, LANES), 1)
        rope_lanes = (lane >= MLA_NOPE) & (lane < MLA_QK)
        dkb = jnp.zeros((tm, LANES), F32)
        for h in range(MLA_H):
            sl = slice(LANES * h, LANES * (h + 1))
            dqb_s[:, sl] = _rope_t(dq_ref[:, sl], c, s1, s2).astype(BF16)
            dkh = dk_ref[:, sl]
            dkb = dkb + dkh
            dkvb_s[:, sl] = dkh.astype(BF16)
        dkvb_s[:, QFW:] = dv_ref[...].astype(BF16)
        dkr_ref[...] = _rope_t(jnp.where(rope_lanes, dkb, 0.0), c, s1, s2).astype(BF16)

        for (x_ref, g_ref, w_ref, d_s, dx_ref, dw_ref, dg_ref) in (
                (qd_ref, gq_ref, wq_ref, dqb_s, dqd_ref, dwq_ref, dgq_ref),
                (kvd_ref, gkv_ref, wkv_ref, dkvb_s, dkvd_ref, dwkv_ref, dgkv_ref)):
            xf, gf, db = x_ref[...], g_ref[...], d_s[...]
            r = _rstd(xf)
            n = xf * r
            dw_ref[...] += _tn((n * gf).astype(BF16), db)
            dx, dgr = _norm_bwd(_nt(db, w_ref[...]), n, r, gf)
            dx_ref[...] = dx.astype(BF16)
            dg_ref[...] += jnp.sum(dgr, axis=0, keepdims=True)

    return pl.pallas_call(
        body, grid=(T // tm,), name="bwd_prep",
        in_specs=[_row(tm, QFW), _row(tm, QFW), _row(tm, MLA_H * MLA_V),
                  _row(tm, QL, Z_QD // QL), _row(tm, KVL, Z_KVD // KVL),
                  _res((1, QL), layer), _res((1, KVL), layer), _res((QL, QFW), layer), _res((KVL, KVW), layer),
                  _row(tm, LANES), _row(tm, LANES), _row(tm, LANES)],
        out_specs=[_row(tm, QL), _row(tm, KVL), _row(tm, LANES),
                   _acc((QL, QFW)), _acc((KVL, KVW)), _acc((1, QL)), _acc((1, KVL))],
        out_shape=[SDS((T, QL), BF16), SDS((T, KVL), BF16), SDS((T, LANES), BF16),
                   SDS((QL, QFW), F32), SDS((KVL, KVW), F32), SDS((1, QL), F32), SDS((1, KVL), F32)],
        scratch_shapes=[pltpu.VMEM((tm, QFW), BF16), pltpu.VMEM((tm, KVW), BF16)],
        compiler_params=_cp("arbitrary"))(dq, dk, dv, z, z, gq, gkv, wq, wkv, tc, ts1, ts2)


def _bwd_in(pieces, x, g, dres, w, tm, layer):
    T = x.shape[0]
    widths = [pc.shape[1] for pc in pieces]
    assert sum(widths) == ZW
    n_p = len(pieces)

    def body(*refs):
        p_refs, (x_ref, g_ref, r_ref, w_ref, dx_ref, dz_ref, dg_ref) = refs[:n_p], refs[n_p:]

        @pl.when(pl.program_id(0) == 0)
        def _():
            dg_ref[...] = jnp.zeros(dg_ref.shape, F32)

        off = 0
        for ref, wd in zip(p_refs, widths):
            dz_ref[:, off:off + wd] = ref[...].astype(BF16)
            off += wd
        xf, gf = x_ref[...], g_ref[...]
        r = _rstd(xf)
        n = xf * r
        dx, dgr = _norm_bwd(_nt(dz_ref[...], w_ref[...]), n, r, gf)
        dx_ref[...] = r_ref[...] + dx
        dg_ref[...] += jnp.sum(dgr, axis=0, keepdims=True)

    return pl.pallas_call(
        body, grid=(T // tm,), name="bwd_in",
        in_specs=[_row(tm, wd) for wd in widths] + [_row(tm, D), _res((1, D), layer), _row(tm, D),
                                                    _res((D, ZW), layer)],
        out_specs=[_row(tm, D), _row(tm, ZW), _acc((1, D))],
        out_shape=[SDS((T, D), F32), SDS((T, ZW), BF16), SDS((1, D), F32)],
        compiler_params=_cp("arbitrary"))(*pieces, x, g, dres, w)


def _wgrad_in(hb, dzb, tm):
    T = hb.shape[0]
    half = ZW // 2

    def body(h_ref, dz_ref, dw_ref):
        @pl.when(pl.program_id(1) == 0)
        def _():
            dw_ref[...] = jnp.zeros(dw_ref.shape, F32)

        dw_ref[...] += _tn(h_ref[...], dz_ref[...])

    return pl.pallas_call(
        body, grid=(2, T // tm), name="wgrad_in",
        in_specs=[pl.BlockSpec((tm, D), lambda j, t: (t, 0)), pl.BlockSpec((tm, half), lambda j, t: (t, j))],
        out_specs=pl.BlockSpec((D, half), lambda j, t: (0, j)),
        out_shape=SDS((D, ZW), F32),
        compiler_params=_cp("parallel", "arbitrary"))(hb, dzb)


IN_PIECES = ((0, 512, Z_AQ), (512, 128, Z_AK), (640, 128, Z_AV), (768, 512, Z_AG), (1280, 256, Z_QD),
             (1536, 128, Z_KVD), (1664, MLA_ROPE, Z_KR + MLA_NOPE), (1696, 512, Z_BG), (2208, 1024, Z_MA),
             (3232, 1024, Z_MB))
WIDE_W = IN_W // N_DEV


def _column_runs():
    runs = []
    for start, width, kstart in IN_PIECES:
        col = start
        while col < start + width:
            dev = col // WIDE_W
            stop = min(start + width, (dev + 1) * WIDE_W)
            runs.append((dev, col - dev * WIDE_W, stop - col, kstart + col - start))
            col = stop
    return runs


def _win_layout(blocks, tm):
    runs = _column_runs()

    def body(g_ref, o_ref):
        o_ref[:, Z_KR:Z_KR + LANES] = jnp.zeros((tm, LANES), o_ref.dtype)
        for dev, lo, n, k in runs:
            o_ref[:, k:k + n] = g_ref[dev, :, lo:lo + n]

    return pl.pallas_call(
        body, grid=(DEPTH, D // tm), name="win_layout",
        in_specs=[pl.BlockSpec((N_DEV, None, tm, WIDE_W), lambda l, i: (0, l, i, 0))],
        out_specs=pl.BlockSpec((None, tm, ZW), lambda l, i: (l, i, 0)),
        out_shape=SDS((DEPTH, D, ZW), blocks.dtype),
        compiler_params=_cp("parallel", "parallel"))(blocks)


def _win_grad_layout(dws, tm):
    runs = _column_runs()
    last = D // tm - 1

    def body(*refs):
        o_ref = refs[DEPTH]
        for layer in range(DEPTH):
            @pl.when(pl.program_id(0) == layer)
            def _(g_ref=refs[layer]):
                for dev, lo, n, k in runs:
                    o_ref[dev, :, lo:lo + n] = g_ref[:, k:k + n]

    spec = lambda layer: pl.BlockSpec(
        (tm, ZW), lambda l, i: (jnp.where(l == layer, i, jnp.where(l > layer, last, 0)), 0))
    return pl.pallas_call(
        body, grid=(DEPTH, D // tm), name="win_grad_layout",
        in_specs=[spec(layer) for layer in range(DEPTH)],
        out_specs=pl.BlockSpec((N_DEV, None, tm, WIDE_W), lambda l, i: (0, l, i, 0)),
        out_shape=SDS((N_DEV, DEPTH, D, WIDE_W), F32),
        compiler_params=_cp("arbitrary", "arbitrary"))(*dws)


def _wuq_to_kernel(w):
    w = w.reshape(w.shape[:-1] + (MLA_H, MLA_QK))
    w = jnp.pad(w, [(0, 0)] * (w.ndim - 1) + [(0, LANES - MLA_QK)])
    return w.reshape(w.shape[:-2] + (QFW,))


def _wuq_from_kernel(g):
    g = g.reshape(g.shape[:-1] + (MLA_H, LANES))[..., :MLA_QK]
    return g.reshape(g.shape[:-2] + (MLA_H * MLA_QK,))


def _wukv_to_kernel(w):
    w = w.reshape(w.shape[:-1] + (MLA_H, MLA_NOPE + MLA_V))
    k = jnp.pad(w[..., :MLA_NOPE], [(0, 0)] * (w.ndim - 1) + [(0, LANES - MLA_NOPE)])
    v = w[..., MLA_NOPE:]
    return jnp.concatenate([k.reshape(k.shape[:-2] + (QFW,)), v.reshape(v.shape[:-2] + (MLA_H * MLA_V,))], axis=-1)


def _wukv_from_kernel(g):
    k = g[..., :QFW].reshape(g.shape[:-1] + (MLA_H, LANES))[..., :MLA_NOPE]
    v = g[..., QFW:].reshape(g.shape[:-1] + (MLA_H, MLA_V))
    kv = jnp.concatenate([k, v], axis=-1)
    return kv.reshape(kv.shape[:-2] + (MLA_H * (MLA_NOPE + MLA_V),))


def _rope_tables(pos):
    half = MLA_ROPE // 2
    inv = 10000.0 ** (-jnp.arange(0, MLA_ROPE, 2, dtype=F32) / MLA_ROPE)
    ang = pos.astype(F32)[:, None] * inv
    cos, sin = jnp.cos(ang), jnp.sin(ang)
    one = jnp.ones((pos.shape[0], MLA_NOPE), F32)
    zero = lambda n: jnp.zeros((pos.shape[0], n), F32)
    tc = jnp.concatenate([one, cos, cos, one[:, :LANES - MLA_QK]], axis=1)
    ts1 = jnp.concatenate([zero(MLA_NOPE + half), sin, zero(LANES - MLA_QK)], axis=1)
    ts2 = jnp.concatenate([zero(MLA_NOPE), -sin, zero(LANES - MLA_NOPE - half)], axis=1)
    return tc, ts1, ts2


def _local_step(x, p, positions, loss_target, small, wts):
    nb, seq, _ = x.shape
    T = nb * seq
    tm = min(256, T)
    tq = min(512, seq)
    xf = x.reshape(T, D)
    pos = positions.reshape(T)
    posf = pos.astype(F32)
    pos_col, pos_row = posf.reshape(T, 1), posf.reshape(T // BLK, 1, BLK)
    tc, ts1, ts2 = _rope_tables(pos)

    w, sm = wts, small
    pl_in = p.reshape(DEPTH, T, PLE)
    saved = []
    for i in range(DEPTH):
        z, hb = _fwd_in(xf, sm["g_mix"], w["w_in"], tm, i)
        oa, lse_a = _swa_fwd(sm["sink"], z, pos_col, pos_row, nb, seq, i)
        qf, kf, v = _fwd_prep(z, sm["g_q"], sm["g_kv"], w["w_uq"], w["w_ukv"], tc, ts1, ts2, tm, i)
        ob, lse_b = _mla_fwd(qf, kf, v, nb, seq, tq)
        x1 = _fwd_merge(xf, oa, ob, z, w["w_br_a"], w["w_br_b"], w["w_out"], tm, i)
        x2, pg, pp = _fwd_ple(x1, pl_in, sm["g_ple"], w["w_ple_gate"], w["w_ple_proj"], tm, i)
        saved.append(dict(x=xf, z=z, hb=hb, oa=oa, lse_a=lse_a, qf=qf, kf=kf, v=v, ob=ob, lse_b=lse_b,
                          x1=x1, pg=pg, pp=pp))
        xf = x2

    dx, dg_final, loss = _loss_head(xf, small["g_final"], loss_target.reshape(T, D), tm)

    grads = [None] * DEPTH
    for i in reversed(range(DEPTH)):
        sv = saved[i]
        dx1, dwpg, dwpp, dg_ple = _bwd_ple(dx, sv["x1"], sv["pg"], sv["pp"], pl_in, sm["g_ple"], w["w_ple_gate"],
                                           tm, i)
        doa, dob, dag, dbg, dma, dmb, dsum_a, dsum_b, dwa, dwb, dwo = _bwd_merge(
            dx1, sv["oa"], sv["ob"], sv["z"], w["w_br_a"], w["w_br_b"], w["w_out"], tm, i)
        dq_b, dk_b, dv_b = _mla_bwd(sv["qf"], sv["kf"], sv["v"], dob, sv["lse_b"],
                                    dsum_b.reshape(MLA_H // 2, 2, T), nb, seq, tq)
        dqd, dkvd, dkr, dwq, dwkv, dgq, dgkv = _bwd_prep(dq_b, dk_b, dv_b, sv["z"], sm["g_q"], sm["g_kv"],
                                                         w["w_uq"], w["w_ukv"], tc, ts1, ts2, tm, i)
        dq_a, dkv_a, dsink = _swa_bwd(sm["sink"], sv["z"], pos_col, pos_row, doa, sv["lse_a"], dsum_a, nb, seq, i)
        dx, dzb, dg_mix = _bwd_in([dma, dmb, dq_a, dag, dbg, dqd, dkv_a, dkvd, dkr], sv["x"], sm["g_mix"], dx1,
                                  w["w_in"], tm, i)
        dwin = _wgrad_in(sv["hb"], dzb, tm)
        grads[i] = dict(g_mix=dg_mix[0], w_in=dwin, sink=dsink[0, :SWA_H], g_q=dgq[0],
                        w_uq=_wuq_from_kernel(dwq), g_kv=dgkv[0], w_ukv=_wukv_from_kernel(dwkv), w_br_a=dwa,
                        w_br_b=dwb, w_out=dwo, g_ple=dg_ple[0], w_ple_gate=dwpg, w_ple_proj=dwpp)
    return loss, dx.reshape(nb, seq, D), grads, dg_final[0]


def _kernel_weights(wide_blocks, full):
    out = dict(full)
    out.update(w_in=_win_layout(wide_blocks, 256), w_uq=_wuq_to_kernel(full["w_uq"]),
               w_ukv=_wukv_to_kernel(full["w_ukv"]))
    return out


def _small_params(g_mix, sink, g_q, g_kv, g_ple, g_final):
    return dict(g_mix=g_mix[:, None], sink=sink, g_q=g_q[:, None], g_kv=g_kv[:, None], g_ple=g_ple[:, None],
                g_final=g_final[None])


UQ_W = MLA_H * MLA_QK // N_DEV
ROWS_PIECES = (("w_uq", DEPTH * QL), ("w_ukv", DEPTH * KVL), ("w_br_a", DEPTH * 512), ("w_br_b", DEPTH * 512),
               ("w_out", DEPTH * D), ("w_ple_gate", DEPTH * D), ("w_ple_proj", DEPTH * PLE))
SMALL = (("g_mix", (DEPTH, D)), ("sink", (DEPTH, SWA_H)), ("g_q", (DEPTH, QL)), ("g_kv", (DEPTH, KVL)),
         ("g_ple", (DEPTH, D)), ("g_final", (D,)))
VEC_ROWS = 48
ROWS_N = sum(r for _, r in ROWS_PIECES)
WIDE_TILE, ROWS_TILE = 256, ROWS_N // 4


def _to_rows(name, a):
    lead = a.shape[:-3]
    if name == "w_uq":
        a = jnp.pad(a, [(0, 0)] * (a.ndim - 1) + [(0, LANES - UQ_W)])
    return a.reshape(lead + (-1, LANES))


def _from_rows(name, r):
    lead = r.shape[:-2]
    if name in ("w_out", "w_ple_gate"):
        return r.reshape(lead + (DEPTH, D // N_DEV, D))
    r = r.reshape(lead + (DEPTH, -1, LANES))
    return r[..., :UQ_W] if name == "w_uq" else r


def _pack_rows(blocks):
    return jnp.concatenate([_to_rows(n, blocks[n]) for n, _ in ROWS_PIECES], axis=-2)


def _unpack_rows(rows):
    blocks, off = {}, 0
    for n, r in ROWS_PIECES:
        blocks[n] = _from_rows(n, rows[..., off:off + r, :])
        off += r
    return blocks


def _pack_vec(vectors, loss=None):
    parts = [vectors[n].reshape(-1) for n, _ in SMALL] + ([] if loss is None else [loss.reshape(1)])
    vec = jnp.concatenate(parts)
    return jnp.pad(vec, (0, VEC_ROWS * LANES - vec.shape[0])).reshape(1, VEC_ROWS, LANES)


def _unpack_vec(vec):
    vec = vec.reshape(-1)
    vectors, off = {}, 0
    for n, shp in SMALL:
        size = 1
        for s in shp:
            size *= s
        vectors[n] = vec[off:off + size].reshape(shp)
        off += size
    return vectors, vec[off]


def _join(name, blocks):
    if name in ("w_out", "w_ple_gate"):
        return jnp.moveaxis(blocks, 0, 1).reshape(DEPTH, -1, blocks.shape[-1])
    return jnp.moveaxis(blocks, 0, 2).reshape(DEPTH, blocks.shape[2], -1)


def _split(name, full):
    if name in ("w_out", "w_ple_gate"):
        return jnp.moveaxis(full.reshape(DEPTH, N_DEV, -1, full.shape[-1]), 1, 0)
    return jnp.moveaxis(full.reshape(DEPTH, full.shape[1], N_DEV, -1), 2, 0)


MESH_ID = pl.DeviceIdType.MESH
ANY = pl.BlockSpec(memory_space=pl.ANY)


def _place():
    return lax.axis_index("x"), lax.axis_index("y"), lax.axis_index("c")


def _all_gather(blocks):
    n = len(blocks)

    def body(*refs):
        x_refs, out_refs, (send_sems, recv_sems, local_sems) = refs[:n], refs[n:2 * n], refs[2 * n:]
        x, y, c = _place()
        me, sibling = (x, y, c), (x, y, 1 - c)
        chips = [(1 - x, y), (x, 1 - y), (1 - x, 1 - y)]

        def slot(a, px, py, pc):
            return out_refs[a].at[4 * px + 2 * py + pc]

        def copy(a, k, blk, to, src=None):
            return pltpu.make_async_remote_copy(
                src_ref=slot(a, *blk) if src is None else src, dst_ref=slot(a, *blk),
                send_sem=send_sems.at[7 * a + k], recv_sem=recv_sems.at[7 * a + k], device_id=to,
                device_id_type=MESH_ID)

        mine = [pltpu.make_async_copy(x_refs[a], slot(a, *me), local_sems.at[a]) for a in range(n)]
        for cp in mine:
            cp.start()
        first = []
        for a in range(n):
            first += [copy(a, 0, me, sibling, src=x_refs[a])]
            first += [copy(a, 1 + j, me, (*chip, c), src=x_refs[a]) for j, chip in enumerate(chips)]
        for cp in first:
            cp.start()
        passed = []
        for j, chip in enumerate(chips):
            for a in range(n):
                copy(a, 1 + j, (*chip, c), me).wait_recv()
                passed.append(copy(a, 4 + j, (*chip, c), sibling))
                passed[-1].start()
        for a in range(n):
            copy(a, 0, sibling, me).wait_recv()
            for j, chip in enumerate(chips):
                copy(a, 4 + j, (*chip, 1 - c), me).wait_recv()
        for cp in first + passed:
            cp.wait_send()
        for cp in mine:
            cp.wait()

    return pl.pallas_call(
        body, name="all_gather_weights",
        out_shape=[SDS((N_DEV,) + b.shape, b.dtype) for b in blocks],
        in_specs=[ANY] * n, out_specs=[ANY] * n,
        scratch_shapes=[pltpu.SemaphoreType.DMA((7 * n,)), pltpu.SemaphoreType.DMA((7 * n,)),
                        pltpu.SemaphoreType.DMA((n,))],
    )(*blocks)


def _swap_sibling(arrs):
    n = len(arrs)

    def body(*refs):
        a_refs, out_refs, (send_sems, recv_sems) = refs[:n], refs[n:2 * n], refs[2 * n:]
        x, y, c = _place()
        copies = [pltpu.make_async_remote_copy(
            src_ref=a_refs[a].at[:, 1 - c], dst_ref=out_refs[a], send_sem=send_sems.at[a], recv_sem=recv_sems.at[a],
            device_id=(x, y, 1 - c), device_id_type=MESH_ID) for a in range(n)]
        for cp in copies:
            cp.start()
        for cp in copies:
            cp.wait()

    return pl.pallas_call(
        body, name="swap_sibling", out_shape=[SDS((a.shape[0],) + a.shape[2:], a.dtype) for a in arrs],
        in_specs=[ANY] * n, out_specs=[ANY] * n,
        scratch_shapes=[pltpu.SemaphoreType.DMA((n,)), pltpu.SemaphoreType.DMA((n,))],
    )(*arrs)


def _exchange_chips(arrs):
    n = len(arrs)

    def body(*refs):
        p_refs, out_refs, (send_sems, recv_sems, local_sems) = refs[:n], refs[n:2 * n], refs[2 * n:]
        x, y, c = _place()
        mine = 2 * x + y
        local = [pltpu.make_async_copy(p_refs[a].at[mine], out_refs[a].at[mine], local_sems.at[a]) for a in range(n)]
        for cp in local:
            cp.start()
        peers = [(1 - x, y), (x, 1 - y), (1 - x, 1 - y)]

        def copy(a, j, src_chip, dst_chip):
            px, py = peers[j]
            return pltpu.make_async_remote_copy(
                src_ref=p_refs[a].at[src_chip], dst_ref=out_refs[a].at[dst_chip], send_sem=send_sems.at[3 * a + j],
                recv_sem=recv_sems.at[3 * a + j], device_id=(px, py, c), device_id_type=MESH_ID)

        copies = [copy(a, j, 2 * px + py, mine) for a in range(n) for j, (px, py) in enumerate(peers)]
        for cp in copies:
            cp.start()
        for a in range(n):
            for j, (px, py) in enumerate(peers):
                copy(a, j, mine, 2 * px + py).wait_recv()
        for cp in copies:
            cp.wait_send()
        for cp in local:
            cp.wait()

    return pl.pallas_call(
        body, name="exchange_chips", out_shape=[SDS(a.shape, a.dtype) for a in arrs],
        in_specs=[ANY] * n, out_specs=[ANY] * n,
        scratch_shapes=[pltpu.SemaphoreType.DMA((3 * n,)), pltpu.SemaphoreType.DMA((3 * n,)),
                        pltpu.SemaphoreType.DMA((n,))],
    )(*arrs)


def _add_mine(g, recv, core, tile, dtype):
    _, _, lead, rows, width = g.shape

    def body(c_ref, g_ref, r_ref, o_ref):
        o_ref[...] = (g_ref[...] + r_ref[...]).astype(dtype)

    spec = pl.BlockSpec((None, None, tile, width), lambda k, l, i, c_ref: (k, l, i, 0))
    return pl.pallas_call(
        body, name="add_sibling", out_shape=SDS(recv.shape, dtype),
        grid_spec=pltpu.PrefetchScalarGridSpec(
            num_scalar_prefetch=1, grid=(g.shape[0], lead, rows // tile),
            in_specs=[pl.BlockSpec((None, None, None, tile, width), lambda k, l, i, c_ref: (k, c_ref[0], l, i, 0)),
                      spec],
            out_specs=spec),
        compiler_params=_cp("parallel", "parallel", "parallel"))(core, g, recv)


def _sum_adamw(parts, w, m, v, tile):
    lead, rows, width = w.shape

    def body(p_ref, w_ref, m_ref, v_ref, g_ref, d_ref, nm_ref, nv_ref):
        g = ((p_ref[0].astype(F32) + p_ref[1].astype(F32)) + p_ref[2].astype(F32)) + p_ref[3].astype(F32)
        nm = ADAM_B1 * m_ref[...] + (1.0 - ADAM_B1) * g
        nv = ADAM_B2 * v_ref[...] + (1.0 - ADAM_B2) * jnp.square(g)
        m_hat = nm / (1.0 - ADAM_B1 ** ADAM_STEP)
        v_hat = nv / (1.0 - ADAM_B2 ** ADAM_STEP)
        g_ref[...] = g
        nm_ref[...] = nm
        nv_ref[...] = nv
        d_ref[...] = -ADAM_LR * (m_hat / (jnp.sqrt(v_hat) + ADAM_EPS) + ADAM_WD * w_ref[...])

    spec = pl.BlockSpec((None, tile, width), lambda l, i: (l, i, 0))
    return pl.pallas_call(
        body, grid=(lead, rows // tile), name="sum_adamw",
        in_specs=[pl.BlockSpec((4, None, tile, width), lambda l, i: (0, l, i, 0)), spec, spec, spec],
        out_specs=[spec] * 4, out_shape=[SDS((lead, rows, width), F32)] * 4,
        compiler_params=_cp("parallel", "parallel"))(parts, w, m, v)


def kernel(x, p, positions, g_mix, w_in, sink, g_q, w_uq, g_kv, w_ukv, w_br_a, w_br_b, w_out, g_ple, w_ple_gate, w_ple_proj, g_final, loss_target, m_g_mix, m_w_in, m_sink, m_g_q, m_w_uq, m_g_kv, m_w_ukv, m_w_br_a, m_w_br_b, m_w_out, m_g_ple, m_w_ple_gate, m_w_ple_proj, m_g_final, v_g_mix, v_w_in, v_sink, v_g_q, v_w_uq, v_g_kv, v_w_ukv, v_w_br_a, v_w_br_b, v_w_out, v_g_ple, v_w_ple_gate, v_w_ple_proj, v_g_final):
    weights = dict(g_mix=g_mix, w_in=w_in, sink=sink, g_q=g_q, w_uq=w_uq, g_kv=g_kv, w_ukv=w_ukv, w_br_a=w_br_a,
                   w_br_b=w_br_b, w_out=w_out, g_ple=g_ple, w_ple_gate=w_ple_gate, w_ple_proj=w_ple_proj,
                   g_final=g_final)
    mom1 = dict(g_mix=m_g_mix, w_in=m_w_in, sink=m_sink, g_q=m_g_q, w_uq=m_w_uq, g_kv=m_g_kv, w_ukv=m_w_ukv,
                w_br_a=m_w_br_a, w_br_b=m_w_br_b, w_out=m_w_out, g_ple=m_g_ple, w_ple_gate=m_w_ple_gate,
                w_ple_proj=m_w_ple_proj, g_final=m_g_final)
    mom2 = dict(g_mix=v_g_mix, w_in=v_w_in, sink=v_sink, g_q=v_g_q, w_uq=v_w_uq, g_kv=v_g_kv, w_ukv=v_w_ukv,
                w_br_a=v_w_br_a, w_br_b=v_w_br_b, w_out=v_w_out, g_ple=v_g_ple, w_ple_gate=v_w_ple_gate,
                w_ple_proj=v_w_ple_proj, g_final=v_g_final)
    wide = lambda d: d["w_in"]
    rows = lambda d: _pack_rows(d)[None]

    got_wide, got_rows = _all_gather([wide(weights).astype(BF16), rows(weights).astype(BF16)])
    blocks = _unpack_rows(got_rows[:, 0])
    wts = _kernel_weights(got_wide, {n: _join(n, blocks[n]) for n, _ in ROWS_PIECES})
    small = _small_params(g_mix, sink, g_q, g_kv, g_ple, g_final)

    loss, grad_x, grads, dg_final = _local_step(x, p, positions, loss_target, small, wts)

    stacked = {n: jnp.stack([grads[i][n] for i in range(DEPTH)]) for n in grads[0] if n != "w_in"}
    stacked["g_final"] = dg_final
    pay_wide = _win_grad_layout([grads[i]["w_in"] for i in range(DEPTH)], 256)
    pay_wide = pay_wide.reshape(N_DEV // 2, 2, DEPTH, D, WIDE_W)
    pay_rows = _pack_rows({n: _split(n, stacked[n]) for n, _ in ROWS_PIECES})
    pay_rows = pay_rows.reshape(N_DEV // 2, 2, 1, ROWS_N, LANES)
    pay_vec = jnp.broadcast_to(_pack_vec(stacked, loss[0, 0]), (N_DEV // 2, 2, 1, VEC_ROWS, LANES))
    core = lax.axis_index("c").astype(jnp.int32).reshape(1)
    from_sibling = _swap_sibling([pay_wide, pay_rows, pay_vec])
    chip_partial = [_add_mine(pay_wide, from_sibling[0], core, WIDE_TILE, BF16),
                    _add_mine(pay_rows, from_sibling[1], core, ROWS_TILE, BF16),
                    _add_mine(pay_vec, from_sibling[2], core, VEC_ROWS, F32)]
    parts_wide, parts_rows, parts_vec = _exchange_chips(chip_partial)
    out_wide = _sum_adamw(parts_wide, wide(weights), wide(mom1), wide(mom2), WIDE_TILE)
    out_rows = _sum_adamw(parts_rows, rows(weights), rows(mom1), rows(mom2), ROWS_TILE)
    out_vec = _sum_adamw(parts_vec, _pack_vec(weights), _pack_vec(mom1), _pack_vec(mom2), VEC_ROWS)

    outs = []
    for ow, orow, ovec in zip(out_wide, out_rows, out_vec):
        named = _unpack_rows(orow[0])
        named.update(_unpack_vec(ovec)[0])
        named["w_in"] = ow
        outs += [named[n] for n in weights]
    loss = _unpack_vec(out_vec[0])[1]
    return (loss, grad_x, *outs)
```

```python
import functools

import jax
import jax.numpy as jnp
from jax import lax
from jax.experimental import pallas as pl
from jax.experimental.pallas import tpu as pltpu

F32, BF16 = jnp.float32, jnp.bfloat16
SDS = jax.ShapeDtypeStruct

D = 1024
DEPTH = 2
PLE = 256
BLK = 128
EPS = 1e-6
NEG = -1e30
SWA_H, SWA_KV, SWA_DH = 8, 2, 64
MLA_H, MLA_NOPE, MLA_ROPE, MLA_V = 8, 64, 32, 64
MLA_QK = MLA_NOPE + MLA_ROPE
QL, KVL = 256, 128
IN_W = 4256
N_DEV = 8

V7X_VMEM_BYTES = 64 * 1024 * 1024
LANES = 128
VMEM_LIMIT = V7X_VMEM_BYTES * 7 // 8

ZW = 4352
Z_MA, Z_MB, Z_AQ, Z_AG, Z_BG, Z_QD, Z_AK, Z_AV, Z_KVD, Z_KR = 0, 1024, 2048, 2560, 3072, 3584, 3840, 3968, 4096, 4224
QFW = MLA_H * LANES
KVW = QFW + MLA_H * MLA_V
MLA_SCALE = MLA_QK ** -0.5
SWA_SCALE = SWA_DH ** -0.5
ROLL_UP, ROLL_DOWN = MLA_ROPE // 2, LANES - MLA_ROPE // 2

ADAM_LR, ADAM_B1, ADAM_B2, ADAM_EPS, ADAM_WD, ADAM_STEP = 0.001, 0.9, 0.999, 1e-08, 0.01, 10

FLAT_W = 1024


def _cp(*sem):
    return pltpu.CompilerParams(dimension_semantics=sem, vmem_limit_bytes=VMEM_LIMIT)


def _row(tm, w, col=0):
    return pl.BlockSpec((tm, w), lambda i: (i, col))


def _res(shape, layer=None):
    if layer is None:
        return pl.BlockSpec(shape, lambda *_: (0,) * len(shape), pipeline_mode=pl.Buffered(1))
    return pl.BlockSpec((None,) + shape, lambda *_: (layer,) + (0,) * len(shape), pipeline_mode=pl.Buffered(1))


def _acc(shape):
    return pl.BlockSpec(shape, lambda *_: (0,) * len(shape))


def _rstd(xf):
    return lax.rsqrt(jnp.mean(xf * xf, axis=-1, keepdims=True) + EPS)


def _norm_bwd(dh, n, r, g):
    dn = dh * g
    return r * (dn - n * jnp.mean(dn * n, axis=-1, keepdims=True)), dh * n


def _nt(a, b):
    return lax.dot_general(a, b, (((1,), (1,)), ((), ())), preferred_element_type=F32)


def _tn(a, b):
    return lax.dot_general(a, b, (((0,), (0,)), ((), ())), preferred_element_type=F32)


def _nn(a, b):
    return jnp.dot(a, b, preferred_element_type=F32)


def _sig(x):
    return jax.nn.sigmoid(x)


def _rope(t, c, s1, s2):
    return t * c + pltpu.roll(t, ROLL_UP, 1) * s1 + pltpu.roll(t, ROLL_DOWN, 1) * s2


def _rope_t(d, c, s1, s2):
    return d * c + pltpu.roll(d * s1, ROLL_DOWN, 1) + pltpu.roll(d * s2, ROLL_UP, 1)


def _fwd_in(x, g, w, tm, layer):
    T = x.shape[0]

    def body(x_ref, g_ref, w_ref, z_ref, h_ref):
        xf = x_ref[...]
        h = ((xf * _rstd(xf)) * g_ref[...]).astype(BF16)
        h_ref[...] = h
        z_ref[...] = _nn(h, w_ref[...])

    return pl.pallas_call(
        body, grid=(T // tm,), name="fwd_in",
        in_specs=[_row(tm, D), _res((1, D), layer), _res((D, ZW), 0)],
        out_specs=[_row(tm, ZW), _row(tm, D)],
        out_shape=[SDS((T, ZW), F32), SDS((T, D), BF16)],
        compiler_params=_cp("parallel"))(x, g, w)


def _fwd_prep(z, gq, gkv, wq, wkv, tc, ts1, ts2, tm, layer):
    T = z.shape[0]

    def body(qd_ref, kvd_ref, kr_ref, gq_ref, gkv_ref, wq_ref, wkv_ref, c_ref, s1_ref, s2_ref, q_ref, k_ref, v_ref):
        qd, kvd = qd_ref[...], kvd_ref[...]
        hq = ((qd * _rstd(qd)) * gq_ref[...]).astype(BF16)
        hkv = ((kvd * _rstd(kvd)) * gkv_ref[...]).astype(BF16)
        qf = _nn(hq, wq_ref[...])
        kvf = _nn(hkv, wkv_ref[...])
        c, s1, s2 = c_ref[...], s1_ref[...], s2_ref[...]
        krb = _rope(kr_ref[...], c, s1, s2)
        for h in range(MLA_H):
            sl = slice(LANES * h, LANES * (h + 1))
            q_ref[:, sl] = _rope(qf[:, sl], c, s1, s2).astype(BF16)
            k_ref[:, sl] = (kvf[:, sl] + krb).astype(BF16)
        v_ref[...] = kvf[:, QFW:].astype(BF16)

    return pl.pallas_call(
        body, grid=(T // tm,), name="fwd_prep",
        in_specs=[_row(tm, QL, Z_QD // QL), _row(tm, KVL, Z_KVD // KVL), _row(tm, LANES, Z_KR // LANES),
                  _res((1, QL), layer), _res((1, KVL), layer), _res((QL, QFW), 0), _res((KVL, KVW), 0),
                  _row(tm, LANES), _row(tm, LANES), _row(tm, LANES)],
        out_specs=[_row(tm, QFW), _row(tm, QFW), _row(tm, MLA_H * MLA_V)],
        out_shape=[SDS((T, QFW), BF16), SDS((T, QFW), BF16), SDS((T, MLA_H * MLA_V), BF16)],
        compiler_params=_cp("parallel"))(z, z, z, gq, gkv, wq, wkv, tc, ts1, ts2)


def _grid_ends(grid):
    ids = [pl.program_id(a) for a in range(len(grid))]
    inner_first = functools.reduce(jnp.logical_and, [i == 0 for i in ids[1:]])
    last = functools.reduce(jnp.logical_and, [i == g - 1 for i, g in zip(ids, grid)])
    return (ids[0] == 0) & inner_first, (ids[0] == grid[0] // 2) & inner_first, last


def _mla_fwd(qf, kf, v, nb, seq, tq, gather=()):
    T = qf.shape[0]
    nq = seq // tq
    pw = 2 * LANES
    pairs = [(qi, ki) for qi in range(nq) for ki in range(qi + 1)]
    qi_tab = jnp.array([qk[0] for qk in pairs], jnp.int32)
    ki_tab = jnp.array([qk[1] for qk in pairs], jnp.int32)
    grid = (nb, MLA_H // 2, len(pairs))
    n_g = len(gather)

    def body(qi_ref, ki_ref, q_ref, k_ref, v_ref, *rest):
        x_refs, (o_ref, lse_ref), got_refs = rest[:n_g], rest[n_g:n_g + 2], rest[n_g + 2:2 * n_g + 2]
        (m_s, l_s, acc_s), sems = rest[2 * n_g + 2:2 * n_g + 5], rest[2 * n_g + 5:]
        qi, ki = qi_ref[pl.program_id(2)], ki_ref[pl.program_id(2)]
        if n_g:
            start, forward, finish = _gather_phases(x_refs, got_refs, *sems)
            at_first, at_middle, at_last = _grid_ends(grid)
            pl.when(at_first)(start)
            pl.when(at_middle)(forward)

        @pl.when(ki == 0)
        def _():
            m_s[...] = jnp.full(m_s.shape, NEG, F32)
            l_s[...] = jnp.zeros(l_s.shape, F32)
            acc_s[...] = jnp.zeros(acc_s.shape, F32)

        def step(masked):
            if masked:
                keys = lax.broadcasted_iota(jnp.int32, (tq, tq), 0)
                queries = lax.broadcasted_iota(jnp.int32, (tq, tq), 1)
                mask = keys <= queries
            ss = []
            for j in range(2):
                wide = slice(LANES * j, LANES * (j + 1))
                s = _nt(k_ref[:, wide], q_ref[:, wide]) * MLA_SCALE
                ss.append(jnp.where(mask, s, NEG) if masked else s)
            ps, alphas = [], []
            for j in range(2):
                m_prev = m_s[j]
                m_new = jnp.maximum(m_prev, jnp.max(ss[j], axis=0, keepdims=True))
                alpha = jnp.exp(m_prev - m_new)
                p = jnp.exp(ss[j] - m_new)
                l_s[j] = alpha * l_s[j] + jnp.sum(p, axis=0, keepdims=True)
                m_s[j] = m_new
                ps.append(p.astype(BF16))
                alphas.append(alpha)
            for j in range(2):
                rows = slice(MLA_V * j, MLA_V * (j + 1))
                acc_s[rows, :] = alphas[j] * acc_s[rows, :] + _tn(v_ref[:, rows], ps[j])

        @pl.when(ki < qi)
        def _():
            step(False)

        @pl.when(ki == qi)
        def _():
            step(True)
            for j in range(2):
                rows = slice(MLA_V * j, MLA_V * (j + 1))
                acc_s[rows, :] = acc_s[rows, :] / l_s[j]
                lse_ref[j:j + 1, :] = m_s[j] + jnp.log(l_s[j])
            o_ref[...] = acc_s[...].T

        if n_g:
            pl.when(at_last)(finish)

    q_map = lambda b, hp, s, qi_ref, ki_ref: (b * nq + qi_ref[s], hp)
    kv_map = lambda b, hp, s, qi_ref, ki_ref: (b * nq + ki_ref[s], hp)
    return pl.pallas_call(
        body, name="mla_fwd_gather" if n_g else "mla_fwd",
        grid_spec=pltpu.PrefetchScalarGridSpec(
            num_scalar_prefetch=2, grid=grid,
            in_specs=[pl.BlockSpec((tq, pw), q_map), pl.BlockSpec((tq, pw), kv_map),
                      pl.BlockSpec((tq, LANES), kv_map)] + [ANY] * n_g,
            out_specs=[pl.BlockSpec((tq, LANES), q_map),
                       pl.BlockSpec((None, 2, tq), lambda b, hp, s, qi_ref, ki_ref: (hp, 0, b * nq + qi_ref[s]))]
            + [ANY] * n_g,
            scratch_shapes=[pltpu.VMEM((2, 1, tq), F32), pltpu.VMEM((2, 1, tq), F32), pltpu.VMEM((LANES, tq), F32)]
            + (_gather_sems(n_g) if n_g else [])),
        out_shape=[SDS((T, MLA_H * MLA_V), F32), SDS((MLA_H // 2, 2, T), F32)] + _gather_out(gather),
        compiler_params=_cp("arbitrary", "arbitrary", "arbitrary"))(qi_tab, ki_tab, qf, kf, v, *gather)


def _swa_specs(nblk):
    cur = lambda b, n: (b * nblk + n, 0)
    prev = lambda b, n: (b * nblk + jnp.maximum(n - 1, 0), 0)
    kvc = Z_AK // (2 * BLK)
    return [pl.BlockSpec(memory_space=pltpu.SMEM),
            pl.BlockSpec((BLK, 512), lambda b, n: (b * nblk + n, Z_AQ // 512)),
            pl.BlockSpec((BLK, 2 * BLK), lambda b, n: (b * nblk + n, kvc)),
            pl.BlockSpec((BLK, 2 * BLK), lambda b, n: (b * nblk + jnp.maximum(n - 1, 0), kvc)),
            pl.BlockSpec((BLK, 1), cur),
            pl.BlockSpec((BLK, 1), prev),
            pl.BlockSpec((1, 1, BLK), lambda b, n: (b * nblk + n, 0, 0))]


def _swa_scores(n, q_ref, kvc_ref, kvp_ref, pcc_ref, pcp_ref, pr_ref):
    kv = jnp.concatenate([kvp_ref[...], kvc_ref[...]], axis=0)
    kb, vb = kv[:, :BLK].astype(BF16), kv[:, BLK:].astype(BF16)
    dist = pr_ref[0] - jnp.concatenate([pcp_ref[...], pcc_ref[...]], axis=0)
    key = lax.broadcasted_iota(jnp.int32, (2 * BLK, BLK), 0)
    qry = lax.broadcasted_iota(jnp.int32, (2 * BLK, BLK), 1)
    valid = (key > qry) & (key <= qry + BLK) & ((key >= BLK) | (n > 0))

    def scores(h):
        g = h // (SWA_H // SWA_KV)
        qh = q_ref[:, SWA_DH * h:SWA_DH * (h + 1)].astype(BF16)
        s = _nt(kb[:, SWA_DH * g:SWA_DH * (g + 1)], qh) * SWA_SCALE - (2.0 ** -(h + 1)) * dist
        return qh, jnp.where(valid, s, NEG)

    return kb, vb, scores


def _swa_fwd(sink, z, pos_col, pos_row, nb, seq, layer):
    T = z.shape[0]
    nblk = seq // BLK

    def body(sink_ref, q_ref, kvc_ref, kvp_ref, pcc_ref, pcp_ref, pr_ref, o_ref, lse_ref):
        kb, vb, scores = _swa_scores(pl.program_id(1), q_ref, kvc_ref, kvp_ref, pcc_ref, pcp_ref, pr_ref)
        ss = [scores(h)[1] for h in range(SWA_H)]
        es, dens = [], []
        for h in range(SWA_H):
            sk = sink_ref[layer, h]
            m = jnp.maximum(jnp.max(ss[h], axis=0, keepdims=True), sk)
            e = jnp.exp(ss[h] - m)
            den = jnp.sum(e, axis=0, keepdims=True) + jnp.exp(sk - m)
            lse_ref[h:h + 1, :] = m + jnp.log(den)
            es.append(e.astype(BF16))
            dens.append(den)
        outs = []
        for h in range(SWA_H):
            g = h // (SWA_H // SWA_KV)
            outs.append(_tn(vb[:, SWA_DH * g:SWA_DH * (g + 1)], es[h]) / dens[h])
        o_ref[...] = jnp.concatenate(outs, axis=0).T

    return pl.pallas_call(
        body, grid=(nb, nblk), name="swa_fwd",
        in_specs=_swa_specs(nblk),
        out_specs=[pl.BlockSpec((BLK, 512), lambda b, n: (b * nblk + n, 0)),
                   pl.BlockSpec((SWA_H, BLK), lambda b, n: (0, b * nblk + n))],
        out_shape=[SDS((T, 512), F32), SDS((SWA_H, T), F32)],
        compiler_params=_cp("parallel", "parallel"))(sink, z, z, z, pos_col, pos_col, pos_row)


def _fwd_merge(x, oa, ob, z, wa, wb, wo, tm, layer):
    T = x.shape[0]

    def body(x_ref, oa_ref, ob_ref, ag_ref, bg_ref, ma_ref, mb_ref, wa_ref, wb_ref, wo_ref, x1_ref):
        ag, bg = ag_ref[...], bg_ref[...]
        ua = _nn((oa_ref[...] * (ag * _sig(ag))).astype(BF16), wa_ref[...])
        ub = _nn((ob_ref[...] * (bg * _sig(bg))).astype(BF16), wb_ref[...])
        y = _sig(ma_ref[...]) * ua + _sig(mb_ref[...]) * ub
        x1_ref[...] = x_ref[...] + _nn(y.astype(BF16), wo_ref[...])

    return pl.pallas_call(
        body, grid=(T // tm,), name="fwd_merge",
        in_specs=[_row(tm, D), _row(tm, 512), _row(tm, 512), _row(tm, 512, Z_AG // 512), _row(tm, 512, Z_BG // 512),
                  _row(tm, D, Z_MA // D), _row(tm, D, Z_MB // D),
                  _res((512, D), 0), _res((512, D), 0), _res((D, D), 0)],
        out_specs=_row(tm, D),
        out_shape=SDS((T, D), F32),
        compiler_params=_cp("parallel"))(x, oa, ob, z, z, z, z, wa, wb, wo)


def _fwd_ple(x1, p, g, wpg, wpp, tm, layer):
    T = x1.shape[0]

    def body(x_ref, p_ref, g_ref, wpg_ref, wpp_ref, x2_ref, pg_ref, pp_ref):
        xf = x_ref[...]
        h1 = ((xf * _rstd(xf)) * g_ref[...]).astype(BF16)
        pg = _sig(_nn(h1, wpg_ref[...]))
        pp = _nn(p_ref[...].astype(BF16), wpp_ref[...])
        pg_ref[...] = pg
        pp_ref[...] = pp
        x2_ref[...] = xf + pg * pp

    return pl.pallas_call(
        body, grid=(T // tm,), name="fwd_ple",
        in_specs=[_row(tm, D), pl.BlockSpec((None, tm, PLE), lambda i: (layer, i, 0)),
                  _res((1, D), layer), _res((D, D), 0), _res((PLE, D), 0)],
        out_specs=[_row(tm, D)] * 3,
        out_shape=[SDS((T, D), F32)] * 3,
        compiler_params=_cp("parallel"))(x1, p, g, wpg, wpp)


def _loss_head(x, g, tgt, tm):
    T = x.shape[0]

    def body(x_ref, g_ref, t_ref, dx_ref, dg_ref, loss_ref):
        @pl.when(pl.program_id(0) == 0)
        def _():
            dg_ref[...] = jnp.zeros(dg_ref.shape, F32)
            loss_ref[...] = jnp.zeros(loss_ref.shape, F32)

        xf, gf = x_ref[...], g_ref[...]
        r = _rstd(xf)
        n = xf * r
        err = n * gf - t_ref[...]
        loss_ref[...] += 0.5 * jnp.sum(jnp.mean(err * err, axis=-1, keepdims=True), axis=0, keepdims=True)
        dx, dgr = _norm_bwd(err * (1.0 / D), n, r, gf)
        dx_ref[...] = dx
        dg_ref[...] += jnp.sum(dgr, axis=0, keepdims=True)

    return pl.pallas_call(
        body, grid=(T // tm,), name="loss_head",
        in_specs=[_row(tm, D), _res((1, D)), _row(tm, D)],
        out_specs=[_row(tm, D), _acc((1, D)), _acc((1, LANES))],
        out_shape=[SDS((T, D), F32), SDS((1, D), F32), SDS((1, LANES), F32)],
        compiler_params=_cp("arbitrary"))(x, g, tgt)


def _bwd_ple(dx2, x1, pg, pp, p, g, wpg, tm, layer):
    T = x1.shape[0]

    def body(d_ref, x_ref, pg_ref, pp_ref, p_ref, g_ref, w_ref, dx_ref, dwg_ref, dwp_ref, dg_ref):
        @pl.when(pl.program_id(0) == 0)
        def _():
            dwg_ref[...] = jnp.zeros(dwg_ref.shape, F32)
            dwp_ref[...] = jnp.zeros(dwp_ref.shape, F32)
            dg_ref[...] = jnp.zeros(dg_ref.shape, F32)

        d, xf, pg, gf = d_ref[...], x_ref[...], pg_ref[...], g_ref[...]
        r = _rstd(xf)
        n = xf * r
        dpgl = (d * pp_ref[...] * pg * (1.0 - pg)).astype(BF16)
        dwg_ref[...] += _tn((n * gf).astype(BF16), dpgl)
        dwp_ref[...] += _tn(p_ref[...].astype(BF16), (d * pg).astype(BF16))
        dxn, dgr = _norm_bwd(_nt(dpgl, w_ref[...]), n, r, gf)
        dx_ref[...] = d + dxn
        dg_ref[...] += jnp.sum(dgr, axis=0, keepdims=True)

    return pl.pallas_call(
        body, grid=(T // tm,), name="bwd_ple",
        in_specs=[_row(tm, D)] * 4 + [pl.BlockSpec((None, tm, PLE), lambda i: (layer, i, 0)),
                                      _res((1, D), layer), _res((D, D), 0)],
        out_specs=[_row(tm, D), _acc((D, D)), _acc((PLE, D)), _acc((1, D))],
        out_shape=[SDS((T, D), F32), SDS((D, D), F32), SDS((PLE, D), F32), SDS((1, D), F32)],
        compiler_params=_cp("arbitrary"))(dx2, x1, pg, pp, p, g, wpg)


def _bwd_merge(dx1, oa, ob, z, wa, wb, wo, tm, layer):
    T = dx1.shape[0]

    def body(d_ref, oa_ref, ob_ref, ag_ref, bg_ref, ma_ref, mb_ref, wa_ref, wb_ref, wo_ref,
             doa_ref, dob_ref, dag_ref, dbg_ref, dma_ref, dmb_ref, dsa_ref, dsb_ref, dwa_ref, dwb_ref, dwo_ref):
        @pl.when(pl.program_id(0) == 0)
        def _():
            dwa_ref[...] = jnp.zeros(dwa_ref.shape, F32)
            dwb_ref[...] = jnp.zeros(dwb_ref.shape, F32)
            dwo_ref[...] = jnp.zeros(dwo_ref.shape, F32)

        db = d_ref[...].astype(BF16)
        gated = []
        for o_ref, gate_ref, w_ref in ((oa_ref, ag_ref, wa_ref), (ob_ref, bg_ref, wb_ref)):
            raw, gate = o_ref[...], gate_ref[...]
            sg = _sig(gate)
            silu = gate * sg
            ob16 = (raw * silu).astype(BF16)
            gated.append((raw, gate, sg, silu, ob16, _nn(ob16, w_ref[...])))
        ua, ub = gated[0][5], gated[1][5]
        sa, sb = _sig(ma_ref[...]), _sig(mb_ref[...])
        dwo_ref[...] += _tn((sa * ua + sb * ub).astype(BF16), db)
        dy = _nt(db, wo_ref[...])
        dma_ref[...] = (dy * ua * sa * (1.0 - sa)).astype(BF16)
        dmb_ref[...] = (dy * ub * sb * (1.0 - sb)).astype(BF16)
        for (s, w_ref, do_ref, dgate_ref, dw_ref, ds_ref), (raw, gate, sg, silu, ob16, _) in zip((
                (sa, wa_ref, doa_ref, dag_ref, dwa_ref, dsa_ref),
                (sb, wb_ref, dob_ref, dbg_ref, dwb_ref, dsb_ref)), gated):
            du = (dy * s).astype(BF16)
            dw_ref[...] += _tn(ob16, du)
            do = _nt(du, w_ref[...])
            draw = do * silu
            do_ref[...] = draw.astype(BF16)
            dgate_ref[...] = (do * raw * (sg * (1.0 + gate * (1.0 - sg)))).astype(BF16)
            ds_ref[...] = jnp.sum((draw * raw).T.reshape(MLA_H, MLA_V, tm), axis=1)

    return pl.pallas_call(
        body, grid=(T // tm,), name="bwd_merge",
        in_specs=[_row(tm, D), _row(tm, 512), _row(tm, 512), _row(tm, 512, Z_AG // 512), _row(tm, 512, Z_BG // 512),
                  _row(tm, D, Z_MA // D), _row(tm, D, Z_MB // D),
                  _res((512, D), 0), _res((512, D), 0), _res((D, D), 0)],
        out_specs=[_row(tm, 512)] * 4 + [_row(tm, D)] * 2 + [pl.BlockSpec((MLA_H, tm), lambda i: (0, i))] * 2
        + [_acc((512, D)), _acc((512, D)), _acc((D, D))],
        out_shape=[SDS((T, 512), BF16), SDS((T, 512), BF16), SDS((T, 512), BF16), SDS((T, 512), BF16),
                   SDS((T, D), BF16), SDS((T, D), BF16), SDS((MLA_H, T), F32), SDS((MLA_H, T), F32),
                   SDS((512, D), F32), SDS((512, D), F32), SDS((D, D), F32)],
        compiler_params=_cp("arbitrary"))(dx1, oa, ob, z, z, z, z, wa, wb, wo)


def _mla_bwd(qf, kf, v, do, lse, dsum, nb, seq, tq, exchange=()):
    T = qf.shape[0]
    nq = seq // tq
    pw = 2 * LANES
    pairs = [(qi, ki) for ki in range(nq) for qi in range(ki, nq)]
    qi_tab = jnp.array([qk[0] for qk in pairs], jnp.int32)
    ki_tab = jnp.array([qk[1] for qk in pairs], jnp.int32)
    grid = (nb, MLA_H // 2, len(pairs))
    n_x = len(exchange)

    def body(qi_ref, ki_ref, q_ref, k_ref, v_ref, do_ref, lse_ref, dsum_ref, *rest):
        p_refs, (dq_ref, dk_ref, dv_ref), got_refs = rest[:n_x], rest[n_x:n_x + 3], rest[n_x + 3:2 * n_x + 3]
        (dk_s, dv_s, dqt_s), sems = rest[2 * n_x + 3:2 * n_x + 6], rest[2 * n_x + 6:]
        step_id = pl.program_id(2)
        qi, ki = qi_ref[step_id], ki_ref[step_id]
        if n_x:
            start, finish = _exchange_phases(p_refs, got_refs, *sems)
            at_first, _, at_last = _grid_ends(grid)
            pl.when(at_first)(start)

        @pl.when(step_id == 0)
        def _():
            dqt_s[...] = jnp.zeros(dqt_s.shape, F32)

        @pl.when(qi == ki)
        def _():
            dk_s[...] = jnp.zeros(dk_s.shape, F32)
            dv_s[...] = jnp.zeros(dv_s.shape, F32)

        def step(masked):
            if masked:
                keys = lax.broadcasted_iota(jnp.int32, (tq, tq), 0)
                queries = lax.broadcasted_iota(jnp.int32, (tq, tq), 1)
                mask = keys <= queries
            for j in range(2):
                wide = slice(LANES * j, LANES * (j + 1))
                sl = slice(MLA_V * j, MLA_V * (j + 1))
                q, k = q_ref[:, wide], k_ref[:, wide]
                dob = do_ref[:, sl].astype(BF16)
                s = _nt(k, q) * MLA_SCALE
                if masked:
                    s = jnp.where(mask, s, NEG)
                p = jnp.exp(s - lse_ref[j:j + 1, :])
                dv_s[:, sl] += _nn(p.astype(BF16), dob)
                ds = (p * (_nt(v_ref[:, sl], dob) - dsum_ref[j:j + 1, :]) * MLA_SCALE).astype(BF16)
                dk_s[:, wide] += _nn(ds, q)
                dqt_s[qi, wide, :] += _tn(k, ds)

        @pl.when(qi > ki)
        def _():
            step(False)

        @pl.when(qi == ki)
        def _():
            step(True)

        @pl.when(qi == nq - 1)
        def _():
            dk_ref[...] = dk_s[...]
            dv_ref[...] = dv_s[...]

        @pl.when(step_id == len(pairs) - 1)
        def _():
            for n in range(nq):
                dq_ref[tq * n:tq * (n + 1), :] = dqt_s[n].T

        if n_x:
            pl.when(at_last)(finish)

    qmap = lambda b, hp, s, qi_ref, ki_ref: (b * nq + qi_ref[s], hp)
    kmap = lambda b, hp, s, qi_ref, ki_ref: (b * nq + ki_ref[s], hp)
    stat = pl.BlockSpec((None, 2, tq), lambda b, hp, s, qi_ref, ki_ref: (hp, 0, b * nq + qi_ref[s]))
    return pl.pallas_call(
        body, name="mla_bwd_exchange" if n_x else "mla_bwd",
        grid_spec=pltpu.PrefetchScalarGridSpec(
            num_scalar_prefetch=2, grid=grid,
            in_specs=[pl.BlockSpec((tq, pw), qmap), pl.BlockSpec((tq, pw), kmap), pl.BlockSpec((tq, LANES), kmap),
                      pl.BlockSpec((tq, LANES), qmap), stat, stat] + [ANY] * n_x,
            out_specs=[pl.BlockSpec((seq, pw), lambda b, hp, s, qi_ref, ki_ref: (b, hp)),
                       pl.BlockSpec((tq, pw), kmap), pl.BlockSpec((tq, LANES), kmap)] + [ANY] * n_x,
            scratch_shapes=[pltpu.VMEM((tq, pw), F32), pltpu.VMEM((tq, LANES), F32), pltpu.VMEM((nq, pw, tq), F32)]
            + (_exchange_sems(n_x) if n_x else [])),
        out_shape=[SDS((T, QFW), F32), SDS((T, QFW), F32), SDS((T, MLA_H * MLA_V), F32)]
        + [SDS(a.shape, a.dtype) for a in exchange],
        compiler_params=_cp("arbitrary", "arbitrary", "arbitrary"))(qi_tab, ki_tab, qf, kf, v, do, lse, dsum, *exchange)


def _swa_bwd(sink, z, pos_col, pos_row, do, lse, dsum, nb, seq, layer):
    T = z.shape[0]
    nblk = seq // BLK

    def body(sink_ref, q_ref, kvc_ref, kvp_ref, pcc_ref, pcp_ref, pr_ref, do_ref, lse_ref, dsum_ref,
             dq_ref, dkv_ref, dsink_ref):
        b, n = pl.program_id(0), pl.program_id(1)

        @pl.when((b == 0) & (n == 0))
        def _():
            dsink_ref[...] = jnp.zeros(dsink_ref.shape, F32)

        @pl.when(n == 0)
        def _():
            dkv_ref[...] = jnp.zeros(dkv_ref.shape, F32)

        kb, vb, scores = _swa_scores(n, q_ref, kvc_ref, kvp_ref, pcc_ref, pcp_ref, pr_ref)
        lane = lax.broadcasted_iota(jnp.int32, (1, LANES), 1)
        dsink = jnp.zeros((1, LANES), F32)
        dkv = [[None, None], [None, None]]
        dqs = []
        gsl = lambda h: slice(SWA_DH * (h // (SWA_H // SWA_KV)), SWA_DH * (h // (SWA_H // SWA_KV) + 1))
        qs, ss, dobs, dps = [], [], [], []
        for h in range(SWA_H):
            qh, s = scores(h)
            dob = do_ref[:, SWA_DH * h:SWA_DH * (h + 1)].astype(BF16)
            qs.append(qh)
            ss.append(s)
            dobs.append(dob)
            dps.append(_nt(vb[:, gsl(h)], dob))
        pbs, dss = [], []
        for h in range(SWA_H):
            lse, dsum = lse_ref[h:h + 1, :], dsum_ref[h:h + 1, :]
            p = jnp.exp(ss[h] - lse)
            pbs.append(p.astype(BF16))
            dss.append((p * (dps[h] - dsum) * SWA_SCALE).astype(BF16))
            dsk = jnp.sum(-jnp.exp(sink_ref[layer, h] - lse) * dsum, axis=1, keepdims=True)
            dsink = dsink + jnp.where(lane == h, dsk, 0.0)
        for h in range(SWA_H):
            g = h // (SWA_H // SWA_KV)
            dqs.append(_tn(kb[:, gsl(h)], dss[h]))
            dk, dv = _nn(dss[h], qs[h]), _nn(pbs[h], dobs[h])
            dkv[g][0] = dk if dkv[g][0] is None else dkv[g][0] + dk
            dkv[g][1] = dv if dkv[g][1] is None else dkv[g][1] + dv
        dq_ref[...] = jnp.concatenate(dqs, axis=0).T.astype(BF16)
        dsink_ref[...] += dsink
        upd = jnp.concatenate([dkv[0][0], dkv[1][0], dkv[0][1], dkv[1][1]], axis=1)
        dkv_ref[pl.ds(pl.multiple_of(n * BLK, BLK), BLK), :] += upd[BLK:]

        @pl.when(n > 0)
        def _():
            dkv_ref[pl.ds(pl.multiple_of((n - 1) * BLK, BLK), BLK), :] += upd[:BLK]

    return pl.pallas_call(
        body, grid=(nb, nblk), name="swa_bwd",
        in_specs=_swa_specs(nblk) + [pl.BlockSpec((BLK, 512), lambda b, n: (b * nblk + n, 0))]
        + [pl.BlockSpec((SWA_H, BLK), lambda b, n: (0, b * nblk + n))] * 2,
        out_specs=[pl.BlockSpec((BLK, 512), lambda b, n: (b * nblk + n, 0)),
                   pl.BlockSpec((seq, 2 * BLK), lambda b, n: (b, 0)),
                   pl.BlockSpec((1, LANES), lambda b, n: (0, 0))],
        out_shape=[SDS((T, 512), BF16), SDS((T, 2 * BLK), F32), SDS((1, LANES), F32)],
        compiler_params=_cp("arbitrary", "arbitrary"))(sink, z, z, z, pos_col, pos_col, pos_row, do, lse, dsum)


def _bwd_prep(dq, dk, dv, z, gq, gkv, wq, wkv, tc, ts1, ts2, tm, layer):
    T = z.shape[0]

    def body(dq_ref, dk_ref, dv_ref, qd_ref, kvd_ref, gq_ref, gkv_ref, wq_ref, wkv_ref, c_ref, s1_ref, s2_ref,
             dqd_ref, dkvd_ref, dkr_ref, dwq_ref, dwkv_ref, dgq_ref, dgkv_ref, dqb_s, dkvb_s):
        @pl.when(pl.program_id(0) == 0)
        def _():
            for ref in (dwq_ref, dwkv_ref, dgq_ref, dgkv_ref):
                ref[...] = jnp.zeros(ref.shape, F32)

        c, s1, s2 = c_ref[...], s1_ref[...], s2_ref[...]
        lane = lax.broadcasted_iota(jnp.int32, (1, LANES), 1)
        rope_lanes = (lane >= MLA_NOPE) & (lane < MLA_QK)
        dkb = jnp.zeros((tm, LANES), F32)
        for h in range(MLA_H):
            sl = slice(LANES * h, LANES * (h + 1))
            dqb_s[:, sl] = _rope_t(dq_ref[:, sl], c, s1, s2).astype(BF16)
            dkh = dk_ref[:, sl]
            dkb = dkb + dkh
            dkvb_s[:, sl] = dkh.astype(BF16)
        dkvb_s[:, QFW:] = dv_ref[...].astype(BF16)
        dkr_ref[...] = _rope_t(jnp.where(rope_lanes, dkb, 0.0), c, s1, s2).astype(BF16)

        for (x_ref, g_ref, w_ref, d_s, dx_ref, dw_ref, dg_ref) in (
                (qd_ref, gq_ref, wq_ref, dqb_s, dqd_ref, dwq_ref, dgq_ref),
                (kvd_ref, gkv_ref, wkv_ref, dkvb_s, dkvd_ref, dwkv_ref, dgkv_ref)):
            xf, gf, db = x_ref[...], g_ref[...], d_s[...]
            r = _rstd(xf)
            n = xf * r
            dw_ref[...] += _tn((n * gf).astype(BF16), db)
            dx, dgr = _norm_bwd(_nt(db, w_ref[...]), n, r, gf)
            dx_ref[...] = dx.astype(BF16)
            dg_ref[...] += jnp.sum(dgr, axis=0, keepdims=True)

    return pl.pallas_call(
        body, grid=(T // tm,), name="bwd_prep",
        in_specs=[_row(tm, QFW), _row(tm, QFW), _row(tm, MLA_H * MLA_V),
                  _row(tm, QL, Z_QD // QL), _row(tm, KVL, Z_KVD // KVL),
                  _res((1, QL), layer), _res((1, KVL), layer), _res((QL, QFW), 0), _res((KVL, KVW), 0),
                  _row(tm, LANES), _row(tm, LANES), _row(tm, LANES)],
        out_specs=[_row(tm, QL), _row(tm, KVL), _row(tm, LANES),
                   _acc((QL, QFW)), _acc((KVL, KVW)), _acc((1, QL)), _acc((1, KVL))],
        out_shape=[SDS((T, QL), BF16), SDS((T, KVL), BF16), SDS((T, LANES), BF16),
                   SDS((QL, QFW), F32), SDS((KVL, KVW), F32), SDS((1, QL), F32), SDS((1, KVL), F32)],
        scratch_shapes=[pltpu.VMEM((tm, QFW), BF16), pltpu.VMEM((tm, KVW), BF16)],
        compiler_params=_cp("arbitrary"))(dq, dk, dv, z, z, gq, gkv, wq, wkv, tc, ts1, ts2)


def _bwd_in(pieces, x, g, dres, w, tm, layer):
    T = x.shape[0]
    widths = [pc.shape[1] for pc in pieces]
    assert sum(widths) == ZW
    n_p = len(pieces)

    def body(*refs):
        p_refs, (x_ref, g_ref, r_ref, w_ref, dx_ref, dz_ref, dg_ref) = refs[:n_p], refs[n_p:]

        @pl.when(pl.program_id(0) == 0)
        def _():
            dg_ref[...] = jnp.zeros(dg_ref.shape, F32)

        off = 0
        for ref, wd in zip(p_refs, widths):
            dz_ref[:, off:off + wd] = ref[...].astype(BF16)
            off += wd
        xf, gf = x_ref[...], g_ref[...]
        r = _rstd(xf)
        n = xf * r
        dx, dgr = _norm_bwd(_nt(dz_ref[...], w_ref[...]), n, r, gf)
        dx_ref[...] = r_ref[...] + dx
        dg_ref[...] += jnp.sum(dgr, axis=0, keepdims=True)

    return pl.pallas_call(
        body, grid=(T // tm,), name="bwd_in",
        in_specs=[_row(tm, wd) for wd in widths] + [_row(tm, D), _res((1, D), layer), _row(tm, D),
                                                    _res((D, ZW), 0)],
        out_specs=[_row(tm, D), _row(tm, ZW), _acc((1, D))],
        out_shape=[SDS((T, D), F32), SDS((T, ZW), BF16), SDS((1, D), F32)],
        compiler_params=_cp("arbitrary"))(*pieces, x, g, dres, w)


def _wgrad_in(hb, dzb, tm):
    T = hb.shape[0]
    half = ZW // 2

    def body(h_ref, dz_ref, dw_ref):
        @pl.when(pl.program_id(1) == 0)
        def _():
            dw_ref[...] = jnp.zeros(dw_ref.shape, F32)

        dw_ref[...] += _tn(h_ref[...], dz_ref[...])

    return pl.pallas_call(
        body, grid=(2, T // tm), name="wgrad_in",
        in_specs=[pl.BlockSpec((tm, D), lambda j, t: (t, 0)), pl.BlockSpec((tm, half), lambda j, t: (t, j))],
        out_specs=pl.BlockSpec((D, half), lambda j, t: (0, j)),
        out_shape=SDS((D, ZW), F32),
        compiler_params=_cp("parallel", "arbitrary"))(hb, dzb)


IN_PIECES = ((0, 512, Z_AQ), (512, 128, Z_AK), (640, 128, Z_AV), (768, 512, Z_AG), (1280, 256, Z_QD),
             (1536, 128, Z_KVD), (1664, MLA_ROPE, Z_KR + MLA_NOPE), (1696, 512, Z_BG), (2208, 1024, Z_MA),
             (3232, 1024, Z_MB))
WIDE_W = IN_W // N_DEV


def _column_runs():
    runs = []
    for start, width, kstart in IN_PIECES:
        col = start
        while col < start + width:
            dev = col // WIDE_W
            stop = min(start + width, (dev + 1) * WIDE_W)
            runs.append((dev, col - dev * WIDE_W, stop - col, kstart + col - start))
            col = stop
    return runs


def _win_layout(blocks, tm):
    runs = _column_runs()

    def body(g_ref, o_ref):
        o_ref[:, Z_KR:Z_KR + LANES] = jnp.zeros((tm, LANES), o_ref.dtype)
        for dev, lo, n, k in runs:
            o_ref[:, k:k + n] = g_ref[dev, :, lo:lo + n]

    return pl.pallas_call(
        body, grid=(D // tm,), name="win_layout",
        in_specs=[pl.BlockSpec((N_DEV, None, tm, WIDE_W), lambda i: (0, 0, i, 0))],
        out_specs=pl.BlockSpec((None, tm, ZW), lambda i: (0, i, 0)),
        out_shape=SDS((1, D, ZW), blocks.dtype),
        compiler_params=_cp("parallel"))(blocks)


def _win_grad_layout(dw, tm):
    runs = _column_runs()

    def body(g_ref, o_ref):
        for dev, lo, n, k in runs:
            o_ref[dev, :, lo:lo + n] = g_ref[:, k:k + n]

    return pl.pallas_call(
        body, grid=(D // tm,), name="win_grad_layout",
        in_specs=[_row(tm, ZW)],
        out_specs=pl.BlockSpec((N_DEV, None, tm, WIDE_W), lambda i: (0, 0, i, 0)),
        out_shape=SDS((N_DEV, 1, D, WIDE_W), F32),
        compiler_params=_cp("parallel"))(dw)


def _wuq_to_kernel(w):
    w = w.reshape(w.shape[:-1] + (MLA_H, MLA_QK))
    w = jnp.pad(w, [(0, 0)] * (w.ndim - 1) + [(0, LANES - MLA_QK)])
    return w.reshape(w.shape[:-2] + (QFW,))


def _wuq_from_kernel(g):
    g = g.reshape(g.shape[:-1] + (MLA_H, LANES))[..., :MLA_QK]
    return g.reshape(g.shape[:-2] + (MLA_H * MLA_QK,))


def _wukv_to_kernel(w):
    w = w.reshape(w.shape[:-1] + (MLA_H, MLA_NOPE + MLA_V))
    k = jnp.pad(w[..., :MLA_NOPE], [(0, 0)] * (w.ndim - 1) + [(0, LANES - MLA_NOPE)])
    v = w[..., MLA_NOPE:]
    return jnp.concatenate([k.reshape(k.shape[:-2] + (QFW,)), v.reshape(v.shape[:-2] + (MLA_H * MLA_V,))], axis=-1)


def _wukv_from_kernel(g):
    k = g[..., :QFW].reshape(g.shape[:-1] + (MLA_H, LANES))[..., :MLA_NOPE]
    v = g[..., QFW:].reshape(g.shape[:-1] + (MLA_H, MLA_V))
    kv = jnp.concatenate([k, v], axis=-1)
    return kv.reshape(kv.shape[:-2] + (MLA_H * (MLA_NOPE + MLA_V),))


def _rope_tables(pos):
    half = MLA_ROPE // 2
    inv = 10000.0 ** (-jnp.arange(0, MLA_ROPE, 2, dtype=F32) / MLA_ROPE)
    ang = pos.astype(F32)[:, None] * inv
    cos, sin = jnp.cos(ang), jnp.sin(ang)
    one = jnp.ones((pos.shape[0], MLA_NOPE), F32)
    zero = lambda n: jnp.zeros((pos.shape[0], n), F32)
    tc = jnp.concatenate([one, cos, cos, one[:, :LANES - MLA_QK]], axis=1)
    ts1 = jnp.concatenate([zero(MLA_NOPE + half), sin, zero(LANES - MLA_QK)], axis=1)
    ts2 = jnp.concatenate([zero(MLA_NOPE), -sin, zero(LANES - MLA_NOPE - half)], axis=1)
    return tc, ts1, ts2


def _local_step(x, p, positions, loss_target, small, wts, next_blocks=(), weights_of=None, partials_of=None):
    nb, seq, _ = x.shape
    T = nb * seq
    tm = min(256, T)
    tq = min(512, seq)
    xf = x.reshape(T, D)
    pos = positions.reshape(T)
    posf = pos.astype(F32)
    pos_col, pos_row = posf.reshape(T, 1), posf.reshape(T // BLK, 1, BLK)
    tc, ts1, ts2 = _rope_tables(pos)

    wts, sm = list(wts), small
    pl_in = p.reshape(DEPTH, T, PLE)
    saved = []
    for i in range(DEPTH):
        w = wts[i]
        z, hb = _fwd_in(xf, sm["g_mix"], w["w_in"], tm, i)
        oa, lse_a = _swa_fwd(sm["sink"], z, pos_col, pos_row, nb, seq, i)
        qf, kf, v = _fwd_prep(z, sm["g_q"], sm["g_kv"], w["w_uq"], w["w_ukv"], tc, ts1, ts2, tm, i)
        if i == 0 and next_blocks:
            ob, lse_b, *gathered = _mla_fwd(qf, kf, v, nb, seq, tq, gather=next_blocks)
            wts.append(weights_of(gathered))
        else:
            ob, lse_b = _mla_fwd(qf, kf, v, nb, seq, tq)
        x1 = _fwd_merge(xf, oa, ob, z, w["w_br_a"], w["w_br_b"], w["w_out"], tm, i)
        x2, pg, pp = _fwd_ple(x1, pl_in, sm["g_ple"], w["w_ple_gate"], w["w_ple_proj"], tm, i)
        saved.append(dict(x=xf, z=z, hb=hb, oa=oa, lse_a=lse_a, qf=qf, kf=kf, v=v, ob=ob, lse_b=lse_b,
                          x1=x1, pg=pg, pp=pp))
        xf = x2

    dx, dg_final, loss = _loss_head(xf, small["g_final"], loss_target.reshape(T, D), tm)

    grads = [None] * DEPTH
    exchanged = []
    for i in reversed(range(DEPTH)):
        sv, w = saved[i], wts[i]
        dx1, dwpg, dwpp, dg_ple = _bwd_ple(dx, sv["x1"], sv["pg"], sv["pp"], pl_in, sm["g_ple"], w["w_ple_gate"],
                                           tm, i)
        doa, dob, dag, dbg, dma, dmb, dsum_a, dsum_b, dwa, dwb, dwo = _bwd_merge(
            dx1, sv["oa"], sv["ob"], sv["z"], w["w_br_a"], w["w_br_b"], w["w_out"], tm, i)
        partials = partials_of(grads[1]) if (i == 0 and partials_of is not None) else ()
        dq_b, dk_b, dv_b, *exchanged = _mla_bwd(sv["qf"], sv["kf"], sv["v"], dob, sv["lse_b"],
                                                dsum_b.reshape(MLA_H // 2, 2, T), nb, seq, tq, exchange=partials)
        dqd, dkvd, dkr, dwq, dwkv, dgq, dgkv = _bwd_prep(dq_b, dk_b, dv_b, sv["z"], sm["g_q"], sm["g_kv"],
                                                         w["w_uq"], w["w_ukv"], tc, ts1, ts2, tm, i)
        dq_a, dkv_a, dsink = _swa_bwd(sm["sink"], sv["z"], pos_col, pos_row, doa, sv["lse_a"], dsum_a, nb, seq, i)
        dx, dzb, dg_mix = _bwd_in([dma, dmb, dq_a, dag, dbg, dqd, dkv_a, dkvd, dkr], sv["x"], sm["g_mix"], dx1,
                                  w["w_in"], tm, i)
        dwin = _wgrad_in(sv["hb"], dzb, tm)
        grads[i] = dict(g_mix=dg_mix[0], w_in=dwin, sink=dsink[0, :SWA_H], g_q=dgq[0],
                        w_uq=_wuq_from_kernel(dwq), g_kv=dgkv[0], w_ukv=_wukv_from_kernel(dwkv), w_br_a=dwa,
                        w_br_b=dwb, w_out=dwo, g_ple=dg_ple[0], w_ple_gate=dwpg, w_ple_proj=dwpp)
    return loss, dx.reshape(nb, seq, D), grads, dg_final[0], exchanged


def _kernel_weights(gathered):
    wide, rows = gathered
    blocks = _unpack_rows(rows)
    out = {n: _join(n, blocks[n]) for n, _ in ROWS_PIECES}
    out.update(w_in=_win_layout(wide, 256), w_uq=_wuq_to_kernel(out["w_uq"]), w_ukv=_wukv_to_kernel(out["w_ukv"]))
    return out


def _small_params(g_mix, sink, g_q, g_kv, g_ple, g_final):
    return dict(g_mix=g_mix[:, None], sink=sink, g_q=g_q[:, None], g_kv=g_kv[:, None], g_ple=g_ple[:, None],
                g_final=g_final[None])


UQ_W = MLA_H * MLA_QK // N_DEV
ROWS_PIECES = (("w_uq", QL), ("w_ukv", KVL), ("w_br_a", 512), ("w_br_b", 512), ("w_out", D), ("w_ple_gate", D),
               ("w_ple_proj", PLE))
SMALL = (("g_mix", (DEPTH, D)), ("sink", (DEPTH, SWA_H)), ("g_q", (DEPTH, QL)), ("g_kv", (DEPTH, KVL)),
         ("g_ple", (DEPTH, D)), ("g_final", (D,)))
VEC_ROWS = 48
ROWS_N = sum(r for _, r in ROWS_PIECES)
WIDE_TILE, ROWS_TILE = 256, ROWS_N // 2


def _to_rows(name, a):
    if name == "w_uq":
        a = jnp.pad(a, [(0, 0)] * (a.ndim - 1) + [(0, LANES - UQ_W)])
    return a.reshape(a.shape[:-2] + (-1, LANES))


def _from_rows(name, r):
    if name in ("w_out", "w_ple_gate"):
        return r.reshape(r.shape[:-2] + (D // N_DEV, D))
    return r[..., :UQ_W] if name == "w_uq" else r


def _pack_rows(blocks):
    return jnp.concatenate([_to_rows(n, blocks[n]) for n, _ in ROWS_PIECES], axis=-2)


def _unpack_rows(rows):
    blocks, off = {}, 0
    for n, r in ROWS_PIECES:
        blocks[n] = _from_rows(n, rows[..., off:off + r, :])
        off += r
    return blocks


def _pack_vec(vectors, loss=None):
    parts = [vectors[n].reshape(-1) for n, _ in SMALL] + ([] if loss is None else [loss.reshape(1)])
    vec = jnp.concatenate(parts)
    return jnp.pad(vec, (0, VEC_ROWS * LANES - vec.shape[0])).reshape(1, VEC_ROWS, LANES)


def _unpack_vec(vec):
    vec = vec.reshape(-1)
    vectors, off = {}, 0
    for n, shp in SMALL:
        size = 1
        for s in shp:
            size *= s
        vectors[n] = vec[off:off + size].reshape(shp)
        off += size
    return vectors, vec[off]


def _join(name, blocks):
    if name in ("w_out", "w_ple_gate"):
        return jnp.moveaxis(blocks, 0, 1).reshape(blocks.shape[1], -1, blocks.shape[-1])
    return jnp.moveaxis(blocks, 0, 2).reshape(blocks.shape[1], blocks.shape[2], -1)


def _split(name, full):
    if name in ("w_out", "w_ple_gate"):
        return jnp.moveaxis(full.reshape(full.shape[0], N_DEV, -1, full.shape[-1]), 1, 0)
    return jnp.moveaxis(full.reshape(full.shape[0], full.shape[1], N_DEV, -1), 2, 0)


MESH_ID = pl.DeviceIdType.MESH
ANY = pl.BlockSpec(memory_space=pl.ANY)


def _place():
    return lax.axis_index("x"), lax.axis_index("y"), lax.axis_index("c")


def _all_gather(blocks):
    n = len(blocks)

    def body(*refs):
        start, forward, finish = _gather_phases(refs[:n], refs[n:2 * n], *refs[2 * n:])
        start()
        forward()
        finish()

    return pl.pallas_call(
        body, name="all_gather_weights", out_shape=_gather_out(blocks),
        in_specs=[ANY] * n, out_specs=[ANY] * n, scratch_shapes=_gather_sems(n))(*blocks)


def _gather_out(blocks):
    return [SDS((N_DEV,) + b.shape, b.dtype) for b in blocks]


def _gather_sems(n):
    return [pltpu.SemaphoreType.DMA((7 * n,)), pltpu.SemaphoreType.DMA((7 * n,)), pltpu.SemaphoreType.DMA((n,))]


def _gather_phases(x_refs, out_refs, send_sems, recv_sems, local_sems):
    n = len(x_refs)
    x, y, c = _place()
    me, sibling = (x, y, c), (x, y, 1 - c)
    chips = [(1 - x, y), (x, 1 - y), (1 - x, 1 - y)]

    def slot(a, px, py, pc):
        return out_refs[a].at[4 * px + 2 * py + pc]

    def copy(a, k, blk, to, src=None):
        return pltpu.make_async_remote_copy(
            src_ref=slot(a, *blk) if src is None else src, dst_ref=slot(a, *blk),
            send_sem=send_sems.at[7 * a + k], recv_sem=recv_sems.at[7 * a + k], device_id=to,
            device_id_type=MESH_ID)

    def mine():
        return [pltpu.make_async_copy(x_refs[a], slot(a, *me), local_sems.at[a]) for a in range(n)]

    def first():
        out = []
        for a in range(n):
            out += [copy(a, 0, me, sibling, src=x_refs[a])]
            out += [copy(a, 1 + j, me, (*chip, c), src=x_refs[a]) for j, chip in enumerate(chips)]
        return out

    def passed():
        return [copy(a, 4 + j, (*chip, c), sibling) for j, chip in enumerate(chips) for a in range(n)]

    def start():
        for cp in mine() + first():
            cp.start()

    def forward():
        for j, chip in enumerate(chips):
            for a in range(n):
                copy(a, 1 + j, (*chip, c), me).wait_recv()
                copy(a, 4 + j, (*chip, c), sibling).start()

    def finish():
        for a in range(n):
            copy(a, 0, sibling, me).wait_recv()
            for j, chip in enumerate(chips):
                copy(a, 4 + j, (*chip, 1 - c), me).wait_recv()
        for cp in first() + passed():
            cp.wait_send()
        for cp in mine():
            cp.wait()

    return start, forward, finish


def _swap_sibling(arrs):
    n = len(arrs)

    def body(*refs):
        a_refs, out_refs, (send_sems, recv_sems) = refs[:n], refs[n:2 * n], refs[2 * n:]
        x, y, c = _place()
        copies = [pltpu.make_async_remote_copy(
            src_ref=a_refs[a].at[:, 1 - c], dst_ref=out_refs[a], send_sem=send_sems.at[a], recv_sem=recv_sems.at[a],
            device_id=(x, y, 1 - c), device_id_type=MESH_ID) for a in range(n)]
        for cp in copies:
            cp.start()
        for cp in copies:
            cp.wait()

    return pl.pallas_call(
        body, name="swap_sibling", out_shape=[SDS((a.shape[0],) + a.shape[2:], a.dtype) for a in arrs],
        in_specs=[ANY] * n, out_specs=[ANY] * n,
        scratch_shapes=[pltpu.SemaphoreType.DMA((n,)), pltpu.SemaphoreType.DMA((n,))],
    )(*arrs)


def _exchange_chips(arrs):
    n = len(arrs)

    def body(*refs):
        start, finish = _exchange_phases(refs[:n], refs[n:2 * n], *refs[2 * n:])
        start()
        finish()

    return pl.pallas_call(
        body, name="exchange_chips", out_shape=[SDS(a.shape, a.dtype) for a in arrs],
        in_specs=[ANY] * n, out_specs=[ANY] * n, scratch_shapes=_exchange_sems(n))(*arrs)


def _exchange_sems(n):
    return [pltpu.SemaphoreType.DMA((3 * n,)), pltpu.SemaphoreType.DMA((3 * n,)), pltpu.SemaphoreType.DMA((n,))]


def _exchange_phases(p_refs, out_refs, send_sems, recv_sems, local_sems):
    n = len(p_refs)
    x, y, c = _place()
    mine = 2 * x + y
    peers = [(1 - x, y), (x, 1 - y), (1 - x, 1 - y)]

    def local():
        return [pltpu.make_async_copy(p_refs[a].at[mine], out_refs[a].at[mine], local_sems.at[a]) for a in range(n)]

    def copy(a, j, src_chip, dst_chip):
        px, py = peers[j]
        return pltpu.make_async_remote_copy(
            src_ref=p_refs[a].at[src_chip], dst_ref=out_refs[a].at[dst_chip], send_sem=send_sems.at[3 * a + j],
            recv_sem=recv_sems.at[3 * a + j], device_id=(px, py, c), device_id_type=MESH_ID)

    def sends():
        return [copy(a, j, 2 * px + py, mine) for a in range(n) for j, (px, py) in enumerate(peers)]

    def start():
        for cp in local() + sends():
            cp.start()

    def finish():
        for a in range(n):
            for j, (px, py) in enumerate(peers):
                copy(a, j, mine, 2 * px + py).wait_recv()
        for cp in sends():
            cp.wait_send()
        for cp in local():
            cp.wait()

    return start, finish


def _add_mine(g, recv, core, tile, dtype):
    _, _, lead, rows, width = g.shape

    def body(c_ref, g_ref, r_ref, o_ref):
        o_ref[...] = (g_ref[...] + r_ref[...]).astype(dtype)

    spec = pl.BlockSpec((None, None, tile, width), lambda k, l, i, c_ref: (k, l, i, 0))
    return pl.pallas_call(
        body, name="add_sibling", out_shape=SDS(recv.shape, dtype),
        grid_spec=pltpu.PrefetchScalarGridSpec(
            num_scalar_prefetch=1, grid=(g.shape[0], lead, rows // tile),
            in_specs=[pl.BlockSpec((None, None, None, tile, width), lambda k, l, i, c_ref: (k, c_ref[0], l, i, 0)),
                      spec],
            out_specs=spec),
        compiler_params=_cp("parallel", "parallel", "parallel"))(core, g, recv)


def _sum_adamw(parts, w, m, v, tile):
    lead, rows, width = w.shape
    last = rows // tile - 1

    def body(*refs):
        p_refs, (w_ref, m_ref, v_ref, g_ref, d_ref, nm_ref, nv_ref) = refs[:lead], refs[lead:]
        for layer in range(lead):
            @pl.when(pl.program_id(0) == layer)
            def _(p_ref=p_refs[layer]):
                g = ((p_ref[0].astype(F32) + p_ref[1].astype(F32)) + p_ref[2].astype(F32)) + p_ref[3].astype(F32)
                nm = ADAM_B1 * m_ref[...] + (1.0 - ADAM_B1) * g
                nv = ADAM_B2 * v_ref[...] + (1.0 - ADAM_B2) * jnp.square(g)
                m_hat = nm / (1.0 - ADAM_B1 ** ADAM_STEP)
                v_hat = nv / (1.0 - ADAM_B2 ** ADAM_STEP)
                g_ref[...] = g
                nm_ref[...] = nm
                nv_ref[...] = nv
                d_ref[...] = -ADAM_LR * (m_hat / (jnp.sqrt(v_hat) + ADAM_EPS) + ADAM_WD * w_ref[...])

    pspec = lambda layer: pl.BlockSpec(
        (4, None, tile, width),
        lambda l, i: (0, 0, jnp.where(l == layer, i, jnp.where(l > layer, last, 0)), 0))
    spec = pl.BlockSpec((None, tile, width), lambda l, i: (l, i, 0))
    return pl.pallas_call(
        body, grid=(lead, rows // tile), name="sum_adamw",
        in_specs=[pspec(layer) for layer in range(lead)] + [spec, spec, spec],
        out_specs=[spec] * 4, out_shape=[SDS((lead, rows, width), F32)] * 4,
        compiler_params=_cp("arbitrary", "arbitrary"))(*parts, w, m, v)


def kernel(x, p, positions, g_mix, w_in, sink, g_q, w_uq, g_kv, w_ukv, w_br_a, w_br_b, w_out, g_ple, w_ple_gate, w_ple_proj, g_final, loss_target, m_g_mix, m_w_in, m_sink, m_g_q, m_w_uq, m_g_kv, m_w_ukv, m_w_br_a, m_w_br_b, m_w_out, m_g_ple, m_w_ple_gate, m_w_ple_proj, m_g_final, v_g_mix, v_w_in, v_sink, v_g_q, v_w_uq, v_g_kv, v_w_ukv, v_w_br_a, v_w_br_b, v_w_out, v_g_ple, v_w_ple_gate, v_w_ple_proj, v_g_final):
    weights = dict(g_mix=g_mix, w_in=w_in, sink=sink, g_q=g_q, w_uq=w_uq, g_kv=g_kv, w_ukv=w_ukv, w_br_a=w_br_a,
                   w_br_b=w_br_b, w_out=w_out, g_ple=g_ple, w_ple_gate=w_ple_gate, w_ple_proj=w_ple_proj,
                   g_final=g_final)
    mom1 = dict(g_mix=m_g_mix, w_in=m_w_in, sink=m_sink, g_q=m_g_q, w_uq=m_w_uq, g_kv=m_g_kv, w_ukv=m_w_ukv,
                w_br_a=m_w_br_a, w_br_b=m_w_br_b, w_out=m_w_out, g_ple=m_g_ple, w_ple_gate=m_w_ple_gate,
                w_ple_proj=m_w_ple_proj, g_final=m_g_final)
    mom2 = dict(g_mix=v_g_mix, w_in=v_w_in, sink=v_sink, g_q=v_g_q, w_uq=v_w_uq, g_kv=v_g_kv, w_ukv=v_w_ukv,
                w_br_a=v_w_br_a, w_br_b=v_w_br_b, w_out=v_w_out, g_ple=v_g_ple, w_ple_gate=v_w_ple_gate,
                w_ple_proj=v_w_ple_proj, g_final=v_g_final)
    assert DEPTH == 2
    wide = lambda d: d["w_in"]
    rows = lambda d: _pack_rows(d)
    core = lax.axis_index("c").astype(jnp.int32).reshape(1)

    w16 = [wide(weights).astype(BF16), rows(weights).astype(BF16)]
    wts0 = _kernel_weights(_all_gather([a[:1] for a in w16]))
    small = _small_params(g_mix, sink, g_q, g_kv, g_ple, g_final)

    def chip_partials(g, vec=None):
        pay = [_win_grad_layout(g["w_in"], 256).reshape(N_DEV // 2, 2, 1, D, WIDE_W),
               _pack_rows({n: _split(n, g[n][None]) for n, _ in ROWS_PIECES}).reshape(N_DEV // 2, 2, 1, ROWS_N, LANES)]
        kinds = [(WIDE_TILE, BF16), (ROWS_TILE, BF16)]
        if vec is not None:
            pay.append(jnp.broadcast_to(vec, (N_DEV // 2, 2, 1, VEC_ROWS, LANES)))
            kinds.append((VEC_ROWS, F32))
        got = _swap_sibling(pay)
        return [_add_mine(a, b, core, tile, dtype) for a, b, (tile, dtype) in zip(pay, got, kinds)]

    loss, grad_x, grads, dg_final, parts1 = _local_step(
        x, p, positions, loss_target, small, [wts0], next_blocks=[a[1:] for a in w16], weights_of=_kernel_weights,
        partials_of=chip_partials)

    vectors = {n: jnp.stack([grads[i][n] for i in range(DEPTH)]) for n, _ in SMALL[:-1]}
    vectors["g_final"] = dg_final
    parts0 = _exchange_chips(chip_partials(grads[0], _pack_vec(vectors, loss[0, 0])))
    out_wide = _sum_adamw([parts0[0], parts1[0]], wide(weights), wide(mom1), wide(mom2), WIDE_TILE)
    out_rows = _sum_adamw([parts0[1], parts1[1]], rows(weights), rows(mom1), rows(mom2), ROWS_TILE)
    out_vec = _sum_adamw([parts0[2]], _pack_vec(weights), _pack_vec(mom1), _pack_vec(mom2), VEC_ROWS)

    outs = []
    for ow, orow, ovec in zip(out_wide, out_rows, out_vec):
        named = _unpack_rows(orow)
        named.update(_unpack_vec(ovec)[0])
        named["w_in"] = ow
        outs += [named[n] for n in weights]
    loss = _unpack_vec(out_vec[0])[1]
    return (loss, grad_x, *outs)
```

```python
import functools

import jax
import jax.numpy as jnp
from jax import lax
from jax.experimental import pallas as pl
from jax.experimental.pallas import tpu as pltpu

F32, BF16 = jnp.float32, jnp.bfloat16
SDS = jax.ShapeDtypeStruct

D = 1024
DEPTH = 2
PLE = 256
BLK = 128
EPS = 1e-6
NEG = -1e30
SWA_H, SWA_KV, SWA_DH = 8, 2, 64
MLA_H, MLA_NOPE, MLA_ROPE, MLA_V = 8, 64, 32, 64
MLA_QK = MLA_NOPE + MLA_ROPE
QL, KVL = 256, 128
IN_W = 4256
N_DEV = 8

V7X_VMEM_BYTES = 64 * 1024 * 1024
LANES = 128
VMEM_LIMIT = V7X_VMEM_BYTES * 7 // 8

ZW = 4352
Z_MA, Z_MB, Z_AQ, Z_AG, Z_BG, Z_QD, Z_AK, Z_AV, Z_KVD, Z_KR = 0, 1024, 2048, 2560, 3072, 3584, 3840, 3968, 4096, 4224
QFW = MLA_H * LANES
KVW = QFW + MLA_H * MLA_V
MLA_SCALE = MLA_QK ** -0.5
SWA_SCALE = SWA_DH ** -0.5
ROLL_UP, ROLL_DOWN = MLA_ROPE // 2, LANES - MLA_ROPE // 2

ADAM_LR, ADAM_B1, ADAM_B2, ADAM_EPS, ADAM_WD, ADAM_STEP = 0.001, 0.9, 0.999, 1e-08, 0.01, 10

FLAT_W = 1024


def _cp(*sem):
    return pltpu.CompilerParams(dimension_semantics=sem, vmem_limit_bytes=VMEM_LIMIT)


def _row(tm, w, col=0):
    return pl.BlockSpec((tm, w), lambda i: (i, col))


def _res(shape, layer=None):
    if layer is None:
        return pl.BlockSpec(shape, lambda *_: (0,) * len(shape), pipeline_mode=pl.Buffered(1))
    return pl.BlockSpec((None,) + shape, lambda *_: (layer,) + (0,) * len(shape), pipeline_mode=pl.Buffered(1))


def _acc(shape):
    return pl.BlockSpec(shape, lambda *_: (0,) * len(shape))


def _rstd(xf):
    return lax.rsqrt(jnp.mean(xf * xf, axis=-1, keepdims=True) + EPS)


def _norm_bwd(dh, n, r, g):
    dn = dh * g
    return r * (dn - n * jnp.mean(dn * n, axis=-1, keepdims=True)), dh * n


def _nt(a, b):
    return lax.dot_general(a, b, (((1,), (1,)), ((), ())), preferred_element_type=F32)


def _tn(a, b):
    return lax.dot_general(a, b, (((0,), (0,)), ((), ())), preferred_element_type=F32)


def _nn(a, b):
    return jnp.dot(a, b, preferred_element_type=F32)


def _sig(x):
    return jax.nn.sigmoid(x)


def _rope(t, c, s1, s2):
    return t * c + pltpu.roll(t, ROLL_UP, 1) * s1 + pltpu.roll(t, ROLL_DOWN, 1) * s2


def _rope_t(d, c, s1, s2):
    return d * c + pltpu.roll(d * s1, ROLL_DOWN, 1) + pltpu.roll(d * s2, ROLL_UP, 1)


def _fwd_in(x, g, w, tm, layer, ride=None):
    T = x.shape[0]
    grid = (T // tm,)

    def body(x_ref, g_ref, w_ref, z_ref, h_ref):
        xf = x_ref[...]
        h = ((xf * _rstd(xf)) * g_ref[...]).astype(BF16)
        h_ref[...] = h
        z_ref[...] = _nn(h, w_ref[...])

    r_in, r_out, r_shape, r_scratch, r_args = _ride_args(ride)
    return pl.pallas_call(
        _riding(ride, body, 3, 2, grid), grid=grid, name="fwd_in_ride" if ride else "fwd_in",
        in_specs=[_row(tm, D), _res((1, D), layer), _res((D, ZW), 0)] + r_in,
        out_specs=[_row(tm, ZW), _row(tm, D)] + r_out,
        out_shape=[SDS((T, ZW), F32), SDS((T, D), BF16)] + r_shape, scratch_shapes=r_scratch,
        compiler_params=_cp("arbitrary"))(x, g, w, *r_args)


def _fwd_prep(z, gq, gkv, wq, wkv, tc, ts1, ts2, tm, layer):
    T = z.shape[0]

    def body(qd_ref, kvd_ref, kr_ref, gq_ref, gkv_ref, wq_ref, wkv_ref, c_ref, s1_ref, s2_ref, q_ref, k_ref, v_ref):
        qd, kvd = qd_ref[...], kvd_ref[...]
        hq = ((qd * _rstd(qd)) * gq_ref[...]).astype(BF16)
        hkv = ((kvd * _rstd(kvd)) * gkv_ref[...]).astype(BF16)
        qf = _nn(hq, wq_ref[...])
        kvf = _nn(hkv, wkv_ref[...])
        c, s1, s2 = c_ref[...], s1_ref[...], s2_ref[...]
        krb = _rope(kr_ref[...], c, s1, s2)
        for h in range(MLA_H):
            sl = slice(LANES * h, LANES * (h + 1))
            q_ref[:, sl] = _rope(qf[:, sl], c, s1, s2).astype(BF16)
            k_ref[:, sl] = (kvf[:, sl] + krb).astype(BF16)
        v_ref[...] = kvf[:, QFW:].astype(BF16)

    return pl.pallas_call(
        body, grid=(T // tm,), name="fwd_prep",
        in_specs=[_row(tm, QL, Z_QD // QL), _row(tm, KVL, Z_KVD // KVL), _row(tm, LANES, Z_KR // LANES),
                  _res((1, QL), layer), _res((1, KVL), layer), _res((QL, QFW), 0), _res((KVL, KVW), 0),
                  _row(tm, LANES), _row(tm, LANES), _row(tm, LANES)],
        out_specs=[_row(tm, QFW), _row(tm, QFW), _row(tm, MLA_H * MLA_V)],
        out_shape=[SDS((T, QFW), BF16), SDS((T, QFW), BF16), SDS((T, MLA_H * MLA_V), BF16)],
        compiler_params=_cp("parallel"))(z, z, z, gq, gkv, wq, wkv, tc, ts1, ts2)


def _grid_ends(grid):
    ids = [pl.program_id(a) for a in range(len(grid))]
    inner_first = functools.reduce(jnp.logical_and, [i == 0 for i in ids[1:]], True)
    last = functools.reduce(jnp.logical_and, [i == g - 1 for i, g in zip(ids, grid)])
    return (ids[0] == 0) & inner_first, (ids[0] == grid[0] // 2) & inner_first, last


class _Ride:
    def __init__(self, kind, arrays):
        self.kind, self.arrays, self.n = kind, list(arrays), len(arrays)

    def out_shape(self):
        if self.kind == "gather":
            return _gather_out(self.arrays)
        if self.kind == "swap":
            return [SDS((a.shape[0],) + a.shape[2:], a.dtype) for a in self.arrays]
        return [SDS(a.shape, a.dtype) for a in self.arrays]

    def sems(self):
        if self.kind == "gather":
            return _gather_sems(self.n)
        if self.kind == "swap":
            return _swap_sems(self.n)
        return _exchange_sems(self.n)

    def phases(self, in_refs, out_refs, *sems):
        if self.kind == "gather":
            return _gather_phases(in_refs, out_refs, *sems)
        start, finish = (_swap_phases if self.kind == "swap" else _exchange_phases)(in_refs, out_refs, *sems)
        return start, None, finish


def _riding(ride, body, n_in, n_out, grid):
    if ride is None:
        return body
    n, n_sems = ride.n, len(ride.sems())

    def wrapped(*refs):
        ins, r_in = refs[:n_in], refs[n_in:n_in + n]
        outs, r_out = refs[n_in + n:n_in + n + n_out], refs[n_in + n + n_out:n_in + 2 * n + n_out]
        rest = refs[n_in + 2 * n + n_out:]
        scratch, sems = rest[:len(rest) - n_sems], rest[len(rest) - n_sems:]
        start, middle, finish = ride.phases(r_in, r_out, *sems)
        at_first, at_middle, at_last = _grid_ends(grid)
        pl.when(at_first)(start)
        if middle is not None:
            pl.when(at_middle)(middle)
        body(*ins, *outs, *scratch)
        pl.when(at_last)(finish)

    return wrapped


def _ride_args(ride):
    if ride is None:
        return [], [], [], [], []
    return [ANY] * ride.n, [ANY] * ride.n, ride.out_shape(), ride.sems(), ride.arrays


def _mla_fwd(qf, kf, v, nb, seq, tq, gather=()):
    T = qf.shape[0]
    nq = seq // tq
    pw = 2 * LANES
    pairs = [(qi, ki) for qi in range(nq) for ki in range(qi + 1)]
    qi_tab = jnp.array([qk[0] for qk in pairs], jnp.int32)
    ki_tab = jnp.array([qk[1] for qk in pairs], jnp.int32)
    grid = (nb, MLA_H // 2, len(pairs))
    n_g = len(gather)

    def body(qi_ref, ki_ref, q_ref, k_ref, v_ref, *rest):
        x_refs, (o_ref, lse_ref), got_refs = rest[:n_g], rest[n_g:n_g + 2], rest[n_g + 2:2 * n_g + 2]
        (m_s, l_s, acc_s), sems = rest[2 * n_g + 2:2 * n_g + 5], rest[2 * n_g + 5:]
        qi, ki = qi_ref[pl.program_id(2)], ki_ref[pl.program_id(2)]
        if n_g:
            start, forward, finish = _gather_phases(x_refs, got_refs, *sems)
            at_first, at_middle, at_last = _grid_ends(grid)
            pl.when(at_first)(start)
            pl.when(at_middle)(forward)

        @pl.when(ki == 0)
        def _():
            m_s[...] = jnp.full(m_s.shape, NEG, F32)
            l_s[...] = jnp.zeros(l_s.shape, F32)
            acc_s[...] = jnp.zeros(acc_s.shape, F32)

        def step(masked):
            if masked:
                keys = lax.broadcasted_iota(jnp.int32, (tq, tq), 0)
                queries = lax.broadcasted_iota(jnp.int32, (tq, tq), 1)
                mask = keys <= queries
            ss = []
            for j in range(2):
                wide = slice(LANES * j, LANES * (j + 1))
                s = _nt(k_ref[:, wide], q_ref[:, wide]) * MLA_SCALE
                ss.append(jnp.where(mask, s, NEG) if masked else s)
            ps, alphas = [], []
            for j in range(2):
                m_prev = m_s[j]
                m_new = jnp.maximum(m_prev, jnp.max(ss[j], axis=0, keepdims=True))
                alpha = jnp.exp(m_prev - m_new)
                p = jnp.exp(ss[j] - m_new)
                l_s[j] = alpha * l_s[j] + jnp.sum(p, axis=0, keepdims=True)
                m_s[j] = m_new
                ps.append(p.astype(BF16))
                alphas.append(alpha)
            for j in range(2):
                rows = slice(MLA_V * j, MLA_V * (j + 1))
                acc_s[rows, :] = alphas[j] * acc_s[rows, :] + _tn(v_ref[:, rows], ps[j])

        @pl.when(ki < qi)
        def _():
            step(False)

        @pl.when(ki == qi)
        def _():
            step(True)
            for j in range(2):
                rows = slice(MLA_V * j, MLA_V * (j + 1))
                acc_s[rows, :] = acc_s[rows, :] / l_s[j]
                lse_ref[j:j + 1, :] = m_s[j] + jnp.log(l_s[j])
            o_ref[...] = acc_s[...].T

        if n_g:
            pl.when(at_last)(finish)

    q_map = lambda b, hp, s, qi_ref, ki_ref: (b * nq + qi_ref[s], hp)
    kv_map = lambda b, hp, s, qi_ref, ki_ref: (b * nq + ki_ref[s], hp)
    return pl.pallas_call(
        body, name="mla_fwd_gather" if n_g else "mla_fwd",
        grid_spec=pltpu.PrefetchScalarGridSpec(
            num_scalar_prefetch=2, grid=grid,
            in_specs=[pl.BlockSpec((tq, pw), q_map), pl.BlockSpec((tq, pw), kv_map),
                      pl.BlockSpec((tq, LANES), kv_map)] + [ANY] * n_g,
            out_specs=[pl.BlockSpec((tq, LANES), q_map),
                       pl.BlockSpec((None, 2, tq), lambda b, hp, s, qi_ref, ki_ref: (hp, 0, b * nq + qi_ref[s]))]
            + [ANY] * n_g,
            scratch_shapes=[pltpu.VMEM((2, 1, tq), F32), pltpu.VMEM((2, 1, tq), F32), pltpu.VMEM((LANES, tq), F32)]
            + (_gather_sems(n_g) if n_g else [])),
        out_shape=[SDS((T, MLA_H * MLA_V), F32), SDS((MLA_H // 2, 2, T), F32)] + _gather_out(gather),
        compiler_params=_cp("arbitrary", "arbitrary", "arbitrary"))(qi_tab, ki_tab, qf, kf, v, *gather)


def _swa_specs(nblk):
    cur = lambda b, n: (b * nblk + n, 0)
    prev = lambda b, n: (b * nblk + jnp.maximum(n - 1, 0), 0)
    kvc = Z_AK // (2 * BLK)
    return [pl.BlockSpec(memory_space=pltpu.SMEM),
            pl.BlockSpec((BLK, 512), lambda b, n: (b * nblk + n, Z_AQ // 512)),
            pl.BlockSpec((BLK, 2 * BLK), lambda b, n: (b * nblk + n, kvc)),
            pl.BlockSpec((BLK, 2 * BLK), lambda b, n: (b * nblk + jnp.maximum(n - 1, 0), kvc)),
            pl.BlockSpec((BLK, 1), cur),
            pl.BlockSpec((BLK, 1), prev),
            pl.BlockSpec((1, 1, BLK), lambda b, n: (b * nblk + n, 0, 0))]


def _swa_scores(n, q_ref, kvc_ref, kvp_ref, pcc_ref, pcp_ref, pr_ref):
    kv = jnp.concatenate([kvp_ref[...], kvc_ref[...]], axis=0)
    kb, vb = kv[:, :BLK].astype(BF16), kv[:, BLK:].astype(BF16)
    dist = pr_ref[0] - jnp.concatenate([pcp_ref[...], pcc_ref[...]], axis=0)
    key = lax.broadcasted_iota(jnp.int32, (2 * BLK, BLK), 0)
    qry = lax.broadcasted_iota(jnp.int32, (2 * BLK, BLK), 1)
    valid = (key > qry) & (key <= qry + BLK) & ((key >= BLK) | (n > 0))

    def scores(h):
        g = h // (SWA_H // SWA_KV)
        qh = q_ref[:, SWA_DH * h:SWA_DH * (h + 1)].astype(BF16)
        s = _nt(kb[:, SWA_DH * g:SWA_DH * (g + 1)], qh) * SWA_SCALE - (2.0 ** -(h + 1)) * dist
        return qh, jnp.where(valid, s, NEG)

    return kb, vb, scores


def _swa_fwd(sink, z, pos_col, pos_row, nb, seq, layer):
    T = z.shape[0]
    nblk = seq // BLK

    def body(sink_ref, q_ref, kvc_ref, kvp_ref, pcc_ref, pcp_ref, pr_ref, o_ref, lse_ref):
        kb, vb, scores = _swa_scores(pl.program_id(1), q_ref, kvc_ref, kvp_ref, pcc_ref, pcp_ref, pr_ref)
        ss = [scores(h)[1] for h in range(SWA_H)]
        es, dens = [], []
        for h in range(SWA_H):
            sk = sink_ref[layer, h]
            m = jnp.maximum(jnp.max(ss[h], axis=0, keepdims=True), sk)
            e = jnp.exp(ss[h] - m)
            den = jnp.sum(e, axis=0, keepdims=True) + jnp.exp(sk - m)
            lse_ref[h:h + 1, :] = m + jnp.log(den)
            es.append(e.astype(BF16))
            dens.append(den)
        outs = []
        for h in range(SWA_H):
            g = h // (SWA_H // SWA_KV)
            outs.append(_tn(vb[:, SWA_DH * g:SWA_DH * (g + 1)], es[h]) / dens[h])
        o_ref[...] = jnp.concatenate(outs, axis=0).T

    return pl.pallas_call(
        body, grid=(nb, nblk), name="swa_fwd",
        in_specs=_swa_specs(nblk),
        out_specs=[pl.BlockSpec((BLK, 512), lambda b, n: (b * nblk + n, 0)),
                   pl.BlockSpec((SWA_H, BLK), lambda b, n: (0, b * nblk + n))],
        out_shape=[SDS((T, 512), F32), SDS((SWA_H, T), F32)],
        compiler_params=_cp("parallel", "parallel"))(sink, z, z, z, pos_col, pos_col, pos_row)


def _fwd_merge(x, oa, ob, z, wa, wb, wo, tm, layer):
    T = x.shape[0]

    def body(x_ref, oa_ref, ob_ref, ag_ref, bg_ref, ma_ref, mb_ref, wa_ref, wb_ref, wo_ref, x1_ref):
        ag, bg = ag_ref[...], bg_ref[...]
        ua = _nn((oa_ref[...] * (ag * _sig(ag))).astype(BF16), wa_ref[...])
        ub = _nn((ob_ref[...] * (bg * _sig(bg))).astype(BF16), wb_ref[...])
        y = _sig(ma_ref[...]) * ua + _sig(mb_ref[...]) * ub
        x1_ref[...] = x_ref[...] + _nn(y.astype(BF16), wo_ref[...])

    return pl.pallas_call(
        body, grid=(T // tm,), name="fwd_merge",
        in_specs=[_row(tm, D), _row(tm, 512), _row(tm, 512), _row(tm, 512, Z_AG // 512), _row(tm, 512, Z_BG // 512),
                  _row(tm, D, Z_MA // D), _row(tm, D, Z_MB // D),
                  _res((512, D), 0), _res((512, D), 0), _res((D, D), 0)],
        out_specs=_row(tm, D),
        out_shape=SDS((T, D), F32),
        compiler_params=_cp("parallel"))(x, oa, ob, z, z, z, z, wa, wb, wo)


def _fwd_ple(x1, p, g, wpg, wpp, tm, layer):
    T = x1.shape[0]

    def body(x_ref, p_ref, g_ref, wpg_ref, wpp_ref, x2_ref, pg_ref, pp_ref):
        xf = x_ref[...]
        h1 = ((xf * _rstd(xf)) * g_ref[...]).astype(BF16)
        pg = _sig(_nn(h1, wpg_ref[...]))
        pp = _nn(p_ref[...].astype(BF16), wpp_ref[...])
        pg_ref[...] = pg
        pp_ref[...] = pp
        x2_ref[...] = xf + pg * pp

    return pl.pallas_call(
        body, grid=(T // tm,), name="fwd_ple",
        in_specs=[_row(tm, D), pl.BlockSpec((None, tm, PLE), lambda i: (layer, i, 0)),
                  _res((1, D), layer), _res((D, D), 0), _res((PLE, D), 0)],
        out_specs=[_row(tm, D)] * 3,
        out_shape=[SDS((T, D), F32)] * 3,
        compiler_params=_cp("parallel"))(x1, p, g, wpg, wpp)


def _loss_head(x, g, tgt, tm):
    T = x.shape[0]

    def body(x_ref, g_ref, t_ref, dx_ref, dg_ref, loss_ref):
        @pl.when(pl.program_id(0) == 0)
        def _():
            dg_ref[...] = jnp.zeros(dg_ref.shape, F32)
            loss_ref[...] = jnp.zeros(loss_ref.shape, F32)

        xf, gf = x_ref[...], g_ref[...]
        r = _rstd(xf)
        n = xf * r
        err = n * gf - t_ref[...]
        loss_ref[...] += 0.5 * jnp.sum(jnp.mean(err * err, axis=-1, keepdims=True), axis=0, keepdims=True)
        dx, dgr = _norm_bwd(err * (1.0 / D), n, r, gf)
        dx_ref[...] = dx
        dg_ref[...] += jnp.sum(dgr, axis=0, keepdims=True)

    return pl.pallas_call(
        body, grid=(T // tm,), name="loss_head",
        in_specs=[_row(tm, D), _res((1, D)), _row(tm, D)],
        out_specs=[_row(tm, D), _acc((1, D)), _acc((1, LANES))],
        out_shape=[SDS((T, D), F32), SDS((1, D), F32), SDS((1, LANES), F32)],
        compiler_params=_cp("arbitrary"))(x, g, tgt)


def _bwd_ple(dx2, x1, pg, pp, p, g, wpg, tm, layer, ride=None):
    T = x1.shape[0]
    grid = (T // tm,)

    def body(d_ref, x_ref, pg_ref, pp_ref, p_ref, g_ref, w_ref, dx_ref, dwg_ref, dwp_ref, dg_ref):
        @pl.when(pl.program_id(0) == 0)
        def _():
            dwg_ref[...] = jnp.zeros(dwg_ref.shape, F32)
            dwp_ref[...] = jnp.zeros(dwp_ref.shape, F32)
            dg_ref[...] = jnp.zeros(dg_ref.shape, F32)

        d, xf, pg, gf = d_ref[...], x_ref[...], pg_ref[...], g_ref[...]
        r = _rstd(xf)
        n = xf * r
        dpgl = (d * pp_ref[...] * pg * (1.0 - pg)).astype(BF16)
        dwg_ref[...] += _tn((n * gf).astype(BF16), dpgl)
        dwp_ref[...] += _tn(p_ref[...].astype(BF16), (d * pg).astype(BF16))
        dxn, dgr = _norm_bwd(_nt(dpgl, w_ref[...]), n, r, gf)
        dx_ref[...] = d + dxn
        dg_ref[...] += jnp.sum(dgr, axis=0, keepdims=True)

    r_in, r_out, r_shape, r_scratch, r_args = _ride_args(ride)
    return pl.pallas_call(
        _riding(ride, body, 7, 4, grid), grid=grid, name="bwd_ple_ride" if ride else "bwd_ple",
        in_specs=[_row(tm, D)] * 4 + [pl.BlockSpec((None, tm, PLE), lambda i: (layer, i, 0)),
                                      _res((1, D), layer), _res((D, D), 0)] + r_in,
        out_specs=[_row(tm, D), _acc((D, D)), _acc((PLE, D)), _acc((1, D))] + r_out,
        out_shape=[SDS((T, D), F32), SDS((D, D), F32), SDS((PLE, D), F32), SDS((1, D), F32)] + r_shape,
        scratch_shapes=r_scratch,
        compiler_params=_cp("arbitrary"))(dx2, x1, pg, pp, p, g, wpg, *r_args)


def _bwd_merge(dx1, oa, ob, z, wa, wb, wo, tm, layer):
    T = dx1.shape[0]

    def body(d_ref, oa_ref, ob_ref, ag_ref, bg_ref, ma_ref, mb_ref, wa_ref, wb_ref, wo_ref,
             doa_ref, dob_ref, dag_ref, dbg_ref, dma_ref, dmb_ref, dsa_ref, dsb_ref, dwa_ref, dwb_ref, dwo_ref):
        @pl.when(pl.program_id(0) == 0)
        def _():
            dwa_ref[...] = jnp.zeros(dwa_ref.shape, F32)
            dwb_ref[...] = jnp.zeros(dwb_ref.shape, F32)
            dwo_ref[...] = jnp.zeros(dwo_ref.shape, F32)

        db = d_ref[...].astype(BF16)
        gated = []
        for o_ref, gate_ref, w_ref in ((oa_ref, ag_ref, wa_ref), (ob_ref, bg_ref, wb_ref)):
            raw, gate = o_ref[...], gate_ref[...]
            sg = _sig(gate)
            silu = gate * sg
            ob16 = (raw * silu).astype(BF16)
            gated.append((raw, gate, sg, silu, ob16, _nn(ob16, w_ref[...])))
        ua, ub = gated[0][5], gated[1][5]
        sa, sb = _sig(ma_ref[...]), _sig(mb_ref[...])
        dwo_ref[...] += _tn((sa * ua + sb * ub).astype(BF16), db)
        dy = _nt(db, wo_ref[...])
        dma_ref[...] = (dy * ua * sa * (1.0 - sa)).astype(BF16)
        dmb_ref[...] = (dy * ub * sb * (1.0 - sb)).astype(BF16)
        for (s, w_ref, do_ref, dgate_ref, dw_ref, ds_ref), (raw, gate, sg, silu, ob16, _) in zip((
                (sa, wa_ref, doa_ref, dag_ref, dwa_ref, dsa_ref),
                (sb, wb_ref, dob_ref, dbg_ref, dwb_ref, dsb_ref)), gated):
            du = (dy * s).astype(BF16)
            dw_ref[...] += _tn(ob16, du)
            do = _nt(du, w_ref[...])
            draw = do * silu
            do_ref[...] = draw.astype(BF16)
            dgate_ref[...] = (do * raw * (sg * (1.0 + gate * (1.0 - sg)))).astype(BF16)
            ds_ref[...] = jnp.sum((draw * raw).T.reshape(MLA_H, MLA_V, tm), axis=1)

    return pl.pallas_call(
        body, grid=(T // tm,), name="bwd_merge",
        in_specs=[_row(tm, D), _row(tm, 512), _row(tm, 512), _row(tm, 512, Z_AG // 512), _row(tm, 512, Z_BG // 512),
                  _row(tm, D, Z_MA // D), _row(tm, D, Z_MB // D),
                  _res((512, D), 0), _res((512, D), 0), _res((D, D), 0)],
        out_specs=[_row(tm, 512)] * 4 + [_row(tm, D)] * 2 + [pl.BlockSpec((MLA_H, tm), lambda i: (0, i))] * 2
        + [_acc((512, D)), _acc((512, D)), _acc((D, D))],
        out_shape=[SDS((T, 512), BF16), SDS((T, 512), BF16), SDS((T, 512), BF16), SDS((T, 512), BF16),
                   SDS((T, D), BF16), SDS((T, D), BF16), SDS((MLA_H, T), F32), SDS((MLA_H, T), F32),
                   SDS((512, D), F32), SDS((512, D), F32), SDS((D, D), F32)],
        compiler_params=_cp("arbitrary"))(dx1, oa, ob, z, z, z, z, wa, wb, wo)


def _mla_bwd(qf, kf, v, do, lse, dsum, nb, seq, tq, exchange=()):
    T = qf.shape[0]
    nq = seq // tq
    pw = 2 * LANES
    pairs = [(qi, ki) for ki in range(nq) for qi in range(ki, nq)]
    qi_tab = jnp.array([qk[0] for qk in pairs], jnp.int32)
    ki_tab = jnp.array([qk[1] for qk in pairs], jnp.int32)
    grid = (nb, MLA_H // 2, len(pairs))
    n_x = len(exchange)

    def body(qi_ref, ki_ref, q_ref, k_ref, v_ref, do_ref, lse_ref, dsum_ref, *rest):
        p_refs, (dq_ref, dk_ref, dv_ref), got_refs = rest[:n_x], rest[n_x:n_x + 3], rest[n_x + 3:2 * n_x + 3]
        (dk_s, dv_s, dqt_s), sems = rest[2 * n_x + 3:2 * n_x + 6], rest[2 * n_x + 6:]
        step_id = pl.program_id(2)
        qi, ki = qi_ref[step_id], ki_ref[step_id]
        if n_x:
            start, finish = _exchange_phases(p_refs, got_refs, *sems)
            at_first, _, at_last = _grid_ends(grid)
            pl.when(at_first)(start)

        @pl.when(step_id == 0)
        def _():
            dqt_s[...] = jnp.zeros(dqt_s.shape, F32)

        @pl.when(qi == ki)
        def _():
            dk_s[...] = jnp.zeros(dk_s.shape, F32)
            dv_s[...] = jnp.zeros(dv_s.shape, F32)

        def step(masked):
            if masked:
                keys = lax.broadcasted_iota(jnp.int32, (tq, tq), 0)
                queries = lax.broadcasted_iota(jnp.int32, (tq, tq), 1)
                mask = keys <= queries
            for j in range(2):
                wide = slice(LANES * j, LANES * (j + 1))
                sl = slice(MLA_V * j, MLA_V * (j + 1))
                q, k = q_ref[:, wide], k_ref[:, wide]
                dob = do_ref[:, sl].astype(BF16)
                s = _nt(k, q) * MLA_SCALE
                if masked:
                    s = jnp.where(mask, s, NEG)
                p = jnp.exp(s - lse_ref[j:j + 1, :])
                dv_s[:, sl] += _nn(p.astype(BF16), dob)
                ds = (p * (_nt(v_ref[:, sl], dob) - dsum_ref[j:j + 1, :]) * MLA_SCALE).astype(BF16)
                dk_s[:, wide] += _nn(ds, q)
                dqt_s[qi, wide, :] += _tn(k, ds)

        @pl.when(qi > ki)
        def _():
            step(False)

        @pl.when(qi == ki)
        def _():
            step(True)

        @pl.when(qi == nq - 1)
        def _():
            dk_ref[...] = dk_s[...]
            dv_ref[...] = dv_s[...]

        @pl.when(step_id == len(pairs) - 1)
        def _():
            for n in range(nq):
                dq_ref[tq * n:tq * (n + 1), :] = dqt_s[n].T

        if n_x:
            pl.when(at_last)(finish)

    qmap = lambda b, hp, s, qi_ref, ki_ref: (b * nq + qi_ref[s], hp)
    kmap = lambda b, hp, s, qi_ref, ki_ref: (b * nq + ki_ref[s], hp)
    stat = pl.BlockSpec((None, 2, tq), lambda b, hp, s, qi_ref, ki_ref: (hp, 0, b * nq + qi_ref[s]))
    return pl.pallas_call(
        body, name="mla_bwd_exchange" if n_x else "mla_bwd",
        grid_spec=pltpu.PrefetchScalarGridSpec(
            num_scalar_prefetch=2, grid=grid,
            in_specs=[pl.BlockSpec((tq, pw), qmap), pl.BlockSpec((tq, pw), kmap), pl.BlockSpec((tq, LANES), kmap),
                      pl.BlockSpec((tq, LANES), qmap), stat, stat] + [ANY] * n_x,
            out_specs=[pl.BlockSpec((seq, pw), lambda b, hp, s, qi_ref, ki_ref: (b, hp)),
                       pl.BlockSpec((tq, pw), kmap), pl.BlockSpec((tq, LANES), kmap)] + [ANY] * n_x,
            scratch_shapes=[pltpu.VMEM((tq, pw), F32), pltpu.VMEM((tq, LANES), F32), pltpu.VMEM((nq, pw, tq), F32)]
            + (_exchange_sems(n_x) if n_x else [])),
        out_shape=[SDS((T, QFW), F32), SDS((T, QFW), F32), SDS((T, MLA_H * MLA_V), F32)]
        + [SDS(a.shape, a.dtype) for a in exchange],
        compiler_params=_cp("arbitrary", "arbitrary", "arbitrary"))(qi_tab, ki_tab, qf, kf, v, do, lse, dsum, *exchange)


def _swa_bwd(sink, z, pos_col, pos_row, do, lse, dsum, nb, seq, layer, ride=None):
    T = z.shape[0]
    nblk = seq // BLK

    def body(sink_ref, q_ref, kvc_ref, kvp_ref, pcc_ref, pcp_ref, pr_ref, do_ref, lse_ref, dsum_ref,
             dq_ref, dkv_ref, dsink_ref):
        b, n = pl.program_id(0), pl.program_id(1)

        @pl.when((b == 0) & (n == 0))
        def _():
            dsink_ref[...] = jnp.zeros(dsink_ref.shape, F32)

        @pl.when(n == 0)
        def _():
            dkv_ref[...] = jnp.zeros(dkv_ref.shape, F32)

        kb, vb, scores = _swa_scores(n, q_ref, kvc_ref, kvp_ref, pcc_ref, pcp_ref, pr_ref)
        lane = lax.broadcasted_iota(jnp.int32, (1, LANES), 1)
        dsink = jnp.zeros((1, LANES), F32)
        dkv = [[None, None], [None, None]]
        dqs = []
        gsl = lambda h: slice(SWA_DH * (h // (SWA_H // SWA_KV)), SWA_DH * (h // (SWA_H // SWA_KV) + 1))
        qs, ss, dobs, dps = [], [], [], []
        for h in range(SWA_H):
            qh, s = scores(h)
            dob = do_ref[:, SWA_DH * h:SWA_DH * (h + 1)].astype(BF16)
            qs.append(qh)
            ss.append(s)
            dobs.append(dob)
            dps.append(_nt(vb[:, gsl(h)], dob))
        pbs, dss = [], []
        for h in range(SWA_H):
            lse, dsum = lse_ref[h:h + 1, :], dsum_ref[h:h + 1, :]
            p = jnp.exp(ss[h] - lse)
            pbs.append(p.astype(BF16))
            dss.append((p * (dps[h] - dsum) * SWA_SCALE).astype(BF16))
            dsk = jnp.sum(-jnp.exp(sink_ref[layer, h] - lse) * dsum, axis=1, keepdims=True)
            dsink = dsink + jnp.where(lane == h, dsk, 0.0)
        for h in range(SWA_H):
            g = h // (SWA_H // SWA_KV)
            dqs.append(_tn(kb[:, gsl(h)], dss[h]))
            dk, dv = _nn(dss[h], qs[h]), _nn(pbs[h], dobs[h])
            dkv[g][0] = dk if dkv[g][0] is None else dkv[g][0] + dk
            dkv[g][1] = dv if dkv[g][1] is None else dkv[g][1] + dv
        dq_ref[...] = jnp.concatenate(dqs, axis=0).T.astype(BF16)
        dsink_ref[...] += dsink
        upd = jnp.concatenate([dkv[0][0], dkv[1][0], dkv[0][1], dkv[1][1]], axis=1)
        dkv_ref[pl.ds(pl.multiple_of(n * BLK, BLK), BLK), :] += upd[BLK:]

        @pl.when(n > 0)
        def _():
            dkv_ref[pl.ds(pl.multiple_of((n - 1) * BLK, BLK), BLK), :] += upd[:BLK]

    r_in, r_out, r_shape, r_scratch, r_args = _ride_args(ride)
    return pl.pallas_call(
        _riding(ride, body, 10, 3, (nb, nblk)), grid=(nb, nblk), name="swa_bwd_ride" if ride else "swa_bwd",
        in_specs=_swa_specs(nblk) + [pl.BlockSpec((BLK, 512), lambda b, n: (b * nblk + n, 0))]
        + [pl.BlockSpec((SWA_H, BLK), lambda b, n: (0, b * nblk + n))] * 2 + r_in,
        out_specs=[pl.BlockSpec((BLK, 512), lambda b, n: (b * nblk + n, 0)),
                   pl.BlockSpec((seq, 2 * BLK), lambda b, n: (b, 0)),
                   pl.BlockSpec((1, LANES), lambda b, n: (0, 0))] + r_out,
        out_shape=[SDS((T, 512), BF16), SDS((T, 2 * BLK), F32), SDS((1, LANES), F32)] + r_shape,
        scratch_shapes=r_scratch,
        compiler_params=_cp("arbitrary", "arbitrary"))(sink, z, z, z, pos_col, pos_col, pos_row, do, lse, dsum,
                                                       *r_args)


def _bwd_prep(dq, dk, dv, z, gq, gkv, wq, wkv, tc, ts1, ts2, tm, layer):
    T = z.shape[0]

    def body(dq_ref, dk_ref, dv_ref, qd_ref, kvd_ref, gq_ref, gkv_ref, wq_ref, wkv_ref, c_ref, s1_ref, s2_ref,
             dqd_ref, dkvd_ref, dkr_ref, dwq_ref, dwkv_ref, dgq_ref, dgkv_ref, dqb_s, dkvb_s):
        @pl.when(pl.program_id(0) == 0)
        def _():
            for ref in (dwq_ref, dwkv_ref, dgq_ref, dgkv_ref):
                ref[...] = jnp.zeros(ref.shape, F32)

        c, s1, s2 = c_ref[...], s1_ref[...], s2_ref[...]
        lane = lax.broadcasted_iota(jnp.int32, (1, LANES), 1)
        rope_lanes = (lane >= MLA_NOPE) & (lane < MLA_QK)
        dkb = jnp.zeros((tm, LANES), F32)
        for h in range(MLA_H):
            sl = slice(LANES * h, LANES * (h + 1))
            dqb_s[:, sl] = _rope_t(dq_ref[:, sl], c, s1, s2).astype(BF16)
            dkh = dk_ref[:, sl]
            dkb = dkb + dkh
            dkvb_s[:, sl] = dkh.astype(BF16)
        dkvb_s[:, QFW:] = dv_ref[...].astype(BF16)
        dkr_ref[...] = _rope_t(jnp.where(rope_lanes, dkb, 0.0), c, s1, s2).astype(BF16)

        for (x_ref, g_ref, w_ref, d_s, dx_ref, dw_ref, dg_ref) in (
                (qd_ref, gq_ref, wq_ref, dqb_s, dqd_ref, dwq_ref, dgq_ref),
                (kvd_ref, gkv_ref, wkv_ref, dkvb_s, dkvd_ref, dwkv_ref, dgkv_ref)):
            xf, gf, db = x_ref[...], g_ref[...], d_s[...]
            r = _rstd(xf)
            n = xf * r
            dw_ref[...] += _tn((n * gf).astype(BF16), db)
            dx, dgr = _norm_bwd(_nt(db, w_ref[...]), n, r, gf)
            dx_ref[...] = dx.astype(BF16)
            dg_ref[...] += jnp.sum(dgr, axis=0, keepdims=True)

    return pl.pallas_call(
        body, grid=(T // tm,), name="bwd_prep",
        in_specs=[_row(tm, QFW), _row(tm, QFW), _row(tm, MLA_H * MLA_V),
                  _row(tm, QL, Z_QD // QL), _row(tm, KVL, Z_KVD // KVL),
                  _res((1, QL), layer), _res((1, KVL), layer), _res((QL, QFW), 0), _res((KVL, KVW), 0),
                  _row(tm, LANES), _row(tm, LANES), _row(tm, LANES)],
        out_specs=[_row(tm, QL), _row(tm, KVL), _row(tm, LANES),
                   _acc((QL, QFW)), _acc((KVL, KVW)), _acc((1, QL)), _acc((1, KVL))],
        out_shape=[SDS((T, QL), BF16), SDS((T, KVL), BF16), SDS((T, LANES), BF16),
                   SDS((QL, QFW), F32), SDS((KVL, KVW), F32), SDS((1, QL), F32), SDS((1, KVL), F32)],
        scratch_shapes=[pltpu.VMEM((tm, QFW), BF16), pltpu.VMEM((tm, KVW), BF16)],
        compiler_params=_cp("arbitrary"))(dq, dk, dv, z, z, gq, gkv, wq, wkv, tc, ts1, ts2)


def _bwd_in(pieces, x, g, dres, w, tm, layer, ride=None):
    T = x.shape[0]
    grid = (T // tm,)
    widths = [pc.shape[1] for pc in pieces]
    assert sum(widths) == ZW
    n_p = len(pieces)

    def body(*refs):
        p_refs, (x_ref, g_ref, r_ref, w_ref, dx_ref, dz_ref, dg_ref) = refs[:n_p], refs[n_p:]

        @pl.when(pl.program_id(0) == 0)
        def _():
            dg_ref[...] = jnp.zeros(dg_ref.shape, F32)

        off = 0
        for ref, wd in zip(p_refs, widths):
            dz_ref[:, off:off + wd] = ref[...].astype(BF16)
            off += wd
        xf, gf = x_ref[...], g_ref[...]
        r = _rstd(xf)
        n = xf * r
        dx, dgr = _norm_bwd(_nt(dz_ref[...], w_ref[...]), n, r, gf)
        dx_ref[...] = r_ref[...] + dx
        dg_ref[...] += jnp.sum(dgr, axis=0, keepdims=True)

    r_in, r_out, r_shape, r_scratch, r_args = _ride_args(ride)
    return pl.pallas_call(
        _riding(ride, body, n_p + 4, 3, grid), grid=grid, name="bwd_in_ride" if ride else "bwd_in",
        in_specs=[_row(tm, wd) for wd in widths] + [_row(tm, D), _res((1, D), layer), _row(tm, D),
                                                    _res((D, ZW), 0)] + r_in,
        out_specs=[_row(tm, D), _row(tm, ZW), _acc((1, D))] + r_out,
        out_shape=[SDS((T, D), F32), SDS((T, ZW), BF16), SDS((1, D), F32)] + r_shape,
        scratch_shapes=r_scratch,
        compiler_params=_cp("arbitrary"))(*pieces, x, g, dres, w, *r_args)


def _wgrad_in(hb, dzb, tm):
    T = hb.shape[0]
    half = ZW // 2

    def body(h_ref, dz_ref, dw_ref):
        @pl.when(pl.program_id(1) == 0)
        def _():
            dw_ref[...] = jnp.zeros(dw_ref.shape, F32)

        dw_ref[...] += _tn(h_ref[...], dz_ref[...])

    return pl.pallas_call(
        body, grid=(2, T // tm), name="wgrad_in",
        in_specs=[pl.BlockSpec((tm, D), lambda j, t: (t, 0)), pl.BlockSpec((tm, half), lambda j, t: (t, j))],
        out_specs=pl.BlockSpec((D, half), lambda j, t: (0, j)),
        out_shape=SDS((D, ZW), F32),
        compiler_params=_cp("parallel", "arbitrary"))(hb, dzb)


IN_PIECES = ((0, 512, Z_AQ), (512, 128, Z_AK), (640, 128, Z_AV), (768, 512, Z_AG), (1280, 256, Z_QD),
             (1536, 128, Z_KVD), (1664, MLA_ROPE, Z_KR + MLA_NOPE), (1696, 512, Z_BG), (2208, 1024, Z_MA),
             (3232, 1024, Z_MB))
WIDE_W = IN_W // N_DEV


def _column_runs():
    runs = []
    for start, width, kstart in IN_PIECES:
        col = start
        while col < start + width:
            dev = col // WIDE_W
            stop = min(start + width, (dev + 1) * WIDE_W)
            runs.append((dev, col - dev * WIDE_W, stop - col, kstart + col - start))
            col = stop
    return runs


def _win_layout(blocks, tm):
    runs = _column_runs()

    def body(g_ref, o_ref):
        o_ref[:, Z_KR:Z_KR + LANES] = jnp.zeros((tm, LANES), o_ref.dtype)
        for dev, lo, n, k in runs:
            o_ref[:, k:k + n] = g_ref[dev, :, lo:lo + n]

    return pl.pallas_call(
        body, grid=(D // tm,), name="win_layout",
        in_specs=[pl.BlockSpec((N_DEV, None, tm, WIDE_W), lambda i: (0, 0, i, 0))],
        out_specs=pl.BlockSpec((None, tm, ZW), lambda i: (0, i, 0)),
        out_shape=SDS((1, D, ZW), blocks.dtype),
        compiler_params=_cp("parallel"))(blocks)


def _win_grad_layout(dw, tm):
    runs = _column_runs()

    def body(g_ref, o_ref):
        for dev, lo, n, k in runs:
            o_ref[dev, :, lo:lo + n] = g_ref[:, k:k + n]

    return pl.pallas_call(
        body, grid=(D // tm,), name="win_grad_layout",
        in_specs=[_row(tm, ZW)],
        out_specs=pl.BlockSpec((N_DEV, None, tm, WIDE_W), lambda i: (0, 0, i, 0)),
        out_shape=SDS((N_DEV, 1, D, WIDE_W), F32),
        compiler_params=_cp("parallel"))(dw)


def _wuq_to_kernel(w):
    w = w.reshape(w.shape[:-1] + (MLA_H, MLA_QK))
    w = jnp.pad(w, [(0, 0)] * (w.ndim - 1) + [(0, LANES - MLA_QK)])
    return w.reshape(w.shape[:-2] + (QFW,))


def _wuq_from_kernel(g):
    g = g.reshape(g.shape[:-1] + (MLA_H, LANES))[..., :MLA_QK]
    return g.reshape(g.shape[:-2] + (MLA_H * MLA_QK,))


def _wukv_to_kernel(w):
    w = w.reshape(w.shape[:-1] + (MLA_H, MLA_NOPE + MLA_V))
    k = jnp.pad(w[..., :MLA_NOPE], [(0, 0)] * (w.ndim - 1) + [(0, LANES - MLA_NOPE)])
    v = w[..., MLA_NOPE:]
    return jnp.concatenate([k.reshape(k.shape[:-2] + (QFW,)), v.reshape(v.shape[:-2] + (MLA_H * MLA_V,))], axis=-1)


def _wukv_from_kernel(g):
    k = g[..., :QFW].reshape(g.shape[:-1] + (MLA_H, LANES))[..., :MLA_NOPE]
    v = g[..., QFW:].reshape(g.shape[:-1] + (MLA_H, MLA_V))
    kv = jnp.concatenate([k, v], axis=-1)
    return kv.reshape(kv.shape[:-2] + (MLA_H * (MLA_NOPE + MLA_V),))


def _rope_tables(pos):
    half = MLA_ROPE // 2
    inv = 10000.0 ** (-jnp.arange(0, MLA_ROPE, 2, dtype=F32) / MLA_ROPE)
    ang = pos.astype(F32)[:, None] * inv
    cos, sin = jnp.cos(ang), jnp.sin(ang)
    one = jnp.ones((pos.shape[0], MLA_NOPE), F32)
    zero = lambda n: jnp.zeros((pos.shape[0], n), F32)
    tc = jnp.concatenate([one, cos, cos, one[:, :LANES - MLA_QK]], axis=1)
    ts1 = jnp.concatenate([zero(MLA_NOPE + half), sin, zero(LANES - MLA_QK)], axis=1)
    ts2 = jnp.concatenate([zero(MLA_NOPE), -sin, zero(LANES - MLA_NOPE - half)], axis=1)
    return tc, ts1, ts2


def _local_step(x, p, positions, loss_target, small, wts, plan=None):
    nb, seq, _ = x.shape
    T = nb * seq
    tm = min(256, T)
    tq = min(512, seq)
    xf = x.reshape(T, D)
    pos = positions.reshape(T)
    posf = pos.astype(F32)
    pos_col, pos_row = posf.reshape(T, 1), posf.reshape(T // BLK, 1, BLK)
    tc, ts1, ts2 = _rope_tables(pos)

    wts, sm = list(wts), small
    pl_in = p.reshape(DEPTH, T, PLE)
    saved = []
    for i in range(DEPTH):
        riding = plan is not None and i == 0
        w = wts[i]
        z, hb, *got = _fwd_in(xf, sm["g_mix"], w["w_in"], tm, i,
                              ride=_Ride("gather", plan["rows0"]) if riding else None)
        if riding:
            w = wts[0] = dict(w, **plan["row_weights"](got[0]))
        oa, lse_a = _swa_fwd(sm["sink"], z, pos_col, pos_row, nb, seq, i)
        qf, kf, v = _fwd_prep(z, sm["g_q"], sm["g_kv"], w["w_uq"], w["w_ukv"], tc, ts1, ts2, tm, i)
        ob, lse_b, *got = _mla_fwd(qf, kf, v, nb, seq, tq, gather=plan["blocks1"] if riding else ())
        if riding:
            wts.append(dict(w_in=plan["w_in"](got[0]), **plan["row_weights"](got[1])))
        x1 = _fwd_merge(xf, oa, ob, z, w["w_br_a"], w["w_br_b"], w["w_out"], tm, i)
        x2, pg, pp = _fwd_ple(x1, pl_in, sm["g_ple"], w["w_ple_gate"], w["w_ple_proj"], tm, i)
        saved.append(dict(x=xf, z=z, hb=hb, oa=oa, lse_a=lse_a, qf=qf, kf=kf, v=v, ob=ob, lse_b=lse_b,
                          x1=x1, pg=pg, pp=pp))
        xf = x2

    dx, dg_final, loss = _loss_head(xf, small["g_final"], loss_target.reshape(T, D), tm)

    grads = [None] * DEPTH
    exchanged = {}
    for i in reversed(range(DEPTH)):
        riding = plan is not None and i == 0
        sv, w = saved[i], wts[i]
        pay = plan["payload"](grads[1]) if riding else []
        dx1, dwpg, dwpp, dg_ple, *got = _bwd_ple(dx, sv["x1"], sv["pg"], sv["pp"], pl_in, sm["g_ple"],
                                                 w["w_ple_gate"], tm, i, ride=_Ride("swap", pay) if riding else None)
        doa, dob, dag, dbg, dma, dmb, dsum_a, dsum_b, dwa, dwb, dwo = _bwd_merge(
            dx1, sv["oa"], sv["ob"], sv["z"], w["w_br_a"], w["w_br_b"], w["w_out"], tm, i)
        dq_b, dk_b, dv_b, *exchanged["layer1"] = _mla_bwd(
            sv["qf"], sv["kf"], sv["v"], dob, sv["lse_b"], dsum_b.reshape(MLA_H // 2, 2, T), nb, seq, tq,
            exchange=plan["add"](pay, got) if riding else ())
        dqd, dkvd, dkr, dwq, dwkv, dgq, dgkv = _bwd_prep(dq_b, dk_b, dv_b, sv["z"], sm["g_q"], sm["g_kv"],
                                                         w["w_uq"], w["w_ukv"], tc, ts1, ts2, tm, i)
        g = dict(w_uq=_wuq_from_kernel(dwq), w_ukv=_wukv_from_kernel(dwkv), w_br_a=dwa, w_br_b=dwb, w_out=dwo,
                 w_ple_gate=dwpg, w_ple_proj=dwpp)
        pay = [plan["rows_payload"](g)] if riding else []
        dq_a, dkv_a, dsink, *got = _swa_bwd(sm["sink"], sv["z"], pos_col, pos_row, doa, sv["lse_a"], dsum_a, nb, seq,
                                            i, ride=_Ride("swap", pay) if riding else None)
        dx, dzb, dg_mix, *exchanged["rows0"] = _bwd_in(
            [dma, dmb, dq_a, dag, dbg, dqd, dkv_a, dkvd, dkr], sv["x"], sm["g_mix"], dx1, w["w_in"], tm, i,
            ride=_Ride("exchange", plan["add"](pay, got)) if riding else None)
        g.update(g_mix=dg_mix[0], w_in=_wgrad_in(sv["hb"], dzb, tm), sink=dsink[0, :SWA_H], g_q=dgq[0], g_kv=dgkv[0],
                 g_ple=dg_ple[0])
        grads[i] = g
    return loss, dx.reshape(nb, seq, D), grads, dg_final[0], exchanged


def _row_weights(rows):
    blocks = _unpack_rows(rows)
    out = {n: _join(n, blocks[n]) for n, _ in ROWS_PIECES}
    out.update(w_uq=_wuq_to_kernel(out["w_uq"]), w_ukv=_wukv_to_kernel(out["w_ukv"]))
    return out


def _kernel_weights(gathered):
    return dict(w_in=_win_layout(gathered[0], 256), **_row_weights(gathered[1]))


def _small_params(g_mix, sink, g_q, g_kv, g_ple, g_final):
    return dict(g_mix=g_mix[:, None], sink=sink, g_q=g_q[:, None], g_kv=g_kv[:, None], g_ple=g_ple[:, None],
                g_final=g_final[None])


UQ_W = MLA_H * MLA_QK // N_DEV
ROWS_PIECES = (("w_uq", QL), ("w_ukv", KVL), ("w_br_a", 512), ("w_br_b", 512), ("w_out", D), ("w_ple_gate", D),
               ("w_ple_proj", PLE))
SMALL = (("g_mix", (DEPTH, D)), ("sink", (DEPTH, SWA_H)), ("g_q", (DEPTH, QL)), ("g_kv", (DEPTH, KVL)),
         ("g_ple", (DEPTH, D)), ("g_final", (D,)))
VEC_ROWS = 48
ROWS_N = sum(r for _, r in ROWS_PIECES)
WIDE_TILE, ROWS_TILE = 256, ROWS_N // 2


def _to_rows(name, a):
    if name == "w_uq":
        a = jnp.pad(a, [(0, 0)] * (a.ndim - 1) + [(0, LANES - UQ_W)])
    return a.reshape(a.shape[:-2] + (-1, LANES))


def _from_rows(name, r):
    if name in ("w_out", "w_ple_gate"):
        return r.reshape(r.shape[:-2] + (D // N_DEV, D))
    return r[..., :UQ_W] if name == "w_uq" else r


def _pack_rows(blocks):
    return jnp.concatenate([_to_rows(n, blocks[n]) for n, _ in ROWS_PIECES], axis=-2)


def _unpack_rows(rows):
    blocks, off = {}, 0
    for n, r in ROWS_PIECES:
        blocks[n] = _from_rows(n, rows[..., off:off + r, :])
        off += r
    return blocks


def _pack_vec(vectors, loss=None):
    parts = [vectors[n].reshape(-1) for n, _ in SMALL] + ([] if loss is None else [loss.reshape(1)])
    vec = jnp.concatenate(parts)
    return jnp.pad(vec, (0, VEC_ROWS * LANES - vec.shape[0])).reshape(1, VEC_ROWS, LANES)


def _unpack_vec(vec):
    vec = vec.reshape(-1)
    vectors, off = {}, 0
    for n, shp in SMALL:
        size = 1
        for s in shp:
            size *= s
        vectors[n] = vec[off:off + size].reshape(shp)
        off += size
    return vectors, vec[off]


def _join(name, blocks):
    if name in ("w_out", "w_ple_gate"):
        return jnp.moveaxis(blocks, 0, 1).reshape(blocks.shape[1], -1, blocks.shape[-1])
    return jnp.moveaxis(blocks, 0, 2).reshape(blocks.shape[1], blocks.shape[2], -1)


def _split(name, full):
    if name in ("w_out", "w_ple_gate"):
        return jnp.moveaxis(full.reshape(full.shape[0], N_DEV, -1, full.shape[-1]), 1, 0)
    return jnp.moveaxis(full.reshape(full.shape[0], full.shape[1], N_DEV, -1), 2, 0)


MESH_ID = pl.DeviceIdType.MESH
ANY = pl.BlockSpec(memory_space=pl.ANY)


def _place():
    return lax.axis_index("x"), lax.axis_index("y"), lax.axis_index("c")


def _all_gather(blocks):
    n = len(blocks)

    def body(*refs):
        start, forward, finish = _gather_phases(refs[:n], refs[n:2 * n], *refs[2 * n:])
        start()
        forward()
        finish()

    return pl.pallas_call(
        body, name="all_gather_weights", out_shape=_gather_out(blocks),
        in_specs=[ANY] * n, out_specs=[ANY] * n, scratch_shapes=_gather_sems(n))(*blocks)


def _gather_out(blocks):
    return [SDS((N_DEV,) + b.shape, b.dtype) for b in blocks]


def _gather_sems(n):
    return [pltpu.SemaphoreType.DMA((7 * n,)), pltpu.SemaphoreType.DMA((7 * n,)), pltpu.SemaphoreType.DMA((n,))]


def _gather_phases(x_refs, out_refs, send_sems, recv_sems, local_sems):
    n = len(x_refs)
    x, y, c = _place()
    me, sibling = (x, y, c), (x, y, 1 - c)
    chips = [(1 - x, y), (x, 1 - y), (1 - x, 1 - y)]

    def slot(a, px, py, pc):
        return out_refs[a].at[4 * px + 2 * py + pc]

    def copy(a, k, blk, to, src=None):
        return pltpu.make_async_remote_copy(
            src_ref=slot(a, *blk) if src is None else src, dst_ref=slot(a, *blk),
            send_sem=send_sems.at[7 * a + k], recv_sem=recv_sems.at[7 * a + k], device_id=to,
            device_id_type=MESH_ID)

    def mine():
        return [pltpu.make_async_copy(x_refs[a], slot(a, *me), local_sems.at[a]) for a in range(n)]

    def first():
        out = []
        for a in range(n):
            out += [copy(a, 0, me, sibling, src=x_refs[a])]
            out += [copy(a, 1 + j, me, (*chip, c), src=x_refs[a]) for j, chip in enumerate(chips)]
        return out

    def passed():
        return [copy(a, 4 + j, (*chip, c), sibling) for j, chip in enumerate(chips) for a in range(n)]

    def start():
        for cp in mine() + first():
            cp.start()

    def forward():
        for j, chip in enumerate(chips):
            for a in range(n):
                copy(a, 1 + j, (*chip, c), me).wait_recv()
                copy(a, 4 + j, (*chip, c), sibling).start()

    def finish():
        for a in range(n):
            copy(a, 0, sibling, me).wait_recv()
            for j, chip in enumerate(chips):
                copy(a, 4 + j, (*chip, 1 - c), me).wait_recv()
        for cp in first() + passed():
            cp.wait_send()
        for cp in mine():
            cp.wait()

    return start, forward, finish


def _swap_sibling(arrs):
    n = len(arrs)

    def body(*refs):
        start, finish = _swap_phases(refs[:n], refs[n:2 * n], *refs[2 * n:])
        start()
        finish()

    return pl.pallas_call(
        body, name="swap_sibling", out_shape=[SDS((a.shape[0],) + a.shape[2:], a.dtype) for a in arrs],
        in_specs=[ANY] * n, out_specs=[ANY] * n, scratch_shapes=_swap_sems(n))(*arrs)


def _swap_sems(n):
    return [pltpu.SemaphoreType.DMA((n,)), pltpu.SemaphoreType.DMA((n,))]


def _swap_phases(a_refs, out_refs, send_sems, recv_sems):
    x, y, c = _place()

    def copies():
        return [pltpu.make_async_remote_copy(
            src_ref=a_refs[a].at[:, 1 - c], dst_ref=out_refs[a], send_sem=send_sems.at[a], recv_sem=recv_sems.at[a],
            device_id=(x, y, 1 - c), device_id_type=MESH_ID) for a in range(len(a_refs))]

    def start():
        for cp in copies():
            cp.start()

    def finish():
        for cp in copies():
            cp.wait()

    return start, finish


def _exchange_chips(arrs):
    n = len(arrs)

    def body(*refs):
        start, finish = _exchange_phases(refs[:n], refs[n:2 * n], *refs[2 * n:])
        start()
        finish()

    return pl.pallas_call(
        body, name="exchange_chips", out_shape=[SDS(a.shape, a.dtype) for a in arrs],
        in_specs=[ANY] * n, out_specs=[ANY] * n, scratch_shapes=_exchange_sems(n))(*arrs)


def _exchange_sems(n):
    return [pltpu.SemaphoreType.DMA((3 * n,)), pltpu.SemaphoreType.DMA((3 * n,)), pltpu.SemaphoreType.DMA((n,))]


def _exchange_phases(p_refs, out_refs, send_sems, recv_sems, local_sems):
    n = len(p_refs)
    x, y, c = _place()
    mine = 2 * x + y
    peers = [(1 - x, y), (x, 1 - y), (1 - x, 1 - y)]

    def local():
        return [pltpu.make_async_copy(p_refs[a].at[mine], out_refs[a].at[mine], local_sems.at[a]) for a in range(n)]

    def copy(a, j, src_chip, dst_chip):
        px, py = peers[j]
        return pltpu.make_async_remote_copy(
            src_ref=p_refs[a].at[src_chip], dst_ref=out_refs[a].at[dst_chip], send_sem=send_sems.at[3 * a + j],
            recv_sem=recv_sems.at[3 * a + j], device_id=(px, py, c), device_id_type=MESH_ID)

    def sends():
        return [copy(a, j, 2 * px + py, mine) for a in range(n) for j, (px, py) in enumerate(peers)]

    def start():
        for cp in local() + sends():
            cp.start()

    def finish():
        for a in range(n):
            for j, (px, py) in enumerate(peers):
                copy(a, j, mine, 2 * px + py).wait_recv()
        for cp in sends():
            cp.wait_send()
        for cp in local():
            cp.wait()

    return start, finish


def _add_mine(g, recv, core, tile, dtype):
    _, _, lead, rows, width = g.shape

    def body(c_ref, g_ref, r_ref, o_ref):
        o_ref[...] = (g_ref[...] + r_ref[...]).astype(dtype)

    spec = pl.BlockSpec((None, None, tile, width), lambda k, l, i, c_ref: (k, l, i, 0))
    return pl.pallas_call(
        body, name="add_sibling", out_shape=SDS(recv.shape, dtype),
        grid_spec=pltpu.PrefetchScalarGridSpec(
            num_scalar_prefetch=1, grid=(g.shape[0], lead, rows // tile),
            in_specs=[pl.BlockSpec((None, None, None, tile, width), lambda k, l, i, c_ref: (k, c_ref[0], l, i, 0)),
                      spec],
            out_specs=spec),
        compiler_params=_cp("parallel", "parallel", "parallel"))(core, g, recv)


def _sum_adamw(parts, w, m, v, tile):
    lead, rows, width = w.shape
    last = rows // tile - 1

    def body(*refs):
        p_refs, (w_ref, m_ref, v_ref, g_ref, d_ref, nm_ref, nv_ref) = refs[:lead], refs[lead:]
        for layer in range(lead):
            @pl.when(pl.program_id(0) == layer)
            def _(p_ref=p_refs[layer]):
                g = ((p_ref[0].astype(F32) + p_ref[1].astype(F32)) + p_ref[2].astype(F32)) + p_ref[3].astype(F32)
                nm = ADAM_B1 * m_ref[...] + (1.0 - ADAM_B1) * g
                nv = ADAM_B2 * v_ref[...] + (1.0 - ADAM_B2) * jnp.square(g)
                m_hat = nm / (1.0 - ADAM_B1 ** ADAM_STEP)
                v_hat = nv / (1.0 - ADAM_B2 ** ADAM_STEP)
                g_ref[...] = g
                nm_ref[...] = nm
                nv_ref[...] = nv
                d_ref[...] = -ADAM_LR * (m_hat / (jnp.sqrt(v_hat) + ADAM_EPS) + ADAM_WD * w_ref[...])

    pspec = lambda layer: pl.BlockSpec(
        (4, None, tile, width),
        lambda l, i: (0, 0, jnp.where(l == layer, i, jnp.where(l > layer, last, 0)), 0))
    spec = pl.BlockSpec((None, tile, width), lambda l, i: (l, i, 0))
    return pl.pallas_call(
        body, grid=(lead, rows // tile), name="sum_adamw",
        in_specs=[pspec(layer) for layer in range(lead)] + [spec, spec, spec],
        out_specs=[spec] * 4, out_shape=[SDS((lead, rows, width), F32)] * 4,
        compiler_params=_cp("arbitrary", "arbitrary"))(*parts, w, m, v)


def kernel(x, p, positions, g_mix, w_in, sink, g_q, w_uq, g_kv, w_ukv, w_br_a, w_br_b, w_out, g_ple, w_ple_gate, w_ple_proj, g_final, loss_target, m_g_mix, m_w_in, m_sink, m_g_q, m_w_uq, m_g_kv, m_w_ukv, m_w_br_a, m_w_br_b, m_w_out, m_g_ple, m_w_ple_gate, m_w_ple_proj, m_g_final, v_g_mix, v_w_in, v_sink, v_g_q, v_w_uq, v_g_kv, v_w_ukv, v_w_br_a, v_w_br_b, v_w_out, v_g_ple, v_w_ple_gate, v_w_ple_proj, v_g_final):
    weights = dict(g_mix=g_mix, w_in=w_in, sink=sink, g_q=g_q, w_uq=w_uq, g_kv=g_kv, w_ukv=w_ukv, w_br_a=w_br_a,
                   w_br_b=w_br_b, w_out=w_out, g_ple=g_ple, w_ple_gate=w_ple_gate, w_ple_proj=w_ple_proj,
                   g_final=g_final)
    mom1 = dict(g_mix=m_g_mix, w_in=m_w_in, sink=m_sink, g_q=m_g_q, w_uq=m_w_uq, g_kv=m_g_kv, w_ukv=m_w_ukv,
                w_br_a=m_w_br_a, w_br_b=m_w_br_b, w_out=m_w_out, g_ple=m_g_ple, w_ple_gate=m_w_ple_gate,
                w_ple_proj=m_w_ple_proj, g_final=m_g_final)
    mom2 = dict(g_mix=v_g_mix, w_in=v_w_in, sink=v_sink, g_q=v_g_q, w_uq=v_w_uq, g_kv=v_g_kv, w_ukv=v_w_ukv,
                w_br_a=v_w_br_a, w_br_b=v_w_br_b, w_out=v_w_out, g_ple=v_g_ple, w_ple_gate=v_w_ple_gate,
                w_ple_proj=v_w_ple_proj, g_final=v_g_final)
    assert DEPTH == 2
    wide = lambda d: d["w_in"]
    rows = lambda d: _pack_rows(d)
    core = lax.axis_index("c").astype(jnp.int32).reshape(1)

    w16 = [wide(weights).astype(BF16), rows(weights).astype(BF16)]
    wts0 = dict(w_in=_win_layout(_all_gather([w16[0][:1]])[0], 256))
    small = _small_params(g_mix, sink, g_q, g_kv, g_ple, g_final)

    def wide_payload(g):
        return _win_grad_layout(g["w_in"], 256).reshape(N_DEV // 2, 2, 1, D, WIDE_W)

    def rows_payload(g):
        return _pack_rows({n: _split(n, g[n][None]) for n, _ in ROWS_PIECES}).reshape(N_DEV // 2, 2, 1, ROWS_N, LANES)

    def add(pay, got):
        tiles = {D: (WIDE_TILE, BF16), ROWS_N: (ROWS_TILE, BF16), VEC_ROWS: (VEC_ROWS, F32)}
        return [_add_mine(a, b, core, *tiles[a.shape[-2]]) for a, b in zip(pay, got)]

    plan = dict(rows0=[w16[1][:1]], blocks1=[a[1:] for a in w16], w_in=lambda blocks: _win_layout(blocks, 256),
                row_weights=_row_weights, payload=lambda g: [wide_payload(g), rows_payload(g)],
                rows_payload=rows_payload, add=add)
    loss, grad_x, grads, dg_final, rode = _local_step(x, p, positions, loss_target, small, [wts0], plan)

    vectors = {n: jnp.stack([grads[i][n] for i in range(DEPTH)]) for n, _ in SMALL[:-1]}
    vectors["g_final"] = dg_final
    pay = [wide_payload(grads[0]),
           jnp.broadcast_to(_pack_vec(vectors, loss[0, 0]), (N_DEV // 2, 2, 1, VEC_ROWS, LANES))]
    parts_wide0, parts_vec = _exchange_chips(add(pay, _swap_sibling(pay)))
    out_wide = _sum_adamw([parts_wide0, rode["layer1"][0]], wide(weights), wide(mom1), wide(mom2), WIDE_TILE)
    out_rows = _sum_adamw([rode["rows0"][0], rode["layer1"][1]], rows(weights), rows(mom1), rows(mom2), ROWS_TILE)
    out_vec = _sum_adamw([parts_vec], _pack_vec(weights), _pack_vec(mom1), _pack_vec(mom2), VEC_ROWS)

    outs = []
    for ow, orow, ovec in zip(out_wide, out_rows, out_vec):
        named = _unpack_rows(orow)
        named.update(_unpack_vec(ovec)[0])
        named["w_in"] = ow
        outs += [named[n] for n in weights]
    loss = _unpack_vec(out_vec[0])[1]
    return (loss, grad_x, *outs)
```

```python
import functools

import jax
import jax.numpy as jnp
from jax import lax
from jax.experimental import pallas as pl
from jax.experimental.pallas import tpu as pltpu

F32, BF16 = jnp.float32, jnp.bfloat16
SDS = jax.ShapeDtypeStruct

D = 1024
DEPTH = 2
PLE = 256
BLK = 128
EPS = 1e-6
NEG = -1e30
SWA_H, SWA_KV, SWA_DH = 8, 2, 64
MLA_H, MLA_NOPE, MLA_ROPE, MLA_V = 8, 64, 32, 64
MLA_QK = MLA_NOPE + MLA_ROPE
QL, KVL = 256, 128
IN_W = 4256
N_DEV = 8

V7X_VMEM_BYTES = 64 * 1024 * 1024
LANES = 128
VMEM_LIMIT = V7X_VMEM_BYTES * 7 // 8

ZW = 4352
Z_MA, Z_MB, Z_AQ, Z_AG, Z_BG, Z_QD, Z_AK, Z_AV, Z_KVD, Z_KR = 0, 1024, 2048, 2560, 3072, 3584, 3840, 3968, 4096, 4224
QFW = MLA_H * LANES
KVW = QFW + MLA_H * MLA_V
MLA_SCALE = MLA_QK ** -0.5
SWA_SCALE = SWA_DH ** -0.5
ROLL_UP, ROLL_DOWN = MLA_ROPE // 2, LANES - MLA_ROPE // 2

ADAM_LR, ADAM_B1, ADAM_B2, ADAM_EPS, ADAM_WD, ADAM_STEP = 0.001, 0.9, 0.999, 1e-08, 0.01, 10

FLAT_W = 1024


def _cp(*sem):
    return pltpu.CompilerParams(dimension_semantics=sem, vmem_limit_bytes=VMEM_LIMIT)


def _row(tm, w, col=0):
    return pl.BlockSpec((tm, w), lambda i: (i, col))


def _res(shape, layer=None):
    if layer is None:
        return pl.BlockSpec(shape, lambda *_: (0,) * len(shape), pipeline_mode=pl.Buffered(1))
    return pl.BlockSpec((None,) + shape, lambda *_: (layer,) + (0,) * len(shape), pipeline_mode=pl.Buffered(1))


def _acc(shape):
    return pl.BlockSpec(shape, lambda *_: (0,) * len(shape))


def _rstd(xf):
    return lax.rsqrt(jnp.mean(xf * xf, axis=-1, keepdims=True) + EPS)


def _norm_bwd(dh, n, r, g):
    dn = dh * g
    return r * (dn - n * jnp.mean(dn * n, axis=-1, keepdims=True)), dh * n


def _nt(a, b):
    return lax.dot_general(a, b, (((1,), (1,)), ((), ())), preferred_element_type=F32)


def _tn(a, b):
    return lax.dot_general(a, b, (((0,), (0,)), ((), ())), preferred_element_type=F32)


def _nn(a, b):
    return jnp.dot(a, b, preferred_element_type=F32)


def _sig(x):
    return jax.nn.sigmoid(x)


def _rope(t, c, s1, s2):
    return t * c + pltpu.roll(t, ROLL_UP, 1) * s1 + pltpu.roll(t, ROLL_DOWN, 1) * s2


def _rope_t(d, c, s1, s2):
    return d * c + pltpu.roll(d * s1, ROLL_DOWN, 1) + pltpu.roll(d * s2, ROLL_UP, 1)


def _fwd_in(x, g, w, tm, layer, ride=None):
    T = x.shape[0]
    grid = (T // tm,)

    def body(x_ref, g_ref, w_ref, z_ref, h_ref):
        xf = x_ref[...]
        h = ((xf * _rstd(xf)) * g_ref[...]).astype(BF16)
        h_ref[...] = h
        z_ref[...] = _nn(h, w_ref[...])

    r_in, r_out, r_shape, r_scratch, r_args = _ride_args(ride)
    return pl.pallas_call(
        _riding(ride, body, 3, 2, grid), grid=grid, name="fwd_in_ride" if ride else "fwd_in",
        in_specs=[_row(tm, D), _res((1, D), layer), _res((D, ZW), 0)] + r_in,
        out_specs=[_row(tm, ZW), _row(tm, D)] + r_out,
        out_shape=[SDS((T, ZW), F32), SDS((T, D), BF16)] + r_shape, scratch_shapes=r_scratch,
        compiler_params=_cp("arbitrary"))(x, g, w, *r_args)


def _fwd_prep(z, gq, gkv, wq, wkv, tc, ts1, ts2, tm, layer):
    T = z.shape[0]

    def body(qd_ref, kvd_ref, kr_ref, gq_ref, gkv_ref, wq_ref, wkv_ref, c_ref, s1_ref, s2_ref, q_ref, k_ref, v_ref):
        qd, kvd = qd_ref[...], kvd_ref[...]
        hq = ((qd * _rstd(qd)) * gq_ref[...]).astype(BF16)
        hkv = ((kvd * _rstd(kvd)) * gkv_ref[...]).astype(BF16)
        qf = _nn(hq, wq_ref[...])
        kvf = _nn(hkv, wkv_ref[...])
        c, s1, s2 = c_ref[...], s1_ref[...], s2_ref[...]
        krb = _rope(kr_ref[...], c, s1, s2)
        for h in range(MLA_H):
            sl = slice(LANES * h, LANES * (h + 1))
            q_ref[:, sl] = _rope(qf[:, sl], c, s1, s2).astype(BF16)
            k_ref[:, sl] = (kvf[:, sl] + krb).astype(BF16)
        v_ref[...] = kvf[:, QFW:].astype(BF16)

    return pl.pallas_call(
        body, grid=(T // tm,), name="fwd_prep",
        in_specs=[_row(tm, QL, Z_QD // QL), _row(tm, KVL, Z_KVD // KVL), _row(tm, LANES, Z_KR // LANES),
                  _res((1, QL), layer), _res((1, KVL), layer), _res((QL, QFW), 0), _res((KVL, KVW), 0),
                  _row(tm, LANES), _row(tm, LANES), _row(tm, LANES)],
        out_specs=[_row(tm, QFW), _row(tm, QFW), _row(tm, MLA_H * MLA_V)],
        out_shape=[SDS((T, QFW), BF16), SDS((T, QFW), BF16), SDS((T, MLA_H * MLA_V), BF16)],
        compiler_params=_cp("parallel"))(z, z, z, gq, gkv, wq, wkv, tc, ts1, ts2)


def _grid_ends(grid):
    ids = [pl.program_id(a) for a in range(len(grid))]
    inner_first = functools.reduce(jnp.logical_and, [i == 0 for i in ids[1:]], True)
    last = functools.reduce(jnp.logical_and, [i == g - 1 for i, g in zip(ids, grid)])
    return (ids[0] == 0) & inner_first, (ids[0] == grid[0] // 2) & inner_first, last


class _Ride:
    def __init__(self, kind, arrays):
        self.kind, self.arrays, self.n = kind, list(arrays), len(arrays)

    def out_shape(self):
        if self.kind == "gather":
            return _gather_out(self.arrays)
        if self.kind == "swap":
            return [SDS((a.shape[0],) + a.shape[2:], a.dtype) for a in self.arrays]
        return [SDS(a.shape, a.dtype) for a in self.arrays]

    def sems(self):
        if self.kind == "gather":
            return _gather_sems(self.n)
        if self.kind == "swap":
            return _swap_sems(self.n)
        return _exchange_sems(self.n)

    def phases(self, in_refs, out_refs, *sems):
        if self.kind == "gather":
            return _gather_phases(in_refs, out_refs, *sems)
        start, finish = (_swap_phases if self.kind == "swap" else _exchange_phases)(in_refs, out_refs, *sems)
        return start, None, finish


def _riding(ride, body, n_in, n_out, grid):
    if ride is None:
        return body
    n, n_sems = ride.n, len(ride.sems())

    def wrapped(*refs):
        ins, r_in = refs[:n_in], refs[n_in:n_in + n]
        outs, r_out = refs[n_in + n:n_in + n + n_out], refs[n_in + n + n_out:n_in + 2 * n + n_out]
        rest = refs[n_in + 2 * n + n_out:]
        scratch, sems = rest[:len(rest) - n_sems], rest[len(rest) - n_sems:]
        start, middle, finish = ride.phases(r_in, r_out, *sems)
        at_first, at_middle, at_last = _grid_ends(grid)
        pl.when(at_first)(start)
        if middle is not None:
            pl.when(at_middle)(middle)
        body(*ins, *outs, *scratch)
        pl.when(at_last)(finish)

    return wrapped


def _ride_args(ride):
    if ride is None:
        return [], [], [], [], []
    return [ANY] * ride.n, [ANY] * ride.n, ride.out_shape(), ride.sems(), ride.arrays


def _mla_fwd(qf, kf, v, nb, seq, tq, gather=()):
    T = qf.shape[0]
    nq = seq // tq
    pw = 2 * LANES
    pairs = [(qi, ki) for qi in range(nq) for ki in range(qi + 1)]
    qi_tab = jnp.array([qk[0] for qk in pairs], jnp.int32)
    ki_tab = jnp.array([qk[1] for qk in pairs], jnp.int32)
    grid = (nb, MLA_H // 2, len(pairs))
    n_g = len(gather)

    def body(qi_ref, ki_ref, q_ref, k_ref, v_ref, *rest):
        x_refs, (o_ref, lse_ref), got_refs = rest[:n_g], rest[n_g:n_g + 2], rest[n_g + 2:2 * n_g + 2]
        (m_s, l_s, acc_s), sems = rest[2 * n_g + 2:2 * n_g + 5], rest[2 * n_g + 5:]
        qi, ki = qi_ref[pl.program_id(2)], ki_ref[pl.program_id(2)]
        if n_g:
            start, forward, finish = _gather_phases(x_refs, got_refs, *sems)
            at_first, at_middle, at_last = _grid_ends(grid)
            pl.when(at_first)(start)
            pl.when(at_middle)(forward)

        @pl.when(ki == 0)
        def _():
            m_s[...] = jnp.full(m_s.shape, NEG, F32)
            l_s[...] = jnp.zeros(l_s.shape, F32)
            acc_s[...] = jnp.zeros(acc_s.shape, F32)

        def step(masked):
            if masked:
                keys = lax.broadcasted_iota(jnp.int32, (tq, tq), 0)
                queries = lax.broadcasted_iota(jnp.int32, (tq, tq), 1)
                mask = keys <= queries
            ss = []
            for j in range(2):
                wide = slice(LANES * j, LANES * (j + 1))
                s = _nt(k_ref[:, wide], q_ref[:, wide]) * MLA_SCALE
                ss.append(jnp.where(mask, s, NEG) if masked else s)
            ps, alphas = [], []
            for j in range(2):
                m_prev = m_s[j]
                m_new = jnp.maximum(m_prev, jnp.max(ss[j], axis=0, keepdims=True))
                alpha = jnp.exp(m_prev - m_new)
                p = jnp.exp(ss[j] - m_new)
                l_s[j] = alpha * l_s[j] + jnp.sum(p, axis=0, keepdims=True)
                m_s[j] = m_new
                ps.append(p.astype(BF16))
                alphas.append(alpha)
            for j in range(2):
                rows = slice(MLA_V * j, MLA_V * (j + 1))
                acc_s[rows, :] = alphas[j] * acc_s[rows, :] + _tn(v_ref[:, rows], ps[j])

        @pl.when(ki < qi)
        def _():
            step(False)

        @pl.when(ki == qi)
        def _():
            step(True)
            for j in range(2):
                rows = slice(MLA_V * j, MLA_V * (j + 1))
                acc_s[rows, :] = acc_s[rows, :] / l_s[j]
                lse_ref[j:j + 1, :] = m_s[j] + jnp.log(l_s[j])
            o_ref[...] = acc_s[...].T

        if n_g:
            pl.when(at_last)(finish)

    q_map = lambda b, hp, s, qi_ref, ki_ref: (b * nq + qi_ref[s], hp)
    kv_map = lambda b, hp, s, qi_ref, ki_ref: (b * nq + ki_ref[s], hp)
    return pl.pallas_call(
        body, name="mla_fwd_gather" if n_g else "mla_fwd",
        grid_spec=pltpu.PrefetchScalarGridSpec(
            num_scalar_prefetch=2, grid=grid,
            in_specs=[pl.BlockSpec((tq, pw), q_map), pl.BlockSpec((tq, pw), kv_map),
                      pl.BlockSpec((tq, LANES), kv_map)] + [ANY] * n_g,
            out_specs=[pl.BlockSpec((tq, LANES), q_map),
                       pl.BlockSpec((None, 2, tq), lambda b, hp, s, qi_ref, ki_ref: (hp, 0, b * nq + qi_ref[s]))]
            + [ANY] * n_g,
            scratch_shapes=[pltpu.VMEM((2, 1, tq), F32), pltpu.VMEM((2, 1, tq), F32), pltpu.VMEM((LANES, tq), F32)]
            + (_gather_sems(n_g) if n_g else [])),
        out_shape=[SDS((T, MLA_H * MLA_V), F32), SDS((MLA_H // 2, 2, T), F32)] + _gather_out(gather),
        compiler_params=_cp("arbitrary", "arbitrary", "arbitrary"))(qi_tab, ki_tab, qf, kf, v, *gather)


def _swa_specs(nblk):
    cur = lambda b, n: (b * nblk + n, 0)
    prev = lambda b, n: (b * nblk + jnp.maximum(n - 1, 0), 0)
    kvc = Z_AK // (2 * BLK)
    return [pl.BlockSpec(memory_space=pltpu.SMEM),
            pl.BlockSpec((BLK, 512), lambda b, n: (b * nblk + n, Z_AQ // 512)),
            pl.BlockSpec((BLK, 2 * BLK), lambda b, n: (b * nblk + n, kvc)),
            pl.BlockSpec((BLK, 2 * BLK), lambda b, n: (b * nblk + jnp.maximum(n - 1, 0), kvc)),
            pl.BlockSpec((BLK, 1), cur),
            pl.BlockSpec((BLK, 1), prev),
            pl.BlockSpec((1, 1, BLK), lambda b, n: (b * nblk + n, 0, 0))]


def _swa_scores(n, q_ref, kvc_ref, kvp_ref, pcc_ref, pcp_ref, pr_ref):
    kv = jnp.concatenate([kvp_ref[...], kvc_ref[...]], axis=0)
    kb, vb = kv[:, :BLK].astype(BF16), kv[:, BLK:].astype(BF16)
    dist = pr_ref[0] - jnp.concatenate([pcp_ref[...], pcc_ref[...]], axis=0)
    key = lax.broadcasted_iota(jnp.int32, (2 * BLK, BLK), 0)
    qry = lax.broadcasted_iota(jnp.int32, (2 * BLK, BLK), 1)
    valid = (key > qry) & (key <= qry + BLK) & ((key >= BLK) | (n > 0))

    def scores(h):
        g = h // (SWA_H // SWA_KV)
        qh = q_ref[:, SWA_DH * h:SWA_DH * (h + 1)].astype(BF16)
        s = _nt(kb[:, SWA_DH * g:SWA_DH * (g + 1)], qh) * SWA_SCALE - (2.0 ** -(h + 1)) * dist
        return qh, jnp.where(valid, s, NEG)

    return kb, vb, scores


def _swa_fwd(sink, z, pos_col, pos_row, nb, seq, layer):
    T = z.shape[0]
    nblk = seq // BLK

    def body(sink_ref, q_ref, kvc_ref, kvp_ref, pcc_ref, pcp_ref, pr_ref, o_ref, lse_ref):
        kb, vb, scores = _swa_scores(pl.program_id(1), q_ref, kvc_ref, kvp_ref, pcc_ref, pcp_ref, pr_ref)
        ss = [scores(h)[1] for h in range(SWA_H)]
        es, dens = [], []
        for h in range(SWA_H):
            sk = sink_ref[layer, h]
            m = jnp.maximum(jnp.max(ss[h], axis=0, keepdims=True), sk)
            e = jnp.exp(ss[h] - m)
            den = jnp.sum(e, axis=0, keepdims=True) + jnp.exp(sk - m)
            lse_ref[h:h + 1, :] = m + jnp.log(den)
            es.append(e.astype(BF16))
            dens.append(den)
        outs = []
        for h in range(SWA_H):
            g = h // (SWA_H // SWA_KV)
            outs.append(_tn(vb[:, SWA_DH * g:SWA_DH * (g + 1)], es[h]) / dens[h])
        o_ref[...] = jnp.concatenate(outs, axis=0).T

    return pl.pallas_call(
        body, grid=(nb, nblk), name="swa_fwd",
        in_specs=_swa_specs(nblk),
        out_specs=[pl.BlockSpec((BLK, 512), lambda b, n: (b * nblk + n, 0)),
                   pl.BlockSpec((SWA_H, BLK), lambda b, n: (0, b * nblk + n))],
        out_shape=[SDS((T, 512), F32), SDS((SWA_H, T), F32)],
        compiler_params=_cp("parallel", "parallel"))(sink, z, z, z, pos_col, pos_col, pos_row)


def _fwd_merge(x, oa, ob, z, wa, wb, wo, tm, layer):
    T = x.shape[0]

    def body(x_ref, oa_ref, ob_ref, ag_ref, bg_ref, ma_ref, mb_ref, wa_ref, wb_ref, wo_ref, x1_ref):
        ag, bg = ag_ref[...], bg_ref[...]
        ua = _nn((oa_ref[...] * (ag * _sig(ag))).astype(BF16), wa_ref[...])
        ub = _nn((ob_ref[...] * (bg * _sig(bg))).astype(BF16), wb_ref[...])
        y = _sig(ma_ref[...]) * ua + _sig(mb_ref[...]) * ub
        x1_ref[...] = x_ref[...] + _nn(y.astype(BF16), wo_ref[...])

    return pl.pallas_call(
        body, grid=(T // tm,), name="fwd_merge",
        in_specs=[_row(tm, D), _row(tm, 512), _row(tm, 512), _row(tm, 512, Z_AG // 512), _row(tm, 512, Z_BG // 512),
                  _row(tm, D, Z_MA // D), _row(tm, D, Z_MB // D),
                  _res((512, D), 0), _res((512, D), 0), _res((D, D), 0)],
        out_specs=_row(tm, D),
        out_shape=SDS((T, D), F32),
        compiler_params=_cp("parallel"))(x, oa, ob, z, z, z, z, wa, wb, wo)


def _fwd_ple(x1, p, g, wpg, wpp, tm, layer):
    T = x1.shape[0]

    def body(x_ref, p_ref, g_ref, wpg_ref, wpp_ref, x2_ref, pg_ref, pp_ref):
        xf = x_ref[...]
        h1 = ((xf * _rstd(xf)) * g_ref[...]).astype(BF16)
        pg = _sig(_nn(h1, wpg_ref[...]))
        pp = _nn(p_ref[...].astype(BF16), wpp_ref[...])
        pg_ref[...] = pg
        pp_ref[...] = pp
        x2_ref[...] = xf + pg * pp

    return pl.pallas_call(
        body, grid=(T // tm,), name="fwd_ple",
        in_specs=[_row(tm, D), pl.BlockSpec((None, tm, PLE), lambda i: (layer, i, 0)),
                  _res((1, D), layer), _res((D, D), 0), _res((PLE, D), 0)],
        out_specs=[_row(tm, D)] * 3,
        out_shape=[SDS((T, D), F32)] * 3,
        compiler_params=_cp("parallel"))(x1, p, g, wpg, wpp)


def _loss_head(x, g, tgt, tm):
    T = x.shape[0]

    def body(x_ref, g_ref, t_ref, dx_ref, dg_ref, loss_ref):
        @pl.when(pl.program_id(0) == 0)
        def _():
            dg_ref[...] = jnp.zeros(dg_ref.shape, F32)
            loss_ref[...] = jnp.zeros(loss_ref.shape, F32)

        xf, gf = x_ref[...], g_ref[...]
        r = _rstd(xf)
        n = xf * r
        err = n * gf - t_ref[...]
        loss_ref[...] += 0.5 * jnp.sum(jnp.mean(err * err, axis=-1, keepdims=True), axis=0, keepdims=True)
        dx, dgr = _norm_bwd(err * (1.0 / D), n, r, gf)
        dx_ref[...] = dx
        dg_ref[...] += jnp.sum(dgr, axis=0, keepdims=True)

    return pl.pallas_call(
        body, grid=(T // tm,), name="loss_head",
        in_specs=[_row(tm, D), _res((1, D)), _row(tm, D)],
        out_specs=[_row(tm, D), _acc((1, D)), _acc((1, LANES))],
        out_shape=[SDS((T, D), F32), SDS((1, D), F32), SDS((1, LANES), F32)],
        compiler_params=_cp("arbitrary"))(x, g, tgt)


def _bwd_ple(dx2, x1, pg, pp, p, g, wpg, tm, layer, ride=None):
    T = x1.shape[0]
    grid = (T // tm,)

    def body(d_ref, x_ref, pg_ref, pp_ref, p_ref, g_ref, w_ref, dx_ref, dwg_ref, dwp_ref, dg_ref):
        @pl.when(pl.program_id(0) == 0)
        def _():
            dwg_ref[...] = jnp.zeros(dwg_ref.shape, F32)
            dwp_ref[...] = jnp.zeros(dwp_ref.shape, F32)
            dg_ref[...] = jnp.zeros(dg_ref.shape, F32)

        d, xf, pg, gf = d_ref[...], x_ref[...], pg_ref[...], g_ref[...]
        r = _rstd(xf)
        n = xf * r
        dpgl = (d * pp_ref[...] * pg * (1.0 - pg)).astype(BF16)
        dwg_ref[...] += _tn((n * gf).astype(BF16), dpgl)
        dwp_ref[...] += _tn(p_ref[...].astype(BF16), (d * pg).astype(BF16))
        dxn, dgr = _norm_bwd(_nt(dpgl, w_ref[...]), n, r, gf)
        dx_ref[...] = d + dxn
        dg_ref[...] += jnp.sum(dgr, axis=0, keepdims=True)

    r_in, r_out, r_shape, r_scratch, r_args = _ride_args(ride)
    return pl.pallas_call(
        _riding(ride, body, 7, 4, grid), grid=grid, name="bwd_ple_ride" if ride else "bwd_ple",
        in_specs=[_row(tm, D)] * 4 + [pl.BlockSpec((None, tm, PLE), lambda i: (layer, i, 0)),
                                      _res((1, D), layer), _res((D, D), 0)] + r_in,
        out_specs=[_row(tm, D), _acc((D, D)), _acc((PLE, D)), _acc((1, D))] + r_out,
        out_shape=[SDS((T, D), F32), SDS((D, D), F32), SDS((PLE, D), F32), SDS((1, D), F32)] + r_shape,
        scratch_shapes=r_scratch,
        compiler_params=_cp("arbitrary"))(dx2, x1, pg, pp, p, g, wpg, *r_args)


def _bwd_merge(dx1, oa, ob, z, wa, wb, wo, tm, layer):
    T = dx1.shape[0]

    def body(d_ref, oa_ref, ob_ref, ag_ref, bg_ref, ma_ref, mb_ref, wa_ref, wb_ref, wo_ref,
             doa_ref, dob_ref, dag_ref, dbg_ref, dma_ref, dmb_ref, dsa_ref, dsb_ref, dwa_ref, dwb_ref, dwo_ref):
        @pl.when(pl.program_id(0) == 0)
        def _():
            dwa_ref[...] = jnp.zeros(dwa_ref.shape, F32)
            dwb_ref[...] = jnp.zeros(dwb_ref.shape, F32)
            dwo_ref[...] = jnp.zeros(dwo_ref.shape, F32)

        db = d_ref[...].astype(BF16)
        gated = []
        for o_ref, gate_ref, w_ref in ((oa_ref, ag_ref, wa_ref), (ob_ref, bg_ref, wb_ref)):
            raw, gate = o_ref[...], gate_ref[...]
            sg = _sig(gate)
            silu = gate * sg
            ob16 = (raw * silu).astype(BF16)
            gated.append((raw, gate, sg, silu, ob16, _nn(ob16, w_ref[...])))
        ua, ub = gated[0][5], gated[1][5]
        sa, sb = _sig(ma_ref[...]), _sig(mb_ref[...])
        dwo_ref[...] += _tn((sa * ua + sb * ub).astype(BF16), db)
        dy = _nt(db, wo_ref[...])
        dma_ref[...] = (dy * ua * sa * (1.0 - sa)).astype(BF16)
        dmb_ref[...] = (dy * ub * sb * (1.0 - sb)).astype(BF16)
        for (s, w_ref, do_ref, dgate_ref, dw_ref, ds_ref), (raw, gate, sg, silu, ob16, _) in zip((
                (sa, wa_ref, doa_ref, dag_ref, dwa_ref, dsa_ref),
                (sb, wb_ref, dob_ref, dbg_ref, dwb_ref, dsb_ref)), gated):
            du = (dy * s).astype(BF16)
            dw_ref[...] += _tn(ob16, du)
            do = _nt(du, w_ref[...])
            draw = do * silu
            do_ref[...] = draw.astype(BF16)
            dgate_ref[...] = (do * raw * (sg * (1.0 + gate * (1.0 - sg)))).astype(BF16)
            ds_ref[...] = jnp.sum((draw * raw).T.reshape(MLA_H, MLA_V, tm), axis=1)

    return pl.pallas_call(
        body, grid=(T // tm,), name="bwd_merge",
        in_specs=[_row(tm, D), _row(tm, 512), _row(tm, 512), _row(tm, 512, Z_AG // 512), _row(tm, 512, Z_BG // 512),
                  _row(tm, D, Z_MA // D), _row(tm, D, Z_MB // D),
                  _res((512, D), 0), _res((512, D), 0), _res((D, D), 0)],
        out_specs=[_row(tm, 512)] * 4 + [_row(tm, D)] * 2 + [pl.BlockSpec((MLA_H, tm), lambda i: (0, i))] * 2
        + [_acc((512, D)), _acc((512, D)), _acc((D, D))],
        out_shape=[SDS((T, 512), BF16), SDS((T, 512), BF16), SDS((T, 512), BF16), SDS((T, 512), BF16),
                   SDS((T, D), BF16), SDS((T, D), BF16), SDS((MLA_H, T), F32), SDS((MLA_H, T), F32),
                   SDS((512, D), F32), SDS((512, D), F32), SDS((D, D), F32)],
        compiler_params=_cp("arbitrary"))(dx1, oa, ob, z, z, z, z, wa, wb, wo)


def _mla_bwd(qf, kf, v, do, lse, dsum, nb, seq, tq, exchange=()):
    T = qf.shape[0]
    nq = seq // tq
    pw = 2 * LANES
    pairs = [(qi, ki) for ki in range(nq) for qi in range(ki, nq)]
    qi_tab = jnp.array([qk[0] for qk in pairs], jnp.int32)
    ki_tab = jnp.array([qk[1] for qk in pairs], jnp.int32)
    grid = (nb, MLA_H // 2, len(pairs))
    n_x = len(exchange)

    def body(qi_ref, ki_ref, q_ref, k_ref, v_ref, do_ref, lse_ref, dsum_ref, *rest):
        p_refs, (dq_ref, dk_ref, dv_ref), got_refs = rest[:n_x], rest[n_x:n_x + 3], rest[n_x + 3:2 * n_x + 3]
        (dk_s, dv_s, dqt_s), sems = rest[2 * n_x + 3:2 * n_x + 6], rest[2 * n_x + 6:]
        step_id = pl.program_id(2)
        qi, ki = qi_ref[step_id], ki_ref[step_id]
        if n_x:
            start, finish = _exchange_phases(p_refs, got_refs, *sems)
            at_first, _, at_last = _grid_ends(grid)
            pl.when(at_first)(start)

        @pl.when(step_id == 0)
        def _():
            dqt_s[...] = jnp.zeros(dqt_s.shape, F32)

        @pl.when(qi == ki)
        def _():
            dk_s[...] = jnp.zeros(dk_s.shape, F32)
            dv_s[...] = jnp.zeros(dv_s.shape, F32)

        def step(masked):
            if masked:
                keys = lax.broadcasted_iota(jnp.int32, (tq, tq), 0)
                queries = lax.broadcasted_iota(jnp.int32, (tq, tq), 1)
                mask = keys <= queries
            for j in range(2):
                wide = slice(LANES * j, LANES * (j + 1))
                sl = slice(MLA_V * j, MLA_V * (j + 1))
                q, k = q_ref[:, wide], k_ref[:, wide]
                dob = do_ref[:, sl].astype(BF16)
                s = _nt(k, q) * MLA_SCALE
                if masked:
                    s = jnp.where(mask, s, NEG)
                p = jnp.exp(s - lse_ref[j:j + 1, :])
                dv_s[:, sl] += _nn(p.astype(BF16), dob)
                ds = (p * (_nt(v_ref[:, sl], dob) - dsum_ref[j:j + 1, :]) * MLA_SCALE).astype(BF16)
                dk_s[:, wide] += _nn(ds, q)
                dqt_s[qi, wide, :] += _tn(k, ds)

        @pl.when(qi > ki)
        def _():
            step(False)

        @pl.when(qi == ki)
        def _():
            step(True)

        @pl.when(qi == nq - 1)
        def _():
            dk_ref[...] = dk_s[...]
            dv_ref[...] = dv_s[...]

        @pl.when(step_id == len(pairs) - 1)
        def _():
            for n in range(nq):
                dq_ref[tq * n:tq * (n + 1), :] = dqt_s[n].T

        if n_x:
            pl.when(at_last)(finish)

    qmap = lambda b, hp, s, qi_ref, ki_ref: (b * nq + qi_ref[s], hp)
    kmap = lambda b, hp, s, qi_ref, ki_ref: (b * nq + ki_ref[s], hp)
    stat = pl.BlockSpec((None, 2, tq), lambda b, hp, s, qi_ref, ki_ref: (hp, 0, b * nq + qi_ref[s]))
    return pl.pallas_call(
        body, name="mla_bwd_exchange" if n_x else "mla_bwd",
        grid_spec=pltpu.PrefetchScalarGridSpec(
            num_scalar_prefetch=2, grid=grid,
            in_specs=[pl.BlockSpec((tq, pw), qmap), pl.BlockSpec((tq, pw), kmap), pl.BlockSpec((tq, LANES), kmap),
                      pl.BlockSpec((tq, LANES), qmap), stat, stat] + [ANY] * n_x,
            out_specs=[pl.BlockSpec((seq, pw), lambda b, hp, s, qi_ref, ki_ref: (b, hp)),
                       pl.BlockSpec((tq, pw), kmap), pl.BlockSpec((tq, LANES), kmap)] + [ANY] * n_x,
            scratch_shapes=[pltpu.VMEM((tq, pw), F32), pltpu.VMEM((tq, LANES), F32), pltpu.VMEM((nq, pw, tq), F32)]
            + (_exchange_sems(n_x) if n_x else [])),
        out_shape=[SDS((T, QFW), F32), SDS((T, QFW), F32), SDS((T, MLA_H * MLA_V), F32)]
        + [SDS(a.shape, a.dtype) for a in exchange],
        compiler_params=_cp("arbitrary", "arbitrary", "arbitrary"))(qi_tab, ki_tab, qf, kf, v, do, lse, dsum, *exchange)


def _swa_bwd(sink, z, pos_col, pos_row, do, lse, dsum, nb, seq, layer, ride=None):
    T = z.shape[0]
    nblk = seq // BLK

    def body(sink_ref, q_ref, kvc_ref, kvp_ref, pcc_ref, pcp_ref, pr_ref, do_ref, lse_ref, dsum_ref,
             dq_ref, dkv_ref, dsink_ref):
        b, n = pl.program_id(0), pl.program_id(1)

        @pl.when((b == 0) & (n == 0))
        def _():
            dsink_ref[...] = jnp.zeros(dsink_ref.shape, F32)

        @pl.when(n == 0)
        def _():
            dkv_ref[...] = jnp.zeros(dkv_ref.shape, F32)

        kb, vb, scores = _swa_scores(n, q_ref, kvc_ref, kvp_ref, pcc_ref, pcp_ref, pr_ref)
        lane = lax.broadcasted_iota(jnp.int32, (1, LANES), 1)
        dsink = jnp.zeros((1, LANES), F32)
        dkv = [[None, None], [None, None]]
        dqs = []
        gsl = lambda h: slice(SWA_DH * (h // (SWA_H // SWA_KV)), SWA_DH * (h // (SWA_H // SWA_KV) + 1))
        qs, ss, dobs, dps = [], [], [], []
        for h in range(SWA_H):
            qh, s = scores(h)
            dob = do_ref[:, SWA_DH * h:SWA_DH * (h + 1)].astype(BF16)
            qs.append(qh)
            ss.append(s)
            dobs.append(dob)
            dps.append(_nt(vb[:, gsl(h)], dob))
        pbs, dss = [], []
        for h in range(SWA_H):
            lse, dsum = lse_ref[h:h + 1, :], dsum_ref[h:h + 1, :]
            p = jnp.exp(ss[h] - lse)
            pbs.append(p.astype(BF16))
            dss.append((p * (dps[h] - dsum) * SWA_SCALE).astype(BF16))
            dsk = jnp.sum(-jnp.exp(sink_ref[layer, h] - lse) * dsum, axis=1, keepdims=True)
            dsink = dsink + jnp.where(lane == h, dsk, 0.0)
        for h in range(SWA_H):
            g = h // (SWA_H // SWA_KV)
            dqs.append(_tn(kb[:, gsl(h)], dss[h]))
            dk, dv = _nn(dss[h], qs[h]), _nn(pbs[h], dobs[h])
            dkv[g][0] = dk if dkv[g][0] is None else dkv[g][0] + dk
            dkv[g][1] = dv if dkv[g][1] is None else dkv[g][1] + dv
        dq_ref[...] = jnp.concatenate(dqs, axis=0).T.astype(BF16)
        dsink_ref[...] += dsink
        upd = jnp.concatenate([dkv[0][0], dkv[1][0], dkv[0][1], dkv[1][1]], axis=1)
        dkv_ref[pl.ds(pl.multiple_of(n * BLK, BLK), BLK), :] += upd[BLK:]

        @pl.when(n > 0)
        def _():
            dkv_ref[pl.ds(pl.multiple_of((n - 1) * BLK, BLK), BLK), :] += upd[:BLK]

    r_in, r_out, r_shape, r_scratch, r_args = _ride_args(ride)
    return pl.pallas_call(
        _riding(ride, body, 10, 3, (nb, nblk)), grid=(nb, nblk), name="swa_bwd_ride" if ride else "swa_bwd",
        in_specs=_swa_specs(nblk) + [pl.BlockSpec((BLK, 512), lambda b, n: (b * nblk + n, 0))]
        + [pl.BlockSpec((SWA_H, BLK), lambda b, n: (0, b * nblk + n))] * 2 + r_in,
        out_specs=[pl.BlockSpec((BLK, 512), lambda b, n: (b * nblk + n, 0)),
                   pl.BlockSpec((seq, 2 * BLK), lambda b, n: (b, 0)),
                   pl.BlockSpec((1, LANES), lambda b, n: (0, 0))] + r_out,
        out_shape=[SDS((T, 512), BF16), SDS((T, 2 * BLK), F32), SDS((1, LANES), F32)] + r_shape,
        scratch_shapes=r_scratch,
        compiler_params=_cp("arbitrary", "arbitrary"))(sink, z, z, z, pos_col, pos_col, pos_row, do, lse, dsum,
                                                       *r_args)


def _bwd_prep(dq, dk, dv, z, gq, gkv, wq, wkv, tc, ts1, ts2, tm, layer):
    T = z.shape[0]

    def body(dq_ref, dk_ref, dv_ref, qd_ref, kvd_ref, gq_ref, gkv_ref, wq_ref, wkv_ref, c_ref, s1_ref, s2_ref,
             dqd_ref, dkvd_ref, dkr_ref, dwq_ref, dwkv_ref, dgq_ref, dgkv_ref, dqb_s, dkvb_s):
        @pl.when(pl.program_id(0) == 0)
        def _():
            for ref in (dwq_ref, dwkv_ref, dgq_ref, dgkv_ref):
                ref[...] = jnp.zeros(ref.shape, F32)

        c, s1, s2 = c_ref[...], s1_ref[...], s2_ref[...]
        lane = lax.broadcasted_iota(jnp.int32, (1, LANES), 1)
        rope_lanes = (lane >= MLA_NOPE) & (lane < MLA_QK)
        dkb = jnp.zeros((tm, LANES), F32)
        for h in range(MLA_H):
            sl = slice(LANES * h, LANES * (h + 1))
            dqb_s[:, sl] = _rope_t(dq_ref[:, sl], c, s1, s2).astype(BF16)
            dkh = dk_ref[:, sl]
            dkb = dkb + dkh
            dkvb_s[:, sl] = dkh.astype(BF16)
        dkvb_s[:, QFW:] = dv_ref[...].astype(BF16)
        dkr_ref[...] = _rope_t(jnp.where(rope_lanes, dkb, 0.0), c, s1, s2).astype(BF16)

        for (x_ref, g_ref, w_ref, d_s, dx_ref, dw_ref, dg_ref) in (
                (qd_ref, gq_ref, wq_ref, dqb_s, dqd_ref, dwq_ref, dgq_ref),
                (kvd_ref, gkv_ref, wkv_ref, dkvb_s, dkvd_ref, dwkv_ref, dgkv_ref)):
            xf, gf, db = x_ref[...], g_ref[...], d_s[...]
            r = _rstd(xf)
            n = xf * r
            dw_ref[...] += _tn((n * gf).astype(BF16), db)
            dx, dgr = _norm_bwd(_nt(db, w_ref[...]), n, r, gf)
            dx_ref[...] = dx.astype(BF16)
            dg_ref[...] += jnp.sum(dgr, axis=0, keepdims=True)

    return pl.pallas_call(
        body, grid=(T // tm,), name="bwd_prep",
        in_specs=[_row(tm, QFW), _row(tm, QFW), _row(tm, MLA_H * MLA_V),
                  _row(tm, QL, Z_QD // QL), _row(tm, KVL, Z_KVD // KVL),
                  _res((1, QL), layer), _res((1, KVL), layer), _res((QL, QFW), 0), _res((KVL, KVW), 0),
                  _row(tm, LANES), _row(tm, LANES), _row(tm, LANES)],
        out_specs=[_row(tm, QL), _row(tm, KVL), _row(tm, LANES),
                   _acc((QL, QFW)), _acc((KVL, KVW)), _acc((1, QL)), _acc((1, KVL))],
        out_shape=[SDS((T, QL), BF16), SDS((T, KVL), BF16), SDS((T, LANES), BF16),
                   SDS((QL, QFW), F32), SDS((KVL, KVW), F32), SDS((1, QL), F32), SDS((1, KVL), F32)],
        scratch_shapes=[pltpu.VMEM((tm, QFW), BF16), pltpu.VMEM((tm, KVW), BF16)],
        compiler_params=_cp("arbitrary"))(dq, dk, dv, z, z, gq, gkv, wq, wkv, tc, ts1, ts2)


def _bwd_in(pieces, x, g, dres, w, tm, layer):
    T = x.shape[0]
    grid = (T // tm,)
    widths = [pc.shape[1] for pc in pieces]
    assert sum(widths) == ZW
    n_p = len(pieces)

    def body(*refs):
        p_refs, (x_ref, g_ref, r_ref, w_ref, dx_ref, dz_ref, dg_ref) = refs[:n_p], refs[n_p:]

        @pl.when(pl.program_id(0) == 0)
        def _():
            dg_ref[...] = jnp.zeros(dg_ref.shape, F32)

        off = 0
        for ref, wd in zip(p_refs, widths):
            dz_ref[:, off:off + wd] = ref[...].astype(BF16)
            off += wd
        xf, gf = x_ref[...], g_ref[...]
        r = _rstd(xf)
        n = xf * r
        dx, dgr = _norm_bwd(_nt(dz_ref[...], w_ref[...]), n, r, gf)
        dx_ref[...] = r_ref[...] + dx
        dg_ref[...] += jnp.sum(dgr, axis=0, keepdims=True)

    return pl.pallas_call(
        body, grid=grid, name="bwd_in",
        in_specs=[_row(tm, wd) for wd in widths] + [_row(tm, D), _res((1, D), layer), _row(tm, D),
                                                    _res((D, ZW), 0)],
        out_specs=[_row(tm, D), _row(tm, ZW), _acc((1, D))],
        out_shape=[SDS((T, D), F32), SDS((T, ZW), BF16), SDS((1, D), F32)],
        compiler_params=_cp("arbitrary"))(*pieces, x, g, dres, w)


def _wgrad_in(hb, dzb, tm, ride=None):
    T = hb.shape[0]
    half = ZW // 2
    grid = (2, T // tm)

    def body(h_ref, dz_ref, dw_ref):
        @pl.when(pl.program_id(1) == 0)
        def _():
            dw_ref[...] = jnp.zeros(dw_ref.shape, F32)

        dw_ref[...] += _tn(h_ref[...], dz_ref[...])

    r_in, r_out, r_shape, r_scratch, r_args = _ride_args(ride)
    out = pl.pallas_call(
        _riding(ride, body, 2, 1, grid), grid=grid, name="wgrad_in_ride" if ride else "wgrad_in",
        in_specs=[pl.BlockSpec((tm, D), lambda j, t: (t, 0)), pl.BlockSpec((tm, half), lambda j, t: (t, j))] + r_in,
        out_specs=[pl.BlockSpec((D, half), lambda j, t: (0, j))] + r_out,
        out_shape=[SDS((D, ZW), F32)] + r_shape, scratch_shapes=r_scratch,
        compiler_params=_cp("arbitrary", "arbitrary"))(hb, dzb, *r_args)
    return out


IN_PIECES = ((0, 512, Z_AQ), (512, 128, Z_AK), (640, 128, Z_AV), (768, 512, Z_AG), (1280, 256, Z_QD),
             (1536, 128, Z_KVD), (1664, MLA_ROPE, Z_KR + MLA_NOPE), (1696, 512, Z_BG), (2208, 1024, Z_MA),
             (3232, 1024, Z_MB))
WIDE_W = IN_W // N_DEV


def _column_runs():
    runs = []
    for start, width, kstart in IN_PIECES:
        col = start
        while col < start + width:
            dev = col // WIDE_W
            stop = min(start + width, (dev + 1) * WIDE_W)
            runs.append((dev, col - dev * WIDE_W, stop - col, kstart + col - start))
            col = stop
    return runs


def _win_layout(blocks, tm):
    runs = _column_runs()

    def body(g_ref, o_ref):
        o_ref[:, Z_KR:Z_KR + LANES] = jnp.zeros((tm, LANES), o_ref.dtype)
        for dev, lo, n, k in runs:
            o_ref[:, k:k + n] = g_ref[dev, :, lo:lo + n]

    return pl.pallas_call(
        body, grid=(D // tm,), name="win_layout",
        in_specs=[pl.BlockSpec((N_DEV, None, tm, WIDE_W), lambda i: (0, 0, i, 0))],
        out_specs=pl.BlockSpec((None, tm, ZW), lambda i: (0, i, 0)),
        out_shape=SDS((1, D, ZW), blocks.dtype),
        compiler_params=_cp("parallel"))(blocks)


def _win_grad_layout(dw, tm):
    runs = _column_runs()

    def body(g_ref, o_ref):
        for dev, lo, n, k in runs:
            o_ref[dev, :, lo:lo + n] = g_ref[:, k:k + n]

    return pl.pallas_call(
        body, grid=(D // tm,), name="win_grad_layout",
        in_specs=[_row(tm, ZW)],
        out_specs=pl.BlockSpec((N_DEV, None, tm, WIDE_W), lambda i: (0, 0, i, 0)),
        out_shape=SDS((N_DEV, 1, D, WIDE_W), F32),
        compiler_params=_cp("parallel"))(dw)


def _wuq_to_kernel(w):
    w = w.reshape(w.shape[:-1] + (MLA_H, MLA_QK))
    w = jnp.pad(w, [(0, 0)] * (w.ndim - 1) + [(0, LANES - MLA_QK)])
    return w.reshape(w.shape[:-2] + (QFW,))


def _wuq_from_kernel(g):
    g = g.reshape(g.shape[:-1] + (MLA_H, LANES))[..., :MLA_QK]
    return g.reshape(g.shape[:-2] + (MLA_H * MLA_QK,))


def _wukv_to_kernel(w):
    w = w.reshape(w.shape[:-1] + (MLA_H, MLA_NOPE + MLA_V))
    k = jnp.pad(w[..., :MLA_NOPE], [(0, 0)] * (w.ndim - 1) + [(0, LANES - MLA_NOPE)])
    v = w[..., MLA_NOPE:]
    return jnp.concatenate([k.reshape(k.shape[:-2] + (QFW,)), v.reshape(v.shape[:-2] + (MLA_H * MLA_V,))], axis=-1)


def _wukv_from_kernel(g):
    k = g[..., :QFW].reshape(g.shape[:-1] + (MLA_H, LANES))[..., :MLA_NOPE]
    v = g[..., QFW:].reshape(g.shape[:-1] + (MLA_H, MLA_V))
    kv = jnp.concatenate([k, v], axis=-1)
    return kv.reshape(kv.shape[:-2] + (MLA_H * (MLA_NOPE + MLA_V),))


def _rope_tables(pos):
    half = MLA_ROPE // 2
    inv = 10000.0 ** (-jnp.arange(0, MLA_ROPE, 2, dtype=F32) / MLA_ROPE)
    ang = pos.astype(F32)[:, None] * inv
    cos, sin = jnp.cos(ang), jnp.sin(ang)
    one = jnp.ones((pos.shape[0], MLA_NOPE), F32)
    zero = lambda n: jnp.zeros((pos.shape[0], n), F32)
    tc = jnp.concatenate([one, cos, cos, one[:, :LANES - MLA_QK]], axis=1)
    ts1 = jnp.concatenate([zero(MLA_NOPE + half), sin, zero(LANES - MLA_QK)], axis=1)
    ts2 = jnp.concatenate([zero(MLA_NOPE), -sin, zero(LANES - MLA_NOPE - half)], axis=1)
    return tc, ts1, ts2


def _local_step(x, p, positions, loss_target, small, wts, plan=None):
    nb, seq, _ = x.shape
    T = nb * seq
    tm = min(512, T)
    tq = min(512, seq)
    xf = x.reshape(T, D)
    pos = positions.reshape(T)
    posf = pos.astype(F32)
    pos_col, pos_row = posf.reshape(T, 1), posf.reshape(T // BLK, 1, BLK)
    tc, ts1, ts2 = _rope_tables(pos)

    wts, sm = list(wts), small
    pl_in = p.reshape(DEPTH, T, PLE)
    saved = []
    for i in range(DEPTH):
        riding = plan is not None and i == 0
        w = wts[i]
        z, hb, *got = _fwd_in(xf, sm["g_mix"], w["w_in"], tm, i,
                              ride=_Ride("gather", plan["rows0"]) if riding else None)
        if riding:
            w = wts[0] = dict(w, **plan["row_weights"](got[0]))
        oa, lse_a = _swa_fwd(sm["sink"], z, pos_col, pos_row, nb, seq, i)
        qf, kf, v = _fwd_prep(z, sm["g_q"], sm["g_kv"], w["w_uq"], w["w_ukv"], tc, ts1, ts2, tm, i)
        ob, lse_b, *got = _mla_fwd(qf, kf, v, nb, seq, tq, gather=plan["blocks1"] if riding else ())
        if riding:
            wts.append(dict(w_in=plan["w_in"](got[0]), **plan["row_weights"](got[1])))
        x1 = _fwd_merge(xf, oa, ob, z, w["w_br_a"], w["w_br_b"], w["w_out"], tm, i)
        x2, pg, pp = _fwd_ple(x1, pl_in, sm["g_ple"], w["w_ple_gate"], w["w_ple_proj"], tm, i)
        saved.append(dict(x=xf, z=z, hb=hb, oa=oa, lse_a=lse_a, qf=qf, kf=kf, v=v, ob=ob, lse_b=lse_b,
                          x1=x1, pg=pg, pp=pp))
        xf = x2

    dx, dg_final, loss = _loss_head(xf, small["g_final"], loss_target.reshape(T, D), tm)

    grads = [None] * DEPTH
    exchanged = {}
    for i in reversed(range(DEPTH)):
        riding = plan is not None and i == 0
        sv, w = saved[i], wts[i]
        pay = plan["payload"](grads[1]) if riding else []
        dx1, dwpg, dwpp, dg_ple, *got = _bwd_ple(dx, sv["x1"], sv["pg"], sv["pp"], pl_in, sm["g_ple"],
                                                 w["w_ple_gate"], tm, i, ride=_Ride("swap", pay) if riding else None)
        doa, dob, dag, dbg, dma, dmb, dsum_a, dsum_b, dwa, dwb, dwo = _bwd_merge(
            dx1, sv["oa"], sv["ob"], sv["z"], w["w_br_a"], w["w_br_b"], w["w_out"], tm, i)
        dq_b, dk_b, dv_b, *exchanged["layer1"] = _mla_bwd(
            sv["qf"], sv["kf"], sv["v"], dob, sv["lse_b"], dsum_b.reshape(MLA_H // 2, 2, T), nb, seq, tq,
            exchange=plan["add"](pay, got) if riding else ())
        dqd, dkvd, dkr, dwq, dwkv, dgq, dgkv = _bwd_prep(dq_b, dk_b, dv_b, sv["z"], sm["g_q"], sm["g_kv"],
                                                         w["w_uq"], w["w_ukv"], tc, ts1, ts2, tm, i)
        g = dict(w_uq=_wuq_from_kernel(dwq), w_ukv=_wukv_from_kernel(dwkv), w_br_a=dwa, w_br_b=dwb, w_out=dwo,
                 w_ple_gate=dwpg, w_ple_proj=dwpp)
        pay = [plan["rows_payload"](g)] if riding else []
        dq_a, dkv_a, dsink, *got = _swa_bwd(sm["sink"], sv["z"], pos_col, pos_row, doa, sv["lse_a"], dsum_a, nb, seq,
                                            i, ride=_Ride("swap", pay) if riding else None)
        dx, dzb, dg_mix = _bwd_in([dma, dmb, dq_a, dag, dbg, dqd, dkv_a, dkvd, dkr], sv["x"], sm["g_mix"], dx1,
                                  w["w_in"], tm, i)
        dwin, *exchanged["rows0"] = _wgrad_in(sv["hb"], dzb, tm,
                                              ride=_Ride("exchange", plan["add"](pay, got)) if riding else None)
        g.update(g_mix=dg_mix[0], w_in=dwin, sink=dsink[0, :SWA_H], g_q=dgq[0], g_kv=dgkv[0], g_ple=dg_ple[0])
        grads[i] = g
    return loss, dx.reshape(nb, seq, D), grads, dg_final[0], exchanged


def _row_weights(rows):
    blocks = _unpack_rows(rows)
    out = {n: _join(n, blocks[n]) for n, _ in ROWS_PIECES}
    out.update(w_uq=_wuq_to_kernel(out["w_uq"]), w_ukv=_wukv_to_kernel(out["w_ukv"]))
    return out


def _kernel_weights(gathered):
    return dict(w_in=_win_layout(gathered[0], 256), **_row_weights(gathered[1]))


def _small_params(g_mix, sink, g_q, g_kv, g_ple, g_final):
    return dict(g_mix=g_mix[:, None], sink=sink, g_q=g_q[:, None], g_kv=g_kv[:, None], g_ple=g_ple[:, None],
                g_final=g_final[None])


UQ_W = MLA_H * MLA_QK // N_DEV
ROWS_PIECES = (("w_uq", QL), ("w_ukv", KVL), ("w_br_a", 512), ("w_br_b", 512), ("w_out", D), ("w_ple_gate", D),
               ("w_ple_proj", PLE))
SMALL = (("g_mix", (DEPTH, D)), ("sink", (DEPTH, SWA_H)), ("g_q", (DEPTH, QL)), ("g_kv", (DEPTH, KVL)),
         ("g_ple", (DEPTH, D)), ("g_final", (D,)))
VEC_ROWS = 48
ROWS_N = sum(r for _, r in ROWS_PIECES)
WIDE_TILE, ROWS_TILE = 256, ROWS_N // 2


def _to_rows(name, a):
    if name == "w_uq":
        a = jnp.pad(a, [(0, 0)] * (a.ndim - 1) + [(0, LANES - UQ_W)])
    return a.reshape(a.shape[:-2] + (-1, LANES))


def _from_rows(name, r):
    if name in ("w_out", "w_ple_gate"):
        return r.reshape(r.shape[:-2] + (D // N_DEV, D))
    return r[..., :UQ_W] if name == "w_uq" else r


def _pack_rows(blocks):
    return jnp.concatenate([_to_rows(n, blocks[n]) for n, _ in ROWS_PIECES], axis=-2)


def _unpack_rows(rows):
    blocks, off = {}, 0
    for n, r in ROWS_PIECES:
        blocks[n] = _from_rows(n, rows[..., off:off + r, :])
        off += r
    return blocks


def _pack_vec(vectors, loss=None):
    parts = [vectors[n].reshape(-1) for n, _ in SMALL] + ([] if loss is None else [loss.reshape(1)])
    vec = jnp.concatenate(parts)
    return jnp.pad(vec, (0, VEC_ROWS * LANES - vec.shape[0])).reshape(1, VEC_ROWS, LANES)


def _unpack_vec(vec):
    vec = vec.reshape(-1)
    vectors, off = {}, 0
    for n, shp in SMALL:
        size = 1
        for s in shp:
            size *= s
        vectors[n] = vec[off:off + size].reshape(shp)
        off += size
    return vectors, vec[off]


def _join(name, blocks):
    if name in ("w_out", "w_ple_gate"):
        return jnp.moveaxis(blocks, 0, 1).reshape(blocks.shape[1], -1, blocks.shape[-1])
    return jnp.moveaxis(blocks, 0, 2).reshape(blocks.shape[1], blocks.shape[2], -1)


def _split(name, full):
    if name in ("w_out", "w_ple_gate"):
        return jnp.moveaxis(full.reshape(full.shape[0], N_DEV, -1, full.shape[-1]), 1, 0)
    return jnp.moveaxis(full.reshape(full.shape[0], full.shape[1], N_DEV, -1), 2, 0)


MESH_ID = pl.DeviceIdType.MESH
ANY = pl.BlockSpec(memory_space=pl.ANY)


def _place():
    return lax.axis_index("x"), lax.axis_index("y"), lax.axis_index("c")


def _all_gather(blocks):
    n = len(blocks)

    def body(*refs):
        start, forward, finish = _gather_phases(refs[:n], refs[n:2 * n], *refs[2 * n:])
        start()
        forward()
        finish()

    return pl.pallas_call(
        body, name="all_gather_weights", out_shape=_gather_out(blocks),
        in_specs=[ANY] * n, out_specs=[ANY] * n, scratch_shapes=_gather_sems(n))(*blocks)


def _gather_out(blocks):
    return [SDS((N_DEV,) + b.shape, b.dtype) for b in blocks]


def _gather_sems(n):
    return [pltpu.SemaphoreType.DMA((7 * n,)), pltpu.SemaphoreType.DMA((7 * n,)), pltpu.SemaphoreType.DMA((n,))]


def _gather_phases(x_refs, out_refs, send_sems, recv_sems, local_sems):
    n = len(x_refs)
    x, y, c = _place()
    me, sibling = (x, y, c), (x, y, 1 - c)
    chips = [(1 - x, y), (x, 1 - y), (1 - x, 1 - y)]

    def slot(a, px, py, pc):
        return out_refs[a].at[4 * px + 2 * py + pc]

    def copy(a, k, blk, to, src=None):
        return pltpu.make_async_remote_copy(
            src_ref=slot(a, *blk) if src is None else src, dst_ref=slot(a, *blk),
            send_sem=send_sems.at[7 * a + k], recv_sem=recv_sems.at[7 * a + k], device_id=to,
            device_id_type=MESH_ID)

    def mine():
        return [pltpu.make_async_copy(x_refs[a], slot(a, *me), local_sems.at[a]) for a in range(n)]

    def first():
        out = []
        for a in range(n):
            out += [copy(a, 0, me, sibling, src=x_refs[a])]
            out += [copy(a, 1 + j, me, (*chip, c), src=x_refs[a]) for j, chip in enumerate(chips)]
        return out

    def passed():
        return [copy(a, 4 + j, (*chip, c), sibling) for j, chip in enumerate(chips) for a in range(n)]

    def start():
        for cp in mine() + first():
            cp.start()

    def forward():
        for j, chip in enumerate(chips):
            for a in range(n):
                copy(a, 1 + j, (*chip, c), me).wait_recv()
                copy(a, 4 + j, (*chip, c), sibling).start()

    def finish():
        for a in range(n):
            copy(a, 0, sibling, me).wait_recv()
            for j, chip in enumerate(chips):
                copy(a, 4 + j, (*chip, 1 - c), me).wait_recv()
        for cp in first() + passed():
            cp.wait_send()
        for cp in mine():
            cp.wait()

    return start, forward, finish


def _swap_sibling(arrs):
    n = len(arrs)

    def body(*refs):
        start, finish = _swap_phases(refs[:n], refs[n:2 * n], *refs[2 * n:])
        start()
        finish()

    return pl.pallas_call(
        body, name="swap_sibling", out_shape=[SDS((a.shape[0],) + a.shape[2:], a.dtype) for a in arrs],
        in_specs=[ANY] * n, out_specs=[ANY] * n, scratch_shapes=_swap_sems(n))(*arrs)


def _swap_sems(n):
    return [pltpu.SemaphoreType.DMA((n,)), pltpu.SemaphoreType.DMA((n,))]


def _swap_phases(a_refs, out_refs, send_sems, recv_sems):
    x, y, c = _place()

    def copies():
        return [pltpu.make_async_remote_copy(
            src_ref=a_refs[a].at[:, 1 - c], dst_ref=out_refs[a], send_sem=send_sems.at[a], recv_sem=recv_sems.at[a],
            device_id=(x, y, 1 - c), device_id_type=MESH_ID) for a in range(len(a_refs))]

    def start():
        for cp in copies():
            cp.start()

    def finish():
        for cp in copies():
            cp.wait()

    return start, finish


def _exchange_chips(arrs):
    n = len(arrs)

    def body(*refs):
        start, finish = _exchange_phases(refs[:n], refs[n:2 * n], *refs[2 * n:])
        start()
        finish()

    return pl.pallas_call(
        body, name="exchange_chips", out_shape=[SDS(a.shape, a.dtype) for a in arrs],
        in_specs=[ANY] * n, out_specs=[ANY] * n, scratch_shapes=_exchange_sems(n))(*arrs)


def _exchange_sems(n):
    return [pltpu.SemaphoreType.DMA((3 * n,)), pltpu.SemaphoreType.DMA((3 * n,)), pltpu.SemaphoreType.DMA((n,))]


def _exchange_phases(p_refs, out_refs, send_sems, recv_sems, local_sems):
    n = len(p_refs)
    x, y, c = _place()
    mine = 2 * x + y
    peers = [(1 - x, y), (x, 1 - y), (1 - x, 1 - y)]

    def local():
        return [pltpu.make_async_copy(p_refs[a].at[mine], out_refs[a].at[mine], local_sems.at[a]) for a in range(n)]

    def copy(a, j, src_chip, dst_chip):
        px, py = peers[j]
        return pltpu.make_async_remote_copy(
            src_ref=p_refs[a].at[src_chip], dst_ref=out_refs[a].at[dst_chip], send_sem=send_sems.at[3 * a + j],
            recv_sem=recv_sems.at[3 * a + j], device_id=(px, py, c), device_id_type=MESH_ID)

    def sends():
        return [copy(a, j, 2 * px + py, mine) for a in range(n) for j, (px, py) in enumerate(peers)]

    def start():
        for cp in local() + sends():
            cp.start()

    def finish():
        for a in range(n):
            for j, (px, py) in enumerate(peers):
                copy(a, j, mine, 2 * px + py).wait_recv()
        for cp in sends():
            cp.wait_send()
        for cp in local():
            cp.wait()

    return start, finish


def _add_mine(g, recv, core, tile, dtype):
    _, _, lead, rows, width = g.shape

    def body(c_ref, g_ref, r_ref, o_ref):
        o_ref[...] = (g_ref[...] + r_ref[...]).astype(dtype)

    spec = pl.BlockSpec((None, None, tile, width), lambda k, l, i, c_ref: (k, l, i, 0))
    return pl.pallas_call(
        body, name="add_sibling", out_shape=SDS(recv.shape, dtype),
        grid_spec=pltpu.PrefetchScalarGridSpec(
            num_scalar_prefetch=1, grid=(g.shape[0], lead, rows // tile),
            in_specs=[pl.BlockSpec((None, None, None, tile, width), lambda k, l, i, c_ref: (k, c_ref[0], l, i, 0)),
                      spec],
            out_specs=spec),
        compiler_params=_cp("parallel", "parallel", "parallel"))(core, g, recv)


def _sum_adamw(parts, w, m, v, tile):
    lead, rows, width = w.shape
    last = rows // tile - 1

    def body(*refs):
        p_refs, (w_ref, m_ref, v_ref, g_ref, d_ref, nm_ref, nv_ref) = refs[:lead], refs[lead:]
        for layer in range(lead):
            @pl.when(pl.program_id(0) == layer)
            def _(p_ref=p_refs[layer]):
                g = ((p_ref[0].astype(F32) + p_ref[1].astype(F32)) + p_ref[2].astype(F32)) + p_ref[3].astype(F32)
                nm = ADAM_B1 * m_ref[...] + (1.0 - ADAM_B1) * g
                nv = ADAM_B2 * v_ref[...] + (1.0 - ADAM_B2) * jnp.square(g)
                m_hat = nm / (1.0 - ADAM_B1 ** ADAM_STEP)
                v_hat = nv / (1.0 - ADAM_B2 ** ADAM_STEP)
                g_ref[...] = g
                nm_ref[...] = nm
                nv_ref[...] = nv
                d_ref[...] = -ADAM_LR * (m_hat / (jnp.sqrt(v_hat) + ADAM_EPS) + ADAM_WD * w_ref[...])

    pspec = lambda layer: pl.BlockSpec(
        (4, None, tile, width),
        lambda l, i: (0, 0, jnp.where(l == layer, i, jnp.where(l > layer, last, 0)), 0))
    spec = pl.BlockSpec((None, tile, width), lambda l, i: (l, i, 0))
    return pl.pallas_call(
        body, grid=(lead, rows // tile), name="sum_adamw",
        in_specs=[pspec(layer) for layer in range(lead)] + [spec, spec, spec],
        out_specs=[spec] * 4, out_shape=[SDS((lead, rows, width), F32)] * 4,
        compiler_params=_cp("arbitrary", "arbitrary"))(*parts, w, m, v)


def kernel(x, p, positions, g_mix, w_in, sink, g_q, w_uq, g_kv, w_ukv, w_br_a, w_br_b, w_out, g_ple, w_ple_gate, w_ple_proj, g_final, loss_target, m_g_mix, m_w_in, m_sink, m_g_q, m_w_uq, m_g_kv, m_w_ukv, m_w_br_a, m_w_br_b, m_w_out, m_g_ple, m_w_ple_gate, m_w_ple_proj, m_g_final, v_g_mix, v_w_in, v_sink, v_g_q, v_w_uq, v_g_kv, v_w_ukv, v_w_br_a, v_w_br_b, v_w_out, v_g_ple, v_w_ple_gate, v_w_ple_proj, v_g_final):
    weights = dict(g_mix=g_mix, w_in=w_in, sink=sink, g_q=g_q, w_uq=w_uq, g_kv=g_kv, w_ukv=w_ukv, w_br_a=w_br_a,
                   w_br_b=w_br_b, w_out=w_out, g_ple=g_ple, w_ple_gate=w_ple_gate, w_ple_proj=w_ple_proj,
                   g_final=g_final)
    mom1 = dict(g_mix=m_g_mix, w_in=m_w_in, sink=m_sink, g_q=m_g_q, w_uq=m_w_uq, g_kv=m_g_kv, w_ukv=m_w_ukv,
                w_br_a=m_w_br_a, w_br_b=m_w_br_b, w_out=m_w_out, g_ple=m_g_ple, w_ple_gate=m_w_ple_gate,
                w_ple_proj=m_w_ple_proj, g_final=m_g_final)
    mom2 = dict(g_mix=v_g_mix, w_in=v_w_in, sink=v_sink, g_q=v_g_q, w_uq=v_w_uq, g_kv=v_g_kv, w_ukv=v_w_ukv,
                w_br_a=v_w_br_a, w_br_b=v_w_br_b, w_out=v_w_out, g_ple=v_g_ple, w_ple_gate=v_w_ple_gate,
                w_ple_proj=v_w_ple_proj, g_final=v_g_final)
    assert DEPTH == 2
    wide = lambda d: d["w_in"]
    rows = lambda d: _pack_rows(d)
    core = lax.axis_index("c").astype(jnp.int32).reshape(1)

    w16 = [wide(weights).astype(BF16), rows(weights).astype(BF16)]
    wts0 = dict(w_in=_win_layout(_all_gather([w16[0][:1]])[0], 256))
    small = _small_params(g_mix, sink, g_q, g_kv, g_ple, g_final)

    def wide_payload(g):
        return _win_grad_layout(g["w_in"], 256).reshape(N_DEV // 2, 2, 1, D, WIDE_W)

    def rows_payload(g):
        return _pack_rows({n: _split(n, g[n][None]) for n, _ in ROWS_PIECES}).reshape(N_DEV // 2, 2, 1, ROWS_N, LANES)

    def add(pay, got):
        tiles = {D: (WIDE_TILE, BF16), ROWS_N: (ROWS_TILE, BF16), VEC_ROWS: (VEC_ROWS, F32)}
        return [_add_mine(a, b, core, *tiles[a.shape[-2]]) for a, b in zip(pay, got)]

    plan = dict(rows0=[w16[1][:1]], blocks1=[a[1:] for a in w16], w_in=lambda blocks: _win_layout(blocks, 256),
                row_weights=_row_weights, payload=lambda g: [wide_payload(g), rows_payload(g)],
                rows_payload=rows_payload, add=add)
    loss, grad_x, grads, dg_final, rode = _local_step(x, p, positions, loss_target, small, [wts0], plan)

    vectors = {n: jnp.stack([grads[i][n] for i in range(DEPTH)]) for n, _ in SMALL[:-1]}
    vectors["g_final"] = dg_final
    pay = [wide_payload(grads[0]),
           jnp.broadcast_to(_pack_vec(vectors, loss[0, 0]), (N_DEV // 2, 2, 1, VEC_ROWS, LANES))]
    parts_wide0, parts_vec = _exchange_chips(add(pay, _swap_sibling(pay)))
    out_wide = _sum_adamw([parts_wide0, rode["layer1"][0]], wide(weights), wide(mom1), wide(mom2), WIDE_TILE)
    out_rows = _sum_adamw([rode["rows0"][0], rode["layer1"][1]], rows(weights), rows(mom1), rows(mom2), ROWS_TILE)
    out_vec = _sum_adamw([parts_vec], _pack_vec(weights), _pack_vec(mom1), _pack_vec(mom2), VEC_ROWS)

    outs = []
    for ow, orow, ovec in zip(out_wide, out_rows, out_vec):
        named = _unpack_rows(orow)
        named.update(_unpack_vec(ovec)[0])
        named["w_in"] = ow
        outs += [named[n] for n in weights]
    loss = _unpack_vec(out_vec[0])[1]
    return (loss, grad_x, *outs)
```

```python
import functools

import jax
import jax.numpy as jnp
from jax import lax
from jax.experimental import pallas as pl
from jax.experimental.pallas import tpu as pltpu

F32, BF16 = jnp.float32, jnp.bfloat16
SDS = jax.ShapeDtypeStruct

D = 1024
DEPTH = 2
PLE = 256
BLK = 128
EPS = 1e-6
NEG = -1e30
SWA_H, SWA_KV, SWA_DH = 8, 2, 64
MLA_H, MLA_NOPE, MLA_ROPE, MLA_V = 8, 64, 32, 64
MLA_QK = MLA_NOPE + MLA_ROPE
QL, KVL = 256, 128
IN_W = 4256
N_DEV = 8

V7X_VMEM_BYTES = 64 * 1024 * 1024
LANES = 128
VMEM_LIMIT = V7X_VMEM_BYTES * 7 // 8

ZW = 4352
Z_MA, Z_MB, Z_AQ, Z_AG, Z_BG, Z_QD, Z_AK, Z_AV, Z_KVD, Z_KR = 0, 1024, 2048, 2560, 3072, 3584, 3840, 3968, 4096, 4224
QFW = MLA_H * LANES
KVW = QFW + MLA_H * MLA_V
MLA_SCALE = MLA_QK ** -0.5
LOG2E = 1.4426950408889634
SWA_SCALE = SWA_DH ** -0.5
ROLL_UP, ROLL_DOWN = MLA_ROPE // 2, LANES - MLA_ROPE // 2

ADAM_LR, ADAM_B1, ADAM_B2, ADAM_EPS, ADAM_WD, ADAM_STEP = 0.001, 0.9, 0.999, 1e-08, 0.01, 10


def _cp(*sem):
    return pltpu.CompilerParams(dimension_semantics=sem, vmem_limit_bytes=VMEM_LIMIT)


def _row(tm, w, col=0):
    return pl.BlockSpec((tm, w), lambda i: (i, col))


def _res(shape, layer=None):
    if layer is None:
        return pl.BlockSpec(shape, lambda *_: (0,) * len(shape), pipeline_mode=pl.Buffered(1))
    return pl.BlockSpec((None,) + shape, lambda *_: (layer,) + (0,) * len(shape), pipeline_mode=pl.Buffered(1))


def _acc(shape):
    return pl.BlockSpec(shape, lambda *_: (0,) * len(shape))


def _rstd(xf):
    return lax.rsqrt(jnp.mean(xf * xf, axis=-1, keepdims=True) + EPS)


def _norm_bwd(dh, n, r, g):
    dn = dh * g
    return r * (dn - n * jnp.mean(dn * n, axis=-1, keepdims=True)), dh * n


def _nt(a, b):
    return lax.dot_general(a, b, (((1,), (1,)), ((), ())), preferred_element_type=F32)


def _tn(a, b):
    return lax.dot_general(a, b, (((0,), (0,)), ((), ())), preferred_element_type=F32)


def _nn(a, b):
    return jnp.dot(a, b, preferred_element_type=F32)


def _sig(x):
    return jax.nn.sigmoid(x)


def _rope(t, c, s1, s2):
    return t * c + pltpu.roll(t, ROLL_UP, 1) * s1 + pltpu.roll(t, ROLL_DOWN, 1) * s2


def _rope_t(d, c, s1, s2):
    return d * c + pltpu.roll(d * s1, ROLL_DOWN, 1) + pltpu.roll(d * s2, ROLL_UP, 1)


def _fwd_in(x, g, w, tm, layer, ride=None):
    T = x.shape[0]
    grid = (T // tm,)

    def body(x_ref, g_ref, w_ref, z_ref, h_ref):
        xf = x_ref[...]
        h = ((xf * _rstd(xf)) * g_ref[...]).astype(BF16)
        h_ref[...] = h
        z_ref[...] = _nn(h, w_ref[...])

    r_in, r_out, r_shape, r_scratch, r_args = _ride_args(ride)
    return pl.pallas_call(
        _riding(ride, body, 3, 2, grid), grid=grid, name="fwd_in_ride" if ride else "fwd_in",
        in_specs=[_row(tm, D), _res((1, D), layer), _res((D, ZW), 0)] + r_in,
        out_specs=[_row(tm, ZW), _row(tm, D)] + r_out,
        out_shape=[SDS((T, ZW), F32), SDS((T, D), BF16)] + r_shape, scratch_shapes=r_scratch,
        compiler_params=_cp("arbitrary"))(x, g, w, *r_args)


def _fwd_prep(z, gq, gkv, wq, wkv, tc, ts1, ts2, tm, layer):
    T = z.shape[0]

    def body(qd_ref, kvd_ref, kr_ref, gq_ref, gkv_ref, wq_ref, wkv_ref, c_ref, s1_ref, s2_ref, q_ref, k_ref, v_ref):
        qd, kvd = qd_ref[...], kvd_ref[...]
        hq = ((qd * _rstd(qd)) * gq_ref[...]).astype(BF16)
        hkv = ((kvd * _rstd(kvd)) * gkv_ref[...]).astype(BF16)
        qf = _nn(hq, wq_ref[...])
        kvf = _nn(hkv, wkv_ref[...])
        c, s1, s2 = c_ref[...], s1_ref[...], s2_ref[...]
        krb = _rope(kr_ref[...], c, s1, s2)
        for h in range(MLA_H):
            sl = slice(LANES * h, LANES * (h + 1))
            q_ref[:, sl] = _rope(qf[:, sl], c, s1, s2).astype(BF16)
            k_ref[:, sl] = (kvf[:, sl] + krb).astype(BF16)
        v_ref[...] = kvf[:, QFW:].astype(BF16)

    return pl.pallas_call(
        body, grid=(T // tm,), name="fwd_prep",
        in_specs=[_row(tm, QL, Z_QD // QL), _row(tm, KVL, Z_KVD // KVL), _row(tm, LANES, Z_KR // LANES),
                  _res((1, QL), layer), _res((1, KVL), layer), _res((QL, QFW), 0), _res((KVL, KVW), 0),
                  _row(tm, LANES), _row(tm, LANES), _row(tm, LANES)],
        out_specs=[_row(tm, QFW), _row(tm, QFW), _row(tm, MLA_H * MLA_V)],
        out_shape=[SDS((T, QFW), BF16), SDS((T, QFW), BF16), SDS((T, MLA_H * MLA_V), BF16)],
        compiler_params=_cp("parallel"))(z, z, z, gq, gkv, wq, wkv, tc, ts1, ts2)


def _grid_ends(grid):
    ids = [pl.program_id(a) for a in range(len(grid))]
    inner_first = functools.reduce(jnp.logical_and, [i == 0 for i in ids[1:]], True)
    last = functools.reduce(jnp.logical_and, [i == g - 1 for i, g in zip(ids, grid)])
    return (ids[0] == 0) & inner_first, (ids[0] == grid[0] // 2) & inner_first, last


class _Ride:
    def __init__(self, kind, arrays):
        self.kind, self.arrays, self.n = kind, list(arrays), len(arrays)

    def out_shape(self):
        if self.kind == "gather":
            return _gather_out(self.arrays)
        if self.kind == "swap":
            return [SDS((a.shape[0],) + a.shape[2:], a.dtype) for a in self.arrays]
        return [SDS(a.shape, a.dtype) for a in self.arrays]

    def sems(self):
        if self.kind == "gather":
            return _gather_sems(self.n)
        if self.kind == "swap":
            return _swap_sems(self.n)
        return _exchange_sems(self.n)

    def phases(self, in_refs, out_refs, *sems):
        if self.kind == "gather":
            return _gather_phases(in_refs, out_refs, *sems)
        start, finish = (_swap_phases if self.kind == "swap" else _exchange_phases)(in_refs, out_refs, *sems)
        return start, None, finish


def _riding(ride, body, n_in, n_out, grid):
    if ride is None:
        return body
    n, n_sems = ride.n, len(ride.sems())

    def wrapped(*refs):
        ins, r_in = refs[:n_in], refs[n_in:n_in + n]
        outs, r_out = refs[n_in + n:n_in + n + n_out], refs[n_in + n + n_out:n_in + 2 * n + n_out]
        rest = refs[n_in + 2 * n + n_out:]
        scratch, sems = rest[:len(rest) - n_sems], rest[len(rest) - n_sems:]
        start, middle, finish = ride.phases(r_in, r_out, *sems)
        at_first, at_middle, at_last = _grid_ends(grid)
        pl.when(at_first)(start)
        if middle is not None:
            pl.when(at_middle)(middle)
        body(*ins, *outs, *scratch)
        pl.when(at_last)(finish)

    return wrapped


def _ride_args(ride):
    if ride is None:
        return [], [], [], [], []
    return [ANY] * ride.n, [ANY] * ride.n, ride.out_shape(), ride.sems(), ride.arrays


def _mla_fwd(qf, kf, v, nb, seq, tq, gather=()):
    T = qf.shape[0]
    nq = seq // tq
    pw = 2 * LANES
    pairs = [(qi, ki) for qi in range(nq) for ki in range(qi + 1)]
    qi_tab = jnp.array([qk[0] for qk in pairs], jnp.int32)
    ki_tab = jnp.array([qk[1] for qk in pairs], jnp.int32)
    grid = (nb, MLA_H // 2, len(pairs))
    n_g = len(gather)

    def body(qi_ref, ki_ref, q_ref, k_ref, v_ref, *rest):
        x_refs, (o_ref, lse_ref), got_refs = rest[:n_g], rest[n_g:n_g + 2], rest[n_g + 2:2 * n_g + 2]
        (m_s, l_s, acc_s), sems = rest[2 * n_g + 2:2 * n_g + 5], rest[2 * n_g + 5:]
        qi, ki = qi_ref[pl.program_id(2)], ki_ref[pl.program_id(2)]
        if n_g:
            start, forward, finish = _gather_phases(x_refs, got_refs, *sems)
            at_first, at_middle, at_last = _grid_ends(grid)
            pl.when(at_first)(start)
            pl.when(at_middle)(forward)

        @pl.when(ki == 0)
        def _():
            m_s[...] = jnp.full(m_s.shape, NEG, F32)
            l_s[...] = jnp.zeros(l_s.shape, F32)
            acc_s[...] = jnp.zeros(acc_s.shape, F32)

        def step(masked):
            if masked:
                keys = lax.broadcasted_iota(jnp.int32, (tq, tq), 0)
                queries = lax.broadcasted_iota(jnp.int32, (tq, tq), 1)
                mask = keys <= queries
            ss = []
            for j in range(2):
                wide = slice(LANES * j, LANES * (j + 1))
                s = _nt(k_ref[:, wide], q_ref[:, wide]) * (MLA_SCALE * LOG2E)
                ss.append(jnp.where(mask, s, NEG) if masked else s)
            ps, alphas = [], []
            for j in range(2):
                m_prev = m_s[j]
                m_new = jnp.maximum(m_prev, jnp.max(ss[j], axis=0, keepdims=True))
                alpha = jnp.exp2(m_prev - m_new)
                p = jnp.exp2(ss[j] - m_new)
                l_s[j] = alpha * l_s[j] + jnp.sum(p, axis=0, keepdims=True)
                m_s[j] = m_new
                ps.append(p.astype(BF16))
                alphas.append(alpha)
            for j in range(2):
                rows = slice(MLA_V * j, MLA_V * (j + 1))
                acc_s[rows, :] = alphas[j] * acc_s[rows, :] + _tn(v_ref[:, rows], ps[j])

        @pl.when(ki < qi)
        def _():
            step(False)

        @pl.when(ki == qi)
        def _():
            step(True)
            for j in range(2):
                rows = slice(MLA_V * j, MLA_V * (j + 1))
                acc_s[rows, :] = acc_s[rows, :] / l_s[j]
                lse_ref[j:j + 1, :] = m_s[j] + jnp.log2(l_s[j])
            o_ref[...] = acc_s[...].T

        if n_g:
            pl.when(at_last)(finish)

    q_map = lambda b, hp, s, qi_ref, ki_ref: (b * nq + qi_ref[s], hp)
    kv_map = lambda b, hp, s, qi_ref, ki_ref: (b * nq + ki_ref[s], hp)
    return pl.pallas_call(
        body, name="mla_fwd_gather" if n_g else "mla_fwd",
        grid_spec=pltpu.PrefetchScalarGridSpec(
            num_scalar_prefetch=2, grid=grid,
            in_specs=[pl.BlockSpec((tq, pw), q_map), pl.BlockSpec((tq, pw), kv_map),
                      pl.BlockSpec((tq, LANES), kv_map)] + [ANY] * n_g,
            out_specs=[pl.BlockSpec((tq, LANES), q_map),
                       pl.BlockSpec((None, 2, tq), lambda b, hp, s, qi_ref, ki_ref: (hp, 0, b * nq + qi_ref[s]))]
            + [ANY] * n_g,
            scratch_shapes=[pltpu.VMEM((2, 1, tq), F32), pltpu.VMEM((2, 1, tq), F32), pltpu.VMEM((LANES, tq), F32)]
            + (_gather_sems(n_g) if n_g else [])),
        out_shape=[SDS((T, MLA_H * MLA_V), F32), SDS((MLA_H // 2, 2, T), F32)] + _gather_out(gather),
        compiler_params=_cp("arbitrary", "arbitrary", "arbitrary"))(qi_tab, ki_tab, qf, kf, v, *gather)


def _swa_specs(nblk):
    cur = lambda b, n: (b * nblk + n, 0)
    prev = lambda b, n: (b * nblk + jnp.maximum(n - 1, 0), 0)
    kvc = Z_AK // (2 * BLK)
    return [pl.BlockSpec(memory_space=pltpu.SMEM),
            pl.BlockSpec((BLK, 512), lambda b, n: (b * nblk + n, Z_AQ // 512)),
            pl.BlockSpec((BLK, 2 * BLK), lambda b, n: (b * nblk + n, kvc)),
            pl.BlockSpec((BLK, 2 * BLK), lambda b, n: (b * nblk + jnp.maximum(n - 1, 0), kvc)),
            pl.BlockSpec((BLK, 1), cur),
            pl.BlockSpec((BLK, 1), prev),
            pl.BlockSpec((1, 1, BLK), lambda b, n: (b * nblk + n, 0, 0))]


def _swa_scores(n, q_ref, kvc_ref, kvp_ref, pcc_ref, pcp_ref, pr_ref):
    kv = jnp.concatenate([kvp_ref[...], kvc_ref[...]], axis=0)
    kb, vb = kv[:, :BLK].astype(BF16), kv[:, BLK:].astype(BF16)
    dist = pr_ref[0] - jnp.concatenate([pcp_ref[...], pcc_ref[...]], axis=0)
    key = lax.broadcasted_iota(jnp.int32, (2 * BLK, BLK), 0)
    qry = lax.broadcasted_iota(jnp.int32, (2 * BLK, BLK), 1)
    valid = (key > qry) & (key <= qry + BLK) & ((key >= BLK) | (n > 0))

    def scores(h):
        g = h // (SWA_H // SWA_KV)
        qh = q_ref[:, SWA_DH * h:SWA_DH * (h + 1)].astype(BF16)
        s = _nt(kb[:, SWA_DH * g:SWA_DH * (g + 1)], qh) * SWA_SCALE - (2.0 ** -(h + 1)) * dist
        return qh, jnp.where(valid, s, NEG)

    return kb, vb, scores


def _swa_fwd(sink, z, pos_col, pos_row, nb, seq, layer):
    T = z.shape[0]
    nblk = seq // BLK

    def body(sink_ref, q_ref, kvc_ref, kvp_ref, pcc_ref, pcp_ref, pr_ref, o_ref, lse_ref):
        kb, vb, scores = _swa_scores(pl.program_id(1), q_ref, kvc_ref, kvp_ref, pcc_ref, pcp_ref, pr_ref)
        ss = [scores(h)[1] for h in range(SWA_H)]
        es, dens = [], []
        for h in range(SWA_H):
            sk = sink_ref[layer, h]
            m = jnp.maximum(jnp.max(ss[h], axis=0, keepdims=True), sk)
            e = jnp.exp(ss[h] - m)
            den = jnp.sum(e, axis=0, keepdims=True) + jnp.exp(sk - m)
            lse_ref[h:h + 1, :] = m + jnp.log(den)
            es.append(e.astype(BF16))
            dens.append(den)
        outs = []
        for h in range(SWA_H):
            g = h // (SWA_H // SWA_KV)
            outs.append(_tn(vb[:, SWA_DH * g:SWA_DH * (g + 1)], es[h]) / dens[h])
        o_ref[...] = jnp.concatenate(outs, axis=0).T

    return pl.pallas_call(
        body, grid=(nb, nblk), name="swa_fwd",
        in_specs=_swa_specs(nblk),
        out_specs=[pl.BlockSpec((BLK, 512), lambda b, n: (b * nblk + n, 0)),
                   pl.BlockSpec((SWA_H, BLK), lambda b, n: (0, b * nblk + n))],
        out_shape=[SDS((T, 512), F32), SDS((SWA_H, T), F32)],
        compiler_params=_cp("parallel", "parallel"))(sink, z, z, z, pos_col, pos_col, pos_row)


def _fwd_merge(x, oa, ob, z, wa, wb, wo, tm, layer):
    T = x.shape[0]

    def body(x_ref, oa_ref, ob_ref, ag_ref, bg_ref, ma_ref, mb_ref, wa_ref, wb_ref, wo_ref, x1_ref):
        ag, bg = ag_ref[...], bg_ref[...]
        ua = _nn((oa_ref[...] * (ag * _sig(ag))).astype(BF16), wa_ref[...])
        ub = _nn((ob_ref[...] * (bg * _sig(bg))).astype(BF16), wb_ref[...])
        y = _sig(ma_ref[...]) * ua + _sig(mb_ref[...]) * ub
        x1_ref[...] = x_ref[...] + _nn(y.astype(BF16), wo_ref[...])

    return pl.pallas_call(
        body, grid=(T // tm,), name="fwd_merge",
        in_specs=[_row(tm, D), _row(tm, 512), _row(tm, 512), _row(tm, 512, Z_AG // 512), _row(tm, 512, Z_BG // 512),
                  _row(tm, D, Z_MA // D), _row(tm, D, Z_MB // D),
                  _res((512, D), 0), _res((512, D), 0), _res((D, D), 0)],
        out_specs=_row(tm, D),
        out_shape=SDS((T, D), F32),
        compiler_params=_cp("parallel"))(x, oa, ob, z, z, z, z, wa, wb, wo)


def _fwd_ple(x1, p, g, wpg, wpp, tm, layer):
    T = x1.shape[0]

    def body(x_ref, p_ref, g_ref, wpg_ref, wpp_ref, x2_ref, pg_ref, pp_ref):
        xf = x_ref[...]
        h1 = ((xf * _rstd(xf)) * g_ref[...]).astype(BF16)
        pg = _sig(_nn(h1, wpg_ref[...]))
        pp = _nn(p_ref[...].astype(BF16), wpp_ref[...])
        pg_ref[...] = pg
        pp_ref[...] = pp
        x2_ref[...] = xf + pg * pp

    return pl.pallas_call(
        body, grid=(T // tm,), name="fwd_ple",
        in_specs=[_row(tm, D), pl.BlockSpec((None, tm, PLE), lambda i: (layer, i, 0)),
                  _res((1, D), layer), _res((D, D), 0), _res((PLE, D), 0)],
        out_specs=[_row(tm, D)] * 3,
        out_shape=[SDS((T, D), F32)] * 3,
        compiler_params=_cp("parallel"))(x1, p, g, wpg, wpp)


def _loss_head(x, g, tgt, tm):
    T = x.shape[0]

    def body(x_ref, g_ref, t_ref, dx_ref, dg_ref, loss_ref):
        @pl.when(pl.program_id(0) == 0)
        def _():
            dg_ref[...] = jnp.zeros(dg_ref.shape, F32)
            loss_ref[...] = jnp.zeros(loss_ref.shape, F32)

        xf, gf = x_ref[...], g_ref[...]
        r = _rstd(xf)
        n = xf * r
        err = n * gf - t_ref[...]
        loss_ref[...] += 0.5 * jnp.sum(jnp.mean(err * err, axis=-1, keepdims=True), axis=0, keepdims=True)
        dx, dgr = _norm_bwd(err * (1.0 / D), n, r, gf)
        dx_ref[...] = dx
        dg_ref[...] += jnp.sum(dgr, axis=0, keepdims=True)

    return pl.pallas_call(
        body, grid=(T // tm,), name="loss_head",
        in_specs=[_row(tm, D), _res((1, D)), _row(tm, D)],
        out_specs=[_row(tm, D), _acc((1, D)), _acc((1, LANES))],
        out_shape=[SDS((T, D), F32), SDS((1, D), F32), SDS((1, LANES), F32)],
        compiler_params=_cp("arbitrary"))(x, g, tgt)


def _bwd_ple(dx2, x1, pg, pp, p, g, wpg, tm, layer, ride=None):
    T = x1.shape[0]
    grid = (T // tm,)

    def body(d_ref, x_ref, pg_ref, pp_ref, p_ref, g_ref, w_ref, dx_ref, dwg_ref, dwp_ref, dg_ref):
        @pl.when(pl.program_id(0) == 0)
        def _():
            dwg_ref[...] = jnp.zeros(dwg_ref.shape, F32)
            dwp_ref[...] = jnp.zeros(dwp_ref.shape, F32)
            dg_ref[...] = jnp.zeros(dg_ref.shape, F32)

        d, xf, pg, gf = d_ref[...], x_ref[...], pg_ref[...], g_ref[...]
        r = _rstd(xf)
        n = xf * r
        dpgl = (d * pp_ref[...] * pg * (1.0 - pg)).astype(BF16)
        dwg_ref[...] += _tn((n * gf).astype(BF16), dpgl)
        dwp_ref[...] += _tn(p_ref[...].astype(BF16), (d * pg).astype(BF16))
        dxn, dgr = _norm_bwd(_nt(dpgl, w_ref[...]), n, r, gf)
        dx_ref[...] = d + dxn
        dg_ref[...] += jnp.sum(dgr, axis=0, keepdims=True)

    r_in, r_out, r_shape, r_scratch, r_args = _ride_args(ride)
    return pl.pallas_call(
        _riding(ride, body, 7, 4, grid), grid=grid, name="bwd_ple_ride" if ride else "bwd_ple",
        in_specs=[_row(tm, D)] * 4 + [pl.BlockSpec((None, tm, PLE), lambda i: (layer, i, 0)),
                                      _res((1, D), layer), _res((D, D), 0)] + r_in,
        out_specs=[_row(tm, D), _acc((D, D)), _acc((PLE, D)), _acc((1, D))] + r_out,
        out_shape=[SDS((T, D), F32), SDS((D, D), F32), SDS((PLE, D), F32), SDS((1, D), F32)] + r_shape,
        scratch_shapes=r_scratch,
        compiler_params=_cp("arbitrary"))(dx2, x1, pg, pp, p, g, wpg, *r_args)


def _bwd_merge(dx1, oa, ob, z, wa, wb, wo, tm, layer):
    T = dx1.shape[0]

    def body(d_ref, oa_ref, ob_ref, ag_ref, bg_ref, ma_ref, mb_ref, wa_ref, wb_ref, wo_ref,
             doa_ref, dob_ref, dag_ref, dbg_ref, dma_ref, dmb_ref, dsa_ref, dsb_ref, dwa_ref, dwb_ref, dwo_ref):
        @pl.when(pl.program_id(0) == 0)
        def _():
            dwa_ref[...] = jnp.zeros(dwa_ref.shape, F32)
            dwb_ref[...] = jnp.zeros(dwb_ref.shape, F32)
            dwo_ref[...] = jnp.zeros(dwo_ref.shape, F32)

        db = d_ref[...].astype(BF16)
        gated = []
        for o_ref, gate_ref, w_ref in ((oa_ref, ag_ref, wa_ref), (ob_ref, bg_ref, wb_ref)):
            raw, gate = o_ref[...], gate_ref[...]
            sg = _sig(gate)
            silu = gate * sg
            ob16 = (raw * silu).astype(BF16)
            gated.append((raw, gate, sg, silu, ob16, _nn(ob16, w_ref[...])))
        ua, ub = gated[0][5], gated[1][5]
        sa, sb = _sig(ma_ref[...]), _sig(mb_ref[...])
        dwo_ref[...] += _tn((sa * ua + sb * ub).astype(BF16), db)
        dy = _nt(db, wo_ref[...])
        dma_ref[...] = (dy * ua * sa * (1.0 - sa)).astype(BF16)
        dmb_ref[...] = (dy * ub * sb * (1.0 - sb)).astype(BF16)
        for (s, w_ref, do_ref, dgate_ref, dw_ref, ds_ref), (raw, gate, sg, silu, ob16, _) in zip((
                (sa, wa_ref, doa_ref, dag_ref, dwa_ref, dsa_ref),
                (sb, wb_ref, dob_ref, dbg_ref, dwb_ref, dsb_ref)), gated):
            du = (dy * s).astype(BF16)
            dw_ref[...] += _tn(ob16, du)
            do = _nt(du, w_ref[...])
            draw = do * silu
            do_ref[...] = draw.astype(BF16)
            dgate_ref[...] = (do * raw * (sg * (1.0 + gate * (1.0 - sg)))).astype(BF16)
            ds_ref[...] = jnp.sum((draw * raw).T.reshape(MLA_H, MLA_V, tm), axis=1)

    return pl.pallas_call(
        body, grid=(T // tm,), name="bwd_merge",
        in_specs=[_row(tm, D), _row(tm, 512), _row(tm, 512), _row(tm, 512, Z_AG // 512), _row(tm, 512, Z_BG // 512),
                  _row(tm, D, Z_MA // D), _row(tm, D, Z_MB // D),
                  _res((512, D), 0), _res((512, D), 0), _res((D, D), 0)],
        out_specs=[_row(tm, 512)] * 4 + [_row(tm, D)] * 2 + [pl.BlockSpec((MLA_H, tm), lambda i: (0, i))] * 2
        + [_acc((512, D)), _acc((512, D)), _acc((D, D))],
        out_shape=[SDS((T, 512), BF16), SDS((T, 512), BF16), SDS((T, 512), BF16), SDS((T, 512), BF16),
                   SDS((T, D), BF16), SDS((T, D), BF16), SDS((MLA_H, T), F32), SDS((MLA_H, T), F32),
                   SDS((512, D), F32), SDS((512, D), F32), SDS((D, D), F32)],
        compiler_params=_cp("arbitrary"))(dx1, oa, ob, z, z, z, z, wa, wb, wo)


def _mla_bwd(qf, kf, v, do, lse, dsum, nb, seq, tq, exchange=()):
    T = qf.shape[0]
    nq = seq // tq
    pw = 2 * LANES
    pairs = [(qi, ki) for ki in range(nq) for qi in range(ki, nq)]
    qi_tab = jnp.array([qk[0] for qk in pairs], jnp.int32)
    ki_tab = jnp.array([qk[1] for qk in pairs], jnp.int32)
    grid = (nb, MLA_H // 2, len(pairs))
    n_x = len(exchange)

    def body(qi_ref, ki_ref, q_ref, k_ref, v_ref, do_ref, lse_ref, dsum_ref, *rest):
        p_refs, (dq_ref, dk_ref, dv_ref), got_refs = rest[:n_x], rest[n_x:n_x + 3], rest[n_x + 3:2 * n_x + 3]
        (dk_s, dv_s, dqt_s), sems = rest[2 * n_x + 3:2 * n_x + 6], rest[2 * n_x + 6:]
        step_id = pl.program_id(2)
        qi, ki = qi_ref[step_id], ki_ref[step_id]
        if n_x:
            start, finish = _exchange_phases(p_refs, got_refs, *sems)
            at_first, _, at_last = _grid_ends(grid)
            pl.when(at_first)(start)

        @pl.when(step_id == 0)
        def _():
            dqt_s[...] = jnp.zeros(dqt_s.shape, F32)

        @pl.when(qi == ki)
        def _():
            dk_s[...] = jnp.zeros(dk_s.shape, F32)
            dv_s[...] = jnp.zeros(dv_s.shape, F32)

        def step(masked):
            if masked:
                keys = lax.broadcasted_iota(jnp.int32, (tq, tq), 0)
                queries = lax.broadcasted_iota(jnp.int32, (tq, tq), 1)
                mask = keys <= queries
            for j in range(2):
                wide = slice(LANES * j, LANES * (j + 1))
                sl = slice(MLA_V * j, MLA_V * (j + 1))
                q, k = q_ref[:, wide], k_ref[:, wide]
                dob = do_ref[:, sl].astype(BF16)
                s = _nt(k, q) * (MLA_SCALE * LOG2E)
                if masked:
                    s = jnp.where(mask, s, NEG)
                p = jnp.exp2(s - lse_ref[j:j + 1, :])
                dv_s[:, sl] += _nn(p.astype(BF16), dob)
                ds = (p * (_nt(v_ref[:, sl], dob) - dsum_ref[j:j + 1, :]) * MLA_SCALE).astype(BF16)
                dk_s[:, wide] += _nn(ds, q)
                dqt_s[qi, wide, :] += _tn(k, ds)

        @pl.when(qi > ki)
        def _():
            step(False)

        @pl.when(qi == ki)
        def _():
            step(True)

        @pl.when(qi == nq - 1)
        def _():
            dk_ref[...] = dk_s[...]
            dv_ref[...] = dv_s[...]

        @pl.when(step_id == len(pairs) - 1)
        def _():
            for n in range(nq):
                dq_ref[tq * n:tq * (n + 1), :] = dqt_s[n].T

        if n_x:
            pl.when(at_last)(finish)

    qmap = lambda b, hp, s, qi_ref, ki_ref: (b * nq + qi_ref[s], hp)
    kmap = lambda b, hp, s, qi_ref, ki_ref: (b * nq + ki_ref[s], hp)
    stat = pl.BlockSpec((None, 2, tq), lambda b, hp, s, qi_ref, ki_ref: (hp, 0, b * nq + qi_ref[s]))
    return pl.pallas_call(
        body, name="mla_bwd_exchange" if n_x else "mla_bwd",
        grid_spec=pltpu.PrefetchScalarGridSpec(
            num_scalar_prefetch=2, grid=grid,
            in_specs=[pl.BlockSpec((tq, pw), qmap), pl.BlockSpec((tq, pw), kmap), pl.BlockSpec((tq, LANES), kmap),
                      pl.BlockSpec((tq, LANES), qmap), stat, stat] + [ANY] * n_x,
            out_specs=[pl.BlockSpec((seq, pw), lambda b, hp, s, qi_ref, ki_ref: (b, hp)),
                       pl.BlockSpec((tq, pw), kmap), pl.BlockSpec((tq, LANES), kmap)] + [ANY] * n_x,
            scratch_shapes=[pltpu.VMEM((tq, pw), F32), pltpu.VMEM((tq, LANES), F32), pltpu.VMEM((nq, pw, tq), F32)]
            + (_exchange_sems(n_x) if n_x else [])),
        out_shape=[SDS((T, QFW), F32), SDS((T, QFW), F32), SDS((T, MLA_H * MLA_V), F32)]
        + [SDS(a.shape, a.dtype) for a in exchange],
        compiler_params=_cp("arbitrary", "arbitrary", "arbitrary"))(qi_tab, ki_tab, qf, kf, v, do, lse, dsum, *exchange)


def _swa_bwd(sink, z, pos_col, pos_row, do, lse, dsum, nb, seq, layer, ride=None):
    T = z.shape[0]
    nblk = seq // BLK

    def body(sink_ref, q_ref, kvc_ref, kvp_ref, pcc_ref, pcp_ref, pr_ref, do_ref, lse_ref, dsum_ref,
             dq_ref, dkv_ref, dsink_ref):
        b, n = pl.program_id(0), pl.program_id(1)

        @pl.when((b == 0) & (n == 0))
        def _():
            dsink_ref[...] = jnp.zeros(dsink_ref.shape, F32)

        @pl.when(n == 0)
        def _():
            dkv_ref[...] = jnp.zeros(dkv_ref.shape, F32)

        kb, vb, scores = _swa_scores(n, q_ref, kvc_ref, kvp_ref, pcc_ref, pcp_ref, pr_ref)
        lane = lax.broadcasted_iota(jnp.int32, (1, LANES), 1)
        dsink = jnp.zeros((1, LANES), F32)
        dkv = [[None, None], [None, None]]
        dqs = []
        gsl = lambda h: slice(SWA_DH * (h // (SWA_H // SWA_KV)), SWA_DH * (h // (SWA_H // SWA_KV) + 1))
        qs, ss, dobs, dps = [], [], [], []
        for h in range(SWA_H):
            qh, s = scores(h)
            dob = do_ref[:, SWA_DH * h:SWA_DH * (h + 1)].astype(BF16)
            qs.append(qh)
            ss.append(s)
            dobs.append(dob)
            dps.append(_nt(vb[:, gsl(h)], dob))
        pbs, dss = [], []
        for h in range(SWA_H):
            lse, dsum = lse_ref[h:h + 1, :], dsum_ref[h:h + 1, :]
            p = jnp.exp(ss[h] - lse)
            pbs.append(p.astype(BF16))
            dss.append((p * (dps[h] - dsum) * SWA_SCALE).astype(BF16))
            dsk = jnp.sum(-jnp.exp(sink_ref[layer, h] - lse) * dsum, axis=1, keepdims=True)
            dsink = dsink + jnp.where(lane == h, dsk, 0.0)
        for h in range(SWA_H):
            g = h // (SWA_H // SWA_KV)
            dqs.append(_tn(kb[:, gsl(h)], dss[h]))
            dk, dv = _nn(dss[h], qs[h]), _nn(pbs[h], dobs[h])
            dkv[g][0] = dk if dkv[g][0] is None else dkv[g][0] + dk
            dkv[g][1] = dv if dkv[g][1] is None else dkv[g][1] + dv
        dq_ref[...] = jnp.concatenate(dqs, axis=0).T.astype(BF16)
        dsink_ref[...] += dsink
        upd = jnp.concatenate([dkv[0][0], dkv[1][0], dkv[0][1], dkv[1][1]], axis=1)
        dkv_ref[pl.ds(pl.multiple_of(n * BLK, BLK), BLK), :] += upd[BLK:]

        @pl.when(n > 0)
        def _():
            dkv_ref[pl.ds(pl.multiple_of((n - 1) * BLK, BLK), BLK), :] += upd[:BLK]

    r_in, r_out, r_shape, r_scratch, r_args = _ride_args(ride)
    return pl.pallas_call(
        _riding(ride, body, 10, 3, (nb, nblk)), grid=(nb, nblk), name="swa_bwd_ride" if ride else "swa_bwd",
        in_specs=_swa_specs(nblk) + [pl.BlockSpec((BLK, 512), lambda b, n: (b * nblk + n, 0))]
        + [pl.BlockSpec((SWA_H, BLK), lambda b, n: (0, b * nblk + n))] * 2 + r_in,
        out_specs=[pl.BlockSpec((BLK, 512), lambda b, n: (b * nblk + n, 0)),
                   pl.BlockSpec((seq, 2 * BLK), lambda b, n: (b, 0)),
                   pl.BlockSpec((1, LANES), lambda b, n: (0, 0))] + r_out,
        out_shape=[SDS((T, 512), BF16), SDS((T, 2 * BLK), F32), SDS((1, LANES), F32)] + r_shape,
        scratch_shapes=r_scratch,
        compiler_params=_cp("arbitrary", "arbitrary"))(sink, z, z, z, pos_col, pos_col, pos_row, do, lse, dsum,
                                                       *r_args)


def _bwd_prep(dq, dk, dv, z, gq, gkv, wq, wkv, tc, ts1, ts2, tm, layer):
    T = z.shape[0]

    def body(dq_ref, dk_ref, dv_ref, qd_ref, kvd_ref, gq_ref, gkv_ref, wq_ref, wkv_ref, c_ref, s1_ref, s2_ref,
             dqd_ref, dkvd_ref, dkr_ref, dwq_ref, dwkv_ref, dgq_ref, dgkv_ref, dqb_s, dkvb_s):
        @pl.when(pl.program_id(0) == 0)
        def _():
            for ref in (dwq_ref, dwkv_ref, dgq_ref, dgkv_ref):
                ref[...] = jnp.zeros(ref.shape, F32)

        c, s1, s2 = c_ref[...], s1_ref[...], s2_ref[...]
        lane = lax.broadcasted_iota(jnp.int32, (1, LANES), 1)
        rope_lanes = (lane >= MLA_NOPE) & (lane < MLA_QK)
        dkb = jnp.zeros((tm, LANES), F32)
        for h in range(MLA_H):
            sl = slice(LANES * h, LANES * (h + 1))
            dqb_s[:, sl] = _rope_t(dq_ref[:, sl], c, s1, s2).astype(BF16)
            dkh = dk_ref[:, sl]
            dkb = dkb + dkh
            dkvb_s[:, sl] = dkh.astype(BF16)
        dkvb_s[:, QFW:] = dv_ref[...].astype(BF16)
        dkr_ref[...] = _rope_t(jnp.where(rope_lanes, dkb, 0.0), c, s1, s2).astype(BF16)

        for (x_ref, g_ref, w_ref, d_s, dx_ref, dw_ref, dg_ref) in (
                (qd_ref, gq_ref, wq_ref, dqb_s, dqd_ref, dwq_ref, dgq_ref),
                (kvd_ref, gkv_ref, wkv_ref, dkvb_s, dkvd_ref, dwkv_ref, dgkv_ref)):
            xf, gf, db = x_ref[...], g_ref[...], d_s[...]
            r = _rstd(xf)
            n = xf * r
            dw_ref[...] += _tn((n * gf).astype(BF16), db)
            dx, dgr = _norm_bwd(_nt(db, w_ref[...]), n, r, gf)
            dx_ref[...] = dx.astype(BF16)
            dg_ref[...] += jnp.sum(dgr, axis=0, keepdims=True)

    return pl.pallas_call(
        body, grid=(T // tm,), name="bwd_prep",
        in_specs=[_row(tm, QFW), _row(tm, QFW), _row(tm, MLA_H * MLA_V),
                  _row(tm, QL, Z_QD // QL), _row(tm, KVL, Z_KVD // KVL),
                  _res((1, QL), layer), _res((1, KVL), layer), _res((QL, QFW), 0), _res((KVL, KVW), 0),
                  _row(tm, LANES), _row(tm, LANES), _row(tm, LANES)],
        out_specs=[_row(tm, QL), _row(tm, KVL), _row(tm, LANES),
                   _acc((QL, QFW)), _acc((KVL, KVW)), _acc((1, QL)), _acc((1, KVL))],
        out_shape=[SDS((T, QL), BF16), SDS((T, KVL), BF16), SDS((T, LANES), BF16),
                   SDS((QL, QFW), F32), SDS((KVL, KVW), F32), SDS((1, QL), F32), SDS((1, KVL), F32)],
        scratch_shapes=[pltpu.VMEM((tm, QFW), BF16), pltpu.VMEM((tm, KVW), BF16)],
        compiler_params=_cp("arbitrary"))(dq, dk, dv, z, z, gq, gkv, wq, wkv, tc, ts1, ts2)


def _bwd_in(pieces, x, g, dres, w, tm, layer):
    T = x.shape[0]
    grid = (T // tm,)
    widths = [pc.shape[1] for pc in pieces]
    assert sum(widths) == ZW
    n_p = len(pieces)

    def body(*refs):
        p_refs, (x_ref, g_ref, r_ref, w_ref, dx_ref, dz_ref, dg_ref) = refs[:n_p], refs[n_p:]

        @pl.when(pl.program_id(0) == 0)
        def _():
            dg_ref[...] = jnp.zeros(dg_ref.shape, F32)

        off = 0
        for ref, wd in zip(p_refs, widths):
            dz_ref[:, off:off + wd] = ref[...].astype(BF16)
            off += wd
        xf, gf = x_ref[...], g_ref[...]
        r = _rstd(xf)
        n = xf * r
        dx, dgr = _norm_bwd(_nt(dz_ref[...], w_ref[...]), n, r, gf)
        dx_ref[...] = r_ref[...] + dx
        dg_ref[...] += jnp.sum(dgr, axis=0, keepdims=True)

    return pl.pallas_call(
        body, grid=grid, name="bwd_in",
        in_specs=[_row(tm, wd) for wd in widths] + [_row(tm, D), _res((1, D), layer), _row(tm, D),
                                                    _res((D, ZW), 0)],
        out_specs=[_row(tm, D), _row(tm, ZW), _acc((1, D))],
        out_shape=[SDS((T, D), F32), SDS((T, ZW), BF16), SDS((1, D), F32)],
        compiler_params=_cp("arbitrary"))(*pieces, x, g, dres, w)


def _wgrad_in(hb, dzb, tm, ride=None):
    T = hb.shape[0]
    half = ZW // 2
    grid = (2, T // tm)

    def body(h_ref, dz_ref, dw_ref):
        @pl.when(pl.program_id(1) == 0)
        def _():
            dw_ref[...] = jnp.zeros(dw_ref.shape, F32)

        dw_ref[...] += _tn(h_ref[...], dz_ref[...])

    r_in, r_out, r_shape, r_scratch, r_args = _ride_args(ride)
    out = pl.pallas_call(
        _riding(ride, body, 2, 1, grid), grid=grid, name="wgrad_in_ride" if ride else "wgrad_in",
        in_specs=[pl.BlockSpec((tm, D), lambda j, t: (t, 0)), pl.BlockSpec((tm, half), lambda j, t: (t, j))] + r_in,
        out_specs=[pl.BlockSpec((D, half), lambda j, t: (0, j))] + r_out,
        out_shape=[SDS((D, ZW), F32)] + r_shape, scratch_shapes=r_scratch,
        compiler_params=_cp("arbitrary", "arbitrary"))(hb, dzb, *r_args)
    return out


IN_PIECES = ((0, 512, Z_AQ), (512, 128, Z_AK), (640, 128, Z_AV), (768, 512, Z_AG), (1280, 256, Z_QD),
             (1536, 128, Z_KVD), (1664, MLA_ROPE, Z_KR + MLA_NOPE), (1696, 512, Z_BG), (2208, 1024, Z_MA),
             (3232, 1024, Z_MB))
WIDE_W = IN_W // N_DEV


def _column_runs():
    runs = []
    for start, width, kstart in IN_PIECES:
        col = start
        while col < start + width:
            dev = col // WIDE_W
            stop = min(start + width, (dev + 1) * WIDE_W)
            runs.append((dev, col - dev * WIDE_W, stop - col, kstart + col - start))
            col = stop
    return runs


def _win_layout(blocks, tm):
    runs = _column_runs()

    def body(g_ref, o_ref):
        o_ref[:, Z_KR:Z_KR + LANES] = jnp.zeros((tm, LANES), o_ref.dtype)
        for dev, lo, n, k in runs:
            o_ref[:, k:k + n] = g_ref[dev, :, lo:lo + n]

    return pl.pallas_call(
        body, grid=(D // tm,), name="win_layout",
        in_specs=[pl.BlockSpec((N_DEV, None, tm, WIDE_W), lambda i: (0, 0, i, 0))],
        out_specs=pl.BlockSpec((None, tm, ZW), lambda i: (0, i, 0)),
        out_shape=SDS((1, D, ZW), blocks.dtype),
        compiler_params=_cp("parallel"))(blocks)


def _win_grad_layout(dw, tm):
    runs = _column_runs()

    def body(g_ref, o_ref):
        for dev, lo, n, k in runs:
            o_ref[dev, :, lo:lo + n] = g_ref[:, k:k + n]

    return pl.pallas_call(
        body, grid=(D // tm,), name="win_grad_layout",
        in_specs=[_row(tm, ZW)],
        out_specs=pl.BlockSpec((N_DEV, None, tm, WIDE_W), lambda i: (0, 0, i, 0)),
        out_shape=SDS((N_DEV, 1, D, WIDE_W), F32),
        compiler_params=_cp("parallel"))(dw)


def _wuq_to_kernel(w):
    w = w.reshape(w.shape[:-1] + (MLA_H, MLA_QK))
    w = jnp.pad(w, [(0, 0)] * (w.ndim - 1) + [(0, LANES - MLA_QK)])
    return w.reshape(w.shape[:-2] + (QFW,))


def _wuq_from_kernel(g):
    g = g.reshape(g.shape[:-1] + (MLA_H, LANES))[..., :MLA_QK]
    return g.reshape(g.shape[:-2] + (MLA_H * MLA_QK,))


def _wukv_to_kernel(w):
    w = w.reshape(w.shape[:-1] + (MLA_H, MLA_NOPE + MLA_V))
    k = jnp.pad(w[..., :MLA_NOPE], [(0, 0)] * (w.ndim - 1) + [(0, LANES - MLA_NOPE)])
    v = w[..., MLA_NOPE:]
    return jnp.concatenate([k.reshape(k.shape[:-2] + (QFW,)), v.reshape(v.shape[:-2] + (MLA_H * MLA_V,))], axis=-1)


def _wukv_from_kernel(g):
    k = g[..., :QFW].reshape(g.shape[:-1] + (MLA_H, LANES))[..., :MLA_NOPE]
    v = g[..., QFW:].reshape(g.shape[:-1] + (MLA_H, MLA_V))
    kv = jnp.concatenate([k, v], axis=-1)
    return kv.reshape(kv.shape[:-2] + (MLA_H * (MLA_NOPE + MLA_V),))


def _rope_tables(pos):
    half = MLA_ROPE // 2
    inv = 10000.0 ** (-jnp.arange(0, MLA_ROPE, 2, dtype=F32) / MLA_ROPE)
    ang = pos.astype(F32)[:, None] * inv
    cos, sin = jnp.cos(ang), jnp.sin(ang)
    one = jnp.ones((pos.shape[0], MLA_NOPE), F32)
    zero = lambda n: jnp.zeros((pos.shape[0], n), F32)
    tc = jnp.concatenate([one, cos, cos, one[:, :LANES - MLA_QK]], axis=1)
    ts1 = jnp.concatenate([zero(MLA_NOPE + half), sin, zero(LANES - MLA_QK)], axis=1)
    ts2 = jnp.concatenate([zero(MLA_NOPE), -sin, zero(LANES - MLA_NOPE - half)], axis=1)
    return tc, ts1, ts2


def _local_step(x, p, positions, loss_target, small, wts, plan=None):
    nb, seq, _ = x.shape
    T = nb * seq
    tm = min(512, T)
    tl = min(1024, T)
    tq = min(512, seq)
    xf = x.reshape(T, D)
    pos = positions.reshape(T)
    posf = pos.astype(F32)
    pos_col, pos_row = posf.reshape(T, 1), posf.reshape(T // BLK, 1, BLK)
    tc, ts1, ts2 = _rope_tables(pos)

    wts, sm = list(wts), small
    pl_in = p.reshape(DEPTH, T, PLE)
    saved = []
    for i in range(DEPTH):
        riding = plan is not None and i == 0
        w = wts[i]
        z, hb, *got = _fwd_in(xf, sm["g_mix"], w["w_in"], tm, i,
                              ride=_Ride("gather", plan["rows0"]) if riding else None)
        if riding:
            w = wts[0] = dict(w, **plan["row_weights"](got[0]))
        oa, lse_a = _swa_fwd(sm["sink"], z, pos_col, pos_row, nb, seq, i)
        qf, kf, v = _fwd_prep(z, sm["g_q"], sm["g_kv"], w["w_uq"], w["w_ukv"], tc, ts1, ts2, tl, i)
        ob, lse_b, *got = _mla_fwd(qf, kf, v, nb, seq, tq, gather=plan["blocks1"] if riding else ())
        if riding:
            wts.append(dict(w_in=plan["w_in"](got[0]), **plan["row_weights"](got[1])))
        x1 = _fwd_merge(xf, oa, ob, z, w["w_br_a"], w["w_br_b"], w["w_out"], tm, i)
        x2, pg, pp = _fwd_ple(x1, pl_in, sm["g_ple"], w["w_ple_gate"], w["w_ple_proj"], tl, i)
        saved.append(dict(x=xf, z=z, hb=hb, oa=oa, lse_a=lse_a, qf=qf, kf=kf, v=v, ob=ob, lse_b=lse_b,
                          x1=x1, pg=pg, pp=pp))
        xf = x2

    dx, dg_final, loss = _loss_head(xf, small["g_final"], loss_target.reshape(T, D), tl)

    grads = [None] * DEPTH
    exchanged = {}
    for i in reversed(range(DEPTH)):
        riding = plan is not None and i == 0
        sv, w = saved[i], wts[i]
        pay = plan["payload"](grads[1]) if riding else []
        dx1, dwpg, dwpp, dg_ple, *got = _bwd_ple(dx, sv["x1"], sv["pg"], sv["pp"], pl_in, sm["g_ple"],
                                                 w["w_ple_gate"], tm, i, ride=_Ride("swap", pay) if riding else None)
        doa, dob, dag, dbg, dma, dmb, dsum_a, dsum_b, dwa, dwb, dwo = _bwd_merge(
            dx1, sv["oa"], sv["ob"], sv["z"], w["w_br_a"], w["w_br_b"], w["w_out"], tm, i)
        dq_b, dk_b, dv_b, *exchanged["layer1"] = _mla_bwd(
            sv["qf"], sv["kf"], sv["v"], dob, sv["lse_b"], dsum_b.reshape(MLA_H // 2, 2, T), nb, seq, tq,
            exchange=plan["add"](pay, got) if riding else ())
        dqd, dkvd, dkr, dwq, dwkv, dgq, dgkv = _bwd_prep(dq_b, dk_b, dv_b, sv["z"], sm["g_q"], sm["g_kv"],
                                                         w["w_uq"], w["w_ukv"], tc, ts1, ts2, tl, i)
        g = dict(w_uq=_wuq_from_kernel(dwq), w_ukv=_wukv_from_kernel(dwkv), w_br_a=dwa, w_br_b=dwb, w_out=dwo,
                 w_ple_gate=dwpg, w_ple_proj=dwpp)
        pay = [plan["rows_payload"](g)] if riding else []
        dq_a, dkv_a, dsink, *got = _swa_bwd(sm["sink"], sv["z"], pos_col, pos_row, doa, sv["lse_a"], dsum_a, nb, seq,
                                            i, ride=_Ride("swap", pay) if riding else None)
        dx, dzb, dg_mix = _bwd_in([dma, dmb, dq_a, dag, dbg, dqd, dkv_a, dkvd, dkr], sv["x"], sm["g_mix"], dx1,
                                  w["w_in"], tm, i)
        dwin, *exchanged["rows0"] = _wgrad_in(sv["hb"], dzb, tm,
                                              ride=_Ride("exchange", plan["add"](pay, got)) if riding else None)
        g.update(g_mix=dg_mix[0], w_in=dwin, sink=dsink[0, :SWA_H], g_q=dgq[0], g_kv=dgkv[0], g_ple=dg_ple[0])
        grads[i] = g
    return loss, dx.reshape(nb, seq, D), grads, dg_final[0], exchanged


def _row_weights(rows):
    blocks = _unpack_rows(rows)
    out = {n: _join(n, blocks[n]) for n, _ in ROWS_PIECES}
    out.update(w_uq=_wuq_to_kernel(out["w_uq"]), w_ukv=_wukv_to_kernel(out["w_ukv"]))
    return out


def _small_params(g_mix, sink, g_q, g_kv, g_ple, g_final):
    return dict(g_mix=g_mix[:, None], sink=sink, g_q=g_q[:, None], g_kv=g_kv[:, None], g_ple=g_ple[:, None],
                g_final=g_final[None])


UQ_W = MLA_H * MLA_QK // N_DEV
ROWS_PIECES = (("w_uq", QL), ("w_ukv", KVL), ("w_br_a", 512), ("w_br_b", 512), ("w_out", D), ("w_ple_gate", D),
               ("w_ple_proj", PLE))
SMALL = (("g_mix", (DEPTH, D)), ("sink", (DEPTH, SWA_H)), ("g_q", (DEPTH, QL)), ("g_kv", (DEPTH, KVL)),
         ("g_ple", (DEPTH, D)), ("g_final", (D,)))
VEC_ROWS = 48
ROWS_N = sum(r for _, r in ROWS_PIECES)
WIDE_TILE, ROWS_TILE = 256, ROWS_N // 2


def _to_rows(name, a):
    if name == "w_uq":
        a = jnp.pad(a, [(0, 0)] * (a.ndim - 1) + [(0, LANES - UQ_W)])
    return a.reshape(a.shape[:-2] + (-1, LANES))


def _from_rows(name, r):
    if name in ("w_out", "w_ple_gate"):
        return r.reshape(r.shape[:-2] + (D // N_DEV, D))
    return r[..., :UQ_W] if name == "w_uq" else r


def _pack_rows(blocks):
    return jnp.concatenate([_to_rows(n, blocks[n]) for n, _ in ROWS_PIECES], axis=-2)


def _unpack_rows(rows):
    blocks, off = {}, 0
    for n, r in ROWS_PIECES:
        blocks[n] = _from_rows(n, rows[..., off:off + r, :])
        off += r
    return blocks


def _pack_vec(vectors, loss=None):
    parts = [vectors[n].reshape(-1) for n, _ in SMALL] + ([] if loss is None else [loss.reshape(1)])
    vec = jnp.concatenate(parts)
    return jnp.pad(vec, (0, VEC_ROWS * LANES - vec.shape[0])).reshape(1, VEC_ROWS, LANES)


def _unpack_vec(vec):
    vec = vec.reshape(-1)
    vectors, off = {}, 0
    for n, shp in SMALL:
        size = 1
        for s in shp:
            size *= s
        vectors[n] = vec[off:off + size].reshape(shp)
        off += size
    return vectors, vec[off]


def _join(name, blocks):
    if name in ("w_out", "w_ple_gate"):
        return jnp.moveaxis(blocks, 0, 1).reshape(blocks.shape[1], -1, blocks.shape[-1])
    return jnp.moveaxis(blocks, 0, 2).reshape(blocks.shape[1], blocks.shape[2], -1)


def _split(name, full):
    if name in ("w_out", "w_ple_gate"):
        return jnp.moveaxis(full.reshape(full.shape[0], N_DEV, -1, full.shape[-1]), 1, 0)
    return jnp.moveaxis(full.reshape(full.shape[0], full.shape[1], N_DEV, -1), 2, 0)


MESH_ID = pl.DeviceIdType.MESH
ANY = pl.BlockSpec(memory_space=pl.ANY)


def _place():
    return lax.axis_index("x"), lax.axis_index("y"), lax.axis_index("c")


def _all_gather(blocks):
    n = len(blocks)

    def body(*refs):
        start, forward, finish = _gather_phases(refs[:n], refs[n:2 * n], *refs[2 * n:])
        start()
        forward()
        finish()

    return pl.pallas_call(
        body, name="all_gather_weights", out_shape=_gather_out(blocks),
        in_specs=[ANY] * n, out_specs=[ANY] * n, scratch_shapes=_gather_sems(n))(*blocks)


def _gather_out(blocks):
    return [SDS((N_DEV,) + b.shape, b.dtype) for b in blocks]


def _gather_sems(n):
    return [pltpu.SemaphoreType.DMA((7 * n,)), pltpu.SemaphoreType.DMA((7 * n,)), pltpu.SemaphoreType.DMA((n,))]


def _gather_phases(x_refs, out_refs, send_sems, recv_sems, local_sems):
    n = len(x_refs)
    x, y, c = _place()
    me, sibling = (x, y, c), (x, y, 1 - c)
    chips = [(1 - x, y), (x, 1 - y), (1 - x, 1 - y)]

    def slot(a, px, py, pc):
        return out_refs[a].at[4 * px + 2 * py + pc]

    def copy(a, k, blk, to, src=None):
        return pltpu.make_async_remote_copy(
            src_ref=slot(a, *blk) if src is None else src, dst_ref=slot(a, *blk),
            send_sem=send_sems.at[7 * a + k], recv_sem=recv_sems.at[7 * a + k], device_id=to,
            device_id_type=MESH_ID)

    def mine():
        return [pltpu.make_async_copy(x_refs[a], slot(a, *me), local_sems.at[a]) for a in range(n)]

    def first():
        out = []
        for a in range(n):
            out += [copy(a, 0, me, sibling, src=x_refs[a])]
            out += [copy(a, 1 + j, me, (*chip, c), src=x_refs[a]) for j, chip in enumerate(chips)]
        return out

    def passed():
        return [copy(a, 4 + j, (*chip, c), sibling) for j, chip in enumerate(chips) for a in range(n)]

    def start():
        for cp in mine() + first():
            cp.start()

    def forward():
        for j, chip in enumerate(chips):
            for a in range(n):
                copy(a, 1 + j, (*chip, c), me).wait_recv()
                copy(a, 4 + j, (*chip, c), sibling).start()

    def finish():
        for a in range(n):
            copy(a, 0, sibling, me).wait_recv()
            for j, chip in enumerate(chips):
                copy(a, 4 + j, (*chip, 1 - c), me).wait_recv()
        for cp in first() + passed():
            cp.wait_send()
        for cp in mine():
            cp.wait()

    return start, forward, finish


def _swap_sibling(arrs):
    n = len(arrs)

    def body(*refs):
        start, finish = _swap_phases(refs[:n], refs[n:2 * n], *refs[2 * n:])
        start()
        finish()

    return pl.pallas_call(
        body, name="swap_sibling", out_shape=[SDS((a.shape[0],) + a.shape[2:], a.dtype) for a in arrs],
        in_specs=[ANY] * n, out_specs=[ANY] * n, scratch_shapes=_swap_sems(n))(*arrs)


def _swap_sems(n):
    return [pltpu.SemaphoreType.DMA((n,)), pltpu.SemaphoreType.DMA((n,))]


def _swap_phases(a_refs, out_refs, send_sems, recv_sems):
    x, y, c = _place()

    def copies():
        return [pltpu.make_async_remote_copy(
            src_ref=a_refs[a].at[:, 1 - c], dst_ref=out_refs[a], send_sem=send_sems.at[a], recv_sem=recv_sems.at[a],
            device_id=(x, y, 1 - c), device_id_type=MESH_ID) for a in range(len(a_refs))]

    def start():
        for cp in copies():
            cp.start()

    def finish():
        for cp in copies():
            cp.wait()

    return start, finish


def _exchange_chips(arrs):
    n = len(arrs)

    def body(*refs):
        start, finish = _exchange_phases(refs[:n], refs[n:2 * n], *refs[2 * n:])
        start()
        finish()

    return pl.pallas_call(
        body, name="exchange_chips", out_shape=[SDS(a.shape, a.dtype) for a in arrs],
        in_specs=[ANY] * n, out_specs=[ANY] * n, scratch_shapes=_exchange_sems(n))(*arrs)


def _exchange_sems(n):
    return [pltpu.SemaphoreType.DMA((3 * n,)), pltpu.SemaphoreType.DMA((3 * n,)), pltpu.SemaphoreType.DMA((n,))]


def _exchange_phases(p_refs, out_refs, send_sems, recv_sems, local_sems):
    n = len(p_refs)
    x, y, c = _place()
    mine = 2 * x + y
    peers = [(1 - x, y), (x, 1 - y), (1 - x, 1 - y)]

    def local():
        return [pltpu.make_async_copy(p_refs[a].at[mine], out_refs[a].at[mine], local_sems.at[a]) for a in range(n)]

    def copy(a, j, src_chip, dst_chip):
        px, py = peers[j]
        return pltpu.make_async_remote_copy(
            src_ref=p_refs[a].at[src_chip], dst_ref=out_refs[a].at[dst_chip], send_sem=send_sems.at[3 * a + j],
            recv_sem=recv_sems.at[3 * a + j], device_id=(px, py, c), device_id_type=MESH_ID)

    def sends():
        return [copy(a, j, 2 * px + py, mine) for a in range(n) for j, (px, py) in enumerate(peers)]

    def start():
        for cp in local() + sends():
            cp.start()

    def finish():
        for a in range(n):
            for j, (px, py) in enumerate(peers):
                copy(a, j, mine, 2 * px + py).wait_recv()
        for cp in sends():
            cp.wait_send()
        for cp in local():
            cp.wait()

    return start, finish


def _add_mine(g, recv, core, tile, dtype):
    _, _, lead, rows, width = g.shape

    def body(c_ref, g_ref, r_ref, o_ref):
        o_ref[...] = (g_ref[...] + r_ref[...]).astype(dtype)

    spec = pl.BlockSpec((None, None, tile, width), lambda k, l, i, c_ref: (k, l, i, 0))
    return pl.pallas_call(
        body, name="add_sibling", out_shape=SDS(recv.shape, dtype),
        grid_spec=pltpu.PrefetchScalarGridSpec(
            num_scalar_prefetch=1, grid=(g.shape[0], lead, rows // tile),
            in_specs=[pl.BlockSpec((None, None, None, tile, width), lambda k, l, i, c_ref: (k, c_ref[0], l, i, 0)),
                      spec],
            out_specs=spec),
        compiler_params=_cp("parallel", "parallel", "parallel"))(core, g, recv)


def _sum_adamw(parts, w, m, v, tile):
    lead, rows, width = w.shape
    last = rows // tile - 1

    def body(*refs):
        p_refs, (w_ref, m_ref, v_ref, g_ref, d_ref, nm_ref, nv_ref) = refs[:lead], refs[lead:]
        for layer in range(lead):
            @pl.when(pl.program_id(0) == layer)
            def _(p_ref=p_refs[layer]):
                g = ((p_ref[0].astype(F32) + p_ref[1].astype(F32)) + p_ref[2].astype(F32)) + p_ref[3].astype(F32)
                nm = ADAM_B1 * m_ref[...] + (1.0 - ADAM_B1) * g
                nv = ADAM_B2 * v_ref[...] + (1.0 - ADAM_B2) * jnp.square(g)
                m_hat = nm / (1.0 - ADAM_B1 ** ADAM_STEP)
                v_hat = nv / (1.0 - ADAM_B2 ** ADAM_STEP)
                g_ref[...] = g
                nm_ref[...] = nm
                nv_ref[...] = nv
                d_ref[...] = -ADAM_LR * (m_hat / (jnp.sqrt(v_hat) + ADAM_EPS) + ADAM_WD * w_ref[...])

    pspec = lambda layer: pl.BlockSpec(
        (4, None, tile, width),
        lambda l, i: (0, 0, jnp.where(l == layer, i, jnp.where(l > layer, last, 0)), 0))
    spec = pl.BlockSpec((None, tile, width), lambda l, i: (l, i, 0))
    return pl.pallas_call(
        body, grid=(lead, rows // tile), name="sum_adamw",
        in_specs=[pspec(layer) for layer in range(lead)] + [spec, spec, spec],
        out_specs=[spec] * 4, out_shape=[SDS((lead, rows, width), F32)] * 4,
        compiler_params=_cp("arbitrary", "arbitrary"))(*parts, w, m, v)


def kernel(x, p, positions, g_mix, w_in, sink, g_q, w_uq, g_kv, w_ukv, w_br_a, w_br_b, w_out, g_ple, w_ple_gate, w_ple_proj, g_final, loss_target, m_g_mix, m_w_in, m_sink, m_g_q, m_w_uq, m_g_kv, m_w_ukv, m_w_br_a, m_w_br_b, m_w_out, m_g_ple, m_w_ple_gate, m_w_ple_proj, m_g_final, v_g_mix, v_w_in, v_sink, v_g_q, v_w_uq, v_g_kv, v_w_ukv, v_w_br_a, v_w_br_b, v_w_out, v_g_ple, v_w_ple_gate, v_w_ple_proj, v_g_final):
    weights = dict(g_mix=g_mix, w_in=w_in, sink=sink, g_q=g_q, w_uq=w_uq, g_kv=g_kv, w_ukv=w_ukv, w_br_a=w_br_a,
                   w_br_b=w_br_b, w_out=w_out, g_ple=g_ple, w_ple_gate=w_ple_gate, w_ple_proj=w_ple_proj,
                   g_final=g_final)
    mom1 = dict(g_mix=m_g_mix, w_in=m_w_in, sink=m_sink, g_q=m_g_q, w_uq=m_w_uq, g_kv=m_g_kv, w_ukv=m_w_ukv,
                w_br_a=m_w_br_a, w_br_b=m_w_br_b, w_out=m_w_out, g_ple=m_g_ple, w_ple_gate=m_w_ple_gate,
                w_ple_proj=m_w_ple_proj, g_final=m_g_final)
    mom2 = dict(g_mix=v_g_mix, w_in=v_w_in, sink=v_sink, g_q=v_g_q, w_uq=v_w_uq, g_kv=v_g_kv, w_ukv=v_w_ukv,
                w_br_a=v_w_br_a, w_br_b=v_w_br_b, w_out=v_w_out, g_ple=v_g_ple, w_ple_gate=v_w_ple_gate,
                w_ple_proj=v_w_ple_proj, g_final=v_g_final)
    assert DEPTH == 2
    wide = lambda d: d["w_in"]
    rows = lambda d: _pack_rows(d)
    core = lax.axis_index("c").astype(jnp.int32).reshape(1)

    w16 = [wide(weights).astype(BF16), rows(weights).astype(BF16)]
    wts0 = dict(w_in=_win_layout(_all_gather([w16[0][:1]])[0], 256))
    small = _small_params(g_mix, sink, g_q, g_kv, g_ple, g_final)

    def wide_payload(g):
        return _win_grad_layout(g["w_in"], 256).reshape(N_DEV // 2, 2, 1, D, WIDE_W)

    def rows_payload(g):
        return _pack_rows({n: _split(n, g[n][None]) for n, _ in ROWS_PIECES}).reshape(N_DEV // 2, 2, 1, ROWS_N, LANES)

    def add(pay, got):
        tiles = {D: (WIDE_TILE, BF16), ROWS_N: (ROWS_TILE, BF16), VEC_ROWS: (VEC_ROWS, F32)}
        return [_add_mine(a, b, core, *tiles[a.shape[-2]]) for a, b in zip(pay, got)]

    plan = dict(rows0=[w16[1][:1]], blocks1=[a[1:] for a in w16], w_in=lambda blocks: _win_layout(blocks, 256),
                row_weights=_row_weights, payload=lambda g: [wide_payload(g), rows_payload(g)],
                rows_payload=rows_payload, add=add)
    loss, grad_x, grads, dg_final, rode = _local_step(x, p, positions, loss_target, small, [wts0], plan)

    vectors = {n: jnp.stack([grads[i][n] for i in range(DEPTH)]) for n, _ in SMALL[:-1]}
    vectors["g_final"] = dg_final
    pay = [wide_payload(grads[0]),
           jnp.broadcast_to(_pack_vec(vectors, loss[0, 0]), (N_DEV // 2, 2, 1, VEC_ROWS, LANES))]
    parts_wide0, parts_vec = _exchange_chips(add(pay, _swap_sibling(pay)))
    out_wide = _sum_adamw([parts_wide0, rode["layer1"][0]], wide(weights), wide(mom1), wide(mom2), WIDE_TILE)
    out_rows = _sum_adamw([rode["rows0"][0], rode["layer1"][1]], rows(weights), rows(mom1), rows(mom2), ROWS_TILE)
    out_vec = _sum_adamw([parts_vec], _pack_vec(weights), _pack_vec(mom1), _pack_vec(mom2), VEC_ROWS)

    outs = []
    for ow, orow, ovec in zip(out_wide, out_rows, out_vec):
        named = _unpack_rows(orow)
        named.update(_unpack_vec(ovec)[0])
        named["w_in"] = ow
        outs += [named[n] for n in weights]
    loss = _unpack_vec(out_vec[0])[1]
    return (loss, grad_x, *outs)
```

```python
import functools

import jax
import jax.numpy as jnp
from jax import lax
from jax.experimental import pallas as pl
from jax.experimental.pallas import tpu as pltpu

F32, BF16 = jnp.float32, jnp.bfloat16
SDS = jax.ShapeDtypeStruct

D = 1024
DEPTH = 2
PLE = 256
BLK = 128
EPS = 1e-6
NEG = -1e30
SWA_H, SWA_KV, SWA_DH = 8, 2, 64
MLA_H, MLA_NOPE, MLA_ROPE, MLA_V = 8, 64, 32, 64
MLA_QK = MLA_NOPE + MLA_ROPE
QL, KVL = 256, 128
IN_W = 4256
N_DEV = 8

V7X_VMEM_BYTES = 64 * 1024 * 1024
LANES = 128
VMEM_LIMIT = V7X_VMEM_BYTES * 7 // 8

ZW = 4352
Z_MA, Z_MB, Z_AQ, Z_AG, Z_BG, Z_QD, Z_AK, Z_AV, Z_KVD, Z_KR = 0, 1024, 2048, 2560, 3072, 3584, 3840, 3968, 4096, 4224
QFW = MLA_H * LANES
KVW = QFW + MLA_H * MLA_V
MLA_SCALE = MLA_QK ** -0.5
LOG2E = 1.4426950408889634
SWA_SCALE = SWA_DH ** -0.5
ROLL_UP, ROLL_DOWN = MLA_ROPE // 2, LANES - MLA_ROPE // 2

ADAM_LR, ADAM_B1, ADAM_B2, ADAM_EPS, ADAM_WD, ADAM_STEP = 0.001, 0.9, 0.999, 1e-08, 0.01, 10


def _cp(*sem):
    return pltpu.CompilerParams(dimension_semantics=sem, vmem_limit_bytes=VMEM_LIMIT)


def _row(tm, w, col=0):
    return pl.BlockSpec((tm, w), lambda i: (i, col))


def _res(shape, layer=None):
    if layer is None:
        return pl.BlockSpec(shape, lambda *_: (0,) * len(shape), pipeline_mode=pl.Buffered(1))
    return pl.BlockSpec((None,) + shape, lambda *_: (layer,) + (0,) * len(shape), pipeline_mode=pl.Buffered(1))


def _acc(shape):
    return pl.BlockSpec(shape, lambda *_: (0,) * len(shape))


def _rstd(xf):
    return lax.rsqrt(jnp.mean(xf * xf, axis=-1, keepdims=True) + EPS)


def _norm_bwd(dh, n, r, g):
    dn = dh * g
    return r * (dn - n * jnp.mean(dn * n, axis=-1, keepdims=True)), dh * n


def _nt(a, b):
    return lax.dot_general(a, b, (((1,), (1,)), ((), ())), preferred_element_type=F32)


def _tn(a, b):
    return lax.dot_general(a, b, (((0,), (0,)), ((), ())), preferred_element_type=F32)


def _nn(a, b):
    return jnp.dot(a, b, preferred_element_type=F32)


def _sig(x):
    return jax.nn.sigmoid(x)


def _rope(t, c, s1, s2):
    return t * c + pltpu.roll(t, ROLL_UP, 1) * s1 + pltpu.roll(t, ROLL_DOWN, 1) * s2


def _rope_t(d, c, s1, s2):
    return d * c + pltpu.roll(d * s1, ROLL_DOWN, 1) + pltpu.roll(d * s2, ROLL_UP, 1)


def _fwd_in(x, g, w, tm, layer, ride=None):
    T = x.shape[0]
    grid = (T // tm,)

    def body(x_ref, g_ref, w_ref, z_ref, h_ref):
        xf = x_ref[...]
        h = ((xf * _rstd(xf)) * g_ref[...]).astype(BF16)
        h_ref[...] = h
        z_ref[...] = _nn(h, w_ref[...])

    r_in, r_out, r_shape, r_scratch, r_args = _ride_args(ride)
    return pl.pallas_call(
        _riding(ride, body, 3, 2, grid), grid=grid, name="fwd_in_ride" if ride else "fwd_in",
        in_specs=[_row(tm, D), _res((1, D), layer), _res((D, ZW), 0)] + r_in,
        out_specs=[_row(tm, ZW), _row(tm, D)] + r_out,
        out_shape=[SDS((T, ZW), F32), SDS((T, D), BF16)] + r_shape, scratch_shapes=r_scratch,
        compiler_params=_cp("arbitrary"))(x, g, w, *r_args)


def _fwd_prep(z, gq, gkv, wq, wkv, tc, ts1, ts2, tm, layer):
    T = z.shape[0]

    def body(qd_ref, kvd_ref, kr_ref, gq_ref, gkv_ref, wq_ref, wkv_ref, c_ref, s1_ref, s2_ref, q_ref, k_ref, v_ref):
        qd, kvd = qd_ref[...], kvd_ref[...]
        hq = ((qd * _rstd(qd)) * gq_ref[...]).astype(BF16)
        hkv = ((kvd * _rstd(kvd)) * gkv_ref[...]).astype(BF16)
        qf = _nn(hq, wq_ref[...])
        kvf = _nn(hkv, wkv_ref[...])
        c, s1, s2 = c_ref[...], s1_ref[...], s2_ref[...]
        krb = _rope(kr_ref[...], c, s1, s2)
        for h in range(MLA_H):
            sl = slice(LANES * h, LANES * (h + 1))
            q_ref[:, sl] = _rope(qf[:, sl], c, s1, s2).astype(BF16)
            k_ref[:, sl] = (kvf[:, sl] + krb).astype(BF16)
        v_ref[...] = kvf[:, QFW:].astype(BF16)

    return pl.pallas_call(
        body, grid=(T // tm,), name="fwd_prep",
        in_specs=[_row(tm, QL, Z_QD // QL), _row(tm, KVL, Z_KVD // KVL), _row(tm, LANES, Z_KR // LANES),
                  _res((1, QL), layer), _res((1, KVL), layer), _res((QL, QFW), 0), _res((KVL, KVW), 0),
                  _row(tm, LANES), _row(tm, LANES), _row(tm, LANES)],
        out_specs=[_row(tm, QFW), _row(tm, QFW), _row(tm, MLA_H * MLA_V)],
        out_shape=[SDS((T, QFW), BF16), SDS((T, QFW), BF16), SDS((T, MLA_H * MLA_V), BF16)],
        compiler_params=_cp("parallel"))(z, z, z, gq, gkv, wq, wkv, tc, ts1, ts2)


def _grid_ends(grid):
    ids = [pl.program_id(a) for a in range(len(grid))]
    inner_first = functools.reduce(jnp.logical_and, [i == 0 for i in ids[1:]], True)
    last = functools.reduce(jnp.logical_and, [i == g - 1 for i, g in zip(ids, grid)])
    return (ids[0] == 0) & inner_first, (ids[0] == grid[0] // 2) & inner_first, last


class _Ride:
    def __init__(self, kind, arrays):
        self.kind, self.arrays, self.n = kind, list(arrays), len(arrays)

    def out_shape(self):
        if self.kind == "gather":
            return _gather_out(self.arrays)
        if self.kind == "swap":
            return [SDS((a.shape[0],) + a.shape[2:], a.dtype) for a in self.arrays]
        return [SDS(a.shape, a.dtype) for a in self.arrays]

    def sems(self):
        if self.kind == "gather":
            return _gather_sems(self.n)
        if self.kind == "swap":
            return _swap_sems(self.n)
        return _exchange_sems(self.n)

    def phases(self, in_refs, out_refs, *sems):
        if self.kind == "gather":
            return _gather_phases(in_refs, out_refs, *sems)
        start, finish = (_swap_phases if self.kind == "swap" else _exchange_phases)(in_refs, out_refs, *sems)
        return start, None, finish


def _riding(ride, body, n_in, n_out, grid):
    if ride is None:
        return body
    n, n_sems = ride.n, len(ride.sems())

    def wrapped(*refs):
        ins, r_in = refs[:n_in], refs[n_in:n_in + n]
        outs, r_out = refs[n_in + n:n_in + n + n_out], refs[n_in + n + n_out:n_in + 2 * n + n_out]
        rest = refs[n_in + 2 * n + n_out:]
        scratch, sems = rest[:len(rest) - n_sems], rest[len(rest) - n_sems:]
        start, middle, finish = ride.phases(r_in, r_out, *sems)
        at_first, at_middle, at_last = _grid_ends(grid)
        pl.when(at_first)(start)
        if middle is not None:
            pl.when(at_middle)(middle)
        body(*ins, *outs, *scratch)
        pl.when(at_last)(finish)

    return wrapped


def _ride_args(ride):
    if ride is None:
        return [], [], [], [], []
    return [ANY] * ride.n, [ANY] * ride.n, ride.out_shape(), ride.sems(), ride.arrays


def _mla_fwd(qf, kf, v, nb, seq, tq, gather=()):
    T = qf.shape[0]
    nq = seq // tq
    pw = 2 * LANES
    pairs = [(qi, ki) for qi in range(nq) for ki in range(qi + 1)]
    qi_tab = jnp.array([qk[0] for qk in pairs], jnp.int32)
    ki_tab = jnp.array([qk[1] for qk in pairs], jnp.int32)
    grid = (nb, MLA_H // 2, len(pairs))
    n_g = len(gather)

    def body(qi_ref, ki_ref, q_ref, k_ref, v_ref, *rest):
        x_refs, (o_ref, lse_ref), got_refs = rest[:n_g], rest[n_g:n_g + 2], rest[n_g + 2:2 * n_g + 2]
        (m_s, l_s, acc_s), sems = rest[2 * n_g + 2:2 * n_g + 5], rest[2 * n_g + 5:]
        qi, ki = qi_ref[pl.program_id(2)], ki_ref[pl.program_id(2)]
        if n_g:
            start, forward, finish = _gather_phases(x_refs, got_refs, *sems)
            at_first, at_middle, at_last = _grid_ends(grid)
            pl.when(at_first)(start)
            pl.when(at_middle)(forward)

        @pl.when(ki == 0)
        def _():
            m_s[...] = jnp.full(m_s.shape, NEG, F32)
            l_s[...] = jnp.zeros(l_s.shape, F32)
            acc_s[...] = jnp.zeros(acc_s.shape, F32)

        def step(masked):
            parts = [(0, tq // 2, tq // 2), (tq // 2, tq, tq)] if masked else [(0, tq, tq)]
            work = [(j, a, b, kh) for j in range(2) for a, b, kh in parts]
            ss = []
            for j, a, b, kh in work:
                wide = slice(LANES * j, LANES * (j + 1))
                s = _nt(k_ref[:kh, wide], q_ref[a:b, wide]) * (MLA_SCALE * LOG2E)
                if masked:
                    keys = lax.broadcasted_iota(jnp.int32, (kh, b - a), 0)
                    queries = a + lax.broadcasted_iota(jnp.int32, (kh, b - a), 1)
                    s = jnp.where(keys <= queries, s, NEG)
                ss.append(s)
            ps, alphas = [], []
            for (j, a, b, kh), s in zip(work, ss):
                m_prev = m_s[j, :, a:b]
                m_new = jnp.maximum(m_prev, jnp.max(s, axis=0, keepdims=True))
                alpha = jnp.exp2(m_prev - m_new)
                p = jnp.exp2(s - m_new)
                l_s[j, :, a:b] = alpha * l_s[j, :, a:b] + jnp.sum(p, axis=0, keepdims=True)
                m_s[j, :, a:b] = m_new
                ps.append(p.astype(BF16))
                alphas.append(alpha)
            for (j, a, b, kh), p, alpha in zip(work, ps, alphas):
                rows = slice(MLA_V * j, MLA_V * (j + 1))
                acc_s[rows, a:b] = alpha * acc_s[rows, a:b] + _tn(v_ref[:kh, rows], p)

        @pl.when(ki < qi)
        def _():
            step(False)

        @pl.when(ki == qi)
        def _():
            step(True)
            for j in range(2):
                rows = slice(MLA_V * j, MLA_V * (j + 1))
                acc_s[rows, :] = acc_s[rows, :] / l_s[j]
                lse_ref[j:j + 1, :] = m_s[j] + jnp.log2(l_s[j])
            o_ref[...] = acc_s[...].T

        if n_g:
            pl.when(at_last)(finish)

    q_map = lambda b, hp, s, qi_ref, ki_ref: (b * nq + qi_ref[s], hp)
    kv_map = lambda b, hp, s, qi_ref, ki_ref: (b * nq + ki_ref[s], hp)
    return pl.pallas_call(
        body, name="mla_fwd_gather" if n_g else "mla_fwd",
        grid_spec=pltpu.PrefetchScalarGridSpec(
            num_scalar_prefetch=2, grid=grid,
            in_specs=[pl.BlockSpec((tq, pw), q_map), pl.BlockSpec((tq, pw), kv_map),
                      pl.BlockSpec((tq, LANES), kv_map)] + [ANY] * n_g,
            out_specs=[pl.BlockSpec((tq, LANES), q_map),
                       pl.BlockSpec((None, 2, tq), lambda b, hp, s, qi_ref, ki_ref: (hp, 0, b * nq + qi_ref[s]))]
            + [ANY] * n_g,
            scratch_shapes=[pltpu.VMEM((2, 1, tq), F32), pltpu.VMEM((2, 1, tq), F32), pltpu.VMEM((LANES, tq), F32)]
            + (_gather_sems(n_g) if n_g else [])),
        out_shape=[SDS((T, MLA_H * MLA_V), F32), SDS((MLA_H // 2, 2, T), F32)] + _gather_out(gather),
        compiler_params=_cp("arbitrary", "arbitrary", "arbitrary"))(qi_tab, ki_tab, qf, kf, v, *gather)


def _swa_specs(nblk):
    cur = lambda b, n: (b * nblk + n, 0)
    prev = lambda b, n: (b * nblk + jnp.maximum(n - 1, 0), 0)
    kvc = Z_AK // (2 * BLK)
    return [pl.BlockSpec(memory_space=pltpu.SMEM),
            pl.BlockSpec((BLK, 512), lambda b, n: (b * nblk + n, Z_AQ // 512)),
            pl.BlockSpec((BLK, 2 * BLK), lambda b, n: (b * nblk + n, kvc)),
            pl.BlockSpec((BLK, 2 * BLK), lambda b, n: (b * nblk + jnp.maximum(n - 1, 0), kvc)),
            pl.BlockSpec((BLK, 1), cur),
            pl.BlockSpec((BLK, 1), prev),
            pl.BlockSpec((1, 1, BLK), lambda b, n: (b * nblk + n, 0, 0))]


def _swa_scores(n, q_ref, kvc_ref, kvp_ref, pcc_ref, pcp_ref, pr_ref):
    kv = jnp.concatenate([kvp_ref[...], kvc_ref[...]], axis=0)
    kb, vb = kv[:, :BLK].astype(BF16), kv[:, BLK:].astype(BF16)
    dist = pr_ref[0] - jnp.concatenate([pcp_ref[...], pcc_ref[...]], axis=0)
    key = lax.broadcasted_iota(jnp.int32, (2 * BLK, BLK), 0)
    qry = lax.broadcasted_iota(jnp.int32, (2 * BLK, BLK), 1)
    valid = (key > qry) & (key <= qry + BLK) & ((key >= BLK) | (n > 0))

    def scores(h):
        g = h // (SWA_H // SWA_KV)
        qh = q_ref[:, SWA_DH * h:SWA_DH * (h + 1)].astype(BF16)
        s = _nt(kb[:, SWA_DH * g:SWA_DH * (g + 1)], qh) * (SWA_SCALE * LOG2E) - (2.0 ** -(h + 1) * LOG2E) * dist
        return qh, jnp.where(valid, s, NEG)

    return kb, vb, scores


def _swa_fwd(sink, z, pos_col, pos_row, nb, seq, layer):
    T = z.shape[0]
    nblk = seq // BLK

    def body(sink_ref, q_ref, kvc_ref, kvp_ref, pcc_ref, pcp_ref, pr_ref, o_ref, lse_ref):
        kb, vb, scores = _swa_scores(pl.program_id(1), q_ref, kvc_ref, kvp_ref, pcc_ref, pcp_ref, pr_ref)
        ss = [scores(h)[1] for h in range(SWA_H)]
        es, dens = [], []
        for h in range(SWA_H):
            sk = sink_ref[layer, h] * LOG2E
            m = jnp.maximum(jnp.max(ss[h], axis=0, keepdims=True), sk)
            e = jnp.exp2(ss[h] - m)
            den = jnp.sum(e, axis=0, keepdims=True) + jnp.exp2(sk - m)
            lse_ref[h:h + 1, :] = m + jnp.log2(den)
            es.append(e.astype(BF16))
            dens.append(den)
        outs = []
        for h in range(SWA_H):
            g = h // (SWA_H // SWA_KV)
            outs.append(_tn(vb[:, SWA_DH * g:SWA_DH * (g + 1)], es[h]) / dens[h])
        o_ref[...] = jnp.concatenate(outs, axis=0).T

    return pl.pallas_call(
        body, grid=(nb, nblk), name="swa_fwd",
        in_specs=_swa_specs(nblk),
        out_specs=[pl.BlockSpec((BLK, 512), lambda b, n: (b * nblk + n, 0)),
                   pl.BlockSpec((SWA_H, BLK), lambda b, n: (0, b * nblk + n))],
        out_shape=[SDS((T, 512), F32), SDS((SWA_H, T), F32)],
        compiler_params=_cp("parallel", "parallel"))(sink, z, z, z, pos_col, pos_col, pos_row)


def _fwd_merge(x, oa, ob, z, wa, wb, wo, tm, layer):
    T = x.shape[0]

    def body(x_ref, oa_ref, ob_ref, ag_ref, bg_ref, ma_ref, mb_ref, wa_ref, wb_ref, wo_ref, x1_ref):
        ag, bg = ag_ref[...], bg_ref[...]
        ua = _nn((oa_ref[...] * (ag * _sig(ag))).astype(BF16), wa_ref[...])
        ub = _nn((ob_ref[...] * (bg * _sig(bg))).astype(BF16), wb_ref[...])
        y = _sig(ma_ref[...]) * ua + _sig(mb_ref[...]) * ub
        x1_ref[...] = x_ref[...] + _nn(y.astype(BF16), wo_ref[...])

    return pl.pallas_call(
        body, grid=(T // tm,), name="fwd_merge",
        in_specs=[_row(tm, D), _row(tm, 512), _row(tm, 512), _row(tm, 512, Z_AG // 512), _row(tm, 512, Z_BG // 512),
                  _row(tm, D, Z_MA // D), _row(tm, D, Z_MB // D),
                  _res((512, D), 0), _res((512, D), 0), _res((D, D), 0)],
        out_specs=_row(tm, D),
        out_shape=SDS((T, D), F32),
        compiler_params=_cp("parallel"))(x, oa, ob, z, z, z, z, wa, wb, wo)


def _fwd_ple(x1, p, g, wpg, wpp, tm, layer):
    T = x1.shape[0]

    def body(x_ref, p_ref, g_ref, wpg_ref, wpp_ref, x2_ref, pg_ref, pp_ref):
        xf = x_ref[...]
        h1 = ((xf * _rstd(xf)) * g_ref[...]).astype(BF16)
        pg = _sig(_nn(h1, wpg_ref[...]))
        pp = _nn(p_ref[...].astype(BF16), wpp_ref[...])
        pg_ref[...] = pg
        pp_ref[...] = pp
        x2_ref[...] = xf + pg * pp

    return pl.pallas_call(
        body, grid=(T // tm,), name="fwd_ple",
        in_specs=[_row(tm, D), pl.BlockSpec((None, tm, PLE), lambda i: (layer, i, 0)),
                  _res((1, D), layer), _res((D, D), 0), _res((PLE, D), 0)],
        out_specs=[_row(tm, D)] * 3,
        out_shape=[SDS((T, D), F32)] * 3,
        compiler_params=_cp("parallel"))(x1, p, g, wpg, wpp)


def _loss_head(x, g, tgt, tm):
    T = x.shape[0]

    def body(x_ref, g_ref, t_ref, dx_ref, dg_ref, loss_ref):
        @pl.when(pl.program_id(0) == 0)
        def _():
            dg_ref[...] = jnp.zeros(dg_ref.shape, F32)
            loss_ref[...] = jnp.zeros(loss_ref.shape, F32)

        xf, gf = x_ref[...], g_ref[...]
        r = _rstd(xf)
        n = xf * r
        err = n * gf - t_ref[...]
        loss_ref[...] += 0.5 * jnp.sum(jnp.mean(err * err, axis=-1, keepdims=True), axis=0, keepdims=True)
        dx, dgr = _norm_bwd(err * (1.0 / D), n, r, gf)
        dx_ref[...] = dx
        dg_ref[...] += jnp.sum(dgr, axis=0, keepdims=True)

    return pl.pallas_call(
        body, grid=(T // tm,), name="loss_head",
        in_specs=[_row(tm, D), _res((1, D)), _row(tm, D)],
        out_specs=[_row(tm, D), _acc((1, D)), _acc((1, LANES))],
        out_shape=[SDS((T, D), F32), SDS((1, D), F32), SDS((1, LANES), F32)],
        compiler_params=_cp("arbitrary"))(x, g, tgt)


def _bwd_ple(dx2, x1, pg, pp, p, g, wpg, tm, layer, ride=None):
    T = x1.shape[0]
    grid = (T // tm,)

    def body(d_ref, x_ref, pg_ref, pp_ref, p_ref, g_ref, w_ref, dx_ref, dwg_ref, dwp_ref, dg_ref):
        @pl.when(pl.program_id(0) == 0)
        def _():
            dwg_ref[...] = jnp.zeros(dwg_ref.shape, F32)
            dwp_ref[...] = jnp.zeros(dwp_ref.shape, F32)
            dg_ref[...] = jnp.zeros(dg_ref.shape, F32)

        d, xf, pg, gf = d_ref[...], x_ref[...], pg_ref[...], g_ref[...]
        r = _rstd(xf)
        n = xf * r
        dpgl = (d * pp_ref[...] * pg * (1.0 - pg)).astype(BF16)
        dwg_ref[...] += _tn((n * gf).astype(BF16), dpgl)
        dwp_ref[...] += _tn(p_ref[...].astype(BF16), (d * pg).astype(BF16))
        dxn, dgr = _norm_bwd(_nt(dpgl, w_ref[...]), n, r, gf)
        dx_ref[...] = d + dxn
        dg_ref[...] += jnp.sum(dgr, axis=0, keepdims=True)

    r_in, r_out, r_shape, r_scratch, r_args = _ride_args(ride)
    return pl.pallas_call(
        _riding(ride, body, 7, 4, grid), grid=grid, name="bwd_ple_ride" if ride else "bwd_ple",
        in_specs=[_row(tm, D)] * 4 + [pl.BlockSpec((None, tm, PLE), lambda i: (layer, i, 0)),
                                      _res((1, D), layer), _res((D, D), 0)] + r_in,
        out_specs=[_row(tm, D), _acc((D, D)), _acc((PLE, D)), _acc((1, D))] + r_out,
        out_shape=[SDS((T, D), F32), SDS((D, D), F32), SDS((PLE, D), F32), SDS((1, D), F32)] + r_shape,
        scratch_shapes=r_scratch,
        compiler_params=_cp("arbitrary"))(dx2, x1, pg, pp, p, g, wpg, *r_args)


def _bwd_merge(dx1, oa, ob, z, wa, wb, wo, tm, layer):
    T = dx1.shape[0]

    def body(d_ref, oa_ref, ob_ref, ag_ref, bg_ref, ma_ref, mb_ref, wa_ref, wb_ref, wo_ref,
             doa_ref, dob_ref, dag_ref, dbg_ref, dma_ref, dmb_ref, dsa_ref, dsb_ref, dwa_ref, dwb_ref, dwo_ref):
        @pl.when(pl.program_id(0) == 0)
        def _():
            dwa_ref[...] = jnp.zeros(dwa_ref.shape, F32)
            dwb_ref[...] = jnp.zeros(dwb_ref.shape, F32)
            dwo_ref[...] = jnp.zeros(dwo_ref.shape, F32)

        db = d_ref[...].astype(BF16)
        gated = []
        for o_ref, gate_ref, w_ref in ((oa_ref, ag_ref, wa_ref), (ob_ref, bg_ref, wb_ref)):
            raw, gate = o_ref[...], gate_ref[...]
            sg = _sig(gate)
            silu = gate * sg
            ob16 = (raw * silu).astype(BF16)
            gated.append((raw, gate, sg, silu, ob16, _nn(ob16, w_ref[...])))
        ua, ub = gated[0][5], gated[1][5]
        sa, sb = _sig(ma_ref[...]), _sig(mb_ref[...])
        dwo_ref[...] += _tn((sa * ua + sb * ub).astype(BF16), db)
        dy = _nt(db, wo_ref[...])
        dma_ref[...] = (dy * ua * sa * (1.0 - sa)).astype(BF16)
        dmb_ref[...] = (dy * ub * sb * (1.0 - sb)).astype(BF16)
        for (s, w_ref, do_ref, dgate_ref, dw_ref, ds_ref), (raw, gate, sg, silu, ob16, _) in zip((
                (sa, wa_ref, doa_ref, dag_ref, dwa_ref, dsa_ref),
                (sb, wb_ref, dob_ref, dbg_ref, dwb_ref, dsb_ref)), gated):
            du = (dy * s).astype(BF16)
            dw_ref[...] += _tn(ob16, du)
            do = _nt(du, w_ref[...])
            draw = do * silu
            do_ref[...] = draw.astype(BF16)
            dgate_ref[...] = (do * raw * (sg * (1.0 + gate * (1.0 - sg)))).astype(BF16)
            ds_ref[...] = jnp.sum((draw * raw).T.reshape(MLA_H, MLA_V, tm), axis=1)

    return pl.pallas_call(
        body, grid=(T // tm,), name="bwd_merge",
        in_specs=[_row(tm, D), _row(tm, 512), _row(tm, 512), _row(tm, 512, Z_AG // 512), _row(tm, 512, Z_BG // 512),
                  _row(tm, D, Z_MA // D), _row(tm, D, Z_MB // D),
                  _res((512, D), 0), _res((512, D), 0), _res((D, D), 0)],
        out_specs=[_row(tm, 512)] * 4 + [_row(tm, D)] * 2 + [pl.BlockSpec((MLA_H, tm), lambda i: (0, i))] * 2
        + [_acc((512, D)), _acc((512, D)), _acc((D, D))],
        out_shape=[SDS((T, 512), BF16), SDS((T, 512), BF16), SDS((T, 512), BF16), SDS((T, 512), BF16),
                   SDS((T, D), BF16), SDS((T, D), BF16), SDS((MLA_H, T), F32), SDS((MLA_H, T), F32),
                   SDS((512, D), F32), SDS((512, D), F32), SDS((D, D), F32)],
        compiler_params=_cp("arbitrary"))(dx1, oa, ob, z, z, z, z, wa, wb, wo)


def _mla_bwd(qf, kf, v, do, lse, dsum, nb, seq, tq, exchange=()):
    T = qf.shape[0]
    nq = seq // tq
    pw = 2 * LANES
    pairs = [(qi, ki) for ki in range(nq) for qi in range(ki, nq)]
    qi_tab = jnp.array([qk[0] for qk in pairs], jnp.int32)
    ki_tab = jnp.array([qk[1] for qk in pairs], jnp.int32)
    grid = (nb, MLA_H // 2, len(pairs))
    n_x = len(exchange)

    def body(qi_ref, ki_ref, q_ref, k_ref, v_ref, do_ref, lse_ref, dsum_ref, *rest):
        p_refs, (dq_ref, dk_ref, dv_ref), got_refs = rest[:n_x], rest[n_x:n_x + 3], rest[n_x + 3:2 * n_x + 3]
        (dk_s, dv_s, dqt_s), sems = rest[2 * n_x + 3:2 * n_x + 6], rest[2 * n_x + 6:]
        step_id = pl.program_id(2)
        qi, ki = qi_ref[step_id], ki_ref[step_id]
        if n_x:
            start, finish = _exchange_phases(p_refs, got_refs, *sems)
            at_first, _, at_last = _grid_ends(grid)
            pl.when(at_first)(start)

        @pl.when(step_id == 0)
        def _():
            dqt_s[...] = jnp.zeros(dqt_s.shape, F32)

        @pl.when(qi == ki)
        def _():
            dk_s[...] = jnp.zeros(dk_s.shape, F32)
            dv_s[...] = jnp.zeros(dv_s.shape, F32)

        def step(masked):
            if masked:
                keys = lax.broadcasted_iota(jnp.int32, (tq, tq), 0)
                queries = lax.broadcasted_iota(jnp.int32, (tq, tq), 1)
                mask = keys <= queries
            for j in range(2):
                wide = slice(LANES * j, LANES * (j + 1))
                sl = slice(MLA_V * j, MLA_V * (j + 1))
                q, k = q_ref[:, wide], k_ref[:, wide]
                dob = do_ref[:, sl].astype(BF16)
                s = _nt(k, q) * (MLA_SCALE * LOG2E)
                if masked:
                    s = jnp.where(mask, s, NEG)
                p = jnp.exp2(s - lse_ref[j:j + 1, :])
                dv_s[:, sl] += _nn(p.astype(BF16), dob)
                ds = (p * (_nt(v_ref[:, sl], dob) - dsum_ref[j:j + 1, :]) * MLA_SCALE).astype(BF16)
                dk_s[:, wide] += _nn(ds, q)
                dqt_s[qi, wide, :] += _tn(k, ds)

        @pl.when(qi > ki)
        def _():
            step(False)

        @pl.when(qi == ki)
        def _():
            step(True)

        @pl.when(qi == nq - 1)
        def _():
            dk_ref[...] = dk_s[...]
            dv_ref[...] = dv_s[...]

        @pl.when(step_id == len(pairs) - 1)
        def _():
            for n in range(nq):
                dq_ref[tq * n:tq * (n + 1), :] = dqt_s[n].T

        if n_x:
            pl.when(at_last)(finish)

    qmap = lambda b, hp, s, qi_ref, ki_ref: (b * nq + qi_ref[s], hp)
    kmap = lambda b, hp, s, qi_ref, ki_ref: (b * nq + ki_ref[s], hp)
    stat = pl.BlockSpec((None, 2, tq), lambda b, hp, s, qi_ref, ki_ref: (hp, 0, b * nq + qi_ref[s]))
    return pl.pallas_call(
        body, name="mla_bwd_exchange" if n_x else "mla_bwd",
        grid_spec=pltpu.PrefetchScalarGridSpec(
            num_scalar_prefetch=2, grid=grid,
            in_specs=[pl.BlockSpec((tq, pw), qmap), pl.BlockSpec((tq, pw), kmap), pl.BlockSpec((tq, LANES), kmap),
                      pl.BlockSpec((tq, LANES), qmap), stat, stat] + [ANY] * n_x,
            out_specs=[pl.BlockSpec((seq, pw), lambda b, hp, s, qi_ref, ki_ref: (b, hp)),
                       pl.BlockSpec((tq, pw), kmap), pl.BlockSpec((tq, LANES), kmap)] + [ANY] * n_x,
            scratch_shapes=[pltpu.VMEM((tq, pw), F32), pltpu.VMEM((tq, LANES), F32), pltpu.VMEM((nq, pw, tq), F32)]
            + (_exchange_sems(n_x) if n_x else [])),
        out_shape=[SDS((T, QFW), F32), SDS((T, QFW), F32), SDS((T, MLA_H * MLA_V), F32)]
        + [SDS(a.shape, a.dtype) for a in exchange],
        compiler_params=_cp("arbitrary", "arbitrary", "arbitrary"))(qi_tab, ki_tab, qf, kf, v, do, lse, dsum, *exchange)


def _swa_bwd(sink, z, pos_col, pos_row, do, lse, dsum, nb, seq, layer, ride=None):
    T = z.shape[0]
    nblk = seq // BLK

    def body(sink_ref, q_ref, kvc_ref, kvp_ref, pcc_ref, pcp_ref, pr_ref, do_ref, lse_ref, dsum_ref,
             dq_ref, dkv_ref, dsink_ref):
        b, n = pl.program_id(0), pl.program_id(1)

        @pl.when((b == 0) & (n == 0))
        def _():
            dsink_ref[...] = jnp.zeros(dsink_ref.shape, F32)

        @pl.when(n == 0)
        def _():
            dkv_ref[...] = jnp.zeros(dkv_ref.shape, F32)

        kb, vb, scores = _swa_scores(n, q_ref, kvc_ref, kvp_ref, pcc_ref, pcp_ref, pr_ref)
        lane = lax.broadcasted_iota(jnp.int32, (1, LANES), 1)
        dsink = jnp.zeros((1, LANES), F32)
        dkv = [[None, None], [None, None]]
        dqs = []
        gsl = lambda h: slice(SWA_DH * (h // (SWA_H // SWA_KV)), SWA_DH * (h // (SWA_H // SWA_KV) + 1))
        qs, ss, dobs, dps = [], [], [], []
        for h in range(SWA_H):
            qh, s = scores(h)
            dob = do_ref[:, SWA_DH * h:SWA_DH * (h + 1)].astype(BF16)
            qs.append(qh)
            ss.append(s)
            dobs.append(dob)
            dps.append(_nt(vb[:, gsl(h)], dob))
        pbs, dss = [], []
        for h in range(SWA_H):
            lse, dsum = lse_ref[h:h + 1, :], dsum_ref[h:h + 1, :]
            p = jnp.exp2(ss[h] - lse)
            pbs.append(p.astype(BF16))
            dss.append((p * (dps[h] - dsum) * SWA_SCALE).astype(BF16))
            dsk = jnp.sum(-jnp.exp2(sink_ref[layer, h] * LOG2E - lse) * dsum, axis=1, keepdims=True)
            dsink = dsink + jnp.where(lane == h, dsk, 0.0)
        for h in range(SWA_H):
            g = h // (SWA_H // SWA_KV)
            dqs.append(_tn(kb[:, gsl(h)], dss[h]))
            dk, dv = _nn(dss[h], qs[h]), _nn(pbs[h], dobs[h])
            dkv[g][0] = dk if dkv[g][0] is None else dkv[g][0] + dk
            dkv[g][1] = dv if dkv[g][1] is None else dkv[g][1] + dv
        dq_ref[...] = jnp.concatenate(dqs, axis=0).T.astype(BF16)
        dsink_ref[...] += dsink
        upd = jnp.concatenate([dkv[0][0], dkv[1][0], dkv[0][1], dkv[1][1]], axis=1)
        dkv_ref[pl.ds(pl.multiple_of(n * BLK, BLK), BLK), :] += upd[BLK:]

        @pl.when(n > 0)
        def _():
            dkv_ref[pl.ds(pl.multiple_of((n - 1) * BLK, BLK), BLK), :] += upd[:BLK]

    r_in, r_out, r_shape, r_scratch, r_args = _ride_args(ride)
    return pl.pallas_call(
        _riding(ride, body, 10, 3, (nb, nblk)), grid=(nb, nblk), name="swa_bwd_ride" if ride else "swa_bwd",
        in_specs=_swa_specs(nblk) + [pl.BlockSpec((BLK, 512), lambda b, n: (b * nblk + n, 0))]
        + [pl.BlockSpec((SWA_H, BLK), lambda b, n: (0, b * nblk + n))] * 2 + r_in,
        out_specs=[pl.BlockSpec((BLK, 512), lambda b, n: (b * nblk + n, 0)),
                   pl.BlockSpec((seq, 2 * BLK), lambda b, n: (b, 0)),
                   pl.BlockSpec((1, LANES), lambda b, n: (0, 0))] + r_out,
        out_shape=[SDS((T, 512), BF16), SDS((T, 2 * BLK), F32), SDS((1, LANES), F32)] + r_shape,
        scratch_shapes=r_scratch,
        compiler_params=_cp("arbitrary", "arbitrary"))(sink, z, z, z, pos_col, pos_col, pos_row, do, lse, dsum,
                                                       *r_args)


def _bwd_prep(dq, dk, dv, z, gq, gkv, wq, wkv, tc, ts1, ts2, tm, layer):
    T = z.shape[0]

    def body(dq_ref, dk_ref, dv_ref, qd_ref, kvd_ref, gq_ref, gkv_ref, wq_ref, wkv_ref, c_ref, s1_ref, s2_ref,
             dqd_ref, dkvd_ref, dkr_ref, dwq_ref, dwkv_ref, dgq_ref, dgkv_ref, dqb_s, dkvb_s):
        @pl.when(pl.program_id(0) == 0)
        def _():
            for ref in (dwq_ref, dwkv_ref, dgq_ref, dgkv_ref):
                ref[...] = jnp.zeros(ref.shape, F32)

        c, s1, s2 = c_ref[...], s1_ref[...], s2_ref[...]
        lane = lax.broadcasted_iota(jnp.int32, (1, LANES), 1)
        rope_lanes = (lane >= MLA_NOPE) & (lane < MLA_QK)
        dkb = jnp.zeros((tm, LANES), F32)
        for h in range(MLA_H):
            sl = slice(LANES * h, LANES * (h + 1))
            dqb_s[:, sl] = _rope_t(dq_ref[:, sl], c, s1, s2).astype(BF16)
            dkh = dk_ref[:, sl]
            dkb = dkb + dkh
            dkvb_s[:, sl] = dkh.astype(BF16)
        dkvb_s[:, QFW:] = dv_ref[...].astype(BF16)
        dkr_ref[...] = _rope_t(jnp.where(rope_lanes, dkb, 0.0), c, s1, s2).astype(BF16)

        for (x_ref, g_ref, w_ref, d_s, dx_ref, dw_ref, dg_ref) in (
                (qd_ref, gq_ref, wq_ref, dqb_s, dqd_ref, dwq_ref, dgq_ref),
                (kvd_ref, gkv_ref, wkv_ref, dkvb_s, dkvd_ref, dwkv_ref, dgkv_ref)):
            xf, gf, db = x_ref[...], g_ref[...], d_s[...]
            r = _rstd(xf)
            n = xf * r
            dw_ref[...] += _tn((n * gf).astype(BF16), db)
            dx, dgr = _norm_bwd(_nt(db, w_ref[...]), n, r, gf)
            dx_ref[...] = dx.astype(BF16)
            dg_ref[...] += jnp.sum(dgr, axis=0, keepdims=True)

    return pl.pallas_call(
        body, grid=(T // tm,), name="bwd_prep",
        in_specs=[_row(tm, QFW), _row(tm, QFW), _row(tm, MLA_H * MLA_V),
                  _row(tm, QL, Z_QD // QL), _row(tm, KVL, Z_KVD // KVL),
                  _res((1, QL), layer), _res((1, KVL), layer), _res((QL, QFW), 0), _res((KVL, KVW), 0),
                  _row(tm, LANES), _row(tm, LANES), _row(tm, LANES)],
        out_specs=[_row(tm, QL), _row(tm, KVL), _row(tm, LANES),
                   _acc((QL, QFW)), _acc((KVL, KVW)), _acc((1, QL)), _acc((1, KVL))],
        out_shape=[SDS((T, QL), BF16), SDS((T, KVL), BF16), SDS((T, LANES), BF16),
                   SDS((QL, QFW), F32), SDS((KVL, KVW), F32), SDS((1, QL), F32), SDS((1, KVL), F32)],
        scratch_shapes=[pltpu.VMEM((tm, QFW), BF16), pltpu.VMEM((tm, KVW), BF16)],
        compiler_params=_cp("arbitrary"))(dq, dk, dv, z, z, gq, gkv, wq, wkv, tc, ts1, ts2)


def _bwd_in(pieces, x, g, dres, w, tm, layer):
    T = x.shape[0]
    grid = (T // tm,)
    widths = [pc.shape[1] for pc in pieces]
    assert sum(widths) == ZW
    n_p = len(pieces)

    def body(*refs):
        p_refs, (x_ref, g_ref, r_ref, w_ref, dx_ref, dz_ref, dg_ref) = refs[:n_p], refs[n_p:]

        @pl.when(pl.program_id(0) == 0)
        def _():
            dg_ref[...] = jnp.zeros(dg_ref.shape, F32)

        off = 0
        for ref, wd in zip(p_refs, widths):
            dz_ref[:, off:off + wd] = ref[...].astype(BF16)
            off += wd
        xf, gf = x_ref[...], g_ref[...]
        r = _rstd(xf)
        n = xf * r
        dx, dgr = _norm_bwd(_nt(dz_ref[...], w_ref[...]), n, r, gf)
        dx_ref[...] = r_ref[...] + dx
        dg_ref[...] += jnp.sum(dgr, axis=0, keepdims=True)

    return pl.pallas_call(
        body, grid=grid, name="bwd_in",
        in_specs=[_row(tm, wd) for wd in widths] + [_row(tm, D), _res((1, D), layer), _row(tm, D),
                                                    _res((D, ZW), 0)],
        out_specs=[_row(tm, D), _row(tm, ZW), _acc((1, D))],
        out_shape=[SDS((T, D), F32), SDS((T, ZW), BF16), SDS((1, D), F32)],
        compiler_params=_cp("arbitrary"))(*pieces, x, g, dres, w)


def _wgrad_in(hb, dzb, tm, ride=None):
    T = hb.shape[0]
    half = ZW // 2
    grid = (2, T // tm)

    def body(h_ref, dz_ref, dw_ref):
        @pl.when(pl.program_id(1) == 0)
        def _():
            dw_ref[...] = jnp.zeros(dw_ref.shape, F32)

        dw_ref[...] += _tn(h_ref[...], dz_ref[...])

    r_in, r_out, r_shape, r_scratch, r_args = _ride_args(ride)
    out = pl.pallas_call(
        _riding(ride, body, 2, 1, grid), grid=grid, name="wgrad_in_ride" if ride else "wgrad_in",
        in_specs=[pl.BlockSpec((tm, D), lambda j, t: (t, 0)), pl.BlockSpec((tm, half), lambda j, t: (t, j))] + r_in,
        out_specs=[pl.BlockSpec((D, half), lambda j, t: (0, j))] + r_out,
        out_shape=[SDS((D, ZW), F32)] + r_shape, scratch_shapes=r_scratch,
        compiler_params=_cp("arbitrary", "arbitrary"))(hb, dzb, *r_args)
    return out


IN_PIECES = ((0, 512, Z_AQ), (512, 128, Z_AK), (640, 128, Z_AV), (768, 512, Z_AG), (1280, 256, Z_QD),
             (1536, 128, Z_KVD), (1664, MLA_ROPE, Z_KR + MLA_NOPE), (1696, 512, Z_BG), (2208, 1024, Z_MA),
             (3232, 1024, Z_MB))
WIDE_W = IN_W // N_DEV


def _column_runs():
    runs = []
    for start, width, kstart in IN_PIECES:
        col = start
        while col < start + width:
            dev = col // WIDE_W
            stop = min(start + width, (dev + 1) * WIDE_W)
            runs.append((dev, col - dev * WIDE_W, stop - col, kstart + col - start))
            col = stop
    return runs


def _win_layout(blocks, tm):
    runs = _column_runs()

    def body(g_ref, o_ref):
        o_ref[:, Z_KR:Z_KR + LANES] = jnp.zeros((tm, LANES), o_ref.dtype)
        for dev, lo, n, k in runs:
            o_ref[:, k:k + n] = g_ref[dev, :, lo:lo + n]

    return pl.pallas_call(
        body, grid=(D // tm,), name="win_layout",
        in_specs=[pl.BlockSpec((N_DEV, None, tm, WIDE_W), lambda i: (0, 0, i, 0))],
        out_specs=pl.BlockSpec((None, tm, ZW), lambda i: (0, i, 0)),
        out_shape=SDS((1, D, ZW), blocks.dtype),
        compiler_params=_cp("parallel"))(blocks)


def _win_grad_layout(dw, tm):
    runs = _column_runs()

    def body(g_ref, o_ref):
        for dev, lo, n, k in runs:
            o_ref[dev, :, lo:lo + n] = g_ref[:, k:k + n]

    return pl.pallas_call(
        body, grid=(D // tm,), name="win_grad_layout",
        in_specs=[_row(tm, ZW)],
        out_specs=pl.BlockSpec((N_DEV, None, tm, WIDE_W), lambda i: (0, 0, i, 0)),
        out_shape=SDS((N_DEV, 1, D, WIDE_W), F32),
        compiler_params=_cp("parallel"))(dw)


def _wuq_to_kernel(w):
    w = w.reshape(w.shape[:-1] + (MLA_H, MLA_QK))
    w = jnp.pad(w, [(0, 0)] * (w.ndim - 1) + [(0, LANES - MLA_QK)])
    return w.reshape(w.shape[:-2] + (QFW,))


def _wuq_from_kernel(g):
    g = g.reshape(g.shape[:-1] + (MLA_H, LANES))[..., :MLA_QK]
    return g.reshape(g.shape[:-2] + (MLA_H * MLA_QK,))


def _wukv_to_kernel(w):
    w = w.reshape(w.shape[:-1] + (MLA_H, MLA_NOPE + MLA_V))
    k = jnp.pad(w[..., :MLA_NOPE], [(0, 0)] * (w.ndim - 1) + [(0, LANES - MLA_NOPE)])
    v = w[..., MLA_NOPE:]
    return jnp.concatenate([k.reshape(k.shape[:-2] + (QFW,)), v.reshape(v.shape[:-2] + (MLA_H * MLA_V,))], axis=-1)


def _wukv_from_kernel(g):
    k = g[..., :QFW].reshape(g.shape[:-1] + (MLA_H, LANES))[..., :MLA_NOPE]
    v = g[..., QFW:].reshape(g.shape[:-1] + (MLA_H, MLA_V))
    kv = jnp.concatenate([k, v], axis=-1)
    return kv.reshape(kv.shape[:-2] + (MLA_H * (MLA_NOPE + MLA_V),))


def _rope_tables(pos):
    half = MLA_ROPE // 2
    inv = 10000.0 ** (-jnp.arange(0, MLA_ROPE, 2, dtype=F32) / MLA_ROPE)
    ang = pos.astype(F32)[:, None] * inv
    cos, sin = jnp.cos(ang), jnp.sin(ang)
    one = jnp.ones((pos.shape[0], MLA_NOPE), F32)
    zero = lambda n: jnp.zeros((pos.shape[0], n), F32)
    tc = jnp.concatenate([one, cos, cos, one[:, :LANES - MLA_QK]], axis=1)
    ts1 = jnp.concatenate([zero(MLA_NOPE + half), sin, zero(LANES - MLA_QK)], axis=1)
    ts2 = jnp.concatenate([zero(MLA_NOPE), -sin, zero(LANES - MLA_NOPE - half)], axis=1)
    return tc, ts1, ts2


def _local_step(x, p, positions, loss_target, small, wts, plan=None):
    nb, seq, _ = x.shape
    T = nb * seq
    tm = min(512, T)
    tl = min(1024, T)
    tq = min(512, seq)
    xf = x.reshape(T, D)
    pos = positions.reshape(T)
    posf = pos.astype(F32)
    pos_col, pos_row = posf.reshape(T, 1), posf.reshape(T // BLK, 1, BLK)
    tc, ts1, ts2 = _rope_tables(pos)

    wts, sm = list(wts), small
    pl_in = p.reshape(DEPTH, T, PLE)
    saved = []
    for i in range(DEPTH):
        riding = plan is not None and i == 0
        w = wts[i]
        z, hb, *got = _fwd_in(xf, sm["g_mix"], w["w_in"], tm, i,
                              ride=_Ride("gather", plan["rows0"]) if riding else None)
        if riding:
            w = wts[0] = dict(w, **plan["row_weights"](got[0]))
        oa, lse_a = _swa_fwd(sm["sink"], z, pos_col, pos_row, nb, seq, i)
        qf, kf, v = _fwd_prep(z, sm["g_q"], sm["g_kv"], w["w_uq"], w["w_ukv"], tc, ts1, ts2, tl, i)
        ob, lse_b, *got = _mla_fwd(qf, kf, v, nb, seq, tq, gather=plan["blocks1"] if riding else ())
        if riding:
            wts.append(dict(w_in=plan["w_in"](got[0]), **plan["row_weights"](got[1])))
        x1 = _fwd_merge(xf, oa, ob, z, w["w_br_a"], w["w_br_b"], w["w_out"], tm, i)
        x2, pg, pp = _fwd_ple(x1, pl_in, sm["g_ple"], w["w_ple_gate"], w["w_ple_proj"], tl, i)
        saved.append(dict(x=xf, z=z, hb=hb, oa=oa, lse_a=lse_a, qf=qf, kf=kf, v=v, ob=ob, lse_b=lse_b,
                          x1=x1, pg=pg, pp=pp))
        xf = x2

    dx, dg_final, loss = _loss_head(xf, small["g_final"], loss_target.reshape(T, D), tl)

    grads = [None] * DEPTH
    exchanged = {}
    for i in reversed(range(DEPTH)):
        riding = plan is not None and i == 0
        sv, w = saved[i], wts[i]
        pay = plan["payload"](grads[1]) if riding else []
        dx1, dwpg, dwpp, dg_ple, *got = _bwd_ple(dx, sv["x1"], sv["pg"], sv["pp"], pl_in, sm["g_ple"],
                                                 w["w_ple_gate"], tm, i, ride=_Ride("swap", pay) if riding else None)
        doa, dob, dag, dbg, dma, dmb, dsum_a, dsum_b, dwa, dwb, dwo = _bwd_merge(
            dx1, sv["oa"], sv["ob"], sv["z"], w["w_br_a"], w["w_br_b"], w["w_out"], tm, i)
        dq_b, dk_b, dv_b, *exchanged["layer1"] = _mla_bwd(
            sv["qf"], sv["kf"], sv["v"], dob, sv["lse_b"], dsum_b.reshape(MLA_H // 2, 2, T), nb, seq, tq,
            exchange=plan["add"](pay, got) if riding else ())
        dqd, dkvd, dkr, dwq, dwkv, dgq, dgkv = _bwd_prep(dq_b, dk_b, dv_b, sv["z"], sm["g_q"], sm["g_kv"],
                                                         w["w_uq"], w["w_ukv"], tc, ts1, ts2, tl, i)
        g = dict(w_uq=_wuq_from_kernel(dwq), w_ukv=_wukv_from_kernel(dwkv), w_br_a=dwa, w_br_b=dwb, w_out=dwo,
                 w_ple_gate=dwpg, w_ple_proj=dwpp)
        pay = [plan["rows_payload"](g)] if riding else []
        dq_a, dkv_a, dsink, *got = _swa_bwd(sm["sink"], sv["z"], pos_col, pos_row, doa, sv["lse_a"], dsum_a, nb, seq,
                                            i, ride=_Ride("swap", pay) if riding else None)
        dx, dzb, dg_mix = _bwd_in([dma, dmb, dq_a, dag, dbg, dqd, dkv_a, dkvd, dkr], sv["x"], sm["g_mix"], dx1,
                                  w["w_in"], tm, i)
        dwin, *exchanged["rows0"] = _wgrad_in(sv["hb"], dzb, tm,
                                              ride=_Ride("exchange", plan["add"](pay, got)) if riding else None)
        g.update(g_mix=dg_mix[0], w_in=dwin, sink=dsink[0, :SWA_H], g_q=dgq[0], g_kv=dgkv[0], g_ple=dg_ple[0])
        grads[i] = g
    return loss, dx.reshape(nb, seq, D), grads, dg_final[0], exchanged


def _row_weights(rows):
    blocks = _unpack_rows(rows)
    out = {n: _join(n, blocks[n]) for n, _ in ROWS_PIECES}
    out.update(w_uq=_wuq_to_kernel(out["w_uq"]), w_ukv=_wukv_to_kernel(out["w_ukv"]))
    return out


def _small_params(g_mix, sink, g_q, g_kv, g_ple, g_final):
    return dict(g_mix=g_mix[:, None], sink=sink, g_q=g_q[:, None], g_kv=g_kv[:, None], g_ple=g_ple[:, None],
                g_final=g_final[None])


UQ_W = MLA_H * MLA_QK // N_DEV
ROWS_PIECES = (("w_uq", QL), ("w_ukv", KVL), ("w_br_a", 512), ("w_br_b", 512), ("w_out", D), ("w_ple_gate", D),
               ("w_ple_proj", PLE))
SMALL = (("g_mix", (DEPTH, D)), ("sink", (DEPTH, SWA_H)), ("g_q", (DEPTH, QL)), ("g_kv", (DEPTH, KVL)),
         ("g_ple", (DEPTH, D)), ("g_final", (D,)))
VEC_ROWS = 48
ROWS_N = sum(r for _, r in ROWS_PIECES)
WIDE_TILE, ROWS_TILE = 256, ROWS_N // 2


def _to_rows(name, a):
    if name == "w_uq":
        a = jnp.pad(a, [(0, 0)] * (a.ndim - 1) + [(0, LANES - UQ_W)])
    return a.reshape(a.shape[:-2] + (-1, LANES))


def _from_rows(name, r):
    if name in ("w_out", "w_ple_gate"):
        return r.reshape(r.shape[:-2] + (D // N_DEV, D))
    return r[..., :UQ_W] if name == "w_uq" else r


def _pack_rows(blocks):
    return jnp.concatenate([_to_rows(n, blocks[n]) for n, _ in ROWS_PIECES], axis=-2)


def _unpack_rows(rows):
    blocks, off = {}, 0
    for n, r in ROWS_PIECES:
        blocks[n] = _from_rows(n, rows[..., off:off + r, :])
        off += r
    return blocks


def _pack_vec(vectors, loss=None):
    parts = [vectors[n].reshape(-1) for n, _ in SMALL] + ([] if loss is None else [loss.reshape(1)])
    vec = jnp.concatenate(parts)
    return jnp.pad(vec, (0, VEC_ROWS * LANES - vec.shape[0])).reshape(1, VEC_ROWS, LANES)


def _unpack_vec(vec):
    vec = vec.reshape(-1)
    vectors, off = {}, 0
    for n, shp in SMALL:
        size = 1
        for s in shp:
            size *= s
        vectors[n] = vec[off:off + size].reshape(shp)
        off += size
    return vectors, vec[off]


def _join(name, blocks):
    if name in ("w_out", "w_ple_gate"):
        return jnp.moveaxis(blocks, 0, 1).reshape(blocks.shape[1], -1, blocks.shape[-1])
    return jnp.moveaxis(blocks, 0, 2).reshape(blocks.shape[1], blocks.shape[2], -1)


def _split(name, full):
    if name in ("w_out", "w_ple_gate"):
        return jnp.moveaxis(full.reshape(full.shape[0], N_DEV, -1, full.shape[-1]), 1, 0)
    return jnp.moveaxis(full.reshape(full.shape[0], full.shape[1], N_DEV, -1), 2, 0)


MESH_ID = pl.DeviceIdType.MESH
ANY = pl.BlockSpec(memory_space=pl.ANY)


def _place():
    return lax.axis_index("x"), lax.axis_index("y"), lax.axis_index("c")


def _all_gather(blocks):
    n = len(blocks)

    def body(*refs):
        start, forward, finish = _gather_phases(refs[:n], refs[n:2 * n], *refs[2 * n:])
        start()
        forward()
        finish()

    return pl.pallas_call(
        body, name="all_gather_weights", out_shape=_gather_out(blocks),
        in_specs=[ANY] * n, out_specs=[ANY] * n, scratch_shapes=_gather_sems(n))(*blocks)


def _gather_out(blocks):
    return [SDS((N_DEV,) + b.shape, b.dtype) for b in blocks]


def _gather_sems(n):
    return [pltpu.SemaphoreType.DMA((7 * n,)), pltpu.SemaphoreType.DMA((7 * n,)), pltpu.SemaphoreType.DMA((n,))]


def _gather_phases(x_refs, out_refs, send_sems, recv_sems, local_sems):
    n = len(x_refs)
    x, y, c = _place()
    me, sibling = (x, y, c), (x, y, 1 - c)
    chips = [(1 - x, y), (x, 1 - y), (1 - x, 1 - y)]

    def slot(a, px, py, pc):
        return out_refs[a].at[4 * px + 2 * py + pc]

    def copy(a, k, blk, to, src=None):
        return pltpu.make_async_remote_copy(
            src_ref=slot(a, *blk) if src is None else src, dst_ref=slot(a, *blk),
            send_sem=send_sems.at[7 * a + k], recv_sem=recv_sems.at[7 * a + k], device_id=to,
            device_id_type=MESH_ID)

    def mine():
        return [pltpu.make_async_copy(x_refs[a], slot(a, *me), local_sems.at[a]) for a in range(n)]

    def first():
        out = []
        for a in range(n):
            out += [copy(a, 0, me, sibling, src=x_refs[a])]
            out += [copy(a, 1 + j, me, (*chip, c), src=x_refs[a]) for j, chip in enumerate(chips)]
        return out

    def passed():
        return [copy(a, 4 + j, (*chip, c), sibling) for j, chip in enumerate(chips) for a in range(n)]

    def start():
        for cp in mine() + first():
            cp.start()

    def forward():
        for j, chip in enumerate(chips):
            for a in range(n):
                copy(a, 1 + j, (*chip, c), me).wait_recv()
                copy(a, 4 + j, (*chip, c), sibling).start()

    def finish():
        for a in range(n):
            copy(a, 0, sibling, me).wait_recv()
            for j, chip in enumerate(chips):
                copy(a, 4 + j, (*chip, 1 - c), me).wait_recv()
        for cp in first() + passed():
            cp.wait_send()
        for cp in mine():
            cp.wait()

    return start, forward, finish


def _swap_sibling(arrs):
    n = len(arrs)

    def body(*refs):
        start, finish = _swap_phases(refs[:n], refs[n:2 * n], *refs[2 * n:])
        start()
        finish()

    return pl.pallas_call(
        body, name="swap_sibling", out_shape=[SDS((a.shape[0],) + a.shape[2:], a.dtype) for a in arrs],
        in_specs=[ANY] * n, out_specs=[ANY] * n, scratch_shapes=_swap_sems(n))(*arrs)


def _swap_sems(n):
    return [pltpu.SemaphoreType.DMA((n,)), pltpu.SemaphoreType.DMA((n,))]


def _swap_phases(a_refs, out_refs, send_sems, recv_sems):
    x, y, c = _place()

    def copies():
        return [pltpu.make_async_remote_copy(
            src_ref=a_refs[a].at[:, 1 - c], dst_ref=out_refs[a], send_sem=send_sems.at[a], recv_sem=recv_sems.at[a],
            device_id=(x, y, 1 - c), device_id_type=MESH_ID) for a in range(len(a_refs))]

    def start():
        for cp in copies():
            cp.start()

    def finish():
        for cp in copies():
            cp.wait()

    return start, finish


def _exchange_chips(arrs):
    n = len(arrs)

    def body(*refs):
        start, finish = _exchange_phases(refs[:n], refs[n:2 * n], *refs[2 * n:])
        start()
        finish()

    return pl.pallas_call(
        body, name="exchange_chips", out_shape=[SDS(a.shape, a.dtype) for a in arrs],
        in_specs=[ANY] * n, out_specs=[ANY] * n, scratch_shapes=_exchange_sems(n))(*arrs)


def _exchange_sems(n):
    return [pltpu.SemaphoreType.DMA((3 * n,)), pltpu.SemaphoreType.DMA((3 * n,)), pltpu.SemaphoreType.DMA((n,))]


def _exchange_phases(p_refs, out_refs, send_sems, recv_sems, local_sems):
    n = len(p_refs)
    x, y, c = _place()
    mine = 2 * x + y
    peers = [(1 - x, y), (x, 1 - y), (1 - x, 1 - y)]

    def local():
        return [pltpu.make_async_copy(p_refs[a].at[mine], out_refs[a].at[mine], local_sems.at[a]) for a in range(n)]

    def copy(a, j, src_chip, dst_chip):
        px, py = peers[j]
        return pltpu.make_async_remote_copy(
            src_ref=p_refs[a].at[src_chip], dst_ref=out_refs[a].at[dst_chip], send_sem=send_sems.at[3 * a + j],
            recv_sem=recv_sems.at[3 * a + j], device_id=(px, py, c), device_id_type=MESH_ID)

    def sends():
        return [copy(a, j, 2 * px + py, mine) for a in range(n) for j, (px, py) in enumerate(peers)]

    def start():
        for cp in local() + sends():
            cp.start()

    def finish():
        for a in range(n):
            for j, (px, py) in enumerate(peers):
                copy(a, j, mine, 2 * px + py).wait_recv()
        for cp in sends():
            cp.wait_send()
        for cp in local():
            cp.wait()

    return start, finish


def _add_mine(g, recv, core, tile, dtype):
    _, _, lead, rows, width = g.shape

    def body(c_ref, g_ref, r_ref, o_ref):
        o_ref[...] = (g_ref[...] + r_ref[...]).astype(dtype)

    spec = pl.BlockSpec((None, None, tile, width), lambda k, l, i, c_ref: (k, l, i, 0))
    return pl.pallas_call(
        body, name="add_sibling", out_shape=SDS(recv.shape, dtype),
        grid_spec=pltpu.PrefetchScalarGridSpec(
            num_scalar_prefetch=1, grid=(g.shape[0], lead, rows // tile),
            in_specs=[pl.BlockSpec((None, None, None, tile, width), lambda k, l, i, c_ref: (k, c_ref[0], l, i, 0)),
                      spec],
            out_specs=spec),
        compiler_params=_cp("parallel", "parallel", "parallel"))(core, g, recv)


def _sum_adamw(parts, w, m, v, tile):
    lead, rows, width = w.shape
    last = rows // tile - 1

    def body(*refs):
        p_refs, (w_ref, m_ref, v_ref, g_ref, d_ref, nm_ref, nv_ref) = refs[:lead], refs[lead:]
        for layer in range(lead):
            @pl.when(pl.program_id(0) == layer)
            def _(p_ref=p_refs[layer]):
                g = ((p_ref[0].astype(F32) + p_ref[1].astype(F32)) + p_ref[2].astype(F32)) + p_ref[3].astype(F32)
                nm = ADAM_B1 * m_ref[...] + (1.0 - ADAM_B1) * g
                nv = ADAM_B2 * v_ref[...] + (1.0 - ADAM_B2) * jnp.square(g)
                m_hat = nm / (1.0 - ADAM_B1 ** ADAM_STEP)
                v_hat = nv / (1.0 - ADAM_B2 ** ADAM_STEP)
                g_ref[...] = g
                nm_ref[...] = nm
                nv_ref[...] = nv
                d_ref[...] = -ADAM_LR * (m_hat / (jnp.sqrt(v_hat) + ADAM_EPS) + ADAM_WD * w_ref[...])

    pspec = lambda layer: pl.BlockSpec(
        (4, None, tile, width),
        lambda l, i: (0, 0, jnp.where(l == layer, i, jnp.where(l > layer, last, 0)), 0))
    spec = pl.BlockSpec((None, tile, width), lambda l, i: (l, i, 0))
    return pl.pallas_call(
        body, grid=(lead, rows // tile), name="sum_adamw",
        in_specs=[pspec(layer) for layer in range(lead)] + [spec, spec, spec],
        out_specs=[spec] * 4, out_shape=[SDS((lead, rows, width), F32)] * 4,
        compiler_params=_cp("arbitrary", "arbitrary"))(*parts, w, m, v)


def kernel(x, p, positions, g_mix, w_in, sink, g_q, w_uq, g_kv, w_ukv, w_br_a, w_br_b, w_out, g_ple, w_ple_gate, w_ple_proj, g_final, loss_target, m_g_mix, m_w_in, m_sink, m_g_q, m_w_uq, m_g_kv, m_w_ukv, m_w_br_a, m_w_br_b, m_w_out, m_g_ple, m_w_ple_gate, m_w_ple_proj, m_g_final, v_g_mix, v_w_in, v_sink, v_g_q, v_w_uq, v_g_kv, v_w_ukv, v_w_br_a, v_w_br_b, v_w_out, v_g_ple, v_w_ple_gate, v_w_ple_proj, v_g_final):
    weights = dict(g_mix=g_mix, w_in=w_in, sink=sink, g_q=g_q, w_uq=w_uq, g_kv=g_kv, w_ukv=w_ukv, w_br_a=w_br_a,
                   w_br_b=w_br_b, w_out=w_out, g_ple=g_ple, w_ple_gate=w_ple_gate, w_ple_proj=w_ple_proj,
                   g_final=g_final)
    mom1 = dict(g_mix=m_g_mix, w_in=m_w_in, sink=m_sink, g_q=m_g_q, w_uq=m_w_uq, g_kv=m_g_kv, w_ukv=m_w_ukv,
                w_br_a=m_w_br_a, w_br_b=m_w_br_b, w_out=m_w_out, g_ple=m_g_ple, w_ple_gate=m_w_ple_gate,
                w_ple_proj=m_w_ple_proj, g_final=m_g_final)
    mom2 = dict(g_mix=v_g_mix, w_in=v_w_in, sink=v_sink, g_q=v_g_q, w_uq=v_w_uq, g_kv=v_g_kv, w_ukv=v_w_ukv,
                w_br_a=v_w_br_a, w_br_b=v_w_br_b, w_out=v_w_out, g_ple=v_g_ple, w_ple_gate=v_w_ple_gate,
                w_ple_proj=v_w_ple_proj, g_final=v_g_final)
    assert DEPTH == 2
    wide = lambda d: d["w_in"]
    rows = lambda d: _pack_rows(d)
    core = lax.axis_index("c").astype(jnp.int32).reshape(1)

    w16 = [wide(weights).astype(BF16), rows(weights).astype(BF16)]
    wts0 = dict(w_in=_win_layout(_all_gather([w16[0][:1]])[0], 256))
    small = _small_params(g_mix, sink, g_q, g_kv, g_ple, g_final)

    def wide_payload(g):
        return _win_grad_layout(g["w_in"], 256).reshape(N_DEV // 2, 2, 1, D, WIDE_W)

    def rows_payload(g):
        return _pack_rows({n: _split(n, g[n][None]) for n, _ in ROWS_PIECES}).reshape(N_DEV // 2, 2, 1, ROWS_N, LANES)

    def add(pay, got):
        tiles = {D: (WIDE_TILE, BF16), ROWS_N: (ROWS_TILE, BF16), VEC_ROWS: (VEC_ROWS, F32)}
        return [_add_mine(a, b, core, *tiles[a.shape[-2]]) for a, b in zip(pay, got)]

    plan = dict(rows0=[w16[1][:1]], blocks1=[a[1:] for a in w16], w_in=lambda blocks: _win_layout(blocks, 256),
                row_weights=_row_weights, payload=lambda g: [wide_payload(g), rows_payload(g)],
                rows_payload=rows_payload, add=add)
    loss, grad_x, grads, dg_final, rode = _local_step(x, p, positions, loss_target, small, [wts0], plan)

    vectors = {n: jnp.stack([grads[i][n] for i in range(DEPTH)]) for n, _ in SMALL[:-1]}
    vectors["g_final"] = dg_final
    pay = [wide_payload(grads[0]),
           jnp.broadcast_to(_pack_vec(vectors, loss[0, 0]), (N_DEV // 2, 2, 1, VEC_ROWS, LANES))]
    parts_wide0, parts_vec = _exchange_chips(add(pay, _swap_sibling(pay)))
    out_wide = _sum_adamw([parts_wide0, rode["layer1"][0]], wide(weights), wide(mom1), wide(mom2), WIDE_TILE)
    out_rows = _sum_adamw([rode["rows0"][0], rode["layer1"][1]], rows(weights), rows(mom1), rows(mom2), ROWS_TILE)
    out_vec = _sum_adamw([parts_vec], _pack_vec(weights), _pack_vec(mom1), _pack_vec(mom2), VEC_ROWS)

    outs = []
    for ow, orow, ovec in zip(out_wide, out_rows, out_vec):
        named = _unpack_rows(orow)
        named.update(_unpack_vec(ovec)[0])
        named["w_in"] = ow
        outs += [named[n] for n in weights]
    loss = _unpack_vec(out_vec[0])[1]
    return (loss, grad_x, *outs)
```

```python
import functools

import jax
import jax.numpy as jnp
from jax import lax
from jax.experimental import pallas as pl
from jax.experimental.pallas import tpu as pltpu

F32, BF16 = jnp.float32, jnp.bfloat16
SDS = jax.ShapeDtypeStruct

D = 1024
DEPTH = 2
PLE = 256
BLK = 128
EPS = 1e-6
NEG = -1e30
SWA_H, SWA_KV, SWA_DH = 8, 2, 64
MLA_H, MLA_NOPE, MLA_ROPE, MLA_V = 8, 64, 32, 64
MLA_QK = MLA_NOPE + MLA_ROPE
QL, KVL = 256, 128
IN_W = 4256
N_DEV = 8

V7X_VMEM_BYTES = 64 * 1024 * 1024
LANES = 128
VMEM_LIMIT = V7X_VMEM_BYTES * 7 // 8

ZW = 4352
Z_MA, Z_MB, Z_AQ, Z_AG, Z_BG, Z_QD, Z_AK, Z_AV, Z_KVD, Z_KR = 0, 1024, 2048, 2560, 3072, 3584, 3840, 3968, 4096, 4224
QFW = MLA_H * LANES
KVW = QFW + MLA_H * MLA_V
MLA_SCALE = MLA_QK ** -0.5
LOG2E = 1.4426950408889634
SWA_SCALE = SWA_DH ** -0.5
ROLL_UP, ROLL_DOWN = MLA_ROPE // 2, LANES - MLA_ROPE // 2

ADAM_LR, ADAM_B1, ADAM_B2, ADAM_EPS, ADAM_WD, ADAM_STEP = 0.001, 0.9, 0.999, 1e-08, 0.01, 10


def _cp(*sem):
    return pltpu.CompilerParams(dimension_semantics=sem, vmem_limit_bytes=VMEM_LIMIT)


def _row(tm, w, col=0):
    return pl.BlockSpec((tm, w), lambda i: (i, col))


def _res(shape, layer=None):
    if layer is None:
        return pl.BlockSpec(shape, lambda *_: (0,) * len(shape), pipeline_mode=pl.Buffered(1))
    return pl.BlockSpec((None,) + shape, lambda *_: (layer,) + (0,) * len(shape), pipeline_mode=pl.Buffered(1))


def _acc(shape):
    return pl.BlockSpec(shape, lambda *_: (0,) * len(shape))


def _rstd(xf):
    return lax.rsqrt(jnp.mean(xf * xf, axis=-1, keepdims=True) + EPS)


def _norm_bwd(dh, n, r, g):
    dn = dh * g
    return r * (dn - n * jnp.mean(dn * n, axis=-1, keepdims=True)), dh * n


def _nt(a, b):
    return lax.dot_general(a, b, (((1,), (1,)), ((), ())), preferred_element_type=F32)


def _tn(a, b):
    return lax.dot_general(a, b, (((0,), (0,)), ((), ())), preferred_element_type=F32)


def _nn(a, b):
    return jnp.dot(a, b, preferred_element_type=F32)


def _sig(x):
    return jax.nn.sigmoid(x)


def _rope(t, c, s1, s2):
    return t * c + pltpu.roll(t, ROLL_UP, 1) * s1 + pltpu.roll(t, ROLL_DOWN, 1) * s2


def _rope_t(d, c, s1, s2):
    return d * c + pltpu.roll(d * s1, ROLL_DOWN, 1) + pltpu.roll(d * s2, ROLL_UP, 1)


def _fwd_in(x, g, w, tm, layer, ride=None):
    T = x.shape[0]
    grid = (T // tm,)

    def body(x_ref, g_ref, w_ref, z_ref, h_ref):
        xf = x_ref[...]
        h = ((xf * _rstd(xf)) * g_ref[...]).astype(BF16)
        h_ref[...] = h
        z_ref[...] = _nn(h, w_ref[...])

    r_in, r_out, r_shape, r_scratch, r_args = _ride_args(ride)
    return pl.pallas_call(
        _riding(ride, body, 3, 2, grid), grid=grid, name="fwd_in_ride" if ride else "fwd_in",
        in_specs=[_row(tm, D), _res((1, D), layer), _res((D, ZW), 0)] + r_in,
        out_specs=[_row(tm, ZW), _row(tm, D)] + r_out,
        out_shape=[SDS((T, ZW), F32), SDS((T, D), BF16)] + r_shape, scratch_shapes=r_scratch,
        compiler_params=_cp("arbitrary"))(x, g, w, *r_args)


def _fwd_prep(z, gq, gkv, wq, wkv, tc, ts1, ts2, tm, layer):
    T = z.shape[0]

    def body(qd_ref, kvd_ref, kr_ref, gq_ref, gkv_ref, wq_ref, wkv_ref, c_ref, s1_ref, s2_ref, q_ref, k_ref, v_ref):
        qd, kvd = qd_ref[...], kvd_ref[...]
        hq = ((qd * _rstd(qd)) * gq_ref[...]).astype(BF16)
        hkv = ((kvd * _rstd(kvd)) * gkv_ref[...]).astype(BF16)
        qf = _nn(hq, wq_ref[...])
        kvf = _nn(hkv, wkv_ref[...])
        c, s1, s2 = c_ref[...], s1_ref[...], s2_ref[...]
        krb = _rope(kr_ref[...], c, s1, s2)
        for h in range(MLA_H):
            sl = slice(LANES * h, LANES * (h + 1))
            q_ref[:, sl] = _rope(qf[:, sl], c, s1, s2).astype(BF16)
            k_ref[:, sl] = (kvf[:, sl] + krb).astype(BF16)
        v_ref[...] = kvf[:, QFW:].astype(BF16)

    return pl.pallas_call(
        body, grid=(T // tm,), name="fwd_prep",
        in_specs=[_row(tm, QL, Z_QD // QL), _row(tm, KVL, Z_KVD // KVL), _row(tm, LANES, Z_KR // LANES),
                  _res((1, QL), layer), _res((1, KVL), layer), _res((QL, QFW), 0), _res((KVL, KVW), 0),
                  _row(tm, LANES), _row(tm, LANES), _row(tm, LANES)],
        out_specs=[_row(tm, QFW), _row(tm, QFW), _row(tm, MLA_H * MLA_V)],
        out_shape=[SDS((T, QFW), BF16), SDS((T, QFW), BF16), SDS((T, MLA_H * MLA_V), BF16)],
        compiler_params=_cp("parallel"))(z, z, z, gq, gkv, wq, wkv, tc, ts1, ts2)


def _grid_ends(grid):
    ids = [pl.program_id(a) for a in range(len(grid))]
    inner_first = functools.reduce(jnp.logical_and, [i == 0 for i in ids[1:]], True)
    last = functools.reduce(jnp.logical_and, [i == g - 1 for i, g in zip(ids, grid)])
    return (ids[0] == 0) & inner_first, (ids[0] == grid[0] // 2) & inner_first, last


class _Ride:
    def __init__(self, kind, arrays):
        self.kind, self.arrays, self.n = kind, list(arrays), len(arrays)

    def out_shape(self):
        if self.kind == "gather":
            return _gather_out(self.arrays)
        if self.kind == "swap":
            return [SDS((a.shape[0],) + a.shape[2:], a.dtype) for a in self.arrays]
        return [SDS(a.shape, a.dtype) for a in self.arrays]

    def sems(self):
        if self.kind == "gather":
            return _gather_sems(self.n)
        if self.kind == "swap":
            return _swap_sems(self.n)
        return _exchange_sems(self.n)

    def phases(self, in_refs, out_refs, *sems):
        if self.kind == "gather":
            return _gather_phases(in_refs, out_refs, *sems)
        start, finish = (_swap_phases if self.kind == "swap" else _exchange_phases)(in_refs, out_refs, *sems)
        return start, None, finish


def _riding(ride, body, n_in, n_out, grid):
    if ride is None:
        return body
    n, n_sems = ride.n, len(ride.sems())

    def wrapped(*refs):
        ins, r_in = refs[:n_in], refs[n_in:n_in + n]
        outs, r_out = refs[n_in + n:n_in + n + n_out], refs[n_in + n + n_out:n_in + 2 * n + n_out]
        rest = refs[n_in + 2 * n + n_out:]
        scratch, sems = rest[:len(rest) - n_sems], rest[len(rest) - n_sems:]
        start, middle, finish = ride.phases(r_in, r_out, *sems)
        at_first, at_middle, at_last = _grid_ends(grid)
        pl.when(at_first)(start)
        if middle is not None:
            pl.when(at_middle)(middle)
        body(*ins, *outs, *scratch)
        pl.when(at_last)(finish)

    return wrapped


def _ride_args(ride):
    if ride is None:
        return [], [], [], [], []
    return [ANY] * ride.n, [ANY] * ride.n, ride.out_shape(), ride.sems(), ride.arrays


def _mla_fwd(qf, kf, v, nb, seq, tq, gather=()):
    T = qf.shape[0]
    nq = seq // tq
    pw = 2 * LANES
    pairs = [(qi, ki) for qi in range(nq) for ki in range(qi + 1)]
    qi_tab = jnp.array([qk[0] for qk in pairs], jnp.int32)
    ki_tab = jnp.array([qk[1] for qk in pairs], jnp.int32)
    grid = (nb, MLA_H // 2, len(pairs))
    n_g = len(gather)

    def body(qi_ref, ki_ref, q_ref, k_ref, v_ref, *rest):
        x_refs, (o_ref, lse_ref), got_refs = rest[:n_g], rest[n_g:n_g + 2], rest[n_g + 2:2 * n_g + 2]
        (m_s, l_s, acc_s), sems = rest[2 * n_g + 2:2 * n_g + 5], rest[2 * n_g + 5:]
        qi, ki = qi_ref[pl.program_id(2)], ki_ref[pl.program_id(2)]
        if n_g:
            start, forward, finish = _gather_phases(x_refs, got_refs, *sems)
            at_first, at_middle, at_last = _grid_ends(grid)
            pl.when(at_first)(start)
            pl.when(at_middle)(forward)

        @pl.when(ki == 0)
        def _():
            m_s[...] = jnp.full(m_s.shape, NEG, F32)
            l_s[...] = jnp.zeros(l_s.shape, F32)
            acc_s[...] = jnp.zeros(acc_s.shape, F32)

        def step(masked):
            parts = [(0, tq // 2, tq // 2), (tq // 2, tq, tq)] if masked else [(0, tq, tq)]
            work = [(j, a, b, kh) for j in range(2) for a, b, kh in parts]
            ss = []
            for j, a, b, kh in work:
                wide = slice(LANES * j, LANES * (j + 1))
                s = _nt(k_ref[:kh, wide], q_ref[a:b, wide]) * (MLA_SCALE * LOG2E)
                if masked:
                    keys = lax.broadcasted_iota(jnp.int32, (kh, b - a), 0)
                    queries = a + lax.broadcasted_iota(jnp.int32, (kh, b - a), 1)
                    s = jnp.where(keys <= queries, s, NEG)
                ss.append(s)
            ps, alphas = [], []
            for (j, a, b, kh), s in zip(work, ss):
                m_prev = m_s[j, :, a:b]
                m_new = jnp.maximum(m_prev, jnp.max(s, axis=0, keepdims=True))
                alpha = jnp.exp2(m_prev - m_new)
                p = jnp.exp2(s - m_new)
                l_s[j, :, a:b] = alpha * l_s[j, :, a:b] + jnp.sum(p, axis=0, keepdims=True)
                m_s[j, :, a:b] = m_new
                ps.append(p.astype(BF16))
                alphas.append(alpha)
            for (j, a, b, kh), p, alpha in zip(work, ps, alphas):
                rows = slice(MLA_V * j, MLA_V * (j + 1))
                acc_s[rows, a:b] = alpha * acc_s[rows, a:b] + _tn(v_ref[:kh, rows], p)

        @pl.when(ki < qi)
        def _():
            step(False)

        @pl.when(ki == qi)
        def _():
            step(True)
            for j in range(2):
                rows = slice(MLA_V * j, MLA_V * (j + 1))
                acc_s[rows, :] = acc_s[rows, :] / l_s[j]
                lse_ref[j:j + 1, :] = m_s[j] + jnp.log2(l_s[j])
            o_ref[...] = acc_s[...].T

        if n_g:
            pl.when(at_last)(finish)

    q_map = lambda b, hp, s, qi_ref, ki_ref: (b * nq + qi_ref[s], hp)
    kv_map = lambda b, hp, s, qi_ref, ki_ref: (b * nq + ki_ref[s], hp)
    return pl.pallas_call(
        body, name="mla_fwd_gather" if n_g else "mla_fwd",
        grid_spec=pltpu.PrefetchScalarGridSpec(
            num_scalar_prefetch=2, grid=grid,
            in_specs=[pl.BlockSpec((tq, pw), q_map), pl.BlockSpec((tq, pw), kv_map),
                      pl.BlockSpec((tq, LANES), kv_map)] + [ANY] * n_g,
            out_specs=[pl.BlockSpec((tq, LANES), q_map),
                       pl.BlockSpec((None, 2, tq), lambda b, hp, s, qi_ref, ki_ref: (hp, 0, b * nq + qi_ref[s]))]
            + [ANY] * n_g,
            scratch_shapes=[pltpu.VMEM((2, 1, tq), F32), pltpu.VMEM((2, 1, tq), F32), pltpu.VMEM((LANES, tq), F32)]
            + (_gather_sems(n_g) if n_g else [])),
        out_shape=[SDS((T, MLA_H * MLA_V), F32), SDS((MLA_H // 2, 2, T), F32)] + _gather_out(gather),
        compiler_params=_cp("arbitrary", "arbitrary", "arbitrary"))(qi_tab, ki_tab, qf, kf, v, *gather)


def _swa_specs(nblk):
    cur = lambda b, n: (b * nblk + n, 0)
    prev = lambda b, n: (b * nblk + jnp.maximum(n - 1, 0), 0)
    kvc = Z_AK // (2 * BLK)
    return [pl.BlockSpec(memory_space=pltpu.SMEM),
            pl.BlockSpec((BLK, 512), lambda b, n: (b * nblk + n, Z_AQ // 512)),
            pl.BlockSpec((BLK, 2 * BLK), lambda b, n: (b * nblk + n, kvc)),
            pl.BlockSpec((BLK, 2 * BLK), lambda b, n: (b * nblk + jnp.maximum(n - 1, 0), kvc)),
            pl.BlockSpec((BLK, 1), cur),
            pl.BlockSpec((BLK, 1), prev),
            pl.BlockSpec((1, 1, BLK), lambda b, n: (b * nblk + n, 0, 0))]


def _swa_scores(n, q_ref, kvc_ref, kvp_ref, pcc_ref, pcp_ref, pr_ref):
    kv = jnp.concatenate([kvp_ref[...], kvc_ref[...]], axis=0)
    kb, vb = kv[:, :BLK].astype(BF16), kv[:, BLK:].astype(BF16)
    dist = pr_ref[0] - jnp.concatenate([pcp_ref[...], pcc_ref[...]], axis=0)
    key = lax.broadcasted_iota(jnp.int32, (2 * BLK, BLK), 0)
    qry = lax.broadcasted_iota(jnp.int32, (2 * BLK, BLK), 1)
    valid = (key > qry) & (key <= qry + BLK) & ((key >= BLK) | (n > 0))

    def scores(h):
        g = h // (SWA_H // SWA_KV)
        qh = q_ref[:, SWA_DH * h:SWA_DH * (h + 1)].astype(BF16)
        s = _nt(kb[:, SWA_DH * g:SWA_DH * (g + 1)], qh) * (SWA_SCALE * LOG2E) - (2.0 ** -(h + 1) * LOG2E) * dist
        return qh, jnp.where(valid, s, NEG)

    return kb, vb, scores


def _swa_fwd(sink, z, pos_col, pos_row, nb, seq, layer):
    T = z.shape[0]
    nblk = seq // BLK

    def body(sink_ref, q_ref, kvc_ref, kvp_ref, pcc_ref, pcp_ref, pr_ref, o_ref, lse_ref):
        kb, vb, scores = _swa_scores(pl.program_id(1), q_ref, kvc_ref, kvp_ref, pcc_ref, pcp_ref, pr_ref)
        ss = [scores(h)[1] for h in range(SWA_H)]
        es, dens = [], []
        for h in range(SWA_H):
            sk = sink_ref[layer, h] * LOG2E
            m = jnp.maximum(jnp.max(ss[h], axis=0, keepdims=True), sk)
            e = jnp.exp2(ss[h] - m)
            den = jnp.sum(e, axis=0, keepdims=True) + jnp.exp2(sk - m)
            lse_ref[h:h + 1, :] = m + jnp.log2(den)
            es.append(e.astype(BF16))
            dens.append(den)
        outs = []
        for h in range(SWA_H):
            g = h // (SWA_H // SWA_KV)
            outs.append(_tn(vb[:, SWA_DH * g:SWA_DH * (g + 1)], es[h]) / dens[h])
        o_ref[...] = jnp.concatenate(outs, axis=0).T

    return pl.pallas_call(
        body, grid=(nb, nblk), name="swa_fwd",
        in_specs=_swa_specs(nblk),
        out_specs=[pl.BlockSpec((BLK, 512), lambda b, n: (b * nblk + n, 0)),
                   pl.BlockSpec((SWA_H, BLK), lambda b, n: (0, b * nblk + n))],
        out_shape=[SDS((T, 512), F32), SDS((SWA_H, T), F32)],
        compiler_params=_cp("parallel", "parallel"))(sink, z, z, z, pos_col, pos_col, pos_row)


def _fwd_merge(x, oa, ob, z, wa, wb, wo, tm, layer):
    T = x.shape[0]

    def body(x_ref, oa_ref, ob_ref, ag_ref, bg_ref, ma_ref, mb_ref, wa_ref, wb_ref, wo_ref, x1_ref):
        ag, bg = ag_ref[...], bg_ref[...]
        ua = _nn((oa_ref[...] * (ag * _sig(ag))).astype(BF16), wa_ref[...])
        ub = _nn((ob_ref[...] * (bg * _sig(bg))).astype(BF16), wb_ref[...])
        y = _sig(ma_ref[...]) * ua + _sig(mb_ref[...]) * ub
        x1_ref[...] = x_ref[...] + _nn(y.astype(BF16), wo_ref[...])

    return pl.pallas_call(
        body, grid=(T // tm,), name="fwd_merge",
        in_specs=[_row(tm, D), _row(tm, 512), _row(tm, 512), _row(tm, 512, Z_AG // 512), _row(tm, 512, Z_BG // 512),
                  _row(tm, D, Z_MA // D), _row(tm, D, Z_MB // D),
                  _res((512, D), 0), _res((512, D), 0), _res((D, D), 0)],
        out_specs=_row(tm, D),
        out_shape=SDS((T, D), F32),
        compiler_params=_cp("parallel"))(x, oa, ob, z, z, z, z, wa, wb, wo)


def _fwd_ple(x1, p, g, wpg, wpp, tm, layer):
    T = x1.shape[0]

    def body(x_ref, p_ref, g_ref, wpg_ref, wpp_ref, x2_ref, pg_ref, pp_ref):
        xf = x_ref[...]
        h1 = ((xf * _rstd(xf)) * g_ref[...]).astype(BF16)
        pg = _sig(_nn(h1, wpg_ref[...]))
        pp = _nn(p_ref[...].astype(BF16), wpp_ref[...])
        pg_ref[...] = pg
        pp_ref[...] = pp
        x2_ref[...] = xf + pg * pp

    return pl.pallas_call(
        body, grid=(T // tm,), name="fwd_ple",
        in_specs=[_row(tm, D), pl.BlockSpec((None, tm, PLE), lambda i: (layer, i, 0)),
                  _res((1, D), layer), _res((D, D), 0), _res((PLE, D), 0)],
        out_specs=[_row(tm, D)] * 3,
        out_shape=[SDS((T, D), F32)] * 3,
        compiler_params=_cp("parallel"))(x1, p, g, wpg, wpp)


def _ple_loss(x1, p, g, wpg, wpp, g_final, tgt, tm, layer):
    T = x1.shape[0]

    def body(x_ref, p_ref, g_ref, wpg_ref, wpp_ref, gf_ref, t_ref, dx_ref, dwg_ref, dwp_ref, dg_ref, dgf_ref, loss_ref):
        @pl.when(pl.program_id(0) == 0)
        def _():
            for ref in (dwg_ref, dwp_ref, dg_ref, dgf_ref, loss_ref):
                ref[...] = jnp.zeros(ref.shape, F32)

        xf, gp, gf = x_ref[...], g_ref[...], gf_ref[...]
        r = _rstd(xf)
        n = xf * r
        h1 = (n * gp).astype(BF16)
        pb = p_ref[...].astype(BF16)
        pg = _sig(_nn(h1, wpg_ref[...]))
        pp = _nn(pb, wpp_ref[...])
        x2 = xf + pg * pp
        r2 = _rstd(x2)
        n2 = x2 * r2
        err = n2 * gf - t_ref[...]
        loss_ref[...] += 0.5 * jnp.sum(jnp.mean(err * err, axis=-1, keepdims=True), axis=0, keepdims=True)
        d, dgfr = _norm_bwd(err * (1.0 / D), n2, r2, gf)
        dgf_ref[...] += jnp.sum(dgfr, axis=0, keepdims=True)
        dpgl = (d * pp * pg * (1.0 - pg)).astype(BF16)
        dwg_ref[...] += _tn(h1, dpgl)
        dwp_ref[...] += _tn(pb, (d * pg).astype(BF16))
        dxn, dgr = _norm_bwd(_nt(dpgl, wpg_ref[...]), n, r, gp)
        dx_ref[...] = d + dxn
        dg_ref[...] += jnp.sum(dgr, axis=0, keepdims=True)

    return pl.pallas_call(
        body, grid=(T // tm,), name="ple_loss",
        in_specs=[_row(tm, D), pl.BlockSpec((None, tm, PLE), lambda i: (layer, i, 0)), _res((1, D), layer),
                  _res((D, D), 0), _res((PLE, D), 0), _res((1, D)), _row(tm, D)],
        out_specs=[_row(tm, D), _acc((D, D)), _acc((PLE, D)), _acc((1, D)), _acc((1, D)), _acc((1, LANES))],
        out_shape=[SDS((T, D), F32), SDS((D, D), F32), SDS((PLE, D), F32), SDS((1, D), F32), SDS((1, D), F32),
                   SDS((1, LANES), F32)],
        compiler_params=_cp("arbitrary"))(x1, p, g, wpg, wpp, g_final, tgt)


def _bwd_ple(dx2, x1, pg, pp, p, g, wpg, tm, layer, ride=None):
    T = x1.shape[0]
    grid = (T // tm,)

    def body(d_ref, x_ref, pg_ref, pp_ref, p_ref, g_ref, w_ref, dx_ref, dwg_ref, dwp_ref, dg_ref):
        @pl.when(pl.program_id(0) == 0)
        def _():
            dwg_ref[...] = jnp.zeros(dwg_ref.shape, F32)
            dwp_ref[...] = jnp.zeros(dwp_ref.shape, F32)
            dg_ref[...] = jnp.zeros(dg_ref.shape, F32)

        d, xf, pg, gf = d_ref[...], x_ref[...], pg_ref[...], g_ref[...]
        r = _rstd(xf)
        n = xf * r
        dpgl = (d * pp_ref[...] * pg * (1.0 - pg)).astype(BF16)
        dwg_ref[...] += _tn((n * gf).astype(BF16), dpgl)
        dwp_ref[...] += _tn(p_ref[...].astype(BF16), (d * pg).astype(BF16))
        dxn, dgr = _norm_bwd(_nt(dpgl, w_ref[...]), n, r, gf)
        dx_ref[...] = d + dxn
        dg_ref[...] += jnp.sum(dgr, axis=0, keepdims=True)

    r_in, r_out, r_shape, r_scratch, r_args = _ride_args(ride)
    return pl.pallas_call(
        _riding(ride, body, 7, 4, grid), grid=grid, name="bwd_ple_ride" if ride else "bwd_ple",
        in_specs=[_row(tm, D)] * 4 + [pl.BlockSpec((None, tm, PLE), lambda i: (layer, i, 0)),
                                      _res((1, D), layer), _res((D, D), 0)] + r_in,
        out_specs=[_row(tm, D), _acc((D, D)), _acc((PLE, D)), _acc((1, D))] + r_out,
        out_shape=[SDS((T, D), F32), SDS((D, D), F32), SDS((PLE, D), F32), SDS((1, D), F32)] + r_shape,
        scratch_shapes=r_scratch,
        compiler_params=_cp("arbitrary"))(dx2, x1, pg, pp, p, g, wpg, *r_args)


def _bwd_merge(dx1, oa, ob, z, wa, wb, wo, tm, layer):
    T = dx1.shape[0]

    def body(d_ref, oa_ref, ob_ref, ag_ref, bg_ref, ma_ref, mb_ref, wa_ref, wb_ref, wo_ref,
             doa_ref, dob_ref, dag_ref, dbg_ref, dma_ref, dmb_ref, dsa_ref, dsb_ref, dwa_ref, dwb_ref, dwo_ref):
        @pl.when(pl.program_id(0) == 0)
        def _():
            dwa_ref[...] = jnp.zeros(dwa_ref.shape, F32)
            dwb_ref[...] = jnp.zeros(dwb_ref.shape, F32)
            dwo_ref[...] = jnp.zeros(dwo_ref.shape, F32)

        db = d_ref[...].astype(BF16)
        gated = []
        for o_ref, gate_ref, w_ref in ((oa_ref, ag_ref, wa_ref), (ob_ref, bg_ref, wb_ref)):
            raw, gate = o_ref[...], gate_ref[...]
            sg = _sig(gate)
            silu = gate * sg
            ob16 = (raw * silu).astype(BF16)
            gated.append((raw, gate, sg, silu, ob16, _nn(ob16, w_ref[...])))
        ua, ub = gated[0][5], gated[1][5]
        sa, sb = _sig(ma_ref[...]), _sig(mb_ref[...])
        dwo_ref[...] += _tn((sa * ua + sb * ub).astype(BF16), db)
        dy = _nt(db, wo_ref[...])
        dma_ref[...] = (dy * ua * sa * (1.0 - sa)).astype(BF16)
        dmb_ref[...] = (dy * ub * sb * (1.0 - sb)).astype(BF16)
        for (s, w_ref, do_ref, dgate_ref, dw_ref, ds_ref), (raw, gate, sg, silu, ob16, _) in zip((
                (sa, wa_ref, doa_ref, dag_ref, dwa_ref, dsa_ref),
                (sb, wb_ref, dob_ref, dbg_ref, dwb_ref, dsb_ref)), gated):
            du = (dy * s).astype(BF16)
            dw_ref[...] += _tn(ob16, du)
            do = _nt(du, w_ref[...])
            draw = do * silu
            do_ref[...] = draw.astype(BF16)
            dgate_ref[...] = (do * raw * (sg * (1.0 + gate * (1.0 - sg)))).astype(BF16)
            ds_ref[...] = jnp.sum((draw * raw).T.reshape(MLA_H, MLA_V, tm), axis=1)

    return pl.pallas_call(
        body, grid=(T // tm,), name="bwd_merge",
        in_specs=[_row(tm, D), _row(tm, 512), _row(tm, 512), _row(tm, 512, Z_AG // 512), _row(tm, 512, Z_BG // 512),
                  _row(tm, D, Z_MA // D), _row(tm, D, Z_MB // D),
                  _res((512, D), 0), _res((512, D), 0), _res((D, D), 0)],
        out_specs=[_row(tm, 512)] * 4 + [_row(tm, D)] * 2 + [pl.BlockSpec((MLA_H, tm), lambda i: (0, i))] * 2
        + [_acc((512, D)), _acc((512, D)), _acc((D, D))],
        out_shape=[SDS((T, 512), BF16), SDS((T, 512), BF16), SDS((T, 512), BF16), SDS((T, 512), BF16),
                   SDS((T, D), BF16), SDS((T, D), BF16), SDS((MLA_H, T), F32), SDS((MLA_H, T), F32),
                   SDS((512, D), F32), SDS((512, D), F32), SDS((D, D), F32)],
        compiler_params=_cp("arbitrary"))(dx1, oa, ob, z, z, z, z, wa, wb, wo)


def _mla_bwd(qf, kf, v, do, lse, dsum, nb, seq, tq, exchange=()):
    T = qf.shape[0]
    nq = seq // tq
    pw = 2 * LANES
    pairs = [(qi, ki) for ki in range(nq) for qi in range(ki, nq)]
    qi_tab = jnp.array([qk[0] for qk in pairs], jnp.int32)
    ki_tab = jnp.array([qk[1] for qk in pairs], jnp.int32)
    grid = (nb, MLA_H // 2, len(pairs))
    n_x = len(exchange)

    def body(qi_ref, ki_ref, q_ref, k_ref, v_ref, do_ref, lse_ref, dsum_ref, *rest):
        p_refs, (dq_ref, dk_ref, dv_ref), got_refs = rest[:n_x], rest[n_x:n_x + 3], rest[n_x + 3:2 * n_x + 3]
        (dk_s, dv_s, dqt_s), sems = rest[2 * n_x + 3:2 * n_x + 6], rest[2 * n_x + 6:]
        step_id = pl.program_id(2)
        qi, ki = qi_ref[step_id], ki_ref[step_id]
        if n_x:
            start, finish = _exchange_phases(p_refs, got_refs, *sems)
            at_first, _, at_last = _grid_ends(grid)
            pl.when(at_first)(start)

        @pl.when(step_id == 0)
        def _():
            dqt_s[...] = jnp.zeros(dqt_s.shape, F32)

        @pl.when(qi == ki)
        def _():
            dk_s[...] = jnp.zeros(dk_s.shape, F32)
            dv_s[...] = jnp.zeros(dv_s.shape, F32)

        def step(masked):
            if masked:
                keys = lax.broadcasted_iota(jnp.int32, (tq, tq), 0)
                queries = lax.broadcasted_iota(jnp.int32, (tq, tq), 1)
                mask = keys <= queries
            for j in range(2):
                wide = slice(LANES * j, LANES * (j + 1))
                sl = slice(MLA_V * j, MLA_V * (j + 1))
                q, k = q_ref[:, wide], k_ref[:, wide]
                dob = do_ref[:, sl].astype(BF16)
                s = _nt(k, q) * (MLA_SCALE * LOG2E)
                if masked:
                    s = jnp.where(mask, s, NEG)
                p = jnp.exp2(s - lse_ref[j:j + 1, :])
                dv_s[:, sl] += _nn(p.astype(BF16), dob)
                ds = (p * (_nt(v_ref[:, sl], dob) - dsum_ref[j:j + 1, :]) * MLA_SCALE).astype(BF16)
                dk_s[:, wide] += _nn(ds, q)
                dqt_s[qi, wide, :] += _tn(k, ds)

        @pl.when(qi > ki)
        def _():
            step(False)

        @pl.when(qi == ki)
        def _():
            step(True)

        @pl.when(qi == nq - 1)
        def _():
            dk_ref[...] = dk_s[...]
            dv_ref[...] = dv_s[...]

        @pl.when(step_id == len(pairs) - 1)
        def _():
            for n in range(nq):
                dq_ref[tq * n:tq * (n + 1), :] = dqt_s[n].T

        if n_x:
            pl.when(at_last)(finish)

    qmap = lambda b, hp, s, qi_ref, ki_ref: (b * nq + qi_ref[s], hp)
    kmap = lambda b, hp, s, qi_ref, ki_ref: (b * nq + ki_ref[s], hp)
    stat = pl.BlockSpec((None, 2, tq), lambda b, hp, s, qi_ref, ki_ref: (hp, 0, b * nq + qi_ref[s]))
    return pl.pallas_call(
        body, name="mla_bwd_exchange" if n_x else "mla_bwd",
        grid_spec=pltpu.PrefetchScalarGridSpec(
            num_scalar_prefetch=2, grid=grid,
            in_specs=[pl.BlockSpec((tq, pw), qmap), pl.BlockSpec((tq, pw), kmap), pl.BlockSpec((tq, LANES), kmap),
                      pl.BlockSpec((tq, LANES), qmap), stat, stat] + [ANY] * n_x,
            out_specs=[pl.BlockSpec((seq, pw), lambda b, hp, s, qi_ref, ki_ref: (b, hp)),
                       pl.BlockSpec((tq, pw), kmap), pl.BlockSpec((tq, LANES), kmap)] + [ANY] * n_x,
            scratch_shapes=[pltpu.VMEM((tq, pw), F32), pltpu.VMEM((tq, LANES), F32), pltpu.VMEM((nq, pw, tq), F32)]
            + (_exchange_sems(n_x) if n_x else [])),
        out_shape=[SDS((T, QFW), F32), SDS((T, QFW), F32), SDS((T, MLA_H * MLA_V), F32)]
        + [SDS(a.shape, a.dtype) for a in exchange],
        compiler_params=_cp("arbitrary", "arbitrary", "arbitrary"))(qi_tab, ki_tab, qf, kf, v, do, lse, dsum, *exchange)


def _swa_bwd(sink, z, pos_col, pos_row, do, lse, dsum, nb, seq, layer, ride=None):
    T = z.shape[0]
    nblk = seq // BLK

    def body(sink_ref, q_ref, kvc_ref, kvp_ref, pcc_ref, pcp_ref, pr_ref, do_ref, lse_ref, dsum_ref,
             dq_ref, dkv_ref, dsink_ref):
        b, n = pl.program_id(0), pl.program_id(1)

        @pl.when((b == 0) & (n == 0))
        def _():
            dsink_ref[...] = jnp.zeros(dsink_ref.shape, F32)

        @pl.when(n == 0)
        def _():
            dkv_ref[...] = jnp.zeros(dkv_ref.shape, F32)

        kb, vb, scores = _swa_scores(n, q_ref, kvc_ref, kvp_ref, pcc_ref, pcp_ref, pr_ref)
        lane = lax.broadcasted_iota(jnp.int32, (1, LANES), 1)
        dsink = jnp.zeros((1, LANES), F32)
        dkv = [[None, None], [None, None]]
        dqs = []
        gsl = lambda h: slice(SWA_DH * (h // (SWA_H // SWA_KV)), SWA_DH * (h // (SWA_H // SWA_KV) + 1))
        qs, ss, dobs, dps = [], [], [], []
        for h in range(SWA_H):
            qh, s = scores(h)
            dob = do_ref[:, SWA_DH * h:SWA_DH * (h + 1)].astype(BF16)
            qs.append(qh)
            ss.append(s)
            dobs.append(dob)
            dps.append(_nt(vb[:, gsl(h)], dob))
        pbs, dss = [], []
        for h in range(SWA_H):
            lse, dsum = lse_ref[h:h + 1, :], dsum_ref[h:h + 1, :]
            p = jnp.exp2(ss[h] - lse)
            pbs.append(p.astype(BF16))
            dss.append((p * (dps[h] - dsum) * SWA_SCALE).astype(BF16))
            dsk = jnp.sum(-jnp.exp2(sink_ref[layer, h] * LOG2E - lse) * dsum, axis=1, keepdims=True)
            dsink = dsink + jnp.where(lane == h, dsk, 0.0)
        for h in range(SWA_H):
            g = h // (SWA_H // SWA_KV)
            dqs.append(_tn(kb[:, gsl(h)], dss[h]))
            dk, dv = _nn(dss[h], qs[h]), _nn(pbs[h], dobs[h])
            dkv[g][0] = dk if dkv[g][0] is None else dkv[g][0] + dk
            dkv[g][1] = dv if dkv[g][1] is None else dkv[g][1] + dv
        dq_ref[...] = jnp.concatenate(dqs, axis=0).T.astype(BF16)
        dsink_ref[...] += dsink
        upd = jnp.concatenate([dkv[0][0], dkv[1][0], dkv[0][1], dkv[1][1]], axis=1)
        dkv_ref[pl.ds(pl.multiple_of(n * BLK, BLK), BLK), :] += upd[BLK:]

        @pl.when(n > 0)
        def _():
            dkv_ref[pl.ds(pl.multiple_of((n - 1) * BLK, BLK), BLK), :] += upd[:BLK]

    r_in, r_out, r_shape, r_scratch, r_args = _ride_args(ride)
    return pl.pallas_call(
        _riding(ride, body, 10, 3, (nb, nblk)), grid=(nb, nblk), name="swa_bwd_ride" if ride else "swa_bwd",
        in_specs=_swa_specs(nblk) + [pl.BlockSpec((BLK, 512), lambda b, n: (b * nblk + n, 0))]
        + [pl.BlockSpec((SWA_H, BLK), lambda b, n: (0, b * nblk + n))] * 2 + r_in,
        out_specs=[pl.BlockSpec((BLK, 512), lambda b, n: (b * nblk + n, 0)),
                   pl.BlockSpec((seq, 2 * BLK), lambda b, n: (b, 0)),
                   pl.BlockSpec((1, LANES), lambda b, n: (0, 0))] + r_out,
        out_shape=[SDS((T, 512), BF16), SDS((T, 2 * BLK), F32), SDS((1, LANES), F32)] + r_shape,
        scratch_shapes=r_scratch,
        compiler_params=_cp("arbitrary", "arbitrary"))(sink, z, z, z, pos_col, pos_col, pos_row, do, lse, dsum,
                                                       *r_args)


def _bwd_prep(dq, dk, dv, z, gq, gkv, wq, wkv, tc, ts1, ts2, tm, layer):
    T = z.shape[0]

    def body(dq_ref, dk_ref, dv_ref, qd_ref, kvd_ref, gq_ref, gkv_ref, wq_ref, wkv_ref, c_ref, s1_ref, s2_ref,
             dqd_ref, dkvd_ref, dkr_ref, dwq_ref, dwkv_ref, dgq_ref, dgkv_ref, dqb_s, dkvb_s):
        @pl.when(pl.program_id(0) == 0)
        def _():
            for ref in (dwq_ref, dwkv_ref, dgq_ref, dgkv_ref):
                ref[...] = jnp.zeros(ref.shape, F32)

        c, s1, s2 = c_ref[...], s1_ref[...], s2_ref[...]
        lane = lax.broadcasted_iota(jnp.int32, (1, LANES), 1)
        rope_lanes = (lane >= MLA_NOPE) & (lane < MLA_QK)
        dkb = jnp.zeros((tm, LANES), F32)
        for h in range(MLA_H):
            sl = slice(LANES * h, LANES * (h + 1))
            dqb_s[:, sl] = _rope_t(dq_ref[:, sl], c, s1, s2).astype(BF16)
            dkh = dk_ref[:, sl]
            dkb = dkb + dkh
            dkvb_s[:, sl] = dkh.astype(BF16)
        dkvb_s[:, QFW:] = dv_ref[...].astype(BF16)
        dkr_ref[...] = _rope_t(jnp.where(rope_lanes, dkb, 0.0), c, s1, s2).astype(BF16)

        for (x_ref, g_ref, w_ref, d_s, dx_ref, dw_ref, dg_ref) in (
                (qd_ref, gq_ref, wq_ref, dqb_s, dqd_ref, dwq_ref, dgq_ref),
                (kvd_ref, gkv_ref, wkv_ref, dkvb_s, dkvd_ref, dwkv_ref, dgkv_ref)):
            xf, gf, db = x_ref[...], g_ref[...], d_s[...]
            r = _rstd(xf)
            n = xf * r
            dw_ref[...] += _tn((n * gf).astype(BF16), db)
            dx, dgr = _norm_bwd(_nt(db, w_ref[...]), n, r, gf)
            dx_ref[...] = dx.astype(BF16)
            dg_ref[...] += jnp.sum(dgr, axis=0, keepdims=True)

    return pl.pallas_call(
        body, grid=(T // tm,), name="bwd_prep",
        in_specs=[_row(tm, QFW), _row(tm, QFW), _row(tm, MLA_H * MLA_V),
                  _row(tm, QL, Z_QD // QL), _row(tm, KVL, Z_KVD // KVL),
                  _res((1, QL), layer), _res((1, KVL), layer), _res((QL, QFW), 0), _res((KVL, KVW), 0),
                  _row(tm, LANES), _row(tm, LANES), _row(tm, LANES)],
        out_specs=[_row(tm, QL), _row(tm, KVL), _row(tm, LANES),
                   _acc((QL, QFW)), _acc((KVL, KVW)), _acc((1, QL)), _acc((1, KVL))],
        out_shape=[SDS((T, QL), BF16), SDS((T, KVL), BF16), SDS((T, LANES), BF16),
                   SDS((QL, QFW), F32), SDS((KVL, KVW), F32), SDS((1, QL), F32), SDS((1, KVL), F32)],
        scratch_shapes=[pltpu.VMEM((tm, QFW), BF16), pltpu.VMEM((tm, KVW), BF16)],
        compiler_params=_cp("arbitrary"))(dq, dk, dv, z, z, gq, gkv, wq, wkv, tc, ts1, ts2)


def _bwd_in(pieces, x, g, dres, w, tm, layer):
    T = x.shape[0]
    grid = (T // tm,)
    widths = [pc.shape[1] for pc in pieces]
    assert sum(widths) == ZW
    n_p = len(pieces)

    def body(*refs):
        p_refs, (x_ref, g_ref, r_ref, w_ref, dx_ref, dz_ref, dg_ref) = refs[:n_p], refs[n_p:]

        @pl.when(pl.program_id(0) == 0)
        def _():
            dg_ref[...] = jnp.zeros(dg_ref.shape, F32)

        off = 0
        for ref, wd in zip(p_refs, widths):
            dz_ref[:, off:off + wd] = ref[...].astype(BF16)
            off += wd
        xf, gf = x_ref[...], g_ref[...]
        r = _rstd(xf)
        n = xf * r
        dx, dgr = _norm_bwd(_nt(dz_ref[...], w_ref[...]), n, r, gf)
        dx_ref[...] = r_ref[...] + dx
        dg_ref[...] += jnp.sum(dgr, axis=0, keepdims=True)

    return pl.pallas_call(
        body, grid=grid, name="bwd_in",
        in_specs=[_row(tm, wd) for wd in widths] + [_row(tm, D), _res((1, D), layer), _row(tm, D),
                                                    _res((D, ZW), 0)],
        out_specs=[_row(tm, D), _row(tm, ZW), _acc((1, D))],
        out_shape=[SDS((T, D), F32), SDS((T, ZW), BF16), SDS((1, D), F32)],
        compiler_params=_cp("arbitrary"))(*pieces, x, g, dres, w)


def _wgrad_in(hb, dzb, tm, ride=None):
    T = hb.shape[0]
    half = ZW // 2
    grid = (2, T // tm)

    def body(h_ref, dz_ref, dw_ref):
        @pl.when(pl.program_id(1) == 0)
        def _():
            dw_ref[...] = jnp.zeros(dw_ref.shape, F32)

        dw_ref[...] += _tn(h_ref[...], dz_ref[...])

    r_in, r_out, r_shape, r_scratch, r_args = _ride_args(ride)
    out = pl.pallas_call(
        _riding(ride, body, 2, 1, grid), grid=grid, name="wgrad_in_ride" if ride else "wgrad_in",
        in_specs=[pl.BlockSpec((tm, D), lambda j, t: (t, 0)), pl.BlockSpec((tm, half), lambda j, t: (t, j))] + r_in,
        out_specs=[pl.BlockSpec((D, half), lambda j, t: (0, j))] + r_out,
        out_shape=[SDS((D, ZW), F32)] + r_shape, scratch_shapes=r_scratch,
        compiler_params=_cp("arbitrary", "arbitrary"))(hb, dzb, *r_args)
    return out


IN_PIECES = ((0, 512, Z_AQ), (512, 128, Z_AK), (640, 128, Z_AV), (768, 512, Z_AG), (1280, 256, Z_QD),
             (1536, 128, Z_KVD), (1664, MLA_ROPE, Z_KR + MLA_NOPE), (1696, 512, Z_BG), (2208, 1024, Z_MA),
             (3232, 1024, Z_MB))
WIDE_W = IN_W // N_DEV


def _column_runs():
    runs = []
    for start, width, kstart in IN_PIECES:
        col = start
        while col < start + width:
            dev = col // WIDE_W
            stop = min(start + width, (dev + 1) * WIDE_W)
            runs.append((dev, col - dev * WIDE_W, stop - col, kstart + col - start))
            col = stop
    return runs


def _win_layout(blocks, tm):
    runs = _column_runs()

    def body(g_ref, o_ref):
        o_ref[:, Z_KR:Z_KR + LANES] = jnp.zeros((tm, LANES), o_ref.dtype)
        for dev, lo, n, k in runs:
            o_ref[:, k:k + n] = g_ref[dev, :, lo:lo + n]

    return pl.pallas_call(
        body, grid=(D // tm,), name="win_layout",
        in_specs=[pl.BlockSpec((N_DEV, None, tm, WIDE_W), lambda i: (0, 0, i, 0))],
        out_specs=pl.BlockSpec((None, tm, ZW), lambda i: (0, i, 0)),
        out_shape=SDS((1, D, ZW), blocks.dtype),
        compiler_params=_cp("parallel"))(blocks)


def _win_grad_layout(dw, tm):
    runs = _column_runs()

    def body(g_ref, o_ref):
        for dev, lo, n, k in runs:
            o_ref[dev, :, lo:lo + n] = g_ref[:, k:k + n]

    return pl.pallas_call(
        body, grid=(D // tm,), name="win_grad_layout",
        in_specs=[_row(tm, ZW)],
        out_specs=pl.BlockSpec((N_DEV, None, tm, WIDE_W), lambda i: (0, 0, i, 0)),
        out_shape=SDS((N_DEV, 1, D, WIDE_W), F32),
        compiler_params=_cp("parallel"))(dw)


def _wuq_to_kernel(w):
    w = w.reshape(w.shape[:-1] + (MLA_H, MLA_QK))
    w = jnp.pad(w, [(0, 0)] * (w.ndim - 1) + [(0, LANES - MLA_QK)])
    return w.reshape(w.shape[:-2] + (QFW,))


def _wuq_from_kernel(g):
    g = g.reshape(g.shape[:-1] + (MLA_H, LANES))[..., :MLA_QK]
    return g.reshape(g.shape[:-2] + (MLA_H * MLA_QK,))


def _wukv_to_kernel(w):
    w = w.reshape(w.shape[:-1] + (MLA_H, MLA_NOPE + MLA_V))
    k = jnp.pad(w[..., :MLA_NOPE], [(0, 0)] * (w.ndim - 1) + [(0, LANES - MLA_NOPE)])
    v = w[..., MLA_NOPE:]
    return jnp.concatenate([k.reshape(k.shape[:-2] + (QFW,)), v.reshape(v.shape[:-2] + (MLA_H * MLA_V,))], axis=-1)


def _wukv_from_kernel(g):
    k = g[..., :QFW].reshape(g.shape[:-1] + (MLA_H, LANES))[..., :MLA_NOPE]
    v = g[..., QFW:].reshape(g.shape[:-1] + (MLA_H, MLA_V))
    kv = jnp.concatenate([k, v], axis=-1)
    return kv.reshape(kv.shape[:-2] + (MLA_H * (MLA_NOPE + MLA_V),))


def _rope_tables(pos):
    half = MLA_ROPE // 2
    inv = 10000.0 ** (-jnp.arange(0, MLA_ROPE, 2, dtype=F32) / MLA_ROPE)
    ang = pos.astype(F32)[:, None] * inv
    cos, sin = jnp.cos(ang), jnp.sin(ang)
    one = jnp.ones((pos.shape[0], MLA_NOPE), F32)
    zero = lambda n: jnp.zeros((pos.shape[0], n), F32)
    tc = jnp.concatenate([one, cos, cos, one[:, :LANES - MLA_QK]], axis=1)
    ts1 = jnp.concatenate([zero(MLA_NOPE + half), sin, zero(LANES - MLA_QK)], axis=1)
    ts2 = jnp.concatenate([zero(MLA_NOPE), -sin, zero(LANES - MLA_NOPE - half)], axis=1)
    return tc, ts1, ts2


def _local_step(x, p, positions, loss_target, small, wts, plan=None):
    nb, seq, _ = x.shape
    T = nb * seq
    tm = min(512, T)
    tl = min(1024, T)
    tq = min(512, seq)
    xf = x.reshape(T, D)
    pos = positions.reshape(T)
    posf = pos.astype(F32)
    pos_col, pos_row = posf.reshape(T, 1), posf.reshape(T // BLK, 1, BLK)
    tc, ts1, ts2 = _rope_tables(pos)

    wts, sm = list(wts), small
    pl_in = p.reshape(DEPTH, T, PLE)
    saved = []
    for i in range(DEPTH):
        riding = plan is not None and i == 0
        w = wts[i]
        z, hb, *got = _fwd_in(xf, sm["g_mix"], w["w_in"], tm, i,
                              ride=_Ride("gather", plan["rows0"]) if riding else None)
        if riding:
            w = wts[0] = dict(w, **plan["row_weights"](got[0]))
        oa, lse_a = _swa_fwd(sm["sink"], z, pos_col, pos_row, nb, seq, i)
        qf, kf, v = _fwd_prep(z, sm["g_q"], sm["g_kv"], w["w_uq"], w["w_ukv"], tc, ts1, ts2, tl, i)
        ob, lse_b, *got = _mla_fwd(qf, kf, v, nb, seq, tq, gather=plan["blocks1"] if riding else ())
        if riding:
            wts.append(dict(w_in=plan["w_in"](got[0]), **plan["row_weights"](got[1])))
        x1 = _fwd_merge(xf, oa, ob, z, w["w_br_a"], w["w_br_b"], w["w_out"], tm, i)
        saved.append(dict(x=xf, z=z, hb=hb, oa=oa, lse_a=lse_a, qf=qf, kf=kf, v=v, ob=ob, lse_b=lse_b, x1=x1))
        if i < DEPTH - 1:
            xf, saved[i]["pg"], saved[i]["pp"] = _fwd_ple(x1, pl_in, sm["g_ple"], w["w_ple_gate"], w["w_ple_proj"],
                                                          tl, i)

    last = _ple_loss(x1, pl_in, sm["g_ple"], w["w_ple_gate"], w["w_ple_proj"], small["g_final"],
                     loss_target.reshape(T, D), tm, DEPTH - 1)
    dg_final, loss = last[4], last[5]

    grads = [None] * DEPTH
    exchanged = {}
    for i in reversed(range(DEPTH)):
        riding = plan is not None and i == 0
        sv, w = saved[i], wts[i]
        pay = plan["payload"](grads[1]) if riding else []
        if i == DEPTH - 1:
            (dx1, dwpg, dwpp, dg_ple), got = last[:4], []
        else:
            dx1, dwpg, dwpp, dg_ple, *got = _bwd_ple(dx, sv["x1"], sv["pg"], sv["pp"], pl_in, sm["g_ple"],
                                                     w["w_ple_gate"], tm, i,
                                                     ride=_Ride("swap", pay) if riding else None)
        doa, dob, dag, dbg, dma, dmb, dsum_a, dsum_b, dwa, dwb, dwo = _bwd_merge(
            dx1, sv["oa"], sv["ob"], sv["z"], w["w_br_a"], w["w_br_b"], w["w_out"], tm, i)
        dq_b, dk_b, dv_b, *exchanged["layer1"] = _mla_bwd(
            sv["qf"], sv["kf"], sv["v"], dob, sv["lse_b"], dsum_b.reshape(MLA_H // 2, 2, T), nb, seq, tq,
            exchange=plan["add"](pay, got) if riding else ())
        dqd, dkvd, dkr, dwq, dwkv, dgq, dgkv = _bwd_prep(dq_b, dk_b, dv_b, sv["z"], sm["g_q"], sm["g_kv"],
                                                         w["w_uq"], w["w_ukv"], tc, ts1, ts2, tl, i)
        g = dict(w_uq=_wuq_from_kernel(dwq), w_ukv=_wukv_from_kernel(dwkv), w_br_a=dwa, w_br_b=dwb, w_out=dwo,
                 w_ple_gate=dwpg, w_ple_proj=dwpp)
        pay = [plan["rows_payload"](g)] if riding else []
        dq_a, dkv_a, dsink, *got = _swa_bwd(sm["sink"], sv["z"], pos_col, pos_row, doa, sv["lse_a"], dsum_a, nb, seq,
                                            i, ride=_Ride("swap", pay) if riding else None)
        dx, dzb, dg_mix = _bwd_in([dma, dmb, dq_a, dag, dbg, dqd, dkv_a, dkvd, dkr], sv["x"], sm["g_mix"], dx1,
                                  w["w_in"], tm, i)
        dwin, *exchanged["rows0"] = _wgrad_in(sv["hb"], dzb, tm,
                                              ride=_Ride("exchange", plan["add"](pay, got)) if riding else None)
        g.update(g_mix=dg_mix[0], w_in=dwin, sink=dsink[0, :SWA_H], g_q=dgq[0], g_kv=dgkv[0], g_ple=dg_ple[0])
        grads[i] = g
    return loss, dx.reshape(nb, seq, D), grads, dg_final[0], exchanged


def _row_weights(rows):
    blocks = _unpack_rows(rows)
    out = {n: _join(n, blocks[n]) for n, _ in ROWS_PIECES}
    out.update(w_uq=_wuq_to_kernel(out["w_uq"]), w_ukv=_wukv_to_kernel(out["w_ukv"]))
    return out


def _small_params(g_mix, sink, g_q, g_kv, g_ple, g_final):
    return dict(g_mix=g_mix[:, None], sink=sink, g_q=g_q[:, None], g_kv=g_kv[:, None], g_ple=g_ple[:, None],
                g_final=g_final[None])


UQ_W = MLA_H * MLA_QK // N_DEV
ROWS_PIECES = (("w_uq", QL), ("w_ukv", KVL), ("w_br_a", 512), ("w_br_b", 512), ("w_out", D), ("w_ple_gate", D),
               ("w_ple_proj", PLE))
SMALL = (("g_mix", (DEPTH, D)), ("sink", (DEPTH, SWA_H)), ("g_q", (DEPTH, QL)), ("g_kv", (DEPTH, KVL)),
         ("g_ple", (DEPTH, D)), ("g_final", (D,)))
VEC_ROWS = 48
ROWS_N = sum(r for _, r in ROWS_PIECES)
WIDE_TILE, ROWS_TILE = 256, ROWS_N // 2


def _to_rows(name, a):
    if name == "w_uq":
        a = jnp.pad(a, [(0, 0)] * (a.ndim - 1) + [(0, LANES - UQ_W)])
    return a.reshape(a.shape[:-2] + (-1, LANES))


def _from_rows(name, r):
    if name in ("w_out", "w_ple_gate"):
        return r.reshape(r.shape[:-2] + (D // N_DEV, D))
    return r[..., :UQ_W] if name == "w_uq" else r


def _pack_rows(blocks):
    return jnp.concatenate([_to_rows(n, blocks[n]) for n, _ in ROWS_PIECES], axis=-2)


def _unpack_rows(rows):
    blocks, off = {}, 0
    for n, r in ROWS_PIECES:
        blocks[n] = _from_rows(n, rows[..., off:off + r, :])
        off += r
    return blocks


def _pack_vec(vectors, loss=None):
    parts = [vectors[n].reshape(-1) for n, _ in SMALL] + ([] if loss is None else [loss.reshape(1)])
    vec = jnp.concatenate(parts)
    return jnp.pad(vec, (0, VEC_ROWS * LANES - vec.shape[0])).reshape(1, VEC_ROWS, LANES)


def _unpack_vec(vec):
    vec = vec.reshape(-1)
    vectors, off = {}, 0
    for n, shp in SMALL:
        size = 1
        for s in shp:
            size *= s
        vectors[n] = vec[off:off + size].reshape(shp)
        off += size
    return vectors, vec[off]


def _join(name, blocks):
    if name in ("w_out", "w_ple_gate"):
        return jnp.moveaxis(blocks, 0, 1).reshape(blocks.shape[1], -1, blocks.shape[-1])
    return jnp.moveaxis(blocks, 0, 2).reshape(blocks.shape[1], blocks.shape[2], -1)


def _split(name, full):
    if name in ("w_out", "w_ple_gate"):
        return jnp.moveaxis(full.reshape(full.shape[0], N_DEV, -1, full.shape[-1]), 1, 0)
    return jnp.moveaxis(full.reshape(full.shape[0], full.shape[1], N_DEV, -1), 2, 0)


MESH_ID = pl.DeviceIdType.MESH
ANY = pl.BlockSpec(memory_space=pl.ANY)


def _place():
    return lax.axis_index("x"), lax.axis_index("y"), lax.axis_index("c")


def _all_gather(blocks):
    n = len(blocks)

    def body(*refs):
        start, forward, finish = _gather_phases(refs[:n], refs[n:2 * n], *refs[2 * n:])
        start()
        forward()
        finish()

    return pl.pallas_call(
        body, name="all_gather_weights", out_shape=_gather_out(blocks),
        in_specs=[ANY] * n, out_specs=[ANY] * n, scratch_shapes=_gather_sems(n))(*blocks)


def _gather_out(blocks):
    return [SDS((N_DEV,) + b.shape, b.dtype) for b in blocks]


def _gather_sems(n):
    return [pltpu.SemaphoreType.DMA((7 * n,)), pltpu.SemaphoreType.DMA((7 * n,)), pltpu.SemaphoreType.DMA((n,))]


def _gather_phases(x_refs, out_refs, send_sems, recv_sems, local_sems):
    n = len(x_refs)
    x, y, c = _place()
    me, sibling = (x, y, c), (x, y, 1 - c)
    chips = [(1 - x, y), (x, 1 - y), (1 - x, 1 - y)]

    def slot(a, px, py, pc):
        return out_refs[a].at[4 * px + 2 * py + pc]

    def copy(a, k, blk, to, src=None):
        return pltpu.make_async_remote_copy(
            src_ref=slot(a, *blk) if src is None else src, dst_ref=slot(a, *blk),
            send_sem=send_sems.at[7 * a + k], recv_sem=recv_sems.at[7 * a + k], device_id=to,
            device_id_type=MESH_ID)

    def mine():
        return [pltpu.make_async_copy(x_refs[a], slot(a, *me), local_sems.at[a]) for a in range(n)]

    def first():
        out = []
        for a in range(n):
            out += [copy(a, 0, me, sibling, src=x_refs[a])]
            out += [copy(a, 1 + j, me, (*chip, c), src=x_refs[a]) for j, chip in enumerate(chips)]
        return out

    def passed():
        return [copy(a, 4 + j, (*chip, c), sibling) for j, chip in enumerate(chips) for a in range(n)]

    def start():
        for cp in mine() + first():
            cp.start()

    def forward():
        for j, chip in enumerate(chips):
            for a in range(n):
                copy(a, 1 + j, (*chip, c), me).wait_recv()
                copy(a, 4 + j, (*chip, c), sibling).start()

    def finish():
        for a in range(n):
            copy(a, 0, sibling, me).wait_recv()
            for j, chip in enumerate(chips):
                copy(a, 4 + j, (*chip, 1 - c), me).wait_recv()
        for cp in first() + passed():
            cp.wait_send()
        for cp in mine():
            cp.wait()

    return start, forward, finish


def _swap_sibling(arrs):
    n = len(arrs)

    def body(*refs):
        start, finish = _swap_phases(refs[:n], refs[n:2 * n], *refs[2 * n:])
        start()
        finish()

    return pl.pallas_call(
        body, name="swap_sibling", out_shape=[SDS((a.shape[0],) + a.shape[2:], a.dtype) for a in arrs],
        in_specs=[ANY] * n, out_specs=[ANY] * n, scratch_shapes=_swap_sems(n))(*arrs)


def _swap_sems(n):
    return [pltpu.SemaphoreType.DMA((n,)), pltpu.SemaphoreType.DMA((n,))]


def _swap_phases(a_refs, out_refs, send_sems, recv_sems):
    x, y, c = _place()

    def copies():
        return [pltpu.make_async_remote_copy(
            src_ref=a_refs[a].at[:, 1 - c], dst_ref=out_refs[a], send_sem=send_sems.at[a], recv_sem=recv_sems.at[a],
            device_id=(x, y, 1 - c), device_id_type=MESH_ID) for a in range(len(a_refs))]

    def start():
        for cp in copies():
            cp.start()

    def finish():
        for cp in copies():
            cp.wait()

    return start, finish


def _exchange_chips(arrs):
    n = len(arrs)

    def body(*refs):
        start, finish = _exchange_phases(refs[:n], refs[n:2 * n], *refs[2 * n:])
        start()
        finish()

    return pl.pallas_call(
        body, name="exchange_chips", out_shape=[SDS(a.shape, a.dtype) for a in arrs],
        in_specs=[ANY] * n, out_specs=[ANY] * n, scratch_shapes=_exchange_sems(n))(*arrs)


def _exchange_sems(n):
    return [pltpu.SemaphoreType.DMA((3 * n,)), pltpu.SemaphoreType.DMA((3 * n,)), pltpu.SemaphoreType.DMA((n,))]


def _exchange_phases(p_refs, out_refs, send_sems, recv_sems, local_sems):
    n = len(p_refs)
    x, y, c = _place()
    mine = 2 * x + y
    peers = [(1 - x, y), (x, 1 - y), (1 - x, 1 - y)]

    def local():
        return [pltpu.make_async_copy(p_refs[a].at[mine], out_refs[a].at[mine], local_sems.at[a]) for a in range(n)]

    def copy(a, j, src_chip, dst_chip):
        px, py = peers[j]
        return pltpu.make_async_remote_copy(
            src_ref=p_refs[a].at[src_chip], dst_ref=out_refs[a].at[dst_chip], send_sem=send_sems.at[3 * a + j],
            recv_sem=recv_sems.at[3 * a + j], device_id=(px, py, c), device_id_type=MESH_ID)

    def sends():
        return [copy(a, j, 2 * px + py, mine) for a in range(n) for j, (px, py) in enumerate(peers)]

    def start():
        for cp in local() + sends():
            cp.start()

    def finish():
        for a in range(n):
            for j, (px, py) in enumerate(peers):
                copy(a, j, mine, 2 * px + py).wait_recv()
        for cp in sends():
            cp.wait_send()
        for cp in local():
            cp.wait()

    return start, finish


def _add_mine(g, recv, core, tile, dtype):
    _, _, lead, rows, width = g.shape

    def body(c_ref, g_ref, r_ref, o_ref):
        o_ref[...] = (g_ref[...] + r_ref[...]).astype(dtype)

    spec = pl.BlockSpec((None, None, tile, width), lambda k, l, i, c_ref: (k, l, i, 0))
    return pl.pallas_call(
        body, name="add_sibling", out_shape=SDS(recv.shape, dtype),
        grid_spec=pltpu.PrefetchScalarGridSpec(
            num_scalar_prefetch=1, grid=(g.shape[0], lead, rows // tile),
            in_specs=[pl.BlockSpec((None, None, None, tile, width), lambda k, l, i, c_ref: (k, c_ref[0], l, i, 0)),
                      spec],
            out_specs=spec),
        compiler_params=_cp("parallel", "parallel", "parallel"))(core, g, recv)


def _sum_adamw(parts, w, m, v, tile):
    lead, rows, width = w.shape
    last = rows // tile - 1

    def body(*refs):
        p_refs, (w_ref, m_ref, v_ref, g_ref, d_ref, nm_ref, nv_ref) = refs[:lead], refs[lead:]
        for layer in range(lead):
            @pl.when(pl.program_id(0) == layer)
            def _(p_ref=p_refs[layer]):
                g = ((p_ref[0].astype(F32) + p_ref[1].astype(F32)) + p_ref[2].astype(F32)) + p_ref[3].astype(F32)
                nm = ADAM_B1 * m_ref[...] + (1.0 - ADAM_B1) * g
                nv = ADAM_B2 * v_ref[...] + (1.0 - ADAM_B2) * jnp.square(g)
                m_hat = nm / (1.0 - ADAM_B1 ** ADAM_STEP)
                v_hat = nv / (1.0 - ADAM_B2 ** ADAM_STEP)
                g_ref[...] = g
                nm_ref[...] = nm
                nv_ref[...] = nv
                d_ref[...] = -ADAM_LR * (m_hat / (jnp.sqrt(v_hat) + ADAM_EPS) + ADAM_WD * w_ref[...])

    pspec = lambda layer: pl.BlockSpec(
        (4, None, tile, width),
        lambda l, i: (0, 0, jnp.where(l == layer, i, jnp.where(l > layer, last, 0)), 0))
    spec = pl.BlockSpec((None, tile, width), lambda l, i: (l, i, 0))
    return pl.pallas_call(
        body, grid=(lead, rows // tile), name="sum_adamw",
        in_specs=[pspec(layer) for layer in range(lead)] + [spec, spec, spec],
        out_specs=[spec] * 4, out_shape=[SDS((lead, rows, width), F32)] * 4,
        compiler_params=_cp("arbitrary", "arbitrary"))(*parts, w, m, v)


def kernel(x, p, positions, g_mix, w_in, sink, g_q, w_uq, g_kv, w_ukv, w_br_a, w_br_b, w_out, g_ple, w_ple_gate, w_ple_proj, g_final, loss_target, m_g_mix, m_w_in, m_sink, m_g_q, m_w_uq, m_g_kv, m_w_ukv, m_w_br_a, m_w_br_b, m_w_out, m_g_ple, m_w_ple_gate, m_w_ple_proj, m_g_final, v_g_mix, v_w_in, v_sink, v_g_q, v_w_uq, v_g_kv, v_w_ukv, v_w_br_a, v_w_br_b, v_w_out, v_g_ple, v_w_ple_gate, v_w_ple_proj, v_g_final):
    weights = dict(g_mix=g_mix, w_in=w_in, sink=sink, g_q=g_q, w_uq=w_uq, g_kv=g_kv, w_ukv=w_ukv, w_br_a=w_br_a,
                   w_br_b=w_br_b, w_out=w_out, g_ple=g_ple, w_ple_gate=w_ple_gate, w_ple_proj=w_ple_proj,
                   g_final=g_final)
    mom1 = dict(g_mix=m_g_mix, w_in=m_w_in, sink=m_sink, g_q=m_g_q, w_uq=m_w_uq, g_kv=m_g_kv, w_ukv=m_w_ukv,
                w_br_a=m_w_br_a, w_br_b=m_w_br_b, w_out=m_w_out, g_ple=m_g_ple, w_ple_gate=m_w_ple_gate,
                w_ple_proj=m_w_ple_proj, g_final=m_g_final)
    mom2 = dict(g_mix=v_g_mix, w_in=v_w_in, sink=v_sink, g_q=v_g_q, w_uq=v_w_uq, g_kv=v_g_kv, w_ukv=v_w_ukv,
                w_br_a=v_w_br_a, w_br_b=v_w_br_b, w_out=v_w_out, g_ple=v_g_ple, w_ple_gate=v_w_ple_gate,
                w_ple_proj=v_w_ple_proj, g_final=v_g_final)
    assert DEPTH == 2
    wide = lambda d: d["w_in"]
    rows = lambda d: _pack_rows(d)
    core = lax.axis_index("c").astype(jnp.int32).reshape(1)

    w16 = [wide(weights).astype(BF16), rows(weights).astype(BF16)]
    wts0 = dict(w_in=_win_layout(_all_gather([w16[0][:1]])[0], 256))
    small = _small_params(g_mix, sink, g_q, g_kv, g_ple, g_final)

    def wide_payload(g):
        return _win_grad_layout(g["w_in"], 256).reshape(N_DEV // 2, 2, 1, D, WIDE_W)

    def rows_payload(g):
        return _pack_rows({n: _split(n, g[n][None]) for n, _ in ROWS_PIECES}).reshape(N_DEV // 2, 2, 1, ROWS_N, LANES)

    def add(pay, got):
        tiles = {D: (WIDE_TILE, BF16), ROWS_N: (ROWS_TILE, BF16), VEC_ROWS: (VEC_ROWS, F32)}
        return [_add_mine(a, b, core, *tiles[a.shape[-2]]) for a, b in zip(pay, got)]

    plan = dict(rows0=[w16[1][:1]], blocks1=[a[1:] for a in w16], w_in=lambda blocks: _win_layout(blocks, 256),
                row_weights=_row_weights, payload=lambda g: [wide_payload(g), rows_payload(g)],
                rows_payload=rows_payload, add=add)
    loss, grad_x, grads, dg_final, rode = _local_step(x, p, positions, loss_target, small, [wts0], plan)

    vectors = {n: jnp.stack([grads[i][n] for i in range(DEPTH)]) for n, _ in SMALL[:-1]}
    vectors["g_final"] = dg_final
    pay = [wide_payload(grads[0]),
           jnp.broadcast_to(_pack_vec(vectors, loss[0, 0]), (N_DEV // 2, 2, 1, VEC_ROWS, LANES))]
    parts_wide0, parts_vec = _exchange_chips(add(pay, _swap_sibling(pay)))
    out_wide = _sum_adamw([parts_wide0, rode["layer1"][0]], wide(weights), wide(mom1), wide(mom2), WIDE_TILE)
    out_rows = _sum_adamw([rode["rows0"][0], rode["layer1"][1]], rows(weights), rows(mom1), rows(mom2), ROWS_TILE)
    out_vec = _sum_adamw([parts_vec], _pack_vec(weights), _pack_vec(mom1), _pack_vec(mom2), VEC_ROWS)

    outs = []
    for ow, orow, ovec in zip(out_wide, out_rows, out_vec):
        named = _unpack_rows(orow)
        named.update(_unpack_vec(ovec)[0])
        named["w_in"] = ow
        outs += [named[n] for n in weights]
    loss = _unpack_vec(out_vec[0])[1]
    return (loss, grad_x, *outs)
```

```python
import functools

import jax
import jax.numpy as jnp
from jax import lax
from jax.experimental import pallas as pl
from jax.experimental.pallas import tpu as pltpu

F32, BF16 = jnp.float32, jnp.bfloat16
SDS = jax.ShapeDtypeStruct

D = 1024
DEPTH = 2
PLE = 256
BLK = 128
EPS = 1e-6
NEG = -1e30
SWA_H, SWA_KV, SWA_DH = 8, 2, 64
MLA_H, MLA_NOPE, MLA_ROPE, MLA_V = 8, 64, 32, 64
MLA_QK = MLA_NOPE + MLA_ROPE
QL, KVL = 256, 128
IN_W = 4256
N_DEV = 8

V7X_VMEM_BYTES = 64 * 1024 * 1024
LANES = 128
VMEM_LIMIT = V7X_VMEM_BYTES * 7 // 8

ZW = 4352
Z_MA, Z_MB, Z_AQ, Z_AG, Z_BG, Z_QD, Z_AK, Z_AV, Z_KVD, Z_KR = 0, 1024, 2048, 2560, 3072, 3584, 3840, 3968, 4096, 4224
QFW = MLA_H * LANES
KVW = QFW + MLA_H * MLA_V
MLA_SCALE = MLA_QK ** -0.5
LOG2E = 1.4426950408889634
MLA_FWD_HEADS, MLA_BWD_HEADS = 8, 4
SWA_SCALE = SWA_DH ** -0.5
ROLL_UP, ROLL_DOWN = MLA_ROPE // 2, LANES - MLA_ROPE // 2

ADAM_LR, ADAM_B1, ADAM_B2, ADAM_EPS, ADAM_WD, ADAM_STEP = 0.001, 0.9, 0.999, 1e-08, 0.01, 10


def _cp(*sem):
    return pltpu.CompilerParams(dimension_semantics=sem, vmem_limit_bytes=VMEM_LIMIT)


def _row(tm, w, col=0):
    return pl.BlockSpec((tm, w), lambda i: (i, col))


def _res(shape, layer=None):
    if layer is None:
        return pl.BlockSpec(shape, lambda *_: (0,) * len(shape), pipeline_mode=pl.Buffered(1))
    return pl.BlockSpec((None,) + shape, lambda *_: (layer,) + (0,) * len(shape), pipeline_mode=pl.Buffered(1))


def _acc(shape):
    return pl.BlockSpec(shape, lambda *_: (0,) * len(shape))


def _rstd(xf):
    return lax.rsqrt(jnp.mean(xf * xf, axis=-1, keepdims=True) + EPS)


def _norm_bwd(dh, n, r, g):
    dn = dh * g
    return r * (dn - n * jnp.mean(dn * n, axis=-1, keepdims=True)), dh * n


def _nt(a, b):
    return lax.dot_general(a, b, (((1,), (1,)), ((), ())), preferred_element_type=F32)


def _tn(a, b):
    return lax.dot_general(a, b, (((0,), (0,)), ((), ())), preferred_element_type=F32)


def _nn(a, b):
    return jnp.dot(a, b, preferred_element_type=F32)


def _sig(x):
    return jax.nn.sigmoid(x)


def _rope(t, c, s1, s2):
    return t * c + pltpu.roll(t, ROLL_UP, 1) * s1 + pltpu.roll(t, ROLL_DOWN, 1) * s2


def _rope_t(d, c, s1, s2):
    return d * c + pltpu.roll(d * s1, ROLL_DOWN, 1) + pltpu.roll(d * s2, ROLL_UP, 1)


def _fwd_in(x, g, w, tm, layer, ride=None):
    T = x.shape[0]
    grid = (T // tm,)

    def body(x_ref, g_ref, w_ref, z_ref, h_ref):
        xf = x_ref[...]
        h = ((xf * _rstd(xf)) * g_ref[...]).astype(BF16)
        h_ref[...] = h
        z_ref[...] = _nn(h, w_ref[...])

    r_in, r_out, r_shape, r_scratch, r_args = _ride_args(ride)
    return pl.pallas_call(
        _riding(ride, body, 3, 2, grid), grid=grid, name="fwd_in_ride" if ride else "fwd_in",
        in_specs=[_row(tm, D), _res((1, D), layer), _res((D, ZW), 0)] + r_in,
        out_specs=[_row(tm, ZW), _row(tm, D)] + r_out,
        out_shape=[SDS((T, ZW), F32), SDS((T, D), BF16)] + r_shape, scratch_shapes=r_scratch,
        compiler_params=_cp("arbitrary"))(x, g, w, *r_args)


def _fwd_prep(z, gq, gkv, wq, wkv, tc, ts1, ts2, tm, layer):
    T = z.shape[0]

    def body(qd_ref, kvd_ref, kr_ref, gq_ref, gkv_ref, wq_ref, wkv_ref, c_ref, s1_ref, s2_ref, q_ref, k_ref, v_ref):
        qd, kvd = qd_ref[...], kvd_ref[...]
        hq = ((qd * _rstd(qd)) * gq_ref[...]).astype(BF16)
        hkv = ((kvd * _rstd(kvd)) * gkv_ref[...]).astype(BF16)
        qf = _nn(hq, wq_ref[...])
        kvf = _nn(hkv, wkv_ref[...])
        c, s1, s2 = c_ref[...], s1_ref[...], s2_ref[...]
        krb = _rope(kr_ref[...], c, s1, s2)
        for h in range(MLA_H):
            sl = slice(LANES * h, LANES * (h + 1))
            q_ref[:, sl] = _rope(qf[:, sl], c, s1, s2).astype(BF16)
            k_ref[:, sl] = (kvf[:, sl] + krb).astype(BF16)
        v_ref[...] = kvf[:, QFW:].astype(BF16)

    return pl.pallas_call(
        body, grid=(T // tm,), name="fwd_prep",
        in_specs=[_row(tm, QL, Z_QD // QL), _row(tm, KVL, Z_KVD // KVL), _row(tm, LANES, Z_KR // LANES),
                  _res((1, QL), layer), _res((1, KVL), layer), _res((QL, QFW), 0), _res((KVL, KVW), 0),
                  _row(tm, LANES), _row(tm, LANES), _row(tm, LANES)],
        out_specs=[_row(tm, QFW), _row(tm, QFW), _row(tm, MLA_H * MLA_V)],
        out_shape=[SDS((T, QFW), BF16), SDS((T, QFW), BF16), SDS((T, MLA_H * MLA_V), BF16)],
        compiler_params=_cp("parallel"))(z, z, z, gq, gkv, wq, wkv, tc, ts1, ts2)


def _grid_ends(grid):
    ids = [pl.program_id(a) for a in range(len(grid))]
    inner_first = functools.reduce(jnp.logical_and, [i == 0 for i in ids[1:]], True)
    last = functools.reduce(jnp.logical_and, [i == g - 1 for i, g in zip(ids, grid)])
    return (ids[0] == 0) & inner_first, (ids[0] == grid[0] // 2) & inner_first, last


class _Ride:
    def __init__(self, kind, arrays):
        self.kind, self.arrays, self.n = kind, list(arrays), len(arrays)

    def out_shape(self):
        if self.kind == "gather":
            return _gather_out(self.arrays)
        if self.kind == "swap":
            return [SDS((a.shape[0],) + a.shape[2:], a.dtype) for a in self.arrays]
        return [SDS(a.shape, a.dtype) for a in self.arrays]

    def sems(self):
        if self.kind == "gather":
            return _gather_sems(self.n)
        if self.kind == "swap":
            return _swap_sems(self.n)
        return _exchange_sems(self.n)

    def phases(self, in_refs, out_refs, *sems):
        if self.kind == "gather":
            return _gather_phases(in_refs, out_refs, *sems)
        start, finish = (_swap_phases if self.kind == "swap" else _exchange_phases)(in_refs, out_refs, *sems)
        return start, None, finish


def _riding(ride, body, n_in, n_out, grid):
    if ride is None:
        return body
    n, n_sems = ride.n, len(ride.sems())

    def wrapped(*refs):
        ins, r_in = refs[:n_in], refs[n_in:n_in + n]
        outs, r_out = refs[n_in + n:n_in + n + n_out], refs[n_in + n + n_out:n_in + 2 * n + n_out]
        rest = refs[n_in + 2 * n + n_out:]
        scratch, sems = rest[:len(rest) - n_sems], rest[len(rest) - n_sems:]
        start, middle, finish = ride.phases(r_in, r_out, *sems)
        at_first, at_middle, at_last = _grid_ends(grid)
        pl.when(at_first)(start)
        if middle is not None:
            pl.when(at_middle)(middle)
        body(*ins, *outs, *scratch)
        pl.when(at_last)(finish)

    return wrapped


def _ride_args(ride):
    if ride is None:
        return [], [], [], [], []
    return [ANY] * ride.n, [ANY] * ride.n, ride.out_shape(), ride.sems(), ride.arrays


def _mla_fwd(qf, kf, v, nb, seq, tq, gather=()):
    T = qf.shape[0]
    nq = seq // tq
    hp = MLA_FWD_HEADS
    pw = hp * LANES
    pairs = [(qi, ki) for qi in range(nq) for ki in range(qi + 1)]
    qi_tab = jnp.array([qk[0] for qk in pairs], jnp.int32)
    ki_tab = jnp.array([qk[1] for qk in pairs], jnp.int32)
    grid = (nb, MLA_H // hp, len(pairs))
    n_g = len(gather)

    def body(qi_ref, ki_ref, q_ref, k_ref, v_ref, *rest):
        x_refs, (o_ref, lse_ref), got_refs = rest[:n_g], rest[n_g:n_g + 2], rest[n_g + 2:2 * n_g + 2]
        (m_s, l_s, acc_s), sems = rest[2 * n_g + 2:2 * n_g + 5], rest[2 * n_g + 5:]
        qi, ki = qi_ref[pl.program_id(2)], ki_ref[pl.program_id(2)]
        if n_g:
            start, forward, finish = _gather_phases(x_refs, got_refs, *sems)
            at_first, at_middle, at_last = _grid_ends(grid)
            pl.when(at_first)(start)
            pl.when(at_middle)(forward)

        @pl.when(ki == 0)
        def _():
            m_s[...] = jnp.full(m_s.shape, NEG, F32)
            l_s[...] = jnp.zeros(l_s.shape, F32)
            acc_s[...] = jnp.zeros(acc_s.shape, F32)

        def step(masked):
            parts = [(0, tq // 2, tq // 2), (tq // 2, tq, tq)] if masked else [(0, tq, tq)]
            work = [(j, a, b, kh) for j in range(hp) for a, b, kh in parts]
            ss = []
            for j, a, b, kh in work:
                wide = slice(LANES * j, LANES * (j + 1))
                s = _nt(k_ref[:kh, wide], q_ref[a:b, wide]) * (MLA_SCALE * LOG2E)
                if masked:
                    keys = lax.broadcasted_iota(jnp.int32, (kh, b - a), 0)
                    queries = a + lax.broadcasted_iota(jnp.int32, (kh, b - a), 1)
                    s = jnp.where(keys <= queries, s, NEG)
                ss.append(s)
            ps, alphas = [], []
            for (j, a, b, kh), s in zip(work, ss):
                m_prev = m_s[j, :, a:b]
                m_new = jnp.maximum(m_prev, jnp.max(s, axis=0, keepdims=True))
                alpha = jnp.exp2(m_prev - m_new)
                p = jnp.exp2(s - m_new)
                l_s[j, :, a:b] = alpha * l_s[j, :, a:b] + jnp.sum(p, axis=0, keepdims=True)
                m_s[j, :, a:b] = m_new
                ps.append(p.astype(BF16))
                alphas.append(alpha)
            for (j, a, b, kh), p, alpha in zip(work, ps, alphas):
                rows = slice(MLA_V * j, MLA_V * (j + 1))
                acc_s[rows, a:b] = alpha * acc_s[rows, a:b] + _tn(v_ref[:kh, rows], p)

        @pl.when(ki < qi)
        def _():
            step(False)

        @pl.when(ki == qi)
        def _():
            step(True)
            for j in range(hp):
                rows = slice(MLA_V * j, MLA_V * (j + 1))
                acc_s[rows, :] = acc_s[rows, :] / l_s[j]
                lse_ref[j:j + 1, :] = m_s[j] + jnp.log2(l_s[j])
            o_ref[...] = acc_s[...].T

        if n_g:
            pl.when(at_last)(finish)

    q_map = lambda b, g, s, qi_ref, ki_ref: (b * nq + qi_ref[s], g)
    kv_map = lambda b, g, s, qi_ref, ki_ref: (b * nq + ki_ref[s], g)
    return pl.pallas_call(
        body, name="mla_fwd_gather" if n_g else "mla_fwd",
        grid_spec=pltpu.PrefetchScalarGridSpec(
            num_scalar_prefetch=2, grid=grid,
            in_specs=[pl.BlockSpec((tq, pw), q_map), pl.BlockSpec((tq, pw), kv_map),
                      pl.BlockSpec((tq, hp * MLA_V), kv_map)] + [ANY] * n_g,
            out_specs=[pl.BlockSpec((tq, hp * MLA_V), q_map),
                       pl.BlockSpec((hp, tq), lambda b, g, s, qi_ref, ki_ref: (g, b * nq + qi_ref[s]))]
            + [ANY] * n_g,
            scratch_shapes=[pltpu.VMEM((hp, 1, tq), F32), pltpu.VMEM((hp, 1, tq), F32),
                            pltpu.VMEM((hp * MLA_V, tq), F32)]
            + (_gather_sems(n_g) if n_g else [])),
        out_shape=[SDS((T, MLA_H * MLA_V), F32), SDS((MLA_H, T), F32)] + _gather_out(gather),
        compiler_params=_cp("arbitrary", "arbitrary", "arbitrary"))(qi_tab, ki_tab, qf, kf, v, *gather)


def _swa_specs(nblk):
    cur = lambda b, n: (b * nblk + n, 0)
    prev = lambda b, n: (b * nblk + jnp.maximum(n - 1, 0), 0)
    kvc = Z_AK // (2 * BLK)
    return [pl.BlockSpec(memory_space=pltpu.SMEM),
            pl.BlockSpec((BLK, 512), lambda b, n: (b * nblk + n, Z_AQ // 512)),
            pl.BlockSpec((BLK, 2 * BLK), lambda b, n: (b * nblk + n, kvc)),
            pl.BlockSpec((BLK, 2 * BLK), lambda b, n: (b * nblk + jnp.maximum(n - 1, 0), kvc)),
            pl.BlockSpec((BLK, 1), cur),
            pl.BlockSpec((BLK, 1), prev),
            pl.BlockSpec((1, 1, BLK), lambda b, n: (b * nblk + n, 0, 0))]


def _swa_scores(n, q_ref, kvc_ref, kvp_ref, pcc_ref, pcp_ref, pr_ref):
    kv = jnp.concatenate([kvp_ref[...], kvc_ref[...]], axis=0)
    kb, vb = kv[:, :BLK].astype(BF16), kv[:, BLK:].astype(BF16)
    dist = pr_ref[0] - jnp.concatenate([pcp_ref[...], pcc_ref[...]], axis=0)
    key = lax.broadcasted_iota(jnp.int32, (2 * BLK, BLK), 0)
    qry = lax.broadcasted_iota(jnp.int32, (2 * BLK, BLK), 1)
    valid = (key > qry) & (key <= qry + BLK) & ((key >= BLK) | (n > 0))

    def scores(h):
        g = h // (SWA_H // SWA_KV)
        qh = q_ref[:, SWA_DH * h:SWA_DH * (h + 1)].astype(BF16)
        s = _nt(kb[:, SWA_DH * g:SWA_DH * (g + 1)], qh) * (SWA_SCALE * LOG2E) - (2.0 ** -(h + 1) * LOG2E) * dist
        return qh, jnp.where(valid, s, NEG)

    return kb, vb, scores


def _swa_fwd(sink, z, pos_col, pos_row, nb, seq, layer):
    T = z.shape[0]
    nblk = seq // BLK

    def body(sink_ref, q_ref, kvc_ref, kvp_ref, pcc_ref, pcp_ref, pr_ref, o_ref, lse_ref):
        kb, vb, scores = _swa_scores(pl.program_id(1), q_ref, kvc_ref, kvp_ref, pcc_ref, pcp_ref, pr_ref)
        ss = [scores(h)[1] for h in range(SWA_H)]
        es, dens = [], []
        for h in range(SWA_H):
            sk = sink_ref[layer, h] * LOG2E
            m = jnp.maximum(jnp.max(ss[h], axis=0, keepdims=True), sk)
            e = jnp.exp2(ss[h] - m)
            den = jnp.sum(e, axis=0, keepdims=True) + jnp.exp2(sk - m)
            lse_ref[h:h + 1, :] = m + jnp.log2(den)
            es.append(e.astype(BF16))
            dens.append(den)
        outs = []
        for h in range(SWA_H):
            g = h // (SWA_H // SWA_KV)
            outs.append(_tn(vb[:, SWA_DH * g:SWA_DH * (g + 1)], es[h]) / dens[h])
        o_ref[...] = jnp.concatenate(outs, axis=0).T

    return pl.pallas_call(
        body, grid=(nb, nblk), name="swa_fwd",
        in_specs=_swa_specs(nblk),
        out_specs=[pl.BlockSpec((BLK, 512), lambda b, n: (b * nblk + n, 0)),
                   pl.BlockSpec((SWA_H, BLK), lambda b, n: (0, b * nblk + n))],
        out_shape=[SDS((T, 512), F32), SDS((SWA_H, T), F32)],
        compiler_params=_cp("parallel", "parallel"))(sink, z, z, z, pos_col, pos_col, pos_row)


def _fwd_merge(x, oa, ob, z, wa, wb, wo, tm, layer):
    T = x.shape[0]

    def body(x_ref, oa_ref, ob_ref, ag_ref, bg_ref, ma_ref, mb_ref, wa_ref, wb_ref, wo_ref, x1_ref):
        ag, bg = ag_ref[...], bg_ref[...]
        ua = _nn((oa_ref[...] * (ag * _sig(ag))).astype(BF16), wa_ref[...])
        ub = _nn((ob_ref[...] * (bg * _sig(bg))).astype(BF16), wb_ref[...])
        y = _sig(ma_ref[...]) * ua + _sig(mb_ref[...]) * ub
        x1_ref[...] = x_ref[...] + _nn(y.astype(BF16), wo_ref[...])

    return pl.pallas_call(
        body, grid=(T // tm,), name="fwd_merge",
        in_specs=[_row(tm, D), _row(tm, 512), _row(tm, 512), _row(tm, 512, Z_AG // 512), _row(tm, 512, Z_BG // 512),
                  _row(tm, D, Z_MA // D), _row(tm, D, Z_MB // D),
                  _res((512, D), 0), _res((512, D), 0), _res((D, D), 0)],
        out_specs=_row(tm, D),
        out_shape=SDS((T, D), F32),
        compiler_params=_cp("parallel"))(x, oa, ob, z, z, z, z, wa, wb, wo)


def _fwd_ple(x1, p, g, wpg, wpp, tm, layer):
    T = x1.shape[0]

    def body(x_ref, p_ref, g_ref, wpg_ref, wpp_ref, x2_ref, pg_ref, pp_ref):
        xf = x_ref[...]
        h1 = ((xf * _rstd(xf)) * g_ref[...]).astype(BF16)
        pg = _sig(_nn(h1, wpg_ref[...]))
        pp = _nn(p_ref[...].astype(BF16), wpp_ref[...])
        pg_ref[...] = pg
        pp_ref[...] = pp
        x2_ref[...] = xf + pg * pp

    return pl.pallas_call(
        body, grid=(T // tm,), name="fwd_ple",
        in_specs=[_row(tm, D), pl.BlockSpec((None, tm, PLE), lambda i: (layer, i, 0)),
                  _res((1, D), layer), _res((D, D), 0), _res((PLE, D), 0)],
        out_specs=[_row(tm, D)] * 3,
        out_shape=[SDS((T, D), F32)] * 3,
        compiler_params=_cp("parallel"))(x1, p, g, wpg, wpp)


def _ple_loss(x1, p, g, wpg, wpp, g_final, tgt, tm, layer):
    T = x1.shape[0]

    def body(x_ref, p_ref, g_ref, wpg_ref, wpp_ref, gf_ref, t_ref, dx_ref, dwg_ref, dwp_ref, dg_ref, dgf_ref, loss_ref):
        @pl.when(pl.program_id(0) == 0)
        def _():
            for ref in (dwg_ref, dwp_ref, dg_ref, dgf_ref, loss_ref):
                ref[...] = jnp.zeros(ref.shape, F32)

        xf, gp, gf = x_ref[...], g_ref[...], gf_ref[...]
        r = _rstd(xf)
        n = xf * r
        h1 = (n * gp).astype(BF16)
        pb = p_ref[...].astype(BF16)
        pg = _sig(_nn(h1, wpg_ref[...]))
        pp = _nn(pb, wpp_ref[...])
        x2 = xf + pg * pp
        r2 = _rstd(x2)
        n2 = x2 * r2
        err = n2 * gf - t_ref[...]
        loss_ref[...] += 0.5 * jnp.sum(jnp.mean(err * err, axis=-1, keepdims=True), axis=0, keepdims=True)
        d, dgfr = _norm_bwd(err * (1.0 / D), n2, r2, gf)
        dgf_ref[...] += jnp.sum(dgfr, axis=0, keepdims=True)
        dpgl = (d * pp * pg * (1.0 - pg)).astype(BF16)
        dwg_ref[...] += _tn(h1, dpgl)
        dwp_ref[...] += _tn(pb, (d * pg).astype(BF16))
        dxn, dgr = _norm_bwd(_nt(dpgl, wpg_ref[...]), n, r, gp)
        dx_ref[...] = d + dxn
        dg_ref[...] += jnp.sum(dgr, axis=0, keepdims=True)

    return pl.pallas_call(
        body, grid=(T // tm,), name="ple_loss",
        in_specs=[_row(tm, D), pl.BlockSpec((None, tm, PLE), lambda i: (layer, i, 0)), _res((1, D), layer),
                  _res((D, D), 0), _res((PLE, D), 0), _res((1, D)), _row(tm, D)],
        out_specs=[_row(tm, D), _acc((D, D)), _acc((PLE, D)), _acc((1, D)), _acc((1, D)), _acc((1, LANES))],
        out_shape=[SDS((T, D), F32), SDS((D, D), F32), SDS((PLE, D), F32), SDS((1, D), F32), SDS((1, D), F32),
                   SDS((1, LANES), F32)],
        compiler_params=_cp("arbitrary"))(x1, p, g, wpg, wpp, g_final, tgt)


def _bwd_ple(dx2, x1, pg, pp, p, g, wpg, tm, layer, ride=None):
    T = x1.shape[0]
    grid = (T // tm,)

    def body(d_ref, x_ref, pg_ref, pp_ref, p_ref, g_ref, w_ref, dx_ref, dwg_ref, dwp_ref, dg_ref):
        @pl.when(pl.program_id(0) == 0)
        def _():
            dwg_ref[...] = jnp.zeros(dwg_ref.shape, F32)
            dwp_ref[...] = jnp.zeros(dwp_ref.shape, F32)
            dg_ref[...] = jnp.zeros(dg_ref.shape, F32)

        d, xf, pg, gf = d_ref[...], x_ref[...], pg_ref[...], g_ref[...]
        r = _rstd(xf)
        n = xf * r
        dpgl = (d * pp_ref[...] * pg * (1.0 - pg)).astype(BF16)
        dwg_ref[...] += _tn((n * gf).astype(BF16), dpgl)
        dwp_ref[...] += _tn(p_ref[...].astype(BF16), (d * pg).astype(BF16))
        dxn, dgr = _norm_bwd(_nt(dpgl, w_ref[...]), n, r, gf)
        dx_ref[...] = d + dxn
        dg_ref[...] += jnp.sum(dgr, axis=0, keepdims=True)

    r_in, r_out, r_shape, r_scratch, r_args = _ride_args(ride)
    return pl.pallas_call(
        _riding(ride, body, 7, 4, grid), grid=grid, name="bwd_ple_ride" if ride else "bwd_ple",
        in_specs=[_row(tm, D)] * 4 + [pl.BlockSpec((None, tm, PLE), lambda i: (layer, i, 0)),
                                      _res((1, D), layer), _res((D, D), 0)] + r_in,
        out_specs=[_row(tm, D), _acc((D, D)), _acc((PLE, D)), _acc((1, D))] + r_out,
        out_shape=[SDS((T, D), F32), SDS((D, D), F32), SDS((PLE, D), F32), SDS((1, D), F32)] + r_shape,
        scratch_shapes=r_scratch,
        compiler_params=_cp("arbitrary"))(dx2, x1, pg, pp, p, g, wpg, *r_args)


def _bwd_merge(dx1, oa, ob, z, wa, wb, wo, tm, layer):
    T = dx1.shape[0]

    def body(d_ref, oa_ref, ob_ref, ag_ref, bg_ref, ma_ref, mb_ref, wa_ref, wb_ref, wo_ref,
             doa_ref, dob_ref, dag_ref, dbg_ref, dma_ref, dmb_ref, dsa_ref, dsb_ref, dwa_ref, dwb_ref, dwo_ref):
        @pl.when(pl.program_id(0) == 0)
        def _():
            dwa_ref[...] = jnp.zeros(dwa_ref.shape, F32)
            dwb_ref[...] = jnp.zeros(dwb_ref.shape, F32)
            dwo_ref[...] = jnp.zeros(dwo_ref.shape, F32)

        db = d_ref[...].astype(BF16)
        gated = []
        for o_ref, gate_ref, w_ref in ((oa_ref, ag_ref, wa_ref), (ob_ref, bg_ref, wb_ref)):
            raw, gate = o_ref[...], gate_ref[...]
            sg = _sig(gate)
            silu = gate * sg
            ob16 = (raw * silu).astype(BF16)
            gated.append((raw, gate, sg, silu, ob16, _nn(ob16, w_ref[...])))
        ua, ub = gated[0][5], gated[1][5]
        sa, sb = _sig(ma_ref[...]), _sig(mb_ref[...])
        dwo_ref[...] += _tn((sa * ua + sb * ub).astype(BF16), db)
        dy = _nt(db, wo_ref[...])
        dma_ref[...] = (dy * ua * sa * (1.0 - sa)).astype(BF16)
        dmb_ref[...] = (dy * ub * sb * (1.0 - sb)).astype(BF16)
        for (s, w_ref, do_ref, dgate_ref, dw_ref, ds_ref), (raw, gate, sg, silu, ob16, _) in zip((
                (sa, wa_ref, doa_ref, dag_ref, dwa_ref, dsa_ref),
                (sb, wb_ref, dob_ref, dbg_ref, dwb_ref, dsb_ref)), gated):
            du = (dy * s).astype(BF16)
            dw_ref[...] += _tn(ob16, du)
            do = _nt(du, w_ref[...])
            draw = do * silu
            do_ref[...] = draw.astype(BF16)
            dgate_ref[...] = (do * raw * (sg * (1.0 + gate * (1.0 - sg)))).astype(BF16)
            ds_ref[...] = jnp.sum((draw * raw).T.reshape(MLA_H, MLA_V, tm), axis=1)

    return pl.pallas_call(
        body, grid=(T // tm,), name="bwd_merge",
        in_specs=[_row(tm, D), _row(tm, 512), _row(tm, 512), _row(tm, 512, Z_AG // 512), _row(tm, 512, Z_BG // 512),
                  _row(tm, D, Z_MA // D), _row(tm, D, Z_MB // D),
                  _res((512, D), 0), _res((512, D), 0), _res((D, D), 0)],
        out_specs=[_row(tm, 512)] * 4 + [_row(tm, D)] * 2 + [pl.BlockSpec((MLA_H, tm), lambda i: (0, i))] * 2
        + [_acc((512, D)), _acc((512, D)), _acc((D, D))],
        out_shape=[SDS((T, 512), BF16), SDS((T, 512), BF16), SDS((T, 512), BF16), SDS((T, 512), BF16),
                   SDS((T, D), BF16), SDS((T, D), BF16), SDS((MLA_H, T), F32), SDS((MLA_H, T), F32),
                   SDS((512, D), F32), SDS((512, D), F32), SDS((D, D), F32)],
        compiler_params=_cp("arbitrary"))(dx1, oa, ob, z, z, z, z, wa, wb, wo)


def _mla_bwd(qf, kf, v, do, lse, dsum, nb, seq, tq, exchange=()):
    T = qf.shape[0]
    nq = seq // tq
    hp = MLA_BWD_HEADS
    pw = hp * LANES
    pairs = [(qi, ki) for ki in range(nq) for qi in range(ki, nq)]
    qi_tab = jnp.array([qk[0] for qk in pairs], jnp.int32)
    ki_tab = jnp.array([qk[1] for qk in pairs], jnp.int32)
    grid = (nb, MLA_H // hp, len(pairs))
    n_x = len(exchange)

    def body(qi_ref, ki_ref, q_ref, k_ref, v_ref, do_ref, lse_ref, dsum_ref, *rest):
        p_refs, (dq_ref, dk_ref, dv_ref), got_refs = rest[:n_x], rest[n_x:n_x + 3], rest[n_x + 3:2 * n_x + 3]
        (dk_s, dv_s, dqt_s), sems = rest[2 * n_x + 3:2 * n_x + 6], rest[2 * n_x + 6:]
        step_id = pl.program_id(2)
        qi, ki = qi_ref[step_id], ki_ref[step_id]
        if n_x:
            start, finish = _exchange_phases(p_refs, got_refs, *sems)
            at_first, _, at_last = _grid_ends(grid)
            pl.when(at_first)(start)

        @pl.when(step_id == 0)
        def _():
            dqt_s[...] = jnp.zeros(dqt_s.shape, F32)

        @pl.when(qi == ki)
        def _():
            dk_s[...] = jnp.zeros(dk_s.shape, F32)
            dv_s[...] = jnp.zeros(dv_s.shape, F32)

        def step(masked):
            if masked:
                keys = lax.broadcasted_iota(jnp.int32, (tq, tq), 0)
                queries = lax.broadcasted_iota(jnp.int32, (tq, tq), 1)
                mask = keys <= queries
            for j in range(hp):
                wide = slice(LANES * j, LANES * (j + 1))
                sl = slice(MLA_V * j, MLA_V * (j + 1))
                q, k = q_ref[:, wide], k_ref[:, wide]
                dob = do_ref[:, sl].astype(BF16)
                s = _nt(k, q) * (MLA_SCALE * LOG2E)
                if masked:
                    s = jnp.where(mask, s, NEG)
                p = jnp.exp2(s - lse_ref[j:j + 1, :])
                dv_s[:, sl] += _nn(p.astype(BF16), dob)
                ds = (p * (_nt(v_ref[:, sl], dob) - dsum_ref[j:j + 1, :]) * MLA_SCALE).astype(BF16)
                dk_s[:, wide] += _nn(ds, q)
                dqt_s[qi, wide, :] += _tn(k, ds)

        @pl.when(qi > ki)
        def _():
            step(False)

        @pl.when(qi == ki)
        def _():
            step(True)

        @pl.when(qi == nq - 1)
        def _():
            dk_ref[...] = dk_s[...]
            dv_ref[...] = dv_s[...]

        @pl.when(step_id == len(pairs) - 1)
        def _():
            for n in range(nq):
                dq_ref[tq * n:tq * (n + 1), :] = dqt_s[n].T

        if n_x:
            pl.when(at_last)(finish)

    qmap = lambda b, g, s, qi_ref, ki_ref: (b * nq + qi_ref[s], g)
    kmap = lambda b, g, s, qi_ref, ki_ref: (b * nq + ki_ref[s], g)
    stat = pl.BlockSpec((None, hp, tq), lambda b, g, s, qi_ref, ki_ref: (g, 0, b * nq + qi_ref[s]))
    vw = hp * MLA_V
    return pl.pallas_call(
        body, name="mla_bwd_exchange" if n_x else "mla_bwd",
        grid_spec=pltpu.PrefetchScalarGridSpec(
            num_scalar_prefetch=2, grid=grid,
            in_specs=[pl.BlockSpec((tq, pw), qmap), pl.BlockSpec((tq, pw), kmap), pl.BlockSpec((tq, vw), kmap),
                      pl.BlockSpec((tq, vw), qmap), stat, stat] + [ANY] * n_x,
            out_specs=[pl.BlockSpec((seq, pw), lambda b, g, s, qi_ref, ki_ref: (b, g)),
                       pl.BlockSpec((tq, pw), kmap), pl.BlockSpec((tq, vw), kmap)] + [ANY] * n_x,
            scratch_shapes=[pltpu.VMEM((tq, pw), F32), pltpu.VMEM((tq, vw), F32), pltpu.VMEM((nq, pw, tq), F32)]
            + (_exchange_sems(n_x) if n_x else [])),
        out_shape=[SDS((T, QFW), F32), SDS((T, QFW), F32), SDS((T, MLA_H * MLA_V), F32)]
        + [SDS(a.shape, a.dtype) for a in exchange],
        compiler_params=_cp("arbitrary", "arbitrary", "arbitrary"))(qi_tab, ki_tab, qf, kf, v, do, lse, dsum, *exchange)


def _swa_bwd(sink, z, pos_col, pos_row, do, lse, dsum, nb, seq, layer, ride=None):
    T = z.shape[0]
    nblk = seq // BLK

    def body(sink_ref, q_ref, kvc_ref, kvp_ref, pcc_ref, pcp_ref, pr_ref, do_ref, lse_ref, dsum_ref,
             dq_ref, dkv_ref, dsink_ref):
        b, n = pl.program_id(0), pl.program_id(1)

        @pl.when((b == 0) & (n == 0))
        def _():
            dsink_ref[...] = jnp.zeros(dsink_ref.shape, F32)

        @pl.when(n == 0)
        def _():
            dkv_ref[...] = jnp.zeros(dkv_ref.shape, F32)

        kb, vb, scores = _swa_scores(n, q_ref, kvc_ref, kvp_ref, pcc_ref, pcp_ref, pr_ref)
        lane = lax.broadcasted_iota(jnp.int32, (1, LANES), 1)
        dsink = jnp.zeros((1, LANES), F32)
        dkv = [[None, None], [None, None]]
        dqs = []
        gsl = lambda h: slice(SWA_DH * (h // (SWA_H // SWA_KV)), SWA_DH * (h // (SWA_H // SWA_KV) + 1))
        qs, ss, dobs, dps = [], [], [], []
        for h in range(SWA_H):
            qh, s = scores(h)
            dob = do_ref[:, SWA_DH * h:SWA_DH * (h + 1)].astype(BF16)
            qs.append(qh)
            ss.append(s)
            dobs.append(dob)
            dps.append(_nt(vb[:, gsl(h)], dob))
        pbs, dss = [], []
        for h in range(SWA_H):
            lse, dsum = lse_ref[h:h + 1, :], dsum_ref[h:h + 1, :]
            p = jnp.exp2(ss[h] - lse)
            pbs.append(p.astype(BF16))
            dss.append((p * (dps[h] - dsum) * SWA_SCALE).astype(BF16))
            dsk = jnp.sum(-jnp.exp2(sink_ref[layer, h] * LOG2E - lse) * dsum, axis=1, keepdims=True)
            dsink = dsink + jnp.where(lane == h, dsk, 0.0)
        for h in range(SWA_H):
            g = h // (SWA_H // SWA_KV)
            dqs.append(_tn(kb[:, gsl(h)], dss[h]))
            dk, dv = _nn(dss[h], qs[h]), _nn(pbs[h], dobs[h])
            dkv[g][0] = dk if dkv[g][0] is None else dkv[g][0] + dk
            dkv[g][1] = dv if dkv[g][1] is None else dkv[g][1] + dv
        dq_ref[...] = jnp.concatenate(dqs, axis=0).T.astype(BF16)
        dsink_ref[...] += dsink
        upd = jnp.concatenate([dkv[0][0], dkv[1][0], dkv[0][1], dkv[1][1]], axis=1)
        dkv_ref[pl.ds(pl.multiple_of(n * BLK, BLK), BLK), :] += upd[BLK:]

        @pl.when(n > 0)
        def _():
            dkv_ref[pl.ds(pl.multiple_of((n - 1) * BLK, BLK), BLK), :] += upd[:BLK]

    r_in, r_out, r_shape, r_scratch, r_args = _ride_args(ride)
    return pl.pallas_call(
        _riding(ride, body, 10, 3, (nb, nblk)), grid=(nb, nblk), name="swa_bwd_ride" if ride else "swa_bwd",
        in_specs=_swa_specs(nblk) + [pl.BlockSpec((BLK, 512), lambda b, n: (b * nblk + n, 0))]
        + [pl.BlockSpec((SWA_H, BLK), lambda b, n: (0, b * nblk + n))] * 2 + r_in,
        out_specs=[pl.BlockSpec((BLK, 512), lambda b, n: (b * nblk + n, 0)),
                   pl.BlockSpec((seq, 2 * BLK), lambda b, n: (b, 0)),
                   pl.BlockSpec((1, LANES), lambda b, n: (0, 0))] + r_out,
        out_shape=[SDS((T, 512), BF16), SDS((T, 2 * BLK), F32), SDS((1, LANES), F32)] + r_shape,
        scratch_shapes=r_scratch,
        compiler_params=_cp("arbitrary", "arbitrary"))(sink, z, z, z, pos_col, pos_col, pos_row, do, lse, dsum,
                                                       *r_args)


def _bwd_prep(dq, dk, dv, z, gq, gkv, wq, wkv, tc, ts1, ts2, tm, layer):
    T = z.shape[0]

    def body(dq_ref, dk_ref, dv_ref, qd_ref, kvd_ref, gq_ref, gkv_ref, wq_ref, wkv_ref, c_ref, s1_ref, s2_ref,
             dqd_ref, dkvd_ref, dkr_ref, dwq_ref, dwkv_ref, dgq_ref, dgkv_ref, dqb_s, dkvb_s):
        @pl.when(pl.program_id(0) == 0)
        def _():
            for ref in (dwq_ref, dwkv_ref, dgq_ref, dgkv_ref):
                ref[...] = jnp.zeros(ref.shape, F32)

        c, s1, s2 = c_ref[...], s1_ref[...], s2_ref[...]
        lane = lax.broadcasted_iota(jnp.int32, (1, LANES), 1)
        rope_lanes = (lane >= MLA_NOPE) & (lane < MLA_QK)
        dkb = jnp.zeros((tm, LANES), F32)
        for h in range(MLA_H):
            sl = slice(LANES * h, LANES * (h + 1))
            dqb_s[:, sl] = _rope_t(dq_ref[:, sl], c, s1, s2).astype(BF16)
            dkh = dk_ref[:, sl]
            dkb = dkb + dkh
            dkvb_s[:, sl] = dkh.astype(BF16)
        dkvb_s[:, QFW:] = dv_ref[...].astype(BF16)
        dkr_ref[...] = _rope_t(jnp.where(rope_lanes, dkb, 0.0), c, s1, s2).astype(BF16)

        for (x_ref, g_ref, w_ref, d_s, dx_ref, dw_ref, dg_ref) in (
                (qd_ref, gq_ref, wq_ref, dqb_s, dqd_ref, dwq_ref, dgq_ref),
                (kvd_ref, gkv_ref, wkv_ref, dkvb_s, dkvd_ref, dwkv_ref, dgkv_ref)):
            xf, gf, db = x_ref[...], g_ref[...], d_s[...]
            r = _rstd(xf)
            n = xf * r
            dw_ref[...] += _tn((n * gf).astype(BF16), db)
            dx, dgr = _norm_bwd(_nt(db, w_ref[...]), n, r, gf)
            dx_ref[...] = dx.astype(BF16)
            dg_ref[...] += jnp.sum(dgr, axis=0, keepdims=True)

    return pl.pallas_call(
        body, grid=(T // tm,), name="bwd_prep",
        in_specs=[_row(tm, QFW), _row(tm, QFW), _row(tm, MLA_H * MLA_V),
                  _row(tm, QL, Z_QD // QL), _row(tm, KVL, Z_KVD // KVL),
                  _res((1, QL), layer), _res((1, KVL), layer), _res((QL, QFW), 0), _res((KVL, KVW), 0),
                  _row(tm, LANES), _row(tm, LANES), _row(tm, LANES)],
        out_specs=[_row(tm, QL), _row(tm, KVL), _row(tm, LANES),
                   _acc((QL, QFW)), _acc((KVL, KVW)), _acc((1, QL)), _acc((1, KVL))],
        out_shape=[SDS((T, QL), BF16), SDS((T, KVL), BF16), SDS((T, LANES), BF16),
                   SDS((QL, QFW), F32), SDS((KVL, KVW), F32), SDS((1, QL), F32), SDS((1, KVL), F32)],
        scratch_shapes=[pltpu.VMEM((tm, QFW), BF16), pltpu.VMEM((tm, KVW), BF16)],
        compiler_params=_cp("arbitrary"))(dq, dk, dv, z, z, gq, gkv, wq, wkv, tc, ts1, ts2)


def _bwd_in(pieces, x, g, dres, w, tm, layer):
    T = x.shape[0]
    grid = (T // tm,)
    widths = [pc.shape[1] for pc in pieces]
    assert sum(widths) == ZW
    n_p = len(pieces)

    def body(*refs):
        p_refs, (x_ref, g_ref, r_ref, w_ref, dx_ref, dz_ref, dg_ref) = refs[:n_p], refs[n_p:]

        @pl.when(pl.program_id(0) == 0)
        def _():
            dg_ref[...] = jnp.zeros(dg_ref.shape, F32)

        off = 0
        for ref, wd in zip(p_refs, widths):
            dz_ref[:, off:off + wd] = ref[...].astype(BF16)
            off += wd
        xf, gf = x_ref[...], g_ref[...]
        r = _rstd(xf)
        n = xf * r
        dx, dgr = _norm_bwd(_nt(dz_ref[...], w_ref[...]), n, r, gf)
        dx_ref[...] = r_ref[...] + dx
        dg_ref[...] += jnp.sum(dgr, axis=0, keepdims=True)

    return pl.pallas_call(
        body, grid=grid, name="bwd_in",
        in_specs=[_row(tm, wd) for wd in widths] + [_row(tm, D), _res((1, D), layer), _row(tm, D),
                                                    _res((D, ZW), 0)],
        out_specs=[_row(tm, D), _row(tm, ZW), _acc((1, D))],
        out_shape=[SDS((T, D), F32), SDS((T, ZW), BF16), SDS((1, D), F32)],
        compiler_params=_cp("arbitrary"))(*pieces, x, g, dres, w)


def _wgrad_in(hb, dzb, tm, ride=None):
    T = hb.shape[0]
    half = ZW // 2
    grid = (2, T // tm)

    def body(h_ref, dz_ref, dw_ref):
        @pl.when(pl.program_id(1) == 0)
        def _():
            dw_ref[...] = jnp.zeros(dw_ref.shape, F32)

        dw_ref[...] += _tn(h_ref[...], dz_ref[...])

    r_in, r_out, r_shape, r_scratch, r_args = _ride_args(ride)
    out = pl.pallas_call(
        _riding(ride, body, 2, 1, grid), grid=grid, name="wgrad_in_ride" if ride else "wgrad_in",
        in_specs=[pl.BlockSpec((tm, D), lambda j, t: (t, 0)), pl.BlockSpec((tm, half), lambda j, t: (t, j))] + r_in,
        out_specs=[pl.BlockSpec((D, half), lambda j, t: (0, j))] + r_out,
        out_shape=[SDS((D, ZW), F32)] + r_shape, scratch_shapes=r_scratch,
        compiler_params=_cp("arbitrary", "arbitrary"))(hb, dzb, *r_args)
    return out


IN_PIECES = ((0, 512, Z_AQ), (512, 128, Z_AK), (640, 128, Z_AV), (768, 512, Z_AG), (1280, 256, Z_QD),
             (1536, 128, Z_KVD), (1664, MLA_ROPE, Z_KR + MLA_NOPE), (1696, 512, Z_BG), (2208, 1024, Z_MA),
             (3232, 1024, Z_MB))
WIDE_W = IN_W // N_DEV


def _column_runs():
    runs = []
    for start, width, kstart in IN_PIECES:
        col = start
        while col < start + width:
            dev = col // WIDE_W
            stop = min(start + width, (dev + 1) * WIDE_W)
            runs.append((dev, col - dev * WIDE_W, stop - col, kstart + col - start))
            col = stop
    return runs


def _win_layout(blocks, tm):
    runs = _column_runs()

    def body(g_ref, o_ref):
        o_ref[:, Z_KR:Z_KR + LANES] = jnp.zeros((tm, LANES), o_ref.dtype)
        for dev, lo, n, k in runs:
            o_ref[:, k:k + n] = g_ref[dev, :, lo:lo + n]

    return pl.pallas_call(
        body, grid=(D // tm,), name="win_layout",
        in_specs=[pl.BlockSpec((N_DEV, None, tm, WIDE_W), lambda i: (0, 0, i, 0))],
        out_specs=pl.BlockSpec((None, tm, ZW), lambda i: (0, i, 0)),
        out_shape=SDS((1, D, ZW), blocks.dtype),
        compiler_params=_cp("parallel"))(blocks)


def _win_grad_layout(dw, tm):
    runs = _column_runs()

    def body(g_ref, o_ref):
        for dev, lo, n, k in runs:
            o_ref[dev, :, lo:lo + n] = g_ref[:, k:k + n]

    return pl.pallas_call(
        body, grid=(D // tm,), name="win_grad_layout",
        in_specs=[_row(tm, ZW)],
        out_specs=pl.BlockSpec((N_DEV, None, tm, WIDE_W), lambda i: (0, 0, i, 0)),
        out_shape=SDS((N_DEV, 1, D, WIDE_W), F32),
        compiler_params=_cp("parallel"))(dw)


def _wuq_to_kernel(w):
    w = w.reshape(w.shape[:-1] + (MLA_H, MLA_QK))
    w = jnp.pad(w, [(0, 0)] * (w.ndim - 1) + [(0, LANES - MLA_QK)])
    return w.reshape(w.shape[:-2] + (QFW,))


def _wuq_from_kernel(g):
    g = g.reshape(g.shape[:-1] + (MLA_H, LANES))[..., :MLA_QK]
    return g.reshape(g.shape[:-2] + (MLA_H * MLA_QK,))


def _wukv_to_kernel(w):
    w = w.reshape(w.shape[:-1] + (MLA_H, MLA_NOPE + MLA_V))
    k = jnp.pad(w[..., :MLA_NOPE], [(0, 0)] * (w.ndim - 1) + [(0, LANES - MLA_NOPE)])
    v = w[..., MLA_NOPE:]
    return jnp.concatenate([k.reshape(k.shape[:-2] + (QFW,)), v.reshape(v.shape[:-2] + (MLA_H * MLA_V,))], axis=-1)


def _wukv_from_kernel(g):
    k = g[..., :QFW].reshape(g.shape[:-1] + (MLA_H, LANES))[..., :MLA_NOPE]
    v = g[..., QFW:].reshape(g.shape[:-1] + (MLA_H, MLA_V))
    kv = jnp.concatenate([k, v], axis=-1)
    return kv.reshape(kv.shape[:-2] + (MLA_H * (MLA_NOPE + MLA_V),))


def _rope_tables(pos):
    half = MLA_ROPE // 2
    inv = 10000.0 ** (-jnp.arange(0, MLA_ROPE, 2, dtype=F32) / MLA_ROPE)
    ang = pos.astype(F32)[:, None] * inv
    cos, sin = jnp.cos(ang), jnp.sin(ang)
    one = jnp.ones((pos.shape[0], MLA_NOPE), F32)
    zero = lambda n: jnp.zeros((pos.shape[0], n), F32)
    tc = jnp.concatenate([one, cos, cos, one[:, :LANES - MLA_QK]], axis=1)
    ts1 = jnp.concatenate([zero(MLA_NOPE + half), sin, zero(LANES - MLA_QK)], axis=1)
    ts2 = jnp.concatenate([zero(MLA_NOPE), -sin, zero(LANES - MLA_NOPE - half)], axis=1)
    return tc, ts1, ts2


def _local_step(x, p, positions, loss_target, small, wts, plan=None):
    nb, seq, _ = x.shape
    T = nb * seq
    tm = min(512, T)
    tl = min(1024, T)
    tq = min(512, seq)
    xf = x.reshape(T, D)
    pos = positions.reshape(T)
    posf = pos.astype(F32)
    pos_col, pos_row = posf.reshape(T, 1), posf.reshape(T // BLK, 1, BLK)
    tc, ts1, ts2 = _rope_tables(pos)

    wts, sm = list(wts), small
    pl_in = p.reshape(DEPTH, T, PLE)
    saved = []
    for i in range(DEPTH):
        riding = plan is not None and i == 0
        w = wts[i]
        z, hb, *got = _fwd_in(xf, sm["g_mix"], w["w_in"], tm, i,
                              ride=_Ride("gather", plan["rows0"]) if riding else None)
        if riding:
            w = wts[0] = dict(w, **plan["row_weights"](got[0]))
        oa, lse_a = _swa_fwd(sm["sink"], z, pos_col, pos_row, nb, seq, i)
        qf, kf, v = _fwd_prep(z, sm["g_q"], sm["g_kv"], w["w_uq"], w["w_ukv"], tc, ts1, ts2, tl, i)
        ob, lse_b, *got = _mla_fwd(qf, kf, v, nb, seq, tq, gather=plan["blocks1"] if riding else ())
        if riding:
            wts.append(dict(w_in=plan["w_in"](got[0]), **plan["row_weights"](got[1])))
        x1 = _fwd_merge(xf, oa, ob, z, w["w_br_a"], w["w_br_b"], w["w_out"], tm, i)
        saved.append(dict(x=xf, z=z, hb=hb, oa=oa, lse_a=lse_a, qf=qf, kf=kf, v=v, ob=ob, lse_b=lse_b, x1=x1))
        if i < DEPTH - 1:
            xf, saved[i]["pg"], saved[i]["pp"] = _fwd_ple(x1, pl_in, sm["g_ple"], w["w_ple_gate"], w["w_ple_proj"],
                                                          tl, i)

    last = _ple_loss(x1, pl_in, sm["g_ple"], w["w_ple_gate"], w["w_ple_proj"], small["g_final"],
                     loss_target.reshape(T, D), tm, DEPTH - 1)
    dg_final, loss = last[4], last[5]

    grads = [None] * DEPTH
    exchanged = {}
    for i in reversed(range(DEPTH)):
        riding = plan is not None and i == 0
        sv, w = saved[i], wts[i]
        pay = plan["payload"](grads[1]) if riding else []
        if i == DEPTH - 1:
            (dx1, dwpg, dwpp, dg_ple), got = last[:4], []
        else:
            dx1, dwpg, dwpp, dg_ple, *got = _bwd_ple(dx, sv["x1"], sv["pg"], sv["pp"], pl_in, sm["g_ple"],
                                                     w["w_ple_gate"], tm, i,
                                                     ride=_Ride("swap", pay) if riding else None)
        doa, dob, dag, dbg, dma, dmb, dsum_a, dsum_b, dwa, dwb, dwo = _bwd_merge(
            dx1, sv["oa"], sv["ob"], sv["z"], w["w_br_a"], w["w_br_b"], w["w_out"], tm, i)
        stats = (MLA_H // MLA_BWD_HEADS, MLA_BWD_HEADS, T)
        dq_b, dk_b, dv_b, *exchanged["layer1"] = _mla_bwd(
            sv["qf"], sv["kf"], sv["v"], dob, sv["lse_b"].reshape(stats), dsum_b.reshape(stats), nb, seq, tq,
            exchange=plan["add"](pay, got) if riding else ())
        dqd, dkvd, dkr, dwq, dwkv, dgq, dgkv = _bwd_prep(dq_b, dk_b, dv_b, sv["z"], sm["g_q"], sm["g_kv"],
                                                         w["w_uq"], w["w_ukv"], tc, ts1, ts2, tl, i)
        g = dict(w_uq=_wuq_from_kernel(dwq), w_ukv=_wukv_from_kernel(dwkv), w_br_a=dwa, w_br_b=dwb, w_out=dwo,
                 w_ple_gate=dwpg, w_ple_proj=dwpp)
        pay = [plan["rows_payload"](g)] if riding else []
        dq_a, dkv_a, dsink, *got = _swa_bwd(sm["sink"], sv["z"], pos_col, pos_row, doa, sv["lse_a"], dsum_a, nb, seq,
                                            i, ride=_Ride("swap", pay) if riding else None)
        dx, dzb, dg_mix = _bwd_in([dma, dmb, dq_a, dag, dbg, dqd, dkv_a, dkvd, dkr], sv["x"], sm["g_mix"], dx1,
                                  w["w_in"], tm, i)
        dwin, *exchanged["rows0"] = _wgrad_in(sv["hb"], dzb, tm,
                                              ride=_Ride("exchange", plan["add"](pay, got)) if riding else None)
        g.update(g_mix=dg_mix[0], w_in=dwin, sink=dsink[0, :SWA_H], g_q=dgq[0], g_kv=dgkv[0], g_ple=dg_ple[0])
        grads[i] = g
    return loss, dx.reshape(nb, seq, D), grads, dg_final[0], exchanged


def _row_weights(rows):
    blocks = _unpack_rows(rows)
    out = {n: _join(n, blocks[n]) for n, _ in ROWS_PIECES}
    out.update(w_uq=_wuq_to_kernel(out["w_uq"]), w_ukv=_wukv_to_kernel(out["w_ukv"]))
    return out


def _small_params(g_mix, sink, g_q, g_kv, g_ple, g_final):
    return dict(g_mix=g_mix[:, None], sink=sink, g_q=g_q[:, None], g_kv=g_kv[:, None], g_ple=g_ple[:, None],
                g_final=g_final[None])


UQ_W = MLA_H * MLA_QK // N_DEV
ROWS_PIECES = (("w_uq", QL), ("w_ukv", KVL), ("w_br_a", 512), ("w_br_b", 512), ("w_out", D), ("w_ple_gate", D),
               ("w_ple_proj", PLE))
SMALL = (("g_mix", (DEPTH, D)), ("sink", (DEPTH, SWA_H)), ("g_q", (DEPTH, QL)), ("g_kv", (DEPTH, KVL)),
         ("g_ple", (DEPTH, D)), ("g_final", (D,)))
VEC_ROWS = 48
ROWS_N = sum(r for _, r in ROWS_PIECES)
WIDE_TILE, ROWS_TILE = 256, ROWS_N // 2


def _to_rows(name, a):
    if name == "w_uq":
        a = jnp.pad(a, [(0, 0)] * (a.ndim - 1) + [(0, LANES - UQ_W)])
    return a.reshape(a.shape[:-2] + (-1, LANES))


def _from_rows(name, r):
    if name in ("w_out", "w_ple_gate"):
        return r.reshape(r.shape[:-2] + (D // N_DEV, D))
    return r[..., :UQ_W] if name == "w_uq" else r


def _pack_rows(blocks):
    return jnp.concatenate([_to_rows(n, blocks[n]) for n, _ in ROWS_PIECES], axis=-2)


def _unpack_rows(rows):
    blocks, off = {}, 0
    for n, r in ROWS_PIECES:
        blocks[n] = _from_rows(n, rows[..., off:off + r, :])
        off += r
    return blocks


def _pack_vec(vectors, loss=None):
    parts = [vectors[n].reshape(-1) for n, _ in SMALL] + ([] if loss is None else [loss.reshape(1)])
    vec = jnp.concatenate(parts)
    return jnp.pad(vec, (0, VEC_ROWS * LANES - vec.shape[0])).reshape(1, VEC_ROWS, LANES)


def _unpack_vec(vec):
    vec = vec.reshape(-1)
    vectors, off = {}, 0
    for n, shp in SMALL:
        size = 1
        for s in shp:
            size *= s
        vectors[n] = vec[off:off + size].reshape(shp)
        off += size
    return vectors, vec[off]


def _join(name, blocks):
    if name in ("w_out", "w_ple_gate"):
        return jnp.moveaxis(blocks, 0, 1).reshape(blocks.shape[1], -1, blocks.shape[-1])
    return jnp.moveaxis(blocks, 0, 2).reshape(blocks.shape[1], blocks.shape[2], -1)


def _split(name, full):
    if name in ("w_out", "w_ple_gate"):
        return jnp.moveaxis(full.reshape(full.shape[0], N_DEV, -1, full.shape[-1]), 1, 0)
    return jnp.moveaxis(full.reshape(full.shape[0], full.shape[1], N_DEV, -1), 2, 0)


MESH_ID = pl.DeviceIdType.MESH
ANY = pl.BlockSpec(memory_space=pl.ANY)


def _place():
    return lax.axis_index("x"), lax.axis_index("y"), lax.axis_index("c")


def _all_gather(blocks):
    n = len(blocks)

    def body(*refs):
        start, forward, finish = _gather_phases(refs[:n], refs[n:2 * n], *refs[2 * n:])
        start()
        forward()
        finish()

    return pl.pallas_call(
        body, name="all_gather_weights", out_shape=_gather_out(blocks),
        in_specs=[ANY] * n, out_specs=[ANY] * n, scratch_shapes=_gather_sems(n))(*blocks)


def _gather_out(blocks):
    return [SDS((N_DEV,) + b.shape, b.dtype) for b in blocks]


def _gather_sems(n):
    return [pltpu.SemaphoreType.DMA((7 * n,)), pltpu.SemaphoreType.DMA((7 * n,)), pltpu.SemaphoreType.DMA((n,))]


def _gather_phases(x_refs, out_refs, send_sems, recv_sems, local_sems):
    n = len(x_refs)
    x, y, c = _place()
    me, sibling = (x, y, c), (x, y, 1 - c)
    chips = [(1 - x, y), (x, 1 - y), (1 - x, 1 - y)]

    def slot(a, px, py, pc):
        return out_refs[a].at[4 * px + 2 * py + pc]

    def copy(a, k, blk, to, src=None):
        return pltpu.make_async_remote_copy(
            src_ref=slot(a, *blk) if src is None else src, dst_ref=slot(a, *blk),
            send_sem=send_sems.at[7 * a + k], recv_sem=recv_sems.at[7 * a + k], device_id=to,
            device_id_type=MESH_ID)

    def mine():
        return [pltpu.make_async_copy(x_refs[a], slot(a, *me), local_sems.at[a]) for a in range(n)]

    def first():
        out = []
        for a in range(n):
            out += [copy(a, 0, me, sibling, src=x_refs[a])]
            out += [copy(a, 1 + j, me, (*chip, c), src=x_refs[a]) for j, chip in enumerate(chips)]
        return out

    def passed():
        return [copy(a, 4 + j, (*chip, c), sibling) for j, chip in enumerate(chips) for a in range(n)]

    def start():
        for cp in mine() + first():
            cp.start()

    def forward():
        for j, chip in enumerate(chips):
            for a in range(n):
                copy(a, 1 + j, (*chip, c), me).wait_recv()
                copy(a, 4 + j, (*chip, c), sibling).start()

    def finish():
        for a in range(n):
            copy(a, 0, sibling, me).wait_recv()
            for j, chip in enumerate(chips):
                copy(a, 4 + j, (*chip, 1 - c), me).wait_recv()
        for cp in first() + passed():
            cp.wait_send()
        for cp in mine():
            cp.wait()

    return start, forward, finish


def _swap_sibling(arrs):
    n = len(arrs)

    def body(*refs):
        start, finish = _swap_phases(refs[:n], refs[n:2 * n], *refs[2 * n:])
        start()
        finish()

    return pl.pallas_call(
        body, name="swap_sibling", out_shape=[SDS((a.shape[0],) + a.shape[2:], a.dtype) for a in arrs],
        in_specs=[ANY] * n, out_specs=[ANY] * n, scratch_shapes=_swap_sems(n))(*arrs)


def _swap_sems(n):
    return [pltpu.SemaphoreType.DMA((n,)), pltpu.SemaphoreType.DMA((n,))]


def _swap_phases(a_refs, out_refs, send_sems, recv_sems):
    x, y, c = _place()

    def copies():
        return [pltpu.make_async_remote_copy(
            src_ref=a_refs[a].at[:, 1 - c], dst_ref=out_refs[a], send_sem=send_sems.at[a], recv_sem=recv_sems.at[a],
            device_id=(x, y, 1 - c), device_id_type=MESH_ID) for a in range(len(a_refs))]

    def start():
        for cp in copies():
            cp.start()

    def finish():
        for cp in copies():
            cp.wait()

    return start, finish


def _exchange_chips(arrs):
    n = len(arrs)

    def body(*refs):
        start, finish = _exchange_phases(refs[:n], refs[n:2 * n], *refs[2 * n:])
        start()
        finish()

    return pl.pallas_call(
        body, name="exchange_chips", out_shape=[SDS(a.shape, a.dtype) for a in arrs],
        in_specs=[ANY] * n, out_specs=[ANY] * n, scratch_shapes=_exchange_sems(n))(*arrs)


def _exchange_sems(n):
    return [pltpu.SemaphoreType.DMA((3 * n,)), pltpu.SemaphoreType.DMA((3 * n,)), pltpu.SemaphoreType.DMA((n,))]


def _exchange_phases(p_refs, out_refs, send_sems, recv_sems, local_sems):
    n = len(p_refs)
    x, y, c = _place()
    mine = 2 * x + y
    peers = [(1 - x, y), (x, 1 - y), (1 - x, 1 - y)]

    def local():
        return [pltpu.make_async_copy(p_refs[a].at[mine], out_refs[a].at[mine], local_sems.at[a]) for a in range(n)]

    def copy(a, j, src_chip, dst_chip):
        px, py = peers[j]
        return pltpu.make_async_remote_copy(
            src_ref=p_refs[a].at[src_chip], dst_ref=out_refs[a].at[dst_chip], send_sem=send_sems.at[3 * a + j],
            recv_sem=recv_sems.at[3 * a + j], device_id=(px, py, c), device_id_type=MESH_ID)

    def sends():
        return [copy(a, j, 2 * px + py, mine) for a in range(n) for j, (px, py) in enumerate(peers)]

    def start():
        for cp in local() + sends():
            cp.start()

    def finish():
        for a in range(n):
            for j, (px, py) in enumerate(peers):
                copy(a, j, mine, 2 * px + py).wait_recv()
        for cp in sends():
            cp.wait_send()
        for cp in local():
            cp.wait()

    return start, finish


def _add_mine(g, recv, core, tile, dtype):
    _, _, lead, rows, width = g.shape

    def body(c_ref, g_ref, r_ref, o_ref):
        o_ref[...] = (g_ref[...] + r_ref[...]).astype(dtype)

    spec = pl.BlockSpec((None, None, tile, width), lambda k, l, i, c_ref: (k, l, i, 0))
    return pl.pallas_call(
        body, name="add_sibling", out_shape=SDS(recv.shape, dtype),
        grid_spec=pltpu.PrefetchScalarGridSpec(
            num_scalar_prefetch=1, grid=(g.shape[0], lead, rows // tile),
            in_specs=[pl.BlockSpec((None, None, None, tile, width), lambda k, l, i, c_ref: (k, c_ref[0], l, i, 0)),
                      spec],
            out_specs=spec),
        compiler_params=_cp("parallel", "parallel", "parallel"))(core, g, recv)


def _sum_adamw(parts, w, m, v, tile):
    lead, rows, width = w.shape
    last = rows // tile - 1

    def body(*refs):
        p_refs, (w_ref, m_ref, v_ref, g_ref, d_ref, nm_ref, nv_ref) = refs[:lead], refs[lead:]
        for layer in range(lead):
            @pl.when(pl.program_id(0) == layer)
            def _(p_ref=p_refs[layer]):
                g = ((p_ref[0].astype(F32) + p_ref[1].astype(F32)) + p_ref[2].astype(F32)) + p_ref[3].astype(F32)
                nm = ADAM_B1 * m_ref[...] + (1.0 - ADAM_B1) * g
                nv = ADAM_B2 * v_ref[...] + (1.0 - ADAM_B2) * jnp.square(g)
                m_hat = nm / (1.0 - ADAM_B1 ** ADAM_STEP)
                v_hat = nv / (1.0 - ADAM_B2 ** ADAM_STEP)
                g_ref[...] = g
                nm_ref[...] = nm
                nv_ref[...] = nv
                d_ref[...] = -ADAM_LR * (m_hat / (jnp.sqrt(v_hat) + ADAM_EPS) + ADAM_WD * w_ref[...])

    pspec = lambda layer: pl.BlockSpec(
        (4, None, tile, width),
        lambda l, i: (0, 0, jnp.where(l == layer, i, jnp.where(l > layer, last, 0)), 0))
    spec = pl.BlockSpec((None, tile, width), lambda l, i: (l, i, 0))
    return pl.pallas_call(
        body, grid=(lead, rows // tile), name="sum_adamw",
        in_specs=[pspec(layer) for layer in range(lead)] + [spec, spec, spec],
        out_specs=[spec] * 4, out_shape=[SDS((lead, rows, width), F32)] * 4,
        compiler_params=_cp("arbitrary", "arbitrary"))(*parts, w, m, v)


def kernel(x, p, positions, g_mix, w_in, sink, g_q, w_uq, g_kv, w_ukv, w_br_a, w_br_b, w_out, g_ple, w_ple_gate, w_ple_proj, g_final, loss_target, m_g_mix, m_w_in, m_sink, m_g_q, m_w_uq, m_g_kv, m_w_ukv, m_w_br_a, m_w_br_b, m_w_out, m_g_ple, m_w_ple_gate, m_w_ple_proj, m_g_final, v_g_mix, v_w_in, v_sink, v_g_q, v_w_uq, v_g_kv, v_w_ukv, v_w_br_a, v_w_br_b, v_w_out, v_g_ple, v_w_ple_gate, v_w_ple_proj, v_g_final):
    weights = dict(g_mix=g_mix, w_in=w_in, sink=sink, g_q=g_q, w_uq=w_uq, g_kv=g_kv, w_ukv=w_ukv, w_br_a=w_br_a,
                   w_br_b=w_br_b, w_out=w_out, g_ple=g_ple, w_ple_gate=w_ple_gate, w_ple_proj=w_ple_proj,
                   g_final=g_final)
    mom1 = dict(g_mix=m_g_mix, w_in=m_w_in, sink=m_sink, g_q=m_g_q, w_uq=m_w_uq, g_kv=m_g_kv, w_ukv=m_w_ukv,
                w_br_a=m_w_br_a, w_br_b=m_w_br_b, w_out=m_w_out, g_ple=m_g_ple, w_ple_gate=m_w_ple_gate,
                w_ple_proj=m_w_ple_proj, g_final=m_g_final)
    mom2 = dict(g_mix=v_g_mix, w_in=v_w_in, sink=v_sink, g_q=v_g_q, w_uq=v_w_uq, g_kv=v_g_kv, w_ukv=v_w_ukv,
                w_br_a=v_w_br_a, w_br_b=v_w_br_b, w_out=v_w_out, g_ple=v_g_ple, w_ple_gate=v_w_ple_gate,
                w_ple_proj=v_w_ple_proj, g_final=v_g_final)
    assert DEPTH == 2
    wide = lambda d: d["w_in"]
    rows = lambda d: _pack_rows(d)
    core = lax.axis_index("c").astype(jnp.int32).reshape(1)

    w16 = [wide(weights).astype(BF16), rows(weights).astype(BF16)]
    wts0 = dict(w_in=_win_layout(_all_gather([w16[0][:1]])[0], 256))
    small = _small_params(g_mix, sink, g_q, g_kv, g_ple, g_final)

    def wide_payload(g):
        return _win_grad_layout(g["w_in"], 256).reshape(N_DEV // 2, 2, 1, D, WIDE_W)

    def rows_payload(g):
        return _pack_rows({n: _split(n, g[n][None]) for n, _ in ROWS_PIECES}).reshape(N_DEV // 2, 2, 1, ROWS_N, LANES)

    def add(pay, got):
        tiles = {D: (WIDE_TILE, BF16), ROWS_N: (ROWS_TILE, BF16), VEC_ROWS: (VEC_ROWS, F32)}
        return [_add_mine(a, b, core, *tiles[a.shape[-2]]) for a, b in zip(pay, got)]

    plan = dict(rows0=[w16[1][:1]], blocks1=[a[1:] for a in w16], w_in=lambda blocks: _win_layout(blocks, 256),
                row_weights=_row_weights, payload=lambda g: [wide_payload(g), rows_payload(g)],
                rows_payload=rows_payload, add=add)
    loss, grad_x, grads, dg_final, rode = _local_step(x, p, positions, loss_target, small, [wts0], plan)

    vectors = {n: jnp.stack([grads[i][n] for i in range(DEPTH)]) for n, _ in SMALL[:-1]}
    vectors["g_final"] = dg_final
    pay = [wide_payload(grads[0]),
           jnp.broadcast_to(_pack_vec(vectors, loss[0, 0]), (N_DEV // 2, 2, 1, VEC_ROWS, LANES))]
    parts_wide0, parts_vec = _exchange_chips(add(pay, _swap_sibling(pay)))
    out_wide = _sum_adamw([parts_wide0, rode["layer1"][0]], wide(weights), wide(mom1), wide(mom2), WIDE_TILE)
    out_rows = _sum_adamw([rode["rows0"][0], rode["layer1"][1]], rows(weights), rows(mom1), rows(mom2), ROWS_TILE)
    out_vec = _sum_adamw([parts_vec], _pack_vec(weights), _pack_vec(mom1), _pack_vec(mom2), VEC_ROWS)

    outs = []
    for ow, orow, ovec in zip(out_wide, out_rows, out_vec):
        named = _unpack_rows(orow)
        named.update(_unpack_vec(ovec)[0])
        named["w_in"] = ow
        outs += [named[n] for n in weights]
    loss = _unpack_vec(out_vec[0])[1]
    return (loss, grad_x, *outs)
```

```python
import functools

import jax
import jax.numpy as jnp
from jax import lax
from jax.experimental import pallas as pl
from jax.experimental.pallas import tpu as pltpu

F32, BF16 = jnp.float32, jnp.bfloat16
SDS = jax.ShapeDtypeStruct

D = 1024
DEPTH = 2
PLE = 256
BLK = 128
EPS = 1e-6
NEG = -1e30
SWA_H, SWA_KV, SWA_DH = 8, 2, 64
MLA_H, MLA_NOPE, MLA_ROPE, MLA_V = 8, 64, 32, 64
MLA_QK = MLA_NOPE + MLA_ROPE
QL, KVL = 256, 128
IN_W = 4256
N_DEV = 8

V7X_VMEM_BYTES = 64 * 1024 * 1024
LANES = 128
VMEM_LIMIT = V7X_VMEM_BYTES * 7 // 8

ZW = 4352
Z_MA, Z_MB, Z_AQ, Z_AG, Z_BG, Z_QD, Z_AK, Z_AV, Z_KVD, Z_KR = 0, 1024, 2048, 2560, 3072, 3584, 3840, 3968, 4096, 4224
QFW = MLA_H * LANES
KVW = QFW + MLA_H * MLA_V
MLA_SCALE = MLA_QK ** -0.5
LOG2E = 1.4426950408889634
MLA_FWD_HEADS, MLA_BWD_HEADS = 8, 4
SWA_SCALE = SWA_DH ** -0.5
ROLL_UP, ROLL_DOWN = MLA_ROPE // 2, LANES - MLA_ROPE // 2

ADAM_LR, ADAM_B1, ADAM_B2, ADAM_EPS, ADAM_WD, ADAM_STEP = 0.001, 0.9, 0.999, 1e-08, 0.01, 10


def _cp(*sem):
    return pltpu.CompilerParams(dimension_semantics=sem, vmem_limit_bytes=VMEM_LIMIT)


def _row(tm, w, col=0):
    return pl.BlockSpec((tm, w), lambda i: (i, col))


def _res(shape, layer=None):
    if layer is None:
        return pl.BlockSpec(shape, lambda *_: (0,) * len(shape), pipeline_mode=pl.Buffered(1))
    return pl.BlockSpec((None,) + shape, lambda *_: (layer,) + (0,) * len(shape), pipeline_mode=pl.Buffered(1))


def _acc(shape):
    return pl.BlockSpec(shape, lambda *_: (0,) * len(shape))


def _rstd(xf):
    return lax.rsqrt(jnp.mean(xf * xf, axis=-1, keepdims=True) + EPS)


def _norm_bwd(dh, n, r, g):
    dn = dh * g
    return r * (dn - n * jnp.mean(dn * n, axis=-1, keepdims=True)), dh * n


def _nt(a, b):
    return lax.dot_general(a, b, (((1,), (1,)), ((), ())), preferred_element_type=F32)


def _tn(a, b):
    return lax.dot_general(a, b, (((0,), (0,)), ((), ())), preferred_element_type=F32)


def _nn(a, b):
    return jnp.dot(a, b, preferred_element_type=F32)


def _sig(x):
    return jax.nn.sigmoid(x)


def _rope(t, c, s1, s2):
    return t * c + pltpu.roll(t, ROLL_UP, 1) * s1 + pltpu.roll(t, ROLL_DOWN, 1) * s2


def _rope_t(d, c, s1, s2):
    return d * c + pltpu.roll(d * s1, ROLL_DOWN, 1) + pltpu.roll(d * s2, ROLL_UP, 1)


def _fwd_in(x, g, w, tm, layer, ride=None):
    T = x.shape[0]
    grid = (T // tm,)

    def body(x_ref, g_ref, w_ref, z_ref, h_ref):
        xf = x_ref[...]
        h = ((xf * _rstd(xf)) * g_ref[...]).astype(BF16)
        h_ref[...] = h
        z_ref[...] = _nn(h, w_ref[...])

    r_in, r_out, r_shape, r_scratch, r_args = _ride_args(ride)
    return pl.pallas_call(
        _riding(ride, body, 3, 2, grid), grid=grid, name="fwd_in_ride" if ride else "fwd_in",
        in_specs=[_row(tm, D), _res((1, D), layer), _res((D, ZW), 0)] + r_in,
        out_specs=[_row(tm, ZW), _row(tm, D)] + r_out,
        out_shape=[SDS((T, ZW), F32), SDS((T, D), BF16)] + r_shape, scratch_shapes=r_scratch,
        compiler_params=_cp("arbitrary"))(x, g, w, *r_args)


def _fwd_prep(z, gq, gkv, wq, wkv, tc, ts1, ts2, tm, layer):
    T = z.shape[0]

    def body(qd_ref, kvd_ref, kr_ref, gq_ref, gkv_ref, wq_ref, wkv_ref, c_ref, s1_ref, s2_ref, q_ref, k_ref, v_ref):
        qd, kvd = qd_ref[...], kvd_ref[...]
        hq = ((qd * _rstd(qd)) * gq_ref[...]).astype(BF16)
        hkv = ((kvd * _rstd(kvd)) * gkv_ref[...]).astype(BF16)
        qf = _nn(hq, wq_ref[...])
        kvf = _nn(hkv, wkv_ref[...])
        c, s1, s2 = c_ref[...], s1_ref[...], s2_ref[...]
        krb = _rope(kr_ref[...], c, s1, s2)
        for h in range(MLA_H):
            sl = slice(LANES * h, LANES * (h + 1))
            q_ref[:, sl] = _rope(qf[:, sl], c, s1, s2).astype(BF16)
            k_ref[:, sl] = (kvf[:, sl] + krb).astype(BF16)
        v_ref[...] = kvf[:, QFW:].astype(BF16)

    return pl.pallas_call(
        body, grid=(T // tm,), name="fwd_prep",
        in_specs=[_row(tm, QL, Z_QD // QL), _row(tm, KVL, Z_KVD // KVL), _row(tm, LANES, Z_KR // LANES),
                  _res((1, QL), layer), _res((1, KVL), layer), _res((QL, QFW), 0), _res((KVL, KVW), 0),
                  _row(tm, LANES), _row(tm, LANES), _row(tm, LANES)],
        out_specs=[_row(tm, QFW), _row(tm, QFW), _row(tm, MLA_H * MLA_V)],
        out_shape=[SDS((T, QFW), BF16), SDS((T, QFW), BF16), SDS((T, MLA_H * MLA_V), BF16)],
        compiler_params=_cp("parallel"))(z, z, z, gq, gkv, wq, wkv, tc, ts1, ts2)


def _grid_ends(grid):
    ids = [pl.program_id(a) for a in range(len(grid))]
    inner_first = functools.reduce(jnp.logical_and, [i == 0 for i in ids[1:]], True)
    last = functools.reduce(jnp.logical_and, [i == g - 1 for i, g in zip(ids, grid)])
    return (ids[0] == 0) & inner_first, (ids[0] == 3 * grid[0] // 4) & inner_first, last


class _Ride:
    def __init__(self, kind, arrays):
        self.kind, self.arrays, self.n = kind, list(arrays), len(arrays)

    def out_shape(self):
        if self.kind == "gather":
            return _gather_out(self.arrays)
        if self.kind == "swap":
            return [SDS((a.shape[0],) + a.shape[2:], a.dtype) for a in self.arrays]
        return [SDS(a.shape, a.dtype) for a in self.arrays]

    def sems(self):
        if self.kind == "gather":
            return _gather_sems(self.n)
        if self.kind == "swap":
            return _swap_sems(self.n)
        return _exchange_sems(self.n)

    def phases(self, in_refs, out_refs, *sems):
        if self.kind == "gather":
            return _gather_phases(in_refs, out_refs, *sems)
        start, finish = (_swap_phases if self.kind == "swap" else _exchange_phases)(in_refs, out_refs, *sems)
        return start, None, finish


def _riding(ride, body, n_in, n_out, grid):
    if ride is None:
        return body
    n, n_sems = ride.n, len(ride.sems())

    def wrapped(*refs):
        ins, r_in = refs[:n_in], refs[n_in:n_in + n]
        outs, r_out = refs[n_in + n:n_in + n + n_out], refs[n_in + n + n_out:n_in + 2 * n + n_out]
        rest = refs[n_in + 2 * n + n_out:]
        scratch, sems = rest[:len(rest) - n_sems], rest[len(rest) - n_sems:]
        start, middle, finish = ride.phases(r_in, r_out, *sems)
        at_first, at_middle, at_last = _grid_ends(grid)
        pl.when(at_first)(start)
        if middle is not None:
            pl.when(at_middle)(middle)
        body(*ins, *outs, *scratch)
        pl.when(at_last)(finish)

    return wrapped


def _ride_args(ride):
    if ride is None:
        return [], [], [], [], []
    return [ANY] * ride.n, [ANY] * ride.n, ride.out_shape(), ride.sems(), ride.arrays


def _mla_fwd(qf, kf, v, nb, seq, tq, gather=()):
    T = qf.shape[0]
    nq = seq // tq
    hp = MLA_FWD_HEADS
    pw = hp * LANES
    pairs = [(qi, ki) for qi in range(nq) for ki in range(qi + 1)]
    qi_tab = jnp.array([qk[0] for qk in pairs], jnp.int32)
    ki_tab = jnp.array([qk[1] for qk in pairs], jnp.int32)
    grid = (nb, MLA_H // hp, len(pairs))
    n_g = len(gather)

    def body(qi_ref, ki_ref, q_ref, k_ref, v_ref, *rest):
        x_refs, (o_ref, lse_ref), got_refs = rest[:n_g], rest[n_g:n_g + 2], rest[n_g + 2:2 * n_g + 2]
        (m_s, l_s, acc_s), sems = rest[2 * n_g + 2:2 * n_g + 5], rest[2 * n_g + 5:]
        qi, ki = qi_ref[pl.program_id(2)], ki_ref[pl.program_id(2)]
        if n_g:
            start, forward, finish = _gather_phases(x_refs, got_refs, *sems)
            at_first, at_middle, at_last = _grid_ends(grid)
            pl.when(at_first)(start)
            pl.when(at_middle)(forward)

        @pl.when(ki == 0)
        def _():
            m_s[...] = jnp.full(m_s.shape, NEG, F32)
            l_s[...] = jnp.zeros(l_s.shape, F32)
            acc_s[...] = jnp.zeros(acc_s.shape, F32)

        def step(masked):
            parts = [(0, tq // 2, tq // 2), (tq // 2, tq, tq)] if masked else [(0, tq, tq)]
            work = [(j, a, b, kh) for j in range(hp) for a, b, kh in parts]
            ss = []
            for j, a, b, kh in work:
                wide = slice(LANES * j, LANES * (j + 1))
                s = _nt(k_ref[:kh, wide], q_ref[a:b, wide]) * (MLA_SCALE * LOG2E)
                if masked:
                    keys = lax.broadcasted_iota(jnp.int32, (kh, b - a), 0)
                    queries = a + lax.broadcasted_iota(jnp.int32, (kh, b - a), 1)
                    s = jnp.where(keys <= queries, s, NEG)
                ss.append(s)
            ps, alphas = [], []
            for (j, a, b, kh), s in zip(work, ss):
                m_prev = m_s[j, :, a:b]
                m_new = jnp.maximum(m_prev, jnp.max(s, axis=0, keepdims=True))
                alpha = jnp.exp2(m_prev - m_new)
                p = jnp.exp2(s - m_new)
                l_s[j, :, a:b] = alpha * l_s[j, :, a:b] + jnp.sum(p, axis=0, keepdims=True)
                m_s[j, :, a:b] = m_new
                ps.append(p.astype(BF16))
                alphas.append(alpha)
            for (j, a, b, kh), p, alpha in zip(work, ps, alphas):
                rows = slice(MLA_V * j, MLA_V * (j + 1))
                acc_s[rows, a:b] = alpha * acc_s[rows, a:b] + _tn(v_ref[:kh, rows], p)

        @pl.when(ki < qi)
        def _():
            step(False)

        @pl.when(ki == qi)
        def _():
            step(True)
            for j in range(hp):
                rows = slice(MLA_V * j, MLA_V * (j + 1))
                acc_s[rows, :] = acc_s[rows, :] / l_s[j]
                lse_ref[j:j + 1, :] = m_s[j] + jnp.log2(l_s[j])
            o_ref[...] = acc_s[...].T

        if n_g:
            pl.when(at_last)(finish)

    q_map = lambda b, g, s, qi_ref, ki_ref: (b * nq + qi_ref[s], g)
    kv_map = lambda b, g, s, qi_ref, ki_ref: (b * nq + ki_ref[s], g)
    return pl.pallas_call(
        body, name="mla_fwd_gather" if n_g else "mla_fwd",
        grid_spec=pltpu.PrefetchScalarGridSpec(
            num_scalar_prefetch=2, grid=grid,
            in_specs=[pl.BlockSpec((tq, pw), q_map), pl.BlockSpec((tq, pw), kv_map),
                      pl.BlockSpec((tq, hp * MLA_V), kv_map)] + [ANY] * n_g,
            out_specs=[pl.BlockSpec((tq, hp * MLA_V), q_map),
                       pl.BlockSpec((hp, tq), lambda b, g, s, qi_ref, ki_ref: (g, b * nq + qi_ref[s]))]
            + [ANY] * n_g,
            scratch_shapes=[pltpu.VMEM((hp, 1, tq), F32), pltpu.VMEM((hp, 1, tq), F32),
                            pltpu.VMEM((hp * MLA_V, tq), F32)]
            + (_gather_sems(n_g) if n_g else [])),
        out_shape=[SDS((T, MLA_H * MLA_V), F32), SDS((MLA_H, T), F32)] + _gather_out(gather),
        compiler_params=_cp("arbitrary", "arbitrary", "arbitrary"))(qi_tab, ki_tab, qf, kf, v, *gather)


def _swa_specs(nblk):
    cur = lambda b, n: (b * nblk + n, 0)
    prev = lambda b, n: (b * nblk + jnp.maximum(n - 1, 0), 0)
    kvc = Z_AK // (2 * BLK)
    return [pl.BlockSpec(memory_space=pltpu.SMEM),
            pl.BlockSpec((BLK, 512), lambda b, n: (b * nblk + n, Z_AQ // 512)),
            pl.BlockSpec((BLK, 2 * BLK), lambda b, n: (b * nblk + n, kvc)),
            pl.BlockSpec((BLK, 2 * BLK), lambda b, n: (b * nblk + jnp.maximum(n - 1, 0), kvc)),
            pl.BlockSpec((BLK, 1), cur),
            pl.BlockSpec((BLK, 1), prev),
            pl.BlockSpec((1, 1, BLK), lambda b, n: (b * nblk + n, 0, 0))]


def _swa_scores(n, q_ref, kvc_ref, kvp_ref, pcc_ref, pcp_ref, pr_ref):
    kv = jnp.concatenate([kvp_ref[...], kvc_ref[...]], axis=0)
    kb, vb = kv[:, :BLK].astype(BF16), kv[:, BLK:].astype(BF16)
    dist = pr_ref[0] - jnp.concatenate([pcp_ref[...], pcc_ref[...]], axis=0)
    key = lax.broadcasted_iota(jnp.int32, (2 * BLK, BLK), 0)
    qry = lax.broadcasted_iota(jnp.int32, (2 * BLK, BLK), 1)
    valid = (key > qry) & (key <= qry + BLK) & ((key >= BLK) | (n > 0))

    def scores(h):
        g = h // (SWA_H // SWA_KV)
        qh = q_ref[:, SWA_DH * h:SWA_DH * (h + 1)].astype(BF16)
        s = _nt(kb[:, SWA_DH * g:SWA_DH * (g + 1)], qh) * (SWA_SCALE * LOG2E) - (2.0 ** -(h + 1) * LOG2E) * dist
        return qh, jnp.where(valid, s, NEG)

    return kb, vb, scores


def _swa_fwd(sink, z, pos_col, pos_row, nb, seq, layer):
    T = z.shape[0]
    nblk = seq // BLK

    def body(sink_ref, q_ref, kvc_ref, kvp_ref, pcc_ref, pcp_ref, pr_ref, o_ref, lse_ref):
        kb, vb, scores = _swa_scores(pl.program_id(1), q_ref, kvc_ref, kvp_ref, pcc_ref, pcp_ref, pr_ref)
        ss = [scores(h)[1] for h in range(SWA_H)]
        es, dens = [], []
        for h in range(SWA_H):
            sk = sink_ref[layer, h] * LOG2E
            m = jnp.maximum(jnp.max(ss[h], axis=0, keepdims=True), sk)
            e = jnp.exp2(ss[h] - m)
            den = jnp.sum(e, axis=0, keepdims=True) + jnp.exp2(sk - m)
            lse_ref[h:h + 1, :] = m + jnp.log2(den)
            es.append(e.astype(BF16))
            dens.append(den)
        outs = []
        for h in range(SWA_H):
            g = h // (SWA_H // SWA_KV)
            outs.append(_tn(vb[:, SWA_DH * g:SWA_DH * (g + 1)], es[h]) / dens[h])
        o_ref[...] = jnp.concatenate(outs, axis=0).T

    return pl.pallas_call(
        body, grid=(nb, nblk), name="swa_fwd",
        in_specs=_swa_specs(nblk),
        out_specs=[pl.BlockSpec((BLK, 512), lambda b, n: (b * nblk + n, 0)),
                   pl.BlockSpec((SWA_H, BLK), lambda b, n: (0, b * nblk + n))],
        out_shape=[SDS((T, 512), F32), SDS((SWA_H, T), F32)],
        compiler_params=_cp("parallel", "parallel"))(sink, z, z, z, pos_col, pos_col, pos_row)


def _fwd_merge(x, oa, ob, z, wa, wb, wo, tm, layer):
    T = x.shape[0]

    def body(x_ref, oa_ref, ob_ref, ag_ref, bg_ref, ma_ref, mb_ref, wa_ref, wb_ref, wo_ref, x1_ref):
        ag, bg = ag_ref[...], bg_ref[...]
        ua = _nn((oa_ref[...] * (ag * _sig(ag))).astype(BF16), wa_ref[...])
        ub = _nn((ob_ref[...] * (bg * _sig(bg))).astype(BF16), wb_ref[...])
        y = _sig(ma_ref[...]) * ua + _sig(mb_ref[...]) * ub
        x1_ref[...] = x_ref[...] + _nn(y.astype(BF16), wo_ref[...])

    return pl.pallas_call(
        body, grid=(T // tm,), name="fwd_merge",
        in_specs=[_row(tm, D), _row(tm, 512), _row(tm, 512), _row(tm, 512, Z_AG // 512), _row(tm, 512, Z_BG // 512),
                  _row(tm, D, Z_MA // D), _row(tm, D, Z_MB // D),
                  _res((512, D), 0), _res((512, D), 0), _res((D, D), 0)],
        out_specs=_row(tm, D),
        out_shape=SDS((T, D), F32),
        compiler_params=_cp("parallel"))(x, oa, ob, z, z, z, z, wa, wb, wo)


def _fwd_ple(x1, p, g, wpg, wpp, tm, layer):
    T = x1.shape[0]

    def body(x_ref, p_ref, g_ref, wpg_ref, wpp_ref, x2_ref, pg_ref, pp_ref):
        xf = x_ref[...]
        h1 = ((xf * _rstd(xf)) * g_ref[...]).astype(BF16)
        pg = _sig(_nn(h1, wpg_ref[...]))
        pp = _nn(p_ref[...].astype(BF16), wpp_ref[...])
        pg_ref[...] = pg
        pp_ref[...] = pp
        x2_ref[...] = xf + pg * pp

    return pl.pallas_call(
        body, grid=(T // tm,), name="fwd_ple",
        in_specs=[_row(tm, D), pl.BlockSpec((None, tm, PLE), lambda i: (layer, i, 0)),
                  _res((1, D), layer), _res((D, D), 0), _res((PLE, D), 0)],
        out_specs=[_row(tm, D)] * 3,
        out_shape=[SDS((T, D), F32)] * 3,
        compiler_params=_cp("parallel"))(x1, p, g, wpg, wpp)


def _ple_loss(x1, p, g, wpg, wpp, g_final, tgt, tm, layer):
    T = x1.shape[0]

    def body(x_ref, p_ref, g_ref, wpg_ref, wpp_ref, gf_ref, t_ref, dx_ref, dwg_ref, dwp_ref, dg_ref, dgf_ref, loss_ref):
        @pl.when(pl.program_id(0) == 0)
        def _():
            for ref in (dwg_ref, dwp_ref, dg_ref, dgf_ref, loss_ref):
                ref[...] = jnp.zeros(ref.shape, F32)

        xf, gp, gf = x_ref[...], g_ref[...], gf_ref[...]
        r = _rstd(xf)
        n = xf * r
        h1 = (n * gp).astype(BF16)
        pb = p_ref[...].astype(BF16)
        pg = _sig(_nn(h1, wpg_ref[...]))
        pp = _nn(pb, wpp_ref[...])
        x2 = xf + pg * pp
        r2 = _rstd(x2)
        n2 = x2 * r2
        err = n2 * gf - t_ref[...]
        loss_ref[...] += 0.5 * jnp.sum(jnp.mean(err * err, axis=-1, keepdims=True), axis=0, keepdims=True)
        d, dgfr = _norm_bwd(err * (1.0 / D), n2, r2, gf)
        dgf_ref[...] += jnp.sum(dgfr, axis=0, keepdims=True)
        dpgl = (d * pp * pg * (1.0 - pg)).astype(BF16)
        dwg_ref[...] += _tn(h1, dpgl)
        dwp_ref[...] += _tn(pb, (d * pg).astype(BF16))
        dxn, dgr = _norm_bwd(_nt(dpgl, wpg_ref[...]), n, r, gp)
        dx_ref[...] = d + dxn
        dg_ref[...] += jnp.sum(dgr, axis=0, keepdims=True)

    return pl.pallas_call(
        body, grid=(T // tm,), name="ple_loss",
        in_specs=[_row(tm, D), pl.BlockSpec((None, tm, PLE), lambda i: (layer, i, 0)), _res((1, D), layer),
                  _res((D, D), 0), _res((PLE, D), 0), _res((1, D)), _row(tm, D)],
        out_specs=[_row(tm, D), _acc((D, D)), _acc((PLE, D)), _acc((1, D)), _acc((1, D)), _acc((1, LANES))],
        out_shape=[SDS((T, D), F32), SDS((D, D), F32), SDS((PLE, D), F32), SDS((1, D), F32), SDS((1, D), F32),
                   SDS((1, LANES), F32)],
        compiler_params=_cp("arbitrary"))(x1, p, g, wpg, wpp, g_final, tgt)


def _bwd_ple(dx2, x1, pg, pp, p, g, wpg, tm, layer, ride=None):
    T = x1.shape[0]
    grid = (T // tm,)

    def body(d_ref, x_ref, pg_ref, pp_ref, p_ref, g_ref, w_ref, dx_ref, dwg_ref, dwp_ref, dg_ref):
        @pl.when(pl.program_id(0) == 0)
        def _():
            dwg_ref[...] = jnp.zeros(dwg_ref.shape, F32)
            dwp_ref[...] = jnp.zeros(dwp_ref.shape, F32)
            dg_ref[...] = jnp.zeros(dg_ref.shape, F32)

        d, xf, pg, gf = d_ref[...], x_ref[...], pg_ref[...], g_ref[...]
        r = _rstd(xf)
        n = xf * r
        dpgl = (d * pp_ref[...] * pg * (1.0 - pg)).astype(BF16)
        dwg_ref[...] += _tn((n * gf).astype(BF16), dpgl)
        dwp_ref[...] += _tn(p_ref[...].astype(BF16), (d * pg).astype(BF16))
        dxn, dgr = _norm_bwd(_nt(dpgl, w_ref[...]), n, r, gf)
        dx_ref[...] = d + dxn
        dg_ref[...] += jnp.sum(dgr, axis=0, keepdims=True)

    r_in, r_out, r_shape, r_scratch, r_args = _ride_args(ride)
    return pl.pallas_call(
        _riding(ride, body, 7, 4, grid), grid=grid, name="bwd_ple_ride" if ride else "bwd_ple",
        in_specs=[_row(tm, D)] * 4 + [pl.BlockSpec((None, tm, PLE), lambda i: (layer, i, 0)),
                                      _res((1, D), layer), _res((D, D), 0)] + r_in,
        out_specs=[_row(tm, D), _acc((D, D)), _acc((PLE, D)), _acc((1, D))] + r_out,
        out_shape=[SDS((T, D), F32), SDS((D, D), F32), SDS((PLE, D), F32), SDS((1, D), F32)] + r_shape,
        scratch_shapes=r_scratch,
        compiler_params=_cp("arbitrary"))(dx2, x1, pg, pp, p, g, wpg, *r_args)


def _bwd_merge(dx1, oa, ob, z, wa, wb, wo, tm, layer):
    T = dx1.shape[0]

    def body(d_ref, oa_ref, ob_ref, ag_ref, bg_ref, ma_ref, mb_ref, wa_ref, wb_ref, wo_ref,
             doa_ref, dob_ref, dag_ref, dbg_ref, dma_ref, dmb_ref, dsa_ref, dsb_ref, dwa_ref, dwb_ref, dwo_ref):
        @pl.when(pl.program_id(0) == 0)
        def _():
            dwa_ref[...] = jnp.zeros(dwa_ref.shape, F32)
            dwb_ref[...] = jnp.zeros(dwb_ref.shape, F32)
            dwo_ref[...] = jnp.zeros(dwo_ref.shape, F32)

        db = d_ref[...].astype(BF16)
        gated = []
        for o_ref, gate_ref, w_ref in ((oa_ref, ag_ref, wa_ref), (ob_ref, bg_ref, wb_ref)):
            raw, gate = o_ref[...], gate_ref[...]
            sg = _sig(gate)
            silu = gate * sg
            ob16 = (raw * silu).astype(BF16)
            gated.append((raw, gate, sg, silu, ob16, _nn(ob16, w_ref[...])))
        ua, ub = gated[0][5], gated[1][5]
        sa, sb = _sig(ma_ref[...]), _sig(mb_ref[...])
        dwo_ref[...] += _tn((sa * ua + sb * ub).astype(BF16), db)
        dy = _nt(db, wo_ref[...])
        dma_ref[...] = (dy * ua * sa * (1.0 - sa)).astype(BF16)
        dmb_ref[...] = (dy * ub * sb * (1.0 - sb)).astype(BF16)
        for (s, w_ref, do_ref, dgate_ref, dw_ref, ds_ref), (raw, gate, sg, silu, ob16, _) in zip((
                (sa, wa_ref, doa_ref, dag_ref, dwa_ref, dsa_ref),
                (sb, wb_ref, dob_ref, dbg_ref, dwb_ref, dsb_ref)), gated):
            du = (dy * s).astype(BF16)
            dw_ref[...] += _tn(ob16, du)
            do = _nt(du, w_ref[...])
            draw = do * silu
            do_ref[...] = draw.astype(BF16)
            dgate_ref[...] = (do * raw * (sg * (1.0 + gate * (1.0 - sg)))).astype(BF16)
            ds_ref[...] = jnp.sum((draw * raw).T.reshape(MLA_H, MLA_V, tm), axis=1)

    return pl.pallas_call(
        body, grid=(T // tm,), name="bwd_merge",
        in_specs=[_row(tm, D), _row(tm, 512), _row(tm, 512), _row(tm, 512, Z_AG // 512), _row(tm, 512, Z_BG // 512),
                  _row(tm, D, Z_MA // D), _row(tm, D, Z_MB // D),
                  _res((512, D), 0), _res((512, D), 0), _res((D, D), 0)],
        out_specs=[_row(tm, 512)] * 4 + [_row(tm, D)] * 2 + [pl.BlockSpec((MLA_H, tm), lambda i: (0, i))] * 2
        + [_acc((512, D)), _acc((512, D)), _acc((D, D))],
        out_shape=[SDS((T, 512), BF16), SDS((T, 512), BF16), SDS((T, 512), BF16), SDS((T, 512), BF16),
                   SDS((T, D), BF16), SDS((T, D), BF16), SDS((MLA_H, T), F32), SDS((MLA_H, T), F32),
                   SDS((512, D), F32), SDS((512, D), F32), SDS((D, D), F32)],
        compiler_params=_cp("arbitrary"))(dx1, oa, ob, z, z, z, z, wa, wb, wo)


def _mla_bwd(qf, kf, v, do, lse, dsum, nb, seq, tq, exchange=()):
    T = qf.shape[0]
    nq = seq // tq
    hp = MLA_BWD_HEADS
    pw = hp * LANES
    pairs = [(qi, ki) for ki in range(nq) for qi in range(ki, nq)]
    qi_tab = jnp.array([qk[0] for qk in pairs], jnp.int32)
    ki_tab = jnp.array([qk[1] for qk in pairs], jnp.int32)
    grid = (nb, MLA_H // hp, len(pairs))
    n_x = len(exchange)

    def body(qi_ref, ki_ref, q_ref, k_ref, v_ref, do_ref, lse_ref, dsum_ref, *rest):
        p_refs, (dq_ref, dk_ref, dv_ref), got_refs = rest[:n_x], rest[n_x:n_x + 3], rest[n_x + 3:2 * n_x + 3]
        (dk_s, dv_s, dqt_s), sems = rest[2 * n_x + 3:2 * n_x + 6], rest[2 * n_x + 6:]
        step_id = pl.program_id(2)
        qi, ki = qi_ref[step_id], ki_ref[step_id]
        if n_x:
            start, finish = _exchange_phases(p_refs, got_refs, *sems)
            at_first, _, at_last = _grid_ends(grid)
            pl.when(at_first)(start)

        @pl.when(step_id == 0)
        def _():
            dqt_s[...] = jnp.zeros(dqt_s.shape, F32)

        @pl.when(qi == ki)
        def _():
            dk_s[...] = jnp.zeros(dk_s.shape, F32)
            dv_s[...] = jnp.zeros(dv_s.shape, F32)

        def step(masked):
            if masked:
                keys = lax.broadcasted_iota(jnp.int32, (tq, tq), 0)
                queries = lax.broadcasted_iota(jnp.int32, (tq, tq), 1)
                mask = keys <= queries
            for j in range(hp):
                wide = slice(LANES * j, LANES * (j + 1))
                sl = slice(MLA_V * j, MLA_V * (j + 1))
                q, k = q_ref[:, wide], k_ref[:, wide]
                dob = do_ref[:, sl].astype(BF16)
                s = _nt(k, q) * (MLA_SCALE * LOG2E)
                if masked:
                    s = jnp.where(mask, s, NEG)
                p = jnp.exp2(s - lse_ref[j:j + 1, :])
                dv_s[:, sl] += _nn(p.astype(BF16), dob)
                ds = (p * (_nt(v_ref[:, sl], dob) - dsum_ref[j:j + 1, :]) * MLA_SCALE).astype(BF16)
                dk_s[:, wide] += _nn(ds, q)
                dqt_s[qi, wide, :] += _tn(k, ds)

        @pl.when(qi > ki)
        def _():
            step(False)

        @pl.when(qi == ki)
        def _():
            step(True)

        @pl.when(qi == nq - 1)
        def _():
            dk_ref[...] = dk_s[...]
            dv_ref[...] = dv_s[...]

        @pl.when(step_id == len(pairs) - 1)
        def _():
            for n in range(nq):
                dq_ref[tq * n:tq * (n + 1), :] = dqt_s[n].T

        if n_x:
            pl.when(at_last)(finish)

    qmap = lambda b, g, s, qi_ref, ki_ref: (b * nq + qi_ref[s], g)
    kmap = lambda b, g, s, qi_ref, ki_ref: (b * nq + ki_ref[s], g)
    stat = pl.BlockSpec((None, hp, tq), lambda b, g, s, qi_ref, ki_ref: (g, 0, b * nq + qi_ref[s]))
    vw = hp * MLA_V
    return pl.pallas_call(
        body, name="mla_bwd_exchange" if n_x else "mla_bwd",
        grid_spec=pltpu.PrefetchScalarGridSpec(
            num_scalar_prefetch=2, grid=grid,
            in_specs=[pl.BlockSpec((tq, pw), qmap), pl.BlockSpec((tq, pw), kmap), pl.BlockSpec((tq, vw), kmap),
                      pl.BlockSpec((tq, vw), qmap), stat, stat] + [ANY] * n_x,
            out_specs=[pl.BlockSpec((seq, pw), lambda b, g, s, qi_ref, ki_ref: (b, g)),
                       pl.BlockSpec((tq, pw), kmap), pl.BlockSpec((tq, vw), kmap)] + [ANY] * n_x,
            scratch_shapes=[pltpu.VMEM((tq, pw), F32), pltpu.VMEM((tq, vw), F32), pltpu.VMEM((nq, pw, tq), F32)]
            + (_exchange_sems(n_x) if n_x else [])),
        out_shape=[SDS((T, QFW), F32), SDS((T, QFW), F32), SDS((T, MLA_H * MLA_V), F32)]
        + [SDS(a.shape, a.dtype) for a in exchange],
        compiler_params=_cp("arbitrary", "arbitrary", "arbitrary"))(qi_tab, ki_tab, qf, kf, v, do, lse, dsum, *exchange)


def _swa_bwd(sink, z, pos_col, pos_row, do, lse, dsum, nb, seq, layer, ride=None):
    T = z.shape[0]
    nblk = seq // BLK

    def body(sink_ref, q_ref, kvc_ref, kvp_ref, pcc_ref, pcp_ref, pr_ref, do_ref, lse_ref, dsum_ref,
             dq_ref, dkv_ref, dsink_ref):
        b, n = pl.program_id(0), pl.program_id(1)

        @pl.when((b == 0) & (n == 0))
        def _():
            dsink_ref[...] = jnp.zeros(dsink_ref.shape, F32)

        @pl.when(n == 0)
        def _():
            dkv_ref[...] = jnp.zeros(dkv_ref.shape, F32)

        kb, vb, scores = _swa_scores(n, q_ref, kvc_ref, kvp_ref, pcc_ref, pcp_ref, pr_ref)
        lane = lax.broadcasted_iota(jnp.int32, (1, LANES), 1)
        dsink = jnp.zeros((1, LANES), F32)
        dkv = [[None, None], [None, None]]
        dqs = []
        gsl = lambda h: slice(SWA_DH * (h // (SWA_H // SWA_KV)), SWA_DH * (h // (SWA_H // SWA_KV) + 1))
        qs, ss, dobs, dps = [], [], [], []
        for h in range(SWA_H):
            qh, s = scores(h)
            dob = do_ref[:, SWA_DH * h:SWA_DH * (h + 1)].astype(BF16)
            qs.append(qh)
            ss.append(s)
            dobs.append(dob)
            dps.append(_nt(vb[:, gsl(h)], dob))
        pbs, dss = [], []
        for h in range(SWA_H):
            lse, dsum = lse_ref[h:h + 1, :], dsum_ref[h:h + 1, :]
            p = jnp.exp2(ss[h] - lse)
            pbs.append(p.astype(BF16))
            dss.append((p * (dps[h] - dsum) * SWA_SCALE).astype(BF16))
            dsk = jnp.sum(-jnp.exp2(sink_ref[layer, h] * LOG2E - lse) * dsum, axis=1, keepdims=True)
            dsink = dsink + jnp.where(lane == h, dsk, 0.0)
        for h in range(SWA_H):
            g = h // (SWA_H // SWA_KV)
            dqs.append(_tn(kb[:, gsl(h)], dss[h]))
            dk, dv = _nn(dss[h], qs[h]), _nn(pbs[h], dobs[h])
            dkv[g][0] = dk if dkv[g][0] is None else dkv[g][0] + dk
            dkv[g][1] = dv if dkv[g][1] is None else dkv[g][1] + dv
        dq_ref[...] = jnp.concatenate(dqs, axis=0).T.astype(BF16)
        dsink_ref[...] += dsink
        upd = jnp.concatenate([dkv[0][0], dkv[1][0], dkv[0][1], dkv[1][1]], axis=1)
        dkv_ref[pl.ds(pl.multiple_of(n * BLK, BLK), BLK), :] += upd[BLK:]

        @pl.when(n > 0)
        def _():
            dkv_ref[pl.ds(pl.multiple_of((n - 1) * BLK, BLK), BLK), :] += upd[:BLK]

    r_in, r_out, r_shape, r_scratch, r_args = _ride_args(ride)
    return pl.pallas_call(
        _riding(ride, body, 10, 3, (nb, nblk)), grid=(nb, nblk), name="swa_bwd_ride" if ride else "swa_bwd",
        in_specs=_swa_specs(nblk) + [pl.BlockSpec((BLK, 512), lambda b, n: (b * nblk + n, 0))]
        + [pl.BlockSpec((SWA_H, BLK), lambda b, n: (0, b * nblk + n))] * 2 + r_in,
        out_specs=[pl.BlockSpec((BLK, 512), lambda b, n: (b * nblk + n, 0)),
                   pl.BlockSpec((seq, 2 * BLK), lambda b, n: (b, 0)),
                   pl.BlockSpec((1, LANES), lambda b, n: (0, 0))] + r_out,
        out_shape=[SDS((T, 512), BF16), SDS((T, 2 * BLK), F32), SDS((1, LANES), F32)] + r_shape,
        scratch_shapes=r_scratch,
        compiler_params=_cp("arbitrary", "arbitrary"))(sink, z, z, z, pos_col, pos_col, pos_row, do, lse, dsum,
                                                       *r_args)


def _bwd_prep(dq, dk, dv, z, gq, gkv, wq, wkv, tc, ts1, ts2, tm, layer):
    T = z.shape[0]

    def body(dq_ref, dk_ref, dv_ref, qd_ref, kvd_ref, gq_ref, gkv_ref, wq_ref, wkv_ref, c_ref, s1_ref, s2_ref,
             dqd_ref, dkvd_ref, dkr_ref, dwq_ref, dwkv_ref, dgq_ref, dgkv_ref, dqb_s, dkvb_s):
        @pl.when(pl.program_id(0) == 0)
        def _():
            for ref in (dwq_ref, dwkv_ref, dgq_ref, dgkv_ref):
                ref[...] = jnp.zeros(ref.shape, F32)

        c, s1, s2 = c_ref[...], s1_ref[...], s2_ref[...]
        lane = lax.broadcasted_iota(jnp.int32, (1, LANES), 1)
        rope_lanes = (lane >= MLA_NOPE) & (lane < MLA_QK)
        dkb = jnp.zeros((tm, LANES), F32)
        for h in range(MLA_H):
            sl = slice(LANES * h, LANES * (h + 1))
            dqb_s[:, sl] = _rope_t(dq_ref[:, sl], c, s1, s2).astype(BF16)
            dkh = dk_ref[:, sl]
            dkb = dkb + dkh
            dkvb_s[:, sl] = dkh.astype(BF16)
        dkvb_s[:, QFW:] = dv_ref[...].astype(BF16)
        dkr_ref[...] = _rope_t(jnp.where(rope_lanes, dkb, 0.0), c, s1, s2).astype(BF16)

        for (x_ref, g_ref, w_ref, d_s, dx_ref, dw_ref, dg_ref) in (
                (qd_ref, gq_ref, wq_ref, dqb_s, dqd_ref, dwq_ref, dgq_ref),
                (kvd_ref, gkv_ref, wkv_ref, dkvb_s, dkvd_ref, dwkv_ref, dgkv_ref)):
            xf, gf, db = x_ref[...], g_ref[...], d_s[...]
            r = _rstd(xf)
            n = xf * r
            dw_ref[...] += _tn((n * gf).astype(BF16), db)
            dx, dgr = _norm_bwd(_nt(db, w_ref[...]), n, r, gf)
            dx_ref[...] = dx.astype(BF16)
            dg_ref[...] += jnp.sum(dgr, axis=0, keepdims=True)

    return pl.pallas_call(
        body, grid=(T // tm,), name="bwd_prep",
        in_specs=[_row(tm, QFW), _row(tm, QFW), _row(tm, MLA_H * MLA_V),
                  _row(tm, QL, Z_QD // QL), _row(tm, KVL, Z_KVD // KVL),
                  _res((1, QL), layer), _res((1, KVL), layer), _res((QL, QFW), 0), _res((KVL, KVW), 0),
                  _row(tm, LANES), _row(tm, LANES), _row(tm, LANES)],
        out_specs=[_row(tm, QL), _row(tm, KVL), _row(tm, LANES),
                   _acc((QL, QFW)), _acc((KVL, KVW)), _acc((1, QL)), _acc((1, KVL))],
        out_shape=[SDS((T, QL), BF16), SDS((T, KVL), BF16), SDS((T, LANES), BF16),
                   SDS((QL, QFW), F32), SDS((KVL, KVW), F32), SDS((1, QL), F32), SDS((1, KVL), F32)],
        scratch_shapes=[pltpu.VMEM((tm, QFW), BF16), pltpu.VMEM((tm, KVW), BF16)],
        compiler_params=_cp("arbitrary"))(dq, dk, dv, z, z, gq, gkv, wq, wkv, tc, ts1, ts2)


def _bwd_in(pieces, x, g, dres, w, tm, layer):
    T = x.shape[0]
    grid = (T // tm,)
    widths = [pc.shape[1] for pc in pieces]
    assert sum(widths) == ZW
    n_p = len(pieces)

    def body(*refs):
        p_refs, (x_ref, g_ref, r_ref, w_ref, dx_ref, dz_ref, dg_ref) = refs[:n_p], refs[n_p:]

        @pl.when(pl.program_id(0) == 0)
        def _():
            dg_ref[...] = jnp.zeros(dg_ref.shape, F32)

        off = 0
        for ref, wd in zip(p_refs, widths):
            dz_ref[:, off:off + wd] = ref[...].astype(BF16)
            off += wd
        xf, gf = x_ref[...], g_ref[...]
        r = _rstd(xf)
        n = xf * r
        dx, dgr = _norm_bwd(_nt(dz_ref[...], w_ref[...]), n, r, gf)
        dx_ref[...] = r_ref[...] + dx
        dg_ref[...] += jnp.sum(dgr, axis=0, keepdims=True)

    return pl.pallas_call(
        body, grid=grid, name="bwd_in",
        in_specs=[_row(tm, wd) for wd in widths] + [_row(tm, D), _res((1, D), layer), _row(tm, D),
                                                    _res((D, ZW), 0)],
        out_specs=[_row(tm, D), _row(tm, ZW), _acc((1, D))],
        out_shape=[SDS((T, D), F32), SDS((T, ZW), BF16), SDS((1, D), F32)],
        compiler_params=_cp("arbitrary"))(*pieces, x, g, dres, w)


def _wgrad_in(hb, dzb, tm, ride=None):
    T = hb.shape[0]
    half = ZW // 2
    grid = (2, T // tm)

    def body(h_ref, dz_ref, dw_ref):
        @pl.when(pl.program_id(1) == 0)
        def _():
            dw_ref[...] = jnp.zeros(dw_ref.shape, F32)

        dw_ref[...] += _tn(h_ref[...], dz_ref[...])

    r_in, r_out, r_shape, r_scratch, r_args = _ride_args(ride)
    out = pl.pallas_call(
        _riding(ride, body, 2, 1, grid), grid=grid, name="wgrad_in_ride" if ride else "wgrad_in",
        in_specs=[pl.BlockSpec((tm, D), lambda j, t: (t, 0)), pl.BlockSpec((tm, half), lambda j, t: (t, j))] + r_in,
        out_specs=[pl.BlockSpec((D, half), lambda j, t: (0, j))] + r_out,
        out_shape=[SDS((D, ZW), F32)] + r_shape, scratch_shapes=r_scratch,
        compiler_params=_cp("arbitrary", "arbitrary"))(hb, dzb, *r_args)
    return out


IN_PIECES = ((0, 512, Z_AQ), (512, 128, Z_AK), (640, 128, Z_AV), (768, 512, Z_AG), (1280, 256, Z_QD),
             (1536, 128, Z_KVD), (1664, MLA_ROPE, Z_KR + MLA_NOPE), (1696, 512, Z_BG), (2208, 1024, Z_MA),
             (3232, 1024, Z_MB))
WIDE_W = IN_W // N_DEV


def _column_runs():
    runs = []
    for start, width, kstart in IN_PIECES:
        col = start
        while col < start + width:
            dev = col // WIDE_W
            stop = min(start + width, (dev + 1) * WIDE_W)
            runs.append((dev, col - dev * WIDE_W, stop - col, kstart + col - start))
            col = stop
    return runs


def _win_layout(blocks, tm):
    runs = _column_runs()

    def body(g_ref, o_ref):
        o_ref[:, Z_KR:Z_KR + LANES] = jnp.zeros((tm, LANES), o_ref.dtype)
        for dev, lo, n, k in runs:
            o_ref[:, k:k + n] = g_ref[dev, :, lo:lo + n]

    return pl.pallas_call(
        body, grid=(D // tm,), name="win_layout",
        in_specs=[pl.BlockSpec((N_DEV, None, tm, WIDE_W), lambda i: (0, 0, i, 0))],
        out_specs=pl.BlockSpec((None, tm, ZW), lambda i: (0, i, 0)),
        out_shape=SDS((1, D, ZW), blocks.dtype),
        compiler_params=_cp("parallel"))(blocks)


def _win_grad_layout(dw, tm):
    runs = _column_runs()

    def body(g_ref, o_ref):
        for dev, lo, n, k in runs:
            o_ref[dev, :, lo:lo + n] = g_ref[:, k:k + n]

    return pl.pallas_call(
        body, grid=(D // tm,), name="win_grad_layout",
        in_specs=[_row(tm, ZW)],
        out_specs=pl.BlockSpec((N_DEV, None, tm, WIDE_W), lambda i: (0, 0, i, 0)),
        out_shape=SDS((N_DEV, 1, D, WIDE_W), F32),
        compiler_params=_cp("parallel"))(dw)


def _wuq_to_kernel(w):
    w = w.reshape(w.shape[:-1] + (MLA_H, MLA_QK))
    w = jnp.pad(w, [(0, 0)] * (w.ndim - 1) + [(0, LANES - MLA_QK)])
    return w.reshape(w.shape[:-2] + (QFW,))


def _wuq_from_kernel(g):
    g = g.reshape(g.shape[:-1] + (MLA_H, LANES))[..., :MLA_QK]
    return g.reshape(g.shape[:-2] + (MLA_H * MLA_QK,))


def _wukv_to_kernel(w):
    w = w.reshape(w.shape[:-1] + (MLA_H, MLA_NOPE + MLA_V))
    k = jnp.pad(w[..., :MLA_NOPE], [(0, 0)] * (w.ndim - 1) + [(0, LANES - MLA_NOPE)])
    v = w[..., MLA_NOPE:]
    return jnp.concatenate([k.reshape(k.shape[:-2] + (QFW,)), v.reshape(v.shape[:-2] + (MLA_H * MLA_V,))], axis=-1)


def _wukv_from_kernel(g):
    k = g[..., :QFW].reshape(g.shape[:-1] + (MLA_H, LANES))[..., :MLA_NOPE]
    v = g[..., QFW:].reshape(g.shape[:-1] + (MLA_H, MLA_V))
    kv = jnp.concatenate([k, v], axis=-1)
    return kv.reshape(kv.shape[:-2] + (MLA_H * (MLA_NOPE + MLA_V),))


def _rope_tables(pos):
    half = MLA_ROPE // 2
    inv = 10000.0 ** (-jnp.arange(0, MLA_ROPE, 2, dtype=F32) / MLA_ROPE)
    ang = pos.astype(F32)[:, None] * inv
    cos, sin = jnp.cos(ang), jnp.sin(ang)
    one = jnp.ones((pos.shape[0], MLA_NOPE), F32)
    zero = lambda n: jnp.zeros((pos.shape[0], n), F32)
    tc = jnp.concatenate([one, cos, cos, one[:, :LANES - MLA_QK]], axis=1)
    ts1 = jnp.concatenate([zero(MLA_NOPE + half), sin, zero(LANES - MLA_QK)], axis=1)
    ts2 = jnp.concatenate([zero(MLA_NOPE), -sin, zero(LANES - MLA_NOPE - half)], axis=1)
    return tc, ts1, ts2


def _local_step(x, p, positions, loss_target, small, wts, plan=None):
    nb, seq, _ = x.shape
    T = nb * seq
    tm = min(512, T)
    tl = min(1024, T)
    tq = min(512, seq)
    xf = x.reshape(T, D)
    pos = positions.reshape(T)
    posf = pos.astype(F32)
    pos_col, pos_row = posf.reshape(T, 1), posf.reshape(T // BLK, 1, BLK)
    tc, ts1, ts2 = _rope_tables(pos)

    wts, sm = list(wts), small
    pl_in = p.reshape(DEPTH, T, PLE)
    saved = []
    for i in range(DEPTH):
        riding = plan is not None and i == 0
        w = wts[i]
        z, hb, *got = _fwd_in(xf, sm["g_mix"], w["w_in"], tm, i,
                              ride=_Ride("gather", plan["behind_fwd_in"]) if riding else None)
        if riding:
            w = wts[0] = dict(w, **plan["row_weights"](got[0]))
            wide1 = got[1]
        oa, lse_a = _swa_fwd(sm["sink"], z, pos_col, pos_row, nb, seq, i)
        qf, kf, v = _fwd_prep(z, sm["g_q"], sm["g_kv"], w["w_uq"], w["w_ukv"], tc, ts1, ts2, tl, i)
        ob, lse_b, *got = _mla_fwd(qf, kf, v, nb, seq, tq, gather=plan["behind_mla_fwd"] if riding else ())
        if riding:
            wts.append(dict(w_in=plan["w_in"](wide1), **plan["row_weights"](got[0])))
        x1 = _fwd_merge(xf, oa, ob, z, w["w_br_a"], w["w_br_b"], w["w_out"], tm, i)
        saved.append(dict(x=xf, z=z, hb=hb, oa=oa, lse_a=lse_a, qf=qf, kf=kf, v=v, ob=ob, lse_b=lse_b, x1=x1))
        if i < DEPTH - 1:
            xf, saved[i]["pg"], saved[i]["pp"] = _fwd_ple(x1, pl_in, sm["g_ple"], w["w_ple_gate"], w["w_ple_proj"],
                                                          tl, i)

    last = _ple_loss(x1, pl_in, sm["g_ple"], w["w_ple_gate"], w["w_ple_proj"], small["g_final"],
                     loss_target.reshape(T, D), tm, DEPTH - 1)
    dg_final, loss = last[4], last[5]

    grads = [None] * DEPTH
    exchanged = {}
    for i in reversed(range(DEPTH)):
        riding = plan is not None and i == 0
        sv, w = saved[i], wts[i]
        pay = plan["payload"](grads[1]) if riding else []
        if i == DEPTH - 1:
            (dx1, dwpg, dwpp, dg_ple), got = last[:4], []
        else:
            dx1, dwpg, dwpp, dg_ple, *got = _bwd_ple(dx, sv["x1"], sv["pg"], sv["pp"], pl_in, sm["g_ple"],
                                                     w["w_ple_gate"], tm, i,
                                                     ride=_Ride("swap", pay) if riding else None)
        doa, dob, dag, dbg, dma, dmb, dsum_a, dsum_b, dwa, dwb, dwo = _bwd_merge(
            dx1, sv["oa"], sv["ob"], sv["z"], w["w_br_a"], w["w_br_b"], w["w_out"], tm, i)
        stats = (MLA_H // MLA_BWD_HEADS, MLA_BWD_HEADS, T)
        dq_b, dk_b, dv_b, *exchanged["layer1"] = _mla_bwd(
            sv["qf"], sv["kf"], sv["v"], dob, sv["lse_b"].reshape(stats), dsum_b.reshape(stats), nb, seq, tq,
            exchange=plan["add"](pay, got) if riding else ())
        dqd, dkvd, dkr, dwq, dwkv, dgq, dgkv = _bwd_prep(dq_b, dk_b, dv_b, sv["z"], sm["g_q"], sm["g_kv"],
                                                         w["w_uq"], w["w_ukv"], tc, ts1, ts2, tl, i)
        g = dict(w_uq=_wuq_from_kernel(dwq), w_ukv=_wukv_from_kernel(dwkv), w_br_a=dwa, w_br_b=dwb, w_out=dwo,
                 w_ple_gate=dwpg, w_ple_proj=dwpp)
        pay = [plan["rows_payload"](g)] if riding else []
        dq_a, dkv_a, dsink, *got = _swa_bwd(sm["sink"], sv["z"], pos_col, pos_row, doa, sv["lse_a"], dsum_a, nb, seq,
                                            i, ride=_Ride("swap", pay) if riding else None)
        dx, dzb, dg_mix = _bwd_in([dma, dmb, dq_a, dag, dbg, dqd, dkv_a, dkvd, dkr], sv["x"], sm["g_mix"], dx1,
                                  w["w_in"], tm, i)
        dwin, *exchanged["rows0"] = _wgrad_in(sv["hb"], dzb, tm,
                                              ride=_Ride("exchange", plan["add"](pay, got)) if riding else None)
        g.update(g_mix=dg_mix[0], w_in=dwin, sink=dsink[0, :SWA_H], g_q=dgq[0], g_kv=dgkv[0], g_ple=dg_ple[0])
        grads[i] = g
    return loss, dx.reshape(nb, seq, D), grads, dg_final[0], exchanged


def _row_weights(rows):
    blocks = _unpack_rows(rows)
    out = {n: _join(n, blocks[n]) for n, _ in ROWS_PIECES}
    out.update(w_uq=_wuq_to_kernel(out["w_uq"]), w_ukv=_wukv_to_kernel(out["w_ukv"]))
    return out


def _small_params(g_mix, sink, g_q, g_kv, g_ple, g_final):
    return dict(g_mix=g_mix[:, None], sink=sink, g_q=g_q[:, None], g_kv=g_kv[:, None], g_ple=g_ple[:, None],
                g_final=g_final[None])


UQ_W = MLA_H * MLA_QK // N_DEV
ROWS_PIECES = (("w_uq", QL), ("w_ukv", KVL), ("w_br_a", 512), ("w_br_b", 512), ("w_out", D), ("w_ple_gate", D),
               ("w_ple_proj", PLE))
SMALL = (("g_mix", (DEPTH, D)), ("sink", (DEPTH, SWA_H)), ("g_q", (DEPTH, QL)), ("g_kv", (DEPTH, KVL)),
         ("g_ple", (DEPTH, D)), ("g_final", (D,)))
VEC_ROWS = 48
ROWS_N = sum(r for _, r in ROWS_PIECES)
WIDE_TILE, ROWS_TILE = 256, ROWS_N // 2


def _to_rows(name, a):
    if name == "w_uq":
        a = jnp.pad(a, [(0, 0)] * (a.ndim - 1) + [(0, LANES - UQ_W)])
    return a.reshape(a.shape[:-2] + (-1, LANES))


def _from_rows(name, r):
    if name in ("w_out", "w_ple_gate"):
        return r.reshape(r.shape[:-2] + (D // N_DEV, D))
    return r[..., :UQ_W] if name == "w_uq" else r


def _pack_rows(blocks):
    return jnp.concatenate([_to_rows(n, blocks[n]) for n, _ in ROWS_PIECES], axis=-2)


def _unpack_rows(rows):
    blocks, off = {}, 0
    for n, r in ROWS_PIECES:
        blocks[n] = _from_rows(n, rows[..., off:off + r, :])
        off += r
    return blocks


def _pack_vec(vectors, loss=None):
    parts = [vectors[n].reshape(-1) for n, _ in SMALL] + ([] if loss is None else [loss.reshape(1)])
    vec = jnp.concatenate(parts)
    return jnp.pad(vec, (0, VEC_ROWS * LANES - vec.shape[0])).reshape(1, VEC_ROWS, LANES)


def _unpack_vec(vec):
    vec = vec.reshape(-1)
    vectors, off = {}, 0
    for n, shp in SMALL:
        size = 1
        for s in shp:
            size *= s
        vectors[n] = vec[off:off + size].reshape(shp)
        off += size
    return vectors, vec[off]


def _join(name, blocks):
    if name in ("w_out", "w_ple_gate"):
        return jnp.moveaxis(blocks, 0, 1).reshape(blocks.shape[1], -1, blocks.shape[-1])
    return jnp.moveaxis(blocks, 0, 2).reshape(blocks.shape[1], blocks.shape[2], -1)


def _split(name, full):
    if name in ("w_out", "w_ple_gate"):
        return jnp.moveaxis(full.reshape(full.shape[0], N_DEV, -1, full.shape[-1]), 1, 0)
    return jnp.moveaxis(full.reshape(full.shape[0], full.shape[1], N_DEV, -1), 2, 0)


MESH_ID = pl.DeviceIdType.MESH
ANY = pl.BlockSpec(memory_space=pl.ANY)


def _place():
    return lax.axis_index("x"), lax.axis_index("y"), lax.axis_index("c")


def _all_gather(blocks):
    n = len(blocks)

    def body(*refs):
        start, forward, finish = _gather_phases(refs[:n], refs[n:2 * n], *refs[2 * n:])
        start()
        forward()
        finish()

    return pl.pallas_call(
        body, name="all_gather_weights", out_shape=_gather_out(blocks),
        in_specs=[ANY] * n, out_specs=[ANY] * n, scratch_shapes=_gather_sems(n))(*blocks)


def _gather_out(blocks):
    return [SDS((N_DEV,) + b.shape, b.dtype) for b in blocks]


def _gather_sems(n):
    return [pltpu.SemaphoreType.DMA((7 * n,)), pltpu.SemaphoreType.DMA((7 * n,)), pltpu.SemaphoreType.DMA((n,))]


def _gather_phases(x_refs, out_refs, send_sems, recv_sems, local_sems):
    n = len(x_refs)
    x, y, c = _place()
    me, sibling = (x, y, c), (x, y, 1 - c)
    chips = [(1 - x, y), (x, 1 - y), (1 - x, 1 - y)]

    def slot(a, px, py, pc):
        return out_refs[a].at[4 * px + 2 * py + pc]

    def copy(a, k, blk, to, src=None):
        return pltpu.make_async_remote_copy(
            src_ref=slot(a, *blk) if src is None else src, dst_ref=slot(a, *blk),
            send_sem=send_sems.at[7 * a + k], recv_sem=recv_sems.at[7 * a + k], device_id=to,
            device_id_type=MESH_ID)

    def mine():
        return [pltpu.make_async_copy(x_refs[a], slot(a, *me), local_sems.at[a]) for a in range(n)]

    def first():
        out = []
        for a in range(n):
            out += [copy(a, 0, me, sibling, src=x_refs[a])]
            out += [copy(a, 1 + j, me, (*chip, c), src=x_refs[a]) for j, chip in enumerate(chips)]
        return out

    def passed():
        return [copy(a, 4 + j, (*chip, c), sibling) for j, chip in enumerate(chips) for a in range(n)]

    def start():
        for cp in mine() + first():
            cp.start()

    def forward():
        for j, chip in enumerate(chips):
            for a in range(n):
                copy(a, 1 + j, (*chip, c), me).wait_recv()
                copy(a, 4 + j, (*chip, c), sibling).start()

    def finish():
        for a in range(n):
            copy(a, 0, sibling, me).wait_recv()
            for j, chip in enumerate(chips):
                copy(a, 4 + j, (*chip, 1 - c), me).wait_recv()
        for cp in first() + passed():
            cp.wait_send()
        for cp in mine():
            cp.wait()

    return start, forward, finish


def _swap_sibling(arrs):
    n = len(arrs)

    def body(*refs):
        start, finish = _swap_phases(refs[:n], refs[n:2 * n], *refs[2 * n:])
        start()
        finish()

    return pl.pallas_call(
        body, name="swap_sibling", out_shape=[SDS((a.shape[0],) + a.shape[2:], a.dtype) for a in arrs],
        in_specs=[ANY] * n, out_specs=[ANY] * n, scratch_shapes=_swap_sems(n))(*arrs)


def _swap_sems(n):
    return [pltpu.SemaphoreType.DMA((n,)), pltpu.SemaphoreType.DMA((n,))]


def _swap_phases(a_refs, out_refs, send_sems, recv_sems):
    x, y, c = _place()

    def copies():
        return [pltpu.make_async_remote_copy(
            src_ref=a_refs[a].at[:, 1 - c], dst_ref=out_refs[a], send_sem=send_sems.at[a], recv_sem=recv_sems.at[a],
            device_id=(x, y, 1 - c), device_id_type=MESH_ID) for a in range(len(a_refs))]

    def start():
        for cp in copies():
            cp.start()

    def finish():
        for cp in copies():
            cp.wait()

    return start, finish


def _exchange_chips(arrs):
    n = len(arrs)

    def body(*refs):
        start, finish = _exchange_phases(refs[:n], refs[n:2 * n], *refs[2 * n:])
        start()
        finish()

    return pl.pallas_call(
        body, name="exchange_chips", out_shape=[SDS(a.shape, a.dtype) for a in arrs],
        in_specs=[ANY] * n, out_specs=[ANY] * n, scratch_shapes=_exchange_sems(n))(*arrs)


def _exchange_sems(n):
    return [pltpu.SemaphoreType.DMA((3 * n,)), pltpu.SemaphoreType.DMA((3 * n,)), pltpu.SemaphoreType.DMA((n,))]


def _exchange_phases(p_refs, out_refs, send_sems, recv_sems, local_sems):
    n = len(p_refs)
    x, y, c = _place()
    mine = 2 * x + y
    peers = [(1 - x, y), (x, 1 - y), (1 - x, 1 - y)]

    def local():
        return [pltpu.make_async_copy(p_refs[a].at[mine], out_refs[a].at[mine], local_sems.at[a]) for a in range(n)]

    def copy(a, j, src_chip, dst_chip):
        px, py = peers[j]
        return pltpu.make_async_remote_copy(
            src_ref=p_refs[a].at[src_chip], dst_ref=out_refs[a].at[dst_chip], send_sem=send_sems.at[3 * a + j],
            recv_sem=recv_sems.at[3 * a + j], device_id=(px, py, c), device_id_type=MESH_ID)

    def sends():
        return [copy(a, j, 2 * px + py, mine) for a in range(n) for j, (px, py) in enumerate(peers)]

    def start():
        for cp in local() + sends():
            cp.start()

    def finish():
        for a in range(n):
            for j, (px, py) in enumerate(peers):
                copy(a, j, mine, 2 * px + py).wait_recv()
        for cp in sends():
            cp.wait_send()
        for cp in local():
            cp.wait()

    return start, finish


def _add_mine(g, recv, core, tile, dtype):
    _, _, lead, rows, width = g.shape

    def body(c_ref, g_ref, r_ref, o_ref):
        o_ref[...] = (g_ref[...] + r_ref[...]).astype(dtype)

    spec = pl.BlockSpec((None, None, tile, width), lambda k, l, i, c_ref: (k, l, i, 0))
    return pl.pallas_call(
        body, name="add_sibling", out_shape=SDS(recv.shape, dtype),
        grid_spec=pltpu.PrefetchScalarGridSpec(
            num_scalar_prefetch=1, grid=(g.shape[0], lead, rows // tile),
            in_specs=[pl.BlockSpec((None, None, None, tile, width), lambda k, l, i, c_ref: (k, c_ref[0], l, i, 0)),
                      spec],
            out_specs=spec),
        compiler_params=_cp("parallel", "parallel", "parallel"))(core, g, recv)


def _sum_adamw(parts, w, m, v, tile):
    lead, rows, width = w.shape
    last = rows // tile - 1

    def body(*refs):
        p_refs, (w_ref, m_ref, v_ref, g_ref, d_ref, nm_ref, nv_ref) = refs[:lead], refs[lead:]
        for layer in range(lead):
            @pl.when(pl.program_id(0) == layer)
            def _(p_ref=p_refs[layer]):
                g = ((p_ref[0].astype(F32) + p_ref[1].astype(F32)) + p_ref[2].astype(F32)) + p_ref[3].astype(F32)
                nm = ADAM_B1 * m_ref[...] + (1.0 - ADAM_B1) * g
                nv = ADAM_B2 * v_ref[...] + (1.0 - ADAM_B2) * jnp.square(g)
                m_hat = nm / (1.0 - ADAM_B1 ** ADAM_STEP)
                v_hat = nv / (1.0 - ADAM_B2 ** ADAM_STEP)
                g_ref[...] = g
                nm_ref[...] = nm
                nv_ref[...] = nv
                d_ref[...] = -ADAM_LR * (m_hat / (jnp.sqrt(v_hat) + ADAM_EPS) + ADAM_WD * w_ref[...])

    pspec = lambda layer: pl.BlockSpec(
        (4, None, tile, width),
        lambda l, i: (0, 0, jnp.where(l == layer, i, jnp.where(l > layer, last, 0)), 0))
    spec = pl.BlockSpec((None, tile, width), lambda l, i: (l, i, 0))
    return pl.pallas_call(
        body, grid=(lead, rows // tile), name="sum_adamw",
        in_specs=[pspec(layer) for layer in range(lead)] + [spec, spec, spec],
        out_specs=[spec] * 4, out_shape=[SDS((lead, rows, width), F32)] * 4,
        compiler_params=_cp("arbitrary", "arbitrary"))(*parts, w, m, v)


def kernel(x, p, positions, g_mix, w_in, sink, g_q, w_uq, g_kv, w_ukv, w_br_a, w_br_b, w_out, g_ple, w_ple_gate, w_ple_proj, g_final, loss_target, m_g_mix, m_w_in, m_sink, m_g_q, m_w_uq, m_g_kv, m_w_ukv, m_w_br_a, m_w_br_b, m_w_out, m_g_ple, m_w_ple_gate, m_w_ple_proj, m_g_final, v_g_mix, v_w_in, v_sink, v_g_q, v_w_uq, v_g_kv, v_w_ukv, v_w_br_a, v_w_br_b, v_w_out, v_g_ple, v_w_ple_gate, v_w_ple_proj, v_g_final):
    weights = dict(g_mix=g_mix, w_in=w_in, sink=sink, g_q=g_q, w_uq=w_uq, g_kv=g_kv, w_ukv=w_ukv, w_br_a=w_br_a,
                   w_br_b=w_br_b, w_out=w_out, g_ple=g_ple, w_ple_gate=w_ple_gate, w_ple_proj=w_ple_proj,
                   g_final=g_final)
    mom1 = dict(g_mix=m_g_mix, w_in=m_w_in, sink=m_sink, g_q=m_g_q, w_uq=m_w_uq, g_kv=m_g_kv, w_ukv=m_w_ukv,
                w_br_a=m_w_br_a, w_br_b=m_w_br_b, w_out=m_w_out, g_ple=m_g_ple, w_ple_gate=m_w_ple_gate,
                w_ple_proj=m_w_ple_proj, g_final=m_g_final)
    mom2 = dict(g_mix=v_g_mix, w_in=v_w_in, sink=v_sink, g_q=v_g_q, w_uq=v_w_uq, g_kv=v_g_kv, w_ukv=v_w_ukv,
                w_br_a=v_w_br_a, w_br_b=v_w_br_b, w_out=v_w_out, g_ple=v_g_ple, w_ple_gate=v_w_ple_gate,
                w_ple_proj=v_w_ple_proj, g_final=v_g_final)
    assert DEPTH == 2
    wide = lambda d: d["w_in"]
    rows = lambda d: _pack_rows(d)
    core = lax.axis_index("c").astype(jnp.int32).reshape(1)

    w16 = [wide(weights).astype(BF16), rows(weights).astype(BF16)]
    wts0 = dict(w_in=_win_layout(_all_gather([w16[0][:1]])[0], 256))
    small = _small_params(g_mix, sink, g_q, g_kv, g_ple, g_final)

    def wide_payload(g):
        return _win_grad_layout(g["w_in"], 256).reshape(N_DEV // 2, 2, 1, D, WIDE_W)

    def rows_payload(g):
        return _pack_rows({n: _split(n, g[n][None]) for n, _ in ROWS_PIECES}).reshape(N_DEV // 2, 2, 1, ROWS_N, LANES)

    def add(pay, got):
        tiles = {D: (WIDE_TILE, BF16), ROWS_N: (ROWS_TILE, BF16), VEC_ROWS: (VEC_ROWS, F32)}
        return [_add_mine(a, b, core, *tiles[a.shape[-2]]) for a, b in zip(pay, got)]

    plan = dict(behind_fwd_in=[w16[1][:1], w16[0][1:]], behind_mla_fwd=[w16[1][1:]],
                w_in=lambda blocks: _win_layout(blocks, 256),
                row_weights=_row_weights, payload=lambda g: [wide_payload(g), rows_payload(g)],
                rows_payload=rows_payload, add=add)
    loss, grad_x, grads, dg_final, rode = _local_step(x, p, positions, loss_target, small, [wts0], plan)

    vectors = {n: jnp.stack([grads[i][n] for i in range(DEPTH)]) for n, _ in SMALL[:-1]}
    vectors["g_final"] = dg_final
    pay = [wide_payload(grads[0]),
           jnp.broadcast_to(_pack_vec(vectors, loss[0, 0]), (N_DEV // 2, 2, 1, VEC_ROWS, LANES))]
    parts_wide0, parts_vec = _exchange_chips(add(pay, _swap_sibling(pay)))
    out_wide = _sum_adamw([parts_wide0, rode["layer1"][0]], wide(weights), wide(mom1), wide(mom2), WIDE_TILE)
    out_rows = _sum_adamw([rode["rows0"][0], rode["layer1"][1]], rows(weights), rows(mom1), rows(mom2), ROWS_TILE)
    out_vec = _sum_adamw([parts_vec], _pack_vec(weights), _pack_vec(mom1), _pack_vec(mom2), VEC_ROWS)

    outs = []
    for ow, orow, ovec in zip(out_wide, out_rows, out_vec):
        named = _unpack_rows(orow)
        named.update(_unpack_vec(ovec)[0])
        named["w_in"] = ow
        outs += [named[n] for n in weights]
    loss = _unpack_vec(out_vec[0])[1]
    return (loss, grad_x, *outs)
```

```python
import functools

import jax
import jax.numpy as jnp
from jax import lax
from jax.experimental import pallas as pl
from jax.experimental.pallas import tpu as pltpu

F32, BF16 = jnp.float32, jnp.bfloat16
SDS = jax.ShapeDtypeStruct

D = 1024
DEPTH = 2
PLE = 256
BLK = 128
EPS = 1e-6
NEG = -1e30
SWA_H, SWA_KV, SWA_DH = 8, 2, 64
MLA_H, MLA_NOPE, MLA_ROPE, MLA_V = 8, 64, 32, 64
MLA_QK = MLA_NOPE + MLA_ROPE
QL, KVL = 256, 128
IN_W = 4256
N_DEV = 8

V7X_VMEM_BYTES = 64 * 1024 * 1024
LANES = 128
VMEM_LIMIT = V7X_VMEM_BYTES * 7 // 8

ZW = 4352
Z_MA, Z_MB, Z_AQ, Z_AG, Z_BG, Z_QD, Z_AK, Z_AV, Z_KVD, Z_KR = 0, 1024, 2048, 2560, 3072, 3584, 3840, 3968, 4096, 4224
QFW = MLA_H * LANES
KVW = QFW + MLA_H * MLA_V
MLA_SCALE = MLA_QK ** -0.5
LOG2E = 1.4426950408889634
MLA_FWD_HEADS, MLA_BWD_HEADS = 8, 4
SWA_SCALE = SWA_DH ** -0.5
ROLL_UP, ROLL_DOWN = MLA_ROPE // 2, LANES - MLA_ROPE // 2

ADAM_LR, ADAM_B1, ADAM_B2, ADAM_EPS, ADAM_WD, ADAM_STEP = 0.001, 0.9, 0.999, 1e-08, 0.01, 10


def _cp(*sem):
    return pltpu.CompilerParams(dimension_semantics=sem, vmem_limit_bytes=VMEM_LIMIT)


def _row(tm, w, col=0):
    return pl.BlockSpec((tm, w), lambda i: (i, col))


def _res(shape, layer=None):
    if layer is None:
        return pl.BlockSpec(shape, lambda *_: (0,) * len(shape), pipeline_mode=pl.Buffered(1))
    return pl.BlockSpec((None,) + shape, lambda *_: (layer,) + (0,) * len(shape), pipeline_mode=pl.Buffered(1))


def _acc(shape):
    return pl.BlockSpec(shape, lambda *_: (0,) * len(shape))


def _rstd(xf):
    return lax.rsqrt(jnp.mean(xf * xf, axis=-1, keepdims=True) + EPS)


def _norm_bwd(dh, n, r, g):
    dn = dh * g
    return r * (dn - n * jnp.mean(dn * n, axis=-1, keepdims=True)), dh * n


def _nt(a, b):
    return lax.dot_general(a, b, (((1,), (1,)), ((), ())), preferred_element_type=F32)


def _tn(a, b):
    return lax.dot_general(a, b, (((0,), (0,)), ((), ())), preferred_element_type=F32)


def _nn(a, b):
    return jnp.dot(a, b, preferred_element_type=F32)


def _sig(x):
    return jax.nn.sigmoid(x)


def _rope(t, c, s1, s2):
    return t * c + pltpu.roll(t, ROLL_UP, 1) * s1 + pltpu.roll(t, ROLL_DOWN, 1) * s2


def _rope_t(d, c, s1, s2):
    return d * c + pltpu.roll(d * s1, ROLL_DOWN, 1) + pltpu.roll(d * s2, ROLL_UP, 1)


def _fwd_in(x, g, w, tm, layer, ride=None):
    T = x.shape[0]
    grid = (T // tm,)

    def body(x_ref, g_ref, w_ref, z_ref, h_ref):
        xf = x_ref[...]
        h = ((xf * _rstd(xf)) * g_ref[...]).astype(BF16)
        h_ref[...] = h
        z_ref[...] = _nn(h, w_ref[...])

    r_in, r_out, r_shape, r_scratch, r_args = _ride_args(ride)
    return pl.pallas_call(
        _riding(ride, body, 3, 2, grid), grid=grid, name="fwd_in_ride" if ride else "fwd_in",
        in_specs=[_row(tm, D), _res((1, D), layer), _res((D, ZW), 0)] + r_in,
        out_specs=[_row(tm, ZW), _row(tm, D)] + r_out,
        out_shape=[SDS((T, ZW), F32), SDS((T, D), BF16)] + r_shape, scratch_shapes=r_scratch,
        compiler_params=_cp("arbitrary"))(x, g, w, *r_args)


def _fwd_prep(z, gq, gkv, wq, wkv, tc, ts1, ts2, tm, layer):
    T = z.shape[0]

    def body(qd_ref, kvd_ref, kr_ref, gq_ref, gkv_ref, wq_ref, wkv_ref, c_ref, s1_ref, s2_ref, q_ref, k_ref, v_ref):
        qd, kvd = qd_ref[...], kvd_ref[...]
        hq = ((qd * _rstd(qd)) * gq_ref[...]).astype(BF16)
        hkv = ((kvd * _rstd(kvd)) * gkv_ref[...]).astype(BF16)
        qf = _nn(hq, wq_ref[...])
        kvf = _nn(hkv, wkv_ref[...])
        c, s1, s2 = c_ref[...], s1_ref[...], s2_ref[...]
        krb = _rope(kr_ref[...], c, s1, s2)
        for h in range(MLA_H):
            sl = slice(LANES * h, LANES * (h + 1))
            q_ref[:, sl] = _rope(qf[:, sl], c, s1, s2).astype(BF16)
            k_ref[:, sl] = (kvf[:, sl] + krb).astype(BF16)
        v_ref[...] = kvf[:, QFW:].astype(BF16)

    return pl.pallas_call(
        body, grid=(T // tm,), name="fwd_prep",
        in_specs=[_row(tm, QL, Z_QD // QL), _row(tm, KVL, Z_KVD // KVL), _row(tm, LANES, Z_KR // LANES),
                  _res((1, QL), layer), _res((1, KVL), layer), _res((QL, QFW), 0), _res((KVL, KVW), 0),
                  _row(tm, LANES), _row(tm, LANES), _row(tm, LANES)],
        out_specs=[_row(tm, QFW), _row(tm, QFW), _row(tm, MLA_H * MLA_V)],
        out_shape=[SDS((T, QFW), BF16), SDS((T, QFW), BF16), SDS((T, MLA_H * MLA_V), BF16)],
        compiler_params=_cp("parallel"))(z, z, z, gq, gkv, wq, wkv, tc, ts1, ts2)


def _grid_ends(grid):
    ids = [pl.program_id(a) for a in range(len(grid))]
    inner_first = functools.reduce(jnp.logical_and, [i == 0 for i in ids[1:]], True)
    last = functools.reduce(jnp.logical_and, [i == g - 1 for i, g in zip(ids, grid)])
    return (ids[0] == 0) & inner_first, (ids[0] == 3 * grid[0] // 4) & inner_first, last


class _Ride:
    def __init__(self, kind, arrays):
        self.kind, self.arrays, self.n = kind, list(arrays), len(arrays)

    def out_shape(self):
        if self.kind == "gather":
            return _gather_out(self.arrays)
        if self.kind == "swap":
            return [SDS((a.shape[0],) + a.shape[2:], a.dtype) for a in self.arrays]
        return [SDS(a.shape, a.dtype) for a in self.arrays]

    def sems(self):
        if self.kind == "gather":
            return _gather_sems(self.n)
        if self.kind == "swap":
            return _swap_sems(self.n)
        return _exchange_sems(self.n)

    def phases(self, in_refs, out_refs, *sems):
        if self.kind == "gather":
            return _gather_phases(in_refs, out_refs, *sems)
        start, finish = (_swap_phases if self.kind == "swap" else _exchange_phases)(in_refs, out_refs, *sems)
        return start, None, finish


def _riding(ride, body, n_in, n_out, grid):
    if ride is None:
        return body
    n, n_sems = ride.n, len(ride.sems())

    def wrapped(*refs):
        ins, r_in = refs[:n_in], refs[n_in:n_in + n]
        outs, r_out = refs[n_in + n:n_in + n + n_out], refs[n_in + n + n_out:n_in + 2 * n + n_out]
        rest = refs[n_in + 2 * n + n_out:]
        scratch, sems = rest[:len(rest) - n_sems], rest[len(rest) - n_sems:]
        start, middle, finish = ride.phases(r_in, r_out, *sems)
        at_first, at_middle, at_last = _grid_ends(grid)
        pl.when(at_first)(start)
        if middle is not None:
            pl.when(at_middle)(middle)
        body(*ins, *outs, *scratch)
        pl.when(at_last)(finish)

    return wrapped


def _ride_args(ride):
    if ride is None:
        return [], [], [], [], []
    return [ANY] * ride.n, [ANY] * ride.n, ride.out_shape(), ride.sems(), ride.arrays


def _mla_fwd(qf, kf, v, nb, seq, tq, gather=()):
    T = qf.shape[0]
    nq = seq // tq
    hp = MLA_FWD_HEADS
    pw = hp * LANES
    pairs = [(qi, ki) for qi in range(nq) for ki in range(qi + 1)]
    qi_tab = jnp.array([qk[0] for qk in pairs], jnp.int32)
    ki_tab = jnp.array([qk[1] for qk in pairs], jnp.int32)
    grid = (nb, MLA_H // hp, len(pairs))
    n_g = len(gather)

    def body(qi_ref, ki_ref, q_ref, k_ref, v_ref, *rest):
        x_refs, (o_ref, lse_ref), got_refs = rest[:n_g], rest[n_g:n_g + 2], rest[n_g + 2:2 * n_g + 2]
        (m_s, l_s, acc_s), sems = rest[2 * n_g + 2:2 * n_g + 5], rest[2 * n_g + 5:]
        qi, ki = qi_ref[pl.program_id(2)], ki_ref[pl.program_id(2)]
        if n_g:
            start, forward, finish = _gather_phases(x_refs, got_refs, *sems)
            at_first, at_middle, at_last = _grid_ends(grid)
            pl.when(at_first)(start)
            pl.when(at_middle)(forward)

        @pl.when(ki == 0)
        def _():
            m_s[...] = jnp.full(m_s.shape, NEG, F32)
            l_s[...] = jnp.zeros(l_s.shape, F32)
            acc_s[...] = jnp.zeros(acc_s.shape, F32)

        def step(masked):
            parts = [(0, tq // 2, tq // 2), (tq // 2, tq, tq)] if masked else [(0, tq, tq)]
            work = [(j, a, b, kh) for j in range(hp) for a, b, kh in parts]
            ss = []
            for j, a, b, kh in work:
                wide = slice(LANES * j, LANES * (j + 1))
                s = _nt(k_ref[:kh, wide], q_ref[a:b, wide]) * (MLA_SCALE * LOG2E)
                if masked:
                    keys = lax.broadcasted_iota(jnp.int32, (kh, b - a), 0)
                    queries = a + lax.broadcasted_iota(jnp.int32, (kh, b - a), 1)
                    s = jnp.where(keys <= queries, s, NEG)
                ss.append(s)
            ps, alphas = [], []
            for (j, a, b, kh), s in zip(work, ss):
                m_prev = m_s[j, :, a:b]
                m_new = jnp.maximum(m_prev, jnp.max(s, axis=0, keepdims=True))
                alpha = jnp.exp2(m_prev - m_new)
                p = jnp.exp2(s - m_new)
                l_s[j, :, a:b] = alpha * l_s[j, :, a:b] + jnp.sum(p, axis=0, keepdims=True)
                m_s[j, :, a:b] = m_new
                ps.append(p.astype(BF16))
                alphas.append(alpha)
            for (j, a, b, kh), p, alpha in zip(work, ps, alphas):
                rows = slice(MLA_V * j, MLA_V * (j + 1))
                acc_s[rows, a:b] = alpha * acc_s[rows, a:b] + _tn(v_ref[:kh, rows], p)

        @pl.when(ki < qi)
        def _():
            step(False)

        @pl.when(ki == qi)
        def _():
            step(True)
            for j in range(hp):
                rows = slice(MLA_V * j, MLA_V * (j + 1))
                acc_s[rows, :] = acc_s[rows, :] / l_s[j]
                lse_ref[j:j + 1, :] = m_s[j] + jnp.log2(l_s[j])
            o_ref[...] = acc_s[...].T

        if n_g:
            pl.when(at_last)(finish)

    q_map = lambda b, g, s, qi_ref, ki_ref: (b * nq + qi_ref[s], g)
    kv_map = lambda b, g, s, qi_ref, ki_ref: (b * nq + ki_ref[s], g)
    return pl.pallas_call(
        body, name="mla_fwd_gather" if n_g else "mla_fwd",
        grid_spec=pltpu.PrefetchScalarGridSpec(
            num_scalar_prefetch=2, grid=grid,
            in_specs=[pl.BlockSpec((tq, pw), q_map), pl.BlockSpec((tq, pw), kv_map),
                      pl.BlockSpec((tq, hp * MLA_V), kv_map)] + [ANY] * n_g,
            out_specs=[pl.BlockSpec((tq, hp * MLA_V), q_map),
                       pl.BlockSpec((hp, tq), lambda b, g, s, qi_ref, ki_ref: (g, b * nq + qi_ref[s]))]
            + [ANY] * n_g,
            scratch_shapes=[pltpu.VMEM((hp, 1, tq), F32), pltpu.VMEM((hp, 1, tq), F32),
                            pltpu.VMEM((hp * MLA_V, tq), F32)]
            + (_gather_sems(n_g) if n_g else [])),
        out_shape=[SDS((T, MLA_H * MLA_V), F32), SDS((MLA_H, T), F32)] + _gather_out(gather),
        compiler_params=_cp("arbitrary", "arbitrary", "arbitrary"))(qi_tab, ki_tab, qf, kf, v, *gather)


def _swa_specs(nblk):
    cur = lambda b, n: (b * nblk + n, 0)
    prev = lambda b, n: (b * nblk + jnp.maximum(n - 1, 0), 0)
    kvc = Z_AK // (2 * BLK)
    return [pl.BlockSpec(memory_space=pltpu.SMEM),
            pl.BlockSpec((BLK, 512), lambda b, n: (b * nblk + n, Z_AQ // 512)),
            pl.BlockSpec((BLK, 2 * BLK), lambda b, n: (b * nblk + n, kvc)),
            pl.BlockSpec((BLK, 2 * BLK), lambda b, n: (b * nblk + jnp.maximum(n - 1, 0), kvc)),
            pl.BlockSpec((BLK, 1), cur),
            pl.BlockSpec((BLK, 1), prev),
            pl.BlockSpec((1, 1, BLK), lambda b, n: (b * nblk + n, 0, 0))]


def _swa_scores(n, q_ref, kvc_ref, kvp_ref, pcc_ref, pcp_ref, pr_ref):
    kv = jnp.concatenate([kvp_ref[...], kvc_ref[...]], axis=0)
    kb, vb = kv[:, :BLK].astype(BF16), kv[:, BLK:].astype(BF16)
    dist = pr_ref[0] - jnp.concatenate([pcp_ref[...], pcc_ref[...]], axis=0)
    key = lax.broadcasted_iota(jnp.int32, (2 * BLK, BLK), 0)
    qry = lax.broadcasted_iota(jnp.int32, (2 * BLK, BLK), 1)
    valid = (key > qry) & (key <= qry + BLK) & ((key >= BLK) | (n > 0))

    def scores(h):
        g = h // (SWA_H // SWA_KV)
        qh = q_ref[:, SWA_DH * h:SWA_DH * (h + 1)].astype(BF16)
        s = _nt(kb[:, SWA_DH * g:SWA_DH * (g + 1)], qh) * (SWA_SCALE * LOG2E) - (2.0 ** -(h + 1) * LOG2E) * dist
        return qh, jnp.where(valid, s, NEG)

    return kb, vb, scores


def _swa_fwd(sink, z, pos_col, pos_row, nb, seq, layer, ride=None):
    T = z.shape[0]
    nblk = seq // BLK

    def body(sink_ref, q_ref, kvc_ref, kvp_ref, pcc_ref, pcp_ref, pr_ref, o_ref, lse_ref):
        kb, vb, scores = _swa_scores(pl.program_id(1), q_ref, kvc_ref, kvp_ref, pcc_ref, pcp_ref, pr_ref)
        ss = [scores(h)[1] for h in range(SWA_H)]
        es, dens = [], []
        for h in range(SWA_H):
            sk = sink_ref[layer, h] * LOG2E
            m = jnp.maximum(jnp.max(ss[h], axis=0, keepdims=True), sk)
            e = jnp.exp2(ss[h] - m)
            den = jnp.sum(e, axis=0, keepdims=True) + jnp.exp2(sk - m)
            lse_ref[h:h + 1, :] = m + jnp.log2(den)
            es.append(e.astype(BF16))
            dens.append(den)
        outs = []
        for h in range(SWA_H):
            g = h // (SWA_H // SWA_KV)
            outs.append(_tn(vb[:, SWA_DH * g:SWA_DH * (g + 1)], es[h]) / dens[h])
        o_ref[...] = jnp.concatenate(outs, axis=0).T

    r_in, r_out, r_shape, r_scratch, r_args = _ride_args(ride)
    return pl.pallas_call(
        _riding(ride, body, 7, 2, (nb, nblk)), grid=(nb, nblk), name="swa_fwd_ride" if ride else "swa_fwd",
        in_specs=_swa_specs(nblk) + r_in,
        out_specs=[pl.BlockSpec((BLK, 512), lambda b, n: (b * nblk + n, 0)),
                   pl.BlockSpec((SWA_H, BLK), lambda b, n: (0, b * nblk + n))] + r_out,
        out_shape=[SDS((T, 512), F32), SDS((SWA_H, T), F32)] + r_shape, scratch_shapes=r_scratch,
        compiler_params=_cp("arbitrary", "arbitrary"))(sink, z, z, z, pos_col, pos_col, pos_row, *r_args)


def _fwd_merge(x, oa, ob, z, wa, wb, wo, tm, layer):
    T = x.shape[0]

    def body(x_ref, oa_ref, ob_ref, ag_ref, bg_ref, ma_ref, mb_ref, wa_ref, wb_ref, wo_ref, x1_ref):
        ag, bg = ag_ref[...], bg_ref[...]
        ua = _nn((oa_ref[...] * (ag * _sig(ag))).astype(BF16), wa_ref[...])
        ub = _nn((ob_ref[...] * (bg * _sig(bg))).astype(BF16), wb_ref[...])
        y = _sig(ma_ref[...]) * ua + _sig(mb_ref[...]) * ub
        x1_ref[...] = x_ref[...] + _nn(y.astype(BF16), wo_ref[...])

    return pl.pallas_call(
        body, grid=(T // tm,), name="fwd_merge",
        in_specs=[_row(tm, D), _row(tm, 512), _row(tm, 512), _row(tm, 512, Z_AG // 512), _row(tm, 512, Z_BG // 512),
                  _row(tm, D, Z_MA // D), _row(tm, D, Z_MB // D),
                  _res((512, D), 0), _res((512, D), 0), _res((D, D), 0)],
        out_specs=_row(tm, D),
        out_shape=SDS((T, D), F32),
        compiler_params=_cp("parallel"))(x, oa, ob, z, z, z, z, wa, wb, wo)


def _fwd_ple(x1, p, g, wpg, wpp, tm, layer):
    T = x1.shape[0]

    def body(x_ref, p_ref, g_ref, wpg_ref, wpp_ref, x2_ref, pg_ref, pp_ref):
        xf = x_ref[...]
        h1 = ((xf * _rstd(xf)) * g_ref[...]).astype(BF16)
        pg = _sig(_nn(h1, wpg_ref[...]))
        pp = _nn(p_ref[...].astype(BF16), wpp_ref[...])
        pg_ref[...] = pg
        pp_ref[...] = pp
        x2_ref[...] = xf + pg * pp

    return pl.pallas_call(
        body, grid=(T // tm,), name="fwd_ple",
        in_specs=[_row(tm, D), pl.BlockSpec((None, tm, PLE), lambda i: (layer, i, 0)),
                  _res((1, D), layer), _res((D, D), 0), _res((PLE, D), 0)],
        out_specs=[_row(tm, D)] * 3,
        out_shape=[SDS((T, D), F32)] * 3,
        compiler_params=_cp("parallel"))(x1, p, g, wpg, wpp)


def _ple_loss(x1, p, g, wpg, wpp, g_final, tgt, tm, layer):
    T = x1.shape[0]

    def body(x_ref, p_ref, g_ref, wpg_ref, wpp_ref, gf_ref, t_ref, dx_ref, dwg_ref, dwp_ref, dg_ref, dgf_ref, loss_ref):
        @pl.when(pl.program_id(0) == 0)
        def _():
            for ref in (dwg_ref, dwp_ref, dg_ref, dgf_ref, loss_ref):
                ref[...] = jnp.zeros(ref.shape, F32)

        xf, gp, gf = x_ref[...], g_ref[...], gf_ref[...]
        r = _rstd(xf)
        n = xf * r
        h1 = (n * gp).astype(BF16)
        pb = p_ref[...].astype(BF16)
        pg = _sig(_nn(h1, wpg_ref[...]))
        pp = _nn(pb, wpp_ref[...])
        x2 = xf + pg * pp
        r2 = _rstd(x2)
        n2 = x2 * r2
        err = n2 * gf - t_ref[...]
        loss_ref[...] += 0.5 * jnp.sum(jnp.mean(err * err, axis=-1, keepdims=True), axis=0, keepdims=True)
        d, dgfr = _norm_bwd(err * (1.0 / D), n2, r2, gf)
        dgf_ref[...] += jnp.sum(dgfr, axis=0, keepdims=True)
        dpgl = (d * pp * pg * (1.0 - pg)).astype(BF16)
        dwg_ref[...] += _tn(h1, dpgl)
        dwp_ref[...] += _tn(pb, (d * pg).astype(BF16))
        dxn, dgr = _norm_bwd(_nt(dpgl, wpg_ref[...]), n, r, gp)
        dx_ref[...] = d + dxn
        dg_ref[...] += jnp.sum(dgr, axis=0, keepdims=True)

    return pl.pallas_call(
        body, grid=(T // tm,), name="ple_loss",
        in_specs=[_row(tm, D), pl.BlockSpec((None, tm, PLE), lambda i: (layer, i, 0)), _res((1, D), layer),
                  _res((D, D), 0), _res((PLE, D), 0), _res((1, D)), _row(tm, D)],
        out_specs=[_row(tm, D), _acc((D, D)), _acc((PLE, D)), _acc((1, D)), _acc((1, D)), _acc((1, LANES))],
        out_shape=[SDS((T, D), F32), SDS((D, D), F32), SDS((PLE, D), F32), SDS((1, D), F32), SDS((1, D), F32),
                   SDS((1, LANES), F32)],
        compiler_params=_cp("arbitrary"))(x1, p, g, wpg, wpp, g_final, tgt)


def _bwd_ple(dx2, x1, pg, pp, p, g, wpg, tm, layer, ride=None):
    T = x1.shape[0]
    grid = (T // tm,)

    def body(d_ref, x_ref, pg_ref, pp_ref, p_ref, g_ref, w_ref, dx_ref, dwg_ref, dwp_ref, dg_ref):
        @pl.when(pl.program_id(0) == 0)
        def _():
            dwg_ref[...] = jnp.zeros(dwg_ref.shape, F32)
            dwp_ref[...] = jnp.zeros(dwp_ref.shape, F32)
            dg_ref[...] = jnp.zeros(dg_ref.shape, F32)

        d, xf, pg, gf = d_ref[...], x_ref[...], pg_ref[...], g_ref[...]
        r = _rstd(xf)
        n = xf * r
        dpgl = (d * pp_ref[...] * pg * (1.0 - pg)).astype(BF16)
        dwg_ref[...] += _tn((n * gf).astype(BF16), dpgl)
        dwp_ref[...] += _tn(p_ref[...].astype(BF16), (d * pg).astype(BF16))
        dxn, dgr = _norm_bwd(_nt(dpgl, w_ref[...]), n, r, gf)
        dx_ref[...] = d + dxn
        dg_ref[...] += jnp.sum(dgr, axis=0, keepdims=True)

    r_in, r_out, r_shape, r_scratch, r_args = _ride_args(ride)
    return pl.pallas_call(
        _riding(ride, body, 7, 4, grid), grid=grid, name="bwd_ple_ride" if ride else "bwd_ple",
        in_specs=[_row(tm, D)] * 4 + [pl.BlockSpec((None, tm, PLE), lambda i: (layer, i, 0)),
                                      _res((1, D), layer), _res((D, D), 0)] + r_in,
        out_specs=[_row(tm, D), _acc((D, D)), _acc((PLE, D)), _acc((1, D))] + r_out,
        out_shape=[SDS((T, D), F32), SDS((D, D), F32), SDS((PLE, D), F32), SDS((1, D), F32)] + r_shape,
        scratch_shapes=r_scratch,
        compiler_params=_cp("arbitrary"))(dx2, x1, pg, pp, p, g, wpg, *r_args)


def _bwd_merge(dx1, oa, ob, z, wa, wb, wo, tm, layer):
    T = dx1.shape[0]

    def body(d_ref, oa_ref, ob_ref, ag_ref, bg_ref, ma_ref, mb_ref, wa_ref, wb_ref, wo_ref,
             doa_ref, dob_ref, dag_ref, dbg_ref, dma_ref, dmb_ref, dsa_ref, dsb_ref, dwa_ref, dwb_ref, dwo_ref):
        @pl.when(pl.program_id(0) == 0)
        def _():
            dwa_ref[...] = jnp.zeros(dwa_ref.shape, F32)
            dwb_ref[...] = jnp.zeros(dwb_ref.shape, F32)
            dwo_ref[...] = jnp.zeros(dwo_ref.shape, F32)

        db = d_ref[...].astype(BF16)
        gated = []
        for o_ref, gate_ref, w_ref in ((oa_ref, ag_ref, wa_ref), (ob_ref, bg_ref, wb_ref)):
            raw, gate = o_ref[...], gate_ref[...]
            sg = _sig(gate)
            silu = gate * sg
            ob16 = (raw * silu).astype(BF16)
            gated.append((raw, gate, sg, silu, ob16, _nn(ob16, w_ref[...])))
        ua, ub = gated[0][5], gated[1][5]
        sa, sb = _sig(ma_ref[...]), _sig(mb_ref[...])
        dwo_ref[...] += _tn((sa * ua + sb * ub).astype(BF16), db)
        dy = _nt(db, wo_ref[...])
        dma_ref[...] = (dy * ua * sa * (1.0 - sa)).astype(BF16)
        dmb_ref[...] = (dy * ub * sb * (1.0 - sb)).astype(BF16)
        for (s, w_ref, do_ref, dgate_ref, dw_ref, ds_ref), (raw, gate, sg, silu, ob16, _) in zip((
                (sa, wa_ref, doa_ref, dag_ref, dwa_ref, dsa_ref),
                (sb, wb_ref, dob_ref, dbg_ref, dwb_ref, dsb_ref)), gated):
            du = (dy * s).astype(BF16)
            dw_ref[...] += _tn(ob16, du)
            do = _nt(du, w_ref[...])
            draw = do * silu
            do_ref[...] = draw.astype(BF16)
            dgate_ref[...] = (do * raw * (sg * (1.0 + gate * (1.0 - sg)))).astype(BF16)
            ds_ref[...] = jnp.sum((draw * raw).T.reshape(MLA_H, MLA_V, tm), axis=1)

    return pl.pallas_call(
        body, grid=(T // tm,), name="bwd_merge",
        in_specs=[_row(tm, D), _row(tm, 512), _row(tm, 512), _row(tm, 512, Z_AG // 512), _row(tm, 512, Z_BG // 512),
                  _row(tm, D, Z_MA // D), _row(tm, D, Z_MB // D),
                  _res((512, D), 0), _res((512, D), 0), _res((D, D), 0)],
        out_specs=[_row(tm, 512)] * 4 + [_row(tm, D)] * 2 + [pl.BlockSpec((MLA_H, tm), lambda i: (0, i))] * 2
        + [_acc((512, D)), _acc((512, D)), _acc((D, D))],
        out_shape=[SDS((T, 512), BF16), SDS((T, 512), BF16), SDS((T, 512), BF16), SDS((T, 512), BF16),
                   SDS((T, D), BF16), SDS((T, D), BF16), SDS((MLA_H, T), F32), SDS((MLA_H, T), F32),
                   SDS((512, D), F32), SDS((512, D), F32), SDS((D, D), F32)],
        compiler_params=_cp("arbitrary"))(dx1, oa, ob, z, z, z, z, wa, wb, wo)


def _mla_bwd(qf, kf, v, do, lse, dsum, nb, seq, tq, exchange=()):
    T = qf.shape[0]
    nq = seq // tq
    hp = MLA_BWD_HEADS
    pw = hp * LANES
    pairs = [(qi, ki) for ki in range(nq) for qi in range(ki, nq)]
    qi_tab = jnp.array([qk[0] for qk in pairs], jnp.int32)
    ki_tab = jnp.array([qk[1] for qk in pairs], jnp.int32)
    grid = (nb, MLA_H // hp, len(pairs))
    n_x = len(exchange)

    def body(qi_ref, ki_ref, q_ref, k_ref, v_ref, do_ref, lse_ref, dsum_ref, *rest):
        p_refs, (dq_ref, dk_ref, dv_ref), got_refs = rest[:n_x], rest[n_x:n_x + 3], rest[n_x + 3:2 * n_x + 3]
        (dk_s, dv_s, dqt_s), sems = rest[2 * n_x + 3:2 * n_x + 6], rest[2 * n_x + 6:]
        step_id = pl.program_id(2)
        qi, ki = qi_ref[step_id], ki_ref[step_id]
        if n_x:
            start, finish = _exchange_phases(p_refs, got_refs, *sems)
            at_first, _, at_last = _grid_ends(grid)
            pl.when(at_first)(start)

        @pl.when(step_id == 0)
        def _():
            dqt_s[...] = jnp.zeros(dqt_s.shape, F32)

        @pl.when(qi == ki)
        def _():
            dk_s[...] = jnp.zeros(dk_s.shape, F32)
            dv_s[...] = jnp.zeros(dv_s.shape, F32)

        def step(masked):
            if masked:
                keys = lax.broadcasted_iota(jnp.int32, (tq, tq), 0)
                queries = lax.broadcasted_iota(jnp.int32, (tq, tq), 1)
                mask = keys <= queries
            for j in range(hp):
                wide = slice(LANES * j, LANES * (j + 1))
                sl = slice(MLA_V * j, MLA_V * (j + 1))
                q, k = q_ref[:, wide], k_ref[:, wide]
                dob = do_ref[:, sl].astype(BF16)
                s = _nt(k, q) * (MLA_SCALE * LOG2E)
                if masked:
                    s = jnp.where(mask, s, NEG)
                p = jnp.exp2(s - lse_ref[j:j + 1, :])
                dv_s[:, sl] += _nn(p.astype(BF16), dob)
                ds = (p * (_nt(v_ref[:, sl], dob) - dsum_ref[j:j + 1, :]) * MLA_SCALE).astype(BF16)
                dk_s[:, wide] += _nn(ds, q)
                dqt_s[qi, wide, :] += _tn(k, ds)

        @pl.when(qi > ki)
        def _():
            step(False)

        @pl.when(qi == ki)
        def _():
            step(True)

        @pl.when(qi == nq - 1)
        def _():
            dk_ref[...] = dk_s[...]
            dv_ref[...] = dv_s[...]

        @pl.when(step_id == len(pairs) - 1)
        def _():
            for n in range(nq):
                dq_ref[tq * n:tq * (n + 1), :] = dqt_s[n].T

        if n_x:
            pl.when(at_last)(finish)

    qmap = lambda b, g, s, qi_ref, ki_ref: (b * nq + qi_ref[s], g)
    kmap = lambda b, g, s, qi_ref, ki_ref: (b * nq + ki_ref[s], g)
    stat = pl.BlockSpec((None, hp, tq), lambda b, g, s, qi_ref, ki_ref: (g, 0, b * nq + qi_ref[s]))
    vw = hp * MLA_V
    return pl.pallas_call(
        body, name="mla_bwd_exchange" if n_x else "mla_bwd",
        grid_spec=pltpu.PrefetchScalarGridSpec(
            num_scalar_prefetch=2, grid=grid,
            in_specs=[pl.BlockSpec((tq, pw), qmap), pl.BlockSpec((tq, pw), kmap), pl.BlockSpec((tq, vw), kmap),
                      pl.BlockSpec((tq, vw), qmap), stat, stat] + [ANY] * n_x,
            out_specs=[pl.BlockSpec((seq, pw), lambda b, g, s, qi_ref, ki_ref: (b, g)),
                       pl.BlockSpec((tq, pw), kmap), pl.BlockSpec((tq, vw), kmap)] + [ANY] * n_x,
            scratch_shapes=[pltpu.VMEM((tq, pw), F32), pltpu.VMEM((tq, vw), F32), pltpu.VMEM((nq, pw, tq), F32)]
            + (_exchange_sems(n_x) if n_x else [])),
        out_shape=[SDS((T, QFW), F32), SDS((T, QFW), F32), SDS((T, MLA_H * MLA_V), F32)]
        + [SDS(a.shape, a.dtype) for a in exchange],
        compiler_params=_cp("arbitrary", "arbitrary", "arbitrary"))(qi_tab, ki_tab, qf, kf, v, do, lse, dsum, *exchange)


def _swa_bwd(sink, z, pos_col, pos_row, do, lse, dsum, nb, seq, layer, ride=None):
    T = z.shape[0]
    nblk = seq // BLK

    def body(sink_ref, q_ref, kvc_ref, kvp_ref, pcc_ref, pcp_ref, pr_ref, do_ref, lse_ref, dsum_ref,
             dq_ref, dkv_ref, dsink_ref):
        b, n = pl.program_id(0), pl.program_id(1)

        @pl.when((b == 0) & (n == 0))
        def _():
            dsink_ref[...] = jnp.zeros(dsink_ref.shape, F32)

        @pl.when(n == 0)
        def _():
            dkv_ref[...] = jnp.zeros(dkv_ref.shape, F32)

        kb, vb, scores = _swa_scores(n, q_ref, kvc_ref, kvp_ref, pcc_ref, pcp_ref, pr_ref)
        lane = lax.broadcasted_iota(jnp.int32, (1, LANES), 1)
        dsink = jnp.zeros((1, LANES), F32)
        dkv = [[None, None], [None, None]]
        dqs = []
        gsl = lambda h: slice(SWA_DH * (h // (SWA_H // SWA_KV)), SWA_DH * (h // (SWA_H // SWA_KV) + 1))
        qs, ss, dobs, dps = [], [], [], []
        for h in range(SWA_H):
            qh, s = scores(h)
            dob = do_ref[:, SWA_DH * h:SWA_DH * (h + 1)].astype(BF16)
            qs.append(qh)
            ss.append(s)
            dobs.append(dob)
            dps.append(_nt(vb[:, gsl(h)], dob))
        pbs, dss = [], []
        for h in range(SWA_H):
            lse, dsum = lse_ref[h:h + 1, :], dsum_ref[h:h + 1, :]
            p = jnp.exp2(ss[h] - lse)
            pbs.append(p.astype(BF16))
            dss.append((p * (dps[h] - dsum) * SWA_SCALE).astype(BF16))
            dsk = jnp.sum(-jnp.exp2(sink_ref[layer, h] * LOG2E - lse) * dsum, axis=1, keepdims=True)
            dsink = dsink + jnp.where(lane == h, dsk, 0.0)
        for h in range(SWA_H):
            g = h // (SWA_H // SWA_KV)
            dqs.append(_tn(kb[:, gsl(h)], dss[h]))
            dk, dv = _nn(dss[h], qs[h]), _nn(pbs[h], dobs[h])
            dkv[g][0] = dk if dkv[g][0] is None else dkv[g][0] + dk
            dkv[g][1] = dv if dkv[g][1] is None else dkv[g][1] + dv
        dq_ref[...] = jnp.concatenate(dqs, axis=0).T.astype(BF16)
        dsink_ref[...] += dsink
        upd = jnp.concatenate([dkv[0][0], dkv[1][0], dkv[0][1], dkv[1][1]], axis=1)
        dkv_ref[pl.ds(pl.multiple_of(n * BLK, BLK), BLK), :] += upd[BLK:]

        @pl.when(n > 0)
        def _():
            dkv_ref[pl.ds(pl.multiple_of((n - 1) * BLK, BLK), BLK), :] += upd[:BLK]

    r_in, r_out, r_shape, r_scratch, r_args = _ride_args(ride)
    return pl.pallas_call(
        _riding(ride, body, 10, 3, (nb, nblk)), grid=(nb, nblk), name="swa_bwd_ride" if ride else "swa_bwd",
        in_specs=_swa_specs(nblk) + [pl.BlockSpec((BLK, 512), lambda b, n: (b * nblk + n, 0))]
        + [pl.BlockSpec((SWA_H, BLK), lambda b, n: (0, b * nblk + n))] * 2 + r_in,
        out_specs=[pl.BlockSpec((BLK, 512), lambda b, n: (b * nblk + n, 0)),
                   pl.BlockSpec((seq, 2 * BLK), lambda b, n: (b, 0)),
                   pl.BlockSpec((1, LANES), lambda b, n: (0, 0))] + r_out,
        out_shape=[SDS((T, 512), BF16), SDS((T, 2 * BLK), F32), SDS((1, LANES), F32)] + r_shape,
        scratch_shapes=r_scratch,
        compiler_params=_cp("arbitrary", "arbitrary"))(sink, z, z, z, pos_col, pos_col, pos_row, do, lse, dsum,
                                                       *r_args)


def _bwd_prep(dq, dk, dv, z, gq, gkv, wq, wkv, tc, ts1, ts2, tm, layer):
    T = z.shape[0]

    def body(dq_ref, dk_ref, dv_ref, qd_ref, kvd_ref, gq_ref, gkv_ref, wq_ref, wkv_ref, c_ref, s1_ref, s2_ref,
             dqd_ref, dkvd_ref, dkr_ref, dwq_ref, dwkv_ref, dgq_ref, dgkv_ref, dqb_s, dkvb_s):
        @pl.when(pl.program_id(0) == 0)
        def _():
            for ref in (dwq_ref, dwkv_ref, dgq_ref, dgkv_ref):
                ref[...] = jnp.zeros(ref.shape, F32)

        c, s1, s2 = c_ref[...], s1_ref[...], s2_ref[...]
        lane = lax.broadcasted_iota(jnp.int32, (1, LANES), 1)
        rope_lanes = (lane >= MLA_NOPE) & (lane < MLA_QK)
        dkb = jnp.zeros((tm, LANES), F32)
        for h in range(MLA_H):
            sl = slice(LANES * h, LANES * (h + 1))
            dqb_s[:, sl] = _rope_t(dq_ref[:, sl], c, s1, s2).astype(BF16)
            dkh = dk_ref[:, sl]
            dkb = dkb + dkh
            dkvb_s[:, sl] = dkh.astype(BF16)
        dkvb_s[:, QFW:] = dv_ref[...].astype(BF16)
        dkr_ref[...] = _rope_t(jnp.where(rope_lanes, dkb, 0.0), c, s1, s2).astype(BF16)

        for (x_ref, g_ref, w_ref, d_s, dx_ref, dw_ref, dg_ref) in (
                (qd_ref, gq_ref, wq_ref, dqb_s, dqd_ref, dwq_ref, dgq_ref),
                (kvd_ref, gkv_ref, wkv_ref, dkvb_s, dkvd_ref, dwkv_ref, dgkv_ref)):
            xf, gf, db = x_ref[...], g_ref[...], d_s[...]
            r = _rstd(xf)
            n = xf * r
            dw_ref[...] += _tn((n * gf).astype(BF16), db)
            dx, dgr = _norm_bwd(_nt(db, w_ref[...]), n, r, gf)
            dx_ref[...] = dx.astype(BF16)
            dg_ref[...] += jnp.sum(dgr, axis=0, keepdims=True)

    return pl.pallas_call(
        body, grid=(T // tm,), name="bwd_prep",
        in_specs=[_row(tm, QFW), _row(tm, QFW), _row(tm, MLA_H * MLA_V),
                  _row(tm, QL, Z_QD // QL), _row(tm, KVL, Z_KVD // KVL),
                  _res((1, QL), layer), _res((1, KVL), layer), _res((QL, QFW), 0), _res((KVL, KVW), 0),
                  _row(tm, LANES), _row(tm, LANES), _row(tm, LANES)],
        out_specs=[_row(tm, QL), _row(tm, KVL), _row(tm, LANES),
                   _acc((QL, QFW)), _acc((KVL, KVW)), _acc((1, QL)), _acc((1, KVL))],
        out_shape=[SDS((T, QL), BF16), SDS((T, KVL), BF16), SDS((T, LANES), BF16),
                   SDS((QL, QFW), F32), SDS((KVL, KVW), F32), SDS((1, QL), F32), SDS((1, KVL), F32)],
        scratch_shapes=[pltpu.VMEM((tm, QFW), BF16), pltpu.VMEM((tm, KVW), BF16)],
        compiler_params=_cp("arbitrary"))(dq, dk, dv, z, z, gq, gkv, wq, wkv, tc, ts1, ts2)


def _bwd_in(pieces, x, g, dres, w, tm, layer):
    T = x.shape[0]
    grid = (T // tm,)
    widths = [pc.shape[1] for pc in pieces]
    assert sum(widths) == ZW
    n_p = len(pieces)

    def body(*refs):
        p_refs, (x_ref, g_ref, r_ref, w_ref, dx_ref, dz_ref, dg_ref) = refs[:n_p], refs[n_p:]

        @pl.when(pl.program_id(0) == 0)
        def _():
            dg_ref[...] = jnp.zeros(dg_ref.shape, F32)

        off = 0
        for ref, wd in zip(p_refs, widths):
            dz_ref[:, off:off + wd] = ref[...].astype(BF16)
            off += wd
        xf, gf = x_ref[...], g_ref[...]
        r = _rstd(xf)
        n = xf * r
        dx, dgr = _norm_bwd(_nt(dz_ref[...], w_ref[...]), n, r, gf)
        dx_ref[...] = r_ref[...] + dx
        dg_ref[...] += jnp.sum(dgr, axis=0, keepdims=True)

    return pl.pallas_call(
        body, grid=grid, name="bwd_in",
        in_specs=[_row(tm, wd) for wd in widths] + [_row(tm, D), _res((1, D), layer), _row(tm, D),
                                                    _res((D, ZW), 0)],
        out_specs=[_row(tm, D), _row(tm, ZW), _acc((1, D))],
        out_shape=[SDS((T, D), F32), SDS((T, ZW), BF16), SDS((1, D), F32)],
        compiler_params=_cp("arbitrary"))(*pieces, x, g, dres, w)


def _wgrad_in(hb, dzb, tm, ride=None):
    T = hb.shape[0]
    half = ZW // 2
    grid = (2, T // tm)

    def body(h_ref, dz_ref, dw_ref):
        @pl.when(pl.program_id(1) == 0)
        def _():
            dw_ref[...] = jnp.zeros(dw_ref.shape, F32)

        dw_ref[...] += _tn(h_ref[...], dz_ref[...])

    r_in, r_out, r_shape, r_scratch, r_args = _ride_args(ride)
    out = pl.pallas_call(
        _riding(ride, body, 2, 1, grid), grid=grid, name="wgrad_in_ride" if ride else "wgrad_in",
        in_specs=[pl.BlockSpec((tm, D), lambda j, t: (t, 0)), pl.BlockSpec((tm, half), lambda j, t: (t, j))] + r_in,
        out_specs=[pl.BlockSpec((D, half), lambda j, t: (0, j))] + r_out,
        out_shape=[SDS((D, ZW), F32)] + r_shape, scratch_shapes=r_scratch,
        compiler_params=_cp("arbitrary", "arbitrary"))(hb, dzb, *r_args)
    return out


IN_PIECES = ((0, 512, Z_AQ), (512, 128, Z_AK), (640, 128, Z_AV), (768, 512, Z_AG), (1280, 256, Z_QD),
             (1536, 128, Z_KVD), (1664, MLA_ROPE, Z_KR + MLA_NOPE), (1696, 512, Z_BG), (2208, 1024, Z_MA),
             (3232, 1024, Z_MB))
WIDE_W = IN_W // N_DEV


def _column_runs():
    runs = []
    for start, width, kstart in IN_PIECES:
        col = start
        while col < start + width:
            dev = col // WIDE_W
            stop = min(start + width, (dev + 1) * WIDE_W)
            runs.append((dev, col - dev * WIDE_W, stop - col, kstart + col - start))
            col = stop
    return runs


def _win_layout(blocks, tm):
    runs = _column_runs()

    def body(g_ref, o_ref):
        o_ref[:, Z_KR:Z_KR + LANES] = jnp.zeros((tm, LANES), o_ref.dtype)
        for dev, lo, n, k in runs:
            o_ref[:, k:k + n] = g_ref[dev, :, lo:lo + n]

    return pl.pallas_call(
        body, grid=(D // tm,), name="win_layout",
        in_specs=[pl.BlockSpec((N_DEV, None, tm, WIDE_W), lambda i: (0, 0, i, 0))],
        out_specs=pl.BlockSpec((None, tm, ZW), lambda i: (0, i, 0)),
        out_shape=SDS((1, D, ZW), blocks.dtype),
        compiler_params=_cp("parallel"))(blocks)


def _win_grad_layout(dw, tm):
    runs = _column_runs()

    def body(g_ref, o_ref):
        for dev, lo, n, k in runs:
            o_ref[dev, :, lo:lo + n] = g_ref[:, k:k + n]

    return pl.pallas_call(
        body, grid=(D // tm,), name="win_grad_layout",
        in_specs=[_row(tm, ZW)],
        out_specs=pl.BlockSpec((N_DEV, None, tm, WIDE_W), lambda i: (0, 0, i, 0)),
        out_shape=SDS((N_DEV, 1, D, WIDE_W), F32),
        compiler_params=_cp("parallel"))(dw)


def _wuq_to_kernel(w):
    w = w.reshape(w.shape[:-1] + (MLA_H, MLA_QK))
    w = jnp.pad(w, [(0, 0)] * (w.ndim - 1) + [(0, LANES - MLA_QK)])
    return w.reshape(w.shape[:-2] + (QFW,))


def _wuq_from_kernel(g):
    g = g.reshape(g.shape[:-1] + (MLA_H, LANES))[..., :MLA_QK]
    return g.reshape(g.shape[:-2] + (MLA_H * MLA_QK,))


def _wukv_to_kernel(w):
    w = w.reshape(w.shape[:-1] + (MLA_H, MLA_NOPE + MLA_V))
    k = jnp.pad(w[..., :MLA_NOPE], [(0, 0)] * (w.ndim - 1) + [(0, LANES - MLA_NOPE)])
    v = w[..., MLA_NOPE:]
    return jnp.concatenate([k.reshape(k.shape[:-2] + (QFW,)), v.reshape(v.shape[:-2] + (MLA_H * MLA_V,))], axis=-1)


def _wukv_from_kernel(g):
    k = g[..., :QFW].reshape(g.shape[:-1] + (MLA_H, LANES))[..., :MLA_NOPE]
    v = g[..., QFW:].reshape(g.shape[:-1] + (MLA_H, MLA_V))
    kv = jnp.concatenate([k, v], axis=-1)
    return kv.reshape(kv.shape[:-2] + (MLA_H * (MLA_NOPE + MLA_V),))


def _rope_tables(pos):
    half = MLA_ROPE // 2
    inv = 10000.0 ** (-jnp.arange(0, MLA_ROPE, 2, dtype=F32) / MLA_ROPE)
    ang = pos.astype(F32)[:, None] * inv
    cos, sin = jnp.cos(ang), jnp.sin(ang)
    one = jnp.ones((pos.shape[0], MLA_NOPE), F32)
    zero = lambda n: jnp.zeros((pos.shape[0], n), F32)
    tc = jnp.concatenate([one, cos, cos, one[:, :LANES - MLA_QK]], axis=1)
    ts1 = jnp.concatenate([zero(MLA_NOPE + half), sin, zero(LANES - MLA_QK)], axis=1)
    ts2 = jnp.concatenate([zero(MLA_NOPE), -sin, zero(LANES - MLA_NOPE - half)], axis=1)
    return tc, ts1, ts2


def _local_step(x, p, positions, loss_target, small, wts, plan=None):
    nb, seq, _ = x.shape
    T = nb * seq
    tm = min(512, T)
    tl = min(1024, T)
    tq = min(512, seq)
    xf = x.reshape(T, D)
    pos = positions.reshape(T)
    posf = pos.astype(F32)
    pos_col, pos_row = posf.reshape(T, 1), posf.reshape(T // BLK, 1, BLK)
    tc, ts1, ts2 = _rope_tables(pos)

    wts, sm = list(wts), small
    pl_in = p.reshape(DEPTH, T, PLE)
    saved = []
    for i in range(DEPTH):
        riding = plan is not None and i == 0
        w = wts[i]
        z, hb, *got = _fwd_in(xf, sm["g_mix"], w["w_in"], tm, i,
                              ride=_Ride("gather", plan["behind_fwd_in"]) if riding else None)
        if riding:
            w = wts[0] = dict(w, **plan["row_weights"](got[0]))
        oa, lse_a, *wide1 = _swa_fwd(sm["sink"], z, pos_col, pos_row, nb, seq, i,
                                     ride=_Ride("gather", plan["behind_swa_fwd"]) if riding else None)
        qf, kf, v = _fwd_prep(z, sm["g_q"], sm["g_kv"], w["w_uq"], w["w_ukv"], tc, ts1, ts2, tl, i)
        ob, lse_b, *got = _mla_fwd(qf, kf, v, nb, seq, tq, gather=plan["behind_mla_fwd"] if riding else ())
        if riding:
            wts.append(dict(w_in=plan["w_in"](wide1[0]), **plan["row_weights"](got[0])))
        x1 = _fwd_merge(xf, oa, ob, z, w["w_br_a"], w["w_br_b"], w["w_out"], tm, i)
        saved.append(dict(x=xf, z=z, hb=hb, oa=oa, lse_a=lse_a, qf=qf, kf=kf, v=v, ob=ob, lse_b=lse_b, x1=x1))
        if i < DEPTH - 1:
            xf, saved[i]["pg"], saved[i]["pp"] = _fwd_ple(x1, pl_in, sm["g_ple"], w["w_ple_gate"], w["w_ple_proj"],
                                                          tl, i)

    last = _ple_loss(x1, pl_in, sm["g_ple"], w["w_ple_gate"], w["w_ple_proj"], small["g_final"],
                     loss_target.reshape(T, D), tm, DEPTH - 1)
    dg_final, loss = last[4], last[5]

    grads = [None] * DEPTH
    exchanged = {}
    for i in reversed(range(DEPTH)):
        riding = plan is not None and i == 0
        sv, w = saved[i], wts[i]
        pay = plan["payload"](grads[1]) if riding else []
        if i == DEPTH - 1:
            (dx1, dwpg, dwpp, dg_ple), got = last[:4], []
        else:
            dx1, dwpg, dwpp, dg_ple, *got = _bwd_ple(dx, sv["x1"], sv["pg"], sv["pp"], pl_in, sm["g_ple"],
                                                     w["w_ple_gate"], tm, i,
                                                     ride=_Ride("swap", pay) if riding else None)
        doa, dob, dag, dbg, dma, dmb, dsum_a, dsum_b, dwa, dwb, dwo = _bwd_merge(
            dx1, sv["oa"], sv["ob"], sv["z"], w["w_br_a"], w["w_br_b"], w["w_out"], tm, i)
        stats = (MLA_H // MLA_BWD_HEADS, MLA_BWD_HEADS, T)
        dq_b, dk_b, dv_b, *exchanged["layer1"] = _mla_bwd(
            sv["qf"], sv["kf"], sv["v"], dob, sv["lse_b"].reshape(stats), dsum_b.reshape(stats), nb, seq, tq,
            exchange=plan["add"](pay, got) if riding else ())
        dqd, dkvd, dkr, dwq, dwkv, dgq, dgkv = _bwd_prep(dq_b, dk_b, dv_b, sv["z"], sm["g_q"], sm["g_kv"],
                                                         w["w_uq"], w["w_ukv"], tc, ts1, ts2, tl, i)
        g = dict(w_uq=_wuq_from_kernel(dwq), w_ukv=_wukv_from_kernel(dwkv), w_br_a=dwa, w_br_b=dwb, w_out=dwo,
                 w_ple_gate=dwpg, w_ple_proj=dwpp)
        pay = [plan["rows_payload"](g)] if riding else []
        dq_a, dkv_a, dsink, *got = _swa_bwd(sm["sink"], sv["z"], pos_col, pos_row, doa, sv["lse_a"], dsum_a, nb, seq,
                                            i, ride=_Ride("swap", pay) if riding else None)
        dx, dzb, dg_mix = _bwd_in([dma, dmb, dq_a, dag, dbg, dqd, dkv_a, dkvd, dkr], sv["x"], sm["g_mix"], dx1,
                                  w["w_in"], tm, i)
        dwin, *exchanged["rows0"] = _wgrad_in(sv["hb"], dzb, tm,
                                              ride=_Ride("exchange", plan["add"](pay, got)) if riding else None)
        g.update(g_mix=dg_mix[0], w_in=dwin, sink=dsink[0, :SWA_H], g_q=dgq[0], g_kv=dgkv[0], g_ple=dg_ple[0])
        grads[i] = g
    return loss, dx.reshape(nb, seq, D), grads, dg_final[0], exchanged


def _row_weights(rows):
    blocks = _unpack_rows(rows)
    out = {n: _join(n, blocks[n]) for n, _ in ROWS_PIECES}
    out.update(w_uq=_wuq_to_kernel(out["w_uq"]), w_ukv=_wukv_to_kernel(out["w_ukv"]))
    return out


def _small_params(g_mix, sink, g_q, g_kv, g_ple, g_final):
    return dict(g_mix=g_mix[:, None], sink=sink, g_q=g_q[:, None], g_kv=g_kv[:, None], g_ple=g_ple[:, None],
                g_final=g_final[None])


UQ_W = MLA_H * MLA_QK // N_DEV
ROWS_PIECES = (("w_uq", QL), ("w_ukv", KVL), ("w_br_a", 512), ("w_br_b", 512), ("w_out", D), ("w_ple_gate", D),
               ("w_ple_proj", PLE))
SMALL = (("g_mix", (DEPTH, D)), ("sink", (DEPTH, SWA_H)), ("g_q", (DEPTH, QL)), ("g_kv", (DEPTH, KVL)),
         ("g_ple", (DEPTH, D)), ("g_final", (D,)))
VEC_ROWS = 48
ROWS_N = sum(r for _, r in ROWS_PIECES)
WIDE_TILE, ROWS_TILE = 256, ROWS_N // 2


def _to_rows(name, a):
    if name == "w_uq":
        a = jnp.pad(a, [(0, 0)] * (a.ndim - 1) + [(0, LANES - UQ_W)])
    return a.reshape(a.shape[:-2] + (-1, LANES))


def _from_rows(name, r):
    if name in ("w_out", "w_ple_gate"):
        return r.reshape(r.shape[:-2] + (D // N_DEV, D))
    return r[..., :UQ_W] if name == "w_uq" else r


def _pack_rows(blocks):
    return jnp.concatenate([_to_rows(n, blocks[n]) for n, _ in ROWS_PIECES], axis=-2)


def _unpack_rows(rows):
    blocks, off = {}, 0
    for n, r in ROWS_PIECES:
        blocks[n] = _from_rows(n, rows[..., off:off + r, :])
        off += r
    return blocks


def _pack_vec(vectors, loss=None):
    parts = [vectors[n].reshape(-1) for n, _ in SMALL] + ([] if loss is None else [loss.reshape(1)])
    vec = jnp.concatenate(parts)
    return jnp.pad(vec, (0, VEC_ROWS * LANES - vec.shape[0])).reshape(1, VEC_ROWS, LANES)


def _unpack_vec(vec):
    vec = vec.reshape(-1)
    vectors, off = {}, 0
    for n, shp in SMALL:
        size = 1
        for s in shp:
            size *= s
        vectors[n] = vec[off:off + size].reshape(shp)
        off += size
    return vectors, vec[off]


def _join(name, blocks):
    if name in ("w_out", "w_ple_gate"):
        return jnp.moveaxis(blocks, 0, 1).reshape(blocks.shape[1], -1, blocks.shape[-1])
    return jnp.moveaxis(blocks, 0, 2).reshape(blocks.shape[1], blocks.shape[2], -1)


def _split(name, full):
    if name in ("w_out", "w_ple_gate"):
        return jnp.moveaxis(full.reshape(full.shape[0], N_DEV, -1, full.shape[-1]), 1, 0)
    return jnp.moveaxis(full.reshape(full.shape[0], full.shape[1], N_DEV, -1), 2, 0)


MESH_ID = pl.DeviceIdType.MESH
ANY = pl.BlockSpec(memory_space=pl.ANY)


def _place():
    return lax.axis_index("x"), lax.axis_index("y"), lax.axis_index("c")


def _all_gather(blocks):
    n = len(blocks)

    def body(*refs):
        start, forward, finish = _gather_phases(refs[:n], refs[n:2 * n], *refs[2 * n:])
        start()
        forward()
        finish()

    return pl.pallas_call(
        body, name="all_gather_weights", out_shape=_gather_out(blocks),
        in_specs=[ANY] * n, out_specs=[ANY] * n, scratch_shapes=_gather_sems(n))(*blocks)


def _gather_out(blocks):
    return [SDS((N_DEV,) + b.shape, b.dtype) for b in blocks]


def _gather_sems(n):
    return [pltpu.SemaphoreType.DMA((7 * n,)), pltpu.SemaphoreType.DMA((7 * n,)), pltpu.SemaphoreType.DMA((n,))]


def _gather_phases(x_refs, out_refs, send_sems, recv_sems, local_sems):
    n = len(x_refs)
    x, y, c = _place()
    me, sibling = (x, y, c), (x, y, 1 - c)
    chips = [(1 - x, y), (x, 1 - y), (1 - x, 1 - y)]

    def slot(a, px, py, pc):
        return out_refs[a].at[4 * px + 2 * py + pc]

    def copy(a, k, blk, to, src=None):
        return pltpu.make_async_remote_copy(
            src_ref=slot(a, *blk) if src is None else src, dst_ref=slot(a, *blk),
            send_sem=send_sems.at[7 * a + k], recv_sem=recv_sems.at[7 * a + k], device_id=to,
            device_id_type=MESH_ID)

    def mine():
        return [pltpu.make_async_copy(x_refs[a], slot(a, *me), local_sems.at[a]) for a in range(n)]

    def first():
        out = []
        for a in range(n):
            out += [copy(a, 0, me, sibling, src=x_refs[a])]
            out += [copy(a, 1 + j, me, (*chip, c), src=x_refs[a]) for j, chip in enumerate(chips)]
        return out

    def passed():
        return [copy(a, 4 + j, (*chip, c), sibling) for j, chip in enumerate(chips) for a in range(n)]

    def start():
        for cp in mine() + first():
            cp.start()

    def forward():
        for j, chip in enumerate(chips):
            for a in range(n):
                copy(a, 1 + j, (*chip, c), me).wait_recv()
                copy(a, 4 + j, (*chip, c), sibling).start()

    def finish():
        for a in range(n):
            copy(a, 0, sibling, me).wait_recv()
            for j, chip in enumerate(chips):
                copy(a, 4 + j, (*chip, 1 - c), me).wait_recv()
        for cp in first() + passed():
            cp.wait_send()
        for cp in mine():
            cp.wait()

    return start, forward, finish


def _swap_sibling(arrs):
    n = len(arrs)

    def body(*refs):
        start, finish = _swap_phases(refs[:n], refs[n:2 * n], *refs[2 * n:])
        start()
        finish()

    return pl.pallas_call(
        body, name="swap_sibling", out_shape=[SDS((a.shape[0],) + a.shape[2:], a.dtype) for a in arrs],
        in_specs=[ANY] * n, out_specs=[ANY] * n, scratch_shapes=_swap_sems(n))(*arrs)


def _swap_sems(n):
    return [pltpu.SemaphoreType.DMA((n,)), pltpu.SemaphoreType.DMA((n,))]


def _swap_phases(a_refs, out_refs, send_sems, recv_sems):
    x, y, c = _place()

    def copies():
        return [pltpu.make_async_remote_copy(
            src_ref=a_refs[a].at[:, 1 - c], dst_ref=out_refs[a], send_sem=send_sems.at[a], recv_sem=recv_sems.at[a],
            device_id=(x, y, 1 - c), device_id_type=MESH_ID) for a in range(len(a_refs))]

    def start():
        for cp in copies():
            cp.start()

    def finish():
        for cp in copies():
            cp.wait()

    return start, finish


def _exchange_chips(arrs):
    n = len(arrs)

    def body(*refs):
        start, finish = _exchange_phases(refs[:n], refs[n:2 * n], *refs[2 * n:])
        start()
        finish()

    return pl.pallas_call(
        body, name="exchange_chips", out_shape=[SDS(a.shape, a.dtype) for a in arrs],
        in_specs=[ANY] * n, out_specs=[ANY] * n, scratch_shapes=_exchange_sems(n))(*arrs)


def _exchange_sems(n):
    return [pltpu.SemaphoreType.DMA((3 * n,)), pltpu.SemaphoreType.DMA((3 * n,)), pltpu.SemaphoreType.DMA((n,))]


def _exchange_phases(p_refs, out_refs, send_sems, recv_sems, local_sems):
    n = len(p_refs)
    x, y, c = _place()
    mine = 2 * x + y
    peers = [(1 - x, y), (x, 1 - y), (1 - x, 1 - y)]

    def local():
        return [pltpu.make_async_copy(p_refs[a].at[mine], out_refs[a].at[mine], local_sems.at[a]) for a in range(n)]

    def copy(a, j, src_chip, dst_chip):
        px, py = peers[j]
        return pltpu.make_async_remote_copy(
            src_ref=p_refs[a].at[src_chip], dst_ref=out_refs[a].at[dst_chip], send_sem=send_sems.at[3 * a + j],
            recv_sem=recv_sems.at[3 * a + j], device_id=(px, py, c), device_id_type=MESH_ID)

    def sends():
        return [copy(a, j, 2 * px + py, mine) for a in range(n) for j, (px, py) in enumerate(peers)]

    def start():
        for cp in local() + sends():
            cp.start()

    def finish():
        for a in range(n):
            for j, (px, py) in enumerate(peers):
                copy(a, j, mine, 2 * px + py).wait_recv()
        for cp in sends():
            cp.wait_send()
        for cp in local():
            cp.wait()

    return start, finish


def _add_mine(g, recv, core, tile, dtype):
    _, _, lead, rows, width = g.shape

    def body(c_ref, g_ref, r_ref, o_ref):
        o_ref[...] = (g_ref[...] + r_ref[...]).astype(dtype)

    spec = pl.BlockSpec((None, None, tile, width), lambda k, l, i, c_ref: (k, l, i, 0))
    return pl.pallas_call(
        body, name="add_sibling", out_shape=SDS(recv.shape, dtype),
        grid_spec=pltpu.PrefetchScalarGridSpec(
            num_scalar_prefetch=1, grid=(g.shape[0], lead, rows // tile),
            in_specs=[pl.BlockSpec((None, None, None, tile, width), lambda k, l, i, c_ref: (k, c_ref[0], l, i, 0)),
                      spec],
            out_specs=spec),
        compiler_params=_cp("parallel", "parallel", "parallel"))(core, g, recv)


def _sum_adamw(parts, w, m, v, tile):
    lead, rows, width = w.shape
    last = rows // tile - 1

    def body(*refs):
        p_refs, (w_ref, m_ref, v_ref, g_ref, d_ref, nm_ref, nv_ref) = refs[:lead], refs[lead:]
        for layer in range(lead):
            @pl.when(pl.program_id(0) == layer)
            def _(p_ref=p_refs[layer]):
                g = ((p_ref[0].astype(F32) + p_ref[1].astype(F32)) + p_ref[2].astype(F32)) + p_ref[3].astype(F32)
                nm = ADAM_B1 * m_ref[...] + (1.0 - ADAM_B1) * g
                nv = ADAM_B2 * v_ref[...] + (1.0 - ADAM_B2) * jnp.square(g)
                m_hat = nm / (1.0 - ADAM_B1 ** ADAM_STEP)
                v_hat = nv / (1.0 - ADAM_B2 ** ADAM_STEP)
                g_ref[...] = g
                nm_ref[...] = nm
                nv_ref[...] = nv
                d_ref[...] = -ADAM_LR * (m_hat / (jnp.sqrt(v_hat) + ADAM_EPS) + ADAM_WD * w_ref[...])

    pspec = lambda layer: pl.BlockSpec(
        (4, None, tile, width),
        lambda l, i: (0, 0, jnp.where(l == layer, i, jnp.where(l > layer, last, 0)), 0))
    spec = pl.BlockSpec((None, tile, width), lambda l, i: (l, i, 0))
    return pl.pallas_call(
        body, grid=(lead, rows // tile), name="sum_adamw",
        in_specs=[pspec(layer) for layer in range(lead)] + [spec, spec, spec],
        out_specs=[spec] * 4, out_shape=[SDS((lead, rows, width), F32)] * 4,
        compiler_params=_cp("arbitrary", "arbitrary"))(*parts, w, m, v)


def kernel(x, p, positions, g_mix, w_in, sink, g_q, w_uq, g_kv, w_ukv, w_br_a, w_br_b, w_out, g_ple, w_ple_gate, w_ple_proj, g_final, loss_target, m_g_mix, m_w_in, m_sink, m_g_q, m_w_uq, m_g_kv, m_w_ukv, m_w_br_a, m_w_br_b, m_w_out, m_g_ple, m_w_ple_gate, m_w_ple_proj, m_g_final, v_g_mix, v_w_in, v_sink, v_g_q, v_w_uq, v_g_kv, v_w_ukv, v_w_br_a, v_w_br_b, v_w_out, v_g_ple, v_w_ple_gate, v_w_ple_proj, v_g_final):
    weights = dict(g_mix=g_mix, w_in=w_in, sink=sink, g_q=g_q, w_uq=w_uq, g_kv=g_kv, w_ukv=w_ukv, w_br_a=w_br_a,
                   w_br_b=w_br_b, w_out=w_out, g_ple=g_ple, w_ple_gate=w_ple_gate, w_ple_proj=w_ple_proj,
                   g_final=g_final)
    mom1 = dict(g_mix=m_g_mix, w_in=m_w_in, sink=m_sink, g_q=m_g_q, w_uq=m_w_uq, g_kv=m_g_kv, w_ukv=m_w_ukv,
                w_br_a=m_w_br_a, w_br_b=m_w_br_b, w_out=m_w_out, g_ple=m_g_ple, w_ple_gate=m_w_ple_gate,
                w_ple_proj=m_w_ple_proj, g_final=m_g_final)
    mom2 = dict(g_mix=v_g_mix, w_in=v_w_in, sink=v_sink, g_q=v_g_q, w_uq=v_w_uq, g_kv=v_g_kv, w_ukv=v_w_ukv,
                w_br_a=v_w_br_a, w_br_b=v_w_br_b, w_out=v_w_out, g_ple=v_g_ple, w_ple_gate=v_w_ple_gate,
                w_ple_proj=v_w_ple_proj, g_final=v_g_final)
    assert DEPTH == 2
    wide = lambda d: d["w_in"]
    rows = lambda d: _pack_rows(d)
    core = lax.axis_index("c").astype(jnp.int32).reshape(1)

    w16 = [wide(weights).astype(BF16), rows(weights).astype(BF16)]
    wts0 = dict(w_in=_win_layout(_all_gather([w16[0][:1]])[0], 256))
    small = _small_params(g_mix, sink, g_q, g_kv, g_ple, g_final)

    def wide_payload(g):
        return _win_grad_layout(g["w_in"], 256).reshape(N_DEV // 2, 2, 1, D, WIDE_W)

    def rows_payload(g):
        return _pack_rows({n: _split(n, g[n][None]) for n, _ in ROWS_PIECES}).reshape(N_DEV // 2, 2, 1, ROWS_N, LANES)

    def add(pay, got):
        tiles = {D: (WIDE_TILE, BF16), ROWS_N: (ROWS_TILE, BF16), VEC_ROWS: (VEC_ROWS, F32)}
        return [_add_mine(a, b, core, *tiles[a.shape[-2]]) for a, b in zip(pay, got)]

    plan = dict(behind_fwd_in=[w16[1][:1]], behind_swa_fwd=[w16[0][1:]], behind_mla_fwd=[w16[1][1:]],
                w_in=lambda blocks: _win_layout(blocks, 256),
                row_weights=_row_weights, payload=lambda g: [wide_payload(g), rows_payload(g)],
                rows_payload=rows_payload, add=add)
    loss, grad_x, grads, dg_final, rode = _local_step(x, p, positions, loss_target, small, [wts0], plan)

    vectors = {n: jnp.stack([grads[i][n] for i in range(DEPTH)]) for n, _ in SMALL[:-1]}
    vectors["g_final"] = dg_final
    pay = [wide_payload(grads[0]),
           jnp.broadcast_to(_pack_vec(vectors, loss[0, 0]), (N_DEV // 2, 2, 1, VEC_ROWS, LANES))]
    parts_wide0, parts_vec = _exchange_chips(add(pay, _swap_sibling(pay)))
    out_wide = _sum_adamw([parts_wide0, rode["layer1"][0]], wide(weights), wide(mom1), wide(mom2), WIDE_TILE)
    out_rows = _sum_adamw([rode["rows0"][0], rode["layer1"][1]], rows(weights), rows(mom1), rows(mom2), ROWS_TILE)
    out_vec = _sum_adamw([parts_vec], _pack_vec(weights), _pack_vec(mom1), _pack_vec(mom2), VEC_ROWS)

    outs = []
    for ow, orow, ovec in zip(out_wide, out_rows, out_vec):
        named = _unpack_rows(orow)
        named.update(_unpack_vec(ovec)[0])
        named["w_in"] = ow
        outs += [named[n] for n in weights]
    loss = _unpack_vec(out_vec[0])[1]
    return (loss, grad_x, *outs)
```

```python
import functools

import jax
import jax.numpy as jnp
from jax import lax
from jax.experimental import pallas as pl
from jax.experimental.pallas import tpu as pltpu

F32, BF16 = jnp.float32, jnp.bfloat16
SDS = jax.ShapeDtypeStruct

D = 1024
DEPTH = 2
PLE = 256
BLK = 128
EPS = 1e-6
NEG = -1e30
SWA_H, SWA_KV, SWA_DH = 8, 2, 64
MLA_H, MLA_NOPE, MLA_ROPE, MLA_V = 8, 64, 32, 64
MLA_QK = MLA_NOPE + MLA_ROPE
QL, KVL = 256, 128
IN_W = 4256
N_DEV = 8

V7X_VMEM_BYTES = 64 * 1024 * 1024
LANES = 128
VMEM_LIMIT = V7X_VMEM_BYTES * 7 // 8

ZW = 4352
Z_MA, Z_MB, Z_AQ, Z_AG, Z_BG, Z_QD, Z_AK, Z_AV, Z_KVD, Z_KR = 0, 1024, 2048, 2560, 3072, 3584, 3840, 3968, 4096, 4224
QFW = MLA_H * LANES
KVW = QFW + MLA_H * MLA_V
MLA_SCALE = MLA_QK ** -0.5
LOG2E = 1.4426950408889634
MLA_FWD_HEADS, MLA_BWD_HEADS = 8, 4
SWA_SCALE = SWA_DH ** -0.5
ROLL_UP, ROLL_DOWN = MLA_ROPE // 2, LANES - MLA_ROPE // 2

ADAM_LR, ADAM_B1, ADAM_B2, ADAM_EPS, ADAM_WD, ADAM_STEP = 0.001, 0.9, 0.999, 1e-08, 0.01, 10


def _cp(*sem):
    return pltpu.CompilerParams(dimension_semantics=sem, vmem_limit_bytes=VMEM_LIMIT)


def _row(tm, w, col=0):
    return pl.BlockSpec((tm, w), lambda i: (i, col))


def _res(shape, layer=None):
    if layer is None:
        return pl.BlockSpec(shape, lambda *_: (0,) * len(shape), pipeline_mode=pl.Buffered(1))
    return pl.BlockSpec((None,) + shape, lambda *_: (layer,) + (0,) * len(shape), pipeline_mode=pl.Buffered(1))


def _acc(shape):
    return pl.BlockSpec(shape, lambda *_: (0,) * len(shape))


def _rstd(xf):
    return lax.rsqrt(jnp.mean(xf * xf, axis=-1, keepdims=True) + EPS)


def _norm_bwd(dh, n, r, g):
    dn = dh * g
    return r * (dn - n * jnp.mean(dn * n, axis=-1, keepdims=True)), dh * n


def _nt(a, b):
    return lax.dot_general(a, b, (((1,), (1,)), ((), ())), preferred_element_type=F32)


def _tn(a, b):
    return lax.dot_general(a, b, (((0,), (0,)), ((), ())), preferred_element_type=F32)


def _nn(a, b):
    return jnp.dot(a, b, preferred_element_type=F32)


def _sig(x):
    return jax.nn.sigmoid(x)


def _rope(t, c, s1, s2):
    return t * c + pltpu.roll(t, ROLL_UP, 1) * s1 + pltpu.roll(t, ROLL_DOWN, 1) * s2


def _rope_t(d, c, s1, s2):
    return d * c + pltpu.roll(d * s1, ROLL_DOWN, 1) + pltpu.roll(d * s2, ROLL_UP, 1)


def _fwd_in(x, g, w, tm, layer, ride=None):
    T = x.shape[0]
    grid = (T // tm,)

    def body(x_ref, g_ref, w_ref, z_ref, h_ref):
        xf = x_ref[...]
        h = ((xf * _rstd(xf)) * g_ref[...]).astype(BF16)
        h_ref[...] = h
        z_ref[...] = _nn(h, w_ref[...])

    r_in, r_out, r_shape, r_scratch, r_args = _ride_args(ride)
    return pl.pallas_call(
        _riding(ride, body, 3, 2, grid), grid=grid, name="fwd_in_ride" if ride else "fwd_in",
        in_specs=[_row(tm, D), _res((1, D), layer), _res((D, ZW), 0)] + r_in,
        out_specs=[_row(tm, ZW), _row(tm, D)] + r_out,
        out_shape=[SDS((T, ZW), F32), SDS((T, D), BF16)] + r_shape, scratch_shapes=r_scratch,
        compiler_params=_cp("arbitrary"))(x, g, w, *r_args)


def _fwd_prep(z, gq, gkv, wq, wkv, tc, ts1, ts2, tm, layer):
    T = z.shape[0]

    def body(qd_ref, kvd_ref, kr_ref, gq_ref, gkv_ref, wq_ref, wkv_ref, c_ref, s1_ref, s2_ref, q_ref, k_ref, v_ref):
        qd, kvd = qd_ref[...], kvd_ref[...]
        hq = ((qd * _rstd(qd)) * gq_ref[...]).astype(BF16)
        hkv = ((kvd * _rstd(kvd)) * gkv_ref[...]).astype(BF16)
        qf = _nn(hq, wq_ref[...])
        kvf = _nn(hkv, wkv_ref[...])
        c, s1, s2 = c_ref[...], s1_ref[...], s2_ref[...]
        krb = _rope(kr_ref[...], c, s1, s2)
        for h in range(MLA_H):
            sl = slice(LANES * h, LANES * (h + 1))
            q_ref[:, sl] = _rope(qf[:, sl], c, s1, s2).astype(BF16)
            k_ref[:, sl] = (kvf[:, sl] + krb).astype(BF16)
        v_ref[...] = kvf[:, QFW:].astype(BF16)

    return pl.pallas_call(
        body, grid=(T // tm,), name="fwd_prep",
        in_specs=[_row(tm, QL, Z_QD // QL), _row(tm, KVL, Z_KVD // KVL), _row(tm, LANES, Z_KR // LANES),
                  _res((1, QL), layer), _res((1, KVL), layer), _res((QL, QFW), 0), _res((KVL, KVW), 0),
                  _row(tm, LANES), _row(tm, LANES), _row(tm, LANES)],
        out_specs=[_row(tm, QFW), _row(tm, QFW), _row(tm, MLA_H * MLA_V)],
        out_shape=[SDS((T, QFW), BF16), SDS((T, QFW), BF16), SDS((T, MLA_H * MLA_V), BF16)],
        compiler_params=_cp("parallel"))(z, z, z, gq, gkv, wq, wkv, tc, ts1, ts2)


def _grid_ends(grid):
    ids = [pl.program_id(a) for a in range(len(grid))]
    inner_first = functools.reduce(jnp.logical_and, [i == 0 for i in ids[1:]], True)
    last = functools.reduce(jnp.logical_and, [i == g - 1 for i, g in zip(ids, grid)])
    return (ids[0] == 0) & inner_first, (ids[0] == 3 * grid[0] // 4) & inner_first, last


class _Ride:
    def __init__(self, kind, arrays):
        self.kind, self.arrays, self.n = kind, list(arrays), len(arrays)

    def out_shape(self):
        if self.kind == "gather":
            return _gather_out(self.arrays)
        if self.kind == "swap":
            return [SDS((a.shape[0],) + a.shape[2:], a.dtype) for a in self.arrays]
        return [SDS(a.shape, a.dtype) for a in self.arrays]

    def sems(self):
        if self.kind == "gather":
            return _gather_sems(self.n)
        if self.kind == "swap":
            return _swap_sems(self.n)
        return _exchange_sems(self.n)

    def phases(self, in_refs, out_refs, *sems):
        if self.kind == "gather":
            return _gather_phases(in_refs, out_refs, *sems)
        start, finish = (_swap_phases if self.kind == "swap" else _exchange_phases)(in_refs, out_refs, *sems)
        return start, None, finish


def _riding(ride, body, n_in, n_out, grid):
    if ride is None:
        return body
    n, n_sems = ride.n, len(ride.sems())

    def wrapped(*refs):
        ins, r_in = refs[:n_in], refs[n_in:n_in + n]
        outs, r_out = refs[n_in + n:n_in + n + n_out], refs[n_in + n + n_out:n_in + 2 * n + n_out]
        rest = refs[n_in + 2 * n + n_out:]
        scratch, sems = rest[:len(rest) - n_sems], rest[len(rest) - n_sems:]
        start, middle, finish = ride.phases(r_in, r_out, *sems)
        at_first, at_middle, at_last = _grid_ends(grid)
        pl.when(at_first)(start)
        if middle is not None:
            pl.when(at_middle)(middle)
        body(*ins, *outs, *scratch)
        pl.when(at_last)(finish)

    return wrapped


def _ride_args(ride):
    if ride is None:
        return [], [], [], [], []
    return [ANY] * ride.n, [ANY] * ride.n, ride.out_shape(), ride.sems(), ride.arrays


def _mla_fwd(qf, kf, v, nb, seq, tq, gather=()):
    T = qf.shape[0]
    nq = seq // tq
    hp = MLA_FWD_HEADS
    pw = hp * LANES
    pairs = [(qi, ki) for qi in range(nq) for ki in range(qi + 1)]
    qi_tab = jnp.array([qk[0] for qk in pairs], jnp.int32)
    ki_tab = jnp.array([qk[1] for qk in pairs], jnp.int32)
    grid = (nb, MLA_H // hp, len(pairs))
    n_g = len(gather)

    def body(qi_ref, ki_ref, q_ref, k_ref, v_ref, *rest):
        x_refs, (o_ref, lse_ref), got_refs = rest[:n_g], rest[n_g:n_g + 2], rest[n_g + 2:2 * n_g + 2]
        (m_s, l_s, acc_s), sems = rest[2 * n_g + 2:2 * n_g + 5], rest[2 * n_g + 5:]
        qi, ki = qi_ref[pl.program_id(2)], ki_ref[pl.program_id(2)]
        if n_g:
            start, forward, finish = _gather_phases(x_refs, got_refs, *sems)
            at_first, at_middle, at_last = _grid_ends(grid)
            pl.when(at_first)(start)
            pl.when(at_middle)(forward)

        @pl.when(ki == 0)
        def _():
            m_s[...] = jnp.full(m_s.shape, NEG, F32)
            l_s[...] = jnp.zeros(l_s.shape, F32)
            acc_s[...] = jnp.zeros(acc_s.shape, F32)

        def step(masked):
            parts = [(0, tq // 2, tq // 2), (tq // 2, tq, tq)] if masked else [(0, tq, tq)]
            work = [(j, a, b, kh) for j in range(hp) for a, b, kh in parts]
            ss = []
            for j, a, b, kh in work:
                wide = slice(LANES * j, LANES * (j + 1))
                s = _nt(k_ref[:kh, wide], q_ref[a:b, wide]) * (MLA_SCALE * LOG2E)
                if masked:
                    keys = lax.broadcasted_iota(jnp.int32, (kh, b - a), 0)
                    queries = a + lax.broadcasted_iota(jnp.int32, (kh, b - a), 1)
                    s = jnp.where(keys <= queries, s, NEG)
                ss.append(s)
            ps, alphas = [], []
            for (j, a, b, kh), s in zip(work, ss):
                m_prev = m_s[j, :, a:b]
                m_new = jnp.maximum(m_prev, jnp.max(s, axis=0, keepdims=True))
                alpha = jnp.exp2(m_prev - m_new)
                p = jnp.exp2(s - m_new)
                l_s[j, :, a:b] = alpha * l_s[j, :, a:b] + jnp.sum(p, axis=0, keepdims=True)
                m_s[j, :, a:b] = m_new
                ps.append(p.astype(BF16))
                alphas.append(alpha)
            for (j, a, b, kh), p, alpha in zip(work, ps, alphas):
                rows = slice(MLA_V * j, MLA_V * (j + 1))
                acc_s[rows, a:b] = alpha * acc_s[rows, a:b] + _tn(v_ref[:kh, rows], p)

        @pl.when(ki < qi)
        def _():
            step(False)

        @pl.when(ki == qi)
        def _():
            step(True)
            for j in range(hp):
                rows = slice(MLA_V * j, MLA_V * (j + 1))
                acc_s[rows, :] = acc_s[rows, :] / l_s[j]
                lse_ref[j:j + 1, :] = m_s[j] + jnp.log2(l_s[j])
            o_ref[...] = acc_s[...].T

        if n_g:
            pl.when(at_last)(finish)

    q_map = lambda b, g, s, qi_ref, ki_ref: (b * nq + qi_ref[s], g)
    kv_map = lambda b, g, s, qi_ref, ki_ref: (b * nq + ki_ref[s], g)
    return pl.pallas_call(
        body, name="mla_fwd_gather" if n_g else "mla_fwd",
        grid_spec=pltpu.PrefetchScalarGridSpec(
            num_scalar_prefetch=2, grid=grid,
            in_specs=[pl.BlockSpec((tq, pw), q_map), pl.BlockSpec((tq, pw), kv_map),
                      pl.BlockSpec((tq, hp * MLA_V), kv_map)] + [ANY] * n_g,
            out_specs=[pl.BlockSpec((tq, hp * MLA_V), q_map),
                       pl.BlockSpec((hp, tq), lambda b, g, s, qi_ref, ki_ref: (g, b * nq + qi_ref[s]))]
            + [ANY] * n_g,
            scratch_shapes=[pltpu.VMEM((hp, 1, tq), F32), pltpu.VMEM((hp, 1, tq), F32),
                            pltpu.VMEM((hp * MLA_V, tq), F32)]
            + (_gather_sems(n_g) if n_g else [])),
        out_shape=[SDS((T, MLA_H * MLA_V), F32), SDS((MLA_H, T), F32)] + _gather_out(gather),
        compiler_params=_cp("arbitrary", "arbitrary", "arbitrary"))(qi_tab, ki_tab, qf, kf, v, *gather)


SWA_STEP = 2 * BLK


def _swa_specs(nstep):
    cur = lambda b, m: (b * nstep + m, 0)
    prev = lambda b, m: (2 * b * nstep + jnp.maximum(2 * m - 1, 0), 0)
    kvc = Z_AK // (2 * BLK)
    return [pl.BlockSpec(memory_space=pltpu.SMEM),
            pl.BlockSpec((SWA_STEP, 512), lambda b, m: (b * nstep + m, Z_AQ // 512)),
            pl.BlockSpec((SWA_STEP, 2 * BLK), lambda b, m: (b * nstep + m, kvc)),
            pl.BlockSpec((BLK, 2 * BLK), lambda b, m: (2 * b * nstep + jnp.maximum(2 * m - 1, 0), kvc)),
            pl.BlockSpec((SWA_STEP, 1), cur),
            pl.BlockSpec((BLK, 1), prev),
            pl.BlockSpec((1, 1, SWA_STEP), lambda b, m: (b * nstep + m, 0, 0))]


def _swa_scores(m, q_ref, kvc_ref, kvp_ref, pcc_ref, pcp_ref, pr_ref):
    kv = jnp.concatenate([kvp_ref[...], kvc_ref[...]], axis=0)
    kb, vb = kv[:, :BLK].astype(BF16), kv[:, BLK:].astype(BF16)
    pos_keys = jnp.concatenate([pcp_ref[...], pcc_ref[...]], axis=0)
    key = lax.broadcasted_iota(jnp.int32, (2 * BLK, BLK), 0)
    qry = lax.broadcasted_iota(jnp.int32, (2 * BLK, BLK), 1)
    in_window = (key > qry) & (key <= qry + BLK)
    valid = [in_window & ((key >= BLK) | (m > 0)), in_window]
    dist = [pr_ref[0][:, BLK * u:BLK * (u + 1)] - pos_keys[BLK * u:BLK * (u + 2)] for u in range(2)]

    def band(t, u, g):
        return t[BLK * u:BLK * (u + 2), SWA_DH * g:SWA_DH * (g + 1)]

    def scores(u, h):
        g = h // (SWA_H // SWA_KV)
        qh = q_ref[BLK * u:BLK * (u + 1), SWA_DH * h:SWA_DH * (h + 1)].astype(BF16)
        s = _nt(band(kb, u, g), qh) * (SWA_SCALE * LOG2E) - (2.0 ** -(h + 1) * LOG2E) * dist[u]
        return qh, jnp.where(valid[u], s, NEG)

    return kb, vb, band, scores


def _swa_fwd(sink, z, pos_col, pos_row, nb, seq, layer, ride=None):
    T = z.shape[0]
    nstep = seq // SWA_STEP
    chains = [(u, h) for u in range(2) for h in range(SWA_H)]

    def body(sink_ref, q_ref, kvc_ref, kvp_ref, pcc_ref, pcp_ref, pr_ref, o_ref, lse_ref):
        kb, vb, band, scores = _swa_scores(pl.program_id(1), q_ref, kvc_ref, kvp_ref, pcc_ref, pcp_ref, pr_ref)
        ss = [scores(u, h)[1] for u, h in chains]
        es, dens = [], []
        for (u, h), s in zip(chains, ss):
            sk = sink_ref[layer, h] * LOG2E
            m = jnp.maximum(jnp.max(s, axis=0, keepdims=True), sk)
            e = jnp.exp2(s - m)
            den = jnp.sum(e, axis=0, keepdims=True) + jnp.exp2(sk - m)
            lse_ref[h:h + 1, BLK * u:BLK * (u + 1)] = m + jnp.log2(den)
            es.append(e.astype(BF16))
            dens.append(den)
        outs = [_tn(band(vb, u, h // (SWA_H // SWA_KV)), e) / den for (u, h), e, den in zip(chains, es, dens)]
        for u in range(2):
            o_ref[BLK * u:BLK * (u + 1), :] = jnp.concatenate(outs[SWA_H * u:SWA_H * (u + 1)], axis=0).T

    r_in, r_out, r_shape, r_scratch, r_args = _ride_args(ride)
    return pl.pallas_call(
        _riding(ride, body, 7, 2, (nb, nstep)), grid=(nb, nstep), name="swa_fwd_ride" if ride else "swa_fwd",
        in_specs=_swa_specs(nstep) + r_in,
        out_specs=[pl.BlockSpec((SWA_STEP, 512), lambda b, m: (b * nstep + m, 0)),
                   pl.BlockSpec((SWA_H, SWA_STEP), lambda b, m: (0, b * nstep + m))] + r_out,
        out_shape=[SDS((T, 512), F32), SDS((SWA_H, T), F32)] + r_shape, scratch_shapes=r_scratch,
        compiler_params=_cp("arbitrary", "arbitrary"))(sink, z, z, z, pos_col, pos_col, pos_row, *r_args)


def _fwd_merge(x, oa, ob, z, wa, wb, wo, tm, layer):
    T = x.shape[0]

    def body(x_ref, oa_ref, ob_ref, ag_ref, bg_ref, ma_ref, mb_ref, wa_ref, wb_ref, wo_ref, x1_ref):
        ag, bg = ag_ref[...], bg_ref[...]
        ua = _nn((oa_ref[...] * (ag * _sig(ag))).astype(BF16), wa_ref[...])
        ub = _nn((ob_ref[...] * (bg * _sig(bg))).astype(BF16), wb_ref[...])
        y = _sig(ma_ref[...]) * ua + _sig(mb_ref[...]) * ub
        x1_ref[...] = x_ref[...] + _nn(y.astype(BF16), wo_ref[...])

    return pl.pallas_call(
        body, grid=(T // tm,), name="fwd_merge",
        in_specs=[_row(tm, D), _row(tm, 512), _row(tm, 512), _row(tm, 512, Z_AG // 512), _row(tm, 512, Z_BG // 512),
                  _row(tm, D, Z_MA // D), _row(tm, D, Z_MB // D),
                  _res((512, D), 0), _res((512, D), 0), _res((D, D), 0)],
        out_specs=_row(tm, D),
        out_shape=SDS((T, D), F32),
        compiler_params=_cp("parallel"))(x, oa, ob, z, z, z, z, wa, wb, wo)


def _fwd_ple(x1, p, g, wpg, wpp, tm, layer):
    T = x1.shape[0]

    def body(x_ref, p_ref, g_ref, wpg_ref, wpp_ref, x2_ref, pg_ref, pp_ref):
        xf = x_ref[...]
        h1 = ((xf * _rstd(xf)) * g_ref[...]).astype(BF16)
        pg = _sig(_nn(h1, wpg_ref[...]))
        pp = _nn(p_ref[...].astype(BF16), wpp_ref[...])
        pg_ref[...] = pg
        pp_ref[...] = pp
        x2_ref[...] = xf + pg * pp

    return pl.pallas_call(
        body, grid=(T // tm,), name="fwd_ple",
        in_specs=[_row(tm, D), pl.BlockSpec((None, tm, PLE), lambda i: (layer, i, 0)),
                  _res((1, D), layer), _res((D, D), 0), _res((PLE, D), 0)],
        out_specs=[_row(tm, D)] * 3,
        out_shape=[SDS((T, D), F32)] * 3,
        compiler_params=_cp("parallel"))(x1, p, g, wpg, wpp)


def _ple_loss(x1, p, g, wpg, wpp, g_final, tgt, tm, layer):
    T = x1.shape[0]

    def body(x_ref, p_ref, g_ref, wpg_ref, wpp_ref, gf_ref, t_ref, dx_ref, dwg_ref, dwp_ref, dg_ref, dgf_ref, loss_ref):
        @pl.when(pl.program_id(0) == 0)
        def _():
            for ref in (dwg_ref, dwp_ref, dg_ref, dgf_ref, loss_ref):
                ref[...] = jnp.zeros(ref.shape, F32)

        xf, gp, gf = x_ref[...], g_ref[...], gf_ref[...]
        r = _rstd(xf)
        n = xf * r
        h1 = (n * gp).astype(BF16)
        pb = p_ref[...].astype(BF16)
        pg = _sig(_nn(h1, wpg_ref[...]))
        pp = _nn(pb, wpp_ref[...])
        x2 = xf + pg * pp
        r2 = _rstd(x2)
        n2 = x2 * r2
        err = n2 * gf - t_ref[...]
        loss_ref[...] += 0.5 * jnp.sum(jnp.mean(err * err, axis=-1, keepdims=True), axis=0, keepdims=True)
        d, dgfr = _norm_bwd(err * (1.0 / D), n2, r2, gf)
        dgf_ref[...] += jnp.sum(dgfr, axis=0, keepdims=True)
        dpgl = (d * pp * pg * (1.0 - pg)).astype(BF16)
        dwg_ref[...] += _tn(h1, dpgl)
        dwp_ref[...] += _tn(pb, (d * pg).astype(BF16))
        dxn, dgr = _norm_bwd(_nt(dpgl, wpg_ref[...]), n, r, gp)
        dx_ref[...] = d + dxn
        dg_ref[...] += jnp.sum(dgr, axis=0, keepdims=True)

    return pl.pallas_call(
        body, grid=(T // tm,), name="ple_loss",
        in_specs=[_row(tm, D), pl.BlockSpec((None, tm, PLE), lambda i: (layer, i, 0)), _res((1, D), layer),
                  _res((D, D), 0), _res((PLE, D), 0), _res((1, D)), _row(tm, D)],
        out_specs=[_row(tm, D), _acc((D, D)), _acc((PLE, D)), _acc((1, D)), _acc((1, D)), _acc((1, LANES))],
        out_shape=[SDS((T, D), F32), SDS((D, D), F32), SDS((PLE, D), F32), SDS((1, D), F32), SDS((1, D), F32),
                   SDS((1, LANES), F32)],
        compiler_params=_cp("arbitrary"))(x1, p, g, wpg, wpp, g_final, tgt)


def _bwd_ple(dx2, x1, pg, pp, p, g, wpg, tm, layer, ride=None):
    T = x1.shape[0]
    grid = (T // tm,)

    def body(d_ref, x_ref, pg_ref, pp_ref, p_ref, g_ref, w_ref, dx_ref, dwg_ref, dwp_ref, dg_ref):
        @pl.when(pl.program_id(0) == 0)
        def _():
            dwg_ref[...] = jnp.zeros(dwg_ref.shape, F32)
            dwp_ref[...] = jnp.zeros(dwp_ref.shape, F32)
            dg_ref[...] = jnp.zeros(dg_ref.shape, F32)

        d, xf, pg, gf = d_ref[...], x_ref[...], pg_ref[...], g_ref[...]
        r = _rstd(xf)
        n = xf * r
        dpgl = (d * pp_ref[...] * pg * (1.0 - pg)).astype(BF16)
        dwg_ref[...] += _tn((n * gf).astype(BF16), dpgl)
        dwp_ref[...] += _tn(p_ref[...].astype(BF16), (d * pg).astype(BF16))
        dxn, dgr = _norm_bwd(_nt(dpgl, w_ref[...]), n, r, gf)
        dx_ref[...] = d + dxn
        dg_ref[...] += jnp.sum(dgr, axis=0, keepdims=True)

    r_in, r_out, r_shape, r_scratch, r_args = _ride_args(ride)
    return pl.pallas_call(
        _riding(ride, body, 7, 4, grid), grid=grid, name="bwd_ple_ride" if ride else "bwd_ple",
        in_specs=[_row(tm, D)] * 4 + [pl.BlockSpec((None, tm, PLE), lambda i: (layer, i, 0)),
                                      _res((1, D), layer), _res((D, D), 0)] + r_in,
        out_specs=[_row(tm, D), _acc((D, D)), _acc((PLE, D)), _acc((1, D))] + r_out,
        out_shape=[SDS((T, D), F32), SDS((D, D), F32), SDS((PLE, D), F32), SDS((1, D), F32)] + r_shape,
        scratch_shapes=r_scratch,
        compiler_params=_cp("arbitrary"))(dx2, x1, pg, pp, p, g, wpg, *r_args)


def _bwd_merge(dx1, oa, ob, z, wa, wb, wo, tm, layer):
    T = dx1.shape[0]

    def body(d_ref, oa_ref, ob_ref, ag_ref, bg_ref, ma_ref, mb_ref, wa_ref, wb_ref, wo_ref,
             doa_ref, dob_ref, dag_ref, dbg_ref, dma_ref, dmb_ref, dsa_ref, dsb_ref, dwa_ref, dwb_ref, dwo_ref):
        @pl.when(pl.program_id(0) == 0)
        def _():
            dwa_ref[...] = jnp.zeros(dwa_ref.shape, F32)
            dwb_ref[...] = jnp.zeros(dwb_ref.shape, F32)
            dwo_ref[...] = jnp.zeros(dwo_ref.shape, F32)

        db = d_ref[...].astype(BF16)
        gated = []
        for o_ref, gate_ref, w_ref in ((oa_ref, ag_ref, wa_ref), (ob_ref, bg_ref, wb_ref)):
            raw, gate = o_ref[...], gate_ref[...]
            sg = _sig(gate)
            silu = gate * sg
            ob16 = (raw * silu).astype(BF16)
            gated.append((raw, gate, sg, silu, ob16, _nn(ob16, w_ref[...])))
        ua, ub = gated[0][5], gated[1][5]
        sa, sb = _sig(ma_ref[...]), _sig(mb_ref[...])
        dwo_ref[...] += _tn((sa * ua + sb * ub).astype(BF16), db)
        dy = _nt(db, wo_ref[...])
        dma_ref[...] = (dy * ua * sa * (1.0 - sa)).astype(BF16)
        dmb_ref[...] = (dy * ub * sb * (1.0 - sb)).astype(BF16)
        for (s, w_ref, do_ref, dgate_ref, dw_ref, ds_ref), (raw, gate, sg, silu, ob16, _) in zip((
                (sa, wa_ref, doa_ref, dag_ref, dwa_ref, dsa_ref),
                (sb, wb_ref, dob_ref, dbg_ref, dwb_ref, dsb_ref)), gated):
            du = (dy * s).astype(BF16)
            dw_ref[...] += _tn(ob16, du)
            do = _nt(du, w_ref[...])
            draw = do * silu
            do_ref[...] = draw.astype(BF16)
            dgate_ref[...] = (do * raw * (sg * (1.0 + gate * (1.0 - sg)))).astype(BF16)
            ds_ref[...] = jnp.sum((draw * raw).T.reshape(MLA_H, MLA_V, tm), axis=1)

    return pl.pallas_call(
        body, grid=(T // tm,), name="bwd_merge",
        in_specs=[_row(tm, D), _row(tm, 512), _row(tm, 512), _row(tm, 512, Z_AG // 512), _row(tm, 512, Z_BG // 512),
                  _row(tm, D, Z_MA // D), _row(tm, D, Z_MB // D),
                  _res((512, D), 0), _res((512, D), 0), _res((D, D), 0)],
        out_specs=[_row(tm, 512)] * 4 + [_row(tm, D)] * 2 + [pl.BlockSpec((MLA_H, tm), lambda i: (0, i))] * 2
        + [_acc((512, D)), _acc((512, D)), _acc((D, D))],
        out_shape=[SDS((T, 512), BF16), SDS((T, 512), BF16), SDS((T, 512), BF16), SDS((T, 512), BF16),
                   SDS((T, D), BF16), SDS((T, D), BF16), SDS((MLA_H, T), F32), SDS((MLA_H, T), F32),
                   SDS((512, D), F32), SDS((512, D), F32), SDS((D, D), F32)],
        compiler_params=_cp("arbitrary"))(dx1, oa, ob, z, z, z, z, wa, wb, wo)


def _mla_bwd(qf, kf, v, do, lse, dsum, nb, seq, tq, exchange=()):
    T = qf.shape[0]
    nq = seq // tq
    hp = MLA_BWD_HEADS
    pw = hp * LANES
    pairs = [(qi, ki) for ki in range(nq) for qi in range(ki, nq)]
    qi_tab = jnp.array([qk[0] for qk in pairs], jnp.int32)
    ki_tab = jnp.array([qk[1] for qk in pairs], jnp.int32)
    grid = (nb, MLA_H // hp, len(pairs))
    n_x = len(exchange)

    def body(qi_ref, ki_ref, q_ref, k_ref, v_ref, do_ref, lse_ref, dsum_ref, *rest):
        p_refs, (dq_ref, dk_ref, dv_ref), got_refs = rest[:n_x], rest[n_x:n_x + 3], rest[n_x + 3:2 * n_x + 3]
        (dk_s, dv_s, dqt_s), sems = rest[2 * n_x + 3:2 * n_x + 6], rest[2 * n_x + 6:]
        step_id = pl.program_id(2)
        qi, ki = qi_ref[step_id], ki_ref[step_id]
        if n_x:
            start, finish = _exchange_phases(p_refs, got_refs, *sems)
            at_first, _, at_last = _grid_ends(grid)
            pl.when(at_first)(start)

        @pl.when(step_id == 0)
        def _():
            dqt_s[...] = jnp.zeros(dqt_s.shape, F32)

        @pl.when(qi == ki)
        def _():
            dk_s[...] = jnp.zeros(dk_s.shape, F32)
            dv_s[...] = jnp.zeros(dv_s.shape, F32)

        def step(masked):
            if masked:
                keys = lax.broadcasted_iota(jnp.int32, (tq, tq), 0)
                queries = lax.broadcasted_iota(jnp.int32, (tq, tq), 1)
                mask = keys <= queries
            for j in range(hp):
                wide = slice(LANES * j, LANES * (j + 1))
                sl = slice(MLA_V * j, MLA_V * (j + 1))
                q, k = q_ref[:, wide], k_ref[:, wide]
                dob = do_ref[:, sl].astype(BF16)
                s = _nt(k, q) * (MLA_SCALE * LOG2E)
                if masked:
                    s = jnp.where(mask, s, NEG)
                p = jnp.exp2(s - lse_ref[j:j + 1, :])
                dv_s[:, sl] += _nn(p.astype(BF16), dob)
                ds = (p * (_nt(v_ref[:, sl], dob) - dsum_ref[j:j + 1, :]) * MLA_SCALE).astype(BF16)
                dk_s[:, wide] += _nn(ds, q)
                dqt_s[qi, wide, :] += _tn(k, ds)

        @pl.when(qi > ki)
        def _():
            step(False)

        @pl.when(qi == ki)
        def _():
            step(True)

        @pl.when(qi == nq - 1)
        def _():
            dk_ref[...] = dk_s[...]
            dv_ref[...] = dv_s[...]

        @pl.when(step_id == len(pairs) - 1)
        def _():
            for n in range(nq):
                dq_ref[tq * n:tq * (n + 1), :] = dqt_s[n].T

        if n_x:
            pl.when(at_last)(finish)

    qmap = lambda b, g, s, qi_ref, ki_ref: (b * nq + qi_ref[s], g)
    kmap = lambda b, g, s, qi_ref, ki_ref: (b * nq + ki_ref[s], g)
    stat = pl.BlockSpec((None, hp, tq), lambda b, g, s, qi_ref, ki_ref: (g, 0, b * nq + qi_ref[s]))
    vw = hp * MLA_V
    return pl.pallas_call(
        body, name="mla_bwd_exchange" if n_x else "mla_bwd",
        grid_spec=pltpu.PrefetchScalarGridSpec(
            num_scalar_prefetch=2, grid=grid,
            in_specs=[pl.BlockSpec((tq, pw), qmap), pl.BlockSpec((tq, pw), kmap), pl.BlockSpec((tq, vw), kmap),
                      pl.BlockSpec((tq, vw), qmap), stat, stat] + [ANY] * n_x,
            out_specs=[pl.BlockSpec((seq, pw), lambda b, g, s, qi_ref, ki_ref: (b, g)),
                       pl.BlockSpec((tq, pw), kmap), pl.BlockSpec((tq, vw), kmap)] + [ANY] * n_x,
            scratch_shapes=[pltpu.VMEM((tq, pw), F32), pltpu.VMEM((tq, vw), F32), pltpu.VMEM((nq, pw, tq), F32)]
            + (_exchange_sems(n_x) if n_x else [])),
        out_shape=[SDS((T, QFW), F32), SDS((T, QFW), F32), SDS((T, MLA_H * MLA_V), F32)]
        + [SDS(a.shape, a.dtype) for a in exchange],
        compiler_params=_cp("arbitrary", "arbitrary", "arbitrary"))(qi_tab, ki_tab, qf, kf, v, do, lse, dsum, *exchange)


def _swa_bwd(sink, z, pos_col, pos_row, do, lse, dsum, nb, seq, layer, ride=None):
    T = z.shape[0]
    nstep = seq // SWA_STEP
    chains = [(u, h) for u in range(2) for h in range(SWA_H)]

    def body(sink_ref, q_ref, kvc_ref, kvp_ref, pcc_ref, pcp_ref, pr_ref, do_ref, lse_ref, dsum_ref,
             dq_ref, dkv_ref, dsink_ref):
        b, m = pl.program_id(0), pl.program_id(1)

        @pl.when((b == 0) & (m == 0))
        def _():
            dsink_ref[...] = jnp.zeros(dsink_ref.shape, F32)

        @pl.when(m == 0)
        def _():
            dkv_ref[...] = jnp.zeros(dkv_ref.shape, F32)

        kb, vb, band, scores = _swa_scores(m, q_ref, kvc_ref, kvp_ref, pcc_ref, pcp_ref, pr_ref)
        lane = lax.broadcasted_iota(jnp.int32, (1, LANES), 1)
        dsink = jnp.zeros((1, LANES), F32)
        group = lambda h: h // (SWA_H // SWA_KV)
        qs, ss, dobs, dps = [], [], [], []
        for u, h in chains:
            qh, s = scores(u, h)
            dob = do_ref[BLK * u:BLK * (u + 1), SWA_DH * h:SWA_DH * (h + 1)].astype(BF16)
            qs.append(qh)
            ss.append(s)
            dobs.append(dob)
            dps.append(_nt(band(vb, u, group(h)), dob))
        pbs, dss = [], []
        for (u, h), s, dp in zip(chains, ss, dps):
            cols = slice(BLK * u, BLK * (u + 1))
            lse, dsum = lse_ref[h:h + 1, cols], dsum_ref[h:h + 1, cols]
            p = jnp.exp2(s - lse)
            pbs.append(p.astype(BF16))
            dss.append((p * (dp - dsum) * SWA_SCALE).astype(BF16))
            dsk = jnp.sum(-jnp.exp2(sink_ref[layer, h] * LOG2E - lse) * dsum, axis=1, keepdims=True)
            dsink = dsink + jnp.where(lane == h, dsk, 0.0)
        dqs, dkv = [], [[[None, None], [None, None]] for _ in range(2)]
        for i, (u, h) in enumerate(chains):
            g = group(h)
            dqs.append(_tn(band(kb, u, g), dss[i]))
            dk, dv = _nn(dss[i], qs[i]), _nn(pbs[i], dobs[i])
            dkv[u][g][0] = dk if dkv[u][g][0] is None else dkv[u][g][0] + dk
            dkv[u][g][1] = dv if dkv[u][g][1] is None else dkv[u][g][1] + dv
        for u in range(2):
            dq_ref[BLK * u:BLK * (u + 1), :] = jnp.concatenate(dqs[SWA_H * u:SWA_H * (u + 1)], axis=0).T.astype(BF16)
        dsink_ref[...] += dsink
        upd = [jnp.concatenate([dkv[u][0][0], dkv[u][1][0], dkv[u][0][1], dkv[u][1][1]], axis=1) for u in range(2)]
        base = pl.multiple_of(m * SWA_STEP, SWA_STEP)
        dkv_ref[pl.ds(base, BLK), :] += upd[0][BLK:] + upd[1][:BLK]
        dkv_ref[pl.ds(base + BLK, BLK), :] += upd[1][BLK:]

        @pl.when(m > 0)
        def _():
            dkv_ref[pl.ds(pl.multiple_of(m * SWA_STEP - BLK, BLK), BLK), :] += upd[0][:BLK]

    r_in, r_out, r_shape, r_scratch, r_args = _ride_args(ride)
    return pl.pallas_call(
        _riding(ride, body, 10, 3, (nb, nstep)), grid=(nb, nstep), name="swa_bwd_ride" if ride else "swa_bwd",
        in_specs=_swa_specs(nstep) + [pl.BlockSpec((SWA_STEP, 512), lambda b, m: (b * nstep + m, 0))]
        + [pl.BlockSpec((SWA_H, SWA_STEP), lambda b, m: (0, b * nstep + m))] * 2 + r_in,
        out_specs=[pl.BlockSpec((SWA_STEP, 512), lambda b, m: (b * nstep + m, 0)),
                   pl.BlockSpec((seq, 2 * BLK), lambda b, m: (b, 0)),
                   pl.BlockSpec((1, LANES), lambda b, m: (0, 0))] + r_out,
        out_shape=[SDS((T, 512), BF16), SDS((T, 2 * BLK), F32), SDS((1, LANES), F32)] + r_shape,
        scratch_shapes=r_scratch,
        compiler_params=_cp("arbitrary", "arbitrary"))(sink, z, z, z, pos_col, pos_col, pos_row, do, lse, dsum,
                                                       *r_args)


def _bwd_prep(dq, dk, dv, z, gq, gkv, wq, wkv, tc, ts1, ts2, tm, layer):
    T = z.shape[0]

    def body(dq_ref, dk_ref, dv_ref, qd_ref, kvd_ref, gq_ref, gkv_ref, wq_ref, wkv_ref, c_ref, s1_ref, s2_ref,
             dqd_ref, dkvd_ref, dkr_ref, dwq_ref, dwkv_ref, dgq_ref, dgkv_ref, dqb_s, dkvb_s):
        @pl.when(pl.program_id(0) == 0)
        def _():
            for ref in (dwq_ref, dwkv_ref, dgq_ref, dgkv_ref):
                ref[...] = jnp.zeros(ref.shape, F32)

        c, s1, s2 = c_ref[...], s1_ref[...], s2_ref[...]
        lane = lax.broadcasted_iota(jnp.int32, (1, LANES), 1)
        rope_lanes = (lane >= MLA_NOPE) & (lane < MLA_QK)
        dkb = jnp.zeros((tm, LANES), F32)
        for h in range(MLA_H):
            sl = slice(LANES * h, LANES * (h + 1))
            dqb_s[:, sl] = _rope_t(dq_ref[:, sl], c, s1, s2).astype(BF16)
            dkh = dk_ref[:, sl]
            dkb = dkb + dkh
            dkvb_s[:, sl] = dkh.astype(BF16)
        dkvb_s[:, QFW:] = dv_ref[...].astype(BF16)
        dkr_ref[...] = _rope_t(jnp.where(rope_lanes, dkb, 0.0), c, s1, s2).astype(BF16)

        for (x_ref, g_ref, w_ref, d_s, dx_ref, dw_ref, dg_ref) in (
                (qd_ref, gq_ref, wq_ref, dqb_s, dqd_ref, dwq_ref, dgq_ref),
                (kvd_ref, gkv_ref, wkv_ref, dkvb_s, dkvd_ref, dwkv_ref, dgkv_ref)):
            xf, gf, db = x_ref[...], g_ref[...], d_s[...]
            r = _rstd(xf)
            n = xf * r
            dw_ref[...] += _tn((n * gf).astype(BF16), db)
            dx, dgr = _norm_bwd(_nt(db, w_ref[...]), n, r, gf)
            dx_ref[...] = dx.astype(BF16)
            dg_ref[...] += jnp.sum(dgr, axis=0, keepdims=True)

    return pl.pallas_call(
        body, grid=(T // tm,), name="bwd_prep",
        in_specs=[_row(tm, QFW), _row(tm, QFW), _row(tm, MLA_H * MLA_V),
                  _row(tm, QL, Z_QD // QL), _row(tm, KVL, Z_KVD // KVL),
                  _res((1, QL), layer), _res((1, KVL), layer), _res((QL, QFW), 0), _res((KVL, KVW), 0),
                  _row(tm, LANES), _row(tm, LANES), _row(tm, LANES)],
        out_specs=[_row(tm, QL), _row(tm, KVL), _row(tm, LANES),
                   _acc((QL, QFW)), _acc((KVL, KVW)), _acc((1, QL)), _acc((1, KVL))],
        out_shape=[SDS((T, QL), BF16), SDS((T, KVL), BF16), SDS((T, LANES), BF16),
                   SDS((QL, QFW), F32), SDS((KVL, KVW), F32), SDS((1, QL), F32), SDS((1, KVL), F32)],
        scratch_shapes=[pltpu.VMEM((tm, QFW), BF16), pltpu.VMEM((tm, KVW), BF16)],
        compiler_params=_cp("arbitrary"))(dq, dk, dv, z, z, gq, gkv, wq, wkv, tc, ts1, ts2)


def _bwd_in(pieces, x, g, dres, w, tm, layer):
    T = x.shape[0]
    grid = (T // tm,)
    widths = [pc.shape[1] for pc in pieces]
    assert sum(widths) == ZW
    n_p = len(pieces)

    def body(*refs):
        p_refs, (x_ref, g_ref, r_ref, w_ref, dx_ref, dz_ref, dg_ref) = refs[:n_p], refs[n_p:]

        @pl.when(pl.program_id(0) == 0)
        def _():
            dg_ref[...] = jnp.zeros(dg_ref.shape, F32)

        off = 0
        for ref, wd in zip(p_refs, widths):
            dz_ref[:, off:off + wd] = ref[...].astype(BF16)
            off += wd
        xf, gf = x_ref[...], g_ref[...]
        r = _rstd(xf)
        n = xf * r
        dx, dgr = _norm_bwd(_nt(dz_ref[...], w_ref[...]), n, r, gf)
        dx_ref[...] = r_ref[...] + dx
        dg_ref[...] += jnp.sum(dgr, axis=0, keepdims=True)

    return pl.pallas_call(
        body, grid=grid, name="bwd_in",
        in_specs=[_row(tm, wd) for wd in widths] + [_row(tm, D), _res((1, D), layer), _row(tm, D),
                                                    _res((D, ZW), 0)],
        out_specs=[_row(tm, D), _row(tm, ZW), _acc((1, D))],
        out_shape=[SDS((T, D), F32), SDS((T, ZW), BF16), SDS((1, D), F32)],
        compiler_params=_cp("arbitrary"))(*pieces, x, g, dres, w)


def _wgrad_in(hb, dzb, tm, ride=None):
    T = hb.shape[0]
    half = ZW // 2
    grid = (2, T // tm)

    def body(h_ref, dz_ref, dw_ref):
        @pl.when(pl.program_id(1) == 0)
        def _():
            dw_ref[...] = jnp.zeros(dw_ref.shape, F32)

        dw_ref[...] += _tn(h_ref[...], dz_ref[...])

    r_in, r_out, r_shape, r_scratch, r_args = _ride_args(ride)
    out = pl.pallas_call(
        _riding(ride, body, 2, 1, grid), grid=grid, name="wgrad_in_ride" if ride else "wgrad_in",
        in_specs=[pl.BlockSpec((tm, D), lambda j, t: (t, 0)), pl.BlockSpec((tm, half), lambda j, t: (t, j))] + r_in,
        out_specs=[pl.BlockSpec((D, half), lambda j, t: (0, j))] + r_out,
        out_shape=[SDS((D, ZW), F32)] + r_shape, scratch_shapes=r_scratch,
        compiler_params=_cp("arbitrary", "arbitrary"))(hb, dzb, *r_args)
    return out


IN_PIECES = ((0, 512, Z_AQ), (512, 128, Z_AK), (640, 128, Z_AV), (768, 512, Z_AG), (1280, 256, Z_QD),
             (1536, 128, Z_KVD), (1664, MLA_ROPE, Z_KR + MLA_NOPE), (1696, 512, Z_BG), (2208, 1024, Z_MA),
             (3232, 1024, Z_MB))
WIDE_W = IN_W // N_DEV


def _column_runs():
    runs = []
    for start, width, kstart in IN_PIECES:
        col = start
        while col < start + width:
            dev = col // WIDE_W
            stop = min(start + width, (dev + 1) * WIDE_W)
            runs.append((dev, col - dev * WIDE_W, stop - col, kstart + col - start))
            col = stop
    return runs


def _win_layout(blocks, tm):
    runs = _column_runs()

    def body(g_ref, o_ref):
        o_ref[:, Z_KR:Z_KR + LANES] = jnp.zeros((tm, LANES), o_ref.dtype)
        for dev, lo, n, k in runs:
            o_ref[:, k:k + n] = g_ref[dev, :, lo:lo + n]

    return pl.pallas_call(
        body, grid=(D // tm,), name="win_layout",
        in_specs=[pl.BlockSpec((N_DEV, None, tm, WIDE_W), lambda i: (0, 0, i, 0))],
        out_specs=pl.BlockSpec((None, tm, ZW), lambda i: (0, i, 0)),
        out_shape=SDS((1, D, ZW), blocks.dtype),
        compiler_params=_cp("parallel"))(blocks)


def _win_grad_layout(dw, tm):
    runs = _column_runs()

    def body(g_ref, o_ref):
        for dev, lo, n, k in runs:
            o_ref[dev, :, lo:lo + n] = g_ref[:, k:k + n]

    return pl.pallas_call(
        body, grid=(D // tm,), name="win_grad_layout",
        in_specs=[_row(tm, ZW)],
        out_specs=pl.BlockSpec((N_DEV, None, tm, WIDE_W), lambda i: (0, 0, i, 0)),
        out_shape=SDS((N_DEV, 1, D, WIDE_W), F32),
        compiler_params=_cp("parallel"))(dw)


def _wuq_to_kernel(w):
    w = w.reshape(w.shape[:-1] + (MLA_H, MLA_QK))
    w = jnp.pad(w, [(0, 0)] * (w.ndim - 1) + [(0, LANES - MLA_QK)])
    return w.reshape(w.shape[:-2] + (QFW,))


def _wuq_from_kernel(g):
    g = g.reshape(g.shape[:-1] + (MLA_H, LANES))[..., :MLA_QK]
    return g.reshape(g.shape[:-2] + (MLA_H * MLA_QK,))


def _wukv_to_kernel(w):
    w = w.reshape(w.shape[:-1] + (MLA_H, MLA_NOPE + MLA_V))
    k = jnp.pad(w[..., :MLA_NOPE], [(0, 0)] * (w.ndim - 1) + [(0, LANES - MLA_NOPE)])
    v = w[..., MLA_NOPE:]
    return jnp.concatenate([k.reshape(k.shape[:-2] + (QFW,)), v.reshape(v.shape[:-2] + (MLA_H * MLA_V,))], axis=-1)


def _wukv_from_kernel(g):
    k = g[..., :QFW].reshape(g.shape[:-1] + (MLA_H, LANES))[..., :MLA_NOPE]
    v = g[..., QFW:].reshape(g.shape[:-1] + (MLA_H, MLA_V))
    kv = jnp.concatenate([k, v], axis=-1)
    return kv.reshape(kv.shape[:-2] + (MLA_H * (MLA_NOPE + MLA_V),))


def _rope_tables(pos):
    half = MLA_ROPE // 2
    inv = 10000.0 ** (-jnp.arange(0, MLA_ROPE, 2, dtype=F32) / MLA_ROPE)
    ang = pos.astype(F32)[:, None] * inv
    cos, sin = jnp.cos(ang), jnp.sin(ang)
    one = jnp.ones((pos.shape[0], MLA_NOPE), F32)
    zero = lambda n: jnp.zeros((pos.shape[0], n), F32)
    tc = jnp.concatenate([one, cos, cos, one[:, :LANES - MLA_QK]], axis=1)
    ts1 = jnp.concatenate([zero(MLA_NOPE + half), sin, zero(LANES - MLA_QK)], axis=1)
    ts2 = jnp.concatenate([zero(MLA_NOPE), -sin, zero(LANES - MLA_NOPE - half)], axis=1)
    return tc, ts1, ts2


def _local_step(x, p, positions, loss_target, small, wts, plan=None):
    nb, seq, _ = x.shape
    T = nb * seq
    tm = min(512, T)
    tl = min(1024, T)
    tq = min(512, seq)
    xf = x.reshape(T, D)
    pos = positions.reshape(T)
    posf = pos.astype(F32)
    pos_col, pos_row = posf.reshape(T, 1), posf.reshape(T // SWA_STEP, 1, SWA_STEP)
    tc, ts1, ts2 = _rope_tables(pos)

    wts, sm = list(wts), small
    pl_in = p.reshape(DEPTH, T, PLE)
    saved = []
    for i in range(DEPTH):
        riding = plan is not None and i == 0
        w = wts[i]
        z, hb, *got = _fwd_in(xf, sm["g_mix"], w["w_in"], tm, i,
                              ride=_Ride("gather", plan["behind_fwd_in"]) if riding else None)
        if riding:
            w = wts[0] = dict(w, **plan["row_weights"](got[0]))
        oa, lse_a, *wide1 = _swa_fwd(sm["sink"], z, pos_col, pos_row, nb, seq, i,
                                     ride=_Ride("gather", plan["behind_swa_fwd"]) if riding else None)
        qf, kf, v = _fwd_prep(z, sm["g_q"], sm["g_kv"], w["w_uq"], w["w_ukv"], tc, ts1, ts2, tl, i)
        ob, lse_b, *got = _mla_fwd(qf, kf, v, nb, seq, tq, gather=plan["behind_mla_fwd"] if riding else ())
        if riding:
            wts.append(dict(w_in=plan["w_in"](wide1[0]), **plan["row_weights"](got[0])))
        x1 = _fwd_merge(xf, oa, ob, z, w["w_br_a"], w["w_br_b"], w["w_out"], tm, i)
        saved.append(dict(x=xf, z=z, hb=hb, oa=oa, lse_a=lse_a, qf=qf, kf=kf, v=v, ob=ob, lse_b=lse_b, x1=x1))
        if i < DEPTH - 1:
            xf, saved[i]["pg"], saved[i]["pp"] = _fwd_ple(x1, pl_in, sm["g_ple"], w["w_ple_gate"], w["w_ple_proj"],
                                                          tl, i)

    last = _ple_loss(x1, pl_in, sm["g_ple"], w["w_ple_gate"], w["w_ple_proj"], small["g_final"],
                     loss_target.reshape(T, D), tm, DEPTH - 1)
    dg_final, loss = last[4], last[5]

    grads = [None] * DEPTH
    exchanged = {}
    for i in reversed(range(DEPTH)):
        riding = plan is not None and i == 0
        sv, w = saved[i], wts[i]
        pay = plan["payload"](grads[1]) if riding else []
        if i == DEPTH - 1:
            (dx1, dwpg, dwpp, dg_ple), got = last[:4], []
        else:
            dx1, dwpg, dwpp, dg_ple, *got = _bwd_ple(dx, sv["x1"], sv["pg"], sv["pp"], pl_in, sm["g_ple"],
                                                     w["w_ple_gate"], tm, i,
                                                     ride=_Ride("swap", pay) if riding else None)
        doa, dob, dag, dbg, dma, dmb, dsum_a, dsum_b, dwa, dwb, dwo = _bwd_merge(
            dx1, sv["oa"], sv["ob"], sv["z"], w["w_br_a"], w["w_br_b"], w["w_out"], tm, i)
        stats = (MLA_H // MLA_BWD_HEADS, MLA_BWD_HEADS, T)
        dq_b, dk_b, dv_b, *exchanged["layer1"] = _mla_bwd(
            sv["qf"], sv["kf"], sv["v"], dob, sv["lse_b"].reshape(stats), dsum_b.reshape(stats), nb, seq, tq,
            exchange=plan["add"](pay, got) if riding else ())
        dqd, dkvd, dkr, dwq, dwkv, dgq, dgkv = _bwd_prep(dq_b, dk_b, dv_b, sv["z"], sm["g_q"], sm["g_kv"],
                                                         w["w_uq"], w["w_ukv"], tc, ts1, ts2, tl, i)
        g = dict(w_uq=_wuq_from_kernel(dwq), w_ukv=_wukv_from_kernel(dwkv), w_br_a=dwa, w_br_b=dwb, w_out=dwo,
                 w_ple_gate=dwpg, w_ple_proj=dwpp)
        pay = [plan["rows_payload"](g)] if riding else []
        dq_a, dkv_a, dsink, *got = _swa_bwd(sm["sink"], sv["z"], pos_col, pos_row, doa, sv["lse_a"], dsum_a, nb, seq,
                                            i, ride=_Ride("swap", pay) if riding else None)
        dx, dzb, dg_mix = _bwd_in([dma, dmb, dq_a, dag, dbg, dqd, dkv_a, dkvd, dkr], sv["x"], sm["g_mix"], dx1,
                                  w["w_in"], tm, i)
        dwin, *exchanged["rows0"] = _wgrad_in(sv["hb"], dzb, tm,
                                              ride=_Ride("exchange", plan["add"](pay, got)) if riding else None)
        g.update(g_mix=dg_mix[0], w_in=dwin, sink=dsink[0, :SWA_H], g_q=dgq[0], g_kv=dgkv[0], g_ple=dg_ple[0])
        grads[i] = g
    return loss, dx.reshape(nb, seq, D), grads, dg_final[0], exchanged


def _row_weights(rows):
    blocks = _unpack_rows(rows)
    out = {n: _join(n, blocks[n]) for n, _ in ROWS_PIECES}
    out.update(w_uq=_wuq_to_kernel(out["w_uq"]), w_ukv=_wukv_to_kernel(out["w_ukv"]))
    return out


def _small_params(g_mix, sink, g_q, g_kv, g_ple, g_final):
    return dict(g_mix=g_mix[:, None], sink=sink, g_q=g_q[:, None], g_kv=g_kv[:, None], g_ple=g_ple[:, None],
                g_final=g_final[None])


UQ_W = MLA_H * MLA_QK // N_DEV
ROWS_PIECES = (("w_uq", QL), ("w_ukv", KVL), ("w_br_a", 512), ("w_br_b", 512), ("w_out", D), ("w_ple_gate", D),
               ("w_ple_proj", PLE))
SMALL = (("g_mix", (DEPTH, D)), ("sink", (DEPTH, SWA_H)), ("g_q", (DEPTH, QL)), ("g_kv", (DEPTH, KVL)),
         ("g_ple", (DEPTH, D)), ("g_final", (D,)))
VEC_ROWS = 48
ROWS_N = sum(r for _, r in ROWS_PIECES)
WIDE_TILE, ROWS_TILE = 256, ROWS_N // 2


def _to_rows(name, a):
    if name == "w_uq":
        a = jnp.pad(a, [(0, 0)] * (a.ndim - 1) + [(0, LANES - UQ_W)])
    return a.reshape(a.shape[:-2] + (-1, LANES))


def _from_rows(name, r):
    if name in ("w_out", "w_ple_gate"):
        return r.reshape(r.shape[:-2] + (D // N_DEV, D))
    return r[..., :UQ_W] if name == "w_uq" else r


def _pack_rows(blocks):
    return jnp.concatenate([_to_rows(n, blocks[n]) for n, _ in ROWS_PIECES], axis=-2)


def _unpack_rows(rows):
    blocks, off = {}, 0
    for n, r in ROWS_PIECES:
        blocks[n] = _from_rows(n, rows[..., off:off + r, :])
        off += r
    return blocks


def _pack_vec(vectors, loss=None):
    parts = [vectors[n].reshape(-1) for n, _ in SMALL] + ([] if loss is None else [loss.reshape(1)])
    vec = jnp.concatenate(parts)
    return jnp.pad(vec, (0, VEC_ROWS * LANES - vec.shape[0])).reshape(1, VEC_ROWS, LANES)


def _unpack_vec(vec):
    vec = vec.reshape(-1)
    vectors, off = {}, 0
    for n, shp in SMALL:
        size = 1
        for s in shp:
            size *= s
        vectors[n] = vec[off:off + size].reshape(shp)
        off += size
    return vectors, vec[off]


def _join(name, blocks):
    if name in ("w_out", "w_ple_gate"):
        return jnp.moveaxis(blocks, 0, 1).reshape(blocks.shape[1], -1, blocks.shape[-1])
    return jnp.moveaxis(blocks, 0, 2).reshape(blocks.shape[1], blocks.shape[2], -1)


def _split(name, full):
    if name in ("w_out", "w_ple_gate"):
        return jnp.moveaxis(full.reshape(full.shape[0], N_DEV, -1, full.shape[-1]), 1, 0)
    return jnp.moveaxis(full.reshape(full.shape[0], full.shape[1], N_DEV, -1), 2, 0)


MESH_ID = pl.DeviceIdType.MESH
ANY = pl.BlockSpec(memory_space=pl.ANY)


def _place():
    return lax.axis_index("x"), lax.axis_index("y"), lax.axis_index("c")


def _all_gather(blocks):
    n = len(blocks)

    def body(*refs):
        start, forward, finish = _gather_phases(refs[:n], refs[n:2 * n], *refs[2 * n:])
        start()
        forward()
        finish()

    return pl.pallas_call(
        body, name="all_gather_weights", out_shape=_gather_out(blocks),
        in_specs=[ANY] * n, out_specs=[ANY] * n, scratch_shapes=_gather_sems(n))(*blocks)


def _gather_out(blocks):
    return [SDS((N_DEV,) + b.shape, b.dtype) for b in blocks]


def _gather_sems(n):
    return [pltpu.SemaphoreType.DMA((7 * n,)), pltpu.SemaphoreType.DMA((7 * n,)), pltpu.SemaphoreType.DMA((n,))]


def _gather_phases(x_refs, out_refs, send_sems, recv_sems, local_sems):
    n = len(x_refs)
    x, y, c = _place()
    me, sibling = (x, y, c), (x, y, 1 - c)
    chips = [(1 - x, y), (x, 1 - y), (1 - x, 1 - y)]

    def slot(a, px, py, pc):
        return out_refs[a].at[4 * px + 2 * py + pc]

    def copy(a, k, blk, to, src=None):
        return pltpu.make_async_remote_copy(
            src_ref=slot(a, *blk) if src is None else src, dst_ref=slot(a, *blk),
            send_sem=send_sems.at[7 * a + k], recv_sem=recv_sems.at[7 * a + k], device_id=to,
            device_id_type=MESH_ID)

    def mine():
        return [pltpu.make_async_copy(x_refs[a], slot(a, *me), local_sems.at[a]) for a in range(n)]

    def first():
        out = []
        for a in range(n):
            out += [copy(a, 0, me, sibling, src=x_refs[a])]
            out += [copy(a, 1 + j, me, (*chip, c), src=x_refs[a]) for j, chip in enumerate(chips)]
        return out

    def passed():
        return [copy(a, 4 + j, (*chip, c), sibling) for j, chip in enumerate(chips) for a in range(n)]

    def start():
        for cp in mine() + first():
            cp.start()

    def forward():
        for j, chip in enumerate(chips):
            for a in range(n):
                copy(a, 1 + j, (*chip, c), me).wait_recv()
                copy(a, 4 + j, (*chip, c), sibling).start()

    def finish():
        for a in range(n):
            copy(a, 0, sibling, me).wait_recv()
            for j, chip in enumerate(chips):
                copy(a, 4 + j, (*chip, 1 - c), me).wait_recv()
        for cp in first() + passed():
            cp.wait_send()
        for cp in mine():
            cp.wait()

    return start, forward, finish


def _swap_sibling(arrs):
    n = len(arrs)

    def body(*refs):
        start, finish = _swap_phases(refs[:n], refs[n:2 * n], *refs[2 * n:])
        start()
        finish()

    return pl.pallas_call(
        body, name="swap_sibling", out_shape=[SDS((a.shape[0],) + a.shape[2:], a.dtype) for a in arrs],
        in_specs=[ANY] * n, out_specs=[ANY] * n, scratch_shapes=_swap_sems(n))(*arrs)


def _swap_sems(n):
    return [pltpu.SemaphoreType.DMA((n,)), pltpu.SemaphoreType.DMA((n,))]


def _swap_phases(a_refs, out_refs, send_sems, recv_sems):
    x, y, c = _place()

    def copies():
        return [pltpu.make_async_remote_copy(
            src_ref=a_refs[a].at[:, 1 - c], dst_ref=out_refs[a], send_sem=send_sems.at[a], recv_sem=recv_sems.at[a],
            device_id=(x, y, 1 - c), device_id_type=MESH_ID) for a in range(len(a_refs))]

    def start():
        for cp in copies():
            cp.start()

    def finish():
        for cp in copies():
            cp.wait()

    return start, finish


def _exchange_chips(arrs):
    n = len(arrs)

    def body(*refs):
        start, finish = _exchange_phases(refs[:n], refs[n:2 * n], *refs[2 * n:])
        start()
        finish()

    return pl.pallas_call(
        body, name="exchange_chips", out_shape=[SDS(a.shape, a.dtype) for a in arrs],
        in_specs=[ANY] * n, out_specs=[ANY] * n, scratch_shapes=_exchange_sems(n))(*arrs)


def _exchange_sems(n):
    return [pltpu.SemaphoreType.DMA((3 * n,)), pltpu.SemaphoreType.DMA((3 * n,)), pltpu.SemaphoreType.DMA((n,))]


def _exchange_phases(p_refs, out_refs, send_sems, recv_sems, local_sems):
    n = len(p_refs)
    x, y, c = _place()
    mine = 2 * x + y
    peers = [(1 - x, y), (x, 1 - y), (1 - x, 1 - y)]

    def local():
        return [pltpu.make_async_copy(p_refs[a].at[mine], out_refs[a].at[mine], local_sems.at[a]) for a in range(n)]

    def copy(a, j, src_chip, dst_chip):
        px, py = peers[j]
        return pltpu.make_async_remote_copy(
            src_ref=p_refs[a].at[src_chip], dst_ref=out_refs[a].at[dst_chip], send_sem=send_sems.at[3 * a + j],
            recv_sem=recv_sems.at[3 * a + j], device_id=(px, py, c), device_id_type=MESH_ID)

    def sends():
        return [copy(a, j, 2 * px + py, mine) for a in range(n) for j, (px, py) in enumerate(peers)]

    def start():
        for cp in local() + sends():
            cp.start()

    def finish():
        for a in range(n):
            for j, (px, py) in enumerate(peers):
                copy(a, j, mine, 2 * px + py).wait_recv()
        for cp in sends():
            cp.wait_send()
        for cp in local():
            cp.wait()

    return start, finish


def _add_mine(g, recv, core, tile, dtype):
    _, _, lead, rows, width = g.shape

    def body(c_ref, g_ref, r_ref, o_ref):
        o_ref[...] = (g_ref[...] + r_ref[...]).astype(dtype)

    spec = pl.BlockSpec((None, None, tile, width), lambda k, l, i, c_ref: (k, l, i, 0))
    return pl.pallas_call(
        body, name="add_sibling", out_shape=SDS(recv.shape, dtype),
        grid_spec=pltpu.PrefetchScalarGridSpec(
            num_scalar_prefetch=1, grid=(g.shape[0], lead, rows // tile),
            in_specs=[pl.BlockSpec((None, None, None, tile, width), lambda k, l, i, c_ref: (k, c_ref[0], l, i, 0)),
                      spec],
            out_specs=spec),
        compiler_params=_cp("parallel", "parallel", "parallel"))(core, g, recv)


def _sum_adamw(parts, w, m, v, tile):
    lead, rows, width = w.shape
    last = rows // tile - 1

    def body(*refs):
        p_refs, (w_ref, m_ref, v_ref, g_ref, d_ref, nm_ref, nv_ref) = refs[:lead], refs[lead:]
        for layer in range(lead):
            @pl.when(pl.program_id(0) == layer)
            def _(p_ref=p_refs[layer]):
                g = ((p_ref[0].astype(F32) + p_ref[1].astype(F32)) + p_ref[2].astype(F32)) + p_ref[3].astype(F32)
                nm = ADAM_B1 * m_ref[...] + (1.0 - ADAM_B1) * g
                nv = ADAM_B2 * v_ref[...] + (1.0 - ADAM_B2) * jnp.square(g)
                m_hat = nm / (1.0 - ADAM_B1 ** ADAM_STEP)
                v_hat = nv / (1.0 - ADAM_B2 ** ADAM_STEP)
                g_ref[...] = g
                nm_ref[...] = nm
                nv_ref[...] = nv
                d_ref[...] = -ADAM_LR * (m_hat / (jnp.sqrt(v_hat) + ADAM_EPS) + ADAM_WD * w_ref[...])

    pspec = lambda layer: pl.BlockSpec(
        (4, None, tile, width),
        lambda l, i: (0, 0, jnp.where(l == layer, i, jnp.where(l > layer, last, 0)), 0))
    spec = pl.BlockSpec((None, tile, width), lambda l, i: (l, i, 0))
    return pl.pallas_call(
        body, grid=(lead, rows // tile), name="sum_adamw",
        in_specs=[pspec(layer) for layer in range(lead)] + [spec, spec, spec],
        out_specs=[spec] * 4, out_shape=[SDS((lead, rows, width), F32)] * 4,
        compiler_params=_cp("arbitrary", "arbitrary"))(*parts, w, m, v)


def kernel(x, p, positions, g_mix, w_in, sink, g_q, w_uq, g_kv, w_ukv, w_br_a, w_br_b, w_out, g_ple, w_ple_gate, w_ple_proj, g_final, loss_target, m_g_mix, m_w_in, m_sink, m_g_q, m_w_uq, m_g_kv, m_w_ukv, m_w_br_a, m_w_br_b, m_w_out, m_g_ple, m_w_ple_gate, m_w_ple_proj, m_g_final, v_g_mix, v_w_in, v_sink, v_g_q, v_w_uq, v_g_kv, v_w_ukv, v_w_br_a, v_w_br_b, v_w_out, v_g_ple, v_w_ple_gate, v_w_ple_proj, v_g_final):
    weights = dict(g_mix=g_mix, w_in=w_in, sink=sink, g_q=g_q, w_uq=w_uq, g_kv=g_kv, w_ukv=w_ukv, w_br_a=w_br_a,
                   w_br_b=w_br_b, w_out=w_out, g_ple=g_ple, w_ple_gate=w_ple_gate, w_ple_proj=w_ple_proj,
                   g_final=g_final)
    mom1 = dict(g_mix=m_g_mix, w_in=m_w_in, sink=m_sink, g_q=m_g_q, w_uq=m_w_uq, g_kv=m_g_kv, w_ukv=m_w_ukv,
                w_br_a=m_w_br_a, w_br_b=m_w_br_b, w_out=m_w_out, g_ple=m_g_ple, w_ple_gate=m_w_ple_gate,
                w_ple_proj=m_w_ple_proj, g_final=m_g_final)
    mom2 = dict(g_mix=v_g_mix, w_in=v_w_in, sink=v_sink, g_q=v_g_q, w_uq=v_w_uq, g_kv=v_g_kv, w_ukv=v_w_ukv,
                w_br_a=v_w_br_a, w_br_b=v_w_br_b, w_out=v_w_out, g_ple=v_g_ple, w_ple_gate=v_w_ple_gate,
                w_ple_proj=v_w_ple_proj, g_final=v_g_final)
    assert DEPTH == 2
    wide = lambda d: d["w_in"]
    rows = lambda d: _pack_rows(d)
    core = lax.axis_index("c").astype(jnp.int32).reshape(1)

    w16 = [wide(weights).astype(BF16), rows(weights).astype(BF16)]
    wts0 = dict(w_in=_win_layout(_all_gather([w16[0][:1]])[0], 256))
    small = _small_params(g_mix, sink, g_q, g_kv, g_ple, g_final)

    def wide_payload(g):
        return _win_grad_layout(g["w_in"], 256).reshape(N_DEV // 2, 2, 1, D, WIDE_W)

    def rows_payload(g):
        return _pack_rows({n: _split(n, g[n][None]) for n, _ in ROWS_PIECES}).reshape(N_DEV // 2, 2, 1, ROWS_N, LANES)

    def add(pay, got):
        tiles = {D: (WIDE_TILE, BF16), ROWS_N: (ROWS_TILE, BF16), VEC_ROWS: (VEC_ROWS, F32)}
        return [_add_mine(a, b, core, *tiles[a.shape[-2]]) for a, b in zip(pay, got)]

    plan = dict(behind_fwd_in=[w16[1][:1]], behind_swa_fwd=[w16[0][1:]], behind_mla_fwd=[w16[1][1:]],
                w_in=lambda blocks: _win_layout(blocks, 256),
                row_weights=_row_weights, payload=lambda g: [wide_payload(g), rows_payload(g)],
                rows_payload=rows_payload, add=add)
    loss, grad_x, grads, dg_final, rode = _local_step(x, p, positions, loss_target, small, [wts0], plan)

    vectors = {n: jnp.stack([grads[i][n] for i in range(DEPTH)]) for n, _ in SMALL[:-1]}
    vectors["g_final"] = dg_final
    pay = [wide_payload(grads[0]),
           jnp.broadcast_to(_pack_vec(vectors, loss[0, 0]), (N_DEV // 2, 2, 1, VEC_ROWS, LANES))]
    parts_wide0, parts_vec = _exchange_chips(add(pay, _swap_sibling(pay)))
    out_wide = _sum_adamw([parts_wide0, rode["layer1"][0]], wide(weights), wide(mom1), wide(mom2), WIDE_TILE)
    out_rows = _sum_adamw([rode["rows0"][0], rode["layer1"][1]], rows(weights), rows(mom1), rows(mom2), ROWS_TILE)
    out_vec = _sum_adamw([parts_vec], _pack_vec(weights), _pack_vec(mom1), _pack_vec(mom2), VEC_ROWS)

    outs = []
    for ow, orow, ovec in zip(out_wide, out_rows, out_vec):
        named = _unpack_rows(orow)
        named.update(_unpack_vec(ovec)[0])
        named["w_in"] = ow
        outs += [named[n] for n in weights]
    loss = _unpack_vec(out_vec[0])[1]
    return (loss, grad_x, *outs)
```

```python
import functools

import jax
import jax.numpy as jnp
from jax import lax
from jax.experimental import pallas as pl
from jax.experimental.pallas import tpu as pltpu

F32, BF16 = jnp.float32, jnp.bfloat16
SDS = jax.ShapeDtypeStruct

D = 1024
DEPTH = 2
PLE = 256
BLK = 128
EPS = 1e-6
NEG = -1e30
SWA_H, SWA_KV, SWA_DH = 8, 2, 64
MLA_H, MLA_NOPE, MLA_ROPE, MLA_V = 8, 64, 32, 64
MLA_QK = MLA_NOPE + MLA_ROPE
QL, KVL = 256, 128
IN_W = 4256
N_DEV = 8

V7X_VMEM_BYTES = 64 * 1024 * 1024
LANES = 128
VMEM_LIMIT = V7X_VMEM_BYTES * 7 // 8

ZW = 4352
Z_MA, Z_MB, Z_AQ, Z_AG, Z_BG, Z_QD, Z_AK, Z_AV, Z_KVD, Z_KR = 0, 1024, 2048, 2560, 3072, 3584, 3840, 3968, 4096, 4224
QFW = MLA_H * LANES
KVW = QFW + MLA_H * MLA_V
MLA_SCALE = MLA_QK ** -0.5
LOG2E = 1.4426950408889634
MLA_FWD_HEADS, MLA_BWD_HEADS = 8, 4
SWA_SCALE = SWA_DH ** -0.5
ROLL_UP, ROLL_DOWN = MLA_ROPE // 2, LANES - MLA_ROPE // 2

ADAM_LR, ADAM_B1, ADAM_B2, ADAM_EPS, ADAM_WD, ADAM_STEP = 0.001, 0.9, 0.999, 1e-08, 0.01, 10


def _cp(*sem):
    return pltpu.CompilerParams(dimension_semantics=sem, vmem_limit_bytes=VMEM_LIMIT)


def _row(tm, w, col=0):
    return pl.BlockSpec((tm, w), lambda i: (i, col))


def _res(shape, layer=None):
    if layer is None:
        return pl.BlockSpec(shape, lambda *_: (0,) * len(shape), pipeline_mode=pl.Buffered(1))
    return pl.BlockSpec((None,) + shape, lambda *_: (layer,) + (0,) * len(shape), pipeline_mode=pl.Buffered(1))


def _acc(shape):
    return pl.BlockSpec(shape, lambda *_: (0,) * len(shape))


def _rstd(xf):
    return lax.rsqrt(jnp.mean(xf * xf, axis=-1, keepdims=True) + EPS)


def _norm_bwd(dh, n, r, g):
    dn = dh * g
    return r * (dn - n * jnp.mean(dn * n, axis=-1, keepdims=True)), dh * n


def _nt(a, b):
    return lax.dot_general(a, b, (((1,), (1,)), ((), ())), preferred_element_type=F32)


def _tn(a, b):
    return lax.dot_general(a, b, (((0,), (0,)), ((), ())), preferred_element_type=F32)


def _nn(a, b):
    return jnp.dot(a, b, preferred_element_type=F32)


def _sig(x):
    return jax.nn.sigmoid(x)


def _rope(t, c, s1, s2):
    return t * c + pltpu.roll(t, ROLL_UP, 1) * s1 + pltpu.roll(t, ROLL_DOWN, 1) * s2


def _rope_t(d, c, s1, s2):
    return d * c + pltpu.roll(d * s1, ROLL_DOWN, 1) + pltpu.roll(d * s2, ROLL_UP, 1)


def _fwd_in(x, g, w, tm, layer, ride=None):
    T = x.shape[0]
    grid = (T // tm,)

    def body(x_ref, g_ref, w_ref, z_ref, h_ref):
        xf = x_ref[...]
        h = ((xf * _rstd(xf)) * g_ref[...]).astype(BF16)
        h_ref[...] = h
        z_ref[...] = _nn(h, w_ref[...])

    r_in, r_out, r_shape, r_scratch, r_args = _ride_args(ride)
    return pl.pallas_call(
        _riding(ride, body, 3, 2, grid), grid=grid, name="fwd_in_ride" if ride else "fwd_in",
        in_specs=[_row(tm, D), _res((1, D), layer), _res((D, ZW), 0)] + r_in,
        out_specs=[_row(tm, ZW), _row(tm, D)] + r_out,
        out_shape=[SDS((T, ZW), F32), SDS((T, D), BF16)] + r_shape, scratch_shapes=r_scratch,
        compiler_params=_cp("arbitrary"))(x, g, w, *r_args)


def _fwd_prep(z, gq, gkv, wq, wkv, tc, ts1, ts2, tm, layer):
    T = z.shape[0]

    def body(qd_ref, kvd_ref, kr_ref, gq_ref, gkv_ref, wq_ref, wkv_ref, c_ref, s1_ref, s2_ref, q_ref, k_ref, v_ref):
        qd, kvd = qd_ref[...], kvd_ref[...]
        hq = ((qd * _rstd(qd)) * gq_ref[...]).astype(BF16)
        hkv = ((kvd * _rstd(kvd)) * gkv_ref[...]).astype(BF16)
        qf = _nn(hq, wq_ref[...])
        kvf = _nn(hkv, wkv_ref[...])
        c, s1, s2 = c_ref[...], s1_ref[...], s2_ref[...]
        krb = _rope(kr_ref[...], c, s1, s2)
        for h in range(MLA_H):
            sl = slice(LANES * h, LANES * (h + 1))
            q_ref[:, sl] = _rope(qf[:, sl], c, s1, s2).astype(BF16)
            k_ref[:, sl] = (kvf[:, sl] + krb).astype(BF16)
        v_ref[...] = kvf[:, QFW:].astype(BF16)

    return pl.pallas_call(
        body, grid=(T // tm,), name="fwd_prep",
        in_specs=[_row(tm, QL, Z_QD // QL), _row(tm, KVL, Z_KVD // KVL), _row(tm, LANES, Z_KR // LANES),
                  _res((1, QL), layer), _res((1, KVL), layer), _res((QL, QFW), 0), _res((KVL, KVW), 0),
                  _row(tm, LANES), _row(tm, LANES), _row(tm, LANES)],
        out_specs=[_row(tm, QFW), _row(tm, QFW), _row(tm, MLA_H * MLA_V)],
        out_shape=[SDS((T, QFW), BF16), SDS((T, QFW), BF16), SDS((T, MLA_H * MLA_V), BF16)],
        compiler_params=_cp("parallel"))(z, z, z, gq, gkv, wq, wkv, tc, ts1, ts2)


def _grid_ends(grid):
    ids = [pl.program_id(a) for a in range(len(grid))]
    inner_first = functools.reduce(jnp.logical_and, [i == 0 for i in ids[1:]], True)
    last = functools.reduce(jnp.logical_and, [i == g - 1 for i, g in zip(ids, grid)])
    return (ids[0] == 0) & inner_first, (ids[0] == 3 * grid[0] // 4) & inner_first, last


class _Ride:
    def __init__(self, kind, arrays):
        self.kind, self.arrays, self.n = kind, list(arrays), len(arrays)

    def out_shape(self):
        if self.kind == "gather":
            return _gather_out(self.arrays)
        if self.kind == "swap":
            return [SDS((a.shape[0],) + a.shape[2:], a.dtype) for a in self.arrays]
        return [SDS(a.shape, a.dtype) for a in self.arrays]

    def sems(self):
        if self.kind == "gather":
            return _gather_sems(self.n)
        if self.kind == "swap":
            return _swap_sems(self.n)
        return _exchange_sems(self.n)

    def phases(self, in_refs, out_refs, *sems):
        if self.kind == "gather":
            return _gather_phases(in_refs, out_refs, *sems)
        start, finish = (_swap_phases if self.kind == "swap" else _exchange_phases)(in_refs, out_refs, *sems)
        return start, None, finish


def _riding(ride, body, n_in, n_out, grid):
    if ride is None:
        return body
    n, n_sems = ride.n, len(ride.sems())

    def wrapped(*refs):
        ins, r_in = refs[:n_in], refs[n_in:n_in + n]
        outs, r_out = refs[n_in + n:n_in + n + n_out], refs[n_in + n + n_out:n_in + 2 * n + n_out]
        rest = refs[n_in + 2 * n + n_out:]
        scratch, sems = rest[:len(rest) - n_sems], rest[len(rest) - n_sems:]
        start, middle, finish = ride.phases(r_in, r_out, *sems)
        at_first, at_middle, at_last = _grid_ends(grid)
        pl.when(at_first)(start)
        if middle is not None:
            pl.when(at_middle)(middle)
        body(*ins, *outs, *scratch)
        pl.when(at_last)(finish)

    return wrapped


def _ride_args(ride):
    if ride is None:
        return [], [], [], [], []
    return [ANY] * ride.n, [ANY] * ride.n, ride.out_shape(), ride.sems(), ride.arrays


def _mla_fwd(qf, kf, v, nb, seq, tq, gather=()):
    T = qf.shape[0]
    nq = seq // tq
    hp = MLA_FWD_HEADS
    pw = hp * LANES
    pairs = [(qi, ki) for qi in range(nq) for ki in range(qi + 1)]
    qi_tab = jnp.array([qk[0] for qk in pairs], jnp.int32)
    ki_tab = jnp.array([qk[1] for qk in pairs], jnp.int32)
    grid = (nb, MLA_H // hp, len(pairs))
    n_g = len(gather)

    def body(qi_ref, ki_ref, q_ref, k_ref, v_ref, *rest):
        x_refs, (o_ref, lse_ref), got_refs = rest[:n_g], rest[n_g:n_g + 2], rest[n_g + 2:2 * n_g + 2]
        (m_s, l_s, acc_s), sems = rest[2 * n_g + 2:2 * n_g + 5], rest[2 * n_g + 5:]
        qi, ki = qi_ref[pl.program_id(2)], ki_ref[pl.program_id(2)]
        if n_g:
            start, forward, finish = _gather_phases(x_refs, got_refs, *sems)
            at_first, at_middle, at_last = _grid_ends(grid)
            pl.when(at_first)(start)
            pl.when(at_middle)(forward)

        @pl.when(ki == 0)
        def _():
            m_s[...] = jnp.full(m_s.shape, NEG, F32)
            l_s[...] = jnp.zeros(l_s.shape, F32)
            acc_s[...] = jnp.zeros(acc_s.shape, F32)

        def step(masked):
            parts = [(0, tq // 2, tq // 2), (tq // 2, tq, tq)] if masked else [(0, tq, tq)]
            work = [(j, a, b, kh) for j in range(hp) for a, b, kh in parts]
            ss = []
            for j, a, b, kh in work:
                wide = slice(LANES * j, LANES * (j + 1))
                s = _nt(k_ref[:kh, wide], q_ref[a:b, wide]) * (MLA_SCALE * LOG2E)
                if masked:
                    keys = lax.broadcasted_iota(jnp.int32, (kh, b - a), 0)
                    queries = a + lax.broadcasted_iota(jnp.int32, (kh, b - a), 1)
                    s = jnp.where(keys <= queries, s, NEG)
                ss.append(s)
            ps, alphas = [], []
            for (j, a, b, kh), s in zip(work, ss):
                m_prev = m_s[j, :, a:b]
                m_new = jnp.maximum(m_prev, jnp.max(s, axis=0, keepdims=True))
                alpha = jnp.exp2(m_prev - m_new)
                p = jnp.exp2(s - m_new)
                l_s[j, :, a:b] = alpha * l_s[j, :, a:b] + jnp.sum(p, axis=0, keepdims=True)
                m_s[j, :, a:b] = m_new
                ps.append(p.astype(BF16))
                alphas.append(alpha)
            for (j, a, b, kh), p, alpha in zip(work, ps, alphas):
                rows = slice(MLA_V * j, MLA_V * (j + 1))
                acc_s[rows, a:b] = alpha * acc_s[rows, a:b] + _tn(v_ref[:kh, rows], p)

        @pl.when(ki < qi)
        def _():
            step(False)

        @pl.when(ki == qi)
        def _():
            step(True)
            for j in range(hp):
                rows = slice(MLA_V * j, MLA_V * (j + 1))
                acc_s[rows, :] = acc_s[rows, :] / l_s[j]
                lse_ref[j:j + 1, :] = m_s[j] + jnp.log2(l_s[j])
            o_ref[...] = acc_s[...].T

        if n_g:
            pl.when(at_last)(finish)

    q_map = lambda b, g, s, qi_ref, ki_ref: (b * nq + qi_ref[s], g)
    kv_map = lambda b, g, s, qi_ref, ki_ref: (b * nq + ki_ref[s], g)
    return pl.pallas_call(
        body, name="mla_fwd_gather" if n_g else "mla_fwd",
        grid_spec=pltpu.PrefetchScalarGridSpec(
            num_scalar_prefetch=2, grid=grid,
            in_specs=[pl.BlockSpec((tq, pw), q_map), pl.BlockSpec((tq, pw), kv_map),
                      pl.BlockSpec((tq, hp * MLA_V), kv_map)] + [ANY] * n_g,
            out_specs=[pl.BlockSpec((tq, hp * MLA_V), q_map),
                       pl.BlockSpec((hp, tq), lambda b, g, s, qi_ref, ki_ref: (g, b * nq + qi_ref[s]))]
            + [ANY] * n_g,
            scratch_shapes=[pltpu.VMEM((hp, 1, tq), F32), pltpu.VMEM((hp, 1, tq), F32),
                            pltpu.VMEM((hp * MLA_V, tq), F32)]
            + (_gather_sems(n_g) if n_g else [])),
        out_shape=[SDS((T, MLA_H * MLA_V), F32), SDS((MLA_H, T), F32)] + _gather_out(gather),
        compiler_params=_cp("arbitrary", "arbitrary", "arbitrary"))(qi_tab, ki_tab, qf, kf, v, *gather)


SWA_STEP = 2 * BLK


def _swa_specs(nstep):
    cur = lambda b, m: (b * nstep + m, 0)
    prev = lambda b, m: (2 * b * nstep + jnp.maximum(2 * m - 1, 0), 0)
    kvc = Z_AK // (2 * BLK)
    return [pl.BlockSpec(memory_space=pltpu.SMEM),
            pl.BlockSpec((SWA_STEP, 512), lambda b, m: (b * nstep + m, Z_AQ // 512)),
            pl.BlockSpec((SWA_STEP, 2 * BLK), lambda b, m: (b * nstep + m, kvc)),
            pl.BlockSpec((BLK, 2 * BLK), lambda b, m: (2 * b * nstep + jnp.maximum(2 * m - 1, 0), kvc)),
            pl.BlockSpec((SWA_STEP, 1), cur),
            pl.BlockSpec((BLK, 1), prev),
            pl.BlockSpec((1, 1, SWA_STEP), lambda b, m: (b * nstep + m, 0, 0))]


def _swa_scores(m, q_ref, kvc_ref, kvp_ref, pcc_ref, pcp_ref, pr_ref):
    kv = jnp.concatenate([kvp_ref[...], kvc_ref[...]], axis=0)
    kb, vb = kv[:, :BLK].astype(BF16), kv[:, BLK:].astype(BF16)
    pos_keys = jnp.concatenate([pcp_ref[...], pcc_ref[...]], axis=0)
    key = lax.broadcasted_iota(jnp.int32, (2 * BLK, BLK), 0)
    qry = lax.broadcasted_iota(jnp.int32, (2 * BLK, BLK), 1)
    in_window = (key > qry) & (key <= qry + BLK)
    valid = [in_window & ((key >= BLK) | (m > 0)), in_window]
    dist = [pr_ref[0][:, BLK * u:BLK * (u + 1)] - pos_keys[BLK * u:BLK * (u + 2)] for u in range(2)]

    def band(t, u, g):
        return t[BLK * u:BLK * (u + 2), SWA_DH * g:SWA_DH * (g + 1)]

    def scores(u, h):
        g = h // (SWA_H // SWA_KV)
        qh = q_ref[BLK * u:BLK * (u + 1), SWA_DH * h:SWA_DH * (h + 1)].astype(BF16)
        s = _nt(band(kb, u, g), qh) * (SWA_SCALE * LOG2E) - (2.0 ** -(h + 1) * LOG2E) * dist[u]
        return qh, jnp.where(valid[u], s, NEG)

    return kb, vb, band, scores


def _swa_fwd(sink, z, pos_col, pos_row, nb, seq, layer):
    T = z.shape[0]
    nstep = seq // SWA_STEP
    chains = [(u, h) for u in range(2) for h in range(SWA_H)]

    def body(sink_ref, q_ref, kvc_ref, kvp_ref, pcc_ref, pcp_ref, pr_ref, o_ref, lse_ref):
        kb, vb, band, scores = _swa_scores(pl.program_id(1), q_ref, kvc_ref, kvp_ref, pcc_ref, pcp_ref, pr_ref)
        ss = [scores(u, h)[1] for u, h in chains]
        es, dens = [], []
        for (u, h), s in zip(chains, ss):
            sk = sink_ref[layer, h] * LOG2E
            m = jnp.maximum(jnp.max(s, axis=0, keepdims=True), sk)
            e = jnp.exp2(s - m)
            den = jnp.sum(e, axis=0, keepdims=True) + jnp.exp2(sk - m)
            lse_ref[h:h + 1, BLK * u:BLK * (u + 1)] = m + jnp.log2(den)
            es.append(e.astype(BF16))
            dens.append(den)
        outs = [_tn(band(vb, u, h // (SWA_H // SWA_KV)), e) / den for (u, h), e, den in zip(chains, es, dens)]
        for u in range(2):
            o_ref[BLK * u:BLK * (u + 1), :] = jnp.concatenate(outs[SWA_H * u:SWA_H * (u + 1)], axis=0).T

    return pl.pallas_call(
        body, grid=(nb, nstep), name="swa_fwd",
        in_specs=_swa_specs(nstep),
        out_specs=[pl.BlockSpec((SWA_STEP, 512), lambda b, m: (b * nstep + m, 0)),
                   pl.BlockSpec((SWA_H, SWA_STEP), lambda b, m: (0, b * nstep + m))],
        out_shape=[SDS((T, 512), F32), SDS((SWA_H, T), F32)],
        compiler_params=_cp("parallel", "parallel"))(sink, z, z, z, pos_col, pos_col, pos_row)


def _fwd_merge(x, oa, ob, z, wa, wb, wo, tm, layer):
    T = x.shape[0]

    def body(x_ref, oa_ref, ob_ref, ag_ref, bg_ref, ma_ref, mb_ref, wa_ref, wb_ref, wo_ref, x1_ref):
        ag, bg = ag_ref[...], bg_ref[...]
        ua = _nn((oa_ref[...] * (ag * _sig(ag))).astype(BF16), wa_ref[...])
        ub = _nn((ob_ref[...] * (bg * _sig(bg))).astype(BF16), wb_ref[...])
        y = _sig(ma_ref[...]) * ua + _sig(mb_ref[...]) * ub
        x1_ref[...] = x_ref[...] + _nn(y.astype(BF16), wo_ref[...])

    return pl.pallas_call(
        body, grid=(T // tm,), name="fwd_merge",
        in_specs=[_row(tm, D), _row(tm, 512), _row(tm, 512), _row(tm, 512, Z_AG // 512), _row(tm, 512, Z_BG // 512),
                  _row(tm, D, Z_MA // D), _row(tm, D, Z_MB // D),
                  _res((512, D), 0), _res((512, D), 0), _res((D, D), 0)],
        out_specs=_row(tm, D),
        out_shape=SDS((T, D), F32),
        compiler_params=_cp("parallel"))(x, oa, ob, z, z, z, z, wa, wb, wo)


def _fwd_ple(x1, p, g, wpg, wpp, tm, layer):
    T = x1.shape[0]

    def body(x_ref, p_ref, g_ref, wpg_ref, wpp_ref, x2_ref, pg_ref, pp_ref):
        xf = x_ref[...]
        h1 = ((xf * _rstd(xf)) * g_ref[...]).astype(BF16)
        pg = _sig(_nn(h1, wpg_ref[...]))
        pp = _nn(p_ref[...].astype(BF16), wpp_ref[...])
        pg_ref[...] = pg
        pp_ref[...] = pp
        x2_ref[...] = xf + pg * pp

    return pl.pallas_call(
        body, grid=(T // tm,), name="fwd_ple",
        in_specs=[_row(tm, D), pl.BlockSpec((None, tm, PLE), lambda i: (layer, i, 0)),
                  _res((1, D), layer), _res((D, D), 0), _res((PLE, D), 0)],
        out_specs=[_row(tm, D)] * 3,
        out_shape=[SDS((T, D), F32)] * 3,
        compiler_params=_cp("parallel"))(x1, p, g, wpg, wpp)


def _ple_loss(x1, p, g, wpg, wpp, g_final, tgt, tm, layer):
    T = x1.shape[0]

    def body(x_ref, p_ref, g_ref, wpg_ref, wpp_ref, gf_ref, t_ref, dx_ref, dwg_ref, dwp_ref, dg_ref, dgf_ref, loss_ref):
        @pl.when(pl.program_id(0) == 0)
        def _():
            for ref in (dwg_ref, dwp_ref, dg_ref, dgf_ref, loss_ref):
                ref[...] = jnp.zeros(ref.shape, F32)

        xf, gp, gf = x_ref[...], g_ref[...], gf_ref[...]
        r = _rstd(xf)
        n = xf * r
        h1 = (n * gp).astype(BF16)
        pb = p_ref[...].astype(BF16)
        pg = _sig(_nn(h1, wpg_ref[...]))
        pp = _nn(pb, wpp_ref[...])
        x2 = xf + pg * pp
        r2 = _rstd(x2)
        n2 = x2 * r2
        err = n2 * gf - t_ref[...]
        loss_ref[...] += 0.5 * jnp.sum(jnp.mean(err * err, axis=-1, keepdims=True), axis=0, keepdims=True)
        d, dgfr = _norm_bwd(err * (1.0 / D), n2, r2, gf)
        dgf_ref[...] += jnp.sum(dgfr, axis=0, keepdims=True)
        dpgl = (d * pp * pg * (1.0 - pg)).astype(BF16)
        dwg_ref[...] += _tn(h1, dpgl)
        dwp_ref[...] += _tn(pb, (d * pg).astype(BF16))
        dxn, dgr = _norm_bwd(_nt(dpgl, wpg_ref[...]), n, r, gp)
        dx_ref[...] = d + dxn
        dg_ref[...] += jnp.sum(dgr, axis=0, keepdims=True)

    return pl.pallas_call(
        body, grid=(T // tm,), name="ple_loss",
        in_specs=[_row(tm, D), pl.BlockSpec((None, tm, PLE), lambda i: (layer, i, 0)), _res((1, D), layer),
                  _res((D, D), 0), _res((PLE, D), 0), _res((1, D)), _row(tm, D)],
        out_specs=[_row(tm, D), _acc((D, D)), _acc((PLE, D)), _acc((1, D)), _acc((1, D)), _acc((1, LANES))],
        out_shape=[SDS((T, D), F32), SDS((D, D), F32), SDS((PLE, D), F32), SDS((1, D), F32), SDS((1, D), F32),
                   SDS((1, LANES), F32)],
        compiler_params=_cp("arbitrary"))(x1, p, g, wpg, wpp, g_final, tgt)


def _bwd_ple(dx2, x1, pg, pp, p, g, wpg, tm, layer, ride=None):
    T = x1.shape[0]
    grid = (T // tm,)

    def body(d_ref, x_ref, pg_ref, pp_ref, p_ref, g_ref, w_ref, dx_ref, dwg_ref, dwp_ref, dg_ref):
        @pl.when(pl.program_id(0) == 0)
        def _():
            dwg_ref[...] = jnp.zeros(dwg_ref.shape, F32)
            dwp_ref[...] = jnp.zeros(dwp_ref.shape, F32)
            dg_ref[...] = jnp.zeros(dg_ref.shape, F32)

        d, xf, pg, gf = d_ref[...], x_ref[...], pg_ref[...], g_ref[...]
        r = _rstd(xf)
        n = xf * r
        dpgl = (d * pp_ref[...] * pg * (1.0 - pg)).astype(BF16)
        dwg_ref[...] += _tn((n * gf).astype(BF16), dpgl)
        dwp_ref[...] += _tn(p_ref[...].astype(BF16), (d * pg).astype(BF16))
        dxn, dgr = _norm_bwd(_nt(dpgl, w_ref[...]), n, r, gf)
        dx_ref[...] = d + dxn
        dg_ref[...] += jnp.sum(dgr, axis=0, keepdims=True)

    r_in, r_out, r_shape, r_scratch, r_args = _ride_args(ride)
    return pl.pallas_call(
        _riding(ride, body, 7, 4, grid), grid=grid, name="bwd_ple_ride" if ride else "bwd_ple",
        in_specs=[_row(tm, D)] * 4 + [pl.BlockSpec((None, tm, PLE), lambda i: (layer, i, 0)),
                                      _res((1, D), layer), _res((D, D), 0)] + r_in,
        out_specs=[_row(tm, D), _acc((D, D)), _acc((PLE, D)), _acc((1, D))] + r_out,
        out_shape=[SDS((T, D), F32), SDS((D, D), F32), SDS((PLE, D), F32), SDS((1, D), F32)] + r_shape,
        scratch_shapes=r_scratch,
        compiler_params=_cp("arbitrary"))(dx2, x1, pg, pp, p, g, wpg, *r_args)


def _bwd_merge(dx1, oa, ob, z, wa, wb, wo, tm, layer):
    T = dx1.shape[0]

    def body(d_ref, oa_ref, ob_ref, ag_ref, bg_ref, ma_ref, mb_ref, wa_ref, wb_ref, wo_ref,
             doa_ref, dob_ref, dag_ref, dbg_ref, dma_ref, dmb_ref, dsa_ref, dsb_ref, dwa_ref, dwb_ref, dwo_ref):
        @pl.when(pl.program_id(0) == 0)
        def _():
            dwa_ref[...] = jnp.zeros(dwa_ref.shape, F32)
            dwb_ref[...] = jnp.zeros(dwb_ref.shape, F32)
            dwo_ref[...] = jnp.zeros(dwo_ref.shape, F32)

        db = d_ref[...].astype(BF16)
        gated = []
        for o_ref, gate_ref, w_ref in ((oa_ref, ag_ref, wa_ref), (ob_ref, bg_ref, wb_ref)):
            raw, gate = o_ref[...], gate_ref[...]
            sg = _sig(gate)
            silu = gate * sg
            ob16 = (raw * silu).astype(BF16)
            gated.append((raw, gate, sg, silu, ob16, _nn(ob16, w_ref[...])))
        ua, ub = gated[0][5], gated[1][5]
        sa, sb = _sig(ma_ref[...]), _sig(mb_ref[...])
        dwo_ref[...] += _tn((sa * ua + sb * ub).astype(BF16), db)
        dy = _nt(db, wo_ref[...])
        dma_ref[...] = (dy * ua * sa * (1.0 - sa)).astype(BF16)
        dmb_ref[...] = (dy * ub * sb * (1.0 - sb)).astype(BF16)
        for (s, w_ref, do_ref, dgate_ref, dw_ref, ds_ref), (raw, gate, sg, silu, ob16, _) in zip((
                (sa, wa_ref, doa_ref, dag_ref, dwa_ref, dsa_ref),
                (sb, wb_ref, dob_ref, dbg_ref, dwb_ref, dsb_ref)), gated):
            du = (dy * s).astype(BF16)
            dw_ref[...] += _tn(ob16, du)
            do = _nt(du, w_ref[...])
            draw = do * silu
            do_ref[...] = draw.astype(BF16)
            dgate_ref[...] = (do * raw * (sg * (1.0 + gate * (1.0 - sg)))).astype(BF16)
            ds_ref[...] = jnp.sum((draw * raw).T.reshape(MLA_H, MLA_V, tm), axis=1)

    return pl.pallas_call(
        body, grid=(T // tm,), name="bwd_merge",
        in_specs=[_row(tm, D), _row(tm, 512), _row(tm, 512), _row(tm, 512, Z_AG // 512), _row(tm, 512, Z_BG // 512),
                  _row(tm, D, Z_MA // D), _row(tm, D, Z_MB // D),
                  _res((512, D), 0), _res((512, D), 0), _res((D, D), 0)],
        out_specs=[_row(tm, 512)] * 4 + [_row(tm, D)] * 2 + [pl.BlockSpec((MLA_H, tm), lambda i: (0, i))] * 2
        + [_acc((512, D)), _acc((512, D)), _acc((D, D))],
        out_shape=[SDS((T, 512), BF16), SDS((T, 512), BF16), SDS((T, 512), BF16), SDS((T, 512), BF16),
                   SDS((T, D), BF16), SDS((T, D), BF16), SDS((MLA_H, T), F32), SDS((MLA_H, T), F32),
                   SDS((512, D), F32), SDS((512, D), F32), SDS((D, D), F32)],
        compiler_params=_cp("arbitrary"))(dx1, oa, ob, z, z, z, z, wa, wb, wo)


def _mla_bwd(qf, kf, v, do, lse, dsum, nb, seq, tq, exchange=()):
    T = qf.shape[0]
    nq = seq // tq
    hp = MLA_BWD_HEADS
    pw = hp * LANES
    pairs = [(qi, ki) for ki in range(nq) for qi in range(ki, nq)]
    qi_tab = jnp.array([qk[0] for qk in pairs], jnp.int32)
    ki_tab = jnp.array([qk[1] for qk in pairs], jnp.int32)
    grid = (nb, MLA_H // hp, len(pairs))
    n_x = len(exchange)

    def body(qi_ref, ki_ref, q_ref, k_ref, v_ref, do_ref, lse_ref, dsum_ref, *rest):
        p_refs, (dq_ref, dk_ref, dv_ref), got_refs = rest[:n_x], rest[n_x:n_x + 3], rest[n_x + 3:2 * n_x + 3]
        (dk_s, dv_s, dqt_s), sems = rest[2 * n_x + 3:2 * n_x + 6], rest[2 * n_x + 6:]
        step_id = pl.program_id(2)
        qi, ki = qi_ref[step_id], ki_ref[step_id]
        if n_x:
            start, finish = _exchange_phases(p_refs, got_refs, *sems)
            at_first, _, at_last = _grid_ends(grid)
            pl.when(at_first)(start)

        @pl.when(step_id == 0)
        def _():
            dqt_s[...] = jnp.zeros(dqt_s.shape, F32)

        @pl.when(qi == ki)
        def _():
            dk_s[...] = jnp.zeros(dk_s.shape, F32)
            dv_s[...] = jnp.zeros(dv_s.shape, F32)

        def step(masked):
            if masked:
                keys = lax.broadcasted_iota(jnp.int32, (tq, tq), 0)
                queries = lax.broadcasted_iota(jnp.int32, (tq, tq), 1)
                mask = keys <= queries
            for j in range(hp):
                wide = slice(LANES * j, LANES * (j + 1))
                sl = slice(MLA_V * j, MLA_V * (j + 1))
                q, k = q_ref[:, wide], k_ref[:, wide]
                dob = do_ref[:, sl].astype(BF16)
                s = _nt(k, q) * (MLA_SCALE * LOG2E)
                if masked:
                    s = jnp.where(mask, s, NEG)
                p = jnp.exp2(s - lse_ref[j:j + 1, :])
                dv_s[:, sl] += _nn(p.astype(BF16), dob)
                ds = (p * (_nt(v_ref[:, sl], dob) - dsum_ref[j:j + 1, :]) * MLA_SCALE).astype(BF16)
                dk_s[:, wide] += _nn(ds, q)
                dqt_s[qi, wide, :] += _tn(k, ds)

        @pl.when(qi > ki)
        def _():
            step(False)

        @pl.when(qi == ki)
        def _():
            step(True)

        @pl.when(qi == nq - 1)
        def _():
            dk_ref[...] = dk_s[...]
            dv_ref[...] = dv_s[...]

        @pl.when(step_id == len(pairs) - 1)
        def _():
            for n in range(nq):
                dq_ref[tq * n:tq * (n + 1), :] = dqt_s[n].T

        if n_x:
            pl.when(at_last)(finish)

    qmap = lambda b, g, s, qi_ref, ki_ref: (b * nq + qi_ref[s], g)
    kmap = lambda b, g, s, qi_ref, ki_ref: (b * nq + ki_ref[s], g)
    stat = pl.BlockSpec((None, hp, tq), lambda b, g, s, qi_ref, ki_ref: (g, 0, b * nq + qi_ref[s]))
    vw = hp * MLA_V
    return pl.pallas_call(
        body, name="mla_bwd_exchange" if n_x else "mla_bwd",
        grid_spec=pltpu.PrefetchScalarGridSpec(
            num_scalar_prefetch=2, grid=grid,
            in_specs=[pl.BlockSpec((tq, pw), qmap), pl.BlockSpec((tq, pw), kmap), pl.BlockSpec((tq, vw), kmap),
                      pl.BlockSpec((tq, vw), qmap), stat, stat] + [ANY] * n_x,
            out_specs=[pl.BlockSpec((seq, pw), lambda b, g, s, qi_ref, ki_ref: (b, g)),
                       pl.BlockSpec((tq, pw), kmap), pl.BlockSpec((tq, vw), kmap)] + [ANY] * n_x,
            scratch_shapes=[pltpu.VMEM((tq, pw), F32), pltpu.VMEM((tq, vw), F32), pltpu.VMEM((nq, pw, tq), F32)]
            + (_exchange_sems(n_x) if n_x else [])),
        out_shape=[SDS((T, QFW), F32), SDS((T, QFW), F32), SDS((T, MLA_H * MLA_V), F32)]
        + [SDS(a.shape, a.dtype) for a in exchange],
        compiler_params=_cp("arbitrary", "arbitrary", "arbitrary"))(qi_tab, ki_tab, qf, kf, v, do, lse, dsum, *exchange)


def _swa_bwd(sink, z, pos_col, pos_row, do, lse, dsum, nb, seq, layer, ride=None):
    T = z.shape[0]
    nstep = seq // SWA_STEP
    chains = [(u, h) for u in range(2) for h in range(SWA_H)]

    def body(sink_ref, q_ref, kvc_ref, kvp_ref, pcc_ref, pcp_ref, pr_ref, do_ref, lse_ref, dsum_ref,
             dq_ref, dkv_ref, dsink_ref):
        b, m = pl.program_id(0), pl.program_id(1)

        @pl.when((b == 0) & (m == 0))
        def _():
            dsink_ref[...] = jnp.zeros(dsink_ref.shape, F32)

        @pl.when(m == 0)
        def _():
            dkv_ref[...] = jnp.zeros(dkv_ref.shape, F32)

        kb, vb, band, scores = _swa_scores(m, q_ref, kvc_ref, kvp_ref, pcc_ref, pcp_ref, pr_ref)
        lane = lax.broadcasted_iota(jnp.int32, (1, LANES), 1)
        dsink = jnp.zeros((1, LANES), F32)
        group = lambda h: h // (SWA_H // SWA_KV)
        qs, ss, dobs, dps = [], [], [], []
        for u, h in chains:
            qh, s = scores(u, h)
            dob = do_ref[BLK * u:BLK * (u + 1), SWA_DH * h:SWA_DH * (h + 1)].astype(BF16)
            qs.append(qh)
            ss.append(s)
            dobs.append(dob)
            dps.append(_nt(band(vb, u, group(h)), dob))
        pbs, dss = [], []
        for (u, h), s, dp in zip(chains, ss, dps):
            cols = slice(BLK * u, BLK * (u + 1))
            lse, dsum = lse_ref[h:h + 1, cols], dsum_ref[h:h + 1, cols]
            p = jnp.exp2(s - lse)
            pbs.append(p.astype(BF16))
            dss.append((p * (dp - dsum) * SWA_SCALE).astype(BF16))
            dsk = jnp.sum(-jnp.exp2(sink_ref[layer, h] * LOG2E - lse) * dsum, axis=1, keepdims=True)
            dsink = dsink + jnp.where(lane == h, dsk, 0.0)
        dqs, dkv = [], [[[None, None], [None, None]] for _ in range(2)]
        for i, (u, h) in enumerate(chains):
            g = group(h)
            dqs.append(_tn(band(kb, u, g), dss[i]))
            dk, dv = _nn(dss[i], qs[i]), _nn(pbs[i], dobs[i])
            dkv[u][g][0] = dk if dkv[u][g][0] is None else dkv[u][g][0] + dk
            dkv[u][g][1] = dv if dkv[u][g][1] is None else dkv[u][g][1] + dv
        for u in range(2):
            dq_ref[BLK * u:BLK * (u + 1), :] = jnp.concatenate(dqs[SWA_H * u:SWA_H * (u + 1)], axis=0).T.astype(BF16)
        dsink_ref[...] += dsink
        upd = [jnp.concatenate([dkv[u][0][0], dkv[u][1][0], dkv[u][0][1], dkv[u][1][1]], axis=1) for u in range(2)]
        base = pl.multiple_of(m * SWA_STEP, SWA_STEP)
        dkv_ref[pl.ds(base, BLK), :] += upd[0][BLK:] + upd[1][:BLK]
        dkv_ref[pl.ds(base + BLK, BLK), :] += upd[1][BLK:]

        @pl.when(m > 0)
        def _():
            dkv_ref[pl.ds(pl.multiple_of(m * SWA_STEP - BLK, BLK), BLK), :] += upd[0][:BLK]

    r_in, r_out, r_shape, r_scratch, r_args = _ride_args(ride)
    return pl.pallas_call(
        _riding(ride, body, 10, 3, (nb, nstep)), grid=(nb, nstep), name="swa_bwd_ride" if ride else "swa_bwd",
        in_specs=_swa_specs(nstep) + [pl.BlockSpec((SWA_STEP, 512), lambda b, m: (b * nstep + m, 0))]
        + [pl.BlockSpec((SWA_H, SWA_STEP), lambda b, m: (0, b * nstep + m))] * 2 + r_in,
        out_specs=[pl.BlockSpec((SWA_STEP, 512), lambda b, m: (b * nstep + m, 0)),
                   pl.BlockSpec((seq, 2 * BLK), lambda b, m: (b, 0)),
                   pl.BlockSpec((1, LANES), lambda b, m: (0, 0))] + r_out,
        out_shape=[SDS((T, 512), BF16), SDS((T, 2 * BLK), F32), SDS((1, LANES), F32)] + r_shape,
        scratch_shapes=r_scratch,
        compiler_params=_cp("arbitrary", "arbitrary"))(sink, z, z, z, pos_col, pos_col, pos_row, do, lse, dsum,
                                                       *r_args)


def _bwd_prep(dq, dk, dv, z, gq, gkv, wq, wkv, tc, ts1, ts2, tm, layer):
    T = z.shape[0]

    def body(dq_ref, dk_ref, dv_ref, qd_ref, kvd_ref, gq_ref, gkv_ref, wq_ref, wkv_ref, c_ref, s1_ref, s2_ref,
             dqd_ref, dkvd_ref, dkr_ref, dwq_ref, dwkv_ref, dgq_ref, dgkv_ref, dqb_s, dkvb_s):
        @pl.when(pl.program_id(0) == 0)
        def _():
            for ref in (dwq_ref, dwkv_ref, dgq_ref, dgkv_ref):
                ref[...] = jnp.zeros(ref.shape, F32)

        c, s1, s2 = c_ref[...], s1_ref[...], s2_ref[...]
        lane = lax.broadcasted_iota(jnp.int32, (1, LANES), 1)
        rope_lanes = (lane >= MLA_NOPE) & (lane < MLA_QK)
        dkb = jnp.zeros((tm, LANES), F32)
        for h in range(MLA_H):
            sl = slice(LANES * h, LANES * (h + 1))
            dqb_s[:, sl] = _rope_t(dq_ref[:, sl], c, s1, s2).astype(BF16)
            dkh = dk_ref[:, sl]
            dkb = dkb + dkh
            dkvb_s[:, sl] = dkh.astype(BF16)
        dkvb_s[:, QFW:] = dv_ref[...].astype(BF16)
        dkr_ref[...] = _rope_t(jnp.where(rope_lanes, dkb, 0.0), c, s1, s2).astype(BF16)

        for (x_ref, g_ref, w_ref, d_s, dx_ref, dw_ref, dg_ref) in (
                (qd_ref, gq_ref, wq_ref, dqb_s, dqd_ref, dwq_ref, dgq_ref),
                (kvd_ref, gkv_ref, wkv_ref, dkvb_s, dkvd_ref, dwkv_ref, dgkv_ref)):
            xf, gf, db = x_ref[...], g_ref[...], d_s[...]
            r = _rstd(xf)
            n = xf * r
            dw_ref[...] += _tn((n * gf).astype(BF16), db)
            dx, dgr = _norm_bwd(_nt(db, w_ref[...]), n, r, gf)
            dx_ref[...] = dx.astype(BF16)
            dg_ref[...] += jnp.sum(dgr, axis=0, keepdims=True)

    return pl.pallas_call(
        body, grid=(T // tm,), name="bwd_prep",
        in_specs=[_row(tm, QFW), _row(tm, QFW), _row(tm, MLA_H * MLA_V),
                  _row(tm, QL, Z_QD // QL), _row(tm, KVL, Z_KVD // KVL),
                  _res((1, QL), layer), _res((1, KVL), layer), _res((QL, QFW), 0), _res((KVL, KVW), 0),
                  _row(tm, LANES), _row(tm, LANES), _row(tm, LANES)],
        out_specs=[_row(tm, QL), _row(tm, KVL), _row(tm, LANES),
                   _acc((QL, QFW)), _acc((KVL, KVW)), _acc((1, QL)), _acc((1, KVL))],
        out_shape=[SDS((T, QL), BF16), SDS((T, KVL), BF16), SDS((T, LANES), BF16),
                   SDS((QL, QFW), F32), SDS((KVL, KVW), F32), SDS((1, QL), F32), SDS((1, KVL), F32)],
        scratch_shapes=[pltpu.VMEM((tm, QFW), BF16), pltpu.VMEM((tm, KVW), BF16)],
        compiler_params=_cp("arbitrary"))(dq, dk, dv, z, z, gq, gkv, wq, wkv, tc, ts1, ts2)


def _bwd_in(pieces, x, g, dres, w, tm, layer):
    T = x.shape[0]
    grid = (T // tm,)
    widths = [pc.shape[1] for pc in pieces]
    assert sum(widths) == ZW
    n_p = len(pieces)

    def body(*refs):
        p_refs, (x_ref, g_ref, r_ref, w_ref, dx_ref, dz_ref, dg_ref) = refs[:n_p], refs[n_p:]

        @pl.when(pl.program_id(0) == 0)
        def _():
            dg_ref[...] = jnp.zeros(dg_ref.shape, F32)

        off = 0
        for ref, wd in zip(p_refs, widths):
            dz_ref[:, off:off + wd] = ref[...].astype(BF16)
            off += wd
        xf, gf = x_ref[...], g_ref[...]
        r = _rstd(xf)
        n = xf * r
        dx, dgr = _norm_bwd(_nt(dz_ref[...], w_ref[...]), n, r, gf)
        dx_ref[...] = r_ref[...] + dx
        dg_ref[...] += jnp.sum(dgr, axis=0, keepdims=True)

    return pl.pallas_call(
        body, grid=grid, name="bwd_in",
        in_specs=[_row(tm, wd) for wd in widths] + [_row(tm, D), _res((1, D), layer), _row(tm, D),
                                                    _res((D, ZW), 0)],
        out_specs=[_row(tm, D), _row(tm, ZW), _acc((1, D))],
        out_shape=[SDS((T, D), F32), SDS((T, ZW), BF16), SDS((1, D), F32)],
        compiler_params=_cp("arbitrary"))(*pieces, x, g, dres, w)


def _wgrad_in(hb, dzb, tm, ride=None):
    T = hb.shape[0]
    half = ZW // 2
    grid = (2, T // tm)

    def body(h_ref, dz_ref, dw_ref):
        @pl.when(pl.program_id(1) == 0)
        def _():
            dw_ref[...] = jnp.zeros(dw_ref.shape, F32)

        dw_ref[...] += _tn(h_ref[...], dz_ref[...])

    r_in, r_out, r_shape, r_scratch, r_args = _ride_args(ride)
    out = pl.pallas_call(
        _riding(ride, body, 2, 1, grid), grid=grid, name="wgrad_in_ride" if ride else "wgrad_in",
        in_specs=[pl.BlockSpec((tm, D), lambda j, t: (t, 0)), pl.BlockSpec((tm, half), lambda j, t: (t, j))] + r_in,
        out_specs=[pl.BlockSpec((D, half), lambda j, t: (0, j))] + r_out,
        out_shape=[SDS((D, ZW), F32)] + r_shape, scratch_shapes=r_scratch,
        compiler_params=_cp("arbitrary", "arbitrary"))(hb, dzb, *r_args)
    return out


IN_PIECES = ((0, 512, Z_AQ), (512, 128, Z_AK), (640, 128, Z_AV), (768, 512, Z_AG), (1280, 256, Z_QD),
             (1536, 128, Z_KVD), (1664, MLA_ROPE, Z_KR + MLA_NOPE), (1696, 512, Z_BG), (2208, 1024, Z_MA),
             (3232, 1024, Z_MB))
WIDE_W = IN_W // N_DEV


def _column_runs():
    runs = []
    for start, width, kstart in IN_PIECES:
        col = start
        while col < start + width:
            dev = col // WIDE_W
            stop = min(start + width, (dev + 1) * WIDE_W)
            runs.append((dev, col - dev * WIDE_W, stop - col, kstart + col - start))
            col = stop
    return runs


def _win_layout(blocks, tm):
    runs = _column_runs()

    def body(g_ref, o_ref):
        o_ref[:, Z_KR:Z_KR + LANES] = jnp.zeros((tm, LANES), o_ref.dtype)
        for dev, lo, n, k in runs:
            o_ref[:, k:k + n] = g_ref[dev, :, lo:lo + n]

    return pl.pallas_call(
        body, grid=(D // tm,), name="win_layout",
        in_specs=[pl.BlockSpec((N_DEV, None, tm, WIDE_W), lambda i: (0, 0, i, 0))],
        out_specs=pl.BlockSpec((None, tm, ZW), lambda i: (0, i, 0)),
        out_shape=SDS((1, D, ZW), blocks.dtype),
        compiler_params=_cp("parallel"))(blocks)


def _win_grad_layout(dw, tm):
    runs = _column_runs()

    def body(g_ref, o_ref):
        for dev, lo, n, k in runs:
            o_ref[dev, :, lo:lo + n] = g_ref[:, k:k + n]

    return pl.pallas_call(
        body, grid=(D // tm,), name="win_grad_layout",
        in_specs=[_row(tm, ZW)],
        out_specs=pl.BlockSpec((N_DEV, None, tm, WIDE_W), lambda i: (0, 0, i, 0)),
        out_shape=SDS((N_DEV, 1, D, WIDE_W), F32),
        compiler_params=_cp("parallel"))(dw)


def _wuq_to_kernel(w):
    w = w.reshape(w.shape[:-1] + (MLA_H, MLA_QK))
    w = jnp.pad(w, [(0, 0)] * (w.ndim - 1) + [(0, LANES - MLA_QK)])
    return w.reshape(w.shape[:-2] + (QFW,))


def _wuq_from_kernel(g):
    g = g.reshape(g.shape[:-1] + (MLA_H, LANES))[..., :MLA_QK]
    return g.reshape(g.shape[:-2] + (MLA_H * MLA_QK,))


def _wukv_to_kernel(w):
    w = w.reshape(w.shape[:-1] + (MLA_H, MLA_NOPE + MLA_V))
    k = jnp.pad(w[..., :MLA_NOPE], [(0, 0)] * (w.ndim - 1) + [(0, LANES - MLA_NOPE)])
    v = w[..., MLA_NOPE:]
    return jnp.concatenate([k.reshape(k.shape[:-2] + (QFW,)), v.reshape(v.shape[:-2] + (MLA_H * MLA_V,))], axis=-1)


def _wukv_from_kernel(g):
    k = g[..., :QFW].reshape(g.shape[:-1] + (MLA_H, LANES))[..., :MLA_NOPE]
    v = g[..., QFW:].reshape(g.shape[:-1] + (MLA_H, MLA_V))
    kv = jnp.concatenate([k, v], axis=-1)
    return kv.reshape(kv.shape[:-2] + (MLA_H * (MLA_NOPE + MLA_V),))


def _rope_tables(pos):
    half = MLA_ROPE // 2
    inv = 10000.0 ** (-jnp.arange(0, MLA_ROPE, 2, dtype=F32) / MLA_ROPE)
    ang = pos.astype(F32)[:, None] * inv
    cos, sin = jnp.cos(ang), jnp.sin(ang)
    one = jnp.ones((pos.shape[0], MLA_NOPE), F32)
    zero = lambda n: jnp.zeros((pos.shape[0], n), F32)
    tc = jnp.concatenate([one, cos, cos, one[:, :LANES - MLA_QK]], axis=1)
    ts1 = jnp.concatenate([zero(MLA_NOPE + half), sin, zero(LANES - MLA_QK)], axis=1)
    ts2 = jnp.concatenate([zero(MLA_NOPE), -sin, zero(LANES - MLA_NOPE - half)], axis=1)
    return tc, ts1, ts2


def _local_step(x, p, positions, loss_target, small, wts, plan=None):
    nb, seq, _ = x.shape
    T = nb * seq
    tm = min(512, T)
    tl = min(1024, T)
    tq = min(512, seq)
    xf = x.reshape(T, D)
    pos = positions.reshape(T)
    posf = pos.astype(F32)
    pos_col, pos_row = posf.reshape(T, 1), posf.reshape(T // SWA_STEP, 1, SWA_STEP)
    tc, ts1, ts2 = _rope_tables(pos)

    wts, sm = list(wts), small
    pl_in = p.reshape(DEPTH, T, PLE)
    saved = []
    for i in range(DEPTH):
        riding = plan is not None and i == 0
        w = wts[i]
        z, hb, *got = _fwd_in(xf, sm["g_mix"], w["w_in"], tm, i,
                              ride=_Ride("gather", plan["behind_fwd_in"]) if riding else None)
        if riding:
            w = wts[0] = dict(w, **plan["row_weights"](got[0]))
        oa, lse_a = _swa_fwd(sm["sink"], z, pos_col, pos_row, nb, seq, i)
        qf, kf, v = _fwd_prep(z, sm["g_q"], sm["g_kv"], w["w_uq"], w["w_ukv"], tc, ts1, ts2, tl, i)
        ob, lse_b, *got = _mla_fwd(qf, kf, v, nb, seq, tq, gather=plan["behind_mla_fwd"] if riding else ())
        if riding:
            wts.append(dict(w_in=plan["w_in"](got[0]), **plan["row_weights"](got[1])))
        x1 = _fwd_merge(xf, oa, ob, z, w["w_br_a"], w["w_br_b"], w["w_out"], tm, i)
        saved.append(dict(x=xf, z=z, hb=hb, oa=oa, lse_a=lse_a, qf=qf, kf=kf, v=v, ob=ob, lse_b=lse_b, x1=x1))
        if i < DEPTH - 1:
            xf, saved[i]["pg"], saved[i]["pp"] = _fwd_ple(x1, pl_in, sm["g_ple"], w["w_ple_gate"], w["w_ple_proj"],
                                                          tl, i)

    last = _ple_loss(x1, pl_in, sm["g_ple"], w["w_ple_gate"], w["w_ple_proj"], small["g_final"],
                     loss_target.reshape(T, D), tm, DEPTH - 1)
    dg_final, loss = last[4], last[5]

    grads = [None] * DEPTH
    exchanged = {}
    for i in reversed(range(DEPTH)):
        riding = plan is not None and i == 0
        sv, w = saved[i], wts[i]
        pay = plan["payload"](grads[1]) if riding else []
        if i == DEPTH - 1:
            (dx1, dwpg, dwpp, dg_ple), got = last[:4], []
        else:
            dx1, dwpg, dwpp, dg_ple, *got = _bwd_ple(dx, sv["x1"], sv["pg"], sv["pp"], pl_in, sm["g_ple"],
                                                     w["w_ple_gate"], tm, i,
                                                     ride=_Ride("swap", pay) if riding else None)
        doa, dob, dag, dbg, dma, dmb, dsum_a, dsum_b, dwa, dwb, dwo = _bwd_merge(
            dx1, sv["oa"], sv["ob"], sv["z"], w["w_br_a"], w["w_br_b"], w["w_out"], tm, i)
        stats = (MLA_H // MLA_BWD_HEADS, MLA_BWD_HEADS, T)
        dq_b, dk_b, dv_b, *exchanged["layer1"] = _mla_bwd(
            sv["qf"], sv["kf"], sv["v"], dob, sv["lse_b"].reshape(stats), dsum_b.reshape(stats), nb, seq, tq,
            exchange=plan["add"](pay, got) if riding else ())
        dqd, dkvd, dkr, dwq, dwkv, dgq, dgkv = _bwd_prep(dq_b, dk_b, dv_b, sv["z"], sm["g_q"], sm["g_kv"],
                                                         w["w_uq"], w["w_ukv"], tc, ts1, ts2, tl, i)
        g = dict(w_uq=_wuq_from_kernel(dwq), w_ukv=_wukv_from_kernel(dwkv), w_br_a=dwa, w_br_b=dwb, w_out=dwo,
                 w_ple_gate=dwpg, w_ple_proj=dwpp)
        pay = [plan["rows_payload"](g)] if riding else []
        dq_a, dkv_a, dsink, *got = _swa_bwd(sm["sink"], sv["z"], pos_col, pos_row, doa, sv["lse_a"], dsum_a, nb, seq,
                                            i, ride=_Ride("swap", pay) if riding else None)
        dx, dzb, dg_mix = _bwd_in([dma, dmb, dq_a, dag, dbg, dqd, dkv_a, dkvd, dkr], sv["x"], sm["g_mix"], dx1,
                                  w["w_in"], tm, i)
        dwin, *exchanged["rows0"] = _wgrad_in(sv["hb"], dzb, tm,
                                              ride=_Ride("exchange", plan["add"](pay, got)) if riding else None)
        g.update(g_mix=dg_mix[0], w_in=dwin, sink=dsink[0, :SWA_H], g_q=dgq[0], g_kv=dgkv[0], g_ple=dg_ple[0])
        grads[i] = g
    return loss, dx.reshape(nb, seq, D), grads, dg_final[0], exchanged


def _row_weights(rows):
    blocks = _unpack_rows(rows)
    out = {n: _join(n, blocks[n]) for n, _ in ROWS_PIECES}
    out.update(w_uq=_wuq_to_kernel(out["w_uq"]), w_ukv=_wukv_to_kernel(out["w_ukv"]))
    return out


def _small_params(g_mix, sink, g_q, g_kv, g_ple, g_final):
    return dict(g_mix=g_mix[:, None], sink=sink, g_q=g_q[:, None], g_kv=g_kv[:, None], g_ple=g_ple[:, None],
                g_final=g_final[None])


UQ_W = MLA_H * MLA_QK // N_DEV
ROWS_PIECES = (("w_uq", QL), ("w_ukv", KVL), ("w_br_a", 512), ("w_br_b", 512), ("w_out", D), ("w_ple_gate", D),
               ("w_ple_proj", PLE))
SMALL = (("g_mix", (DEPTH, D)), ("sink", (DEPTH, SWA_H)), ("g_q", (DEPTH, QL)), ("g_kv", (DEPTH, KVL)),
         ("g_ple", (DEPTH, D)), ("g_final", (D,)))
VEC_ROWS = 48
ROWS_N = sum(r for _, r in ROWS_PIECES)
WIDE_TILE, ROWS_TILE = 256, ROWS_N // 2


def _to_rows(name, a):
    if name == "w_uq":
        a = jnp.pad(a, [(0, 0)] * (a.ndim - 1) + [(0, LANES - UQ_W)])
    return a.reshape(a.shape[:-2] + (-1, LANES))


def _from_rows(name, r):
    if name in ("w_out", "w_ple_gate"):
        return r.reshape(r.shape[:-2] + (D // N_DEV, D))
    return r[..., :UQ_W] if name == "w_uq" else r


def _pack_rows(blocks):
    return jnp.concatenate([_to_rows(n, blocks[n]) for n, _ in ROWS_PIECES], axis=-2)


def _unpack_rows(rows):
    blocks, off = {}, 0
    for n, r in ROWS_PIECES:
        blocks[n] = _from_rows(n, rows[..., off:off + r, :])
        off += r
    return blocks


def _pack_vec(vectors, loss=None):
    parts = [vectors[n].reshape(-1) for n, _ in SMALL] + ([] if loss is None else [loss.reshape(1)])
    vec = jnp.concatenate(parts)
    return jnp.pad(vec, (0, VEC_ROWS * LANES - vec.shape[0])).reshape(1, VEC_ROWS, LANES)


def _unpack_vec(vec):
    vec = vec.reshape(-1)
    vectors, off = {}, 0
    for n, shp in SMALL:
        size = 1
        for s in shp:
            size *= s
        vectors[n] = vec[off:off + size].reshape(shp)
        off += size
    return vectors, vec[off]


def _join(name, blocks):
    if name in ("w_out", "w_ple_gate"):
        return jnp.moveaxis(blocks, 0, 1).reshape(blocks.shape[1], -1, blocks.shape[-1])
    return jnp.moveaxis(blocks, 0, 2).reshape(blocks.shape[1], blocks.shape[2], -1)


def _split(name, full):
    if name in ("w_out", "w_ple_gate"):
        return jnp.moveaxis(full.reshape(full.shape[0], N_DEV, -1, full.shape[-1]), 1, 0)
    return jnp.moveaxis(full.reshape(full.shape[0], full.shape[1], N_DEV, -1), 2, 0)


MESH_ID = pl.DeviceIdType.MESH
ANY = pl.BlockSpec(memory_space=pl.ANY)


def _place():
    return lax.axis_index("x"), lax.axis_index("y"), lax.axis_index("c")


def _all_gather(blocks):
    n = len(blocks)

    def body(*refs):
        start, forward, finish = _gather_phases(refs[:n], refs[n:2 * n], *refs[2 * n:])
        start()
        forward()
        finish()

    return pl.pallas_call(
        body, name="all_gather_weights", out_shape=_gather_out(blocks),
        in_specs=[ANY] * n, out_specs=[ANY] * n, scratch_shapes=_gather_sems(n))(*blocks)


def _gather_out(blocks):
    return [SDS((N_DEV,) + b.shape, b.dtype) for b in blocks]


def _gather_sems(n):
    return [pltpu.SemaphoreType.DMA((7 * n,)), pltpu.SemaphoreType.DMA((7 * n,)), pltpu.SemaphoreType.DMA((n,))]


def _gather_phases(x_refs, out_refs, send_sems, recv_sems, local_sems):
    n = len(x_refs)
    x, y, c = _place()
    me, sibling = (x, y, c), (x, y, 1 - c)
    chips = [(1 - x, y), (x, 1 - y), (1 - x, 1 - y)]

    def slot(a, px, py, pc):
        return out_refs[a].at[4 * px + 2 * py + pc]

    def copy(a, k, blk, to, src=None):
        return pltpu.make_async_remote_copy(
            src_ref=slot(a, *blk) if src is None else src, dst_ref=slot(a, *blk),
            send_sem=send_sems.at[7 * a + k], recv_sem=recv_sems.at[7 * a + k], device_id=to,
            device_id_type=MESH_ID)

    def mine():
        return [pltpu.make_async_copy(x_refs[a], slot(a, *me), local_sems.at[a]) for a in range(n)]

    def first():
        out = []
        for a in range(n):
            out += [copy(a, 0, me, sibling, src=x_refs[a])]
            out += [copy(a, 1 + j, me, (*chip, c), src=x_refs[a]) for j, chip in enumerate(chips)]
        return out

    def passed():
        return [copy(a, 4 + j, (*chip, c), sibling) for j, chip in enumerate(chips) for a in range(n)]

    def start():
        for cp in mine() + first():
            cp.start()

    def forward():
        for j, chip in enumerate(chips):
            for a in range(n):
                copy(a, 1 + j, (*chip, c), me).wait_recv()
                copy(a, 4 + j, (*chip, c), sibling).start()

    def finish():
        for a in range(n):
            copy(a, 0, sibling, me).wait_recv()
            for j, chip in enumerate(chips):
                copy(a, 4 + j, (*chip, 1 - c), me).wait_recv()
        for cp in first() + passed():
            cp.wait_send()
        for cp in mine():
            cp.wait()

    return start, forward, finish


def _swap_sibling(arrs):
    n = len(arrs)

    def body(*refs):
        start, finish = _swap_phases(refs[:n], refs[n:2 * n], *refs[2 * n:])
        start()
        finish()

    return pl.pallas_call(
        body, name="swap_sibling", out_shape=[SDS((a.shape[0],) + a.shape[2:], a.dtype) for a in arrs],
        in_specs=[ANY] * n, out_specs=[ANY] * n, scratch_shapes=_swap_sems(n))(*arrs)


def _swap_sems(n):
    return [pltpu.SemaphoreType.DMA((n,)), pltpu.SemaphoreType.DMA((n,))]


def _swap_phases(a_refs, out_refs, send_sems, recv_sems):
    x, y, c = _place()

    def copies():
        return [pltpu.make_async_remote_copy(
            src_ref=a_refs[a].at[:, 1 - c], dst_ref=out_refs[a], send_sem=send_sems.at[a], recv_sem=recv_sems.at[a],
            device_id=(x, y, 1 - c), device_id_type=MESH_ID) for a in range(len(a_refs))]

    def start():
        for cp in copies():
            cp.start()

    def finish():
        for cp in copies():
            cp.wait()

    return start, finish


def _exchange_chips(arrs):
    n = len(arrs)

    def body(*refs):
        start, finish = _exchange_phases(refs[:n], refs[n:2 * n], *refs[2 * n:])
        start()
        finish()

    return pl.pallas_call(
        body, name="exchange_chips", out_shape=[SDS(a.shape, a.dtype) for a in arrs],
        in_specs=[ANY] * n, out_specs=[ANY] * n, scratch_shapes=_exchange_sems(n))(*arrs)


def _exchange_sems(n):
    return [pltpu.SemaphoreType.DMA((3 * n,)), pltpu.SemaphoreType.DMA((3 * n,)), pltpu.SemaphoreType.DMA((n,))]


def _exchange_phases(p_refs, out_refs, send_sems, recv_sems, local_sems):
    n = len(p_refs)
    x, y, c = _place()
    mine = 2 * x + y
    peers = [(1 - x, y), (x, 1 - y), (1 - x, 1 - y)]

    def local():
        return [pltpu.make_async_copy(p_refs[a].at[mine], out_refs[a].at[mine], local_sems.at[a]) for a in range(n)]

    def copy(a, j, src_chip, dst_chip):
        px, py = peers[j]
        return pltpu.make_async_remote_copy(
            src_ref=p_refs[a].at[src_chip], dst_ref=out_refs[a].at[dst_chip], send_sem=send_sems.at[3 * a + j],
            recv_sem=recv_sems.at[3 * a + j], device_id=(px, py, c), device_id_type=MESH_ID)

    def sends():
        return [copy(a, j, 2 * px + py, mine) for a in range(n) for j, (px, py) in enumerate(peers)]

    def start():
        for cp in local() + sends():
            cp.start()

    def finish():
        for a in range(n):
            for j, (px, py) in enumerate(peers):
                copy(a, j, mine, 2 * px + py).wait_recv()
        for cp in sends():
            cp.wait_send()
        for cp in local():
            cp.wait()

    return start, finish


def _add_mine(g, recv, core, tile, dtype):
    _, _, lead, rows, width = g.shape

    def body(c_ref, g_ref, r_ref, o_ref):
        o_ref[...] = (g_ref[...] + r_ref[...]).astype(dtype)

    spec = pl.BlockSpec((None, None, tile, width), lambda k, l, i, c_ref: (k, l, i, 0))
    return pl.pallas_call(
        body, name="add_sibling", out_shape=SDS(recv.shape, dtype),
        grid_spec=pltpu.PrefetchScalarGridSpec(
            num_scalar_prefetch=1, grid=(g.shape[0], lead, rows // tile),
            in_specs=[pl.BlockSpec((None, None, None, tile, width), lambda k, l, i, c_ref: (k, c_ref[0], l, i, 0)),
                      spec],
            out_specs=spec),
        compiler_params=_cp("parallel", "parallel", "parallel"))(core, g, recv)


def _sum_adamw(parts, w, m, v, tile):
    lead, rows, width = w.shape
    last = rows // tile - 1

    def body(*refs):
        p_refs, (w_ref, m_ref, v_ref, g_ref, d_ref, nm_ref, nv_ref) = refs[:lead], refs[lead:]
        for layer in range(lead):
            @pl.when(pl.program_id(0) == layer)
            def _(p_ref=p_refs[layer]):
                g = ((p_ref[0].astype(F32) + p_ref[1].astype(F32)) + p_ref[2].astype(F32)) + p_ref[3].astype(F32)
                nm = ADAM_B1 * m_ref[...] + (1.0 - ADAM_B1) * g
                nv = ADAM_B2 * v_ref[...] + (1.0 - ADAM_B2) * jnp.square(g)
                m_hat = nm / (1.0 - ADAM_B1 ** ADAM_STEP)
                v_hat = nv / (1.0 - ADAM_B2 ** ADAM_STEP)
                g_ref[...] = g
                nm_ref[...] = nm
                nv_ref[...] = nv
                d_ref[...] = -ADAM_LR * (m_hat / (jnp.sqrt(v_hat) + ADAM_EPS) + ADAM_WD * w_ref[...])

    pspec = lambda layer: pl.BlockSpec(
        (4, None, tile, width),
        lambda l, i: (0, 0, jnp.where(l == layer, i, jnp.where(l > layer, last, 0)), 0))
    spec = pl.BlockSpec((None, tile, width), lambda l, i: (l, i, 0))
    return pl.pallas_call(
        body, grid=(lead, rows // tile), name="sum_adamw",
        in_specs=[pspec(layer) for layer in range(lead)] + [spec, spec, spec],
        out_specs=[spec] * 4, out_shape=[SDS((lead, rows, width), F32)] * 4,
        compiler_params=_cp("arbitrary", "arbitrary"))(*parts, w, m, v)


def kernel(x, p, positions, g_mix, w_in, sink, g_q, w_uq, g_kv, w_ukv, w_br_a, w_br_b, w_out, g_ple, w_ple_gate, w_ple_proj, g_final, loss_target, m_g_mix, m_w_in, m_sink, m_g_q, m_w_uq, m_g_kv, m_w_ukv, m_w_br_a, m_w_br_b, m_w_out, m_g_ple, m_w_ple_gate, m_w_ple_proj, m_g_final, v_g_mix, v_w_in, v_sink, v_g_q, v_w_uq, v_g_kv, v_w_ukv, v_w_br_a, v_w_br_b, v_w_out, v_g_ple, v_w_ple_gate, v_w_ple_proj, v_g_final):
    weights = dict(g_mix=g_mix, w_in=w_in, sink=sink, g_q=g_q, w_uq=w_uq, g_kv=g_kv, w_ukv=w_ukv, w_br_a=w_br_a,
                   w_br_b=w_br_b, w_out=w_out, g_ple=g_ple, w_ple_gate=w_ple_gate, w_ple_proj=w_ple_proj,
                   g_final=g_final)
    mom1 = dict(g_mix=m_g_mix, w_in=m_w_in, sink=m_sink, g_q=m_g_q, w_uq=m_w_uq, g_kv=m_g_kv, w_ukv=m_w_ukv,
                w_br_a=m_w_br_a, w_br_b=m_w_br_b, w_out=m_w_out, g_ple=m_g_ple, w_ple_gate=m_w_ple_gate,
                w_ple_proj=m_w_ple_proj, g_final=m_g_final)
    mom2 = dict(g_mix=v_g_mix, w_in=v_w_in, sink=v_sink, g_q=v_g_q, w_uq=v_w_uq, g_kv=v_g_kv, w_ukv=v_w_ukv,
                w_br_a=v_w_br_a, w_br_b=v_w_br_b, w_out=v_w_out, g_ple=v_g_ple, w_ple_gate=v_w_ple_gate,
                w_ple_proj=v_w_ple_proj, g_final=v_g_final)
    assert DEPTH == 2
    wide = lambda d: d["w_in"]
    rows = lambda d: _pack_rows(d)
    core = lax.axis_index("c").astype(jnp.int32).reshape(1)

    w16 = [wide(weights).astype(BF16), rows(weights).astype(BF16)]
    wts0 = dict(w_in=_win_layout(_all_gather([w16[0][:1]])[0], 256))
    small = _small_params(g_mix, sink, g_q, g_kv, g_ple, g_final)

    def wide_payload(g):
        return _win_grad_layout(g["w_in"], 256).reshape(N_DEV // 2, 2, 1, D, WIDE_W)

    def rows_payload(g):
        return _pack_rows({n: _split(n, g[n][None]) for n, _ in ROWS_PIECES}).reshape(N_DEV // 2, 2, 1, ROWS_N, LANES)

    def add(pay, got):
        tiles = {D: (WIDE_TILE, BF16), ROWS_N: (ROWS_TILE, BF16), VEC_ROWS: (VEC_ROWS, F32)}
        return [_add_mine(a, b, core, *tiles[a.shape[-2]]) for a, b in zip(pay, got)]

    plan = dict(behind_fwd_in=[w16[1][:1]], behind_mla_fwd=[a[1:] for a in w16],
                w_in=lambda blocks: _win_layout(blocks, 256),
                row_weights=_row_weights, payload=lambda g: [wide_payload(g), rows_payload(g)],
                rows_payload=rows_payload, add=add)
    loss, grad_x, grads, dg_final, rode = _local_step(x, p, positions, loss_target, small, [wts0], plan)

    vectors = {n: jnp.stack([grads[i][n] for i in range(DEPTH)]) for n, _ in SMALL[:-1]}
    vectors["g_final"] = dg_final
    pay = [wide_payload(grads[0]),
           jnp.broadcast_to(_pack_vec(vectors, loss[0, 0]), (N_DEV // 2, 2, 1, VEC_ROWS, LANES))]
    parts_wide0, parts_vec = _exchange_chips(add(pay, _swap_sibling(pay)))
    out_wide = _sum_adamw([parts_wide0, rode["layer1"][0]], wide(weights), wide(mom1), wide(mom2), WIDE_TILE)
    out_rows = _sum_adamw([rode["rows0"][0], rode["layer1"][1]], rows(weights), rows(mom1), rows(mom2), ROWS_TILE)
    out_vec = _sum_adamw([parts_vec], _pack_vec(weights), _pack_vec(mom1), _pack_vec(mom2), VEC_ROWS)

    outs = []
    for ow, orow, ovec in zip(out_wide, out_rows, out_vec):
        named = _unpack_rows(orow)
        named.update(_unpack_vec(ovec)[0])
        named["w_in"] = ow
        outs += [named[n] for n in weights]
    loss = _unpack_vec(out_vec[0])[1]
    return (loss, grad_x, *outs)
```

```python
import functools

import jax
import jax.numpy as jnp
from jax import lax
from jax.experimental import pallas as pl
from jax.experimental.pallas import tpu as pltpu

F32, BF16 = jnp.float32, jnp.bfloat16
SDS = jax.ShapeDtypeStruct

D = 1024
DEPTH = 2
PLE = 256
BLK = 128
EPS = 1e-6
NEG = -1e30
SWA_H, SWA_KV, SWA_DH = 8, 2, 64
MLA_H, MLA_NOPE, MLA_ROPE, MLA_V = 8, 64, 32, 64
MLA_QK = MLA_NOPE + MLA_ROPE
QL, KVL = 256, 128
IN_W = 4256
N_DEV = 8

V7X_VMEM_BYTES = 64 * 1024 * 1024
LANES = 128
VMEM_LIMIT = V7X_VMEM_BYTES * 7 // 8

ZW = 4352
Z_MA, Z_MB, Z_AQ, Z_AG, Z_BG, Z_QD, Z_AK, Z_AV, Z_KVD, Z_KR = 0, 1024, 2048, 2560, 3072, 3584, 3840, 3968, 4096, 4224
QFW = MLA_H * LANES
KVW = QFW + MLA_H * MLA_V
MLA_SCALE = MLA_QK ** -0.5
LOG2E = 1.4426950408889634
MLA_FWD_HEADS, MLA_BWD_HEADS = 8, 8
SWA_SCALE = SWA_DH ** -0.5
ROLL_UP, ROLL_DOWN = MLA_ROPE // 2, LANES - MLA_ROPE // 2

ADAM_LR, ADAM_B1, ADAM_B2, ADAM_EPS, ADAM_WD, ADAM_STEP = 0.001, 0.9, 0.999, 1e-08, 0.01, 10


def _cp(*sem):
    return pltpu.CompilerParams(dimension_semantics=sem, vmem_limit_bytes=VMEM_LIMIT)


def _row(tm, w, col=0):
    return pl.BlockSpec((tm, w), lambda i: (i, col))


def _res(shape, layer=None):
    if layer is None:
        return pl.BlockSpec(shape, lambda *_: (0,) * len(shape), pipeline_mode=pl.Buffered(1))
    return pl.BlockSpec((None,) + shape, lambda *_: (layer,) + (0,) * len(shape), pipeline_mode=pl.Buffered(1))


def _acc(shape):
    return pl.BlockSpec(shape, lambda *_: (0,) * len(shape))


def _rstd(xf):
    return lax.rsqrt(jnp.mean(xf * xf, axis=-1, keepdims=True) + EPS)


def _norm_bwd(dh, n, r, g):
    dn = dh * g
    return r * (dn - n * jnp.mean(dn * n, axis=-1, keepdims=True)), dh * n


def _nt(a, b):
    return lax.dot_general(a, b, (((1,), (1,)), ((), ())), preferred_element_type=F32)


def _tn(a, b):
    return lax.dot_general(a, b, (((0,), (0,)), ((), ())), preferred_element_type=F32)


def _nn(a, b):
    return jnp.dot(a, b, preferred_element_type=F32)


def _sig(x):
    return jax.nn.sigmoid(x)


def _rope(t, c, s1, s2):
    return t * c + pltpu.roll(t, ROLL_UP, 1) * s1 + pltpu.roll(t, ROLL_DOWN, 1) * s2


def _rope_t(d, c, s1, s2):
    return d * c + pltpu.roll(d * s1, ROLL_DOWN, 1) + pltpu.roll(d * s2, ROLL_UP, 1)


def _fwd_in(x, g, w, tm, layer, ride=None):
    T = x.shape[0]
    grid = (T // tm,)

    def body(x_ref, g_ref, w_ref, z_ref, h_ref):
        xf = x_ref[...]
        h = ((xf * _rstd(xf)) * g_ref[...]).astype(BF16)
        h_ref[...] = h
        z_ref[...] = _nn(h, w_ref[...])

    r_in, r_out, r_shape, r_scratch, r_args = _ride_args(ride)
    return pl.pallas_call(
        _riding(ride, body, 3, 2, grid), grid=grid, name="fwd_in_ride" if ride else "fwd_in",
        in_specs=[_row(tm, D), _res((1, D), layer), _res((D, ZW), 0)] + r_in,
        out_specs=[_row(tm, ZW), _row(tm, D)] + r_out,
        out_shape=[SDS((T, ZW), F32), SDS((T, D), BF16)] + r_shape, scratch_shapes=r_scratch,
        compiler_params=_cp("arbitrary"))(x, g, w, *r_args)


def _fwd_prep(z, gq, gkv, wq, wkv, tc, ts1, ts2, tm, layer):
    T = z.shape[0]

    def body(qd_ref, kvd_ref, kr_ref, gq_ref, gkv_ref, wq_ref, wkv_ref, c_ref, s1_ref, s2_ref, q_ref, k_ref, v_ref):
        qd, kvd = qd_ref[...], kvd_ref[...]
        hq = ((qd * _rstd(qd)) * gq_ref[...]).astype(BF16)
        hkv = ((kvd * _rstd(kvd)) * gkv_ref[...]).astype(BF16)
        qf = _nn(hq, wq_ref[...])
        kvf = _nn(hkv, wkv_ref[...])
        c, s1, s2 = c_ref[...], s1_ref[...], s2_ref[...]
        krb = _rope(kr_ref[...], c, s1, s2)
        for h in range(MLA_H):
            sl = slice(LANES * h, LANES * (h + 1))
            q_ref[:, sl] = _rope(qf[:, sl], c, s1, s2).astype(BF16)
            k_ref[:, sl] = (kvf[:, sl] + krb).astype(BF16)
        v_ref[...] = kvf[:, QFW:].astype(BF16)

    return pl.pallas_call(
        body, grid=(T // tm,), name="fwd_prep",
        in_specs=[_row(tm, QL, Z_QD // QL), _row(tm, KVL, Z_KVD // KVL), _row(tm, LANES, Z_KR // LANES),
                  _res((1, QL), layer), _res((1, KVL), layer), _res((QL, QFW), 0), _res((KVL, KVW), 0),
                  _row(tm, LANES), _row(tm, LANES), _row(tm, LANES)],
        out_specs=[_row(tm, QFW), _row(tm, QFW), _row(tm, MLA_H * MLA_V)],
        out_shape=[SDS((T, QFW), BF16), SDS((T, QFW), BF16), SDS((T, MLA_H * MLA_V), BF16)],
        compiler_params=_cp("parallel"))(z, z, z, gq, gkv, wq, wkv, tc, ts1, ts2)


def _grid_ends(grid):
    ids = [pl.program_id(a) for a in range(len(grid))]
    inner_first = functools.reduce(jnp.logical_and, [i == 0 for i in ids[1:]], True)
    last = functools.reduce(jnp.logical_and, [i == g - 1 for i, g in zip(ids, grid)])
    return (ids[0] == 0) & inner_first, (ids[0] == 3 * grid[0] // 4) & inner_first, last


class _Ride:
    def __init__(self, kind, arrays):
        self.kind, self.arrays, self.n = kind, list(arrays), len(arrays)

    def out_shape(self):
        if self.kind == "gather":
            return _gather_out(self.arrays)
        if self.kind == "swap":
            return [SDS((a.shape[0],) + a.shape[2:], a.dtype) for a in self.arrays]
        return [SDS(a.shape, a.dtype) for a in self.arrays]

    def sems(self):
        if self.kind == "gather":
            return _gather_sems(self.n)
        if self.kind == "swap":
            return _swap_sems(self.n)
        return _exchange_sems(self.n)

    def phases(self, in_refs, out_refs, *sems):
        if self.kind == "gather":
            return _gather_phases(in_refs, out_refs, *sems)
        start, finish = (_swap_phases if self.kind == "swap" else _exchange_phases)(in_refs, out_refs, *sems)
        return start, None, finish


def _riding(ride, body, n_in, n_out, grid):
    if ride is None:
        return body
    n, n_sems = ride.n, len(ride.sems())

    def wrapped(*refs):
        ins, r_in = refs[:n_in], refs[n_in:n_in + n]
        outs, r_out = refs[n_in + n:n_in + n + n_out], refs[n_in + n + n_out:n_in + 2 * n + n_out]
        rest = refs[n_in + 2 * n + n_out:]
        scratch, sems = rest[:len(rest) - n_sems], rest[len(rest) - n_sems:]
        start, middle, finish = ride.phases(r_in, r_out, *sems)
        at_first, at_middle, at_last = _grid_ends(grid)
        pl.when(at_first)(start)
        if middle is not None:
            pl.when(at_middle)(middle)
        body(*ins, *outs, *scratch)
        pl.when(at_last)(finish)

    return wrapped


def _ride_args(ride):
    if ride is None:
        return [], [], [], [], []
    return [ANY] * ride.n, [ANY] * ride.n, ride.out_shape(), ride.sems(), ride.arrays


def _mla_fwd(qf, kf, v, nb, seq, tq, gather=()):
    T = qf.shape[0]
    nq = seq // tq
    hp = MLA_FWD_HEADS
    pw = hp * LANES
    pairs = [(qi, ki) for qi in range(nq) for ki in range(qi + 1)]
    qi_tab = jnp.array([qk[0] for qk in pairs], jnp.int32)
    ki_tab = jnp.array([qk[1] for qk in pairs], jnp.int32)
    grid = (nb, MLA_H // hp, len(pairs))
    n_g = len(gather)

    def body(qi_ref, ki_ref, q_ref, k_ref, v_ref, *rest):
        x_refs, (o_ref, lse_ref), got_refs = rest[:n_g], rest[n_g:n_g + 2], rest[n_g + 2:2 * n_g + 2]
        (m_s, l_s, acc_s), sems = rest[2 * n_g + 2:2 * n_g + 5], rest[2 * n_g + 5:]
        qi, ki = qi_ref[pl.program_id(2)], ki_ref[pl.program_id(2)]
        if n_g:
            start, forward, finish = _gather_phases(x_refs, got_refs, *sems)
            at_first, at_middle, at_last = _grid_ends(grid)
            pl.when(at_first)(start)
            pl.when(at_middle)(forward)

        @pl.when(ki == 0)
        def _():
            m_s[...] = jnp.full(m_s.shape, NEG, F32)
            l_s[...] = jnp.zeros(l_s.shape, F32)
            acc_s[...] = jnp.zeros(acc_s.shape, F32)

        def step(masked):
            parts = [(0, tq // 2, tq // 2), (tq // 2, tq, tq)] if masked else [(0, tq, tq)]
            work = [(j, a, b, kh) for j in range(hp) for a, b, kh in parts]
            ss = []
            for j, a, b, kh in work:
                wide = slice(LANES * j, LANES * (j + 1))
                s = _nt(k_ref[:kh, wide], q_ref[a:b, wide]) * (MLA_SCALE * LOG2E)
                if masked:
                    keys = lax.broadcasted_iota(jnp.int32, (kh, b - a), 0)
                    queries = a + lax.broadcasted_iota(jnp.int32, (kh, b - a), 1)
                    s = jnp.where(keys <= queries, s, NEG)
                ss.append(s)
            ps, alphas = [], []
            for (j, a, b, kh), s in zip(work, ss):
                m_prev = m_s[j, :, a:b]
                m_new = jnp.maximum(m_prev, jnp.max(s, axis=0, keepdims=True))
                alpha = jnp.exp2(m_prev - m_new)
                p = jnp.exp2(s - m_new)
                l_s[j, :, a:b] = alpha * l_s[j, :, a:b] + jnp.sum(p, axis=0, keepdims=True)
                m_s[j, :, a:b] = m_new
                ps.append(p.astype(BF16))
                alphas.append(alpha)
            for (j, a, b, kh), p, alpha in zip(work, ps, alphas):
                rows = slice(MLA_V * j, MLA_V * (j + 1))
                acc_s[rows, a:b] = alpha * acc_s[rows, a:b] + _tn(v_ref[:kh, rows], p)

        @pl.when(ki < qi)
        def _():
            step(False)

        @pl.when(ki == qi)
        def _():
            step(True)
            for j in range(hp):
                rows = slice(MLA_V * j, MLA_V * (j + 1))
                acc_s[rows, :] = acc_s[rows, :] / l_s[j]
                lse_ref[j:j + 1, :] = m_s[j] + jnp.log2(l_s[j])
            o_ref[...] = acc_s[...].T

        if n_g:
            pl.when(at_last)(finish)

    q_map = lambda b, g, s, qi_ref, ki_ref: (b * nq + qi_ref[s], g)
    kv_map = lambda b, g, s, qi_ref, ki_ref: (b * nq + ki_ref[s], g)
    return pl.pallas_call(
        body, name="mla_fwd_gather" if n_g else "mla_fwd",
        grid_spec=pltpu.PrefetchScalarGridSpec(
            num_scalar_prefetch=2, grid=grid,
            in_specs=[pl.BlockSpec((tq, pw), q_map), pl.BlockSpec((tq, pw), kv_map),
                      pl.BlockSpec((tq, hp * MLA_V), kv_map)] + [ANY] * n_g,
            out_specs=[pl.BlockSpec((tq, hp * MLA_V), q_map),
                       pl.BlockSpec((hp, tq), lambda b, g, s, qi_ref, ki_ref: (g, b * nq + qi_ref[s]))]
            + [ANY] * n_g,
            scratch_shapes=[pltpu.VMEM((hp, 1, tq), F32), pltpu.VMEM((hp, 1, tq), F32),
                            pltpu.VMEM((hp * MLA_V, tq), F32)]
            + (_gather_sems(n_g) if n_g else [])),
        out_shape=[SDS((T, MLA_H * MLA_V), F32), SDS((MLA_H, T), F32)] + _gather_out(gather),
        compiler_params=_cp("arbitrary", "arbitrary", "arbitrary"))(qi_tab, ki_tab, qf, kf, v, *gather)


SWA_STEP = 2 * BLK


def _swa_specs(nstep):
    cur = lambda b, m: (b * nstep + m, 0)
    prev = lambda b, m: (2 * b * nstep + jnp.maximum(2 * m - 1, 0), 0)
    kvc = Z_AK // (2 * BLK)
    return [pl.BlockSpec(memory_space=pltpu.SMEM),
            pl.BlockSpec((SWA_STEP, 512), lambda b, m: (b * nstep + m, Z_AQ // 512)),
            pl.BlockSpec((SWA_STEP, 2 * BLK), lambda b, m: (b * nstep + m, kvc)),
            pl.BlockSpec((BLK, 2 * BLK), lambda b, m: (2 * b * nstep + jnp.maximum(2 * m - 1, 0), kvc)),
            pl.BlockSpec((SWA_STEP, 1), cur),
            pl.BlockSpec((BLK, 1), prev),
            pl.BlockSpec((1, 1, SWA_STEP), lambda b, m: (b * nstep + m, 0, 0))]


def _swa_scores(m, q_ref, kvc_ref, kvp_ref, pcc_ref, pcp_ref, pr_ref):
    kv = jnp.concatenate([kvp_ref[...], kvc_ref[...]], axis=0)
    kb, vb = kv[:, :BLK].astype(BF16), kv[:, BLK:].astype(BF16)
    pos_keys = jnp.concatenate([pcp_ref[...], pcc_ref[...]], axis=0)
    key = lax.broadcasted_iota(jnp.int32, (2 * BLK, BLK), 0)
    qry = lax.broadcasted_iota(jnp.int32, (2 * BLK, BLK), 1)
    in_window = (key > qry) & (key <= qry + BLK)
    valid = [in_window & ((key >= BLK) | (m > 0)), in_window]
    dist = [pr_ref[0][:, BLK * u:BLK * (u + 1)] - pos_keys[BLK * u:BLK * (u + 2)] for u in range(2)]

    def band(t, u, g):
        return t[BLK * u:BLK * (u + 2), SWA_DH * g:SWA_DH * (g + 1)]

    def scores(u, h):
        g = h // (SWA_H // SWA_KV)
        qh = q_ref[BLK * u:BLK * (u + 1), SWA_DH * h:SWA_DH * (h + 1)].astype(BF16)
        s = _nt(band(kb, u, g), qh) * (SWA_SCALE * LOG2E) - (2.0 ** -(h + 1) * LOG2E) * dist[u]
        return qh, jnp.where(valid[u], s, NEG)

    return kb, vb, band, scores


def _swa_fwd(sink, z, pos_col, pos_row, nb, seq, layer):
    T = z.shape[0]
    nstep = seq // SWA_STEP
    chains = [(u, h) for u in range(2) for h in range(SWA_H)]

    def body(sink_ref, q_ref, kvc_ref, kvp_ref, pcc_ref, pcp_ref, pr_ref, o_ref, lse_ref):
        kb, vb, band, scores = _swa_scores(pl.program_id(1), q_ref, kvc_ref, kvp_ref, pcc_ref, pcp_ref, pr_ref)
        ss = [scores(u, h)[1] for u, h in chains]
        es, dens = [], []
        for (u, h), s in zip(chains, ss):
            sk = sink_ref[layer, h] * LOG2E
            m = jnp.maximum(jnp.max(s, axis=0, keepdims=True), sk)
            e = jnp.exp2(s - m)
            den = jnp.sum(e, axis=0, keepdims=True) + jnp.exp2(sk - m)
            lse_ref[h:h + 1, BLK * u:BLK * (u + 1)] = m + jnp.log2(den)
            es.append(e.astype(BF16))
            dens.append(den)
        outs = [_tn(band(vb, u, h // (SWA_H // SWA_KV)), e) / den for (u, h), e, den in zip(chains, es, dens)]
        for u in range(2):
            o_ref[BLK * u:BLK * (u + 1), :] = jnp.concatenate(outs[SWA_H * u:SWA_H * (u + 1)], axis=0).T

    return pl.pallas_call(
        body, grid=(nb, nstep), name="swa_fwd",
        in_specs=_swa_specs(nstep),
        out_specs=[pl.BlockSpec((SWA_STEP, 512), lambda b, m: (b * nstep + m, 0)),
                   pl.BlockSpec((SWA_H, SWA_STEP), lambda b, m: (0, b * nstep + m))],
        out_shape=[SDS((T, 512), F32), SDS((SWA_H, T), F32)],
        compiler_params=_cp("parallel", "parallel"))(sink, z, z, z, pos_col, pos_col, pos_row)


def _fwd_merge(x, oa, ob, z, wa, wb, wo, tm, layer):
    T = x.shape[0]

    def body(x_ref, oa_ref, ob_ref, ag_ref, bg_ref, ma_ref, mb_ref, wa_ref, wb_ref, wo_ref, x1_ref):
        ag, bg = ag_ref[...], bg_ref[...]
        ua = _nn((oa_ref[...] * (ag * _sig(ag))).astype(BF16), wa_ref[...])
        ub = _nn((ob_ref[...] * (bg * _sig(bg))).astype(BF16), wb_ref[...])
        y = _sig(ma_ref[...]) * ua + _sig(mb_ref[...]) * ub
        x1_ref[...] = x_ref[...] + _nn(y.astype(BF16), wo_ref[...])

    return pl.pallas_call(
        body, grid=(T // tm,), name="fwd_merge",
        in_specs=[_row(tm, D), _row(tm, 512), _row(tm, 512), _row(tm, 512, Z_AG // 512), _row(tm, 512, Z_BG // 512),
                  _row(tm, D, Z_MA // D), _row(tm, D, Z_MB // D),
                  _res((512, D), 0), _res((512, D), 0), _res((D, D), 0)],
        out_specs=_row(tm, D),
        out_shape=SDS((T, D), F32),
        compiler_params=_cp("parallel"))(x, oa, ob, z, z, z, z, wa, wb, wo)


def _fwd_ple(x1, p, g, wpg, wpp, tm, layer):
    T = x1.shape[0]

    def body(x_ref, p_ref, g_ref, wpg_ref, wpp_ref, x2_ref, pg_ref, pp_ref):
        xf = x_ref[...]
        h1 = ((xf * _rstd(xf)) * g_ref[...]).astype(BF16)
        pg = _sig(_nn(h1, wpg_ref[...]))
        pp = _nn(p_ref[...].astype(BF16), wpp_ref[...])
        pg_ref[...] = pg
        pp_ref[...] = pp
        x2_ref[...] = xf + pg * pp

    return pl.pallas_call(
        body, grid=(T // tm,), name="fwd_ple",
        in_specs=[_row(tm, D), pl.BlockSpec((None, tm, PLE), lambda i: (layer, i, 0)),
                  _res((1, D), layer), _res((D, D), 0), _res((PLE, D), 0)],
        out_specs=[_row(tm, D)] * 3,
        out_shape=[SDS((T, D), F32)] * 3,
        compiler_params=_cp("parallel"))(x1, p, g, wpg, wpp)


def _ple_loss(x1, p, g, wpg, wpp, g_final, tgt, tm, layer):
    T = x1.shape[0]

    def body(x_ref, p_ref, g_ref, wpg_ref, wpp_ref, gf_ref, t_ref, dx_ref, dwg_ref, dwp_ref, dg_ref, dgf_ref, loss_ref):
        @pl.when(pl.program_id(0) == 0)
        def _():
            for ref in (dwg_ref, dwp_ref, dg_ref, dgf_ref, loss_ref):
                ref[...] = jnp.zeros(ref.shape, F32)

        xf, gp, gf = x_ref[...], g_ref[...], gf_ref[...]
        r = _rstd(xf)
        n = xf * r
        h1 = (n * gp).astype(BF16)
        pb = p_ref[...].astype(BF16)
        pg = _sig(_nn(h1, wpg_ref[...]))
        pp = _nn(pb, wpp_ref[...])
        x2 = xf + pg * pp
        r2 = _rstd(x2)
        n2 = x2 * r2
        err = n2 * gf - t_ref[...]
        loss_ref[...] += 0.5 * jnp.sum(jnp.mean(err * err, axis=-1, keepdims=True), axis=0, keepdims=True)
        d, dgfr = _norm_bwd(err * (1.0 / D), n2, r2, gf)
        dgf_ref[...] += jnp.sum(dgfr, axis=0, keepdims=True)
        dpgl = (d * pp * pg * (1.0 - pg)).astype(BF16)
        dwg_ref[...] += _tn(h1, dpgl)
        dwp_ref[...] += _tn(pb, (d * pg).astype(BF16))
        dxn, dgr = _norm_bwd(_nt(dpgl, wpg_ref[...]), n, r, gp)
        dx_ref[...] = d + dxn
        dg_ref[...] += jnp.sum(dgr, axis=0, keepdims=True)

    return pl.pallas_call(
        body, grid=(T // tm,), name="ple_loss",
        in_specs=[_row(tm, D), pl.BlockSpec((None, tm, PLE), lambda i: (layer, i, 0)), _res((1, D), layer),
                  _res((D, D), 0), _res((PLE, D), 0), _res((1, D)), _row(tm, D)],
        out_specs=[_row(tm, D), _acc((D, D)), _acc((PLE, D)), _acc((1, D)), _acc((1, D)), _acc((1, LANES))],
        out_shape=[SDS((T, D), F32), SDS((D, D), F32), SDS((PLE, D), F32), SDS((1, D), F32), SDS((1, D), F32),
                   SDS((1, LANES), F32)],
        compiler_params=_cp("arbitrary"))(x1, p, g, wpg, wpp, g_final, tgt)


def _bwd_ple(dx2, x1, pg, pp, p, g, wpg, tm, layer, ride=None):
    T = x1.shape[0]
    grid = (T // tm,)

    def body(d_ref, x_ref, pg_ref, pp_ref, p_ref, g_ref, w_ref, dx_ref, dwg_ref, dwp_ref, dg_ref):
        @pl.when(pl.program_id(0) == 0)
        def _():
            dwg_ref[...] = jnp.zeros(dwg_ref.shape, F32)
            dwp_ref[...] = jnp.zeros(dwp_ref.shape, F32)
            dg_ref[...] = jnp.zeros(dg_ref.shape, F32)

        d, xf, pg, gf = d_ref[...], x_ref[...], pg_ref[...], g_ref[...]
        r = _rstd(xf)
        n = xf * r
        dpgl = (d * pp_ref[...] * pg * (1.0 - pg)).astype(BF16)
        dwg_ref[...] += _tn((n * gf).astype(BF16), dpgl)
        dwp_ref[...] += _tn(p_ref[...].astype(BF16), (d * pg).astype(BF16))
        dxn, dgr = _norm_bwd(_nt(dpgl, w_ref[...]), n, r, gf)
        dx_ref[...] = d + dxn
        dg_ref[...] += jnp.sum(dgr, axis=0, keepdims=True)

    r_in, r_out, r_shape, r_scratch, r_args = _ride_args(ride)
    return pl.pallas_call(
        _riding(ride, body, 7, 4, grid), grid=grid, name="bwd_ple_ride" if ride else "bwd_ple",
        in_specs=[_row(tm, D)] * 4 + [pl.BlockSpec((None, tm, PLE), lambda i: (layer, i, 0)),
                                      _res((1, D), layer), _res((D, D), 0)] + r_in,
        out_specs=[_row(tm, D), _acc((D, D)), _acc((PLE, D)), _acc((1, D))] + r_out,
        out_shape=[SDS((T, D), F32), SDS((D, D), F32), SDS((PLE, D), F32), SDS((1, D), F32)] + r_shape,
        scratch_shapes=r_scratch,
        compiler_params=_cp("arbitrary"))(dx2, x1, pg, pp, p, g, wpg, *r_args)


def _bwd_merge(dx1, oa, ob, z, wa, wb, wo, tm, layer):
    T = dx1.shape[0]

    def body(d_ref, oa_ref, ob_ref, ag_ref, bg_ref, ma_ref, mb_ref, wa_ref, wb_ref, wo_ref,
             doa_ref, dob_ref, dag_ref, dbg_ref, dma_ref, dmb_ref, dsa_ref, dsb_ref, dwa_ref, dwb_ref, dwo_ref):
        @pl.when(pl.program_id(0) == 0)
        def _():
            dwa_ref[...] = jnp.zeros(dwa_ref.shape, F32)
            dwb_ref[...] = jnp.zeros(dwb_ref.shape, F32)
            dwo_ref[...] = jnp.zeros(dwo_ref.shape, F32)

        db = d_ref[...].astype(BF16)
        gated = []
        for o_ref, gate_ref, w_ref in ((oa_ref, ag_ref, wa_ref), (ob_ref, bg_ref, wb_ref)):
            raw, gate = o_ref[...], gate_ref[...]
            sg = _sig(gate)
            silu = gate * sg
            ob16 = (raw * silu).astype(BF16)
            gated.append((raw, gate, sg, silu, ob16, _nn(ob16, w_ref[...])))
        ua, ub = gated[0][5], gated[1][5]
        sa, sb = _sig(ma_ref[...]), _sig(mb_ref[...])
        dwo_ref[...] += _tn((sa * ua + sb * ub).astype(BF16), db)
        dy = _nt(db, wo_ref[...])
        dma_ref[...] = (dy * ua * sa * (1.0 - sa)).astype(BF16)
        dmb_ref[...] = (dy * ub * sb * (1.0 - sb)).astype(BF16)
        for (s, w_ref, do_ref, dgate_ref, dw_ref, ds_ref), (raw, gate, sg, silu, ob16, _) in zip((
                (sa, wa_ref, doa_ref, dag_ref, dwa_ref, dsa_ref),
                (sb, wb_ref, dob_ref, dbg_ref, dwb_ref, dsb_ref)), gated):
            du = (dy * s).astype(BF16)
            dw_ref[...] += _tn(ob16, du)
            do = _nt(du, w_ref[...])
            draw = do * silu
            do_ref[...] = draw.astype(BF16)
            dgate_ref[...] = (do * raw * (sg * (1.0 + gate * (1.0 - sg)))).astype(BF16)
            ds_ref[...] = jnp.sum((draw * raw).T.reshape(MLA_H, MLA_V, tm), axis=1)

    return pl.pallas_call(
        body, grid=(T // tm,), name="bwd_merge",
        in_specs=[_row(tm, D), _row(tm, 512), _row(tm, 512), _row(tm, 512, Z_AG // 512), _row(tm, 512, Z_BG // 512),
                  _row(tm, D, Z_MA // D), _row(tm, D, Z_MB // D),
                  _res((512, D), 0), _res((512, D), 0), _res((D, D), 0)],
        out_specs=[_row(tm, 512)] * 4 + [_row(tm, D)] * 2 + [pl.BlockSpec((MLA_H, tm), lambda i: (0, i))] * 2
        + [_acc((512, D)), _acc((512, D)), _acc((D, D))],
        out_shape=[SDS((T, 512), BF16), SDS((T, 512), BF16), SDS((T, 512), BF16), SDS((T, 512), BF16),
                   SDS((T, D), BF16), SDS((T, D), BF16), SDS((MLA_H, T), F32), SDS((MLA_H, T), F32),
                   SDS((512, D), F32), SDS((512, D), F32), SDS((D, D), F32)],
        compiler_params=_cp("arbitrary"))(dx1, oa, ob, z, z, z, z, wa, wb, wo)


def _mla_bwd(qf, kf, v, do, lse, dsum, nb, seq, tq, exchange=()):
    T = qf.shape[0]
    nq = seq // tq
    hp = MLA_BWD_HEADS
    pw = hp * LANES
    pairs = [(qi, ki) for ki in range(nq) for qi in range(ki, nq)]
    qi_tab = jnp.array([qk[0] for qk in pairs], jnp.int32)
    ki_tab = jnp.array([qk[1] for qk in pairs], jnp.int32)
    grid = (nb, MLA_H // hp, len(pairs))
    n_x = len(exchange)

    def body(qi_ref, ki_ref, q_ref, k_ref, v_ref, do_ref, lse_ref, dsum_ref, *rest):
        p_refs, (dq_ref, dk_ref, dv_ref), got_refs = rest[:n_x], rest[n_x:n_x + 3], rest[n_x + 3:2 * n_x + 3]
        (dk_s, dv_s, dqt_s), sems = rest[2 * n_x + 3:2 * n_x + 6], rest[2 * n_x + 6:]
        step_id = pl.program_id(2)
        qi, ki = qi_ref[step_id], ki_ref[step_id]
        if n_x:
            start, finish = _exchange_phases(p_refs, got_refs, *sems)
            at_first, _, at_last = _grid_ends(grid)
            pl.when(at_first)(start)

        @pl.when(step_id == 0)
        def _():
            dqt_s[...] = jnp.zeros(dqt_s.shape, F32)

        @pl.when(qi == ki)
        def _():
            dk_s[...] = jnp.zeros(dk_s.shape, F32)
            dv_s[...] = jnp.zeros(dv_s.shape, F32)

        def step(masked):
            if masked:
                keys = lax.broadcasted_iota(jnp.int32, (tq, tq), 0)
                queries = lax.broadcasted_iota(jnp.int32, (tq, tq), 1)
                mask = keys <= queries
            for j in range(hp):
                wide = slice(LANES * j, LANES * (j + 1))
                sl = slice(MLA_V * j, MLA_V * (j + 1))
                q, k = q_ref[:, wide], k_ref[:, wide]
                dob = do_ref[:, sl].astype(BF16)
                s = _nt(k, q) * (MLA_SCALE * LOG2E)
                if masked:
                    s = jnp.where(mask, s, NEG)
                p = jnp.exp2(s - lse_ref[j:j + 1, :])
                dv_s[:, sl] += _nn(p.astype(BF16), dob)
                ds = (p * (_nt(v_ref[:, sl], dob) - dsum_ref[j:j + 1, :]) * MLA_SCALE).astype(BF16)
                dk_s[:, wide] += _nn(ds, q)
                dqt_s[qi, wide, :] += _tn(k, ds)

        @pl.when(qi > ki)
        def _():
            step(False)

        @pl.when(qi == ki)
        def _():
            step(True)

        @pl.when(qi == nq - 1)
        def _():
            dk_ref[...] = dk_s[...]
            dv_ref[...] = dv_s[...]

        @pl.when(step_id == len(pairs) - 1)
        def _():
            for n in range(nq):
                dq_ref[tq * n:tq * (n + 1), :] = dqt_s[n].T

        if n_x:
            pl.when(at_last)(finish)

    qmap = lambda b, g, s, qi_ref, ki_ref: (b * nq + qi_ref[s], g)
    kmap = lambda b, g, s, qi_ref, ki_ref: (b * nq + ki_ref[s], g)
    stat = pl.BlockSpec((None, hp, tq), lambda b, g, s, qi_ref, ki_ref: (g, 0, b * nq + qi_ref[s]))
    vw = hp * MLA_V
    return pl.pallas_call(
        body, name="mla_bwd_exchange" if n_x else "mla_bwd",
        grid_spec=pltpu.PrefetchScalarGridSpec(
            num_scalar_prefetch=2, grid=grid,
            in_specs=[pl.BlockSpec((tq, pw), qmap), pl.BlockSpec((tq, pw), kmap), pl.BlockSpec((tq, vw), kmap),
                      pl.BlockSpec((tq, vw), qmap), stat, stat] + [ANY] * n_x,
            out_specs=[pl.BlockSpec((seq, pw), lambda b, g, s, qi_ref, ki_ref: (b, g)),
                       pl.BlockSpec((tq, pw), kmap), pl.BlockSpec((tq, vw), kmap)] + [ANY] * n_x,
            scratch_shapes=[pltpu.VMEM((tq, pw), F32), pltpu.VMEM((tq, vw), F32), pltpu.VMEM((nq, pw, tq), F32)]
            + (_exchange_sems(n_x) if n_x else [])),
        out_shape=[SDS((T, QFW), F32), SDS((T, QFW), F32), SDS((T, MLA_H * MLA_V), F32)]
        + [SDS(a.shape, a.dtype) for a in exchange],
        compiler_params=_cp("arbitrary", "arbitrary", "arbitrary"))(qi_tab, ki_tab, qf, kf, v, do, lse, dsum, *exchange)


def _swa_bwd(sink, z, pos_col, pos_row, do, lse, dsum, nb, seq, layer, ride=None):
    T = z.shape[0]
    nstep = seq // SWA_STEP
    chains = [(u, h) for u in range(2) for h in range(SWA_H)]

    def body(sink_ref, q_ref, kvc_ref, kvp_ref, pcc_ref, pcp_ref, pr_ref, do_ref, lse_ref, dsum_ref,
             dq_ref, dkv_ref, dsink_ref):
        b, m = pl.program_id(0), pl.program_id(1)

        @pl.when((b == 0) & (m == 0))
        def _():
            dsink_ref[...] = jnp.zeros(dsink_ref.shape, F32)

        @pl.when(m == 0)
        def _():
            dkv_ref[...] = jnp.zeros(dkv_ref.shape, F32)

        kb, vb, band, scores = _swa_scores(m, q_ref, kvc_ref, kvp_ref, pcc_ref, pcp_ref, pr_ref)
        lane = lax.broadcasted_iota(jnp.int32, (1, LANES), 1)
        dsink = jnp.zeros((1, LANES), F32)
        group = lambda h: h // (SWA_H // SWA_KV)
        qs, ss, dobs, dps = [], [], [], []
        for u, h in chains:
            qh, s = scores(u, h)
            dob = do_ref[BLK * u:BLK * (u + 1), SWA_DH * h:SWA_DH * (h + 1)].astype(BF16)
            qs.append(qh)
            ss.append(s)
            dobs.append(dob)
            dps.append(_nt(band(vb, u, group(h)), dob))
        pbs, dss = [], []
        for (u, h), s, dp in zip(chains, ss, dps):
            cols = slice(BLK * u, BLK * (u + 1))
            lse, dsum = lse_ref[h:h + 1, cols], dsum_ref[h:h + 1, cols]
            p = jnp.exp2(s - lse)
            pbs.append(p.astype(BF16))
            dss.append((p * (dp - dsum) * SWA_SCALE).astype(BF16))
            dsk = jnp.sum(-jnp.exp2(sink_ref[layer, h] * LOG2E - lse) * dsum, axis=1, keepdims=True)
            dsink = dsink + jnp.where(lane == h, dsk, 0.0)
        dqs, dkv = [], [[[None, None], [None, None]] for _ in range(2)]
        for i, (u, h) in enumerate(chains):
            g = group(h)
            dqs.append(_tn(band(kb, u, g), dss[i]))
            dk, dv = _nn(dss[i], qs[i]), _nn(pbs[i], dobs[i])
            dkv[u][g][0] = dk if dkv[u][g][0] is None else dkv[u][g][0] + dk
            dkv[u][g][1] = dv if dkv[u][g][1] is None else dkv[u][g][1] + dv
        for u in range(2):
            dq_ref[BLK * u:BLK * (u + 1), :] = jnp.concatenate(dqs[SWA_H * u:SWA_H * (u + 1)], axis=0).T.astype(BF16)
        dsink_ref[...] += dsink
        upd = [jnp.concatenate([dkv[u][0][0], dkv[u][1][0], dkv[u][0][1], dkv[u][1][1]], axis=1) for u in range(2)]
        base = pl.multiple_of(m * SWA_STEP, SWA_STEP)
        dkv_ref[pl.ds(base, BLK), :] += upd[0][BLK:] + upd[1][:BLK]
        dkv_ref[pl.ds(base + BLK, BLK), :] += upd[1][BLK:]

        @pl.when(m > 0)
        def _():
            dkv_ref[pl.ds(pl.multiple_of(m * SWA_STEP - BLK, BLK), BLK), :] += upd[0][:BLK]

    r_in, r_out, r_shape, r_scratch, r_args = _ride_args(ride)
    return pl.pallas_call(
        _riding(ride, body, 10, 3, (nb, nstep)), grid=(nb, nstep), name="swa_bwd_ride" if ride else "swa_bwd",
        in_specs=_swa_specs(nstep) + [pl.BlockSpec((SWA_STEP, 512), lambda b, m: (b * nstep + m, 0))]
        + [pl.BlockSpec((SWA_H, SWA_STEP), lambda b, m: (0, b * nstep + m))] * 2 + r_in,
        out_specs=[pl.BlockSpec((SWA_STEP, 512), lambda b, m: (b * nstep + m, 0)),
                   pl.BlockSpec((seq, 2 * BLK), lambda b, m: (b, 0)),
                   pl.BlockSpec((1, LANES), lambda b, m: (0, 0))] + r_out,
        out_shape=[SDS((T, 512), BF16), SDS((T, 2 * BLK), F32), SDS((1, LANES), F32)] + r_shape,
        scratch_shapes=r_scratch,
        compiler_params=_cp("arbitrary", "arbitrary"))(sink, z, z, z, pos_col, pos_col, pos_row, do, lse, dsum,
                                                       *r_args)


def _bwd_prep(dq, dk, dv, z, gq, gkv, wq, wkv, tc, ts1, ts2, tm, layer):
    T = z.shape[0]

    def body(dq_ref, dk_ref, dv_ref, qd_ref, kvd_ref, gq_ref, gkv_ref, wq_ref, wkv_ref, c_ref, s1_ref, s2_ref,
             dqd_ref, dkvd_ref, dkr_ref, dwq_ref, dwkv_ref, dgq_ref, dgkv_ref, dqb_s, dkvb_s):
        @pl.when(pl.program_id(0) == 0)
        def _():
            for ref in (dwq_ref, dwkv_ref, dgq_ref, dgkv_ref):
                ref[...] = jnp.zeros(ref.shape, F32)

        c, s1, s2 = c_ref[...], s1_ref[...], s2_ref[...]
        lane = lax.broadcasted_iota(jnp.int32, (1, LANES), 1)
        rope_lanes = (lane >= MLA_NOPE) & (lane < MLA_QK)
        dkb = jnp.zeros((tm, LANES), F32)
        for h in range(MLA_H):
            sl = slice(LANES * h, LANES * (h + 1))
            dqb_s[:, sl] = _rope_t(dq_ref[:, sl], c, s1, s2).astype(BF16)
            dkh = dk_ref[:, sl]
            dkb = dkb + dkh
            dkvb_s[:, sl] = dkh.astype(BF16)
        dkvb_s[:, QFW:] = dv_ref[...].astype(BF16)
        dkr_ref[...] = _rope_t(jnp.where(rope_lanes, dkb, 0.0), c, s1, s2).astype(BF16)

        for (x_ref, g_ref, w_ref, d_s, dx_ref, dw_ref, dg_ref) in (
                (qd_ref, gq_ref, wq_ref, dqb_s, dqd_ref, dwq_ref, dgq_ref),
                (kvd_ref, gkv_ref, wkv_ref, dkvb_s, dkvd_ref, dwkv_ref, dgkv_ref)):
            xf, gf, db = x_ref[...], g_ref[...], d_s[...]
            r = _rstd(xf)
            n = xf * r
            dw_ref[...] += _tn((n * gf).astype(BF16), db)
            dx, dgr = _norm_bwd(_nt(db, w_ref[...]), n, r, gf)
            dx_ref[...] = dx.astype(BF16)
            dg_ref[...] += jnp.sum(dgr, axis=0, keepdims=True)

    return pl.pallas_call(
        body, grid=(T // tm,), name="bwd_prep",
        in_specs=[_row(tm, QFW), _row(tm, QFW), _row(tm, MLA_H * MLA_V),
                  _row(tm, QL, Z_QD // QL), _row(tm, KVL, Z_KVD // KVL),
                  _res((1, QL), layer), _res((1, KVL), layer), _res((QL, QFW), 0), _res((KVL, KVW), 0),
                  _row(tm, LANES), _row(tm, LANES), _row(tm, LANES)],
        out_specs=[_row(tm, QL), _row(tm, KVL), _row(tm, LANES),
                   _acc((QL, QFW)), _acc((KVL, KVW)), _acc((1, QL)), _acc((1, KVL))],
        out_shape=[SDS((T, QL), BF16), SDS((T, KVL), BF16), SDS((T, LANES), BF16),
                   SDS((QL, QFW), F32), SDS((KVL, KVW), F32), SDS((1, QL), F32), SDS((1, KVL), F32)],
        scratch_shapes=[pltpu.VMEM((tm, QFW), BF16), pltpu.VMEM((tm, KVW), BF16)],
        compiler_params=_cp("arbitrary"))(dq, dk, dv, z, z, gq, gkv, wq, wkv, tc, ts1, ts2)


def _bwd_in(pieces, x, g, dres, w, tm, layer):
    T = x.shape[0]
    grid = (T // tm,)
    widths = [pc.shape[1] for pc in pieces]
    assert sum(widths) == ZW
    n_p = len(pieces)

    def body(*refs):
        p_refs, (x_ref, g_ref, r_ref, w_ref, dx_ref, dz_ref, dg_ref) = refs[:n_p], refs[n_p:]

        @pl.when(pl.program_id(0) == 0)
        def _():
            dg_ref[...] = jnp.zeros(dg_ref.shape, F32)

        off = 0
        for ref, wd in zip(p_refs, widths):
            dz_ref[:, off:off + wd] = ref[...].astype(BF16)
            off += wd
        xf, gf = x_ref[...], g_ref[...]
        r = _rstd(xf)
        n = xf * r
        dx, dgr = _norm_bwd(_nt(dz_ref[...], w_ref[...]), n, r, gf)
        dx_ref[...] = r_ref[...] + dx
        dg_ref[...] += jnp.sum(dgr, axis=0, keepdims=True)

    return pl.pallas_call(
        body, grid=grid, name="bwd_in",
        in_specs=[_row(tm, wd) for wd in widths] + [_row(tm, D), _res((1, D), layer), _row(tm, D),
                                                    _res((D, ZW), 0)],
        out_specs=[_row(tm, D), _row(tm, ZW), _acc((1, D))],
        out_shape=[SDS((T, D), F32), SDS((T, ZW), BF16), SDS((1, D), F32)],
        compiler_params=_cp("arbitrary"))(*pieces, x, g, dres, w)


def _wgrad_in(hb, dzb, tm, ride=None):
    T = hb.shape[0]
    half = ZW // 2
    grid = (2, T // tm)

    def body(h_ref, dz_ref, dw_ref):
        @pl.when(pl.program_id(1) == 0)
        def _():
            dw_ref[...] = jnp.zeros(dw_ref.shape, F32)

        dw_ref[...] += _tn(h_ref[...], dz_ref[...])

    r_in, r_out, r_shape, r_scratch, r_args = _ride_args(ride)
    out = pl.pallas_call(
        _riding(ride, body, 2, 1, grid), grid=grid, name="wgrad_in_ride" if ride else "wgrad_in",
        in_specs=[pl.BlockSpec((tm, D), lambda j, t: (t, 0)), pl.BlockSpec((tm, half), lambda j, t: (t, j))] + r_in,
        out_specs=[pl.BlockSpec((D, half), lambda j, t: (0, j))] + r_out,
        out_shape=[SDS((D, ZW), F32)] + r_shape, scratch_shapes=r_scratch,
        compiler_params=_cp("arbitrary", "arbitrary"))(hb, dzb, *r_args)
    return out


IN_PIECES = ((0, 512, Z_AQ), (512, 128, Z_AK), (640, 128, Z_AV), (768, 512, Z_AG), (1280, 256, Z_QD),
             (1536, 128, Z_KVD), (1664, MLA_ROPE, Z_KR + MLA_NOPE), (1696, 512, Z_BG), (2208, 1024, Z_MA),
             (3232, 1024, Z_MB))
WIDE_W = IN_W // N_DEV


def _column_runs():
    runs = []
    for start, width, kstart in IN_PIECES:
        col = start
        while col < start + width:
            dev = col // WIDE_W
            stop = min(start + width, (dev + 1) * WIDE_W)
            runs.append((dev, col - dev * WIDE_W, stop - col, kstart + col - start))
            col = stop
    return runs


def _win_layout(blocks, tm):
    runs = _column_runs()

    def body(g_ref, o_ref):
        o_ref[:, Z_KR:Z_KR + LANES] = jnp.zeros((tm, LANES), o_ref.dtype)
        for dev, lo, n, k in runs:
            o_ref[:, k:k + n] = g_ref[dev, :, lo:lo + n]

    return pl.pallas_call(
        body, grid=(D // tm,), name="win_layout",
        in_specs=[pl.BlockSpec((N_DEV, None, tm, WIDE_W), lambda i: (0, 0, i, 0))],
        out_specs=pl.BlockSpec((None, tm, ZW), lambda i: (0, i, 0)),
        out_shape=SDS((1, D, ZW), blocks.dtype),
        compiler_params=_cp("parallel"))(blocks)


def _win_grad_layout(dw, tm):
    runs = _column_runs()

    def body(g_ref, o_ref):
        for dev, lo, n, k in runs:
            o_ref[dev, :, lo:lo + n] = g_ref[:, k:k + n]

    return pl.pallas_call(
        body, grid=(D // tm,), name="win_grad_layout",
        in_specs=[_row(tm, ZW)],
        out_specs=pl.BlockSpec((N_DEV, None, tm, WIDE_W), lambda i: (0, 0, i, 0)),
        out_shape=SDS((N_DEV, 1, D, WIDE_W), F32),
        compiler_params=_cp("parallel"))(dw)


def _wuq_to_kernel(w):
    w = w.reshape(w.shape[:-1] + (MLA_H, MLA_QK))
    w = jnp.pad(w, [(0, 0)] * (w.ndim - 1) + [(0, LANES - MLA_QK)])
    return w.reshape(w.shape[:-2] + (QFW,))


def _wuq_from_kernel(g):
    g = g.reshape(g.shape[:-1] + (MLA_H, LANES))[..., :MLA_QK]
    return g.reshape(g.shape[:-2] + (MLA_H * MLA_QK,))


def _wukv_to_kernel(w):
    w = w.reshape(w.shape[:-1] + (MLA_H, MLA_NOPE + MLA_V))
    k = jnp.pad(w[..., :MLA_NOPE], [(0, 0)] * (w.ndim - 1) + [(0, LANES - MLA_NOPE)])
    v = w[..., MLA_NOPE:]
    return jnp.concatenate([k.reshape(k.shape[:-2] + (QFW,)), v.reshape(v.shape[:-2] + (MLA_H * MLA_V,))], axis=-1)


def _wukv_from_kernel(g):
    k = g[..., :QFW].reshape(g.shape[:-1] + (MLA_H, LANES))[..., :MLA_NOPE]
    v = g[..., QFW:].reshape(g.shape[:-1] + (MLA_H, MLA_V))
    kv = jnp.concatenate([k, v], axis=-1)
    return kv.reshape(kv.shape[:-2] + (MLA_H * (MLA_NOPE + MLA_V),))


def _rope_tables(pos):
    half = MLA_ROPE // 2
    inv = 10000.0 ** (-jnp.arange(0, MLA_ROPE, 2, dtype=F32) / MLA_ROPE)
    ang = pos.astype(F32)[:, None] * inv
    cos, sin = jnp.cos(ang), jnp.sin(ang)
    one = jnp.ones((pos.shape[0], MLA_NOPE), F32)
    zero = lambda n: jnp.zeros((pos.shape[0], n), F32)
    tc = jnp.concatenate([one, cos, cos, one[:, :LANES - MLA_QK]], axis=1)
    ts1 = jnp.concatenate([zero(MLA_NOPE + half), sin, zero(LANES - MLA_QK)], axis=1)
    ts2 = jnp.concatenate([zero(MLA_NOPE), -sin, zero(LANES - MLA_NOPE - half)], axis=1)
    return tc, ts1, ts2


def _local_step(x, p, positions, loss_target, small, wts, plan=None):
    nb, seq, _ = x.shape
    T = nb * seq
    tm = min(512, T)
    tl = min(1024, T)
    tq = min(512, seq)
    xf = x.reshape(T, D)
    pos = positions.reshape(T)
    posf = pos.astype(F32)
    pos_col, pos_row = posf.reshape(T, 1), posf.reshape(T // SWA_STEP, 1, SWA_STEP)
    tc, ts1, ts2 = _rope_tables(pos)

    wts, sm = list(wts), small
    pl_in = p.reshape(DEPTH, T, PLE)
    saved = []
    for i in range(DEPTH):
        riding = plan is not None and i == 0
        w = wts[i]
        z, hb, *got = _fwd_in(xf, sm["g_mix"], w["w_in"], tm, i,
                              ride=_Ride("gather", plan["behind_fwd_in"]) if riding else None)
        if riding:
            w = wts[0] = dict(w, **plan["row_weights"](got[0]))
        oa, lse_a = _swa_fwd(sm["sink"], z, pos_col, pos_row, nb, seq, i)
        qf, kf, v = _fwd_prep(z, sm["g_q"], sm["g_kv"], w["w_uq"], w["w_ukv"], tc, ts1, ts2, tl, i)
        ob, lse_b, *got = _mla_fwd(qf, kf, v, nb, seq, tq, gather=plan["behind_mla_fwd"] if riding else ())
        if riding:
            wts.append(dict(w_in=plan["w_in"](got[0]), **plan["row_weights"](got[1])))
        x1 = _fwd_merge(xf, oa, ob, z, w["w_br_a"], w["w_br_b"], w["w_out"], tm, i)
        saved.append(dict(x=xf, z=z, hb=hb, oa=oa, lse_a=lse_a, qf=qf, kf=kf, v=v, ob=ob, lse_b=lse_b, x1=x1))
        if i < DEPTH - 1:
            xf, saved[i]["pg"], saved[i]["pp"] = _fwd_ple(x1, pl_in, sm["g_ple"], w["w_ple_gate"], w["w_ple_proj"],
                                                          tl, i)

    last = _ple_loss(x1, pl_in, sm["g_ple"], w["w_ple_gate"], w["w_ple_proj"], small["g_final"],
                     loss_target.reshape(T, D), tm, DEPTH - 1)
    dg_final, loss = last[4], last[5]

    grads = [None] * DEPTH
    exchanged = {}
    for i in reversed(range(DEPTH)):
        riding = plan is not None and i == 0
        sv, w = saved[i], wts[i]
        pay = plan["payload"](grads[1]) if riding else []
        if i == DEPTH - 1:
            (dx1, dwpg, dwpp, dg_ple), got = last[:4], []
        else:
            dx1, dwpg, dwpp, dg_ple, *got = _bwd_ple(dx, sv["x1"], sv["pg"], sv["pp"], pl_in, sm["g_ple"],
                                                     w["w_ple_gate"], tm, i,
                                                     ride=_Ride("swap", pay) if riding else None)
        doa, dob, dag, dbg, dma, dmb, dsum_a, dsum_b, dwa, dwb, dwo = _bwd_merge(
            dx1, sv["oa"], sv["ob"], sv["z"], w["w_br_a"], w["w_br_b"], w["w_out"], tm, i)
        stats = (MLA_H // MLA_BWD_HEADS, MLA_BWD_HEADS, T)
        dq_b, dk_b, dv_b, *exchanged["layer1"] = _mla_bwd(
            sv["qf"], sv["kf"], sv["v"], dob, sv["lse_b"].reshape(stats), dsum_b.reshape(stats), nb, seq, tq,
            exchange=plan["add"](pay, got) if riding else ())
        dqd, dkvd, dkr, dwq, dwkv, dgq, dgkv = _bwd_prep(dq_b, dk_b, dv_b, sv["z"], sm["g_q"], sm["g_kv"],
                                                         w["w_uq"], w["w_ukv"], tc, ts1, ts2, tl, i)
        g = dict(w_uq=_wuq_from_kernel(dwq), w_ukv=_wukv_from_kernel(dwkv), w_br_a=dwa, w_br_b=dwb, w_out=dwo,
                 w_ple_gate=dwpg, w_ple_proj=dwpp)
        pay = [plan["rows_payload"](g)] if riding else []
        dq_a, dkv_a, dsink, *got = _swa_bwd(sm["sink"], sv["z"], pos_col, pos_row, doa, sv["lse_a"], dsum_a, nb, seq,
                                            i, ride=_Ride("swap", pay) if riding else None)
        dx, dzb, dg_mix = _bwd_in([dma, dmb, dq_a, dag, dbg, dqd, dkv_a, dkvd, dkr], sv["x"], sm["g_mix"], dx1,
                                  w["w_in"], tm, i)
        dwin, *exchanged["rows0"] = _wgrad_in(sv["hb"], dzb, tm,
                                              ride=_Ride("exchange", plan["add"](pay, got)) if riding else None)
        g.update(g_mix=dg_mix[0], w_in=dwin, sink=dsink[0, :SWA_H], g_q=dgq[0], g_kv=dgkv[0], g_ple=dg_ple[0])
        grads[i] = g
    return loss, dx.reshape(nb, seq, D), grads, dg_final[0], exchanged


def _row_weights(rows):
    blocks = _unpack_rows(rows)
    out = {n: _join(n, blocks[n]) for n, _ in ROWS_PIECES}
    out.update(w_uq=_wuq_to_kernel(out["w_uq"]), w_ukv=_wukv_to_kernel(out["w_ukv"]))
    return out


def _small_params(g_mix, sink, g_q, g_kv, g_ple, g_final):
    return dict(g_mix=g_mix[:, None], sink=sink, g_q=g_q[:, None], g_kv=g_kv[:, None], g_ple=g_ple[:, None],
                g_final=g_final[None])


UQ_W = MLA_H * MLA_QK // N_DEV
ROWS_PIECES = (("w_uq", QL), ("w_ukv", KVL), ("w_br_a", 512), ("w_br_b", 512), ("w_out", D), ("w_ple_gate", D),
               ("w_ple_proj", PLE))
SMALL = (("g_mix", (DEPTH, D)), ("sink", (DEPTH, SWA_H)), ("g_q", (DEPTH, QL)), ("g_kv", (DEPTH, KVL)),
         ("g_ple", (DEPTH, D)), ("g_final", (D,)))
VEC_ROWS = 48
ROWS_N = sum(r for _, r in ROWS_PIECES)
WIDE_TILE, ROWS_TILE = 512, ROWS_N // 2


def _to_rows(name, a):
    if name == "w_uq":
        a = jnp.pad(a, [(0, 0)] * (a.ndim - 1) + [(0, LANES - UQ_W)])
    return a.reshape(a.shape[:-2] + (-1, LANES))


def _from_rows(name, r):
    if name in ("w_out", "w_ple_gate"):
        return r.reshape(r.shape[:-2] + (D // N_DEV, D))
    return r[..., :UQ_W] if name == "w_uq" else r


def _pack_rows(blocks):
    return jnp.concatenate([_to_rows(n, blocks[n]) for n, _ in ROWS_PIECES], axis=-2)


def _unpack_rows(rows):
    blocks, off = {}, 0
    for n, r in ROWS_PIECES:
        blocks[n] = _from_rows(n, rows[..., off:off + r, :])
        off += r
    return blocks


def _pack_vec(vectors, loss=None):
    parts = [vectors[n].reshape(-1) for n, _ in SMALL] + ([] if loss is None else [loss.reshape(1)])
    vec = jnp.concatenate(parts)
    return jnp.pad(vec, (0, VEC_ROWS * LANES - vec.shape[0])).reshape(1, VEC_ROWS, LANES)


def _unpack_vec(vec):
    vec = vec.reshape(-1)
    vectors, off = {}, 0
    for n, shp in SMALL:
        size = 1
        for s in shp:
            size *= s
        vectors[n] = vec[off:off + size].reshape(shp)
        off += size
    return vectors, vec[off]


def _join(name, blocks):
    if name in ("w_out", "w_ple_gate"):
        return jnp.moveaxis(blocks, 0, 1).reshape(blocks.shape[1], -1, blocks.shape[-1])
    return jnp.moveaxis(blocks, 0, 2).reshape(blocks.shape[1], blocks.shape[2], -1)


def _split(name, full):
    if name in ("w_out", "w_ple_gate"):
        return jnp.moveaxis(full.reshape(full.shape[0], N_DEV, -1, full.shape[-1]), 1, 0)
    return jnp.moveaxis(full.reshape(full.shape[0], full.shape[1], N_DEV, -1), 2, 0)


MESH_ID = pl.DeviceIdType.MESH
ANY = pl.BlockSpec(memory_space=pl.ANY)


def _place():
    return lax.axis_index("x"), lax.axis_index("y"), lax.axis_index("c")


def _all_gather(blocks):
    n = len(blocks)

    def body(*refs):
        start, forward, finish = _gather_phases(refs[:n], refs[n:2 * n], *refs[2 * n:])
        start()
        forward()
        finish()

    return pl.pallas_call(
        body, name="all_gather_weights", out_shape=_gather_out(blocks),
        in_specs=[ANY] * n, out_specs=[ANY] * n, scratch_shapes=_gather_sems(n))(*blocks)


def _gather_out(blocks):
    return [SDS((N_DEV,) + b.shape, b.dtype) for b in blocks]


def _gather_sems(n):
    return [pltpu.SemaphoreType.DMA((7 * n,)), pltpu.SemaphoreType.DMA((7 * n,)), pltpu.SemaphoreType.DMA((n,))]


def _gather_phases(x_refs, out_refs, send_sems, recv_sems, local_sems):
    n = len(x_refs)
    x, y, c = _place()
    me, sibling = (x, y, c), (x, y, 1 - c)
    chips = [(1 - x, y), (x, 1 - y), (1 - x, 1 - y)]

    def slot(a, px, py, pc):
        return out_refs[a].at[4 * px + 2 * py + pc]

    def copy(a, k, blk, to, src=None):
        return pltpu.make_async_remote_copy(
            src_ref=slot(a, *blk) if src is None else src, dst_ref=slot(a, *blk),
            send_sem=send_sems.at[7 * a + k], recv_sem=recv_sems.at[7 * a + k], device_id=to,
            device_id_type=MESH_ID)

    def mine():
        return [pltpu.make_async_copy(x_refs[a], slot(a, *me), local_sems.at[a]) for a in range(n)]

    def first():
        out = []
        for a in range(n):
            out += [copy(a, 0, me, sibling, src=x_refs[a])]
            out += [copy(a, 1 + j, me, (*chip, c), src=x_refs[a]) for j, chip in enumerate(chips)]
        return out

    def passed():
        return [copy(a, 4 + j, (*chip, c), sibling) for j, chip in enumerate(chips) for a in range(n)]

    def start():
        for cp in mine() + first():
            cp.start()

    def forward():
        for j, chip in enumerate(chips):
            for a in range(n):
                copy(a, 1 + j, (*chip, c), me).wait_recv()
                copy(a, 4 + j, (*chip, c), sibling).start()

    def finish():
        for a in range(n):
            copy(a, 0, sibling, me).wait_recv()
            for j, chip in enumerate(chips):
                copy(a, 4 + j, (*chip, 1 - c), me).wait_recv()
        for cp in first() + passed():
            cp.wait_send()
        for cp in mine():
            cp.wait()

    return start, forward, finish


def _swap_sibling(arrs):
    n = len(arrs)

    def body(*refs):
        start, finish = _swap_phases(refs[:n], refs[n:2 * n], *refs[2 * n:])
        start()
        finish()

    return pl.pallas_call(
        body, name="swap_sibling", out_shape=[SDS((a.shape[0],) + a.shape[2:], a.dtype) for a in arrs],
        in_specs=[ANY] * n, out_specs=[ANY] * n, scratch_shapes=_swap_sems(n))(*arrs)


def _swap_sems(n):
    return [pltpu.SemaphoreType.DMA((n,)), pltpu.SemaphoreType.DMA((n,))]


def _swap_phases(a_refs, out_refs, send_sems, recv_sems):
    x, y, c = _place()

    def copies():
        return [pltpu.make_async_remote_copy(
            src_ref=a_refs[a].at[:, 1 - c], dst_ref=out_refs[a], send_sem=send_sems.at[a], recv_sem=recv_sems.at[a],
            device_id=(x, y, 1 - c), device_id_type=MESH_ID) for a in range(len(a_refs))]

    def start():
        for cp in copies():
            cp.start()

    def finish():
        for cp in copies():
            cp.wait()

    return start, finish


def _exchange_chips(arrs):
    n = len(arrs)

    def body(*refs):
        start, finish = _exchange_phases(refs[:n], refs[n:2 * n], *refs[2 * n:])
        start()
        finish()

    return pl.pallas_call(
        body, name="exchange_chips", out_shape=[SDS(a.shape, a.dtype) for a in arrs],
        in_specs=[ANY] * n, out_specs=[ANY] * n, scratch_shapes=_exchange_sems(n))(*arrs)


def _exchange_sems(n):
    return [pltpu.SemaphoreType.DMA((3 * n,)), pltpu.SemaphoreType.DMA((3 * n,)), pltpu.SemaphoreType.DMA((n,))]


def _exchange_phases(p_refs, out_refs, send_sems, recv_sems, local_sems):
    n = len(p_refs)
    x, y, c = _place()
    mine = 2 * x + y
    peers = [(1 - x, y), (x, 1 - y), (1 - x, 1 - y)]

    def local():
        return [pltpu.make_async_copy(p_refs[a].at[mine], out_refs[a].at[mine], local_sems.at[a]) for a in range(n)]

    def copy(a, j, src_chip, dst_chip):
        px, py = peers[j]
        return pltpu.make_async_remote_copy(
            src_ref=p_refs[a].at[src_chip], dst_ref=out_refs[a].at[dst_chip], send_sem=send_sems.at[3 * a + j],
            recv_sem=recv_sems.at[3 * a + j], device_id=(px, py, c), device_id_type=MESH_ID)

    def sends():
        return [copy(a, j, 2 * px + py, mine) for a in range(n) for j, (px, py) in enumerate(peers)]

    def start():
        for cp in local() + sends():
            cp.start()

    def finish():
        for a in range(n):
            for j, (px, py) in enumerate(peers):
                copy(a, j, mine, 2 * px + py).wait_recv()
        for cp in sends():
            cp.wait_send()
        for cp in local():
            cp.wait()

    return start, finish


def _add_mine(g, recv, core, tile, dtype):
    _, _, lead, rows, width = g.shape

    def body(c_ref, g_ref, r_ref, o_ref):
        o_ref[...] = (g_ref[...] + r_ref[...]).astype(dtype)

    spec = pl.BlockSpec((None, None, tile, width), lambda k, l, i, c_ref: (k, l, i, 0))
    return pl.pallas_call(
        body, name="add_sibling", out_shape=SDS(recv.shape, dtype),
        grid_spec=pltpu.PrefetchScalarGridSpec(
            num_scalar_prefetch=1, grid=(g.shape[0], lead, rows // tile),
            in_specs=[pl.BlockSpec((None, None, None, tile, width), lambda k, l, i, c_ref: (k, c_ref[0], l, i, 0)),
                      spec],
            out_specs=spec),
        compiler_params=_cp("parallel", "parallel", "parallel"))(core, g, recv)


def _sum_adamw(parts, w, m, v, tile):
    lead, rows, width = w.shape
    last = rows // tile - 1

    def body(*refs):
        p_refs, (w_ref, m_ref, v_ref, g_ref, d_ref, nm_ref, nv_ref) = refs[:lead], refs[lead:]
        for layer in range(lead):
            @pl.when(pl.program_id(0) == layer)
            def _(p_ref=p_refs[layer]):
                g = ((p_ref[0].astype(F32) + p_ref[1].astype(F32)) + p_ref[2].astype(F32)) + p_ref[3].astype(F32)
                nm = ADAM_B1 * m_ref[...] + (1.0 - ADAM_B1) * g
                nv = ADAM_B2 * v_ref[...] + (1.0 - ADAM_B2) * jnp.square(g)
                m_hat = nm / (1.0 - ADAM_B1 ** ADAM_STEP)
                v_hat = nv / (1.0 - ADAM_B2 ** ADAM_STEP)
                g_ref[...] = g
                nm_ref[...] = nm
                nv_ref[...] = nv
                d_ref[...] = -ADAM_LR * (m_hat / (jnp.sqrt(v_hat) + ADAM_EPS) + ADAM_WD * w_ref[...])

    pspec = lambda layer: pl.BlockSpec(
        (4, None, tile, width),
        lambda l, i: (0, 0, jnp.where(l == layer, i, jnp.where(l > layer, last, 0)), 0))
    spec = pl.BlockSpec((None, tile, width), lambda l, i: (l, i, 0))
    return pl.pallas_call(
        body, grid=(lead, rows // tile), name="sum_adamw",
        in_specs=[pspec(layer) for layer in range(lead)] + [spec, spec, spec],
        out_specs=[spec] * 4, out_shape=[SDS((lead, rows, width), F32)] * 4,
        compiler_params=_cp("arbitrary", "arbitrary"))(*parts, w, m, v)


def kernel(x, p, positions, g_mix, w_in, sink, g_q, w_uq, g_kv, w_ukv, w_br_a, w_br_b, w_out, g_ple, w_ple_gate, w_ple_proj, g_final, loss_target, m_g_mix, m_w_in, m_sink, m_g_q, m_w_uq, m_g_kv, m_w_ukv, m_w_br_a, m_w_br_b, m_w_out, m_g_ple, m_w_ple_gate, m_w_ple_proj, m_g_final, v_g_mix, v_w_in, v_sink, v_g_q, v_w_uq, v_g_kv, v_w_ukv, v_w_br_a, v_w_br_b, v_w_out, v_g_ple, v_w_ple_gate, v_w_ple_proj, v_g_final):
    weights = dict(g_mix=g_mix, w_in=w_in, sink=sink, g_q=g_q, w_uq=w_uq, g_kv=g_kv, w_ukv=w_ukv, w_br_a=w_br_a,
                   w_br_b=w_br_b, w_out=w_out, g_ple=g_ple, w_ple_gate=w_ple_gate, w_ple_proj=w_ple_proj,
                   g_final=g_final)
    mom1 = dict(g_mix=m_g_mix, w_in=m_w_in, sink=m_sink, g_q=m_g_q, w_uq=m_w_uq, g_kv=m_g_kv, w_ukv=m_w_ukv,
                w_br_a=m_w_br_a, w_br_b=m_w_br_b, w_out=m_w_out, g_ple=m_g_ple, w_ple_gate=m_w_ple_gate,
                w_ple_proj=m_w_ple_proj, g_final=m_g_final)
    mom2 = dict(g_mix=v_g_mix, w_in=v_w_in, sink=v_sink, g_q=v_g_q, w_uq=v_w_uq, g_kv=v_g_kv, w_ukv=v_w_ukv,
                w_br_a=v_w_br_a, w_br_b=v_w_br_b, w_out=v_w_out, g_ple=v_g_ple, w_ple_gate=v_w_ple_gate,
                w_ple_proj=v_w_ple_proj, g_final=v_g_final)
    assert DEPTH == 2
    wide = lambda d: d["w_in"]
    rows = lambda d: _pack_rows(d)
    core = lax.axis_index("c").astype(jnp.int32).reshape(1)

    w16 = [wide(weights).astype(BF16), rows(weights).astype(BF16)]
    wts0 = dict(w_in=_win_layout(_all_gather([w16[0][:1]])[0], 256))
    small = _small_params(g_mix, sink, g_q, g_kv, g_ple, g_final)

    def wide_payload(g):
        return _win_grad_layout(g["w_in"], 256).reshape(N_DEV // 2, 2, 1, D, WIDE_W)

    def rows_payload(g):
        return _pack_rows({n: _split(n, g[n][None]) for n, _ in ROWS_PIECES}).reshape(N_DEV // 2, 2, 1, ROWS_N, LANES)

    def add(pay, got):
        tiles = {D: (WIDE_TILE, BF16), ROWS_N: (ROWS_TILE, BF16), VEC_ROWS: (VEC_ROWS, F32)}
        return [_add_mine(a, b, core, *tiles[a.shape[-2]]) for a, b in zip(pay, got)]

    plan = dict(behind_fwd_in=[w16[1][:1]], behind_mla_fwd=[a[1:] for a in w16],
                w_in=lambda blocks: _win_layout(blocks, 256),
                row_weights=_row_weights, payload=lambda g: [wide_payload(g), rows_payload(g)],
                rows_payload=rows_payload, add=add)
    loss, grad_x, grads, dg_final, rode = _local_step(x, p, positions, loss_target, small, [wts0], plan)

    vectors = {n: jnp.stack([grads[i][n] for i in range(DEPTH)]) for n, _ in SMALL[:-1]}
    vectors["g_final"] = dg_final
    pay = [wide_payload(grads[0]),
           jnp.broadcast_to(_pack_vec(vectors, loss[0, 0]), (N_DEV // 2, 2, 1, VEC_ROWS, LANES))]
    parts_wide0, parts_vec = _exchange_chips(add(pay, _swap_sibling(pay)))
    out_wide = _sum_adamw([parts_wide0, rode["layer1"][0]], wide(weights), wide(mom1), wide(mom2), WIDE_TILE)
    out_rows = _sum_adamw([rode["rows0"][0], rode["layer1"][1]], rows(weights), rows(mom1), rows(mom2), ROWS_TILE)
    out_vec = _sum_adamw([parts_vec], _pack_vec(weights), _pack_vec(mom1), _pack_vec(mom2), VEC_ROWS)

    outs = []
    for ow, orow, ovec in zip(out_wide, out_rows, out_vec):
        named = _unpack_rows(orow)
        named.update(_unpack_vec(ovec)[0])
        named["w_in"] = ow
        outs += [named[n] for n in weights]
    loss = _unpack_vec(out_vec[0])[1]
    return (loss, grad_x, *outs)
```

```python
import functools

import jax
import jax.numpy as jnp
from jax import lax
from jax.experimental import pallas as pl
from jax.experimental.pallas import tpu as pltpu

F32, BF16 = jnp.float32, jnp.bfloat16
SDS = jax.ShapeDtypeStruct

D = 1024
DEPTH = 2
PLE = 256
BLK = 128
EPS = 1e-6
NEG = -1e30
SWA_H, SWA_KV, SWA_DH = 8, 2, 64
MLA_H, MLA_NOPE, MLA_ROPE, MLA_V = 8, 64, 32, 64
MLA_QK = MLA_NOPE + MLA_ROPE
QL, KVL = 256, 128
IN_W = 4256
N_DEV = 8

V7X_VMEM_BYTES = 64 * 1024 * 1024
LANES = 128
VMEM_LIMIT = V7X_VMEM_BYTES * 7 // 8

ZW = 4352
Z_MA, Z_MB, Z_AQ, Z_AG, Z_BG, Z_QD, Z_AK, Z_AV, Z_KVD, Z_KR = 0, 1024, 2048, 2560, 3072, 3584, 3840, 3968, 4096, 4224
QFW = MLA_H * LANES
KVW = QFW + MLA_H * MLA_V
MLA_SCALE = MLA_QK ** -0.5
LOG2E = 1.4426950408889634
MLA_FWD_HEADS, MLA_BWD_HEADS = 8, 8
SWA_SCALE = SWA_DH ** -0.5
ROLL_UP, ROLL_DOWN = MLA_ROPE // 2, LANES - MLA_ROPE // 2

ADAM_LR, ADAM_B1, ADAM_B2, ADAM_EPS, ADAM_WD, ADAM_STEP = 0.001, 0.9, 0.999, 1e-08, 0.01, 10


def _cp(*sem):
    return pltpu.CompilerParams(dimension_semantics=sem, vmem_limit_bytes=VMEM_LIMIT)


def _row(tm, w, col=0):
    return pl.BlockSpec((tm, w), lambda i: (i, col))


def _res(shape, layer=None):
    if layer is None:
        return pl.BlockSpec(shape, lambda *_: (0,) * len(shape), pipeline_mode=pl.Buffered(1))
    return pl.BlockSpec((None,) + shape, lambda *_: (layer,) + (0,) * len(shape), pipeline_mode=pl.Buffered(1))


def _acc(shape):
    return pl.BlockSpec(shape, lambda *_: (0,) * len(shape))


def _rstd(xf):
    return lax.rsqrt(jnp.mean(xf * xf, axis=-1, keepdims=True) + EPS)


def _norm_bwd(dh, n, r, g):
    dn = dh * g
    return r * (dn - n * jnp.mean(dn * n, axis=-1, keepdims=True)), dh * n


def _nt(a, b):
    return lax.dot_general(a, b, (((1,), (1,)), ((), ())), preferred_element_type=F32)


def _tn(a, b):
    return lax.dot_general(a, b, (((0,), (0,)), ((), ())), preferred_element_type=F32)


def _nn(a, b):
    return jnp.dot(a, b, preferred_element_type=F32)


def _sig(x):
    return jax.nn.sigmoid(x)


def _rope(t, c, s1, s2):
    return t * c + pltpu.roll(t, ROLL_UP, 1) * s1 + pltpu.roll(t, ROLL_DOWN, 1) * s2


def _rope_t(d, c, s1, s2):
    return d * c + pltpu.roll(d * s1, ROLL_DOWN, 1) + pltpu.roll(d * s2, ROLL_UP, 1)


def _fwd_in(x, g, w, tm, layer, ride=None):
    T = x.shape[0]
    grid = (T // tm,)

    def body(x_ref, g_ref, w_ref, z_ref, h_ref):
        xf = x_ref[...]
        h = ((xf * _rstd(xf)) * g_ref[...]).astype(BF16)
        h_ref[...] = h
        z_ref[...] = _nn(h, w_ref[...])

    r_in, r_out, r_shape, r_scratch, r_args = _ride_args(ride)
    return pl.pallas_call(
        _riding(ride, body, 3, 2, grid), grid=grid, name="fwd_in_ride" if ride else "fwd_in",
        in_specs=[_row(tm, D), _res((1, D), layer), _res((D, ZW), 0)] + r_in,
        out_specs=[_row(tm, ZW), _row(tm, D)] + r_out,
        out_shape=[SDS((T, ZW), F32), SDS((T, D), BF16)] + r_shape, scratch_shapes=r_scratch,
        compiler_params=_cp("arbitrary"))(x, g, w, *r_args)


def _fwd_prep(z, gq, gkv, wq, wkv, tc, ts1, ts2, tm, layer):
    T = z.shape[0]

    def body(qd_ref, kvd_ref, kr_ref, gq_ref, gkv_ref, wq_ref, wkv_ref, c_ref, s1_ref, s2_ref, q_ref, k_ref, v_ref):
        qd, kvd = qd_ref[...], kvd_ref[...]
        hq = ((qd * _rstd(qd)) * gq_ref[...]).astype(BF16)
        hkv = ((kvd * _rstd(kvd)) * gkv_ref[...]).astype(BF16)
        qf = _nn(hq, wq_ref[...])
        kvf = _nn(hkv, wkv_ref[...])
        c, s1, s2 = c_ref[...], s1_ref[...], s2_ref[...]
        krb = _rope(kr_ref[...], c, s1, s2)
        for h in range(MLA_H):
            sl = slice(LANES * h, LANES * (h + 1))
            q_ref[:, sl] = _rope(qf[:, sl], c, s1, s2).astype(BF16)
            k_ref[:, sl] = (kvf[:, sl] + krb).astype(BF16)
        v_ref[...] = kvf[:, QFW:].astype(BF16)

    return pl.pallas_call(
        body, grid=(T // tm,), name="fwd_prep",
        in_specs=[_row(tm, QL, Z_QD // QL), _row(tm, KVL, Z_KVD // KVL), _row(tm, LANES, Z_KR // LANES),
                  _res((1, QL), layer), _res((1, KVL), layer), _res((QL, QFW), 0), _res((KVL, KVW), 0),
                  _row(tm, LANES), _row(tm, LANES), _row(tm, LANES)],
        out_specs=[_row(tm, QFW), _row(tm, QFW), _row(tm, MLA_H * MLA_V)],
        out_shape=[SDS((T, QFW), BF16), SDS((T, QFW), BF16), SDS((T, MLA_H * MLA_V), BF16)],
        compiler_params=_cp("parallel"))(z, z, z, gq, gkv, wq, wkv, tc, ts1, ts2)


def _grid_ends(grid):
    ids = [pl.program_id(a) for a in range(len(grid))]
    inner_first = functools.reduce(jnp.logical_and, [i == 0 for i in ids[1:]], True)
    last = functools.reduce(jnp.logical_and, [i == g - 1 for i, g in zip(ids, grid)])
    return (ids[0] == 0) & inner_first, (ids[0] == 3 * grid[0] // 4) & inner_first, last


class _Ride:
    def __init__(self, kind, arrays):
        self.kind, self.arrays, self.n = kind, list(arrays), len(arrays)

    def out_shape(self):
        if self.kind == "gather":
            return _gather_out(self.arrays)
        if self.kind == "swap":
            return [SDS((a.shape[0],) + a.shape[2:], a.dtype) for a in self.arrays]
        return [SDS(a.shape, a.dtype) for a in self.arrays]

    def sems(self):
        if self.kind == "gather":
            return _gather_sems(self.n)
        if self.kind == "swap":
            return _swap_sems(self.n)
        return _exchange_sems(self.n)

    def phases(self, in_refs, out_refs, *sems):
        if self.kind == "gather":
            return _gather_phases(in_refs, out_refs, *sems)
        start, finish = (_swap_phases if self.kind == "swap" else _exchange_phases)(in_refs, out_refs, *sems)
        return start, None, finish


def _riding(ride, body, n_in, n_out, grid):
    if ride is None:
        return body
    n, n_sems = ride.n, len(ride.sems())

    def wrapped(*refs):
        ins, r_in = refs[:n_in], refs[n_in:n_in + n]
        outs, r_out = refs[n_in + n:n_in + n + n_out], refs[n_in + n + n_out:n_in + 2 * n + n_out]
        rest = refs[n_in + 2 * n + n_out:]
        scratch, sems = rest[:len(rest) - n_sems], rest[len(rest) - n_sems:]
        start, middle, finish = ride.phases(r_in, r_out, *sems)
        at_first, at_middle, at_last = _grid_ends(grid)
        pl.when(at_first)(start)
        if middle is not None:
            pl.when(at_middle)(middle)
        body(*ins, *outs, *scratch)
        pl.when(at_last)(finish)

    return wrapped


def _ride_args(ride):
    if ride is None:
        return [], [], [], [], []
    return [ANY] * ride.n, [ANY] * ride.n, ride.out_shape(), ride.sems(), ride.arrays


def _mla_fwd(qf, kf, v, nb, seq, tq, gather=()):
    T = qf.shape[0]
    nq = seq // tq
    hp = MLA_FWD_HEADS
    pw = hp * LANES
    pairs = [(qi, ki) for qi in range(nq) for ki in range(qi + 1)]
    qi_tab = jnp.array([qk[0] for qk in pairs], jnp.int32)
    ki_tab = jnp.array([qk[1] for qk in pairs], jnp.int32)
    grid = (nb, MLA_H // hp, len(pairs))
    n_g = len(gather)

    def body(qi_ref, ki_ref, q_ref, k_ref, v_ref, *rest):
        x_refs, (o_ref, lse_ref), got_refs = rest[:n_g], rest[n_g:n_g + 2], rest[n_g + 2:2 * n_g + 2]
        (m_s, l_s, acc_s), sems = rest[2 * n_g + 2:2 * n_g + 5], rest[2 * n_g + 5:]
        qi, ki = qi_ref[pl.program_id(2)], ki_ref[pl.program_id(2)]
        if n_g:
            start, forward, finish = _gather_phases(x_refs, got_refs, *sems)
            at_first, at_middle, at_last = _grid_ends(grid)
            pl.when(at_first)(start)
            pl.when(at_middle)(forward)

        @pl.when(ki == 0)
        def _():
            m_s[...] = jnp.full(m_s.shape, NEG, F32)
            l_s[...] = jnp.zeros(l_s.shape, F32)
            acc_s[...] = jnp.zeros(acc_s.shape, F32)

        def step(masked):
            parts = [(0, tq // 2, tq // 2), (tq // 2, tq, tq)] if masked else [(0, tq, tq)]
            work = [(j, a, b, kh) for j in range(hp) for a, b, kh in parts]
            ss = []
            for j, a, b, kh in work:
                wide = slice(LANES * j, LANES * (j + 1))
                s = _nt(k_ref[:kh, wide], q_ref[a:b, wide]) * (MLA_SCALE * LOG2E)
                if masked:
                    keys = lax.broadcasted_iota(jnp.int32, (kh, b - a), 0)
                    queries = a + lax.broadcasted_iota(jnp.int32, (kh, b - a), 1)
                    s = jnp.where(keys <= queries, s, NEG)
                ss.append(s)
            ps, alphas = [], []
            for (j, a, b, kh), s in zip(work, ss):
                m_prev = m_s[j, :, a:b]
                m_new = jnp.maximum(m_prev, jnp.max(s, axis=0, keepdims=True))
                alpha = jnp.exp2(m_prev - m_new)
                p = jnp.exp2(s - m_new)
                l_s[j, :, a:b] = alpha * l_s[j, :, a:b] + jnp.sum(p, axis=0, keepdims=True)
                m_s[j, :, a:b] = m_new
                ps.append(p.astype(BF16))
                alphas.append(alpha)
            for (j, a, b, kh), p, alpha in zip(work, ps, alphas):
                rows = slice(MLA_V * j, MLA_V * (j + 1))
                acc_s[rows, a:b] = alpha * acc_s[rows, a:b] + _tn(v_ref[:kh, rows], p)

        @pl.when(ki < qi)
        def _():
            step(False)

        @pl.when(ki == qi)
        def _():
            step(True)
            for j in range(hp):
                rows = slice(MLA_V * j, MLA_V * (j + 1))
                acc_s[rows, :] = acc_s[rows, :] / l_s[j]
                lse_ref[j:j + 1, :] = m_s[j] + jnp.log2(l_s[j])
            o_ref[...] = acc_s[...].T

        if n_g:
            pl.when(at_last)(finish)

    q_map = lambda b, g, s, qi_ref, ki_ref: (b * nq + qi_ref[s], g)
    kv_map = lambda b, g, s, qi_ref, ki_ref: (b * nq + ki_ref[s], g)
    return pl.pallas_call(
        body, name="mla_fwd_gather" if n_g else "mla_fwd",
        grid_spec=pltpu.PrefetchScalarGridSpec(
            num_scalar_prefetch=2, grid=grid,
            in_specs=[pl.BlockSpec((tq, pw), q_map), pl.BlockSpec((tq, pw), kv_map),
                      pl.BlockSpec((tq, hp * MLA_V), kv_map)] + [ANY] * n_g,
            out_specs=[pl.BlockSpec((tq, hp * MLA_V), q_map),
                       pl.BlockSpec((hp, tq), lambda b, g, s, qi_ref, ki_ref: (g, b * nq + qi_ref[s]))]
            + [ANY] * n_g,
            scratch_shapes=[pltpu.VMEM((hp, 1, tq), F32), pltpu.VMEM((hp, 1, tq), F32),
                            pltpu.VMEM((hp * MLA_V, tq), F32)]
            + (_gather_sems(n_g) if n_g else [])),
        out_shape=[SDS((T, MLA_H * MLA_V), F32), SDS((MLA_H, T), F32)] + _gather_out(gather),
        compiler_params=_cp("arbitrary", "arbitrary", "arbitrary"))(qi_tab, ki_tab, qf, kf, v, *gather)


SWA_BLOCKS = 4


def _swa_blocks(seq):
    return min(SWA_BLOCKS, seq // BLK)


def _swa_specs(nstep, nu):
    step = nu * BLK
    cur = lambda b, m: (b * nstep + m, 0)
    prev = lambda b, m: (nu * b * nstep + jnp.maximum(nu * m - 1, 0), 0)
    kvc = Z_AK // (2 * BLK)
    return [pl.BlockSpec(memory_space=pltpu.SMEM),
            pl.BlockSpec((step, 512), lambda b, m: (b * nstep + m, Z_AQ // 512)),
            pl.BlockSpec((step, 2 * BLK), lambda b, m: (b * nstep + m, kvc)),
            pl.BlockSpec((BLK, 2 * BLK), lambda b, m: (nu * b * nstep + jnp.maximum(nu * m - 1, 0), kvc)),
            pl.BlockSpec((step, 1), cur),
            pl.BlockSpec((BLK, 1), prev),
            pl.BlockSpec((1, 1, step), lambda b, m: (b * nstep + m, 0, 0))]


def _swa_scores(m, nu, q_ref, kvc_ref, kvp_ref, pcc_ref, pcp_ref, pr_ref):
    kv = jnp.concatenate([kvp_ref[...], kvc_ref[...]], axis=0)
    kb, vb = kv[:, :BLK].astype(BF16), kv[:, BLK:].astype(BF16)
    pos_keys = jnp.concatenate([pcp_ref[...], pcc_ref[...]], axis=0)
    key = lax.broadcasted_iota(jnp.int32, (2 * BLK, BLK), 0)
    qry = lax.broadcasted_iota(jnp.int32, (2 * BLK, BLK), 1)
    in_window = (key > qry) & (key <= qry + BLK)
    valid = [in_window & ((key >= BLK) | (m > 0))] + [in_window] * (nu - 1)
    dist = [pr_ref[0][:, BLK * u:BLK * (u + 1)] - pos_keys[BLK * u:BLK * (u + 2)] for u in range(nu)]

    def band(t, u, g):
        return t[BLK * u:BLK * (u + 2), SWA_DH * g:SWA_DH * (g + 1)]

    def scores(u, h):
        g = h // (SWA_H // SWA_KV)
        qh = q_ref[BLK * u:BLK * (u + 1), SWA_DH * h:SWA_DH * (h + 1)].astype(BF16)
        s = _nt(band(kb, u, g), qh) * (SWA_SCALE * LOG2E) - (2.0 ** -(h + 1) * LOG2E) * dist[u]
        return qh, jnp.where(valid[u], s, NEG)

    return kb, vb, band, scores


def _swa_fwd(sink, z, pos_col, pos_row, nb, seq, layer):
    T = z.shape[0]
    nu = _swa_blocks(seq)
    step = nu * BLK
    nstep = seq // step
    chains = [(u, h) for u in range(nu) for h in range(SWA_H)]

    def body(sink_ref, q_ref, kvc_ref, kvp_ref, pcc_ref, pcp_ref, pr_ref, o_ref, lse_ref):
        kb, vb, band, scores = _swa_scores(pl.program_id(1), nu, q_ref, kvc_ref, kvp_ref, pcc_ref, pcp_ref, pr_ref)
        ss = [scores(u, h)[1] for u, h in chains]
        es, dens = [], []
        for (u, h), s in zip(chains, ss):
            sk = sink_ref[layer, h] * LOG2E
            m = jnp.maximum(jnp.max(s, axis=0, keepdims=True), sk)
            e = jnp.exp2(s - m)
            den = jnp.sum(e, axis=0, keepdims=True) + jnp.exp2(sk - m)
            lse_ref[h:h + 1, BLK * u:BLK * (u + 1)] = m + jnp.log2(den)
            es.append(e.astype(BF16))
            dens.append(den)
        outs = [_tn(band(vb, u, h // (SWA_H // SWA_KV)), e) / den for (u, h), e, den in zip(chains, es, dens)]
        for u in range(nu):
            o_ref[BLK * u:BLK * (u + 1), :] = jnp.concatenate(outs[SWA_H * u:SWA_H * (u + 1)], axis=0).T

    return pl.pallas_call(
        body, grid=(nb, nstep), name="swa_fwd",
        in_specs=_swa_specs(nstep, nu),
        out_specs=[pl.BlockSpec((step, 512), lambda b, m: (b * nstep + m, 0)),
                   pl.BlockSpec((SWA_H, step), lambda b, m: (0, b * nstep + m))],
        out_shape=[SDS((T, 512), F32), SDS((SWA_H, T), F32)],
        compiler_params=_cp("parallel", "parallel"))(sink, z, z, z, pos_col, pos_col, pos_row)


def _fwd_merge(x, oa, ob, z, wa, wb, wo, tm, layer):
    T = x.shape[0]

    def body(x_ref, oa_ref, ob_ref, ag_ref, bg_ref, ma_ref, mb_ref, wa_ref, wb_ref, wo_ref, x1_ref):
        ag, bg = ag_ref[...], bg_ref[...]
        ua = _nn((oa_ref[...] * (ag * _sig(ag))).astype(BF16), wa_ref[...])
        ub = _nn((ob_ref[...] * (bg * _sig(bg))).astype(BF16), wb_ref[...])
        y = _sig(ma_ref[...]) * ua + _sig(mb_ref[...]) * ub
        x1_ref[...] = x_ref[...] + _nn(y.astype(BF16), wo_ref[...])

    return pl.pallas_call(
        body, grid=(T // tm,), name="fwd_merge",
        in_specs=[_row(tm, D), _row(tm, 512), _row(tm, 512), _row(tm, 512, Z_AG // 512), _row(tm, 512, Z_BG // 512),
                  _row(tm, D, Z_MA // D), _row(tm, D, Z_MB // D),
                  _res((512, D), 0), _res((512, D), 0), _res((D, D), 0)],
        out_specs=_row(tm, D),
        out_shape=SDS((T, D), F32),
        compiler_params=_cp("parallel"))(x, oa, ob, z, z, z, z, wa, wb, wo)


def _fwd_ple(x1, p, g, wpg, wpp, tm, layer):
    T = x1.shape[0]

    def body(x_ref, p_ref, g_ref, wpg_ref, wpp_ref, x2_ref, pg_ref, pp_ref):
        xf = x_ref[...]
        h1 = ((xf * _rstd(xf)) * g_ref[...]).astype(BF16)
        pg = _sig(_nn(h1, wpg_ref[...]))
        pp = _nn(p_ref[...].astype(BF16), wpp_ref[...])
        pg_ref[...] = pg
        pp_ref[...] = pp
        x2_ref[...] = xf + pg * pp

    return pl.pallas_call(
        body, grid=(T // tm,), name="fwd_ple",
        in_specs=[_row(tm, D), pl.BlockSpec((None, tm, PLE), lambda i: (layer, i, 0)),
                  _res((1, D), layer), _res((D, D), 0), _res((PLE, D), 0)],
        out_specs=[_row(tm, D)] * 3,
        out_shape=[SDS((T, D), F32)] * 3,
        compiler_params=_cp("parallel"))(x1, p, g, wpg, wpp)


def _ple_loss(x1, p, g, wpg, wpp, g_final, tgt, tm, layer):
    T = x1.shape[0]

    def body(x_ref, p_ref, g_ref, wpg_ref, wpp_ref, gf_ref, t_ref, dx_ref, dwg_ref, dwp_ref, dg_ref, dgf_ref, loss_ref):
        @pl.when(pl.program_id(0) == 0)
        def _():
            for ref in (dwg_ref, dwp_ref, dg_ref, dgf_ref, loss_ref):
                ref[...] = jnp.zeros(ref.shape, F32)

        xf, gp, gf = x_ref[...], g_ref[...], gf_ref[...]
        r = _rstd(xf)
        n = xf * r
        h1 = (n * gp).astype(BF16)
        pb = p_ref[...].astype(BF16)
        pg = _sig(_nn(h1, wpg_ref[...]))
        pp = _nn(pb, wpp_ref[...])
        x2 = xf + pg * pp
        r2 = _rstd(x2)
        n2 = x2 * r2
        err = n2 * gf - t_ref[...]
        loss_ref[...] += 0.5 * jnp.sum(jnp.mean(err * err, axis=-1, keepdims=True), axis=0, keepdims=True)
        d, dgfr = _norm_bwd(err * (1.0 / D), n2, r2, gf)
        dgf_ref[...] += jnp.sum(dgfr, axis=0, keepdims=True)
        dpgl = (d * pp * pg * (1.0 - pg)).astype(BF16)
        dwg_ref[...] += _tn(h1, dpgl)
        dwp_ref[...] += _tn(pb, (d * pg).astype(BF16))
        dxn, dgr = _norm_bwd(_nt(dpgl, wpg_ref[...]), n, r, gp)
        dx_ref[...] = d + dxn
        dg_ref[...] += jnp.sum(dgr, axis=0, keepdims=True)

    return pl.pallas_call(
        body, grid=(T // tm,), name="ple_loss",
        in_specs=[_row(tm, D), pl.BlockSpec((None, tm, PLE), lambda i: (layer, i, 0)), _res((1, D), layer),
                  _res((D, D), 0), _res((PLE, D), 0), _res((1, D)), _row(tm, D)],
        out_specs=[_row(tm, D), _acc((D, D)), _acc((PLE, D)), _acc((1, D)), _acc((1, D)), _acc((1, LANES))],
        out_shape=[SDS((T, D), F32), SDS((D, D), F32), SDS((PLE, D), F32), SDS((1, D), F32), SDS((1, D), F32),
                   SDS((1, LANES), F32)],
        compiler_params=_cp("arbitrary"))(x1, p, g, wpg, wpp, g_final, tgt)


def _bwd_ple(dx2, x1, pg, pp, p, g, wpg, tm, layer, ride=None):
    T = x1.shape[0]
    grid = (T // tm,)

    def body(d_ref, x_ref, pg_ref, pp_ref, p_ref, g_ref, w_ref, dx_ref, dwg_ref, dwp_ref, dg_ref):
        @pl.when(pl.program_id(0) == 0)
        def _():
            dwg_ref[...] = jnp.zeros(dwg_ref.shape, F32)
            dwp_ref[...] = jnp.zeros(dwp_ref.shape, F32)
            dg_ref[...] = jnp.zeros(dg_ref.shape, F32)

        d, xf, pg, gf = d_ref[...], x_ref[...], pg_ref[...], g_ref[...]
        r = _rstd(xf)
        n = xf * r
        dpgl = (d * pp_ref[...] * pg * (1.0 - pg)).astype(BF16)
        dwg_ref[...] += _tn((n * gf).astype(BF16), dpgl)
        dwp_ref[...] += _tn(p_ref[...].astype(BF16), (d * pg).astype(BF16))
        dxn, dgr = _norm_bwd(_nt(dpgl, w_ref[...]), n, r, gf)
        dx_ref[...] = d + dxn
        dg_ref[...] += jnp.sum(dgr, axis=0, keepdims=True)

    r_in, r_out, r_shape, r_scratch, r_args = _ride_args(ride)
    return pl.pallas_call(
        _riding(ride, body, 7, 4, grid), grid=grid, name="bwd_ple_ride" if ride else "bwd_ple",
        in_specs=[_row(tm, D)] * 4 + [pl.BlockSpec((None, tm, PLE), lambda i: (layer, i, 0)),
                                      _res((1, D), layer), _res((D, D), 0)] + r_in,
        out_specs=[_row(tm, D), _acc((D, D)), _acc((PLE, D)), _acc((1, D))] + r_out,
        out_shape=[SDS((T, D), F32), SDS((D, D), F32), SDS((PLE, D), F32), SDS((1, D), F32)] + r_shape,
        scratch_shapes=r_scratch,
        compiler_params=_cp("arbitrary"))(dx2, x1, pg, pp, p, g, wpg, *r_args)


def _bwd_merge(dx1, oa, ob, z, wa, wb, wo, tm, layer):
    T = dx1.shape[0]

    def body(d_ref, oa_ref, ob_ref, ag_ref, bg_ref, ma_ref, mb_ref, wa_ref, wb_ref, wo_ref,
             doa_ref, dob_ref, dag_ref, dbg_ref, dma_ref, dmb_ref, dsa_ref, dsb_ref, dwa_ref, dwb_ref, dwo_ref):
        @pl.when(pl.program_id(0) == 0)
        def _():
            dwa_ref[...] = jnp.zeros(dwa_ref.shape, F32)
            dwb_ref[...] = jnp.zeros(dwb_ref.shape, F32)
            dwo_ref[...] = jnp.zeros(dwo_ref.shape, F32)

        db = d_ref[...].astype(BF16)
        gated = []
        for o_ref, gate_ref, w_ref in ((oa_ref, ag_ref, wa_ref), (ob_ref, bg_ref, wb_ref)):
            raw, gate = o_ref[...], gate_ref[...]
            sg = _sig(gate)
            silu = gate * sg
            ob16 = (raw * silu).astype(BF16)
            gated.append((raw, gate, sg, silu, ob16, _nn(ob16, w_ref[...])))
        ua, ub = gated[0][5], gated[1][5]
        sa, sb = _sig(ma_ref[...]), _sig(mb_ref[...])
        dwo_ref[...] += _tn((sa * ua + sb * ub).astype(BF16), db)
        dy = _nt(db, wo_ref[...])
        dma_ref[...] = (dy * ua * sa * (1.0 - sa)).astype(BF16)
        dmb_ref[...] = (dy * ub * sb * (1.0 - sb)).astype(BF16)
        for (s, w_ref, do_ref, dgate_ref, dw_ref, ds_ref), (raw, gate, sg, silu, ob16, _) in zip((
                (sa, wa_ref, doa_ref, dag_ref, dwa_ref, dsa_ref),
                (sb, wb_ref, dob_ref, dbg_ref, dwb_ref, dsb_ref)), gated):
            du = (dy * s).astype(BF16)
            dw_ref[...] += _tn(ob16, du)
            do = _nt(du, w_ref[...])
            draw = do * silu
            do_ref[...] = draw.astype(BF16)
            dgate_ref[...] = (do * raw * (sg * (1.0 + gate * (1.0 - sg)))).astype(BF16)
            ds_ref[...] = jnp.sum((draw * raw).T.reshape(MLA_H, MLA_V, tm), axis=1)

    return pl.pallas_call(
        body, grid=(T // tm,), name="bwd_merge",
        in_specs=[_row(tm, D), _row(tm, 512), _row(tm, 512), _row(tm, 512, Z_AG // 512), _row(tm, 512, Z_BG // 512),
                  _row(tm, D, Z_MA // D), _row(tm, D, Z_MB // D),
                  _res((512, D), 0), _res((512, D), 0), _res((D, D), 0)],
        out_specs=[_row(tm, 512)] * 4 + [_row(tm, D)] * 2 + [pl.BlockSpec((MLA_H, tm), lambda i: (0, i))] * 2
        + [_acc((512, D)), _acc((512, D)), _acc((D, D))],
        out_shape=[SDS((T, 512), BF16), SDS((T, 512), BF16), SDS((T, 512), BF16), SDS((T, 512), BF16),
                   SDS((T, D), BF16), SDS((T, D), BF16), SDS((MLA_H, T), F32), SDS((MLA_H, T), F32),
                   SDS((512, D), F32), SDS((512, D), F32), SDS((D, D), F32)],
        compiler_params=_cp("arbitrary"))(dx1, oa, ob, z, z, z, z, wa, wb, wo)


def _mla_bwd(qf, kf, v, do, lse, dsum, nb, seq, tq, exchange=()):
    T = qf.shape[0]
    nq = seq // tq
    hp = MLA_BWD_HEADS
    pw = hp * LANES
    pairs = [(qi, ki) for ki in range(nq) for qi in range(ki, nq)]
    qi_tab = jnp.array([qk[0] for qk in pairs], jnp.int32)
    ki_tab = jnp.array([qk[1] for qk in pairs], jnp.int32)
    grid = (nb, MLA_H // hp, len(pairs))
    n_x = len(exchange)

    def body(qi_ref, ki_ref, q_ref, k_ref, v_ref, do_ref, lse_ref, dsum_ref, *rest):
        p_refs, (dq_ref, dk_ref, dv_ref), got_refs = rest[:n_x], rest[n_x:n_x + 3], rest[n_x + 3:2 * n_x + 3]
        (dk_s, dv_s, dqt_s), sems = rest[2 * n_x + 3:2 * n_x + 6], rest[2 * n_x + 6:]
        step_id = pl.program_id(2)
        qi, ki = qi_ref[step_id], ki_ref[step_id]
        if n_x:
            start, finish = _exchange_phases(p_refs, got_refs, *sems)
            at_first, _, at_last = _grid_ends(grid)
            pl.when(at_first)(start)

        @pl.when(step_id == 0)
        def _():
            dqt_s[...] = jnp.zeros(dqt_s.shape, F32)

        @pl.when(qi == ki)
        def _():
            dk_s[...] = jnp.zeros(dk_s.shape, F32)
            dv_s[...] = jnp.zeros(dv_s.shape, F32)

        def step(masked):
            if masked:
                keys = lax.broadcasted_iota(jnp.int32, (tq, tq), 0)
                queries = lax.broadcasted_iota(jnp.int32, (tq, tq), 1)
                mask = keys <= queries
            for j in range(hp):
                wide = slice(LANES * j, LANES * (j + 1))
                sl = slice(MLA_V * j, MLA_V * (j + 1))
                q, k = q_ref[:, wide], k_ref[:, wide]
                dob = do_ref[:, sl].astype(BF16)
                s = _nt(k, q) * (MLA_SCALE * LOG2E)
                if masked:
                    s = jnp.where(mask, s, NEG)
                p = jnp.exp2(s - lse_ref[j:j + 1, :])
                dv_s[:, sl] += _nn(p.astype(BF16), dob)
                ds = (p * (_nt(v_ref[:, sl], dob) - dsum_ref[j:j + 1, :]) * MLA_SCALE).astype(BF16)
                dk_s[:, wide] += _nn(ds, q)
                dqt_s[qi, wide, :] += _tn(k, ds)

        @pl.when(qi > ki)
        def _():
            step(False)

        @pl.when(qi == ki)
        def _():
            step(True)

        @pl.when(qi == nq - 1)
        def _():
            dk_ref[...] = dk_s[...]
            dv_ref[...] = dv_s[...]

        @pl.when(step_id == len(pairs) - 1)
        def _():
            for n in range(nq):
                dq_ref[tq * n:tq * (n + 1), :] = dqt_s[n].T

        if n_x:
            pl.when(at_last)(finish)

    qmap = lambda b, g, s, qi_ref, ki_ref: (b * nq + qi_ref[s], g)
    kmap = lambda b, g, s, qi_ref, ki_ref: (b * nq + ki_ref[s], g)
    stat = pl.BlockSpec((None, hp, tq), lambda b, g, s, qi_ref, ki_ref: (g, 0, b * nq + qi_ref[s]))
    vw = hp * MLA_V
    return pl.pallas_call(
        body, name="mla_bwd_exchange" if n_x else "mla_bwd",
        grid_spec=pltpu.PrefetchScalarGridSpec(
            num_scalar_prefetch=2, grid=grid,
            in_specs=[pl.BlockSpec((tq, pw), qmap), pl.BlockSpec((tq, pw), kmap), pl.BlockSpec((tq, vw), kmap),
                      pl.BlockSpec((tq, vw), qmap), stat, stat] + [ANY] * n_x,
            out_specs=[pl.BlockSpec((seq, pw), lambda b, g, s, qi_ref, ki_ref: (b, g)),
                       pl.BlockSpec((tq, pw), kmap), pl.BlockSpec((tq, vw), kmap)] + [ANY] * n_x,
            scratch_shapes=[pltpu.VMEM((tq, pw), F32), pltpu.VMEM((tq, vw), F32), pltpu.VMEM((nq, pw, tq), F32)]
            + (_exchange_sems(n_x) if n_x else [])),
        out_shape=[SDS((T, QFW), F32), SDS((T, QFW), F32), SDS((T, MLA_H * MLA_V), F32)]
        + [SDS(a.shape, a.dtype) for a in exchange],
        compiler_params=_cp("arbitrary", "arbitrary", "arbitrary"))(qi_tab, ki_tab, qf, kf, v, do, lse, dsum, *exchange)


def _swa_bwd(sink, z, pos_col, pos_row, do, lse, dsum, nb, seq, layer, ride=None):
    T = z.shape[0]
    nu = _swa_blocks(seq)
    step = nu * BLK
    nstep = seq // step
    chains = [(u, h) for u in range(nu) for h in range(SWA_H)]

    def body(sink_ref, q_ref, kvc_ref, kvp_ref, pcc_ref, pcp_ref, pr_ref, do_ref, lse_ref, dsum_ref,
             dq_ref, dkv_ref, dsink_ref):
        b, m = pl.program_id(0), pl.program_id(1)

        @pl.when((b == 0) & (m == 0))
        def _():
            dsink_ref[...] = jnp.zeros(dsink_ref.shape, F32)

        @pl.when(m == 0)
        def _():
            dkv_ref[...] = jnp.zeros(dkv_ref.shape, F32)

        kb, vb, band, scores = _swa_scores(m, nu, q_ref, kvc_ref, kvp_ref, pcc_ref, pcp_ref, pr_ref)
        lane = lax.broadcasted_iota(jnp.int32, (1, LANES), 1)
        dsink = jnp.zeros((1, LANES), F32)
        group = lambda h: h // (SWA_H // SWA_KV)
        qs, ss, dobs, dps = [], [], [], []
        for u, h in chains:
            qh, s = scores(u, h)
            dob = do_ref[BLK * u:BLK * (u + 1), SWA_DH * h:SWA_DH * (h + 1)].astype(BF16)
            qs.append(qh)
            ss.append(s)
            dobs.append(dob)
            dps.append(_nt(band(vb, u, group(h)), dob))
        pbs, dss = [], []
        for (u, h), s, dp in zip(chains, ss, dps):
            cols = slice(BLK * u, BLK * (u + 1))
            lse, dsum = lse_ref[h:h + 1, cols], dsum_ref[h:h + 1, cols]
            p = jnp.exp2(s - lse)
            pbs.append(p.astype(BF16))
            dss.append((p * (dp - dsum) * SWA_SCALE).astype(BF16))
            dsk = jnp.sum(-jnp.exp2(sink_ref[layer, h] * LOG2E - lse) * dsum, axis=1, keepdims=True)
            dsink = dsink + jnp.where(lane == h, dsk, 0.0)
        dqs, dkv = [], [[[None, None], [None, None]] for _ in range(nu)]
        for i, (u, h) in enumerate(chains):
            g = group(h)
            dqs.append(_tn(band(kb, u, g), dss[i]))
            dk, dv = _nn(dss[i], qs[i]), _nn(pbs[i], dobs[i])
            dkv[u][g][0] = dk if dkv[u][g][0] is None else dkv[u][g][0] + dk
            dkv[u][g][1] = dv if dkv[u][g][1] is None else dkv[u][g][1] + dv
        for u in range(nu):
            dq_ref[BLK * u:BLK * (u + 1), :] = jnp.concatenate(dqs[SWA_H * u:SWA_H * (u + 1)], axis=0).T.astype(BF16)
        dsink_ref[...] += dsink
        upd = [jnp.concatenate([dkv[u][0][0], dkv[u][1][0], dkv[u][0][1], dkv[u][1][1]], axis=1) for u in range(nu)]
        base = pl.multiple_of(m * step, step)
        for u in range(nu):
            own = upd[u][BLK:] + upd[u + 1][:BLK] if u + 1 < nu else upd[u][BLK:]
            dkv_ref[pl.ds(base + BLK * u, BLK), :] += own

        @pl.when(m > 0)
        def _():
            dkv_ref[pl.ds(pl.multiple_of(m * step - BLK, BLK), BLK), :] += upd[0][:BLK]

    r_in, r_out, r_shape, r_scratch, r_args = _ride_args(ride)
    return pl.pallas_call(
        _riding(ride, body, 10, 3, (nb, nstep)), grid=(nb, nstep), name="swa_bwd_ride" if ride else "swa_bwd",
        in_specs=_swa_specs(nstep, nu) + [pl.BlockSpec((step, 512), lambda b, m: (b * nstep + m, 0))]
        + [pl.BlockSpec((SWA_H, step), lambda b, m: (0, b * nstep + m))] * 2 + r_in,
        out_specs=[pl.BlockSpec((step, 512), lambda b, m: (b * nstep + m, 0)),
                   pl.BlockSpec((seq, 2 * BLK), lambda b, m: (b, 0)),
                   pl.BlockSpec((1, LANES), lambda b, m: (0, 0))] + r_out,
        out_shape=[SDS((T, 512), BF16), SDS((T, 2 * BLK), F32), SDS((1, LANES), F32)] + r_shape,
        scratch_shapes=r_scratch,
        compiler_params=_cp("arbitrary", "arbitrary"))(sink, z, z, z, pos_col, pos_col, pos_row, do, lse, dsum,
                                                       *r_args)


def _bwd_prep(dq, dk, dv, z, gq, gkv, wq, wkv, tc, ts1, ts2, tm, layer):
    T = z.shape[0]

    def body(dq_ref, dk_ref, dv_ref, qd_ref, kvd_ref, gq_ref, gkv_ref, wq_ref, wkv_ref, c_ref, s1_ref, s2_ref,
             dqd_ref, dkvd_ref, dkr_ref, dwq_ref, dwkv_ref, dgq_ref, dgkv_ref, dqb_s, dkvb_s):
        @pl.when(pl.program_id(0) == 0)
        def _():
            for ref in (dwq_ref, dwkv_ref, dgq_ref, dgkv_ref):
                ref[...] = jnp.zeros(ref.shape, F32)

        c, s1, s2 = c_ref[...], s1_ref[...], s2_ref[...]
        lane = lax.broadcasted_iota(jnp.int32, (1, LANES), 1)
        rope_lanes = (lane >= MLA_NOPE) & (lane < MLA_QK)
        dkb = jnp.zeros((tm, LANES), F32)
        for h in range(MLA_H):
            sl = slice(LANES * h, LANES * (h + 1))
            dqb_s[:, sl] = _rope_t(dq_ref[:, sl], c, s1, s2).astype(BF16)
            dkh = dk_ref[:, sl]
            dkb = dkb + dkh
            dkvb_s[:, sl] = dkh.astype(BF16)
        dkvb_s[:, QFW:] = dv_ref[...].astype(BF16)
        dkr_ref[...] = _rope_t(jnp.where(rope_lanes, dkb, 0.0), c, s1, s2).astype(BF16)

        for (x_ref, g_ref, w_ref, d_s, dx_ref, dw_ref, dg_ref) in (
                (qd_ref, gq_ref, wq_ref, dqb_s, dqd_ref, dwq_ref, dgq_ref),
                (kvd_ref, gkv_ref, wkv_ref, dkvb_s, dkvd_ref, dwkv_ref, dgkv_ref)):
            xf, gf, db = x_ref[...], g_ref[...], d_s[...]
            r = _rstd(xf)
            n = xf * r
            dw_ref[...] += _tn((n * gf).astype(BF16), db)
            dx, dgr = _norm_bwd(_nt(db, w_ref[...]), n, r, gf)
            dx_ref[...] = dx.astype(BF16)
            dg_ref[...] += jnp.sum(dgr, axis=0, keepdims=True)

    return pl.pallas_call(
        body, grid=(T // tm,), name="bwd_prep",
        in_specs=[_row(tm, QFW), _row(tm, QFW), _row(tm, MLA_H * MLA_V),
                  _row(tm, QL, Z_QD // QL), _row(tm, KVL, Z_KVD // KVL),
                  _res((1, QL), layer), _res((1, KVL), layer), _res((QL, QFW), 0), _res((KVL, KVW), 0),
                  _row(tm, LANES), _row(tm, LANES), _row(tm, LANES)],
        out_specs=[_row(tm, QL), _row(tm, KVL), _row(tm, LANES),
                   _acc((QL, QFW)), _acc((KVL, KVW)), _acc((1, QL)), _acc((1, KVL))],
        out_shape=[SDS((T, QL), BF16), SDS((T, KVL), BF16), SDS((T, LANES), BF16),
                   SDS((QL, QFW), F32), SDS((KVL, KVW), F32), SDS((1, QL), F32), SDS((1, KVL), F32)],
        scratch_shapes=[pltpu.VMEM((tm, QFW), BF16), pltpu.VMEM((tm, KVW), BF16)],
        compiler_params=_cp("arbitrary"))(dq, dk, dv, z, z, gq, gkv, wq, wkv, tc, ts1, ts2)


def _bwd_in(pieces, x, g, dres, w, tm, layer):
    T = x.shape[0]
    grid = (T // tm,)
    widths = [pc.shape[1] for pc in pieces]
    assert sum(widths) == ZW
    n_p = len(pieces)

    def body(*refs):
        p_refs, (x_ref, g_ref, r_ref, w_ref, dx_ref, dz_ref, dg_ref) = refs[:n_p], refs[n_p:]

        @pl.when(pl.program_id(0) == 0)
        def _():
            dg_ref[...] = jnp.zeros(dg_ref.shape, F32)

        off = 0
        for ref, wd in zip(p_refs, widths):
            dz_ref[:, off:off + wd] = ref[...].astype(BF16)
            off += wd
        xf, gf = x_ref[...], g_ref[...]
        r = _rstd(xf)
        n = xf * r
        dx, dgr = _norm_bwd(_nt(dz_ref[...], w_ref[...]), n, r, gf)
        dx_ref[...] = r_ref[...] + dx
        dg_ref[...] += jnp.sum(dgr, axis=0, keepdims=True)

    return pl.pallas_call(
        body, grid=grid, name="bwd_in",
        in_specs=[_row(tm, wd) for wd in widths] + [_row(tm, D), _res((1, D), layer), _row(tm, D),
                                                    _res((D, ZW), 0)],
        out_specs=[_row(tm, D), _row(tm, ZW), _acc((1, D))],
        out_shape=[SDS((T, D), F32), SDS((T, ZW), BF16), SDS((1, D), F32)],
        compiler_params=_cp("arbitrary"))(*pieces, x, g, dres, w)


def _wgrad_in(hb, dzb, tm, ride=None):
    T = hb.shape[0]
    half = ZW // 2
    grid = (2, T // tm)

    def body(h_ref, dz_ref, dw_ref):
        @pl.when(pl.program_id(1) == 0)
        def _():
            dw_ref[...] = jnp.zeros(dw_ref.shape, F32)

        dw_ref[...] += _tn(h_ref[...], dz_ref[...])

    r_in, r_out, r_shape, r_scratch, r_args = _ride_args(ride)
    out = pl.pallas_call(
        _riding(ride, body, 2, 1, grid), grid=grid, name="wgrad_in_ride" if ride else "wgrad_in",
        in_specs=[pl.BlockSpec((tm, D), lambda j, t: (t, 0)), pl.BlockSpec((tm, half), lambda j, t: (t, j))] + r_in,
        out_specs=[pl.BlockSpec((D, half), lambda j, t: (0, j))] + r_out,
        out_shape=[SDS((D, ZW), F32)] + r_shape, scratch_shapes=r_scratch,
        compiler_params=_cp("arbitrary", "arbitrary"))(hb, dzb, *r_args)
    return out


IN_PIECES = ((0, 512, Z_AQ), (512, 128, Z_AK), (640, 128, Z_AV), (768, 512, Z_AG), (1280, 256, Z_QD),
             (1536, 128, Z_KVD), (1664, MLA_ROPE, Z_KR + MLA_NOPE), (1696, 512, Z_BG), (2208, 1024, Z_MA),
             (3232, 1024, Z_MB))
WIDE_W = IN_W // N_DEV


def _column_runs():
    runs = []
    for start, width, kstart in IN_PIECES:
        col = start
        while col < start + width:
            dev = col // WIDE_W
            stop = min(start + width, (dev + 1) * WIDE_W)
            runs.append((dev, col - dev * WIDE_W, stop - col, kstart + col - start))
            col = stop
    return runs


def _win_layout(blocks, tm):
    runs = _column_runs()

    def body(g_ref, o_ref):
        o_ref[:, Z_KR:Z_KR + LANES] = jnp.zeros((tm, LANES), o_ref.dtype)
        for dev, lo, n, k in runs:
            o_ref[:, k:k + n] = g_ref[dev, :, lo:lo + n]

    return pl.pallas_call(
        body, grid=(D // tm,), name="win_layout",
        in_specs=[pl.BlockSpec((N_DEV, None, tm, WIDE_W), lambda i: (0, 0, i, 0))],
        out_specs=pl.BlockSpec((None, tm, ZW), lambda i: (0, i, 0)),
        out_shape=SDS((1, D, ZW), blocks.dtype),
        compiler_params=_cp("parallel"))(blocks)


def _win_grad_layout(dw, tm):
    runs = _column_runs()

    def body(g_ref, o_ref):
        for dev, lo, n, k in runs:
            o_ref[dev, :, lo:lo + n] = g_ref[:, k:k + n]

    return pl.pallas_call(
        body, grid=(D // tm,), name="win_grad_layout",
        in_specs=[_row(tm, ZW)],
        out_specs=pl.BlockSpec((N_DEV, None, tm, WIDE_W), lambda i: (0, 0, i, 0)),
        out_shape=SDS((N_DEV, 1, D, WIDE_W), F32),
        compiler_params=_cp("parallel"))(dw)


def _wuq_to_kernel(w):
    w = w.reshape(w.shape[:-1] + (MLA_H, MLA_QK))
    w = jnp.pad(w, [(0, 0)] * (w.ndim - 1) + [(0, LANES - MLA_QK)])
    return w.reshape(w.shape[:-2] + (QFW,))


def _wuq_from_kernel(g):
    g = g.reshape(g.shape[:-1] + (MLA_H, LANES))[..., :MLA_QK]
    return g.reshape(g.shape[:-2] + (MLA_H * MLA_QK,))


def _wukv_to_kernel(w):
    w = w.reshape(w.shape[:-1] + (MLA_H, MLA_NOPE + MLA_V))
    k = jnp.pad(w[..., :MLA_NOPE], [(0, 0)] * (w.ndim - 1) + [(0, LANES - MLA_NOPE)])
    v = w[..., MLA_NOPE:]
    return jnp.concatenate([k.reshape(k.shape[:-2] + (QFW,)), v.reshape(v.shape[:-2] + (MLA_H * MLA_V,))], axis=-1)


def _wukv_from_kernel(g):
    k = g[..., :QFW].reshape(g.shape[:-1] + (MLA_H, LANES))[..., :MLA_NOPE]
    v = g[..., QFW:].reshape(g.shape[:-1] + (MLA_H, MLA_V))
    kv = jnp.concatenate([k, v], axis=-1)
    return kv.reshape(kv.shape[:-2] + (MLA_H * (MLA_NOPE + MLA_V),))


def _rope_tables(pos):
    half = MLA_ROPE // 2
    inv = 10000.0 ** (-jnp.arange(0, MLA_ROPE, 2, dtype=F32) / MLA_ROPE)
    ang = pos.astype(F32)[:, None] * inv
    cos, sin = jnp.cos(ang), jnp.sin(ang)
    one = jnp.ones((pos.shape[0], MLA_NOPE), F32)
    zero = lambda n: jnp.zeros((pos.shape[0], n), F32)
    tc = jnp.concatenate([one, cos, cos, one[:, :LANES - MLA_QK]], axis=1)
    ts1 = jnp.concatenate([zero(MLA_NOPE + half), sin, zero(LANES - MLA_QK)], axis=1)
    ts2 = jnp.concatenate([zero(MLA_NOPE), -sin, zero(LANES - MLA_NOPE - half)], axis=1)
    return tc, ts1, ts2


def _local_step(x, p, positions, loss_target, small, wts, plan=None):
    nb, seq, _ = x.shape
    T = nb * seq
    tm = min(512, T)
    tl = min(1024, T)
    tq = min(512, seq)
    xf = x.reshape(T, D)
    pos = positions.reshape(T)
    posf = pos.astype(F32)
    swa_step = _swa_blocks(seq) * BLK
    pos_col, pos_row = posf.reshape(T, 1), posf.reshape(T // swa_step, 1, swa_step)
    tc, ts1, ts2 = _rope_tables(pos)

    wts, sm = list(wts), small
    pl_in = p.reshape(DEPTH, T, PLE)
    saved = []
    for i in range(DEPTH):
        riding = plan is not None and i == 0
        w = wts[i]
        z, hb, *got = _fwd_in(xf, sm["g_mix"], w["w_in"], tm, i,
                              ride=_Ride("gather", plan["behind_fwd_in"]) if riding else None)
        if riding:
            w = wts[0] = dict(w, **plan["row_weights"](got[0]))
        oa, lse_a = _swa_fwd(sm["sink"], z, pos_col, pos_row, nb, seq, i)
        qf, kf, v = _fwd_prep(z, sm["g_q"], sm["g_kv"], w["w_uq"], w["w_ukv"], tc, ts1, ts2, tl, i)
        ob, lse_b, *got = _mla_fwd(qf, kf, v, nb, seq, tq, gather=plan["behind_mla_fwd"] if riding else ())
        if riding:
            wts.append(dict(w_in=plan["w_in"](got[0]), **plan["row_weights"](got[1])))
        x1 = _fwd_merge(xf, oa, ob, z, w["w_br_a"], w["w_br_b"], w["w_out"], tm, i)
        saved.append(dict(x=xf, z=z, hb=hb, oa=oa, lse_a=lse_a, qf=qf, kf=kf, v=v, ob=ob, lse_b=lse_b, x1=x1))
        if i < DEPTH - 1:
            xf, saved[i]["pg"], saved[i]["pp"] = _fwd_ple(x1, pl_in, sm["g_ple"], w["w_ple_gate"], w["w_ple_proj"],
                                                          tl, i)

    last = _ple_loss(x1, pl_in, sm["g_ple"], w["w_ple_gate"], w["w_ple_proj"], small["g_final"],
                     loss_target.reshape(T, D), tm, DEPTH - 1)
    dg_final, loss = last[4], last[5]

    grads = [None] * DEPTH
    exchanged = {}
    for i in reversed(range(DEPTH)):
        riding = plan is not None and i == 0
        sv, w = saved[i], wts[i]
        pay = plan["payload"](grads[1]) if riding else []
        if i == DEPTH - 1:
            (dx1, dwpg, dwpp, dg_ple), got = last[:4], []
        else:
            dx1, dwpg, dwpp, dg_ple, *got = _bwd_ple(dx, sv["x1"], sv["pg"], sv["pp"], pl_in, sm["g_ple"],
                                                     w["w_ple_gate"], tm, i,
                                                     ride=_Ride("swap", pay) if riding else None)
        doa, dob, dag, dbg, dma, dmb, dsum_a, dsum_b, dwa, dwb, dwo = _bwd_merge(
            dx1, sv["oa"], sv["ob"], sv["z"], w["w_br_a"], w["w_br_b"], w["w_out"], tm, i)
        stats = (MLA_H // MLA_BWD_HEADS, MLA_BWD_HEADS, T)
        dq_b, dk_b, dv_b, *exchanged["layer1"] = _mla_bwd(
            sv["qf"], sv["kf"], sv["v"], dob, sv["lse_b"].reshape(stats), dsum_b.reshape(stats), nb, seq, tq,
            exchange=plan["add"](pay, got) if riding else ())
        dqd, dkvd, dkr, dwq, dwkv, dgq, dgkv = _bwd_prep(dq_b, dk_b, dv_b, sv["z"], sm["g_q"], sm["g_kv"],
                                                         w["w_uq"], w["w_ukv"], tc, ts1, ts2, tl, i)
        g = dict(w_uq=_wuq_from_kernel(dwq), w_ukv=_wukv_from_kernel(dwkv), w_br_a=dwa, w_br_b=dwb, w_out=dwo,
                 w_ple_gate=dwpg, w_ple_proj=dwpp)
        pay = [plan["rows_payload"](g)] if riding else []
        dq_a, dkv_a, dsink, *got = _swa_bwd(sm["sink"], sv["z"], pos_col, pos_row, doa, sv["lse_a"], dsum_a, nb, seq,
                                            i, ride=_Ride("swap", pay) if riding else None)
        dx, dzb, dg_mix = _bwd_in([dma, dmb, dq_a, dag, dbg, dqd, dkv_a, dkvd, dkr], sv["x"], sm["g_mix"], dx1,
                                  w["w_in"], tm, i)
        dwin, *exchanged["rows0"] = _wgrad_in(sv["hb"], dzb, tm,
                                              ride=_Ride("exchange", plan["add"](pay, got)) if riding else None)
        g.update(g_mix=dg_mix[0], w_in=dwin, sink=dsink[0, :SWA_H], g_q=dgq[0], g_kv=dgkv[0], g_ple=dg_ple[0])
        grads[i] = g
    return loss, dx.reshape(nb, seq, D), grads, dg_final[0], exchanged


def _row_weights(rows):
    blocks = _unpack_rows(rows)
    out = {n: _join(n, blocks[n]) for n, _ in ROWS_PIECES}
    out.update(w_uq=_wuq_to_kernel(out["w_uq"]), w_ukv=_wukv_to_kernel(out["w_ukv"]))
    return out


def _small_params(g_mix, sink, g_q, g_kv, g_ple, g_final):
    return dict(g_mix=g_mix[:, None], sink=sink, g_q=g_q[:, None], g_kv=g_kv[:, None], g_ple=g_ple[:, None],
                g_final=g_final[None])


UQ_W = MLA_H * MLA_QK // N_DEV
ROWS_PIECES = (("w_uq", QL), ("w_ukv", KVL), ("w_br_a", 512), ("w_br_b", 512), ("w_out", D), ("w_ple_gate", D),
               ("w_ple_proj", PLE))
SMALL = (("g_mix", (DEPTH, D)), ("sink", (DEPTH, SWA_H)), ("g_q", (DEPTH, QL)), ("g_kv", (DEPTH, KVL)),
         ("g_ple", (DEPTH, D)), ("g_final", (D,)))
VEC_ROWS = 48
ROWS_N = sum(r for _, r in ROWS_PIECES)
WIDE_TILE, ROWS_TILE = 512, ROWS_N // 2


def _to_rows(name, a):
    if name == "w_uq":
        a = jnp.pad(a, [(0, 0)] * (a.ndim - 1) + [(0, LANES - UQ_W)])
    return a.reshape(a.shape[:-2] + (-1, LANES))


def _from_rows(name, r):
    if name in ("w_out", "w_ple_gate"):
        return r.reshape(r.shape[:-2] + (D // N_DEV, D))
    return r[..., :UQ_W] if name == "w_uq" else r


def _pack_rows(blocks):
    return jnp.concatenate([_to_rows(n, blocks[n]) for n, _ in ROWS_PIECES], axis=-2)


def _unpack_rows(rows):
    blocks, off = {}, 0
    for n, r in ROWS_PIECES:
        blocks[n] = _from_rows(n, rows[..., off:off + r, :])
        off += r
    return blocks


def _pack_vec(vectors, loss=None):
    parts = [vectors[n].reshape(-1) for n, _ in SMALL] + ([] if loss is None else [loss.reshape(1)])
    vec = jnp.concatenate(parts)
    return jnp.pad(vec, (0, VEC_ROWS * LANES - vec.shape[0])).reshape(1, VEC_ROWS, LANES)


def _unpack_vec(vec):
    vec = vec.reshape(-1)
    vectors, off = {}, 0
    for n, shp in SMALL:
        size = 1
        for s in shp:
            size *= s
        vectors[n] = vec[off:off + size].reshape(shp)
        off += size
    return vectors, vec[off]


def _join(name, blocks):
    if name in ("w_out", "w_ple_gate"):
        return jnp.moveaxis(blocks, 0, 1).reshape(blocks.shape[1], -1, blocks.shape[-1])
    return jnp.moveaxis(blocks, 0, 2).reshape(blocks.shape[1], blocks.shape[2], -1)


def _split(name, full):
    if name in ("w_out", "w_ple_gate"):
        return jnp.moveaxis(full.reshape(full.shape[0], N_DEV, -1, full.shape[-1]), 1, 0)
    return jnp.moveaxis(full.reshape(full.shape[0], full.shape[1], N_DEV, -1), 2, 0)


MESH_ID = pl.DeviceIdType.MESH
ANY = pl.BlockSpec(memory_space=pl.ANY)


def _place():
    return lax.axis_index("x"), lax.axis_index("y"), lax.axis_index("c")


def _all_gather(blocks):
    n = len(blocks)

    def body(*refs):
        start, forward, finish = _gather_phases(refs[:n], refs[n:2 * n], *refs[2 * n:])
        start()
        forward()
        finish()

    return pl.pallas_call(
        body, name="all_gather_weights", out_shape=_gather_out(blocks),
        in_specs=[ANY] * n, out_specs=[ANY] * n, scratch_shapes=_gather_sems(n))(*blocks)


def _gather_out(blocks):
    return [SDS((N_DEV,) + b.shape, b.dtype) for b in blocks]


def _gather_sems(n):
    return [pltpu.SemaphoreType.DMA((7 * n,)), pltpu.SemaphoreType.DMA((7 * n,)), pltpu.SemaphoreType.DMA((n,))]


def _gather_phases(x_refs, out_refs, send_sems, recv_sems, local_sems):
    n = len(x_refs)
    x, y, c = _place()
    me, sibling = (x, y, c), (x, y, 1 - c)
    chips = [(1 - x, y), (x, 1 - y), (1 - x, 1 - y)]

    def slot(a, px, py, pc):
        return out_refs[a].at[4 * px + 2 * py + pc]

    def copy(a, k, blk, to, src=None):
        return pltpu.make_async_remote_copy(
            src_ref=slot(a, *blk) if src is None else src, dst_ref=slot(a, *blk),
            send_sem=send_sems.at[7 * a + k], recv_sem=recv_sems.at[7 * a + k], device_id=to,
            device_id_type=MESH_ID)

    def mine():
        return [pltpu.make_async_copy(x_refs[a], slot(a, *me), local_sems.at[a]) for a in range(n)]

    def first():
        out = []
        for a in range(n):
            out += [copy(a, 0, me, sibling, src=x_refs[a])]
            out += [copy(a, 1 + j, me, (*chip, c), src=x_refs[a]) for j, chip in enumerate(chips)]
        return out

    def passed():
        return [copy(a, 4 + j, (*chip, c), sibling) for j, chip in enumerate(chips) for a in range(n)]

    def start():
        for cp in mine() + first():
            cp.start()

    def forward():
        for j, chip in enumerate(chips):
            for a in range(n):
                copy(a, 1 + j, (*chip, c), me).wait_recv()
                copy(a, 4 + j, (*chip, c), sibling).start()

    def finish():
        for a in range(n):
            copy(a, 0, sibling, me).wait_recv()
            for j, chip in enumerate(chips):
                copy(a, 4 + j, (*chip, 1 - c), me).wait_recv()
        for cp in first() + passed():
            cp.wait_send()
        for cp in mine():
            cp.wait()

    return start, forward, finish


def _swap_sibling(arrs):
    n = len(arrs)

    def body(*refs):
        start, finish = _swap_phases(refs[:n], refs[n:2 * n], *refs[2 * n:])
        start()
        finish()

    return pl.pallas_call(
        body, name="swap_sibling", out_shape=[SDS((a.shape[0],) + a.shape[2:], a.dtype) for a in arrs],
        in_specs=[ANY] * n, out_specs=[ANY] * n, scratch_shapes=_swap_sems(n))(*arrs)


def _swap_sems(n):
    return [pltpu.SemaphoreType.DMA((n,)), pltpu.SemaphoreType.DMA((n,))]


def _swap_phases(a_refs, out_refs, send_sems, recv_sems):
    x, y, c = _place()

    def copies():
        return [pltpu.make_async_remote_copy(
            src_ref=a_refs[a].at[:, 1 - c], dst_ref=out_refs[a], send_sem=send_sems.at[a], recv_sem=recv_sems.at[a],
            device_id=(x, y, 1 - c), device_id_type=MESH_ID) for a in range(len(a_refs))]

    def start():
        for cp in copies():
            cp.start()

    def finish():
        for cp in copies():
            cp.wait()

    return start, finish


def _exchange_chips(arrs):
    n = len(arrs)

    def body(*refs):
        start, finish = _exchange_phases(refs[:n], refs[n:2 * n], *refs[2 * n:])
        start()
        finish()

    return pl.pallas_call(
        body, name="exchange_chips", out_shape=[SDS(a.shape, a.dtype) for a in arrs],
        in_specs=[ANY] * n, out_specs=[ANY] * n, scratch_shapes=_exchange_sems(n))(*arrs)


def _exchange_sems(n):
    return [pltpu.SemaphoreType.DMA((3 * n,)), pltpu.SemaphoreType.DMA((3 * n,)), pltpu.SemaphoreType.DMA((n,))]


def _exchange_phases(p_refs, out_refs, send_sems, recv_sems, local_sems):
    n = len(p_refs)
    x, y, c = _place()
    mine = 2 * x + y
    peers = [(1 - x, y), (x, 1 - y), (1 - x, 1 - y)]

    def local():
        return [pltpu.make_async_copy(p_refs[a].at[mine], out_refs[a].at[mine], local_sems.at[a]) for a in range(n)]

    def copy(a, j, src_chip, dst_chip):
        px, py = peers[j]
        return pltpu.make_async_remote_copy(
            src_ref=p_refs[a].at[src_chip], dst_ref=out_refs[a].at[dst_chip], send_sem=send_sems.at[3 * a + j],
            recv_sem=recv_sems.at[3 * a + j], device_id=(px, py, c), device_id_type=MESH_ID)

    def sends():
        return [copy(a, j, 2 * px + py, mine) for a in range(n) for j, (px, py) in enumerate(peers)]

    def start():
        for cp in local() + sends():
            cp.start()

    def finish():
        for a in range(n):
            for j, (px, py) in enumerate(peers):
                copy(a, j, mine, 2 * px + py).wait_recv()
        for cp in sends():
            cp.wait_send()
        for cp in local():
            cp.wait()

    return start, finish


def _add_mine(g, recv, core, tile, dtype):
    _, _, lead, rows, width = g.shape

    def body(c_ref, g_ref, r_ref, o_ref):
        o_ref[...] = (g_ref[...] + r_ref[...]).astype(dtype)

    spec = pl.BlockSpec((None, None, tile, width), lambda k, l, i, c_ref: (k, l, i, 0))
    return pl.pallas_call(
        body, name="add_sibling", out_shape=SDS(recv.shape, dtype),
        grid_spec=pltpu.PrefetchScalarGridSpec(
            num_scalar_prefetch=1, grid=(g.shape[0], lead, rows // tile),
            in_specs=[pl.BlockSpec((None, None, None, tile, width), lambda k, l, i, c_ref: (k, c_ref[0], l, i, 0)),
                      spec],
            out_specs=spec),
        compiler_params=_cp("parallel", "parallel", "parallel"))(core, g, recv)


def _sum_adamw(parts, w, m, v, tile):
    lead, rows, width = w.shape
    last = rows // tile - 1

    def body(*refs):
        p_refs, (w_ref, m_ref, v_ref, g_ref, d_ref, nm_ref, nv_ref) = refs[:lead], refs[lead:]
        for layer in range(lead):
            @pl.when(pl.program_id(0) == layer)
            def _(p_ref=p_refs[layer]):
                g = ((p_ref[0].astype(F32) + p_ref[1].astype(F32)) + p_ref[2].astype(F32)) + p_ref[3].astype(F32)
                nm = ADAM_B1 * m_ref[...] + (1.0 - ADAM_B1) * g
                nv = ADAM_B2 * v_ref[...] + (1.0 - ADAM_B2) * jnp.square(g)
                m_hat = nm / (1.0 - ADAM_B1 ** ADAM_STEP)
                v_hat = nv / (1.0 - ADAM_B2 ** ADAM_STEP)
                g_ref[...] = g
                nm_ref[...] = nm
                nv_ref[...] = nv
                d_ref[...] = -ADAM_LR * (m_hat / (jnp.sqrt(v_hat) + ADAM_EPS) + ADAM_WD * w_ref[...])

    pspec = lambda layer: pl.BlockSpec(
        (4, None, tile, width),
        lambda l, i: (0, 0, jnp.where(l == layer, i, jnp.where(l > layer, last, 0)), 0))
    spec = pl.BlockSpec((None, tile, width), lambda l, i: (l, i, 0))
    return pl.pallas_call(
        body, grid=(lead, rows // tile), name="sum_adamw",
        in_specs=[pspec(layer) for layer in range(lead)] + [spec, spec, spec],
        out_specs=[spec] * 4, out_shape=[SDS((lead, rows, width), F32)] * 4,
        compiler_params=_cp("arbitrary", "arbitrary"))(*parts, w, m, v)


def kernel(x, p, positions, g_mix, w_in, sink, g_q, w_uq, g_kv, w_ukv, w_br_a, w_br_b, w_out, g_ple, w_ple_gate, w_ple_proj, g_final, loss_target, m_g_mix, m_w_in, m_sink, m_g_q, m_w_uq, m_g_kv, m_w_ukv, m_w_br_a, m_w_br_b, m_w_out, m_g_ple, m_w_ple_gate, m_w_ple_proj, m_g_final, v_g_mix, v_w_in, v_sink, v_g_q, v_w_uq, v_g_kv, v_w_ukv, v_w_br_a, v_w_br_b, v_w_out, v_g_ple, v_w_ple_gate, v_w_ple_proj, v_g_final):
    weights = dict(g_mix=g_mix, w_in=w_in, sink=sink, g_q=g_q, w_uq=w_uq, g_kv=g_kv, w_ukv=w_ukv, w_br_a=w_br_a,
                   w_br_b=w_br_b, w_out=w_out, g_ple=g_ple, w_ple_gate=w_ple_gate, w_ple_proj=w_ple_proj,
                   g_final=g_final)
    mom1 = dict(g_mix=m_g_mix, w_in=m_w_in, sink=m_sink, g_q=m_g_q, w_uq=m_w_uq, g_kv=m_g_kv, w_ukv=m_w_ukv,
                w_br_a=m_w_br_a, w_br_b=m_w_br_b, w_out=m_w_out, g_ple=m_g_ple, w_ple_gate=m_w_ple_gate,
                w_ple_proj=m_w_ple_proj, g_final=m_g_final)
    mom2 = dict(g_mix=v_g_mix, w_in=v_w_in, sink=v_sink, g_q=v_g_q, w_uq=v_w_uq, g_kv=v_g_kv, w_ukv=v_w_ukv,
                w_br_a=v_w_br_a, w_br_b=v_w_br_b, w_out=v_w_out, g_ple=v_g_ple, w_ple_gate=v_w_ple_gate,
                w_ple_proj=v_w_ple_proj, g_final=v_g_final)
    assert DEPTH == 2
    wide = lambda d: d["w_in"]
    rows = lambda d: _pack_rows(d)
    core = lax.axis_index("c").astype(jnp.int32).reshape(1)

    w16 = [wide(weights).astype(BF16), rows(weights).astype(BF16)]
    wts0 = dict(w_in=_win_layout(_all_gather([w16[0][:1]])[0], 256))
    small = _small_params(g_mix, sink, g_q, g_kv, g_ple, g_final)

    def wide_payload(g):
        return _win_grad_layout(g["w_in"], 256).reshape(N_DEV // 2, 2, 1, D, WIDE_W)

    def rows_payload(g):
        return _pack_rows({n: _split(n, g[n][None]) for n, _ in ROWS_PIECES}).reshape(N_DEV // 2, 2, 1, ROWS_N, LANES)

    def add(pay, got):
        tiles = {D: (WIDE_TILE, BF16), ROWS_N: (ROWS_TILE, BF16), VEC_ROWS: (VEC_ROWS, F32)}
        return [_add_mine(a, b, core, *tiles[a.shape[-2]]) for a, b in zip(pay, got)]

    plan = dict(behind_fwd_in=[w16[1][:1]], behind_mla_fwd=[a[1:] for a in w16],
                w_in=lambda blocks: _win_layout(blocks, 256),
                row_weights=_row_weights, payload=lambda g: [wide_payload(g), rows_payload(g)],
                rows_payload=rows_payload, add=add)
    loss, grad_x, grads, dg_final, rode = _local_step(x, p, positions, loss_target, small, [wts0], plan)

    vectors = {n: jnp.stack([grads[i][n] for i in range(DEPTH)]) for n, _ in SMALL[:-1]}
    vectors["g_final"] = dg_final
    pay = [wide_payload(grads[0]),
           jnp.broadcast_to(_pack_vec(vectors, loss[0, 0]), (N_DEV // 2, 2, 1, VEC_ROWS, LANES))]
    parts_wide0, parts_vec = _exchange_chips(add(pay, _swap_sibling(pay)))
    out_wide = _sum_adamw([parts_wide0, rode["layer1"][0]], wide(weights), wide(mom1), wide(mom2), WIDE_TILE)
    out_rows = _sum_adamw([rode["rows0"][0], rode["layer1"][1]], rows(weights), rows(mom1), rows(mom2), ROWS_TILE)
    out_vec = _sum_adamw([parts_vec], _pack_vec(weights), _pack_vec(mom1), _pack_vec(mom2), VEC_ROWS)

    outs = []
    for ow, orow, ovec in zip(out_wide, out_rows, out_vec):
        named = _unpack_rows(orow)
        named.update(_unpack_vec(ovec)[0])
        named["w_in"] = ow
        outs += [named[n] for n in weights]
    loss = _unpack_vec(out_vec[0])[1]
    return (loss, grad_x, *outs)
```

```python
import functools

import jax
import jax.numpy as jnp
from jax import lax
from jax.experimental import pallas as pl
from jax.experimental.pallas import tpu as pltpu

F32, BF16 = jnp.float32, jnp.bfloat16
SDS = jax.ShapeDtypeStruct

D = 1024
DEPTH = 2
PLE = 256
BLK = 128
EPS = 1e-6
NEG = -1e30
SWA_H, SWA_KV, SWA_DH = 8, 2, 64
MLA_H, MLA_NOPE, MLA_ROPE, MLA_V = 8, 64, 32, 64
MLA_QK = MLA_NOPE + MLA_ROPE
QL, KVL = 256, 128
IN_W = 4256
N_DEV = 8

V7X_VMEM_BYTES = 64 * 1024 * 1024
LANES = 128
VMEM_LIMIT = V7X_VMEM_BYTES * 7 // 8

ZW = 4352
Z_MA, Z_MB, Z_AQ, Z_AG, Z_BG, Z_QD, Z_AK, Z_AV, Z_KVD, Z_KR = 0, 1024, 2048, 2560, 3072, 3584, 3840, 3968, 4096, 4224
QFW = MLA_H * LANES
KVW = QFW + MLA_H * MLA_V
MLA_SCALE = MLA_QK ** -0.5
LOG2E = 1.4426950408889634
MLA_FWD_HEADS, MLA_BWD_HEADS = 8, 8
SWA_SCALE = SWA_DH ** -0.5
ROLL_UP, ROLL_DOWN = MLA_ROPE // 2, LANES - MLA_ROPE // 2

ADAM_LR, ADAM_B1, ADAM_B2, ADAM_EPS, ADAM_WD, ADAM_STEP = 0.001, 0.9, 0.999, 1e-08, 0.01, 10


def _cp(*sem):
    return pltpu.CompilerParams(dimension_semantics=sem, vmem_limit_bytes=VMEM_LIMIT)


def _row(tm, w, col=0):
    return pl.BlockSpec((tm, w), lambda i: (i, col))


def _res(shape, layer=None):
    if layer is None:
        return pl.BlockSpec(shape, lambda *_: (0,) * len(shape), pipeline_mode=pl.Buffered(1))
    return pl.BlockSpec((None,) + shape, lambda *_: (layer,) + (0,) * len(shape), pipeline_mode=pl.Buffered(1))


def _acc(shape):
    return pl.BlockSpec(shape, lambda *_: (0,) * len(shape))


def _rstd(xf):
    return lax.rsqrt(jnp.mean(xf * xf, axis=-1, keepdims=True) + EPS)


def _norm_bwd(dh, n, r, g):
    dn = dh * g
    return r * (dn - n * jnp.mean(dn * n, axis=-1, keepdims=True)), dh * n


def _nt(a, b):
    return lax.dot_general(a, b, (((1,), (1,)), ((), ())), preferred_element_type=F32)


def _tn(a, b):
    return lax.dot_general(a, b, (((0,), (0,)), ((), ())), preferred_element_type=F32)


def _nn(a, b):
    return jnp.dot(a, b, preferred_element_type=F32)


def _sig(x):
    return jax.nn.sigmoid(x)


def _rope(t, c, s1, s2):
    return t * c + pltpu.roll(t, ROLL_UP, 1) * s1 + pltpu.roll(t, ROLL_DOWN, 1) * s2


def _rope_t(d, c, s1, s2):
    return d * c + pltpu.roll(d * s1, ROLL_DOWN, 1) + pltpu.roll(d * s2, ROLL_UP, 1)


def _fwd_in(x, g, w, tm, layer, ride=None):
    T = x.shape[0]
    grid = (T // tm,)

    def body(x_ref, g_ref, w_ref, z_ref, h_ref):
        xf = x_ref[...]
        h = ((xf * _rstd(xf)) * g_ref[...]).astype(BF16)
        h_ref[...] = h
        z_ref[...] = _nn(h, w_ref[...])

    r_in, r_out, r_shape, r_scratch, r_args = _ride_args(ride)
    return pl.pallas_call(
        _riding(ride, body, 3, 2, grid), grid=grid, name="fwd_in_ride" if ride else "fwd_in",
        in_specs=[_row(tm, D), _res((1, D), layer), _res((D, ZW), 0)] + r_in,
        out_specs=[_row(tm, ZW), _row(tm, D)] + r_out,
        out_shape=[SDS((T, ZW), F32), SDS((T, D), BF16)] + r_shape, scratch_shapes=r_scratch,
        compiler_params=_cp("arbitrary"))(x, g, w, *r_args)


def _fwd_prep(z, gq, gkv, wq, wkv, tc, ts1, ts2, tm, layer):
    T = z.shape[0]

    def body(qd_ref, kvd_ref, kr_ref, gq_ref, gkv_ref, wq_ref, wkv_ref, c_ref, s1_ref, s2_ref, q_ref, k_ref, v_ref):
        qd, kvd = qd_ref[...], kvd_ref[...]
        hq = ((qd * _rstd(qd)) * gq_ref[...]).astype(BF16)
        hkv = ((kvd * _rstd(kvd)) * gkv_ref[...]).astype(BF16)
        qf = _nn(hq, wq_ref[...])
        kvf = _nn(hkv, wkv_ref[...])
        c, s1, s2 = c_ref[...], s1_ref[...], s2_ref[...]
        krb = _rope(kr_ref[...], c, s1, s2)
        for h in range(MLA_H):
            sl = slice(LANES * h, LANES * (h + 1))
            q_ref[:, sl] = _rope(qf[:, sl], c, s1, s2).astype(BF16)
            k_ref[:, sl] = (kvf[:, sl] + krb).astype(BF16)
        v_ref[...] = kvf[:, QFW:].astype(BF16)

    return pl.pallas_call(
        body, grid=(T // tm,), name="fwd_prep",
        in_specs=[_row(tm, QL, Z_QD // QL), _row(tm, KVL, Z_KVD // KVL), _row(tm, LANES, Z_KR // LANES),
                  _res((1, QL), layer), _res((1, KVL), layer), _res((QL, QFW), 0), _res((KVL, KVW), 0),
                  _row(tm, LANES), _row(tm, LANES), _row(tm, LANES)],
        out_specs=[_row(tm, QFW), _row(tm, QFW), _row(tm, MLA_H * MLA_V)],
        out_shape=[SDS((T, QFW), BF16), SDS((T, QFW), BF16), SDS((T, MLA_H * MLA_V), BF16)],
        compiler_params=_cp("parallel"))(z, z, z, gq, gkv, wq, wkv, tc, ts1, ts2)


def _grid_ends(grid):
    ids = [pl.program_id(a) for a in range(len(grid))]
    inner_first = functools.reduce(jnp.logical_and, [i == 0 for i in ids[1:]], True)
    last = functools.reduce(jnp.logical_and, [i == g - 1 for i, g in zip(ids, grid)])
    return (ids[0] == 0) & inner_first, (ids[0] == 3 * grid[0] // 4) & inner_first, last


class _Ride:
    def __init__(self, kind, arrays):
        self.kind, self.arrays, self.n = kind, list(arrays), len(arrays)

    def out_shape(self):
        if self.kind == "gather":
            return _gather_out(self.arrays)
        if self.kind == "swap":
            return [SDS((a.shape[0],) + a.shape[2:], a.dtype) for a in self.arrays]
        return [SDS(a.shape, a.dtype) for a in self.arrays]

    def sems(self):
        if self.kind == "gather":
            return _gather_sems(self.n)
        if self.kind == "swap":
            return _swap_sems(self.n)
        return _exchange_sems(self.n)

    def phases(self, in_refs, out_refs, *sems):
        if self.kind == "gather":
            return _gather_phases(in_refs, out_refs, *sems)
        start, finish = (_swap_phases if self.kind == "swap" else _exchange_phases)(in_refs, out_refs, *sems)
        return start, None, finish


def _riding(ride, body, n_in, n_out, grid):
    if ride is None:
        return body
    n, n_sems = ride.n, len(ride.sems())

    def wrapped(*refs):
        ins, r_in = refs[:n_in], refs[n_in:n_in + n]
        outs, r_out = refs[n_in + n:n_in + n + n_out], refs[n_in + n + n_out:n_in + 2 * n + n_out]
        rest = refs[n_in + 2 * n + n_out:]
        scratch, sems = rest[:len(rest) - n_sems], rest[len(rest) - n_sems:]
        start, middle, finish = ride.phases(r_in, r_out, *sems)
        at_first, at_middle, at_last = _grid_ends(grid)
        pl.when(at_first)(start)
        if middle is not None:
            pl.when(at_middle)(middle)
        body(*ins, *outs, *scratch)
        pl.when(at_last)(finish)

    return wrapped


def _ride_args(ride):
    if ride is None:
        return [], [], [], [], []
    return [ANY] * ride.n, [ANY] * ride.n, ride.out_shape(), ride.sems(), ride.arrays


def _mla_fwd(qf, kf, v, nb, seq, tq, gather=()):
    T = qf.shape[0]
    nq = seq // tq
    hp = MLA_FWD_HEADS
    pw = hp * LANES
    pairs = [(qi, ki) for qi in range(nq) for ki in range(qi + 1)]
    qi_tab = jnp.array([qk[0] for qk in pairs], jnp.int32)
    ki_tab = jnp.array([qk[1] for qk in pairs], jnp.int32)
    grid = (nb, MLA_H // hp, len(pairs))
    n_g = len(gather)

    def body(qi_ref, ki_ref, q_ref, k_ref, v_ref, *rest):
        x_refs, (o_ref, lse_ref), got_refs = rest[:n_g], rest[n_g:n_g + 2], rest[n_g + 2:2 * n_g + 2]
        (m_s, l_s, acc_s), sems = rest[2 * n_g + 2:2 * n_g + 5], rest[2 * n_g + 5:]
        qi, ki = qi_ref[pl.program_id(2)], ki_ref[pl.program_id(2)]
        if n_g:
            start, forward, finish = _gather_phases(x_refs, got_refs, *sems)
            at_first, at_middle, at_last = _grid_ends(grid)
            pl.when(at_first)(start)
            pl.when(at_middle)(forward)

        @pl.when(ki == 0)
        def _():
            m_s[...] = jnp.full(m_s.shape, NEG, F32)
            l_s[...] = jnp.zeros(l_s.shape, F32)
            acc_s[...] = jnp.zeros(acc_s.shape, F32)

        def step(masked):
            parts = [(0, tq // 2, tq // 2), (tq // 2, tq, tq)] if masked else [(0, tq, tq)]
            work = [(j, a, b, kh) for j in range(hp) for a, b, kh in parts]
            ss = []
            for j, a, b, kh in work:
                wide = slice(LANES * j, LANES * (j + 1))
                s = _nt(k_ref[:kh, wide], q_ref[a:b, wide]) * (MLA_SCALE * LOG2E)
                if masked:
                    keys = lax.broadcasted_iota(jnp.int32, (kh, b - a), 0)
                    queries = a + lax.broadcasted_iota(jnp.int32, (kh, b - a), 1)
                    s = jnp.where(keys <= queries, s, NEG)
                ss.append(s)
            ps, alphas = [], []
            for (j, a, b, kh), s in zip(work, ss):
                m_prev = m_s[j, :, a:b]
                m_new = jnp.maximum(m_prev, jnp.max(s, axis=0, keepdims=True))
                alpha = jnp.exp2(m_prev - m_new)
                p = jnp.exp2(s - m_new)
                l_s[j, :, a:b] = alpha * l_s[j, :, a:b] + jnp.sum(p, axis=0, keepdims=True)
                m_s[j, :, a:b] = m_new
                ps.append(p.astype(BF16))
                alphas.append(alpha)
            for (j, a, b, kh), p, alpha in zip(work, ps, alphas):
                rows = slice(MLA_V * j, MLA_V * (j + 1))
                acc_s[rows, a:b] = alpha * acc_s[rows, a:b] + _tn(v_ref[:kh, rows], p)

        @pl.when(ki < qi)
        def _():
            step(False)

        @pl.when(ki == qi)
        def _():
            step(True)
            for j in range(hp):
                rows = slice(MLA_V * j, MLA_V * (j + 1))
                acc_s[rows, :] = acc_s[rows, :] / l_s[j]
                lse_ref[j:j + 1, :] = m_s[j] + jnp.log2(l_s[j])
            o_ref[...] = acc_s[...].T

        if n_g:
            pl.when(at_last)(finish)

    q_map = lambda b, g, s, qi_ref, ki_ref: (b * nq + qi_ref[s], g)
    kv_map = lambda b, g, s, qi_ref, ki_ref: (b * nq + ki_ref[s], g)
    return pl.pallas_call(
        body, name="mla_fwd_gather" if n_g else "mla_fwd",
        grid_spec=pltpu.PrefetchScalarGridSpec(
            num_scalar_prefetch=2, grid=grid,
            in_specs=[pl.BlockSpec((tq, pw), q_map), pl.BlockSpec((tq, pw), kv_map),
                      pl.BlockSpec((tq, hp * MLA_V), kv_map)] + [ANY] * n_g,
            out_specs=[pl.BlockSpec((tq, hp * MLA_V), q_map),
                       pl.BlockSpec((hp, tq), lambda b, g, s, qi_ref, ki_ref: (g, b * nq + qi_ref[s]))]
            + [ANY] * n_g,
            scratch_shapes=[pltpu.VMEM((hp, 1, tq), F32), pltpu.VMEM((hp, 1, tq), F32),
                            pltpu.VMEM((hp * MLA_V, tq), F32)]
            + (_gather_sems(n_g) if n_g else [])),
        out_shape=[SDS((T, MLA_H * MLA_V), F32), SDS((MLA_H, T), F32)] + _gather_out(gather),
        compiler_params=_cp("arbitrary", "arbitrary", "arbitrary"))(qi_tab, ki_tab, qf, kf, v, *gather)


SWA_BLOCKS = 8


def _swa_blocks(seq):
    return min(SWA_BLOCKS, seq // BLK)


def _swa_specs(nstep, nu):
    step = nu * BLK
    cur = lambda b, m: (b * nstep + m, 0)
    prev = lambda b, m: (nu * b * nstep + jnp.maximum(nu * m - 1, 0), 0)
    kvc = Z_AK // (2 * BLK)
    return [pl.BlockSpec(memory_space=pltpu.SMEM),
            pl.BlockSpec((step, 512), lambda b, m: (b * nstep + m, Z_AQ // 512)),
            pl.BlockSpec((step, 2 * BLK), lambda b, m: (b * nstep + m, kvc)),
            pl.BlockSpec((BLK, 2 * BLK), lambda b, m: (nu * b * nstep + jnp.maximum(nu * m - 1, 0), kvc)),
            pl.BlockSpec((step, 1), cur),
            pl.BlockSpec((BLK, 1), prev),
            pl.BlockSpec((1, 1, step), lambda b, m: (b * nstep + m, 0, 0))]


def _swa_scores(m, nu, q_ref, kvc_ref, kvp_ref, pcc_ref, pcp_ref, pr_ref):
    kv = jnp.concatenate([kvp_ref[...], kvc_ref[...]], axis=0)
    kb, vb = kv[:, :BLK].astype(BF16), kv[:, BLK:].astype(BF16)
    pos_keys = jnp.concatenate([pcp_ref[...], pcc_ref[...]], axis=0)
    key = lax.broadcasted_iota(jnp.int32, (2 * BLK, BLK), 0)
    qry = lax.broadcasted_iota(jnp.int32, (2 * BLK, BLK), 1)
    in_window = (key > qry) & (key <= qry + BLK)
    valid = [in_window & ((key >= BLK) | (m > 0))] + [in_window] * (nu - 1)
    dist = [pr_ref[0][:, BLK * u:BLK * (u + 1)] - pos_keys[BLK * u:BLK * (u + 2)] for u in range(nu)]

    def band(t, u, g):
        return t[BLK * u:BLK * (u + 2), SWA_DH * g:SWA_DH * (g + 1)]

    def scores(u, h):
        g = h // (SWA_H // SWA_KV)
        qh = q_ref[BLK * u:BLK * (u + 1), SWA_DH * h:SWA_DH * (h + 1)].astype(BF16)
        s = _nt(band(kb, u, g), qh) * (SWA_SCALE * LOG2E) - (2.0 ** -(h + 1) * LOG2E) * dist[u]
        return qh, jnp.where(valid[u], s, NEG)

    return kb, vb, band, scores


def _swa_fwd(sink, z, pos_col, pos_row, nb, seq, layer):
    T = z.shape[0]
    nu = _swa_blocks(seq)
    step = nu * BLK
    nstep = seq // step
    chains = [(u, h) for u in range(nu) for h in range(SWA_H)]

    def body(sink_ref, q_ref, kvc_ref, kvp_ref, pcc_ref, pcp_ref, pr_ref, o_ref, lse_ref):
        kb, vb, band, scores = _swa_scores(pl.program_id(1), nu, q_ref, kvc_ref, kvp_ref, pcc_ref, pcp_ref, pr_ref)
        ss = [scores(u, h)[1] for u, h in chains]
        es, dens = [], []
        for (u, h), s in zip(chains, ss):
            sk = sink_ref[layer, h] * LOG2E
            m = jnp.maximum(jnp.max(s, axis=0, keepdims=True), sk)
            e = jnp.exp2(s - m)
            den = jnp.sum(e, axis=0, keepdims=True) + jnp.exp2(sk - m)
            lse_ref[h:h + 1, BLK * u:BLK * (u + 1)] = m + jnp.log2(den)
            es.append(e.astype(BF16))
            dens.append(den)
        outs = [_tn(band(vb, u, h // (SWA_H // SWA_KV)), e) / den for (u, h), e, den in zip(chains, es, dens)]
        for u in range(nu):
            o_ref[BLK * u:BLK * (u + 1), :] = jnp.concatenate(outs[SWA_H * u:SWA_H * (u + 1)], axis=0).T

    return pl.pallas_call(
        body, grid=(nb, nstep), name="swa_fwd",
        in_specs=_swa_specs(nstep, nu),
        out_specs=[pl.BlockSpec((step, 512), lambda b, m: (b * nstep + m, 0)),
                   pl.BlockSpec((SWA_H, step), lambda b, m: (0, b * nstep + m))],
        out_shape=[SDS((T, 512), F32), SDS((SWA_H, T), F32)],
        compiler_params=_cp("parallel", "parallel"))(sink, z, z, z, pos_col, pos_col, pos_row)


def _fwd_merge(x, oa, ob, z, wa, wb, wo, tm, layer):
    T = x.shape[0]

    def body(x_ref, oa_ref, ob_ref, ag_ref, bg_ref, ma_ref, mb_ref, wa_ref, wb_ref, wo_ref, x1_ref):
        ag, bg = ag_ref[...], bg_ref[...]
        ua = _nn((oa_ref[...] * (ag * _sig(ag))).astype(BF16), wa_ref[...])
        ub = _nn((ob_ref[...] * (bg * _sig(bg))).astype(BF16), wb_ref[...])
        y = _sig(ma_ref[...]) * ua + _sig(mb_ref[...]) * ub
        x1_ref[...] = x_ref[...] + _nn(y.astype(BF16), wo_ref[...])

    return pl.pallas_call(
        body, grid=(T // tm,), name="fwd_merge",
        in_specs=[_row(tm, D), _row(tm, 512), _row(tm, 512), _row(tm, 512, Z_AG // 512), _row(tm, 512, Z_BG // 512),
                  _row(tm, D, Z_MA // D), _row(tm, D, Z_MB // D),
                  _res((512, D), 0), _res((512, D), 0), _res((D, D), 0)],
        out_specs=_row(tm, D),
        out_shape=SDS((T, D), F32),
        compiler_params=_cp("parallel"))(x, oa, ob, z, z, z, z, wa, wb, wo)


def _fwd_ple(x1, p, g, wpg, wpp, tm, layer):
    T = x1.shape[0]

    def body(x_ref, p_ref, g_ref, wpg_ref, wpp_ref, x2_ref, pg_ref, pp_ref):
        xf = x_ref[...]
        h1 = ((xf * _rstd(xf)) * g_ref[...]).astype(BF16)
        pg = _sig(_nn(h1, wpg_ref[...]))
        pp = _nn(p_ref[...].astype(BF16), wpp_ref[...])
        pg_ref[...] = pg
        pp_ref[...] = pp
        x2_ref[...] = xf + pg * pp

    return pl.pallas_call(
        body, grid=(T // tm,), name="fwd_ple",
        in_specs=[_row(tm, D), pl.BlockSpec((None, tm, PLE), lambda i: (layer, i, 0)),
                  _res((1, D), layer), _res((D, D), 0), _res((PLE, D), 0)],
        out_specs=[_row(tm, D)] * 3,
        out_shape=[SDS((T, D), F32)] * 3,
        compiler_params=_cp("parallel"))(x1, p, g, wpg, wpp)


def _ple_loss(x1, p, g, wpg, wpp, g_final, tgt, tm, layer):
    T = x1.shape[0]

    def body(x_ref, p_ref, g_ref, wpg_ref, wpp_ref, gf_ref, t_ref, dx_ref, dwg_ref, dwp_ref, dg_ref, dgf_ref, loss_ref):
        @pl.when(pl.program_id(0) == 0)
        def _():
            for ref in (dwg_ref, dwp_ref, dg_ref, dgf_ref, loss_ref):
                ref[...] = jnp.zeros(ref.shape, F32)

        xf, gp, gf = x_ref[...], g_ref[...], gf_ref[...]
        r = _rstd(xf)
        n = xf * r
        h1 = (n * gp).astype(BF16)
        pb = p_ref[...].astype(BF16)
        pg = _sig(_nn(h1, wpg_ref[...]))
        pp = _nn(pb, wpp_ref[...])
        x2 = xf + pg * pp
        r2 = _rstd(x2)
        n2 = x2 * r2
        err = n2 * gf - t_ref[...]
        loss_ref[...] += 0.5 * jnp.sum(jnp.mean(err * err, axis=-1, keepdims=True), axis=0, keepdims=True)
        d, dgfr = _norm_bwd(err * (1.0 / D), n2, r2, gf)
        dgf_ref[...] += jnp.sum(dgfr, axis=0, keepdims=True)
        dpgl = (d * pp * pg * (1.0 - pg)).astype(BF16)
        dwg_ref[...] += _tn(h1, dpgl)
        dwp_ref[...] += _tn(pb, (d * pg).astype(BF16))
        dxn, dgr = _norm_bwd(_nt(dpgl, wpg_ref[...]), n, r, gp)
        dx_ref[...] = d + dxn
        dg_ref[...] += jnp.sum(dgr, axis=0, keepdims=True)

    return pl.pallas_call(
        body, grid=(T // tm,), name="ple_loss",
        in_specs=[_row(tm, D), pl.BlockSpec((None, tm, PLE), lambda i: (layer, i, 0)), _res((1, D), layer),
                  _res((D, D), 0), _res((PLE, D), 0), _res((1, D)), _row(tm, D)],
        out_specs=[_row(tm, D), _acc((D, D)), _acc((PLE, D)), _acc((1, D)), _acc((1, D)), _acc((1, LANES))],
        out_shape=[SDS((T, D), F32), SDS((D, D), F32), SDS((PLE, D), F32), SDS((1, D), F32), SDS((1, D), F32),
                   SDS((1, LANES), F32)],
        compiler_params=_cp("arbitrary"))(x1, p, g, wpg, wpp, g_final, tgt)


def _bwd_ple(dx2, x1, pg, pp, p, g, wpg, tm, layer, ride=None):
    T = x1.shape[0]
    grid = (T // tm,)

    def body(d_ref, x_ref, pg_ref, pp_ref, p_ref, g_ref, w_ref, dx_ref, dwg_ref, dwp_ref, dg_ref):
        @pl.when(pl.program_id(0) == 0)
        def _():
            dwg_ref[...] = jnp.zeros(dwg_ref.shape, F32)
            dwp_ref[...] = jnp.zeros(dwp_ref.shape, F32)
            dg_ref[...] = jnp.zeros(dg_ref.shape, F32)

        d, xf, pg, gf = d_ref[...], x_ref[...], pg_ref[...], g_ref[...]
        r = _rstd(xf)
        n = xf * r
        dpgl = (d * pp_ref[...] * pg * (1.0 - pg)).astype(BF16)
        dwg_ref[...] += _tn((n * gf).astype(BF16), dpgl)
        dwp_ref[...] += _tn(p_ref[...].astype(BF16), (d * pg).astype(BF16))
        dxn, dgr = _norm_bwd(_nt(dpgl, w_ref[...]), n, r, gf)
        dx_ref[...] = d + dxn
        dg_ref[...] += jnp.sum(dgr, axis=0, keepdims=True)

    r_in, r_out, r_shape, r_scratch, r_args = _ride_args(ride)
    return pl.pallas_call(
        _riding(ride, body, 7, 4, grid), grid=grid, name="bwd_ple_ride" if ride else "bwd_ple",
        in_specs=[_row(tm, D)] * 4 + [pl.BlockSpec((None, tm, PLE), lambda i: (layer, i, 0)),
                                      _res((1, D), layer), _res((D, D), 0)] + r_in,
        out_specs=[_row(tm, D), _acc((D, D)), _acc((PLE, D)), _acc((1, D))] + r_out,
        out_shape=[SDS((T, D), F32), SDS((D, D), F32), SDS((PLE, D), F32), SDS((1, D), F32)] + r_shape,
        scratch_shapes=r_scratch,
        compiler_params=_cp("arbitrary"))(dx2, x1, pg, pp, p, g, wpg, *r_args)


def _bwd_merge(dx1, oa, ob, z, wa, wb, wo, tm, layer):
    T = dx1.shape[0]

    def body(d_ref, oa_ref, ob_ref, ag_ref, bg_ref, ma_ref, mb_ref, wa_ref, wb_ref, wo_ref,
             doa_ref, dob_ref, dag_ref, dbg_ref, dma_ref, dmb_ref, dsa_ref, dsb_ref, dwa_ref, dwb_ref, dwo_ref):
        @pl.when(pl.program_id(0) == 0)
        def _():
            dwa_ref[...] = jnp.zeros(dwa_ref.shape, F32)
            dwb_ref[...] = jnp.zeros(dwb_ref.shape, F32)
            dwo_ref[...] = jnp.zeros(dwo_ref.shape, F32)

        db = d_ref[...].astype(BF16)
        gated = []
        for o_ref, gate_ref, w_ref in ((oa_ref, ag_ref, wa_ref), (ob_ref, bg_ref, wb_ref)):
            raw, gate = o_ref[...], gate_ref[...]
            sg = _sig(gate)
            silu = gate * sg
            ob16 = (raw * silu).astype(BF16)
            gated.append((raw, gate, sg, silu, ob16, _nn(ob16, w_ref[...])))
        ua, ub = gated[0][5], gated[1][5]
        sa, sb = _sig(ma_ref[...]), _sig(mb_ref[...])
        dwo_ref[...] += _tn((sa * ua + sb * ub).astype(BF16), db)
        dy = _nt(db, wo_ref[...])
        dma_ref[...] = (dy * ua * sa * (1.0 - sa)).astype(BF16)
        dmb_ref[...] = (dy * ub * sb * (1.0 - sb)).astype(BF16)
        for (s, w_ref, do_ref, dgate_ref, dw_ref, ds_ref), (raw, gate, sg, silu, ob16, _) in zip((
                (sa, wa_ref, doa_ref, dag_ref, dwa_ref, dsa_ref),
                (sb, wb_ref, dob_ref, dbg_ref, dwb_ref, dsb_ref)), gated):
            du = (dy * s).astype(BF16)
            dw_ref[...] += _tn(ob16, du)
            do = _nt(du, w_ref[...])
            draw = do * silu
            do_ref[...] = draw.astype(BF16)
            dgate_ref[...] = (do * raw * (sg * (1.0 + gate * (1.0 - sg)))).astype(BF16)
            ds_ref[...] = jnp.sum((draw * raw).T.reshape(MLA_H, MLA_V, tm), axis=1)

    return pl.pallas_call(
        body, grid=(T // tm,), name="bwd_merge",
        in_specs=[_row(tm, D), _row(tm, 512), _row(tm, 512), _row(tm, 512, Z_AG // 512), _row(tm, 512, Z_BG // 512),
                  _row(tm, D, Z_MA // D), _row(tm, D, Z_MB // D),
                  _res((512, D), 0), _res((512, D), 0), _res((D, D), 0)],
        out_specs=[_row(tm, 512)] * 4 + [_row(tm, D)] * 2 + [pl.BlockSpec((MLA_H, tm), lambda i: (0, i))] * 2
        + [_acc((512, D)), _acc((512, D)), _acc((D, D))],
        out_shape=[SDS((T, 512), BF16), SDS((T, 512), BF16), SDS((T, 512), BF16), SDS((T, 512), BF16),
                   SDS((T, D), BF16), SDS((T, D), BF16), SDS((MLA_H, T), F32), SDS((MLA_H, T), F32),
                   SDS((512, D), F32), SDS((512, D), F32), SDS((D, D), F32)],
        compiler_params=_cp("arbitrary"))(dx1, oa, ob, z, z, z, z, wa, wb, wo)


def _mla_bwd(qf, kf, v, do, lse, dsum, nb, seq, tq, exchange=()):
    T = qf.shape[0]
    nq = seq // tq
    hp = MLA_BWD_HEADS
    pw = hp * LANES
    pairs = [(qi, ki) for ki in range(nq) for qi in range(ki, nq)]
    qi_tab = jnp.array([qk[0] for qk in pairs], jnp.int32)
    ki_tab = jnp.array([qk[1] for qk in pairs], jnp.int32)
    grid = (nb, MLA_H // hp, len(pairs))
    n_x = len(exchange)

    def body(qi_ref, ki_ref, q_ref, k_ref, v_ref, do_ref, lse_ref, dsum_ref, *rest):
        p_refs, (dq_ref, dk_ref, dv_ref), got_refs = rest[:n_x], rest[n_x:n_x + 3], rest[n_x + 3:2 * n_x + 3]
        (dk_s, dv_s, dqt_s), sems = rest[2 * n_x + 3:2 * n_x + 6], rest[2 * n_x + 6:]
        step_id = pl.program_id(2)
        qi, ki = qi_ref[step_id], ki_ref[step_id]
        if n_x:
            start, finish = _exchange_phases(p_refs, got_refs, *sems)
            at_first, _, at_last = _grid_ends(grid)
            pl.when(at_first)(start)

        @pl.when(step_id == 0)
        def _():
            dqt_s[...] = jnp.zeros(dqt_s.shape, F32)

        @pl.when(qi == ki)
        def _():
            dk_s[...] = jnp.zeros(dk_s.shape, F32)
            dv_s[...] = jnp.zeros(dv_s.shape, F32)

        def step(masked):
            if masked:
                keys = lax.broadcasted_iota(jnp.int32, (tq, tq), 0)
                queries = lax.broadcasted_iota(jnp.int32, (tq, tq), 1)
                mask = keys <= queries
            for j in range(hp):
                wide = slice(LANES * j, LANES * (j + 1))
                sl = slice(MLA_V * j, MLA_V * (j + 1))
                q, k = q_ref[:, wide], k_ref[:, wide]
                dob = do_ref[:, sl].astype(BF16)
                s = _nt(k, q) * (MLA_SCALE * LOG2E)
                if masked:
                    s = jnp.where(mask, s, NEG)
                p = jnp.exp2(s - lse_ref[j:j + 1, :])
                dv_s[:, sl] += _nn(p.astype(BF16), dob)
                ds = (p * (_nt(v_ref[:, sl], dob) - dsum_ref[j:j + 1, :]) * MLA_SCALE).astype(BF16)
                dk_s[:, wide] += _nn(ds, q)
                dqt_s[qi, wide, :] += _tn(k, ds)

        @pl.when(qi > ki)
        def _():
            step(False)

        @pl.when(qi == ki)
        def _():
            step(True)

        @pl.when(qi == nq - 1)
        def _():
            dk_ref[...] = dk_s[...]
            dv_ref[...] = dv_s[...]

        @pl.when(step_id == len(pairs) - 1)
        def _():
            for n in range(nq):
                dq_ref[tq * n:tq * (n + 1), :] = dqt_s[n].T

        if n_x:
            pl.when(at_last)(finish)

    qmap = lambda b, g, s, qi_ref, ki_ref: (b * nq + qi_ref[s], g)
    kmap = lambda b, g, s, qi_ref, ki_ref: (b * nq + ki_ref[s], g)
    stat = pl.BlockSpec((None, hp, tq), lambda b, g, s, qi_ref, ki_ref: (g, 0, b * nq + qi_ref[s]))
    vw = hp * MLA_V
    return pl.pallas_call(
        body, name="mla_bwd_exchange" if n_x else "mla_bwd",
        grid_spec=pltpu.PrefetchScalarGridSpec(
            num_scalar_prefetch=2, grid=grid,
            in_specs=[pl.BlockSpec((tq, pw), qmap), pl.BlockSpec((tq, pw), kmap), pl.BlockSpec((tq, vw), kmap),
                      pl.BlockSpec((tq, vw), qmap), stat, stat] + [ANY] * n_x,
            out_specs=[pl.BlockSpec((seq, pw), lambda b, g, s, qi_ref, ki_ref: (b, g)),
                       pl.BlockSpec((tq, pw), kmap), pl.BlockSpec((tq, vw), kmap)] + [ANY] * n_x,
            scratch_shapes=[pltpu.VMEM((tq, pw), F32), pltpu.VMEM((tq, vw), F32), pltpu.VMEM((nq, pw, tq), F32)]
            + (_exchange_sems(n_x) if n_x else [])),
        out_shape=[SDS((T, QFW), F32), SDS((T, QFW), F32), SDS((T, MLA_H * MLA_V), F32)]
        + [SDS(a.shape, a.dtype) for a in exchange],
        compiler_params=_cp("arbitrary", "arbitrary", "arbitrary"))(qi_tab, ki_tab, qf, kf, v, do, lse, dsum, *exchange)


def _swa_bwd(sink, z, pos_col, pos_row, do, lse, dsum, nb, seq, layer, ride=None):
    T = z.shape[0]
    nu = _swa_blocks(seq)
    step = nu * BLK
    nstep = seq // step
    chains = [(u, h) for u in range(nu) for h in range(SWA_H)]

    def body(sink_ref, q_ref, kvc_ref, kvp_ref, pcc_ref, pcp_ref, pr_ref, do_ref, lse_ref, dsum_ref,
             dq_ref, dkv_ref, dsink_ref):
        b, m = pl.program_id(0), pl.program_id(1)

        @pl.when((b == 0) & (m == 0))
        def _():
            dsink_ref[...] = jnp.zeros(dsink_ref.shape, F32)

        @pl.when(m == 0)
        def _():
            dkv_ref[...] = jnp.zeros(dkv_ref.shape, F32)

        kb, vb, band, scores = _swa_scores(m, nu, q_ref, kvc_ref, kvp_ref, pcc_ref, pcp_ref, pr_ref)
        lane = lax.broadcasted_iota(jnp.int32, (1, LANES), 1)
        dsink = jnp.zeros((1, LANES), F32)
        group = lambda h: h // (SWA_H // SWA_KV)
        qs, ss, dobs, dps = [], [], [], []
        for u, h in chains:
            qh, s = scores(u, h)
            dob = do_ref[BLK * u:BLK * (u + 1), SWA_DH * h:SWA_DH * (h + 1)].astype(BF16)
            qs.append(qh)
            ss.append(s)
            dobs.append(dob)
            dps.append(_nt(band(vb, u, group(h)), dob))
        pbs, dss = [], []
        for (u, h), s, dp in zip(chains, ss, dps):
            cols = slice(BLK * u, BLK * (u + 1))
            lse, dsum = lse_ref[h:h + 1, cols], dsum_ref[h:h + 1, cols]
            p = jnp.exp2(s - lse)
            pbs.append(p.astype(BF16))
            dss.append((p * (dp - dsum) * SWA_SCALE).astype(BF16))
            dsk = jnp.sum(-jnp.exp2(sink_ref[layer, h] * LOG2E - lse) * dsum, axis=1, keepdims=True)
            dsink = dsink + jnp.where(lane == h, dsk, 0.0)
        dqs, dkv = [], [[[None, None], [None, None]] for _ in range(nu)]
        for i, (u, h) in enumerate(chains):
            g = group(h)
            dqs.append(_tn(band(kb, u, g), dss[i]))
            dk, dv = _nn(dss[i], qs[i]), _nn(pbs[i], dobs[i])
            dkv[u][g][0] = dk if dkv[u][g][0] is None else dkv[u][g][0] + dk
            dkv[u][g][1] = dv if dkv[u][g][1] is None else dkv[u][g][1] + dv
        for u in range(nu):
            dq_ref[BLK * u:BLK * (u + 1), :] = jnp.concatenate(dqs[SWA_H * u:SWA_H * (u + 1)], axis=0).T.astype(BF16)
        dsink_ref[...] += dsink
        upd = [jnp.concatenate([dkv[u][0][0], dkv[u][1][0], dkv[u][0][1], dkv[u][1][1]], axis=1) for u in range(nu)]
        base = pl.multiple_of(m * step, step)
        for u in range(nu):
            own = upd[u][BLK:] + upd[u + 1][:BLK] if u + 1 < nu else upd[u][BLK:]
            dkv_ref[pl.ds(base + BLK * u, BLK), :] += own

        @pl.when(m > 0)
        def _():
            dkv_ref[pl.ds(pl.multiple_of(m * step - BLK, BLK), BLK), :] += upd[0][:BLK]

    r_in, r_out, r_shape, r_scratch, r_args = _ride_args(ride)
    return pl.pallas_call(
        _riding(ride, body, 10, 3, (nb, nstep)), grid=(nb, nstep), name="swa_bwd_ride" if ride else "swa_bwd",
        in_specs=_swa_specs(nstep, nu) + [pl.BlockSpec((step, 512), lambda b, m: (b * nstep + m, 0))]
        + [pl.BlockSpec((SWA_H, step), lambda b, m: (0, b * nstep + m))] * 2 + r_in,
        out_specs=[pl.BlockSpec((step, 512), lambda b, m: (b * nstep + m, 0)),
                   pl.BlockSpec((seq, 2 * BLK), lambda b, m: (b, 0)),
                   pl.BlockSpec((1, LANES), lambda b, m: (0, 0))] + r_out,
        out_shape=[SDS((T, 512), BF16), SDS((T, 2 * BLK), F32), SDS((1, LANES), F32)] + r_shape,
        scratch_shapes=r_scratch,
        compiler_params=_cp("arbitrary", "arbitrary"))(sink, z, z, z, pos_col, pos_col, pos_row, do, lse, dsum,
                                                       *r_args)


def _bwd_prep(dq, dk, dv, z, gq, gkv, wq, wkv, tc, ts1, ts2, tm, layer):
    T = z.shape[0]

    def body(dq_ref, dk_ref, dv_ref, qd_ref, kvd_ref, gq_ref, gkv_ref, wq_ref, wkv_ref, c_ref, s1_ref, s2_ref,
             dqd_ref, dkvd_ref, dkr_ref, dwq_ref, dwkv_ref, dgq_ref, dgkv_ref, dqb_s, dkvb_s):
        @pl.when(pl.program_id(0) == 0)
        def _():
            for ref in (dwq_ref, dwkv_ref, dgq_ref, dgkv_ref):
                ref[...] = jnp.zeros(ref.shape, F32)

        c, s1, s2 = c_ref[...], s1_ref[...], s2_ref[...]
        lane = lax.broadcasted_iota(jnp.int32, (1, LANES), 1)
        rope_lanes = (lane >= MLA_NOPE) & (lane < MLA_QK)
        dkb = jnp.zeros((tm, LANES), F32)
        for h in range(MLA_H):
            sl = slice(LANES * h, LANES * (h + 1))
            dqb_s[:, sl] = _rope_t(dq_ref[:, sl], c, s1, s2).astype(BF16)
            dkh = dk_ref[:, sl]
            dkb = dkb + dkh
            dkvb_s[:, sl] = dkh.astype(BF16)
        dkvb_s[:, QFW:] = dv_ref[...].astype(BF16)
        dkr_ref[...] = _rope_t(jnp.where(rope_lanes, dkb, 0.0), c, s1, s2).astype(BF16)

        for (x_ref, g_ref, w_ref, d_s, dx_ref, dw_ref, dg_ref) in (
                (qd_ref, gq_ref, wq_ref, dqb_s, dqd_ref, dwq_ref, dgq_ref),
                (kvd_ref, gkv_ref, wkv_ref, dkvb_s, dkvd_ref, dwkv_ref, dgkv_ref)):
            xf, gf, db = x_ref[...], g_ref[...], d_s[...]
            r = _rstd(xf)
            n = xf * r
            dw_ref[...] += _tn((n * gf).astype(BF16), db)
            dx, dgr = _norm_bwd(_nt(db, w_ref[...]), n, r, gf)
            dx_ref[...] = dx.astype(BF16)
            dg_ref[...] += jnp.sum(dgr, axis=0, keepdims=True)

    return pl.pallas_call(
        body, grid=(T // tm,), name="bwd_prep",
        in_specs=[_row(tm, QFW), _row(tm, QFW), _row(tm, MLA_H * MLA_V),
                  _row(tm, QL, Z_QD // QL), _row(tm, KVL, Z_KVD // KVL),
                  _res((1, QL), layer), _res((1, KVL), layer), _res((QL, QFW), 0), _res((KVL, KVW), 0),
                  _row(tm, LANES), _row(tm, LANES), _row(tm, LANES)],
        out_specs=[_row(tm, QL), _row(tm, KVL), _row(tm, LANES),
                   _acc((QL, QFW)), _acc((KVL, KVW)), _acc((1, QL)), _acc((1, KVL))],
        out_shape=[SDS((T, QL), BF16), SDS((T, KVL), BF16), SDS((T, LANES), BF16),
                   SDS((QL, QFW), F32), SDS((KVL, KVW), F32), SDS((1, QL), F32), SDS((1, KVL), F32)],
        scratch_shapes=[pltpu.VMEM((tm, QFW), BF16), pltpu.VMEM((tm, KVW), BF16)],
        compiler_params=_cp("arbitrary"))(dq, dk, dv, z, z, gq, gkv, wq, wkv, tc, ts1, ts2)


def _bwd_in(pieces, x, g, dres, w, tm, layer):
    T = x.shape[0]
    grid = (T // tm,)
    widths = [pc.shape[1] for pc in pieces]
    assert sum(widths) == ZW
    n_p = len(pieces)

    def body(*refs):
        p_refs, (x_ref, g_ref, r_ref, w_ref, dx_ref, dz_ref, dg_ref) = refs[:n_p], refs[n_p:]

        @pl.when(pl.program_id(0) == 0)
        def _():
            dg_ref[...] = jnp.zeros(dg_ref.shape, F32)

        off = 0
        for ref, wd in zip(p_refs, widths):
            dz_ref[:, off:off + wd] = ref[...].astype(BF16)
            off += wd
        xf, gf = x_ref[...], g_ref[...]
        r = _rstd(xf)
        n = xf * r
        dx, dgr = _norm_bwd(_nt(dz_ref[...], w_ref[...]), n, r, gf)
        dx_ref[...] = r_ref[...] + dx
        dg_ref[...] += jnp.sum(dgr, axis=0, keepdims=True)

    return pl.pallas_call(
        body, grid=grid, name="bwd_in",
        in_specs=[_row(tm, wd) for wd in widths] + [_row(tm, D), _res((1, D), layer), _row(tm, D),
                                                    _res((D, ZW), 0)],
        out_specs=[_row(tm, D), _row(tm, ZW), _acc((1, D))],
        out_shape=[SDS((T, D), F32), SDS((T, ZW), BF16), SDS((1, D), F32)],
        compiler_params=_cp("arbitrary"))(*pieces, x, g, dres, w)


def _wgrad_in(hb, dzb, tm, ride=None):
    T = hb.shape[0]
    half = ZW // 2
    grid = (2, T // tm)

    def body(h_ref, dz_ref, dw_ref):
        @pl.when(pl.program_id(1) == 0)
        def _():
            dw_ref[...] = jnp.zeros(dw_ref.shape, F32)

        dw_ref[...] += _tn(h_ref[...], dz_ref[...])

    r_in, r_out, r_shape, r_scratch, r_args = _ride_args(ride)
    out = pl.pallas_call(
        _riding(ride, body, 2, 1, grid), grid=grid, name="wgrad_in_ride" if ride else "wgrad_in",
        in_specs=[pl.BlockSpec((tm, D), lambda j, t: (t, 0)), pl.BlockSpec((tm, half), lambda j, t: (t, j))] + r_in,
        out_specs=[pl.BlockSpec((D, half), lambda j, t: (0, j))] + r_out,
        out_shape=[SDS((D, ZW), F32)] + r_shape, scratch_shapes=r_scratch,
        compiler_params=_cp("arbitrary", "arbitrary"))(hb, dzb, *r_args)
    return out


IN_PIECES = ((0, 512, Z_AQ), (512, 128, Z_AK), (640, 128, Z_AV), (768, 512, Z_AG), (1280, 256, Z_QD),
             (1536, 128, Z_KVD), (1664, MLA_ROPE, Z_KR + MLA_NOPE), (1696, 512, Z_BG), (2208, 1024, Z_MA),
             (3232, 1024, Z_MB))
WIDE_W = IN_W // N_DEV


def _column_runs():
    runs = []
    for start, width, kstart in IN_PIECES:
        col = start
        while col < start + width:
            dev = col // WIDE_W
            stop = min(start + width, (dev + 1) * WIDE_W)
            runs.append((dev, col - dev * WIDE_W, stop - col, kstart + col - start))
            col = stop
    return runs


def _win_layout(blocks, tm):
    runs = _column_runs()

    def body(g_ref, o_ref):
        o_ref[:, Z_KR:Z_KR + LANES] = jnp.zeros((tm, LANES), o_ref.dtype)
        for dev, lo, n, k in runs:
            o_ref[:, k:k + n] = g_ref[dev, :, lo:lo + n]

    return pl.pallas_call(
        body, grid=(D // tm,), name="win_layout",
        in_specs=[pl.BlockSpec((N_DEV, None, tm, WIDE_W), lambda i: (0, 0, i, 0))],
        out_specs=pl.BlockSpec((None, tm, ZW), lambda i: (0, i, 0)),
        out_shape=SDS((1, D, ZW), blocks.dtype),
        compiler_params=_cp("parallel"))(blocks)


def _win_grad_layout(dw, tm):
    runs = _column_runs()

    def body(g_ref, o_ref):
        for dev, lo, n, k in runs:
            o_ref[dev, :, lo:lo + n] = g_ref[:, k:k + n]

    return pl.pallas_call(
        body, grid=(D // tm,), name="win_grad_layout",
        in_specs=[_row(tm, ZW)],
        out_specs=pl.BlockSpec((N_DEV, None, tm, WIDE_W), lambda i: (0, 0, i, 0)),
        out_shape=SDS((N_DEV, 1, D, WIDE_W), F32),
        compiler_params=_cp("parallel"))(dw)


def _wuq_to_kernel(w):
    w = w.reshape(w.shape[:-1] + (MLA_H, MLA_QK))
    w = jnp.pad(w, [(0, 0)] * (w.ndim - 1) + [(0, LANES - MLA_QK)])
    return w.reshape(w.shape[:-2] + (QFW,))


def _wuq_from_kernel(g):
    g = g.reshape(g.shape[:-1] + (MLA_H, LANES))[..., :MLA_QK]
    return g.reshape(g.shape[:-2] + (MLA_H * MLA_QK,))


def _wukv_to_kernel(w):
    w = w.reshape(w.shape[:-1] + (MLA_H, MLA_NOPE + MLA_V))
    k = jnp.pad(w[..., :MLA_NOPE], [(0, 0)] * (w.ndim - 1) + [(0, LANES - MLA_NOPE)])
    v = w[..., MLA_NOPE:]
    return jnp.concatenate([k.reshape(k.shape[:-2] + (QFW,)), v.reshape(v.shape[:-2] + (MLA_H * MLA_V,))], axis=-1)


def _wukv_from_kernel(g):
    k = g[..., :QFW].reshape(g.shape[:-1] + (MLA_H, LANES))[..., :MLA_NOPE]
    v = g[..., QFW:].reshape(g.shape[:-1] + (MLA_H, MLA_V))
    kv = jnp.concatenate([k, v], axis=-1)
    return kv.reshape(kv.shape[:-2] + (MLA_H * (MLA_NOPE + MLA_V),))


def _rope_tables(pos):
    half = MLA_ROPE // 2
    inv = 10000.0 ** (-jnp.arange(0, MLA_ROPE, 2, dtype=F32) / MLA_ROPE)
    ang = pos.astype(F32)[:, None] * inv
    cos, sin = jnp.cos(ang), jnp.sin(ang)
    one = jnp.ones((pos.shape[0], MLA_NOPE), F32)
    zero = lambda n: jnp.zeros((pos.shape[0], n), F32)
    tc = jnp.concatenate([one, cos, cos, one[:, :LANES - MLA_QK]], axis=1)
    ts1 = jnp.concatenate([zero(MLA_NOPE + half), sin, zero(LANES - MLA_QK)], axis=1)
    ts2 = jnp.concatenate([zero(MLA_NOPE), -sin, zero(LANES - MLA_NOPE - half)], axis=1)
    return tc, ts1, ts2


def _local_step(x, p, positions, loss_target, small, wts, plan=None):
    nb, seq, _ = x.shape
    T = nb * seq
    tm = min(512, T)
    tl = min(1024, T)
    tq = min(512, seq)
    xf = x.reshape(T, D)
    pos = positions.reshape(T)
    posf = pos.astype(F32)
    swa_step = _swa_blocks(seq) * BLK
    pos_col, pos_row = posf.reshape(T, 1), posf.reshape(T // swa_step, 1, swa_step)
    tc, ts1, ts2 = _rope_tables(pos)

    wts, sm = list(wts), small
    pl_in = p.reshape(DEPTH, T, PLE)
    saved = []
    for i in range(DEPTH):
        riding = plan is not None and i == 0
        w = wts[i]
        z, hb, *got = _fwd_in(xf, sm["g_mix"], w["w_in"], tm, i,
                              ride=_Ride("gather", plan["behind_fwd_in"]) if riding else None)
        if riding:
            w = wts[0] = dict(w, **plan["row_weights"](got[0]))
        oa, lse_a = _swa_fwd(sm["sink"], z, pos_col, pos_row, nb, seq, i)
        qf, kf, v = _fwd_prep(z, sm["g_q"], sm["g_kv"], w["w_uq"], w["w_ukv"], tc, ts1, ts2, tl, i)
        ob, lse_b, *got = _mla_fwd(qf, kf, v, nb, seq, tq, gather=plan["behind_mla_fwd"] if riding else ())
        if riding:
            wts.append(dict(w_in=plan["w_in"](got[0]), **plan["row_weights"](got[1])))
        x1 = _fwd_merge(xf, oa, ob, z, w["w_br_a"], w["w_br_b"], w["w_out"], tm, i)
        saved.append(dict(x=xf, z=z, hb=hb, oa=oa, lse_a=lse_a, qf=qf, kf=kf, v=v, ob=ob, lse_b=lse_b, x1=x1))
        if i < DEPTH - 1:
            xf, saved[i]["pg"], saved[i]["pp"] = _fwd_ple(x1, pl_in, sm["g_ple"], w["w_ple_gate"], w["w_ple_proj"],
                                                          tl, i)

    last = _ple_loss(x1, pl_in, sm["g_ple"], w["w_ple_gate"], w["w_ple_proj"], small["g_final"],
                     loss_target.reshape(T, D), tm, DEPTH - 1)
    dg_final, loss = last[4], last[5]

    grads = [None] * DEPTH
    exchanged = {}
    for i in reversed(range(DEPTH)):
        riding = plan is not None and i == 0
        sv, w = saved[i], wts[i]
        pay = plan["payload"](grads[1]) if riding else []
        if i == DEPTH - 1:
            (dx1, dwpg, dwpp, dg_ple), got = last[:4], []
        else:
            dx1, dwpg, dwpp, dg_ple, *got = _bwd_ple(dx, sv["x1"], sv["pg"], sv["pp"], pl_in, sm["g_ple"],
                                                     w["w_ple_gate"], tm, i,
                                                     ride=_Ride("swap", pay) if riding else None)
        doa, dob, dag, dbg, dma, dmb, dsum_a, dsum_b, dwa, dwb, dwo = _bwd_merge(
            dx1, sv["oa"], sv["ob"], sv["z"], w["w_br_a"], w["w_br_b"], w["w_out"], tm, i)
        stats = (MLA_H // MLA_BWD_HEADS, MLA_BWD_HEADS, T)
        dq_b, dk_b, dv_b, *exchanged["layer1"] = _mla_bwd(
            sv["qf"], sv["kf"], sv["v"], dob, sv["lse_b"].reshape(stats), dsum_b.reshape(stats), nb, seq, tq,
            exchange=plan["add"](pay, got) if riding else ())
        dqd, dkvd, dkr, dwq, dwkv, dgq, dgkv = _bwd_prep(dq_b, dk_b, dv_b, sv["z"], sm["g_q"], sm["g_kv"],
                                                         w["w_uq"], w["w_ukv"], tc, ts1, ts2, tl, i)
        g = dict(w_uq=_wuq_from_kernel(dwq), w_ukv=_wukv_from_kernel(dwkv), w_br_a=dwa, w_br_b=dwb, w_out=dwo,
                 w_ple_gate=dwpg, w_ple_proj=dwpp)
        pay = [plan["rows_payload"](g)] if riding else []
        dq_a, dkv_a, dsink, *got = _swa_bwd(sm["sink"], sv["z"], pos_col, pos_row, doa, sv["lse_a"], dsum_a, nb, seq,
                                            i, ride=_Ride("swap", pay) if riding else None)
        dx, dzb, dg_mix = _bwd_in([dma, dmb, dq_a, dag, dbg, dqd, dkv_a, dkvd, dkr], sv["x"], sm["g_mix"], dx1,
                                  w["w_in"], tm, i)
        dwin, *exchanged["rows0"] = _wgrad_in(sv["hb"], dzb, tm,
                                              ride=_Ride("exchange", plan["add"](pay, got)) if riding else None)
        g.update(g_mix=dg_mix[0], w_in=dwin, sink=dsink[0, :SWA_H], g_q=dgq[0], g_kv=dgkv[0], g_ple=dg_ple[0])
        grads[i] = g
    return loss, dx.reshape(nb, seq, D), grads, dg_final[0], exchanged


def _row_weights(rows):
    blocks = _unpack_rows(rows)
    out = {n: _join(n, blocks[n]) for n, _ in ROWS_PIECES}
    out.update(w_uq=_wuq_to_kernel(out["w_uq"]), w_ukv=_wukv_to_kernel(out["w_ukv"]))
    return out


def _small_params(g_mix, sink, g_q, g_kv, g_ple, g_final):
    return dict(g_mix=g_mix[:, None], sink=sink, g_q=g_q[:, None], g_kv=g_kv[:, None], g_ple=g_ple[:, None],
                g_final=g_final[None])


UQ_W = MLA_H * MLA_QK // N_DEV
ROWS_PIECES = (("w_uq", QL), ("w_ukv", KVL), ("w_br_a", 512), ("w_br_b", 512), ("w_out", D), ("w_ple_gate", D),
               ("w_ple_proj", PLE))
SMALL = (("g_mix", (DEPTH, D)), ("sink", (DEPTH, SWA_H)), ("g_q", (DEPTH, QL)), ("g_kv", (DEPTH, KVL)),
         ("g_ple", (DEPTH, D)), ("g_final", (D,)))
VEC_ROWS = 48
ROWS_N = sum(r for _, r in ROWS_PIECES)
WIDE_TILE, ROWS_TILE = 512, ROWS_N // 2


def _to_rows(name, a):
    if name == "w_uq":
        a = jnp.pad(a, [(0, 0)] * (a.ndim - 1) + [(0, LANES - UQ_W)])
    return a.reshape(a.shape[:-2] + (-1, LANES))


def _from_rows(name, r):
    if name in ("w_out", "w_ple_gate"):
        return r.reshape(r.shape[:-2] + (D // N_DEV, D))
    return r[..., :UQ_W] if name == "w_uq" else r


def _pack_rows(blocks):
    return jnp.concatenate([_to_rows(n, blocks[n]) for n, _ in ROWS_PIECES], axis=-2)


def _unpack_rows(rows):
    blocks, off = {}, 0
    for n, r in ROWS_PIECES:
        blocks[n] = _from_rows(n, rows[..., off:off + r, :])
        off += r
    return blocks


def _pack_vec(vectors, loss=None):
    parts = [vectors[n].reshape(-1) for n, _ in SMALL] + ([] if loss is None else [loss.reshape(1)])
    vec = jnp.concatenate(parts)
    return jnp.pad(vec, (0, VEC_ROWS * LANES - vec.shape[0])).reshape(1, VEC_ROWS, LANES)


def _unpack_vec(vec):
    vec = vec.reshape(-1)
    vectors, off = {}, 0
    for n, shp in SMALL:
        size = 1
        for s in shp:
            size *= s
        vectors[n] = vec[off:off + size].reshape(shp)
        off += size
    return vectors, vec[off]


def _join(name, blocks):
    if name in ("w_out", "w_ple_gate"):
        return jnp.moveaxis(blocks, 0, 1).reshape(blocks.shape[1], -1, blocks.shape[-1])
    return jnp.moveaxis(blocks, 0, 2).reshape(blocks.shape[1], blocks.shape[2], -1)


def _split(name, full):
    if name in ("w_out", "w_ple_gate"):
        return jnp.moveaxis(full.reshape(full.shape[0], N_DEV, -1, full.shape[-1]), 1, 0)
    return jnp.moveaxis(full.reshape(full.shape[0], full.shape[1], N_DEV, -1), 2, 0)


MESH_ID = pl.DeviceIdType.MESH
ANY = pl.BlockSpec(memory_space=pl.ANY)


def _place():
    return lax.axis_index("x"), lax.axis_index("y"), lax.axis_index("c")


def _all_gather(blocks):
    n = len(blocks)

    def body(*refs):
        start, forward, finish = _gather_phases(refs[:n], refs[n:2 * n], *refs[2 * n:])
        start()
        forward()
        finish()

    return pl.pallas_call(
        body, name="all_gather_weights", out_shape=_gather_out(blocks),
        in_specs=[ANY] * n, out_specs=[ANY] * n, scratch_shapes=_gather_sems(n))(*blocks)


def _gather_out(blocks):
    return [SDS((N_DEV,) + b.shape, b.dtype) for b in blocks]


def _gather_sems(n):
    return [pltpu.SemaphoreType.DMA((7 * n,)), pltpu.SemaphoreType.DMA((7 * n,)), pltpu.SemaphoreType.DMA((n,))]


def _gather_phases(x_refs, out_refs, send_sems, recv_sems, local_sems):
    n = len(x_refs)
    x, y, c = _place()
    me, sibling = (x, y, c), (x, y, 1 - c)
    chips = [(1 - x, y), (x, 1 - y), (1 - x, 1 - y)]

    def slot(a, px, py, pc):
        return out_refs[a].at[4 * px + 2 * py + pc]

    def copy(a, k, blk, to, src=None):
        return pltpu.make_async_remote_copy(
            src_ref=slot(a, *blk) if src is None else src, dst_ref=slot(a, *blk),
            send_sem=send_sems.at[7 * a + k], recv_sem=recv_sems.at[7 * a + k], device_id=to,
            device_id_type=MESH_ID)

    def mine():
        return [pltpu.make_async_copy(x_refs[a], slot(a, *me), local_sems.at[a]) for a in range(n)]

    def first():
        out = []
        for a in range(n):
            out += [copy(a, 0, me, sibling, src=x_refs[a])]
            out += [copy(a, 1 + j, me, (*chip, c), src=x_refs[a]) for j, chip in enumerate(chips)]
        return out

    def passed():
        return [copy(a, 4 + j, (*chip, c), sibling) for j, chip in enumerate(chips) for a in range(n)]

    def start():
        for cp in mine() + first():
            cp.start()

    def forward():
        for j, chip in enumerate(chips):
            for a in range(n):
                copy(a, 1 + j, (*chip, c), me).wait_recv()
                copy(a, 4 + j, (*chip, c), sibling).start()

    def finish():
        for a in range(n):
            copy(a, 0, sibling, me).wait_recv()
            for j, chip in enumerate(chips):
                copy(a, 4 + j, (*chip, 1 - c), me).wait_recv()
        for cp in first() + passed():
            cp.wait_send()
        for cp in mine():
            cp.wait()

    return start, forward, finish


def _swap_sibling(arrs):
    n = len(arrs)

    def body(*refs):
        start, finish = _swap_phases(refs[:n], refs[n:2 * n], *refs[2 * n:])
        start()
        finish()

    return pl.pallas_call(
        body, name="swap_sibling", out_shape=[SDS((a.shape[0],) + a.shape[2:], a.dtype) for a in arrs],
        in_specs=[ANY] * n, out_specs=[ANY] * n, scratch_shapes=_swap_sems(n))(*arrs)


def _swap_sems(n):
    return [pltpu.SemaphoreType.DMA((n,)), pltpu.SemaphoreType.DMA((n,))]


def _swap_phases(a_refs, out_refs, send_sems, recv_sems):
    x, y, c = _place()

    def copies():
        return [pltpu.make_async_remote_copy(
            src_ref=a_refs[a].at[:, 1 - c], dst_ref=out_refs[a], send_sem=send_sems.at[a], recv_sem=recv_sems.at[a],
            device_id=(x, y, 1 - c), device_id_type=MESH_ID) for a in range(len(a_refs))]

    def start():
        for cp in copies():
            cp.start()

    def finish():
        for cp in copies():
            cp.wait()

    return start, finish


def _exchange_chips(arrs):
    n = len(arrs)

    def body(*refs):
        start, finish = _exchange_phases(refs[:n], refs[n:2 * n], *refs[2 * n:])
        start()
        finish()

    return pl.pallas_call(
        body, name="exchange_chips", out_shape=[SDS(a.shape, a.dtype) for a in arrs],
        in_specs=[ANY] * n, out_specs=[ANY] * n, scratch_shapes=_exchange_sems(n))(*arrs)


def _exchange_sems(n):
    return [pltpu.SemaphoreType.DMA((3 * n,)), pltpu.SemaphoreType.DMA((3 * n,)), pltpu.SemaphoreType.DMA((n,))]


def _exchange_phases(p_refs, out_refs, send_sems, recv_sems, local_sems):
    n = len(p_refs)
    x, y, c = _place()
    mine = 2 * x + y
    peers = [(1 - x, y), (x, 1 - y), (1 - x, 1 - y)]

    def local():
        return [pltpu.make_async_copy(p_refs[a].at[mine], out_refs[a].at[mine], local_sems.at[a]) for a in range(n)]

    def copy(a, j, src_chip, dst_chip):
        px, py = peers[j]
        return pltpu.make_async_remote_copy(
            src_ref=p_refs[a].at[src_chip], dst_ref=out_refs[a].at[dst_chip], send_sem=send_sems.at[3 * a + j],
            recv_sem=recv_sems.at[3 * a + j], device_id=(px, py, c), device_id_type=MESH_ID)

    def sends():
        return [copy(a, j, 2 * px + py, mine) for a in range(n) for j, (px, py) in enumerate(peers)]

    def start():
        for cp in local() + sends():
            cp.start()

    def finish():
        for a in range(n):
            for j, (px, py) in enumerate(peers):
                copy(a, j, mine, 2 * px + py).wait_recv()
        for cp in sends():
            cp.wait_send()
        for cp in local():
            cp.wait()

    return start, finish


def _add_mine(g, recv, core, tile, dtype):
    _, _, lead, rows, width = g.shape

    def body(c_ref, g_ref, r_ref, o_ref):
        o_ref[...] = (g_ref[...] + r_ref[...]).astype(dtype)

    spec = pl.BlockSpec((None, None, tile, width), lambda k, l, i, c_ref: (k, l, i, 0))
    return pl.pallas_call(
        body, name="add_sibling", out_shape=SDS(recv.shape, dtype),
        grid_spec=pltpu.PrefetchScalarGridSpec(
            num_scalar_prefetch=1, grid=(g.shape[0], lead, rows // tile),
            in_specs=[pl.BlockSpec((None, None, None, tile, width), lambda k, l, i, c_ref: (k, c_ref[0], l, i, 0)),
                      spec],
            out_specs=spec),
        compiler_params=_cp("parallel", "parallel", "parallel"))(core, g, recv)


def _sum_adamw(parts, w, m, v, tile):
    lead, rows, width = w.shape
    last = rows // tile - 1

    def body(*refs):
        p_refs, (w_ref, m_ref, v_ref, g_ref, d_ref, nm_ref, nv_ref) = refs[:lead], refs[lead:]
        for layer in range(lead):
            @pl.when(pl.program_id(0) == layer)
            def _(p_ref=p_refs[layer]):
                g = ((p_ref[0].astype(F32) + p_ref[1].astype(F32)) + p_ref[2].astype(F32)) + p_ref[3].astype(F32)
                nm = ADAM_B1 * m_ref[...] + (1.0 - ADAM_B1) * g
                nv = ADAM_B2 * v_ref[...] + (1.0 - ADAM_B2) * jnp.square(g)
                m_hat = nm / (1.0 - ADAM_B1 ** ADAM_STEP)
                v_hat = nv / (1.0 - ADAM_B2 ** ADAM_STEP)
                g_ref[...] = g
                nm_ref[...] = nm
                nv_ref[...] = nv
                d_ref[...] = -ADAM_LR * (m_hat / (jnp.sqrt(v_hat) + ADAM_EPS) + ADAM_WD * w_ref[...])

    pspec = lambda layer: pl.BlockSpec(
        (4, None, tile, width),
        lambda l, i: (0, 0, jnp.where(l == layer, i, jnp.where(l > layer, last, 0)), 0))
    spec = pl.BlockSpec((None, tile, width), lambda l, i: (l, i, 0))
    return pl.pallas_call(
        body, grid=(lead, rows // tile), name="sum_adamw",
        in_specs=[pspec(layer) for layer in range(lead)] + [spec, spec, spec],
        out_specs=[spec] * 4, out_shape=[SDS((lead, rows, width), F32)] * 4,
        compiler_params=_cp("arbitrary", "arbitrary"))(*parts, w, m, v)


def kernel(x, p, positions, g_mix, w_in, sink, g_q, w_uq, g_kv, w_ukv, w_br_a, w_br_b, w_out, g_ple, w_ple_gate, w_ple_proj, g_final, loss_target, m_g_mix, m_w_in, m_sink, m_g_q, m_w_uq, m_g_kv, m_w_ukv, m_w_br_a, m_w_br_b, m_w_out, m_g_ple, m_w_ple_gate, m_w_ple_proj, m_g_final, v_g_mix, v_w_in, v_sink, v_g_q, v_w_uq, v_g_kv, v_w_ukv, v_w_br_a, v_w_br_b, v_w_out, v_g_ple, v_w_ple_gate, v_w_ple_proj, v_g_final):
    weights = dict(g_mix=g_mix, w_in=w_in, sink=sink, g_q=g_q, w_uq=w_uq, g_kv=g_kv, w_ukv=w_ukv, w_br_a=w_br_a,
                   w_br_b=w_br_b, w_out=w_out, g_ple=g_ple, w_ple_gate=w_ple_gate, w_ple_proj=w_ple_proj,
                   g_final=g_final)
    mom1 = dict(g_mix=m_g_mix, w_in=m_w_in, sink=m_sink, g_q=m_g_q, w_uq=m_w_uq, g_kv=m_g_kv, w_ukv=m_w_ukv,
                w_br_a=m_w_br_a, w_br_b=m_w_br_b, w_out=m_w_out, g_ple=m_g_ple, w_ple_gate=m_w_ple_gate,
                w_ple_proj=m_w_ple_proj, g_final=m_g_final)
    mom2 = dict(g_mix=v_g_mix, w_in=v_w_in, sink=v_sink, g_q=v_g_q, w_uq=v_w_uq, g_kv=v_g_kv, w_ukv=v_w_ukv,
                w_br_a=v_w_br_a, w_br_b=v_w_br_b, w_out=v_w_out, g_ple=v_g_ple, w_ple_gate=v_w_ple_gate,
                w_ple_proj=v_w_ple_proj, g_final=v_g_final)
    assert DEPTH == 2
    wide = lambda d: d["w_in"]
    rows = lambda d: _pack_rows(d)
    core = lax.axis_index("c").astype(jnp.int32).reshape(1)

    w16 = [wide(weights).astype(BF16), rows(weights).astype(BF16)]
    wts0 = dict(w_in=_win_layout(_all_gather([w16[0][:1]])[0], 256))
    small = _small_params(g_mix, sink, g_q, g_kv, g_ple, g_final)

    def wide_payload(g):
        return _win_grad_layout(g["w_in"], 256).reshape(N_DEV // 2, 2, 1, D, WIDE_W)

    def rows_payload(g):
        return _pack_rows({n: _split(n, g[n][None]) for n, _ in ROWS_PIECES}).reshape(N_DEV // 2, 2, 1, ROWS_N, LANES)

    def add(pay, got):
        tiles = {D: (WIDE_TILE, BF16), ROWS_N: (ROWS_TILE, BF16), VEC_ROWS: (VEC_ROWS, F32)}
        return [_add_mine(a, b, core, *tiles[a.shape[-2]]) for a, b in zip(pay, got)]

    plan = dict(behind_fwd_in=[w16[1][:1]], behind_mla_fwd=[a[1:] for a in w16],
                w_in=lambda blocks: _win_layout(blocks, 256),
                row_weights=_row_weights, payload=lambda g: [wide_payload(g), rows_payload(g)],
                rows_payload=rows_payload, add=add)
    loss, grad_x, grads, dg_final, rode = _local_step(x, p, positions, loss_target, small, [wts0], plan)

    vectors = {n: jnp.stack([grads[i][n] for i in range(DEPTH)]) for n, _ in SMALL[:-1]}
    vectors["g_final"] = dg_final
    pay = [wide_payload(grads[0]),
           jnp.broadcast_to(_pack_vec(vectors, loss[0, 0]), (N_DEV // 2, 2, 1, VEC_ROWS, LANES))]
    parts_wide0, parts_vec = _exchange_chips(add(pay, _swap_sibling(pay)))
    out_wide = _sum_adamw([parts_wide0, rode["layer1"][0]], wide(weights), wide(mom1), wide(mom2), WIDE_TILE)
    out_rows = _sum_adamw([rode["rows0"][0], rode["layer1"][1]], rows(weights), rows(mom1), rows(mom2), ROWS_TILE)
    out_vec = _sum_adamw([parts_vec], _pack_vec(weights), _pack_vec(mom1), _pack_vec(mom2), VEC_ROWS)

    outs = []
    for ow, orow, ovec in zip(out_wide, out_rows, out_vec):
        named = _unpack_rows(orow)
        named.update(_unpack_vec(ovec)[0])
        named["w_in"] = ow
        outs += [named[n] for n in weights]
    loss = _unpack_vec(out_vec[0])[1]
    return (loss, grad_x, *outs)
```

```python
import functools

import jax
import jax.numpy as jnp
from jax import lax
from jax.experimental import pallas as pl
from jax.experimental.pallas import tpu as pltpu

F32, BF16 = jnp.float32, jnp.bfloat16
SDS = jax.ShapeDtypeStruct

D = 1024
DEPTH = 2
PLE = 256
BLK = 128
EPS = 1e-6
NEG = -1e30
SWA_H, SWA_KV, SWA_DH = 8, 2, 64
MLA_H, MLA_NOPE, MLA_ROPE, MLA_V = 8, 64, 32, 64
MLA_QK = MLA_NOPE + MLA_ROPE
QL, KVL = 256, 128
IN_W = 4256
N_DEV = 8

V7X_VMEM_BYTES = 64 * 1024 * 1024
LANES = 128
VMEM_LIMIT = V7X_VMEM_BYTES * 7 // 8

ZW = 4352
Z_MA, Z_MB, Z_AQ, Z_AG, Z_BG, Z_QD, Z_AK, Z_AV, Z_KVD, Z_KR = 0, 1024, 2048, 2560, 3072, 3584, 3840, 3968, 4096, 4224
QFW = MLA_H * LANES
KVW = QFW + MLA_H * MLA_V
MLA_SCALE = MLA_QK ** -0.5
LOG2E = 1.4426950408889634
MLA_FWD_HEADS, MLA_BWD_HEADS = 8, 8
SWA_SCALE = SWA_DH ** -0.5
ROLL_UP, ROLL_DOWN = MLA_ROPE // 2, LANES - MLA_ROPE // 2

ADAM_LR, ADAM_B1, ADAM_B2, ADAM_EPS, ADAM_WD, ADAM_STEP = 0.001, 0.9, 0.999, 1e-08, 0.01, 10


def _cp(*sem):
    return pltpu.CompilerParams(dimension_semantics=sem, vmem_limit_bytes=VMEM_LIMIT)


def _row(tm, w, col=0):
    return pl.BlockSpec((tm, w), lambda i: (i, col))


def _res(shape, layer=None):
    if layer is None:
        return pl.BlockSpec(shape, lambda *_: (0,) * len(shape), pipeline_mode=pl.Buffered(1))
    return pl.BlockSpec((None,) + shape, lambda *_: (layer,) + (0,) * len(shape), pipeline_mode=pl.Buffered(1))


def _acc(shape):
    return pl.BlockSpec(shape, lambda *_: (0,) * len(shape))


def _rstd(xf):
    return lax.rsqrt(jnp.mean(xf * xf, axis=-1, keepdims=True) + EPS)


def _norm_bwd(dh, n, r, g):
    dn = dh * g
    return r * (dn - n * jnp.mean(dn * n, axis=-1, keepdims=True)), dh * n


def _nt(a, b):
    return lax.dot_general(a, b, (((1,), (1,)), ((), ())), preferred_element_type=F32)


def _tn(a, b):
    return lax.dot_general(a, b, (((0,), (0,)), ((), ())), preferred_element_type=F32)


def _nn(a, b):
    return jnp.dot(a, b, preferred_element_type=F32)


def _sig(x):
    return jax.nn.sigmoid(x)


def _rope(t, c, s1, s2):
    return t * c + pltpu.roll(t, ROLL_UP, 1) * s1 + pltpu.roll(t, ROLL_DOWN, 1) * s2


def _rope_t(d, c, s1, s2):
    return d * c + pltpu.roll(d * s1, ROLL_DOWN, 1) + pltpu.roll(d * s2, ROLL_UP, 1)


def _fwd_in(x, g, w, tm, layer, ride=None):
    T = x.shape[0]
    grid = (T // tm,)

    def body(x_ref, g_ref, w_ref, z_ref, h_ref):
        xf = x_ref[...]
        h = ((xf * _rstd(xf)) * g_ref[...]).astype(BF16)
        h_ref[...] = h
        z_ref[...] = _nn(h, w_ref[...]).astype(BF16)

    r_in, r_out, r_shape, r_scratch, r_args = _ride_args(ride)
    return pl.pallas_call(
        _riding(ride, body, 3, 2, grid), grid=grid, name="fwd_in_ride" if ride else "fwd_in",
        in_specs=[_row(tm, D), _res((1, D), layer), _res((D, ZW), 0)] + r_in,
        out_specs=[_row(tm, ZW), _row(tm, D)] + r_out,
        out_shape=[SDS((T, ZW), BF16), SDS((T, D), BF16)] + r_shape, scratch_shapes=r_scratch,
        compiler_params=_cp("arbitrary"))(x, g, w, *r_args)


def _fwd_prep(z, gq, gkv, wq, wkv, tc, ts1, ts2, tm, layer):
    T = z.shape[0]

    def body(qd_ref, kvd_ref, kr_ref, gq_ref, gkv_ref, wq_ref, wkv_ref, c_ref, s1_ref, s2_ref, q_ref, k_ref, v_ref):
        qd, kvd = qd_ref[...].astype(F32), kvd_ref[...].astype(F32)
        hq = ((qd * _rstd(qd)) * gq_ref[...]).astype(BF16)
        hkv = ((kvd * _rstd(kvd)) * gkv_ref[...]).astype(BF16)
        qf = _nn(hq, wq_ref[...])
        kvf = _nn(hkv, wkv_ref[...])
        c, s1, s2 = c_ref[...], s1_ref[...], s2_ref[...]
        krb = _rope(kr_ref[...].astype(F32), c, s1, s2)
        for h in range(MLA_H):
            sl = slice(LANES * h, LANES * (h + 1))
            q_ref[:, sl] = _rope(qf[:, sl], c, s1, s2).astype(BF16)
            k_ref[:, sl] = (kvf[:, sl] + krb).astype(BF16)
        v_ref[...] = kvf[:, QFW:].astype(BF16)

    return pl.pallas_call(
        body, grid=(T // tm,), name="fwd_prep",
        in_specs=[_row(tm, QL, Z_QD // QL), _row(tm, KVL, Z_KVD // KVL), _row(tm, LANES, Z_KR // LANES),
                  _res((1, QL), layer), _res((1, KVL), layer), _res((QL, QFW), 0), _res((KVL, KVW), 0),
                  _row(tm, LANES), _row(tm, LANES), _row(tm, LANES)],
        out_specs=[_row(tm, QFW), _row(tm, QFW), _row(tm, MLA_H * MLA_V)],
        out_shape=[SDS((T, QFW), BF16), SDS((T, QFW), BF16), SDS((T, MLA_H * MLA_V), BF16)],
        compiler_params=_cp("parallel"))(z, z, z, gq, gkv, wq, wkv, tc, ts1, ts2)


def _grid_ends(grid):
    ids = [pl.program_id(a) for a in range(len(grid))]
    inner_first = functools.reduce(jnp.logical_and, [i == 0 for i in ids[1:]], True)
    last = functools.reduce(jnp.logical_and, [i == g - 1 for i, g in zip(ids, grid)])
    return (ids[0] == 0) & inner_first, (ids[0] == 3 * grid[0] // 4) & inner_first, last


class _Ride:
    def __init__(self, kind, arrays):
        self.kind, self.arrays, self.n = kind, list(arrays), len(arrays)

    def out_shape(self):
        if self.kind == "gather":
            return _gather_out(self.arrays)
        if self.kind == "swap":
            return [SDS((a.shape[0],) + a.shape[2:], a.dtype) for a in self.arrays]
        return [SDS(a.shape, a.dtype) for a in self.arrays]

    def sems(self):
        if self.kind == "gather":
            return _gather_sems(self.n)
        if self.kind == "swap":
            return _swap_sems(self.n)
        return _exchange_sems(self.n)

    def phases(self, in_refs, out_refs, *sems):
        if self.kind == "gather":
            return _gather_phases(in_refs, out_refs, *sems)
        start, finish = (_swap_phases if self.kind == "swap" else _exchange_phases)(in_refs, out_refs, *sems)
        return start, None, finish


def _riding(ride, body, n_in, n_out, grid):
    if ride is None:
        return body
    n, n_sems = ride.n, len(ride.sems())

    def wrapped(*refs):
        ins, r_in = refs[:n_in], refs[n_in:n_in + n]
        outs, r_out = refs[n_in + n:n_in + n + n_out], refs[n_in + n + n_out:n_in + 2 * n + n_out]
        rest = refs[n_in + 2 * n + n_out:]
        scratch, sems = rest[:len(rest) - n_sems], rest[len(rest) - n_sems:]
        start, middle, finish = ride.phases(r_in, r_out, *sems)
        at_first, at_middle, at_last = _grid_ends(grid)
        pl.when(at_first)(start)
        if middle is not None:
            pl.when(at_middle)(middle)
        body(*ins, *outs, *scratch)
        pl.when(at_last)(finish)

    return wrapped


def _ride_args(ride):
    if ride is None:
        return [], [], [], [], []
    return [ANY] * ride.n, [ANY] * ride.n, ride.out_shape(), ride.sems(), ride.arrays


def _mla_fwd(qf, kf, v, nb, seq, tq, gather=()):
    T = qf.shape[0]
    nq = seq // tq
    hp = MLA_FWD_HEADS
    pw = hp * LANES
    pairs = [(qi, ki) for qi in range(nq) for ki in range(qi + 1)]
    qi_tab = jnp.array([qk[0] for qk in pairs], jnp.int32)
    ki_tab = jnp.array([qk[1] for qk in pairs], jnp.int32)
    grid = (nb, MLA_H // hp, len(pairs))
    n_g = len(gather)

    def body(qi_ref, ki_ref, q_ref, k_ref, v_ref, *rest):
        x_refs, (o_ref, lse_ref), got_refs = rest[:n_g], rest[n_g:n_g + 2], rest[n_g + 2:2 * n_g + 2]
        (m_s, l_s, acc_s), sems = rest[2 * n_g + 2:2 * n_g + 5], rest[2 * n_g + 5:]
        qi, ki = qi_ref[pl.program_id(2)], ki_ref[pl.program_id(2)]
        if n_g:
            start, forward, finish = _gather_phases(x_refs, got_refs, *sems)
            at_first, at_middle, at_last = _grid_ends(grid)
            pl.when(at_first)(start)
            pl.when(at_middle)(forward)

        @pl.when(ki == 0)
        def _():
            m_s[...] = jnp.full(m_s.shape, NEG, F32)
            l_s[...] = jnp.zeros(l_s.shape, F32)
            acc_s[...] = jnp.zeros(acc_s.shape, F32)

        def step(masked):
            parts = [(0, tq // 2, tq // 2), (tq // 2, tq, tq)] if masked else [(0, tq, tq)]
            work = [(j, a, b, kh) for j in range(hp) for a, b, kh in parts]
            ss = []
            for j, a, b, kh in work:
                wide = slice(LANES * j, LANES * (j + 1))
                s = _nt(k_ref[:kh, wide], q_ref[a:b, wide]) * (MLA_SCALE * LOG2E)
                if masked:
                    keys = lax.broadcasted_iota(jnp.int32, (kh, b - a), 0)
                    queries = a + lax.broadcasted_iota(jnp.int32, (kh, b - a), 1)
                    s = jnp.where(keys <= queries, s, NEG)
                ss.append(s)
            ps, alphas = [], []
            for (j, a, b, kh), s in zip(work, ss):
                m_prev = m_s[j, :, a:b]
                m_new = jnp.maximum(m_prev, jnp.max(s, axis=0, keepdims=True))
                alpha = jnp.exp2(m_prev - m_new)
                p = jnp.exp2(s - m_new)
                l_s[j, :, a:b] = alpha * l_s[j, :, a:b] + jnp.sum(p, axis=0, keepdims=True)
                m_s[j, :, a:b] = m_new
                ps.append(p.astype(BF16))
                alphas.append(alpha)
            for (j, a, b, kh), p, alpha in zip(work, ps, alphas):
                rows = slice(MLA_V * j, MLA_V * (j + 1))
                acc_s[rows, a:b] = alpha * acc_s[rows, a:b] + _tn(v_ref[:kh, rows], p)

        @pl.when(ki < qi)
        def _():
            step(False)

        @pl.when(ki == qi)
        def _():
            step(True)
            for j in range(hp):
                rows = slice(MLA_V * j, MLA_V * (j + 1))
                acc_s[rows, :] = acc_s[rows, :] / l_s[j]
                lse_ref[j:j + 1, :] = m_s[j] + jnp.log2(l_s[j])
            o_ref[...] = acc_s[...].T

        if n_g:
            pl.when(at_last)(finish)

    q_map = lambda b, g, s, qi_ref, ki_ref: (b * nq + qi_ref[s], g)
    kv_map = lambda b, g, s, qi_ref, ki_ref: (b * nq + ki_ref[s], g)
    return pl.pallas_call(
        body, name="mla_fwd_gather" if n_g else "mla_fwd",
        grid_spec=pltpu.PrefetchScalarGridSpec(
            num_scalar_prefetch=2, grid=grid,
            in_specs=[pl.BlockSpec((tq, pw), q_map), pl.BlockSpec((tq, pw), kv_map),
                      pl.BlockSpec((tq, hp * MLA_V), kv_map)] + [ANY] * n_g,
            out_specs=[pl.BlockSpec((tq, hp * MLA_V), q_map),
                       pl.BlockSpec((hp, tq), lambda b, g, s, qi_ref, ki_ref: (g, b * nq + qi_ref[s]))]
            + [ANY] * n_g,
            scratch_shapes=[pltpu.VMEM((hp, 1, tq), F32), pltpu.VMEM((hp, 1, tq), F32),
                            pltpu.VMEM((hp * MLA_V, tq), F32)]
            + (_gather_sems(n_g) if n_g else [])),
        out_shape=[SDS((T, MLA_H * MLA_V), F32), SDS((MLA_H, T), F32)] + _gather_out(gather),
        compiler_params=_cp("arbitrary", "arbitrary", "arbitrary"))(qi_tab, ki_tab, qf, kf, v, *gather)


SWA_BLOCKS = 8


def _swa_blocks(seq):
    return min(SWA_BLOCKS, seq // BLK)


def _swa_specs(nstep, nu):
    step = nu * BLK
    cur = lambda b, m: (b * nstep + m, 0)
    prev = lambda b, m: (nu * b * nstep + jnp.maximum(nu * m - 1, 0), 0)
    kvc = Z_AK // (2 * BLK)
    return [pl.BlockSpec(memory_space=pltpu.SMEM),
            pl.BlockSpec((step, 512), lambda b, m: (b * nstep + m, Z_AQ // 512)),
            pl.BlockSpec((step, 2 * BLK), lambda b, m: (b * nstep + m, kvc)),
            pl.BlockSpec((BLK, 2 * BLK), lambda b, m: (nu * b * nstep + jnp.maximum(nu * m - 1, 0), kvc)),
            pl.BlockSpec((step, 1), cur),
            pl.BlockSpec((BLK, 1), prev),
            pl.BlockSpec((1, 1, step), lambda b, m: (b * nstep + m, 0, 0))]


def _swa_scores(m, nu, q_ref, kvc_ref, kvp_ref, pcc_ref, pcp_ref, pr_ref):
    kv = jnp.concatenate([kvp_ref[...], kvc_ref[...]], axis=0)
    kb, vb = kv[:, :BLK].astype(BF16), kv[:, BLK:].astype(BF16)
    pos_keys = jnp.concatenate([pcp_ref[...], pcc_ref[...]], axis=0)
    key = lax.broadcasted_iota(jnp.int32, (2 * BLK, BLK), 0)
    qry = lax.broadcasted_iota(jnp.int32, (2 * BLK, BLK), 1)
    in_window = (key > qry) & (key <= qry + BLK)
    valid = [in_window & ((key >= BLK) | (m > 0))] + [in_window] * (nu - 1)
    dist = [pr_ref[0][:, BLK * u:BLK * (u + 1)] - pos_keys[BLK * u:BLK * (u + 2)] for u in range(nu)]

    def band(t, u, g):
        return t[BLK * u:BLK * (u + 2), SWA_DH * g:SWA_DH * (g + 1)]

    def scores(u, h):
        g = h // (SWA_H // SWA_KV)
        qh = q_ref[BLK * u:BLK * (u + 1), SWA_DH * h:SWA_DH * (h + 1)].astype(BF16)
        s = _nt(band(kb, u, g), qh) * (SWA_SCALE * LOG2E) - (2.0 ** -(h + 1) * LOG2E) * dist[u]
        return qh, jnp.where(valid[u], s, NEG)

    return kb, vb, band, scores


def _swa_fwd(sink, z, pos_col, pos_row, nb, seq, layer):
    T = z.shape[0]
    nu = _swa_blocks(seq)
    step = nu * BLK
    nstep = seq // step
    chains = [(u, h) for u in range(nu) for h in range(SWA_H)]

    def body(sink_ref, q_ref, kvc_ref, kvp_ref, pcc_ref, pcp_ref, pr_ref, o_ref, lse_ref):
        kb, vb, band, scores = _swa_scores(pl.program_id(1), nu, q_ref, kvc_ref, kvp_ref, pcc_ref, pcp_ref, pr_ref)
        ss = [scores(u, h)[1] for u, h in chains]
        es, dens = [], []
        for (u, h), s in zip(chains, ss):
            sk = sink_ref[layer, h] * LOG2E
            m = jnp.maximum(jnp.max(s, axis=0, keepdims=True), sk)
            e = jnp.exp2(s - m)
            den = jnp.sum(e, axis=0, keepdims=True) + jnp.exp2(sk - m)
            lse_ref[h:h + 1, BLK * u:BLK * (u + 1)] = m + jnp.log2(den)
            es.append(e.astype(BF16))
            dens.append(den)
        outs = [_tn(band(vb, u, h // (SWA_H // SWA_KV)), e) / den for (u, h), e, den in zip(chains, es, dens)]
        for u in range(nu):
            o_ref[BLK * u:BLK * (u + 1), :] = jnp.concatenate(outs[SWA_H * u:SWA_H * (u + 1)], axis=0).T

    return pl.pallas_call(
        body, grid=(nb, nstep), name="swa_fwd",
        in_specs=_swa_specs(nstep, nu),
        out_specs=[pl.BlockSpec((step, 512), lambda b, m: (b * nstep + m, 0)),
                   pl.BlockSpec((SWA_H, step), lambda b, m: (0, b * nstep + m))],
        out_shape=[SDS((T, 512), F32), SDS((SWA_H, T), F32)],
        compiler_params=_cp("parallel", "parallel"))(sink, z, z, z, pos_col, pos_col, pos_row)


def _fwd_merge(x, oa, ob, z, wa, wb, wo, tm, layer):
    T = x.shape[0]

    def body(x_ref, oa_ref, ob_ref, ag_ref, bg_ref, ma_ref, mb_ref, wa_ref, wb_ref, wo_ref, x1_ref):
        ag, bg = ag_ref[...].astype(F32), bg_ref[...].astype(F32)
        ua = _nn((oa_ref[...] * (ag * _sig(ag))).astype(BF16), wa_ref[...])
        ub = _nn((ob_ref[...] * (bg * _sig(bg))).astype(BF16), wb_ref[...])
        y = _sig(ma_ref[...].astype(F32)) * ua + _sig(mb_ref[...].astype(F32)) * ub
        x1_ref[...] = x_ref[...] + _nn(y.astype(BF16), wo_ref[...])

    return pl.pallas_call(
        body, grid=(T // tm,), name="fwd_merge",
        in_specs=[_row(tm, D), _row(tm, 512), _row(tm, 512), _row(tm, 512, Z_AG // 512), _row(tm, 512, Z_BG // 512),
                  _row(tm, D, Z_MA // D), _row(tm, D, Z_MB // D),
                  _res((512, D), 0), _res((512, D), 0), _res((D, D), 0)],
        out_specs=_row(tm, D),
        out_shape=SDS((T, D), F32),
        compiler_params=_cp("parallel"))(x, oa, ob, z, z, z, z, wa, wb, wo)


def _fwd_ple(x1, p, g, wpg, wpp, tm, layer):
    T = x1.shape[0]

    def body(x_ref, p_ref, g_ref, wpg_ref, wpp_ref, x2_ref, pg_ref, pp_ref):
        xf = x_ref[...]
        h1 = ((xf * _rstd(xf)) * g_ref[...]).astype(BF16)
        pg = _sig(_nn(h1, wpg_ref[...]))
        pp = _nn(p_ref[...].astype(BF16), wpp_ref[...])
        pg_ref[...] = pg
        pp_ref[...] = pp
        x2_ref[...] = xf + pg * pp

    return pl.pallas_call(
        body, grid=(T // tm,), name="fwd_ple",
        in_specs=[_row(tm, D), pl.BlockSpec((None, tm, PLE), lambda i: (layer, i, 0)),
                  _res((1, D), layer), _res((D, D), 0), _res((PLE, D), 0)],
        out_specs=[_row(tm, D)] * 3,
        out_shape=[SDS((T, D), F32)] * 3,
        compiler_params=_cp("parallel"))(x1, p, g, wpg, wpp)


def _ple_loss(x1, p, g, wpg, wpp, g_final, tgt, tm, layer):
    T = x1.shape[0]

    def body(x_ref, p_ref, g_ref, wpg_ref, wpp_ref, gf_ref, t_ref, dx_ref, dwg_ref, dwp_ref, dg_ref, dgf_ref, loss_ref):
        @pl.when(pl.program_id(0) == 0)
        def _():
            for ref in (dwg_ref, dwp_ref, dg_ref, dgf_ref, loss_ref):
                ref[...] = jnp.zeros(ref.shape, F32)

        xf, gp, gf = x_ref[...], g_ref[...], gf_ref[...]
        r = _rstd(xf)
        n = xf * r
        h1 = (n * gp).astype(BF16)
        pb = p_ref[...].astype(BF16)
        pg = _sig(_nn(h1, wpg_ref[...]))
        pp = _nn(pb, wpp_ref[...])
        x2 = xf + pg * pp
        r2 = _rstd(x2)
        n2 = x2 * r2
        err = n2 * gf - t_ref[...]
        loss_ref[...] += 0.5 * jnp.sum(jnp.mean(err * err, axis=-1, keepdims=True), axis=0, keepdims=True)
        d, dgfr = _norm_bwd(err * (1.0 / D), n2, r2, gf)
        dgf_ref[...] += jnp.sum(dgfr, axis=0, keepdims=True)
        dpgl = (d * pp * pg * (1.0 - pg)).astype(BF16)
        dwg_ref[...] += _tn(h1, dpgl)
        dwp_ref[...] += _tn(pb, (d * pg).astype(BF16))
        dxn, dgr = _norm_bwd(_nt(dpgl, wpg_ref[...]), n, r, gp)
        dx_ref[...] = d + dxn
        dg_ref[...] += jnp.sum(dgr, axis=0, keepdims=True)

    return pl.pallas_call(
        body, grid=(T // tm,), name="ple_loss",
        in_specs=[_row(tm, D), pl.BlockSpec((None, tm, PLE), lambda i: (layer, i, 0)), _res((1, D), layer),
                  _res((D, D), 0), _res((PLE, D), 0), _res((1, D)), _row(tm, D)],
        out_specs=[_row(tm, D), _acc((D, D)), _acc((PLE, D)), _acc((1, D)), _acc((1, D)), _acc((1, LANES))],
        out_shape=[SDS((T, D), F32), SDS((D, D), F32), SDS((PLE, D), F32), SDS((1, D), F32), SDS((1, D), F32),
                   SDS((1, LANES), F32)],
        compiler_params=_cp("arbitrary"))(x1, p, g, wpg, wpp, g_final, tgt)


def _bwd_ple(dx2, x1, pg, pp, p, g, wpg, tm, layer, ride=None):
    T = x1.shape[0]
    grid = (T // tm,)

    def body(d_ref, x_ref, pg_ref, pp_ref, p_ref, g_ref, w_ref, dx_ref, dwg_ref, dwp_ref, dg_ref):
        @pl.when(pl.program_id(0) == 0)
        def _():
            dwg_ref[...] = jnp.zeros(dwg_ref.shape, F32)
            dwp_ref[...] = jnp.zeros(dwp_ref.shape, F32)
            dg_ref[...] = jnp.zeros(dg_ref.shape, F32)

        d, xf, pg, gf = d_ref[...], x_ref[...], pg_ref[...], g_ref[...]
        r = _rstd(xf)
        n = xf * r
        dpgl = (d * pp_ref[...] * pg * (1.0 - pg)).astype(BF16)
        dwg_ref[...] += _tn((n * gf).astype(BF16), dpgl)
        dwp_ref[...] += _tn(p_ref[...].astype(BF16), (d * pg).astype(BF16))
        dxn, dgr = _norm_bwd(_nt(dpgl, w_ref[...]), n, r, gf)
        dx_ref[...] = d + dxn
        dg_ref[...] += jnp.sum(dgr, axis=0, keepdims=True)

    r_in, r_out, r_shape, r_scratch, r_args = _ride_args(ride)
    return pl.pallas_call(
        _riding(ride, body, 7, 4, grid), grid=grid, name="bwd_ple_ride" if ride else "bwd_ple",
        in_specs=[_row(tm, D)] * 4 + [pl.BlockSpec((None, tm, PLE), lambda i: (layer, i, 0)),
                                      _res((1, D), layer), _res((D, D), 0)] + r_in,
        out_specs=[_row(tm, D), _acc((D, D)), _acc((PLE, D)), _acc((1, D))] + r_out,
        out_shape=[SDS((T, D), F32), SDS((D, D), F32), SDS((PLE, D), F32), SDS((1, D), F32)] + r_shape,
        scratch_shapes=r_scratch,
        compiler_params=_cp("arbitrary"))(dx2, x1, pg, pp, p, g, wpg, *r_args)


def _bwd_merge(dx1, oa, ob, z, wa, wb, wo, tm, layer):
    T = dx1.shape[0]

    def body(d_ref, oa_ref, ob_ref, ag_ref, bg_ref, ma_ref, mb_ref, wa_ref, wb_ref, wo_ref,
             doa_ref, dob_ref, dag_ref, dbg_ref, dma_ref, dmb_ref, dsa_ref, dsb_ref, dwa_ref, dwb_ref, dwo_ref):
        @pl.when(pl.program_id(0) == 0)
        def _():
            dwa_ref[...] = jnp.zeros(dwa_ref.shape, F32)
            dwb_ref[...] = jnp.zeros(dwb_ref.shape, F32)
            dwo_ref[...] = jnp.zeros(dwo_ref.shape, F32)

        db = d_ref[...].astype(BF16)
        gated = []
        for o_ref, gate_ref, w_ref in ((oa_ref, ag_ref, wa_ref), (ob_ref, bg_ref, wb_ref)):
            raw, gate = o_ref[...], gate_ref[...].astype(F32)
            sg = _sig(gate)
            silu = gate * sg
            ob16 = (raw * silu).astype(BF16)
            gated.append((raw, gate, sg, silu, ob16, _nn(ob16, w_ref[...])))
        ua, ub = gated[0][5], gated[1][5]
        sa, sb = _sig(ma_ref[...].astype(F32)), _sig(mb_ref[...].astype(F32))
        dwo_ref[...] += _tn((sa * ua + sb * ub).astype(BF16), db)
        dy = _nt(db, wo_ref[...])
        dma_ref[...] = (dy * ua * sa * (1.0 - sa)).astype(BF16)
        dmb_ref[...] = (dy * ub * sb * (1.0 - sb)).astype(BF16)
        for (s, w_ref, do_ref, dgate_ref, dw_ref, ds_ref), (raw, gate, sg, silu, ob16, _) in zip((
                (sa, wa_ref, doa_ref, dag_ref, dwa_ref, dsa_ref),
                (sb, wb_ref, dob_ref, dbg_ref, dwb_ref, dsb_ref)), gated):
            du = (dy * s).astype(BF16)
            dw_ref[...] += _tn(ob16, du)
            do = _nt(du, w_ref[...])
            draw = do * silu
            do_ref[...] = draw.astype(BF16)
            dgate_ref[...] = (do * raw * (sg * (1.0 + gate * (1.0 - sg)))).astype(BF16)
            ds_ref[...] = jnp.sum((draw * raw).T.reshape(MLA_H, MLA_V, tm), axis=1)

    return pl.pallas_call(
        body, grid=(T // tm,), name="bwd_merge",
        in_specs=[_row(tm, D), _row(tm, 512), _row(tm, 512), _row(tm, 512, Z_AG // 512), _row(tm, 512, Z_BG // 512),
                  _row(tm, D, Z_MA // D), _row(tm, D, Z_MB // D),
                  _res((512, D), 0), _res((512, D), 0), _res((D, D), 0)],
        out_specs=[_row(tm, 512)] * 4 + [_row(tm, D)] * 2 + [pl.BlockSpec((MLA_H, tm), lambda i: (0, i))] * 2
        + [_acc((512, D)), _acc((512, D)), _acc((D, D))],
        out_shape=[SDS((T, 512), BF16), SDS((T, 512), BF16), SDS((T, 512), BF16), SDS((T, 512), BF16),
                   SDS((T, D), BF16), SDS((T, D), BF16), SDS((MLA_H, T), F32), SDS((MLA_H, T), F32),
                   SDS((512, D), F32), SDS((512, D), F32), SDS((D, D), F32)],
        compiler_params=_cp("arbitrary"))(dx1, oa, ob, z, z, z, z, wa, wb, wo)


def _mla_bwd(qf, kf, v, do, lse, dsum, nb, seq, tq, exchange=()):
    T = qf.shape[0]
    nq = seq // tq
    hp = MLA_BWD_HEADS
    pw = hp * LANES
    pairs = [(qi, ki) for ki in range(nq) for qi in range(ki, nq)]
    qi_tab = jnp.array([qk[0] for qk in pairs], jnp.int32)
    ki_tab = jnp.array([qk[1] for qk in pairs], jnp.int32)
    grid = (nb, MLA_H // hp, len(pairs))
    n_x = len(exchange)

    def body(qi_ref, ki_ref, q_ref, k_ref, v_ref, do_ref, lse_ref, dsum_ref, *rest):
        p_refs, (dq_ref, dk_ref, dv_ref), got_refs = rest[:n_x], rest[n_x:n_x + 3], rest[n_x + 3:2 * n_x + 3]
        (dk_s, dv_s, dqt_s), sems = rest[2 * n_x + 3:2 * n_x + 6], rest[2 * n_x + 6:]
        step_id = pl.program_id(2)
        qi, ki = qi_ref[step_id], ki_ref[step_id]
        if n_x:
            start, finish = _exchange_phases(p_refs, got_refs, *sems)
            at_first, _, at_last = _grid_ends(grid)
            pl.when(at_first)(start)

        @pl.when(step_id == 0)
        def _():
            dqt_s[...] = jnp.zeros(dqt_s.shape, F32)

        @pl.when(qi == ki)
        def _():
            dk_s[...] = jnp.zeros(dk_s.shape, F32)
            dv_s[...] = jnp.zeros(dv_s.shape, F32)

        def step(masked):
            if masked:
                keys = lax.broadcasted_iota(jnp.int32, (tq, tq), 0)
                queries = lax.broadcasted_iota(jnp.int32, (tq, tq), 1)
                mask = keys <= queries
            for j in range(hp):
                wide = slice(LANES * j, LANES * (j + 1))
                sl = slice(MLA_V * j, MLA_V * (j + 1))
                q, k = q_ref[:, wide], k_ref[:, wide]
                dob = do_ref[:, sl].astype(BF16)
                s = _nt(k, q) * (MLA_SCALE * LOG2E)
                if masked:
                    s = jnp.where(mask, s, NEG)
                p = jnp.exp2(s - lse_ref[j:j + 1, :])
                dv_s[:, sl] += _nn(p.astype(BF16), dob)
                ds = (p * (_nt(v_ref[:, sl], dob) - dsum_ref[j:j + 1, :]) * MLA_SCALE).astype(BF16)
                dk_s[:, wide] += _nn(ds, q)
                dqt_s[qi, wide, :] += _tn(k, ds)

        @pl.when(qi > ki)
        def _():
            step(False)

        @pl.when(qi == ki)
        def _():
            step(True)

        @pl.when(qi == nq - 1)
        def _():
            dk_ref[...] = dk_s[...]
            dv_ref[...] = dv_s[...]

        @pl.when(step_id == len(pairs) - 1)
        def _():
            for n in range(nq):
                dq_ref[tq * n:tq * (n + 1), :] = dqt_s[n].T

        if n_x:
            pl.when(at_last)(finish)

    qmap = lambda b, g, s, qi_ref, ki_ref: (b * nq + qi_ref[s], g)
    kmap = lambda b, g, s, qi_ref, ki_ref: (b * nq + ki_ref[s], g)
    stat = pl.BlockSpec((None, hp, tq), lambda b, g, s, qi_ref, ki_ref: (g, 0, b * nq + qi_ref[s]))
    vw = hp * MLA_V
    return pl.pallas_call(
        body, name="mla_bwd_exchange" if n_x else "mla_bwd",
        grid_spec=pltpu.PrefetchScalarGridSpec(
            num_scalar_prefetch=2, grid=grid,
            in_specs=[pl.BlockSpec((tq, pw), qmap), pl.BlockSpec((tq, pw), kmap), pl.BlockSpec((tq, vw), kmap),
                      pl.BlockSpec((tq, vw), qmap), stat, stat] + [ANY] * n_x,
            out_specs=[pl.BlockSpec((seq, pw), lambda b, g, s, qi_ref, ki_ref: (b, g)),
                       pl.BlockSpec((tq, pw), kmap), pl.BlockSpec((tq, vw), kmap)] + [ANY] * n_x,
            scratch_shapes=[pltpu.VMEM((tq, pw), F32), pltpu.VMEM((tq, vw), F32), pltpu.VMEM((nq, pw, tq), F32)]
            + (_exchange_sems(n_x) if n_x else [])),
        out_shape=[SDS((T, QFW), F32), SDS((T, QFW), F32), SDS((T, MLA_H * MLA_V), F32)]
        + [SDS(a.shape, a.dtype) for a in exchange],
        compiler_params=_cp("arbitrary", "arbitrary", "arbitrary"))(qi_tab, ki_tab, qf, kf, v, do, lse, dsum, *exchange)


def _swa_bwd(sink, z, pos_col, pos_row, do, lse, dsum, nb, seq, layer, ride=None):
    T = z.shape[0]
    nu = _swa_blocks(seq)
    step = nu * BLK
    nstep = seq // step
    chains = [(u, h) for u in range(nu) for h in range(SWA_H)]

    def body(sink_ref, q_ref, kvc_ref, kvp_ref, pcc_ref, pcp_ref, pr_ref, do_ref, lse_ref, dsum_ref,
             dq_ref, dkv_ref, dsink_ref):
        b, m = pl.program_id(0), pl.program_id(1)

        @pl.when((b == 0) & (m == 0))
        def _():
            dsink_ref[...] = jnp.zeros(dsink_ref.shape, F32)

        @pl.when(m == 0)
        def _():
            dkv_ref[...] = jnp.zeros(dkv_ref.shape, F32)

        kb, vb, band, scores = _swa_scores(m, nu, q_ref, kvc_ref, kvp_ref, pcc_ref, pcp_ref, pr_ref)
        lane = lax.broadcasted_iota(jnp.int32, (1, LANES), 1)
        dsink = jnp.zeros((1, LANES), F32)
        group = lambda h: h // (SWA_H // SWA_KV)
        qs, ss, dobs, dps = [], [], [], []
        for u, h in chains:
            qh, s = scores(u, h)
            dob = do_ref[BLK * u:BLK * (u + 1), SWA_DH * h:SWA_DH * (h + 1)].astype(BF16)
            qs.append(qh)
            ss.append(s)
            dobs.append(dob)
            dps.append(_nt(band(vb, u, group(h)), dob))
        pbs, dss = [], []
        for (u, h), s, dp in zip(chains, ss, dps):
            cols = slice(BLK * u, BLK * (u + 1))
            lse, dsum = lse_ref[h:h + 1, cols], dsum_ref[h:h + 1, cols]
            p = jnp.exp2(s - lse)
            pbs.append(p.astype(BF16))
            dss.append((p * (dp - dsum) * SWA_SCALE).astype(BF16))
            dsk = jnp.sum(-jnp.exp2(sink_ref[layer, h] * LOG2E - lse) * dsum, axis=1, keepdims=True)
            dsink = dsink + jnp.where(lane == h, dsk, 0.0)
        dqs, dkv = [], [[[None, None], [None, None]] for _ in range(nu)]
        for i, (u, h) in enumerate(chains):
            g = group(h)
            dqs.append(_tn(band(kb, u, g), dss[i]))
            dk, dv = _nn(dss[i], qs[i]), _nn(pbs[i], dobs[i])
            dkv[u][g][0] = dk if dkv[u][g][0] is None else dkv[u][g][0] + dk
            dkv[u][g][1] = dv if dkv[u][g][1] is None else dkv[u][g][1] + dv
        for u in range(nu):
            dq_ref[BLK * u:BLK * (u + 1), :] = jnp.concatenate(dqs[SWA_H * u:SWA_H * (u + 1)], axis=0).T.astype(BF16)
        dsink_ref[...] += dsink
        upd = [jnp.concatenate([dkv[u][0][0], dkv[u][1][0], dkv[u][0][1], dkv[u][1][1]], axis=1) for u in range(nu)]
        base = pl.multiple_of(m * step, step)
        for u in range(nu):
            own = upd[u][BLK:] + upd[u + 1][:BLK] if u + 1 < nu else upd[u][BLK:]
            dkv_ref[pl.ds(base + BLK * u, BLK), :] += own

        @pl.when(m > 0)
        def _():
            dkv_ref[pl.ds(pl.multiple_of(m * step - BLK, BLK), BLK), :] += upd[0][:BLK]

    r_in, r_out, r_shape, r_scratch, r_args = _ride_args(ride)
    return pl.pallas_call(
        _riding(ride, body, 10, 3, (nb, nstep)), grid=(nb, nstep), name="swa_bwd_ride" if ride else "swa_bwd",
        in_specs=_swa_specs(nstep, nu) + [pl.BlockSpec((step, 512), lambda b, m: (b * nstep + m, 0))]
        + [pl.BlockSpec((SWA_H, step), lambda b, m: (0, b * nstep + m))] * 2 + r_in,
        out_specs=[pl.BlockSpec((step, 512), lambda b, m: (b * nstep + m, 0)),
                   pl.BlockSpec((seq, 2 * BLK), lambda b, m: (b, 0)),
                   pl.BlockSpec((1, LANES), lambda b, m: (0, 0))] + r_out,
        out_shape=[SDS((T, 512), BF16), SDS((T, 2 * BLK), F32), SDS((1, LANES), F32)] + r_shape,
        scratch_shapes=r_scratch,
        compiler_params=_cp("arbitrary", "arbitrary"))(sink, z, z, z, pos_col, pos_col, pos_row, do, lse, dsum,
                                                       *r_args)


def _bwd_prep(dq, dk, dv, z, gq, gkv, wq, wkv, tc, ts1, ts2, tm, layer):
    T = z.shape[0]

    def body(dq_ref, dk_ref, dv_ref, qd_ref, kvd_ref, gq_ref, gkv_ref, wq_ref, wkv_ref, c_ref, s1_ref, s2_ref,
             dqd_ref, dkvd_ref, dkr_ref, dwq_ref, dwkv_ref, dgq_ref, dgkv_ref, dqb_s, dkvb_s):
        @pl.when(pl.program_id(0) == 0)
        def _():
            for ref in (dwq_ref, dwkv_ref, dgq_ref, dgkv_ref):
                ref[...] = jnp.zeros(ref.shape, F32)

        c, s1, s2 = c_ref[...], s1_ref[...], s2_ref[...]
        lane = lax.broadcasted_iota(jnp.int32, (1, LANES), 1)
        rope_lanes = (lane >= MLA_NOPE) & (lane < MLA_QK)
        dkb = jnp.zeros((tm, LANES), F32)
        for h in range(MLA_H):
            sl = slice(LANES * h, LANES * (h + 1))
            dqb_s[:, sl] = _rope_t(dq_ref[:, sl], c, s1, s2).astype(BF16)
            dkh = dk_ref[:, sl]
            dkb = dkb + dkh
            dkvb_s[:, sl] = dkh.astype(BF16)
        dkvb_s[:, QFW:] = dv_ref[...].astype(BF16)
        dkr_ref[...] = _rope_t(jnp.where(rope_lanes, dkb, 0.0), c, s1, s2).astype(BF16)

        for (x_ref, g_ref, w_ref, d_s, dx_ref, dw_ref, dg_ref) in (
                (qd_ref, gq_ref, wq_ref, dqb_s, dqd_ref, dwq_ref, dgq_ref),
                (kvd_ref, gkv_ref, wkv_ref, dkvb_s, dkvd_ref, dwkv_ref, dgkv_ref)):
            xf, gf, db = x_ref[...].astype(F32), g_ref[...], d_s[...]
            r = _rstd(xf)
            n = xf * r
            dw_ref[...] += _tn((n * gf).astype(BF16), db)
            dx, dgr = _norm_bwd(_nt(db, w_ref[...]), n, r, gf)
            dx_ref[...] = dx.astype(BF16)
            dg_ref[...] += jnp.sum(dgr, axis=0, keepdims=True)

    return pl.pallas_call(
        body, grid=(T // tm,), name="bwd_prep",
        in_specs=[_row(tm, QFW), _row(tm, QFW), _row(tm, MLA_H * MLA_V),
                  _row(tm, QL, Z_QD // QL), _row(tm, KVL, Z_KVD // KVL),
                  _res((1, QL), layer), _res((1, KVL), layer), _res((QL, QFW), 0), _res((KVL, KVW), 0),
                  _row(tm, LANES), _row(tm, LANES), _row(tm, LANES)],
        out_specs=[_row(tm, QL), _row(tm, KVL), _row(tm, LANES),
                   _acc((QL, QFW)), _acc((KVL, KVW)), _acc((1, QL)), _acc((1, KVL))],
        out_shape=[SDS((T, QL), BF16), SDS((T, KVL), BF16), SDS((T, LANES), BF16),
                   SDS((QL, QFW), F32), SDS((KVL, KVW), F32), SDS((1, QL), F32), SDS((1, KVL), F32)],
        scratch_shapes=[pltpu.VMEM((tm, QFW), BF16), pltpu.VMEM((tm, KVW), BF16)],
        compiler_params=_cp("arbitrary"))(dq, dk, dv, z, z, gq, gkv, wq, wkv, tc, ts1, ts2)


def _bwd_in(pieces, x, g, dres, w, tm, layer):
    T = x.shape[0]
    grid = (T // tm,)
    widths = [pc.shape[1] for pc in pieces]
    assert sum(widths) == ZW
    n_p = len(pieces)

    def body(*refs):
        p_refs, (x_ref, g_ref, r_ref, w_ref, dx_ref, dz_ref, dg_ref) = refs[:n_p], refs[n_p:]

        @pl.when(pl.program_id(0) == 0)
        def _():
            dg_ref[...] = jnp.zeros(dg_ref.shape, F32)

        off = 0
        for ref, wd in zip(p_refs, widths):
            dz_ref[:, off:off + wd] = ref[...].astype(BF16)
            off += wd
        xf, gf = x_ref[...], g_ref[...]
        r = _rstd(xf)
        n = xf * r
        dx, dgr = _norm_bwd(_nt(dz_ref[...], w_ref[...]), n, r, gf)
        dx_ref[...] = r_ref[...] + dx
        dg_ref[...] += jnp.sum(dgr, axis=0, keepdims=True)

    return pl.pallas_call(
        body, grid=grid, name="bwd_in",
        in_specs=[_row(tm, wd) for wd in widths] + [_row(tm, D), _res((1, D), layer), _row(tm, D),
                                                    _res((D, ZW), 0)],
        out_specs=[_row(tm, D), _row(tm, ZW), _acc((1, D))],
        out_shape=[SDS((T, D), F32), SDS((T, ZW), BF16), SDS((1, D), F32)],
        compiler_params=_cp("arbitrary"))(*pieces, x, g, dres, w)


def _wgrad_in(hb, dzb, tm, ride=None):
    T = hb.shape[0]
    half = ZW // 2
    grid = (2, T // tm)

    def body(h_ref, dz_ref, dw_ref):
        @pl.when(pl.program_id(1) == 0)
        def _():
            dw_ref[...] = jnp.zeros(dw_ref.shape, F32)

        dw_ref[...] += _tn(h_ref[...], dz_ref[...])

    r_in, r_out, r_shape, r_scratch, r_args = _ride_args(ride)
    out = pl.pallas_call(
        _riding(ride, body, 2, 1, grid), grid=grid, name="wgrad_in_ride" if ride else "wgrad_in",
        in_specs=[pl.BlockSpec((tm, D), lambda j, t: (t, 0)), pl.BlockSpec((tm, half), lambda j, t: (t, j))] + r_in,
        out_specs=[pl.BlockSpec((D, half), lambda j, t: (0, j))] + r_out,
        out_shape=[SDS((D, ZW), F32)] + r_shape, scratch_shapes=r_scratch,
        compiler_params=_cp("arbitrary", "arbitrary"))(hb, dzb, *r_args)
    return out


IN_PIECES = ((0, 512, Z_AQ), (512, 128, Z_AK), (640, 128, Z_AV), (768, 512, Z_AG), (1280, 256, Z_QD),
             (1536, 128, Z_KVD), (1664, MLA_ROPE, Z_KR + MLA_NOPE), (1696, 512, Z_BG), (2208, 1024, Z_MA),
             (3232, 1024, Z_MB))
WIDE_W = IN_W // N_DEV


def _column_runs():
    runs = []
    for start, width, kstart in IN_PIECES:
        col = start
        while col < start + width:
            dev = col // WIDE_W
            stop = min(start + width, (dev + 1) * WIDE_W)
            runs.append((dev, col - dev * WIDE_W, stop - col, kstart + col - start))
            col = stop
    return runs


def _win_layout(blocks, tm):
    runs = _column_runs()

    def body(g_ref, o_ref):
        o_ref[:, Z_KR:Z_KR + LANES] = jnp.zeros((tm, LANES), o_ref.dtype)
        for dev, lo, n, k in runs:
            o_ref[:, k:k + n] = g_ref[dev, :, lo:lo + n]

    return pl.pallas_call(
        body, grid=(D // tm,), name="win_layout",
        in_specs=[pl.BlockSpec((N_DEV, None, tm, WIDE_W), lambda i: (0, 0, i, 0))],
        out_specs=pl.BlockSpec((None, tm, ZW), lambda i: (0, i, 0)),
        out_shape=SDS((1, D, ZW), blocks.dtype),
        compiler_params=_cp("parallel"))(blocks)


def _win_grad_layout(dw, tm):
    runs = _column_runs()

    def body(g_ref, o_ref):
        for dev, lo, n, k in runs:
            o_ref[dev, :, lo:lo + n] = g_ref[:, k:k + n]

    return pl.pallas_call(
        body, grid=(D // tm,), name="win_grad_layout",
        in_specs=[_row(tm, ZW)],
        out_specs=pl.BlockSpec((N_DEV, None, tm, WIDE_W), lambda i: (0, 0, i, 0)),
        out_shape=SDS((N_DEV, 1, D, WIDE_W), F32),
        compiler_params=_cp("parallel"))(dw)


def _wuq_to_kernel(w):
    w = w.reshape(w.shape[:-1] + (MLA_H, MLA_QK))
    w = jnp.pad(w, [(0, 0)] * (w.ndim - 1) + [(0, LANES - MLA_QK)])
    return w.reshape(w.shape[:-2] + (QFW,))


def _wuq_from_kernel(g):
    g = g.reshape(g.shape[:-1] + (MLA_H, LANES))[..., :MLA_QK]
    return g.reshape(g.shape[:-2] + (MLA_H * MLA_QK,))


def _wukv_to_kernel(w):
    w = w.reshape(w.shape[:-1] + (MLA_H, MLA_NOPE + MLA_V))
    k = jnp.pad(w[..., :MLA_NOPE], [(0, 0)] * (w.ndim - 1) + [(0, LANES - MLA_NOPE)])
    v = w[..., MLA_NOPE:]
    return jnp.concatenate([k.reshape(k.shape[:-2] + (QFW,)), v.reshape(v.shape[:-2] + (MLA_H * MLA_V,))], axis=-1)


def _wukv_from_kernel(g):
    k = g[..., :QFW].reshape(g.shape[:-1] + (MLA_H, LANES))[..., :MLA_NOPE]
    v = g[..., QFW:].reshape(g.shape[:-1] + (MLA_H, MLA_V))
    kv = jnp.concatenate([k, v], axis=-1)
    return kv.reshape(kv.shape[:-2] + (MLA_H * (MLA_NOPE + MLA_V),))


def _rope_tables(pos):
    half = MLA_ROPE // 2
    inv = 10000.0 ** (-jnp.arange(0, MLA_ROPE, 2, dtype=F32) / MLA_ROPE)
    ang = pos.astype(F32)[:, None] * inv
    cos, sin = jnp.cos(ang), jnp.sin(ang)
    one = jnp.ones((pos.shape[0], MLA_NOPE), F32)
    zero = lambda n: jnp.zeros((pos.shape[0], n), F32)
    tc = jnp.concatenate([one, cos, cos, one[:, :LANES - MLA_QK]], axis=1)
    ts1 = jnp.concatenate([zero(MLA_NOPE + half), sin, zero(LANES - MLA_QK)], axis=1)
    ts2 = jnp.concatenate([zero(MLA_NOPE), -sin, zero(LANES - MLA_NOPE - half)], axis=1)
    return tc, ts1, ts2


def _local_step(x, p, positions, loss_target, small, wts, plan=None):
    nb, seq, _ = x.shape
    T = nb * seq
    tm = min(512, T)
    tl = min(1024, T)
    tq = min(512, seq)
    xf = x.reshape(T, D)
    pos = positions.reshape(T)
    posf = pos.astype(F32)
    swa_step = _swa_blocks(seq) * BLK
    pos_col, pos_row = posf.reshape(T, 1), posf.reshape(T // swa_step, 1, swa_step)
    tc, ts1, ts2 = _rope_tables(pos)

    wts, sm = list(wts), small
    pl_in = p.reshape(DEPTH, T, PLE)
    saved = []
    for i in range(DEPTH):
        riding = plan is not None and i == 0
        w = wts[i]
        z, hb, *got = _fwd_in(xf, sm["g_mix"], w["w_in"], tm, i,
                              ride=_Ride("gather", plan["behind_fwd_in"]) if riding else None)
        if riding:
            w = wts[0] = dict(w, **plan["row_weights"](got[0]))
        oa, lse_a = _swa_fwd(sm["sink"], z, pos_col, pos_row, nb, seq, i)
        qf, kf, v = _fwd_prep(z, sm["g_q"], sm["g_kv"], w["w_uq"], w["w_ukv"], tc, ts1, ts2, tl, i)
        ob, lse_b, *got = _mla_fwd(qf, kf, v, nb, seq, tq, gather=plan["behind_mla_fwd"] if riding else ())
        if riding:
            wts.append(dict(w_in=plan["w_in"](got[0]), **plan["row_weights"](got[1])))
        x1 = _fwd_merge(xf, oa, ob, z, w["w_br_a"], w["w_br_b"], w["w_out"], tm, i)
        saved.append(dict(x=xf, z=z, hb=hb, oa=oa, lse_a=lse_a, qf=qf, kf=kf, v=v, ob=ob, lse_b=lse_b, x1=x1))
        if i < DEPTH - 1:
            xf, saved[i]["pg"], saved[i]["pp"] = _fwd_ple(x1, pl_in, sm["g_ple"], w["w_ple_gate"], w["w_ple_proj"],
                                                          tl, i)

    last = _ple_loss(x1, pl_in, sm["g_ple"], w["w_ple_gate"], w["w_ple_proj"], small["g_final"],
                     loss_target.reshape(T, D), tm, DEPTH - 1)
    dg_final, loss = last[4], last[5]

    grads = [None] * DEPTH
    exchanged = {}
    for i in reversed(range(DEPTH)):
        riding = plan is not None and i == 0
        sv, w = saved[i], wts[i]
        pay = plan["payload"](grads[1]) if riding else []
        if i == DEPTH - 1:
            (dx1, dwpg, dwpp, dg_ple), got = last[:4], []
        else:
            dx1, dwpg, dwpp, dg_ple, *got = _bwd_ple(dx, sv["x1"], sv["pg"], sv["pp"], pl_in, sm["g_ple"],
                                                     w["w_ple_gate"], tm, i,
                                                     ride=_Ride("swap", pay) if riding else None)
        doa, dob, dag, dbg, dma, dmb, dsum_a, dsum_b, dwa, dwb, dwo = _bwd_merge(
            dx1, sv["oa"], sv["ob"], sv["z"], w["w_br_a"], w["w_br_b"], w["w_out"], tm, i)
        stats = (MLA_H // MLA_BWD_HEADS, MLA_BWD_HEADS, T)
        dq_b, dk_b, dv_b, *exchanged["layer1"] = _mla_bwd(
            sv["qf"], sv["kf"], sv["v"], dob, sv["lse_b"].reshape(stats), dsum_b.reshape(stats), nb, seq, tq,
            exchange=plan["add"](pay, got) if riding else ())
        dqd, dkvd, dkr, dwq, dwkv, dgq, dgkv = _bwd_prep(dq_b, dk_b, dv_b, sv["z"], sm["g_q"], sm["g_kv"],
                                                         w["w_uq"], w["w_ukv"], tc, ts1, ts2, tl, i)
        g = dict(w_uq=_wuq_from_kernel(dwq), w_ukv=_wukv_from_kernel(dwkv), w_br_a=dwa, w_br_b=dwb, w_out=dwo,
                 w_ple_gate=dwpg, w_ple_proj=dwpp)
        pay = [plan["rows_payload"](g)] if riding else []
        dq_a, dkv_a, dsink, *got = _swa_bwd(sm["sink"], sv["z"], pos_col, pos_row, doa, sv["lse_a"], dsum_a, nb, seq,
                                            i, ride=_Ride("swap", pay) if riding else None)
        dx, dzb, dg_mix = _bwd_in([dma, dmb, dq_a, dag, dbg, dqd, dkv_a, dkvd, dkr], sv["x"], sm["g_mix"], dx1,
                                  w["w_in"], tm, i)
        dwin, *exchanged["rows0"] = _wgrad_in(sv["hb"], dzb, tm,
                                              ride=_Ride("exchange", plan["add"](pay, got)) if riding else None)
        g.update(g_mix=dg_mix[0], w_in=dwin, sink=dsink[0, :SWA_H], g_q=dgq[0], g_kv=dgkv[0], g_ple=dg_ple[0])
        grads[i] = g
    return loss, dx.reshape(nb, seq, D), grads, dg_final[0], exchanged


def _row_weights(rows):
    blocks = _unpack_rows(rows)
    out = {n: _join(n, blocks[n]) for n, _ in ROWS_PIECES}
    out.update(w_uq=_wuq_to_kernel(out["w_uq"]), w_ukv=_wukv_to_kernel(out["w_ukv"]))
    return out


def _small_params(g_mix, sink, g_q, g_kv, g_ple, g_final):
    return dict(g_mix=g_mix[:, None], sink=sink, g_q=g_q[:, None], g_kv=g_kv[:, None], g_ple=g_ple[:, None],
                g_final=g_final[None])


UQ_W = MLA_H * MLA_QK // N_DEV
ROWS_PIECES = (("w_uq", QL), ("w_ukv", KVL), ("w_br_a", 512), ("w_br_b", 512), ("w_out", D), ("w_ple_gate", D),
               ("w_ple_proj", PLE))
SMALL = (("g_mix", (DEPTH, D)), ("sink", (DEPTH, SWA_H)), ("g_q", (DEPTH, QL)), ("g_kv", (DEPTH, KVL)),
         ("g_ple", (DEPTH, D)), ("g_final", (D,)))
VEC_ROWS = 48
ROWS_N = sum(r for _, r in ROWS_PIECES)
WIDE_TILE, ROWS_TILE = 512, ROWS_N // 2


def _to_rows(name, a):
    if name == "w_uq":
        a = jnp.pad(a, [(0, 0)] * (a.ndim - 1) + [(0, LANES - UQ_W)])
    return a.reshape(a.shape[:-2] + (-1, LANES))


def _from_rows(name, r):
    if name in ("w_out", "w_ple_gate"):
        return r.reshape(r.shape[:-2] + (D // N_DEV, D))
    return r[..., :UQ_W] if name == "w_uq" else r


def _pack_rows(blocks):
    return jnp.concatenate([_to_rows(n, blocks[n]) for n, _ in ROWS_PIECES], axis=-2)


def _unpack_rows(rows):
    blocks, off = {}, 0
    for n, r in ROWS_PIECES:
        blocks[n] = _from_rows(n, rows[..., off:off + r, :])
        off += r
    return blocks


def _pack_vec(vectors, loss=None):
    parts = [vectors[n].reshape(-1) for n, _ in SMALL] + ([] if loss is None else [loss.reshape(1)])
    vec = jnp.concatenate(parts)
    return jnp.pad(vec, (0, VEC_ROWS * LANES - vec.shape[0])).reshape(1, VEC_ROWS, LANES)


def _unpack_vec(vec):
    vec = vec.reshape(-1)
    vectors, off = {}, 0
    for n, shp in SMALL:
        size = 1
        for s in shp:
            size *= s
        vectors[n] = vec[off:off + size].reshape(shp)
        off += size
    return vectors, vec[off]


def _join(name, blocks):
    if name in ("w_out", "w_ple_gate"):
        return jnp.moveaxis(blocks, 0, 1).reshape(blocks.shape[1], -1, blocks.shape[-1])
    return jnp.moveaxis(blocks, 0, 2).reshape(blocks.shape[1], blocks.shape[2], -1)


def _split(name, full):
    if name in ("w_out", "w_ple_gate"):
        return jnp.moveaxis(full.reshape(full.shape[0], N_DEV, -1, full.shape[-1]), 1, 0)
    return jnp.moveaxis(full.reshape(full.shape[0], full.shape[1], N_DEV, -1), 2, 0)


MESH_ID = pl.DeviceIdType.MESH
ANY = pl.BlockSpec(memory_space=pl.ANY)


def _place():
    return lax.axis_index("x"), lax.axis_index("y"), lax.axis_index("c")


def _all_gather(blocks):
    n = len(blocks)

    def body(*refs):
        start, forward, finish = _gather_phases(refs[:n], refs[n:2 * n], *refs[2 * n:])
        start()
        forward()
        finish()

    return pl.pallas_call(
        body, name="all_gather_weights", out_shape=_gather_out(blocks),
        in_specs=[ANY] * n, out_specs=[ANY] * n, scratch_shapes=_gather_sems(n))(*blocks)


def _gather_out(blocks):
    return [SDS((N_DEV,) + b.shape, b.dtype) for b in blocks]


def _gather_sems(n):
    return [pltpu.SemaphoreType.DMA((7 * n,)), pltpu.SemaphoreType.DMA((7 * n,)), pltpu.SemaphoreType.DMA((n,))]


def _gather_phases(x_refs, out_refs, send_sems, recv_sems, local_sems):
    n = len(x_refs)
    x, y, c = _place()
    me, sibling = (x, y, c), (x, y, 1 - c)
    chips = [(1 - x, y), (x, 1 - y), (1 - x, 1 - y)]

    def slot(a, px, py, pc):
        return out_refs[a].at[4 * px + 2 * py + pc]

    def copy(a, k, blk, to, src=None):
        return pltpu.make_async_remote_copy(
            src_ref=slot(a, *blk) if src is None else src, dst_ref=slot(a, *blk),
            send_sem=send_sems.at[7 * a + k], recv_sem=recv_sems.at[7 * a + k], device_id=to,
            device_id_type=MESH_ID)

    def mine():
        return [pltpu.make_async_copy(x_refs[a], slot(a, *me), local_sems.at[a]) for a in range(n)]

    def first():
        out = []
        for a in range(n):
            out += [copy(a, 0, me, sibling, src=x_refs[a])]
            out += [copy(a, 1 + j, me, (*chip, c), src=x_refs[a]) for j, chip in enumerate(chips)]
        return out

    def passed():
        return [copy(a, 4 + j, (*chip, c), sibling) for j, chip in enumerate(chips) for a in range(n)]

    def start():
        for cp in mine() + first():
            cp.start()

    def forward():
        for j, chip in enumerate(chips):
            for a in range(n):
                copy(a, 1 + j, (*chip, c), me).wait_recv()
                copy(a, 4 + j, (*chip, c), sibling).start()

    def finish():
        for a in range(n):
            copy(a, 0, sibling, me).wait_recv()
            for j, chip in enumerate(chips):
                copy(a, 4 + j, (*chip, 1 - c), me).wait_recv()
        for cp in first() + passed():
            cp.wait_send()
        for cp in mine():
            cp.wait()

    return start, forward, finish


def _swap_sibling(arrs):
    n = len(arrs)

    def body(*refs):
        start, finish = _swap_phases(refs[:n], refs[n:2 * n], *refs[2 * n:])
        start()
        finish()

    return pl.pallas_call(
        body, name="swap_sibling", out_shape=[SDS((a.shape[0],) + a.shape[2:], a.dtype) for a in arrs],
        in_specs=[ANY] * n, out_specs=[ANY] * n, scratch_shapes=_swap_sems(n))(*arrs)


def _swap_sems(n):
    return [pltpu.SemaphoreType.DMA((n,)), pltpu.SemaphoreType.DMA((n,))]


def _swap_phases(a_refs, out_refs, send_sems, recv_sems):
    x, y, c = _place()

    def copies():
        return [pltpu.make_async_remote_copy(
            src_ref=a_refs[a].at[:, 1 - c], dst_ref=out_refs[a], send_sem=send_sems.at[a], recv_sem=recv_sems.at[a],
            device_id=(x, y, 1 - c), device_id_type=MESH_ID) for a in range(len(a_refs))]

    def start():
        for cp in copies():
            cp.start()

    def finish():
        for cp in copies():
            cp.wait()

    return start, finish


def _exchange_chips(arrs):
    n = len(arrs)

    def body(*refs):
        start, finish = _exchange_phases(refs[:n], refs[n:2 * n], *refs[2 * n:])
        start()
        finish()

    return pl.pallas_call(
        body, name="exchange_chips", out_shape=[SDS(a.shape, a.dtype) for a in arrs],
        in_specs=[ANY] * n, out_specs=[ANY] * n, scratch_shapes=_exchange_sems(n))(*arrs)


def _exchange_sems(n):
    return [pltpu.SemaphoreType.DMA((3 * n,)), pltpu.SemaphoreType.DMA((3 * n,)), pltpu.SemaphoreType.DMA((n,))]


def _exchange_phases(p_refs, out_refs, send_sems, recv_sems, local_sems):
    n = len(p_refs)
    x, y, c = _place()
    mine = 2 * x + y
    peers = [(1 - x, y), (x, 1 - y), (1 - x, 1 - y)]

    def local():
        return [pltpu.make_async_copy(p_refs[a].at[mine], out_refs[a].at[mine], local_sems.at[a]) for a in range(n)]

    def copy(a, j, src_chip, dst_chip):
        px, py = peers[j]
        return pltpu.make_async_remote_copy(
            src_ref=p_refs[a].at[src_chip], dst_ref=out_refs[a].at[dst_chip], send_sem=send_sems.at[3 * a + j],
            recv_sem=recv_sems.at[3 * a + j], device_id=(px, py, c), device_id_type=MESH_ID)

    def sends():
        return [copy(a, j, 2 * px + py, mine) for a in range(n) for j, (px, py) in enumerate(peers)]

    def start():
        for cp in local() + sends():
            cp.start()

    def finish():
        for a in range(n):
            for j, (px, py) in enumerate(peers):
                copy(a, j, mine, 2 * px + py).wait_recv()
        for cp in sends():
            cp.wait_send()
        for cp in local():
            cp.wait()

    return start, finish


def _add_mine(g, recv, core, tile, dtype):
    _, _, lead, rows, width = g.shape

    def body(c_ref, g_ref, r_ref, o_ref):
        o_ref[...] = (g_ref[...] + r_ref[...]).astype(dtype)

    spec = pl.BlockSpec((None, None, tile, width), lambda k, l, i, c_ref: (k, l, i, 0))
    return pl.pallas_call(
        body, name="add_sibling", out_shape=SDS(recv.shape, dtype),
        grid_spec=pltpu.PrefetchScalarGridSpec(
            num_scalar_prefetch=1, grid=(g.shape[0], lead, rows // tile),
            in_specs=[pl.BlockSpec((None, None, None, tile, width), lambda k, l, i, c_ref: (k, c_ref[0], l, i, 0)),
                      spec],
            out_specs=spec),
        compiler_params=_cp("parallel", "parallel", "parallel"))(core, g, recv)


def _sum_adamw(parts, w, m, v, tile):
    lead, rows, width = w.shape
    last = rows // tile - 1

    def body(*refs):
        p_refs, (w_ref, m_ref, v_ref, g_ref, d_ref, nm_ref, nv_ref) = refs[:lead], refs[lead:]
        for layer in range(lead):
            @pl.when(pl.program_id(0) == layer)
            def _(p_ref=p_refs[layer]):
                g = ((p_ref[0].astype(F32) + p_ref[1].astype(F32)) + p_ref[2].astype(F32)) + p_ref[3].astype(F32)
                nm = ADAM_B1 * m_ref[...] + (1.0 - ADAM_B1) * g
                nv = ADAM_B2 * v_ref[...] + (1.0 - ADAM_B2) * jnp.square(g)
                m_hat = nm / (1.0 - ADAM_B1 ** ADAM_STEP)
                v_hat = nv / (1.0 - ADAM_B2 ** ADAM_STEP)
                g_ref[...] = g
                nm_ref[...] = nm
                nv_ref[...] = nv
                d_ref[...] = -ADAM_LR * (m_hat / (jnp.sqrt(v_hat) + ADAM_EPS) + ADAM_WD * w_ref[...])

    pspec = lambda layer: pl.BlockSpec(
        (4, None, tile, width),
        lambda l, i: (0, 0, jnp.where(l == layer, i, jnp.where(l > layer, last, 0)), 0))
    spec = pl.BlockSpec((None, tile, width), lambda l, i: (l, i, 0))
    return pl.pallas_call(
        body, grid=(lead, rows // tile), name="sum_adamw",
        in_specs=[pspec(layer) for layer in range(lead)] + [spec, spec, spec],
        out_specs=[spec] * 4, out_shape=[SDS((lead, rows, width), F32)] * 4,
        compiler_params=_cp("arbitrary", "arbitrary"))(*parts, w, m, v)


def kernel(x, p, positions, g_mix, w_in, sink, g_q, w_uq, g_kv, w_ukv, w_br_a, w_br_b, w_out, g_ple, w_ple_gate, w_ple_proj, g_final, loss_target, m_g_mix, m_w_in, m_sink, m_g_q, m_w_uq, m_g_kv, m_w_ukv, m_w_br_a, m_w_br_b, m_w_out, m_g_ple, m_w_ple_gate, m_w_ple_proj, m_g_final, v_g_mix, v_w_in, v_sink, v_g_q, v_w_uq, v_g_kv, v_w_ukv, v_w_br_a, v_w_br_b, v_w_out, v_g_ple, v_w_ple_gate, v_w_ple_proj, v_g_final):
    weights = dict(g_mix=g_mix, w_in=w_in, sink=sink, g_q=g_q, w_uq=w_uq, g_kv=g_kv, w_ukv=w_ukv, w_br_a=w_br_a,
                   w_br_b=w_br_b, w_out=w_out, g_ple=g_ple, w_ple_gate=w_ple_gate, w_ple_proj=w_ple_proj,
                   g_final=g_final)
    mom1 = dict(g_mix=m_g_mix, w_in=m_w_in, sink=m_sink, g_q=m_g_q, w_uq=m_w_uq, g_kv=m_g_kv, w_ukv=m_w_ukv,
                w_br_a=m_w_br_a, w_br_b=m_w_br_b, w_out=m_w_out, g_ple=m_g_ple, w_ple_gate=m_w_ple_gate,
                w_ple_proj=m_w_ple_proj, g_final=m_g_final)
    mom2 = dict(g_mix=v_g_mix, w_in=v_w_in, sink=v_sink, g_q=v_g_q, w_uq=v_w_uq, g_kv=v_g_kv, w_ukv=v_w_ukv,
                w_br_a=v_w_br_a, w_br_b=v_w_br_b, w_out=v_w_out, g_ple=v_g_ple, w_ple_gate=v_w_ple_gate,
                w_ple_proj=v_w_ple_proj, g_final=v_g_final)
    assert DEPTH == 2
    wide = lambda d: d["w_in"]
    rows = lambda d: _pack_rows(d)
    core = lax.axis_index("c").astype(jnp.int32).reshape(1)

    w16 = [wide(weights).astype(BF16), rows(weights).astype(BF16)]
    wts0 = dict(w_in=_win_layout(_all_gather([w16[0][:1]])[0], 256))
    small = _small_params(g_mix, sink, g_q, g_kv, g_ple, g_final)

    def wide_payload(g):
        return _win_grad_layout(g["w_in"], 256).reshape(N_DEV // 2, 2, 1, D, WIDE_W)

    def rows_payload(g):
        return _pack_rows({n: _split(n, g[n][None]) for n, _ in ROWS_PIECES}).reshape(N_DEV // 2, 2, 1, ROWS_N, LANES)

    def add(pay, got):
        tiles = {D: (WIDE_TILE, BF16), ROWS_N: (ROWS_TILE, BF16), VEC_ROWS: (VEC_ROWS, F32)}
        return [_add_mine(a, b, core, *tiles[a.shape[-2]]) for a, b in zip(pay, got)]

    plan = dict(behind_fwd_in=[w16[1][:1]], behind_mla_fwd=[a[1:] for a in w16],
                w_in=lambda blocks: _win_layout(blocks, 256),
                row_weights=_row_weights, payload=lambda g: [wide_payload(g), rows_payload(g)],
                rows_payload=rows_payload, add=add)
    loss, grad_x, grads, dg_final, rode = _local_step(x, p, positions, loss_target, small, [wts0], plan)

    vectors = {n: jnp.stack([grads[i][n] for i in range(DEPTH)]) for n, _ in SMALL[:-1]}
    vectors["g_final"] = dg_final
    pay = [wide_payload(grads[0]),
           jnp.broadcast_to(_pack_vec(vectors, loss[0, 0]), (N_DEV // 2, 2, 1, VEC_ROWS, LANES))]
    parts_wide0, parts_vec = _exchange_chips(add(pay, _swap_sibling(pay)))
    out_wide = _sum_adamw([parts_wide0, rode["layer1"][0]], wide(weights), wide(mom1), wide(mom2), WIDE_TILE)
    out_rows = _sum_adamw([rode["rows0"][0], rode["layer1"][1]], rows(weights), rows(mom1), rows(mom2), ROWS_TILE)
    out_vec = _sum_adamw([parts_vec], _pack_vec(weights), _pack_vec(mom1), _pack_vec(mom2), VEC_ROWS)

    outs = []
    for ow, orow, ovec in zip(out_wide, out_rows, out_vec):
        named = _unpack_rows(orow)
        named.update(_unpack_vec(ovec)[0])
        named["w_in"] = ow
        outs += [named[n] for n in weights]
    loss = _unpack_vec(out_vec[0])[1]
    return (loss, grad_x, *outs)
```

```python
import functools

import jax
import jax.numpy as jnp
from jax import lax
from jax.experimental import pallas as pl
from jax.experimental.pallas import tpu as pltpu

F32, BF16 = jnp.float32, jnp.bfloat16
SDS = jax.ShapeDtypeStruct

D = 1024
DEPTH = 2
PLE = 256
BLK = 128
EPS = 1e-6
NEG = -1e30
SWA_H, SWA_KV, SWA_DH = 8, 2, 64
MLA_H, MLA_NOPE, MLA_ROPE, MLA_V = 8, 64, 32, 64
MLA_QK = MLA_NOPE + MLA_ROPE
QL, KVL = 256, 128
IN_W = 4256
N_DEV = 8

V7X_VMEM_BYTES = 64 * 1024 * 1024
LANES = 128
VMEM_LIMIT = V7X_VMEM_BYTES * 7 // 8

ZW = 4352
Z_MA, Z_MB, Z_AQ, Z_AG, Z_BG, Z_QD, Z_AK, Z_AV, Z_KVD, Z_KR = 0, 1024, 2048, 2560, 3072, 3584, 3840, 3968, 4096, 4224
QFW = MLA_H * LANES
KVW = QFW + MLA_H * MLA_V
MLA_SCALE = MLA_QK ** -0.5
LOG2E = 1.4426950408889634
MLA_FWD_HEADS, MLA_BWD_HEADS = 8, 8
SWA_SCALE = SWA_DH ** -0.5
ROLL_UP, ROLL_DOWN = MLA_ROPE // 2, LANES - MLA_ROPE // 2

ADAM_LR, ADAM_B1, ADAM_B2, ADAM_EPS, ADAM_WD, ADAM_STEP = 0.001, 0.9, 0.999, 1e-08, 0.01, 10


def _cp(*sem):
    return pltpu.CompilerParams(dimension_semantics=sem, vmem_limit_bytes=VMEM_LIMIT)


def _row(tm, w, col=0):
    return pl.BlockSpec((tm, w), lambda i: (i, col))


def _res(shape, layer=None):
    if layer is None:
        return pl.BlockSpec(shape, lambda *_: (0,) * len(shape), pipeline_mode=pl.Buffered(1))
    return pl.BlockSpec((None,) + shape, lambda *_: (layer,) + (0,) * len(shape), pipeline_mode=pl.Buffered(1))


def _acc(shape):
    return pl.BlockSpec(shape, lambda *_: (0,) * len(shape))


def _rstd(xf):
    return lax.rsqrt(jnp.mean(xf * xf, axis=-1, keepdims=True) + EPS)


def _norm_bwd(dh, n, r, g):
    dn = dh * g
    return r * (dn - n * jnp.mean(dn * n, axis=-1, keepdims=True)), dh * n


def _nt(a, b):
    return lax.dot_general(a, b, (((1,), (1,)), ((), ())), preferred_element_type=F32)


def _tn(a, b):
    return lax.dot_general(a, b, (((0,), (0,)), ((), ())), preferred_element_type=F32)


def _nn(a, b):
    return jnp.dot(a, b, preferred_element_type=F32)


def _sig(x):
    return jax.nn.sigmoid(x)


def _rope(t, c, s1, s2):
    return t * c + pltpu.roll(t, ROLL_UP, 1) * s1 + pltpu.roll(t, ROLL_DOWN, 1) * s2


def _rope_t(d, c, s1, s2):
    return d * c + pltpu.roll(d * s1, ROLL_DOWN, 1) + pltpu.roll(d * s2, ROLL_UP, 1)


def _fwd_in(x, g, w, tm, layer, ride=None):
    T = x.shape[0]
    grid = (T // tm,)

    def body(x_ref, g_ref, w_ref, z_ref, h_ref):
        xf = x_ref[...]
        h = ((xf * _rstd(xf)) * g_ref[...]).astype(BF16)
        h_ref[...] = h
        z_ref[...] = _nn(h, w_ref[...]).astype(BF16)

    r_in, r_out, r_shape, r_scratch, r_args = _ride_args(ride)
    return pl.pallas_call(
        _riding(ride, body, 3, 2, grid), grid=grid, name="fwd_in_ride" if ride else "fwd_in",
        in_specs=[_row(tm, D), _res((1, D), layer), _res((D, ZW), 0)] + r_in,
        out_specs=[_row(tm, ZW), _row(tm, D)] + r_out,
        out_shape=[SDS((T, ZW), BF16), SDS((T, D), BF16)] + r_shape, scratch_shapes=r_scratch,
        compiler_params=_cp("arbitrary"))(x, g, w, *r_args)


def _fwd_prep(z, gq, gkv, wq, wkv, tc, ts1, ts2, tm, layer):
    T = z.shape[0]

    def body(qd_ref, kvd_ref, kr_ref, gq_ref, gkv_ref, wq_ref, wkv_ref, c_ref, s1_ref, s2_ref, q_ref, k_ref, v_ref):
        qd, kvd = qd_ref[...].astype(F32), kvd_ref[...].astype(F32)
        hq = ((qd * _rstd(qd)) * gq_ref[...]).astype(BF16)
        hkv = ((kvd * _rstd(kvd)) * gkv_ref[...]).astype(BF16)
        qf = _nn(hq, wq_ref[...])
        kvf = _nn(hkv, wkv_ref[...])
        c, s1, s2 = c_ref[...], s1_ref[...], s2_ref[...]
        krb = _rope(kr_ref[...].astype(F32), c, s1, s2)
        for h in range(MLA_H):
            sl = slice(LANES * h, LANES * (h + 1))
            q_ref[:, sl] = _rope(qf[:, sl], c, s1, s2).astype(BF16)
            k_ref[:, sl] = (kvf[:, sl] + krb).astype(BF16)
        v_ref[...] = kvf[:, QFW:].astype(BF16)

    return pl.pallas_call(
        body, grid=(T // tm,), name="fwd_prep",
        in_specs=[_row(tm, QL, Z_QD // QL), _row(tm, KVL, Z_KVD // KVL), _row(tm, LANES, Z_KR // LANES),
                  _res((1, QL), layer), _res((1, KVL), layer), _res((QL, QFW), 0), _res((KVL, KVW), 0),
                  _row(tm, LANES), _row(tm, LANES), _row(tm, LANES)],
        out_specs=[_row(tm, QFW), _row(tm, QFW), _row(tm, MLA_H * MLA_V)],
        out_shape=[SDS((T, QFW), BF16), SDS((T, QFW), BF16), SDS((T, MLA_H * MLA_V), BF16)],
        compiler_params=_cp("parallel"))(z, z, z, gq, gkv, wq, wkv, tc, ts1, ts2)


def _grid_ends(grid):
    ids = [pl.program_id(a) for a in range(len(grid))]
    inner_first = functools.reduce(jnp.logical_and, [i == 0 for i in ids[1:]], True)
    last = functools.reduce(jnp.logical_and, [i == g - 1 for i, g in zip(ids, grid)])
    return (ids[0] == 0) & inner_first, (ids[0] == 3 * grid[0] // 4) & inner_first, last


class _Ride:
    def __init__(self, kind, arrays):
        self.kind, self.arrays, self.n = kind, list(arrays), len(arrays)

    def out_shape(self):
        if self.kind == "gather":
            return _gather_out(self.arrays)
        if self.kind == "swap":
            return [SDS((a.shape[0],) + a.shape[2:], a.dtype) for a in self.arrays]
        return [SDS(a.shape, a.dtype) for a in self.arrays]

    def sems(self):
        if self.kind == "gather":
            return _gather_sems(self.n)
        if self.kind == "swap":
            return _swap_sems(self.n)
        return _exchange_sems(self.n)

    def phases(self, in_refs, out_refs, *sems):
        if self.kind == "gather":
            return _gather_phases(in_refs, out_refs, *sems)
        start, finish = (_swap_phases if self.kind == "swap" else _exchange_phases)(in_refs, out_refs, *sems)
        return start, None, finish


def _riding(ride, body, n_in, n_out, grid):
    if ride is None:
        return body
    n, n_sems = ride.n, len(ride.sems())

    def wrapped(*refs):
        ins, r_in = refs[:n_in], refs[n_in:n_in + n]
        outs, r_out = refs[n_in + n:n_in + n + n_out], refs[n_in + n + n_out:n_in + 2 * n + n_out]
        rest = refs[n_in + 2 * n + n_out:]
        scratch, sems = rest[:len(rest) - n_sems], rest[len(rest) - n_sems:]
        start, middle, finish = ride.phases(r_in, r_out, *sems)
        at_first, at_middle, at_last = _grid_ends(grid)
        pl.when(at_first)(start)
        if middle is not None:
            pl.when(at_middle)(middle)
        body(*ins, *outs, *scratch)
        pl.when(at_last)(finish)

    return wrapped


def _ride_args(ride):
    if ride is None:
        return [], [], [], [], []
    return [ANY] * ride.n, [ANY] * ride.n, ride.out_shape(), ride.sems(), ride.arrays


def _mla_fwd(qf, kf, v, nb, seq, tq, gather=()):
    T = qf.shape[0]
    nq = seq // tq
    hp = MLA_FWD_HEADS
    pw = hp * LANES
    pairs = [(qi, ki) for qi in range(nq) for ki in range(qi + 1)]
    qi_tab = jnp.array([qk[0] for qk in pairs], jnp.int32)
    ki_tab = jnp.array([qk[1] for qk in pairs], jnp.int32)
    grid = (nb, MLA_H // hp, len(pairs))
    n_g = len(gather)

    def body(qi_ref, ki_ref, q_ref, k_ref, v_ref, *rest):
        x_refs, (o_ref, lse_ref), got_refs = rest[:n_g], rest[n_g:n_g + 2], rest[n_g + 2:2 * n_g + 2]
        (m_s, l_s, acc_s), sems = rest[2 * n_g + 2:2 * n_g + 5], rest[2 * n_g + 5:]
        qi, ki = qi_ref[pl.program_id(2)], ki_ref[pl.program_id(2)]
        if n_g:
            start, forward, finish = _gather_phases(x_refs, got_refs, *sems)
            at_first, at_middle, at_last = _grid_ends(grid)
            pl.when(at_first)(start)
            pl.when(at_middle)(forward)

        @pl.when(ki == 0)
        def _():
            m_s[...] = jnp.full(m_s.shape, NEG, F32)
            l_s[...] = jnp.zeros(l_s.shape, F32)
            acc_s[...] = jnp.zeros(acc_s.shape, F32)

        def step(masked):
            parts = [(0, tq // 2, tq // 2), (tq // 2, tq, tq)] if masked else [(0, tq, tq)]
            work = [(j, a, b, kh) for j in range(hp) for a, b, kh in parts]
            ss = []
            for j, a, b, kh in work:
                wide = slice(LANES * j, LANES * (j + 1))
                s = _nt(k_ref[:kh, wide], q_ref[a:b, wide]) * (MLA_SCALE * LOG2E)
                if masked:
                    keys = lax.broadcasted_iota(jnp.int32, (kh, b - a), 0)
                    queries = a + lax.broadcasted_iota(jnp.int32, (kh, b - a), 1)
                    s = jnp.where(keys <= queries, s, NEG)
                ss.append(s)
            ps, alphas = [], []
            for (j, a, b, kh), s in zip(work, ss):
                m_prev = m_s[j, :, a:b]
                m_new = jnp.maximum(m_prev, jnp.max(s, axis=0, keepdims=True))
                alpha = jnp.exp2(m_prev - m_new)
                p = jnp.exp2(s - m_new)
                l_s[j, :, a:b] = alpha * l_s[j, :, a:b] + jnp.sum(p, axis=0, keepdims=True)
                m_s[j, :, a:b] = m_new
                ps.append(p.astype(BF16))
                alphas.append(alpha)
            for (j, a, b, kh), p, alpha in zip(work, ps, alphas):
                rows = slice(MLA_V * j, MLA_V * (j + 1))
                acc_s[rows, a:b] = alpha * acc_s[rows, a:b] + _tn(v_ref[:kh, rows], p)

        @pl.when(ki < qi)
        def _():
            step(False)

        @pl.when(ki == qi)
        def _():
            step(True)
            for j in range(hp):
                rows = slice(MLA_V * j, MLA_V * (j + 1))
                acc_s[rows, :] = acc_s[rows, :] / l_s[j]
                lse_ref[j:j + 1, :] = m_s[j] + jnp.log2(l_s[j])
            o_ref[...] = acc_s[...].T

        if n_g:
            pl.when(at_last)(finish)

    q_map = lambda b, g, s, qi_ref, ki_ref: (b * nq + qi_ref[s], g)
    kv_map = lambda b, g, s, qi_ref, ki_ref: (b * nq + ki_ref[s], g)
    return pl.pallas_call(
        body, name="mla_fwd_gather" if n_g else "mla_fwd",
        grid_spec=pltpu.PrefetchScalarGridSpec(
            num_scalar_prefetch=2, grid=grid,
            in_specs=[pl.BlockSpec((tq, pw), q_map), pl.BlockSpec((tq, pw), kv_map),
                      pl.BlockSpec((tq, hp * MLA_V), kv_map)] + [ANY] * n_g,
            out_specs=[pl.BlockSpec((tq, hp * MLA_V), q_map),
                       pl.BlockSpec((hp, tq), lambda b, g, s, qi_ref, ki_ref: (g, b * nq + qi_ref[s]))]
            + [ANY] * n_g,
            scratch_shapes=[pltpu.VMEM((hp, 1, tq), F32), pltpu.VMEM((hp, 1, tq), F32),
                            pltpu.VMEM((hp * MLA_V, tq), F32)]
            + (_gather_sems(n_g) if n_g else [])),
        out_shape=[SDS((T, MLA_H * MLA_V), F32), SDS((MLA_H, T), F32)] + _gather_out(gather),
        compiler_params=_cp("arbitrary", "arbitrary", "arbitrary"))(qi_tab, ki_tab, qf, kf, v, *gather)


SWA_BLOCKS = 8


def _swa_blocks(seq):
    return min(SWA_BLOCKS, seq // BLK)


def _swa_specs(nstep, nu):
    step = nu * BLK
    cur = lambda b, m: (b * nstep + m, 0)
    prev = lambda b, m: (nu * b * nstep + jnp.maximum(nu * m - 1, 0), 0)
    kvc = Z_AK // (2 * BLK)
    return [pl.BlockSpec(memory_space=pltpu.SMEM),
            pl.BlockSpec((step, 512), lambda b, m: (b * nstep + m, Z_AQ // 512)),
            pl.BlockSpec((step, 2 * BLK), lambda b, m: (b * nstep + m, kvc)),
            pl.BlockSpec((BLK, 2 * BLK), lambda b, m: (nu * b * nstep + jnp.maximum(nu * m - 1, 0), kvc)),
            pl.BlockSpec((step, 1), cur),
            pl.BlockSpec((BLK, 1), prev),
            pl.BlockSpec((1, 1, step), lambda b, m: (b * nstep + m, 0, 0))]


def _swa_scores(m, nu, q_ref, kvc_ref, kvp_ref, pcc_ref, pcp_ref, pr_ref):
    kv = jnp.concatenate([kvp_ref[...], kvc_ref[...]], axis=0)
    kb, vb = kv[:, :BLK].astype(BF16), kv[:, BLK:].astype(BF16)
    pos_keys = jnp.concatenate([pcp_ref[...], pcc_ref[...]], axis=0)
    key = lax.broadcasted_iota(jnp.int32, (2 * BLK, BLK), 0)
    qry = lax.broadcasted_iota(jnp.int32, (2 * BLK, BLK), 1)
    in_window = (key > qry) & (key <= qry + BLK)
    valid = [in_window & ((key >= BLK) | (m > 0))] + [in_window] * (nu - 1)
    dist = [pr_ref[0][:, BLK * u:BLK * (u + 1)] - pos_keys[BLK * u:BLK * (u + 2)] for u in range(nu)]

    def band(t, u, g):
        return t[BLK * u:BLK * (u + 2), SWA_DH * g:SWA_DH * (g + 1)]

    def scores(u, h):
        g = h // (SWA_H // SWA_KV)
        qh = q_ref[BLK * u:BLK * (u + 1), SWA_DH * h:SWA_DH * (h + 1)].astype(BF16)
        s = _nt(band(kb, u, g), qh) * (SWA_SCALE * LOG2E) - (2.0 ** -(h + 1) * LOG2E) * dist[u]
        return qh, jnp.where(valid[u], s, NEG)

    return kb, vb, band, scores


def _swa_fwd(sink, z, pos_col, pos_row, nb, seq, layer):
    T = z.shape[0]
    nu = _swa_blocks(seq)
    step = nu * BLK
    nstep = seq // step
    chains = [(u, h) for u in range(nu) for h in range(SWA_H)]

    def body(sink_ref, q_ref, kvc_ref, kvp_ref, pcc_ref, pcp_ref, pr_ref, o_ref, lse_ref):
        kb, vb, band, scores = _swa_scores(pl.program_id(1), nu, q_ref, kvc_ref, kvp_ref, pcc_ref, pcp_ref, pr_ref)
        ss = [scores(u, h)[1] for u, h in chains]
        es, dens = [], []
        for (u, h), s in zip(chains, ss):
            sk = sink_ref[layer, h] * LOG2E
            m = jnp.maximum(jnp.max(s, axis=0, keepdims=True), sk)
            e = jnp.exp2(s - m)
            den = jnp.sum(e, axis=0, keepdims=True) + jnp.exp2(sk - m)
            lse_ref[h:h + 1, BLK * u:BLK * (u + 1)] = m + jnp.log2(den)
            es.append(e.astype(BF16))
            dens.append(den)
        outs = [_tn(band(vb, u, h // (SWA_H // SWA_KV)), e) / den for (u, h), e, den in zip(chains, es, dens)]
        for u in range(nu):
            o_ref[BLK * u:BLK * (u + 1), :] = jnp.concatenate(outs[SWA_H * u:SWA_H * (u + 1)], axis=0).T

    return pl.pallas_call(
        body, grid=(nb, nstep), name="swa_fwd",
        in_specs=_swa_specs(nstep, nu),
        out_specs=[pl.BlockSpec((step, 512), lambda b, m: (b * nstep + m, 0)),
                   pl.BlockSpec((SWA_H, step), lambda b, m: (0, b * nstep + m))],
        out_shape=[SDS((T, 512), F32), SDS((SWA_H, T), F32)],
        compiler_params=_cp("parallel", "parallel"))(sink, z, z, z, pos_col, pos_col, pos_row)


def _fwd_merge(x, oa, ob, z, wa, wb, wo, tm, layer):
    T = x.shape[0]

    def body(x_ref, oa_ref, ob_ref, ag_ref, bg_ref, ma_ref, mb_ref, wa_ref, wb_ref, wo_ref, x1_ref):
        ag, bg = ag_ref[...].astype(F32), bg_ref[...].astype(F32)
        ua = _nn((oa_ref[...] * (ag * _sig(ag))).astype(BF16), wa_ref[...])
        ub = _nn((ob_ref[...] * (bg * _sig(bg))).astype(BF16), wb_ref[...])
        y = _sig(ma_ref[...].astype(F32)) * ua + _sig(mb_ref[...].astype(F32)) * ub
        x1_ref[...] = x_ref[...] + _nn(y.astype(BF16), wo_ref[...])

    return pl.pallas_call(
        body, grid=(T // tm,), name="fwd_merge",
        in_specs=[_row(tm, D), _row(tm, 512), _row(tm, 512), _row(tm, 512, Z_AG // 512), _row(tm, 512, Z_BG // 512),
                  _row(tm, D, Z_MA // D), _row(tm, D, Z_MB // D),
                  _res((512, D), 0), _res((512, D), 0), _res((D, D), 0)],
        out_specs=_row(tm, D),
        out_shape=SDS((T, D), F32),
        compiler_params=_cp("parallel"))(x, oa, ob, z, z, z, z, wa, wb, wo)


def _fwd_ple(x1, p, g, wpg, wpp, tm, layer):
    T = x1.shape[0]

    def body(x_ref, p_ref, g_ref, wpg_ref, wpp_ref, x2_ref, pg_ref, pp_ref):
        xf = x_ref[...]
        h1 = ((xf * _rstd(xf)) * g_ref[...]).astype(BF16)
        pg = _sig(_nn(h1, wpg_ref[...]))
        pp = _nn(p_ref[...].astype(BF16), wpp_ref[...])
        pg_ref[...] = pg.astype(BF16)
        pp_ref[...] = pp.astype(BF16)
        x2_ref[...] = xf + pg * pp

    return pl.pallas_call(
        body, grid=(T // tm,), name="fwd_ple",
        in_specs=[_row(tm, D), pl.BlockSpec((None, tm, PLE), lambda i: (layer, i, 0)),
                  _res((1, D), layer), _res((D, D), 0), _res((PLE, D), 0)],
        out_specs=[_row(tm, D)] * 3,
        out_shape=[SDS((T, D), F32), SDS((T, D), BF16), SDS((T, D), BF16)],
        compiler_params=_cp("parallel"))(x1, p, g, wpg, wpp)


def _ple_loss(x1, p, g, wpg, wpp, g_final, tgt, tm, layer):
    T = x1.shape[0]

    def body(x_ref, p_ref, g_ref, wpg_ref, wpp_ref, gf_ref, t_ref, dx_ref, dwg_ref, dwp_ref, dg_ref, dgf_ref, loss_ref):
        @pl.when(pl.program_id(0) == 0)
        def _():
            for ref in (dwg_ref, dwp_ref, dg_ref, dgf_ref, loss_ref):
                ref[...] = jnp.zeros(ref.shape, F32)

        xf, gp, gf = x_ref[...], g_ref[...], gf_ref[...]
        r = _rstd(xf)
        n = xf * r
        h1 = (n * gp).astype(BF16)
        pb = p_ref[...].astype(BF16)
        pg = _sig(_nn(h1, wpg_ref[...]))
        pp = _nn(pb, wpp_ref[...])
        x2 = xf + pg * pp
        r2 = _rstd(x2)
        n2 = x2 * r2
        err = n2 * gf - t_ref[...]
        loss_ref[...] += 0.5 * jnp.sum(jnp.mean(err * err, axis=-1, keepdims=True), axis=0, keepdims=True)
        d, dgfr = _norm_bwd(err * (1.0 / D), n2, r2, gf)
        dgf_ref[...] += jnp.sum(dgfr, axis=0, keepdims=True)
        dpgl = (d * pp * pg * (1.0 - pg)).astype(BF16)
        dwg_ref[...] += _tn(h1, dpgl)
        dwp_ref[...] += _tn(pb, (d * pg).astype(BF16))
        dxn, dgr = _norm_bwd(_nt(dpgl, wpg_ref[...]), n, r, gp)
        dx_ref[...] = d + dxn
        dg_ref[...] += jnp.sum(dgr, axis=0, keepdims=True)

    return pl.pallas_call(
        body, grid=(T // tm,), name="ple_loss",
        in_specs=[_row(tm, D), pl.BlockSpec((None, tm, PLE), lambda i: (layer, i, 0)), _res((1, D), layer),
                  _res((D, D), 0), _res((PLE, D), 0), _res((1, D)), _row(tm, D)],
        out_specs=[_row(tm, D), _acc((D, D)), _acc((PLE, D)), _acc((1, D)), _acc((1, D)), _acc((1, LANES))],
        out_shape=[SDS((T, D), F32), SDS((D, D), F32), SDS((PLE, D), F32), SDS((1, D), F32), SDS((1, D), F32),
                   SDS((1, LANES), F32)],
        compiler_params=_cp("arbitrary"))(x1, p, g, wpg, wpp, g_final, tgt)


def _bwd_ple(dx2, x1, pg, pp, p, g, wpg, tm, layer, ride=None):
    T = x1.shape[0]
    grid = (T // tm,)

    def body(d_ref, x_ref, pg_ref, pp_ref, p_ref, g_ref, w_ref, dx_ref, dwg_ref, dwp_ref, dg_ref):
        @pl.when(pl.program_id(0) == 0)
        def _():
            dwg_ref[...] = jnp.zeros(dwg_ref.shape, F32)
            dwp_ref[...] = jnp.zeros(dwp_ref.shape, F32)
            dg_ref[...] = jnp.zeros(dg_ref.shape, F32)

        d, xf, pg, gf = d_ref[...], x_ref[...], pg_ref[...].astype(F32), g_ref[...]
        r = _rstd(xf)
        n = xf * r
        dpgl = (d * pp_ref[...].astype(F32) * pg * (1.0 - pg)).astype(BF16)
        dwg_ref[...] += _tn((n * gf).astype(BF16), dpgl)
        dwp_ref[...] += _tn(p_ref[...].astype(BF16), (d * pg).astype(BF16))
        dxn, dgr = _norm_bwd(_nt(dpgl, w_ref[...]), n, r, gf)
        dx_ref[...] = d + dxn
        dg_ref[...] += jnp.sum(dgr, axis=0, keepdims=True)

    r_in, r_out, r_shape, r_scratch, r_args = _ride_args(ride)
    return pl.pallas_call(
        _riding(ride, body, 7, 4, grid), grid=grid, name="bwd_ple_ride" if ride else "bwd_ple",
        in_specs=[_row(tm, D)] * 4 + [pl.BlockSpec((None, tm, PLE), lambda i: (layer, i, 0)),
                                      _res((1, D), layer), _res((D, D), 0)] + r_in,
        out_specs=[_row(tm, D), _acc((D, D)), _acc((PLE, D)), _acc((1, D))] + r_out,
        out_shape=[SDS((T, D), F32), SDS((D, D), F32), SDS((PLE, D), F32), SDS((1, D), F32)] + r_shape,
        scratch_shapes=r_scratch,
        compiler_params=_cp("arbitrary"))(dx2, x1, pg, pp, p, g, wpg, *r_args)


def _bwd_merge(dx1, oa, ob, z, wa, wb, wo, tm, layer):
    T = dx1.shape[0]

    def body(d_ref, oa_ref, ob_ref, ag_ref, bg_ref, ma_ref, mb_ref, wa_ref, wb_ref, wo_ref,
             doa_ref, dob_ref, dag_ref, dbg_ref, dma_ref, dmb_ref, dsa_ref, dsb_ref, dwa_ref, dwb_ref, dwo_ref):
        @pl.when(pl.program_id(0) == 0)
        def _():
            dwa_ref[...] = jnp.zeros(dwa_ref.shape, F32)
            dwb_ref[...] = jnp.zeros(dwb_ref.shape, F32)
            dwo_ref[...] = jnp.zeros(dwo_ref.shape, F32)

        db = d_ref[...].astype(BF16)
        gated = []
        for o_ref, gate_ref, w_ref in ((oa_ref, ag_ref, wa_ref), (ob_ref, bg_ref, wb_ref)):
            raw, gate = o_ref[...], gate_ref[...].astype(F32)
            sg = _sig(gate)
            silu = gate * sg
            ob16 = (raw * silu).astype(BF16)
            gated.append((raw, gate, sg, silu, ob16, _nn(ob16, w_ref[...])))
        ua, ub = gated[0][5], gated[1][5]
        sa, sb = _sig(ma_ref[...].astype(F32)), _sig(mb_ref[...].astype(F32))
        dwo_ref[...] += _tn((sa * ua + sb * ub).astype(BF16), db)
        dy = _nt(db, wo_ref[...])
        dma_ref[...] = (dy * ua * sa * (1.0 - sa)).astype(BF16)
        dmb_ref[...] = (dy * ub * sb * (1.0 - sb)).astype(BF16)
        for (s, w_ref, do_ref, dgate_ref, dw_ref, ds_ref), (raw, gate, sg, silu, ob16, _) in zip((
                (sa, wa_ref, doa_ref, dag_ref, dwa_ref, dsa_ref),
                (sb, wb_ref, dob_ref, dbg_ref, dwb_ref, dsb_ref)), gated):
            du = (dy * s).astype(BF16)
            dw_ref[...] += _tn(ob16, du)
            do = _nt(du, w_ref[...])
            draw = do * silu
            do_ref[...] = draw.astype(BF16)
            dgate_ref[...] = (do * raw * (sg * (1.0 + gate * (1.0 - sg)))).astype(BF16)
            ds_ref[...] = jnp.sum((draw * raw).T.reshape(MLA_H, MLA_V, tm), axis=1)

    return pl.pallas_call(
        body, grid=(T // tm,), name="bwd_merge",
        in_specs=[_row(tm, D), _row(tm, 512), _row(tm, 512), _row(tm, 512, Z_AG // 512), _row(tm, 512, Z_BG // 512),
                  _row(tm, D, Z_MA // D), _row(tm, D, Z_MB // D),
                  _res((512, D), 0), _res((512, D), 0), _res((D, D), 0)],
        out_specs=[_row(tm, 512)] * 4 + [_row(tm, D)] * 2 + [pl.BlockSpec((MLA_H, tm), lambda i: (0, i))] * 2
        + [_acc((512, D)), _acc((512, D)), _acc((D, D))],
        out_shape=[SDS((T, 512), BF16), SDS((T, 512), BF16), SDS((T, 512), BF16), SDS((T, 512), BF16),
                   SDS((T, D), BF16), SDS((T, D), BF16), SDS((MLA_H, T), F32), SDS((MLA_H, T), F32),
                   SDS((512, D), F32), SDS((512, D), F32), SDS((D, D), F32)],
        compiler_params=_cp("arbitrary"))(dx1, oa, ob, z, z, z, z, wa, wb, wo)


def _mla_bwd(qf, kf, v, do, lse, dsum, nb, seq, tq, exchange=()):
    T = qf.shape[0]
    nq = seq // tq
    hp = MLA_BWD_HEADS
    pw = hp * LANES
    pairs = [(qi, ki) for ki in range(nq) for qi in range(ki, nq)]
    qi_tab = jnp.array([qk[0] for qk in pairs], jnp.int32)
    ki_tab = jnp.array([qk[1] for qk in pairs], jnp.int32)
    grid = (nb, MLA_H // hp, len(pairs))
    n_x = len(exchange)

    def body(qi_ref, ki_ref, q_ref, k_ref, v_ref, do_ref, lse_ref, dsum_ref, *rest):
        p_refs, (dq_ref, dk_ref, dv_ref), got_refs = rest[:n_x], rest[n_x:n_x + 3], rest[n_x + 3:2 * n_x + 3]
        (dk_s, dv_s, dqt_s), sems = rest[2 * n_x + 3:2 * n_x + 6], rest[2 * n_x + 6:]
        step_id = pl.program_id(2)
        qi, ki = qi_ref[step_id], ki_ref[step_id]
        if n_x:
            start, finish = _exchange_phases(p_refs, got_refs, *sems)
            at_first, _, at_last = _grid_ends(grid)
            pl.when(at_first)(start)

        @pl.when(step_id == 0)
        def _():
            dqt_s[...] = jnp.zeros(dqt_s.shape, F32)

        @pl.when(qi == ki)
        def _():
            dk_s[...] = jnp.zeros(dk_s.shape, F32)
            dv_s[...] = jnp.zeros(dv_s.shape, F32)

        def step(masked):
            if masked:
                keys = lax.broadcasted_iota(jnp.int32, (tq, tq), 0)
                queries = lax.broadcasted_iota(jnp.int32, (tq, tq), 1)
                mask = keys <= queries
            for j in range(hp):
                wide = slice(LANES * j, LANES * (j + 1))
                sl = slice(MLA_V * j, MLA_V * (j + 1))
                q, k = q_ref[:, wide], k_ref[:, wide]
                dob = do_ref[:, sl].astype(BF16)
                s = _nt(k, q) * (MLA_SCALE * LOG2E)
                if masked:
                    s = jnp.where(mask, s, NEG)
                p = jnp.exp2(s - lse_ref[j:j + 1, :])
                dv_s[:, sl] += _nn(p.astype(BF16), dob)
                ds = (p * (_nt(v_ref[:, sl], dob) - dsum_ref[j:j + 1, :]) * MLA_SCALE).astype(BF16)
                dk_s[:, wide] += _nn(ds, q)
                dqt_s[qi, wide, :] += _tn(k, ds)

        @pl.when(qi > ki)
        def _():
            step(False)

        @pl.when(qi == ki)
        def _():
            step(True)

        @pl.when(qi == nq - 1)
        def _():
            dk_ref[...] = dk_s[...]
            dv_ref[...] = dv_s[...]

        @pl.when(step_id == len(pairs) - 1)
        def _():
            for n in range(nq):
                dq_ref[tq * n:tq * (n + 1), :] = dqt_s[n].T

        if n_x:
            pl.when(at_last)(finish)

    qmap = lambda b, g, s, qi_ref, ki_ref: (b * nq + qi_ref[s], g)
    kmap = lambda b, g, s, qi_ref, ki_ref: (b * nq + ki_ref[s], g)
    stat = pl.BlockSpec((None, hp, tq), lambda b, g, s, qi_ref, ki_ref: (g, 0, b * nq + qi_ref[s]))
    vw = hp * MLA_V
    return pl.pallas_call(
        body, name="mla_bwd_exchange" if n_x else "mla_bwd",
        grid_spec=pltpu.PrefetchScalarGridSpec(
            num_scalar_prefetch=2, grid=grid,
            in_specs=[pl.BlockSpec((tq, pw), qmap), pl.BlockSpec((tq, pw), kmap), pl.BlockSpec((tq, vw), kmap),
                      pl.BlockSpec((tq, vw), qmap), stat, stat] + [ANY] * n_x,
            out_specs=[pl.BlockSpec((seq, pw), lambda b, g, s, qi_ref, ki_ref: (b, g)),
                       pl.BlockSpec((tq, pw), kmap), pl.BlockSpec((tq, vw), kmap)] + [ANY] * n_x,
            scratch_shapes=[pltpu.VMEM((tq, pw), F32), pltpu.VMEM((tq, vw), F32), pltpu.VMEM((nq, pw, tq), F32)]
            + (_exchange_sems(n_x) if n_x else [])),
        out_shape=[SDS((T, QFW), F32), SDS((T, QFW), F32), SDS((T, MLA_H * MLA_V), F32)]
        + [SDS(a.shape, a.dtype) for a in exchange],
        compiler_params=_cp("arbitrary", "arbitrary", "arbitrary"))(qi_tab, ki_tab, qf, kf, v, do, lse, dsum, *exchange)


def _swa_bwd(sink, z, pos_col, pos_row, do, lse, dsum, nb, seq, layer, ride=None):
    T = z.shape[0]
    nu = _swa_blocks(seq)
    step = nu * BLK
    nstep = seq // step
    chains = [(u, h) for u in range(nu) for h in range(SWA_H)]

    def body(sink_ref, q_ref, kvc_ref, kvp_ref, pcc_ref, pcp_ref, pr_ref, do_ref, lse_ref, dsum_ref,
             dq_ref, dkv_ref, dsink_ref):
        b, m = pl.program_id(0), pl.program_id(1)

        @pl.when((b == 0) & (m == 0))
        def _():
            dsink_ref[...] = jnp.zeros(dsink_ref.shape, F32)

        @pl.when(m == 0)
        def _():
            dkv_ref[...] = jnp.zeros(dkv_ref.shape, F32)

        kb, vb, band, scores = _swa_scores(m, nu, q_ref, kvc_ref, kvp_ref, pcc_ref, pcp_ref, pr_ref)
        lane = lax.broadcasted_iota(jnp.int32, (1, LANES), 1)
        dsink = jnp.zeros((1, LANES), F32)
        group = lambda h: h // (SWA_H // SWA_KV)
        qs, ss, dobs, dps = [], [], [], []
        for u, h in chains:
            qh, s = scores(u, h)
            dob = do_ref[BLK * u:BLK * (u + 1), SWA_DH * h:SWA_DH * (h + 1)].astype(BF16)
            qs.append(qh)
            ss.append(s)
            dobs.append(dob)
            dps.append(_nt(band(vb, u, group(h)), dob))
        pbs, dss = [], []
        for (u, h), s, dp in zip(chains, ss, dps):
            cols = slice(BLK * u, BLK * (u + 1))
            lse, dsum = lse_ref[h:h + 1, cols], dsum_ref[h:h + 1, cols]
            p = jnp.exp2(s - lse)
            pbs.append(p.astype(BF16))
            dss.append((p * (dp - dsum) * SWA_SCALE).astype(BF16))
            dsk = jnp.sum(-jnp.exp2(sink_ref[layer, h] * LOG2E - lse) * dsum, axis=1, keepdims=True)
            dsink = dsink + jnp.where(lane == h, dsk, 0.0)
        dqs, dkv = [], [[[None, None], [None, None]] for _ in range(nu)]
        for i, (u, h) in enumerate(chains):
            g = group(h)
            dqs.append(_tn(band(kb, u, g), dss[i]))
            dk, dv = _nn(dss[i], qs[i]), _nn(pbs[i], dobs[i])
            dkv[u][g][0] = dk if dkv[u][g][0] is None else dkv[u][g][0] + dk
            dkv[u][g][1] = dv if dkv[u][g][1] is None else dkv[u][g][1] + dv
        for u in range(nu):
            dq_ref[BLK * u:BLK * (u + 1), :] = jnp.concatenate(dqs[SWA_H * u:SWA_H * (u + 1)], axis=0).T.astype(BF16)
        dsink_ref[...] += dsink
        upd = [jnp.concatenate([dkv[u][0][0], dkv[u][1][0], dkv[u][0][1], dkv[u][1][1]], axis=1) for u in range(nu)]
        base = pl.multiple_of(m * step, step)
        for u in range(nu):
            own = upd[u][BLK:] + upd[u + 1][:BLK] if u + 1 < nu else upd[u][BLK:]
            dkv_ref[pl.ds(base + BLK * u, BLK), :] += own

        @pl.when(m > 0)
        def _():
            dkv_ref[pl.ds(pl.multiple_of(m * step - BLK, BLK), BLK), :] += upd[0][:BLK]

    r_in, r_out, r_shape, r_scratch, r_args = _ride_args(ride)
    return pl.pallas_call(
        _riding(ride, body, 10, 3, (nb, nstep)), grid=(nb, nstep), name="swa_bwd_ride" if ride else "swa_bwd",
        in_specs=_swa_specs(nstep, nu) + [pl.BlockSpec((step, 512), lambda b, m: (b * nstep + m, 0))]
        + [pl.BlockSpec((SWA_H, step), lambda b, m: (0, b * nstep + m))] * 2 + r_in,
        out_specs=[pl.BlockSpec((step, 512), lambda b, m: (b * nstep + m, 0)),
                   pl.BlockSpec((seq, 2 * BLK), lambda b, m: (b, 0)),
                   pl.BlockSpec((1, LANES), lambda b, m: (0, 0))] + r_out,
        out_shape=[SDS((T, 512), BF16), SDS((T, 2 * BLK), F32), SDS((1, LANES), F32)] + r_shape,
        scratch_shapes=r_scratch,
        compiler_params=_cp("arbitrary", "arbitrary"))(sink, z, z, z, pos_col, pos_col, pos_row, do, lse, dsum,
                                                       *r_args)


def _bwd_prep(dq, dk, dv, z, gq, gkv, wq, wkv, tc, ts1, ts2, tm, layer):
    T = z.shape[0]

    def body(dq_ref, dk_ref, dv_ref, qd_ref, kvd_ref, gq_ref, gkv_ref, wq_ref, wkv_ref, c_ref, s1_ref, s2_ref,
             dqd_ref, dkvd_ref, dkr_ref, dwq_ref, dwkv_ref, dgq_ref, dgkv_ref, dqb_s, dkvb_s):
        @pl.when(pl.program_id(0) == 0)
        def _():
            for ref in (dwq_ref, dwkv_ref, dgq_ref, dgkv_ref):
                ref[...] = jnp.zeros(ref.shape, F32)

        c, s1, s2 = c_ref[...], s1_ref[...], s2_ref[...]
        lane = lax.broadcasted_iota(jnp.int32, (1, LANES), 1)
        rope_lanes = (lane >= MLA_NOPE) & (lane < MLA_QK)
        dkb = jnp.zeros((tm, LANES), F32)
        for h in range(MLA_H):
            sl = slice(LANES * h, LANES * (h + 1))
            dqb_s[:, sl] = _rope_t(dq_ref[:, sl], c, s1, s2).astype(BF16)
            dkh = dk_ref[:, sl]
            dkb = dkb + dkh
            dkvb_s[:, sl] = dkh.astype(BF16)
        dkvb_s[:, QFW:] = dv_ref[...].astype(BF16)
        dkr_ref[...] = _rope_t(jnp.where(rope_lanes, dkb, 0.0), c, s1, s2).astype(BF16)

        for (x_ref, g_ref, w_ref, d_s, dx_ref, dw_ref, dg_ref) in (
                (qd_ref, gq_ref, wq_ref, dqb_s, dqd_ref, dwq_ref, dgq_ref),
                (kvd_ref, gkv_ref, wkv_ref, dkvb_s, dkvd_ref, dwkv_ref, dgkv_ref)):
            xf, gf, db = x_ref[...].astype(F32), g_ref[...], d_s[...]
            r = _rstd(xf)
            n = xf * r
            dw_ref[...] += _tn((n * gf).astype(BF16), db)
            dx, dgr = _norm_bwd(_nt(db, w_ref[...]), n, r, gf)
            dx_ref[...] = dx.astype(BF16)
            dg_ref[...] += jnp.sum(dgr, axis=0, keepdims=True)

    return pl.pallas_call(
        body, grid=(T // tm,), name="bwd_prep",
        in_specs=[_row(tm, QFW), _row(tm, QFW), _row(tm, MLA_H * MLA_V),
                  _row(tm, QL, Z_QD // QL), _row(tm, KVL, Z_KVD // KVL),
                  _res((1, QL), layer), _res((1, KVL), layer), _res((QL, QFW), 0), _res((KVL, KVW), 0),
                  _row(tm, LANES), _row(tm, LANES), _row(tm, LANES)],
        out_specs=[_row(tm, QL), _row(tm, KVL), _row(tm, LANES),
                   _acc((QL, QFW)), _acc((KVL, KVW)), _acc((1, QL)), _acc((1, KVL))],
        out_shape=[SDS((T, QL), BF16), SDS((T, KVL), BF16), SDS((T, LANES), BF16),
                   SDS((QL, QFW), F32), SDS((KVL, KVW), F32), SDS((1, QL), F32), SDS((1, KVL), F32)],
        scratch_shapes=[pltpu.VMEM((tm, QFW), BF16), pltpu.VMEM((tm, KVW), BF16)],
        compiler_params=_cp("arbitrary"))(dq, dk, dv, z, z, gq, gkv, wq, wkv, tc, ts1, ts2)


def _bwd_in(pieces, x, g, dres, w, tm, layer):
    T = x.shape[0]
    grid = (T // tm,)
    widths = [pc.shape[1] for pc in pieces]
    assert sum(widths) == ZW
    n_p = len(pieces)

    def body(*refs):
        p_refs, (x_ref, g_ref, r_ref, w_ref, dx_ref, dz_ref, dg_ref) = refs[:n_p], refs[n_p:]

        @pl.when(pl.program_id(0) == 0)
        def _():
            dg_ref[...] = jnp.zeros(dg_ref.shape, F32)

        off = 0
        for ref, wd in zip(p_refs, widths):
            dz_ref[:, off:off + wd] = ref[...].astype(BF16)
            off += wd
        xf, gf = x_ref[...], g_ref[...]
        r = _rstd(xf)
        n = xf * r
        dx, dgr = _norm_bwd(_nt(dz_ref[...], w_ref[...]), n, r, gf)
        dx_ref[...] = r_ref[...] + dx
        dg_ref[...] += jnp.sum(dgr, axis=0, keepdims=True)

    return pl.pallas_call(
        body, grid=grid, name="bwd_in",
        in_specs=[_row(tm, wd) for wd in widths] + [_row(tm, D), _res((1, D), layer), _row(tm, D),
                                                    _res((D, ZW), 0)],
        out_specs=[_row(tm, D), _row(tm, ZW), _acc((1, D))],
        out_shape=[SDS((T, D), F32), SDS((T, ZW), BF16), SDS((1, D), F32)],
        compiler_params=_cp("arbitrary"))(*pieces, x, g, dres, w)


def _wgrad_in(hb, dzb, tm, ride=None):
    T = hb.shape[0]
    half = ZW // 2
    grid = (2, T // tm)

    def body(h_ref, dz_ref, dw_ref):
        @pl.when(pl.program_id(1) == 0)
        def _():
            dw_ref[...] = jnp.zeros(dw_ref.shape, F32)

        dw_ref[...] += _tn(h_ref[...], dz_ref[...])

    r_in, r_out, r_shape, r_scratch, r_args = _ride_args(ride)
    out = pl.pallas_call(
        _riding(ride, body, 2, 1, grid), grid=grid, name="wgrad_in_ride" if ride else "wgrad_in",
        in_specs=[pl.BlockSpec((tm, D), lambda j, t: (t, 0)), pl.BlockSpec((tm, half), lambda j, t: (t, j))] + r_in,
        out_specs=[pl.BlockSpec((D, half), lambda j, t: (0, j))] + r_out,
        out_shape=[SDS((D, ZW), F32)] + r_shape, scratch_shapes=r_scratch,
        compiler_params=_cp("arbitrary", "arbitrary"))(hb, dzb, *r_args)
    return out


IN_PIECES = ((0, 512, Z_AQ), (512, 128, Z_AK), (640, 128, Z_AV), (768, 512, Z_AG), (1280, 256, Z_QD),
             (1536, 128, Z_KVD), (1664, MLA_ROPE, Z_KR + MLA_NOPE), (1696, 512, Z_BG), (2208, 1024, Z_MA),
             (3232, 1024, Z_MB))
WIDE_W = IN_W // N_DEV


def _column_runs():
    runs = []
    for start, width, kstart in IN_PIECES:
        col = start
        while col < start + width:
            dev = col // WIDE_W
            stop = min(start + width, (dev + 1) * WIDE_W)
            runs.append((dev, col - dev * WIDE_W, stop - col, kstart + col - start))
            col = stop
    return runs


def _win_layout(blocks, tm):
    runs = _column_runs()

    def body(g_ref, o_ref):
        o_ref[:, Z_KR:Z_KR + LANES] = jnp.zeros((tm, LANES), o_ref.dtype)
        for dev, lo, n, k in runs:
            o_ref[:, k:k + n] = g_ref[dev, :, lo:lo + n]

    return pl.pallas_call(
        body, grid=(D // tm,), name="win_layout",
        in_specs=[pl.BlockSpec((N_DEV, None, tm, WIDE_W), lambda i: (0, 0, i, 0))],
        out_specs=pl.BlockSpec((None, tm, ZW), lambda i: (0, i, 0)),
        out_shape=SDS((1, D, ZW), blocks.dtype),
        compiler_params=_cp("parallel"))(blocks)


def _win_grad_layout(dw, tm):
    runs = _column_runs()

    def body(g_ref, o_ref):
        for dev, lo, n, k in runs:
            o_ref[dev, :, lo:lo + n] = g_ref[:, k:k + n]

    return pl.pallas_call(
        body, grid=(D // tm,), name="win_grad_layout",
        in_specs=[_row(tm, ZW)],
        out_specs=pl.BlockSpec((N_DEV, None, tm, WIDE_W), lambda i: (0, 0, i, 0)),
        out_shape=SDS((N_DEV, 1, D, WIDE_W), F32),
        compiler_params=_cp("parallel"))(dw)


def _wuq_to_kernel(w):
    w = w.reshape(w.shape[:-1] + (MLA_H, MLA_QK))
    w = jnp.pad(w, [(0, 0)] * (w.ndim - 1) + [(0, LANES - MLA_QK)])
    return w.reshape(w.shape[:-2] + (QFW,))


def _wuq_from_kernel(g):
    g = g.reshape(g.shape[:-1] + (MLA_H, LANES))[..., :MLA_QK]
    return g.reshape(g.shape[:-2] + (MLA_H * MLA_QK,))


def _wukv_to_kernel(w):
    w = w.reshape(w.shape[:-1] + (MLA_H, MLA_NOPE + MLA_V))
    k = jnp.pad(w[..., :MLA_NOPE], [(0, 0)] * (w.ndim - 1) + [(0, LANES - MLA_NOPE)])
    v = w[..., MLA_NOPE:]
    return jnp.concatenate([k.reshape(k.shape[:-2] + (QFW,)), v.reshape(v.shape[:-2] + (MLA_H * MLA_V,))], axis=-1)


def _wukv_from_kernel(g):
    k = g[..., :QFW].reshape(g.shape[:-1] + (MLA_H, LANES))[..., :MLA_NOPE]
    v = g[..., QFW:].reshape(g.shape[:-1] + (MLA_H, MLA_V))
    kv = jnp.concatenate([k, v], axis=-1)
    return kv.reshape(kv.shape[:-2] + (MLA_H * (MLA_NOPE + MLA_V),))


def _rope_tables(pos):
    half = MLA_ROPE // 2
    inv = 10000.0 ** (-jnp.arange(0, MLA_ROPE, 2, dtype=F32) / MLA_ROPE)
    ang = pos.astype(F32)[:, None] * inv
    cos, sin = jnp.cos(ang), jnp.sin(ang)
    one = jnp.ones((pos.shape[0], MLA_NOPE), F32)
    zero = lambda n: jnp.zeros((pos.shape[0], n), F32)
    tc = jnp.concatenate([one, cos, cos, one[:, :LANES - MLA_QK]], axis=1)
    ts1 = jnp.concatenate([zero(MLA_NOPE + half), sin, zero(LANES - MLA_QK)], axis=1)
    ts2 = jnp.concatenate([zero(MLA_NOPE), -sin, zero(LANES - MLA_NOPE - half)], axis=1)
    return tc, ts1, ts2


def _local_step(x, p, positions, loss_target, small, wts, plan=None):
    nb, seq, _ = x.shape
    T = nb * seq
    tm = min(512, T)
    tl = min(1024, T)
    tq = min(512, seq)
    xf = x.reshape(T, D)
    pos = positions.reshape(T)
    posf = pos.astype(F32)
    swa_step = _swa_blocks(seq) * BLK
    pos_col, pos_row = posf.reshape(T, 1), posf.reshape(T // swa_step, 1, swa_step)
    tc, ts1, ts2 = _rope_tables(pos)

    wts, sm = list(wts), small
    pl_in = p.reshape(DEPTH, T, PLE)
    saved = []
    for i in range(DEPTH):
        riding = plan is not None and i == 0
        w = wts[i]
        z, hb, *got = _fwd_in(xf, sm["g_mix"], w["w_in"], tm, i,
                              ride=_Ride("gather", plan["behind_fwd_in"]) if riding else None)
        if riding:
            w = wts[0] = dict(w, **plan["row_weights"](got[0]))
        oa, lse_a = _swa_fwd(sm["sink"], z, pos_col, pos_row, nb, seq, i)
        qf, kf, v = _fwd_prep(z, sm["g_q"], sm["g_kv"], w["w_uq"], w["w_ukv"], tc, ts1, ts2, tl, i)
        ob, lse_b, *got = _mla_fwd(qf, kf, v, nb, seq, tq, gather=plan["behind_mla_fwd"] if riding else ())
        if riding:
            wts.append(dict(w_in=plan["w_in"](got[0]), **plan["row_weights"](got[1])))
        x1 = _fwd_merge(xf, oa, ob, z, w["w_br_a"], w["w_br_b"], w["w_out"], tm, i)
        saved.append(dict(x=xf, z=z, hb=hb, oa=oa, lse_a=lse_a, qf=qf, kf=kf, v=v, ob=ob, lse_b=lse_b, x1=x1))
        if i < DEPTH - 1:
            xf, saved[i]["pg"], saved[i]["pp"] = _fwd_ple(x1, pl_in, sm["g_ple"], w["w_ple_gate"], w["w_ple_proj"],
                                                          tl, i)

    last = _ple_loss(x1, pl_in, sm["g_ple"], w["w_ple_gate"], w["w_ple_proj"], small["g_final"],
                     loss_target.reshape(T, D), tm, DEPTH - 1)
    dg_final, loss = last[4], last[5]

    grads = [None] * DEPTH
    exchanged = {}
    for i in reversed(range(DEPTH)):
        riding = plan is not None and i == 0
        sv, w = saved[i], wts[i]
        pay = plan["payload"](grads[1]) if riding else []
        if i == DEPTH - 1:
            (dx1, dwpg, dwpp, dg_ple), got = last[:4], []
        else:
            dx1, dwpg, dwpp, dg_ple, *got = _bwd_ple(dx, sv["x1"], sv["pg"], sv["pp"], pl_in, sm["g_ple"],
                                                     w["w_ple_gate"], tm, i,
                                                     ride=_Ride("swap", pay) if riding else None)
        doa, dob, dag, dbg, dma, dmb, dsum_a, dsum_b, dwa, dwb, dwo = _bwd_merge(
            dx1, sv["oa"], sv["ob"], sv["z"], w["w_br_a"], w["w_br_b"], w["w_out"], tm, i)
        stats = (MLA_H // MLA_BWD_HEADS, MLA_BWD_HEADS, T)
        dq_b, dk_b, dv_b, *exchanged["layer1"] = _mla_bwd(
            sv["qf"], sv["kf"], sv["v"], dob, sv["lse_b"].reshape(stats), dsum_b.reshape(stats), nb, seq, tq,
            exchange=plan["add"](pay, got) if riding else ())
        dqd, dkvd, dkr, dwq, dwkv, dgq, dgkv = _bwd_prep(dq_b, dk_b, dv_b, sv["z"], sm["g_q"], sm["g_kv"],
                                                         w["w_uq"], w["w_ukv"], tc, ts1, ts2, tl, i)
        g = dict(w_uq=_wuq_from_kernel(dwq), w_ukv=_wukv_from_kernel(dwkv), w_br_a=dwa, w_br_b=dwb, w_out=dwo,
                 w_ple_gate=dwpg, w_ple_proj=dwpp)
        pay = [plan["rows_payload"](g)] if riding else []
        dq_a, dkv_a, dsink, *got = _swa_bwd(sm["sink"], sv["z"], pos_col, pos_row, doa, sv["lse_a"], dsum_a, nb, seq,
                                            i, ride=_Ride("swap", pay) if riding else None)
        dx, dzb, dg_mix = _bwd_in([dma, dmb, dq_a, dag, dbg, dqd, dkv_a, dkvd, dkr], sv["x"], sm["g_mix"], dx1,
                                  w["w_in"], tm, i)
        dwin, *exchanged["rows0"] = _wgrad_in(sv["hb"], dzb, tm,
                                              ride=_Ride("exchange", plan["add"](pay, got)) if riding else None)
        g.update(g_mix=dg_mix[0], w_in=dwin, sink=dsink[0, :SWA_H], g_q=dgq[0], g_kv=dgkv[0], g_ple=dg_ple[0])
        grads[i] = g
    return loss, dx.reshape(nb, seq, D), grads, dg_final[0], exchanged


def _row_weights(rows):
    blocks = _unpack_rows(rows)
    out = {n: _join(n, blocks[n]) for n, _ in ROWS_PIECES}
    out.update(w_uq=_wuq_to_kernel(out["w_uq"]), w_ukv=_wukv_to_kernel(out["w_ukv"]))
    return out


def _small_params(g_mix, sink, g_q, g_kv, g_ple, g_final):
    return dict(g_mix=g_mix[:, None], sink=sink, g_q=g_q[:, None], g_kv=g_kv[:, None], g_ple=g_ple[:, None],
                g_final=g_final[None])


UQ_W = MLA_H * MLA_QK // N_DEV
ROWS_PIECES = (("w_uq", QL), ("w_ukv", KVL), ("w_br_a", 512), ("w_br_b", 512), ("w_out", D), ("w_ple_gate", D),
               ("w_ple_proj", PLE))
SMALL = (("g_mix", (DEPTH, D)), ("sink", (DEPTH, SWA_H)), ("g_q", (DEPTH, QL)), ("g_kv", (DEPTH, KVL)),
         ("g_ple", (DEPTH, D)), ("g_final", (D,)))
VEC_ROWS = 48
ROWS_N = sum(r for _, r in ROWS_PIECES)
WIDE_TILE, ROWS_TILE = 512, ROWS_N // 2


def _to_rows(name, a):
    if name == "w_uq":
        a = jnp.pad(a, [(0, 0)] * (a.ndim - 1) + [(0, LANES - UQ_W)])
    return a.reshape(a.shape[:-2] + (-1, LANES))


def _from_rows(name, r):
    if name in ("w_out", "w_ple_gate"):
        return r.reshape(r.shape[:-2] + (D // N_DEV, D))
    return r[..., :UQ_W] if name == "w_uq" else r


def _pack_rows(blocks):
    return jnp.concatenate([_to_rows(n, blocks[n]) for n, _ in ROWS_PIECES], axis=-2)


def _unpack_rows(rows):
    blocks, off = {}, 0
    for n, r in ROWS_PIECES:
        blocks[n] = _from_rows(n, rows[..., off:off + r, :])
        off += r
    return blocks


def _pack_vec(vectors, loss=None):
    parts = [vectors[n].reshape(-1) for n, _ in SMALL] + ([] if loss is None else [loss.reshape(1)])
    vec = jnp.concatenate(parts)
    return jnp.pad(vec, (0, VEC_ROWS * LANES - vec.shape[0])).reshape(1, VEC_ROWS, LANES)


def _unpack_vec(vec):
    vec = vec.reshape(-1)
    vectors, off = {}, 0
    for n, shp in SMALL:
        size = 1
        for s in shp:
            size *= s
        vectors[n] = vec[off:off + size].reshape(shp)
        off += size
    return vectors, vec[off]


def _join(name, blocks):
    if name in ("w_out", "w_ple_gate"):
        return jnp.moveaxis(blocks, 0, 1).reshape(blocks.shape[1], -1, blocks.shape[-1])
    return jnp.moveaxis(blocks, 0, 2).reshape(blocks.shape[1], blocks.shape[2], -1)


def _split(name, full):
    if name in ("w_out", "w_ple_gate"):
        return jnp.moveaxis(full.reshape(full.shape[0], N_DEV, -1, full.shape[-1]), 1, 0)
    return jnp.moveaxis(full.reshape(full.shape[0], full.shape[1], N_DEV, -1), 2, 0)


MESH_ID = pl.DeviceIdType.MESH
ANY = pl.BlockSpec(memory_space=pl.ANY)


def _place():
    return lax.axis_index("x"), lax.axis_index("y"), lax.axis_index("c")


def _all_gather(blocks):
    n = len(blocks)

    def body(*refs):
        start, forward, finish = _gather_phases(refs[:n], refs[n:2 * n], *refs[2 * n:])
        start()
        forward()
        finish()

    return pl.pallas_call(
        body, name="all_gather_weights", out_shape=_gather_out(blocks),
        in_specs=[ANY] * n, out_specs=[ANY] * n, scratch_shapes=_gather_sems(n))(*blocks)


def _gather_out(blocks):
    return [SDS((N_DEV,) + b.shape, b.dtype) for b in blocks]


def _gather_sems(n):
    return [pltpu.SemaphoreType.DMA((7 * n,)), pltpu.SemaphoreType.DMA((7 * n,)), pltpu.SemaphoreType.DMA((n,))]


def _gather_phases(x_refs, out_refs, send_sems, recv_sems, local_sems):
    n = len(x_refs)
    x, y, c = _place()
    me, sibling = (x, y, c), (x, y, 1 - c)
    chips = [(1 - x, y), (x, 1 - y), (1 - x, 1 - y)]

    def slot(a, px, py, pc):
        return out_refs[a].at[4 * px + 2 * py + pc]

    def copy(a, k, blk, to, src=None):
        return pltpu.make_async_remote_copy(
            src_ref=slot(a, *blk) if src is None else src, dst_ref=slot(a, *blk),
            send_sem=send_sems.at[7 * a + k], recv_sem=recv_sems.at[7 * a + k], device_id=to,
            device_id_type=MESH_ID)

    def mine():
        return [pltpu.make_async_copy(x_refs[a], slot(a, *me), local_sems.at[a]) for a in range(n)]

    def first():
        out = []
        for a in range(n):
            out += [copy(a, 0, me, sibling, src=x_refs[a])]
            out += [copy(a, 1 + j, me, (*chip, c), src=x_refs[a]) for j, chip in enumerate(chips)]
        return out

    def passed():
        return [copy(a, 4 + j, (*chip, c), sibling) for j, chip in enumerate(chips) for a in range(n)]

    def start():
        for cp in mine() + first():
            cp.start()

    def forward():
        for j, chip in enumerate(chips):
            for a in range(n):
                copy(a, 1 + j, (*chip, c), me).wait_recv()
                copy(a, 4 + j, (*chip, c), sibling).start()

    def finish():
        for a in range(n):
            copy(a, 0, sibling, me).wait_recv()
            for j, chip in enumerate(chips):
                copy(a, 4 + j, (*chip, 1 - c), me).wait_recv()
        for cp in first() + passed():
            cp.wait_send()
        for cp in mine():
            cp.wait()

    return start, forward, finish


def _swap_sibling(arrs):
    n = len(arrs)

    def body(*refs):
        start, finish = _swap_phases(refs[:n], refs[n:2 * n], *refs[2 * n:])
        start()
        finish()

    return pl.pallas_call(
        body, name="swap_sibling", out_shape=[SDS((a.shape[0],) + a.shape[2:], a.dtype) for a in arrs],
        in_specs=[ANY] * n, out_specs=[ANY] * n, scratch_shapes=_swap_sems(n))(*arrs)


def _swap_sems(n):
    return [pltpu.SemaphoreType.DMA((n,)), pltpu.SemaphoreType.DMA((n,))]


def _swap_phases(a_refs, out_refs, send_sems, recv_sems):
    x, y, c = _place()

    def copies():
        return [pltpu.make_async_remote_copy(
            src_ref=a_refs[a].at[:, 1 - c], dst_ref=out_refs[a], send_sem=send_sems.at[a], recv_sem=recv_sems.at[a],
            device_id=(x, y, 1 - c), device_id_type=MESH_ID) for a in range(len(a_refs))]

    def start():
        for cp in copies():
            cp.start()

    def finish():
        for cp in copies():
            cp.wait()

    return start, finish


def _exchange_chips(arrs):
    n = len(arrs)

    def body(*refs):
        start, finish = _exchange_phases(refs[:n], refs[n:2 * n], *refs[2 * n:])
        start()
        finish()

    return pl.pallas_call(
        body, name="exchange_chips", out_shape=[SDS(a.shape, a.dtype) for a in arrs],
        in_specs=[ANY] * n, out_specs=[ANY] * n, scratch_shapes=_exchange_sems(n))(*arrs)


def _exchange_sems(n):
    return [pltpu.SemaphoreType.DMA((3 * n,)), pltpu.SemaphoreType.DMA((3 * n,)), pltpu.SemaphoreType.DMA((n,))]


def _exchange_phases(p_refs, out_refs, send_sems, recv_sems, local_sems):
    n = len(p_refs)
    x, y, c = _place()
    mine = 2 * x + y
    peers = [(1 - x, y), (x, 1 - y), (1 - x, 1 - y)]

    def local():
        return [pltpu.make_async_copy(p_refs[a].at[mine], out_refs[a].at[mine], local_sems.at[a]) for a in range(n)]

    def copy(a, j, src_chip, dst_chip):
        px, py = peers[j]
        return pltpu.make_async_remote_copy(
            src_ref=p_refs[a].at[src_chip], dst_ref=out_refs[a].at[dst_chip], send_sem=send_sems.at[3 * a + j],
            recv_sem=recv_sems.at[3 * a + j], device_id=(px, py, c), device_id_type=MESH_ID)

    def sends():
        return [copy(a, j, 2 * px + py, mine) for a in range(n) for j, (px, py) in enumerate(peers)]

    def start():
        for cp in local() + sends():
            cp.start()

    def finish():
        for a in range(n):
            for j, (px, py) in enumerate(peers):
                copy(a, j, mine, 2 * px + py).wait_recv()
        for cp in sends():
            cp.wait_send()
        for cp in local():
            cp.wait()

    return start, finish


def _add_mine(g, recv, core, tile, dtype):
    _, _, lead, rows, width = g.shape

    def body(c_ref, g_ref, r_ref, o_ref):
        o_ref[...] = (g_ref[...] + r_ref[...]).astype(dtype)

    spec = pl.BlockSpec((None, None, tile, width), lambda k, l, i, c_ref: (k, l, i, 0))
    return pl.pallas_call(
        body, name="add_sibling", out_shape=SDS(recv.shape, dtype),
        grid_spec=pltpu.PrefetchScalarGridSpec(
            num_scalar_prefetch=1, grid=(g.shape[0], lead, rows // tile),
            in_specs=[pl.BlockSpec((None, None, None, tile, width), lambda k, l, i, c_ref: (k, c_ref[0], l, i, 0)),
                      spec],
            out_specs=spec),
        compiler_params=_cp("parallel", "parallel", "parallel"))(core, g, recv)


def _sum_adamw(parts, w, m, v, tile):
    lead, rows, width = w.shape
    last = rows // tile - 1

    def body(*refs):
        p_refs, (w_ref, m_ref, v_ref, g_ref, d_ref, nm_ref, nv_ref) = refs[:lead], refs[lead:]
        for layer in range(lead):
            @pl.when(pl.program_id(0) == layer)
            def _(p_ref=p_refs[layer]):
                g = ((p_ref[0].astype(F32) + p_ref[1].astype(F32)) + p_ref[2].astype(F32)) + p_ref[3].astype(F32)
                nm = ADAM_B1 * m_ref[...] + (1.0 - ADAM_B1) * g
                nv = ADAM_B2 * v_ref[...] + (1.0 - ADAM_B2) * jnp.square(g)
                m_hat = nm / (1.0 - ADAM_B1 ** ADAM_STEP)
                v_hat = nv / (1.0 - ADAM_B2 ** ADAM_STEP)
                g_ref[...] = g
                nm_ref[...] = nm
                nv_ref[...] = nv
                d_ref[...] = -ADAM_LR * (m_hat / (jnp.sqrt(v_hat) + ADAM_EPS) + ADAM_WD * w_ref[...])

    pspec = lambda layer: pl.BlockSpec(
        (4, None, tile, width),
        lambda l, i: (0, 0, jnp.where(l == layer, i, jnp.where(l > layer, last, 0)), 0))
    spec = pl.BlockSpec((None, tile, width), lambda l, i: (l, i, 0))
    return pl.pallas_call(
        body, grid=(lead, rows // tile), name="sum_adamw",
        in_specs=[pspec(layer) for layer in range(lead)] + [spec, spec, spec],
        out_specs=[spec] * 4, out_shape=[SDS((lead, rows, width), F32)] * 4,
        compiler_params=_cp("arbitrary", "arbitrary"))(*parts, w, m, v)


def kernel(x, p, positions, g_mix, w_in, sink, g_q, w_uq, g_kv, w_ukv, w_br_a, w_br_b, w_out, g_ple, w_ple_gate, w_ple_proj, g_final, loss_target, m_g_mix, m_w_in, m_sink, m_g_q, m_w_uq, m_g_kv, m_w_ukv, m_w_br_a, m_w_br_b, m_w_out, m_g_ple, m_w_ple_gate, m_w_ple_proj, m_g_final, v_g_mix, v_w_in, v_sink, v_g_q, v_w_uq, v_g_kv, v_w_ukv, v_w_br_a, v_w_br_b, v_w_out, v_g_ple, v_w_ple_gate, v_w_ple_proj, v_g_final):
    weights = dict(g_mix=g_mix, w_in=w_in, sink=sink, g_q=g_q, w_uq=w_uq, g_kv=g_kv, w_ukv=w_ukv, w_br_a=w_br_a,
                   w_br_b=w_br_b, w_out=w_out, g_ple=g_ple, w_ple_gate=w_ple_gate, w_ple_proj=w_ple_proj,
                   g_final=g_final)
    mom1 = dict(g_mix=m_g_mix, w_in=m_w_in, sink=m_sink, g_q=m_g_q, w_uq=m_w_uq, g_kv=m_g_kv, w_ukv=m_w_ukv,
                w_br_a=m_w_br_a, w_br_b=m_w_br_b, w_out=m_w_out, g_ple=m_g_ple, w_ple_gate=m_w_ple_gate,
                w_ple_proj=m_w_ple_proj, g_final=m_g_final)
    mom2 = dict(g_mix=v_g_mix, w_in=v_w_in, sink=v_sink, g_q=v_g_q, w_uq=v_w_uq, g_kv=v_g_kv, w_ukv=v_w_ukv,
                w_br_a=v_w_br_a, w_br_b=v_w_br_b, w_out=v_w_out, g_ple=v_g_ple, w_ple_gate=v_w_ple_gate,
                w_ple_proj=v_w_ple_proj, g_final=v_g_final)
    assert DEPTH == 2
    wide = lambda d: d["w_in"]
    rows = lambda d: _pack_rows(d)
    core = lax.axis_index("c").astype(jnp.int32).reshape(1)

    w16 = [wide(weights).astype(BF16), rows(weights).astype(BF16)]
    wts0 = dict(w_in=_win_layout(_all_gather([w16[0][:1]])[0], 256))
    small = _small_params(g_mix, sink, g_q, g_kv, g_ple, g_final)

    def wide_payload(g):
        return _win_grad_layout(g["w_in"], 256).reshape(N_DEV // 2, 2, 1, D, WIDE_W)

    def rows_payload(g):
        return _pack_rows({n: _split(n, g[n][None]) for n, _ in ROWS_PIECES}).reshape(N_DEV // 2, 2, 1, ROWS_N, LANES)

    def add(pay, got):
        tiles = {D: (WIDE_TILE, BF16), ROWS_N: (ROWS_TILE, BF16), VEC_ROWS: (VEC_ROWS, F32)}
        return [_add_mine(a, b, core, *tiles[a.shape[-2]]) for a, b in zip(pay, got)]

    plan = dict(behind_fwd_in=[w16[1][:1]], behind_mla_fwd=[a[1:] for a in w16],
                w_in=lambda blocks: _win_layout(blocks, 256),
                row_weights=_row_weights, payload=lambda g: [wide_payload(g), rows_payload(g)],
                rows_payload=rows_payload, add=add)
    loss, grad_x, grads, dg_final, rode = _local_step(x, p, positions, loss_target, small, [wts0], plan)

    vectors = {n: jnp.stack([grads[i][n] for i in range(DEPTH)]) for n, _ in SMALL[:-1]}
    vectors["g_final"] = dg_final
    pay = [wide_payload(grads[0]),
           jnp.broadcast_to(_pack_vec(vectors, loss[0, 0]), (N_DEV // 2, 2, 1, VEC_ROWS, LANES))]
    parts_wide0, parts_vec = _exchange_chips(add(pay, _swap_sibling(pay)))
    out_wide = _sum_adamw([parts_wide0, rode["layer1"][0]], wide(weights), wide(mom1), wide(mom2), WIDE_TILE)
    out_rows = _sum_adamw([rode["rows0"][0], rode["layer1"][1]], rows(weights), rows(mom1), rows(mom2), ROWS_TILE)
    out_vec = _sum_adamw([parts_vec], _pack_vec(weights), _pack_vec(mom1), _pack_vec(mom2), VEC_ROWS)

    outs = []
    for ow, orow, ovec in zip(out_wide, out_rows, out_vec):
        named = _unpack_rows(orow)
        named.update(_unpack_vec(ovec)[0])
        named["w_in"] = ow
        outs += [named[n] for n in weights]
    loss = _unpack_vec(out_vec[0])[1]
    return (loss, grad_x, *outs)
```
